```python
import jax, jax.numpy as jnp
from jax import lax
import numpy as np

D_MODEL = 1024
BATCH = 8
SEQ = 2048
DEPTH = 1

CHUNK = 64
Q_BLOCK = 128
HEAD_DIM = 64
N_HEADS_SB = 8
N_HEADS_FOX = 8
WIDTH_SB = N_HEADS_SB * HEAD_DIM
WIDTH_FOX = N_HEADS_FOX * HEAD_DIM
IN_COLS = 3 * WIDTH_SB + 3 * WIDTH_FOX + N_HEADS_FOX
D_FF = 2816
RMS_EPS = 1e-6
ATTN_SCALE = HEAD_DIM ** -0.5
FORGET_BIAS_MEAN = 2.0

kernel_name = "macaron_stickbreak_forgetting_gated_block"


def rms_norm(x, g):
    xf = x.astype(jnp.float32)
    y = xf * lax.rsqrt(jnp.mean(xf * xf, axis=-1, keepdims=True) + RMS_EPS)
    return (y * g.astype(jnp.float32)).astype(x.dtype)


def swiglu(h, w_gate, w_up, w_down):
    return (jax.nn.silu(h @ w_gate) * (h @ w_up)) @ w_down


def split_heads(t, n_heads):
    b, s, _ = t.shape
    return t.reshape(b, s, n_heads, HEAD_DIM).transpose(0, 2, 1, 3)


def merge_heads(t):
    b, h, s, d = t.shape
    return t.transpose(0, 2, 1, 3).reshape(b, s, h * d)


def stick_breaking_attention(q, k, v):
    seq = q.shape[2]
    outs = []
    for i in range(seq // Q_BLOCK):
        q0 = i * Q_BLOCK
        k_end = q0 + Q_BLOCK
        z = jnp.einsum('bhqd,bhkd->bhqk', q[:, :, q0:k_end], k[:, :, :k_end]).astype(jnp.float32) * ATTN_SCALE
        t_pos = q0 + jnp.arange(Q_BLOCK)[:, None]
        s_pos = jnp.arange(k_end)[None, :]
        strict = s_pos < t_pos
        log_not_beta = jnp.where(strict, jax.nn.log_sigmoid(-z), 0.0)
        between = lax.cumsum(log_not_beta, axis=3, reverse=True) - log_not_beta
        weights = jnp.where(strict, jnp.exp(jax.nn.log_sigmoid(z) + between), 0.0)
        outs.append(jnp.einsum('bhqk,bhkd->bhqd', weights.astype(v.dtype), v[:, :, :k_end]))
    return jnp.concatenate(outs, axis=2)


def forgetting_attention(q, k, v, log_f_cum):
    seq = q.shape[2]
    outs = []
    for i in range(seq // Q_BLOCK):
        q0 = i * Q_BLOCK
        k_end = q0 + Q_BLOCK
        logits = jnp.einsum('bhqd,bhkd->bhqk', q[:, :, q0:k_end], k[:, :, :k_end]).astype(jnp.float32) * ATTN_SCALE
        logits = logits + log_f_cum[:, :, q0:k_end, None] - log_f_cum[:, :, None, :k_end]
        t_pos = q0 + jnp.arange(Q_BLOCK)[:, None]
        s_pos = jnp.arange(k_end)[None, :]
        logits = jnp.where(s_pos <= t_pos, logits, -jnp.inf)
        probs = jax.nn.softmax(logits, axis=-1)
        outs.append(jnp.einsum('bhqk,bhkd->bhqd', probs.astype(v.dtype), v[:, :, :k_end]))
    return jnp.concatenate(outs, axis=2)


def _fwd_setup_inputs(seed: int = 0) -> dict:
    key = jax.random.key(seed)
    ks = jax.random.split(key, 20)
    f32 = jnp.float32

    def dense(k, fan_in, fan_out):
        return jax.random.normal(k, (DEPTH, fan_in, fan_out), f32) * fan_in ** -0.5

    def gain(k, shape):
        return 1.0 + 0.02 * jax.random.normal(k, shape, f32)

    return {
        'x': jax.random.normal(ks[0], (BATCH, SEQ, D_MODEL), f32),
        'norm_ffn1': gain(ks[1], (DEPTH, D_MODEL)),
        'w_ffn1_gate': dense(ks[2], D_MODEL, D_FF),
        'w_ffn1_up': dense(ks[3], D_MODEL, D_FF),
        'w_ffn1_down': dense(ks[4], D_FF, D_MODEL),
        'norm_mix': gain(ks[5], (DEPTH, D_MODEL)),
        'w_in': dense(ks[6], D_MODEL, IN_COLS),
        'b_forget': FORGET_BIAS_MEAN + 0.1 * jax.random.normal(ks[7], (DEPTH, N_HEADS_FOX), f32),
        'w_gate': dense(ks[8], D_MODEL, 2 * D_MODEL),
        'b_gate': 0.02 * jax.random.normal(ks[9], (DEPTH, 2 * D_MODEL), f32),
        'w_up_a': dense(ks[10], WIDTH_SB, D_MODEL),
        'w_up_b': dense(ks[11], WIDTH_FOX, D_MODEL),
        'w_out': dense(ks[12], D_MODEL, D_MODEL),
        'norm_ffn2': gain(ks[13], (DEPTH, D_MODEL)),
        'w_ffn2_gate': dense(ks[14], D_MODEL, D_FF),
        'w_ffn2_up': dense(ks[15], D_MODEL, D_FF),
        'w_ffn2_down': dense(ks[16], D_FF, D_MODEL),
        'norm_final': gain(ks[17], (D_MODEL,)),
    }


def _fwd_reference(x, norm_ffn1, w_ffn1_gate, w_ffn1_up, w_ffn1_down, norm_mix, w_in, b_forget,
              w_gate, b_gate, w_up_a, w_up_b, w_out, norm_ffn2, w_ffn2_gate, w_ffn2_up,
              w_ffn2_down, norm_final):
    splits = np.cumsum([WIDTH_SB, WIDTH_SB, WIDTH_SB, WIDTH_FOX, WIDTH_FOX, WIDTH_FOX]).tolist()
    for l in range(DEPTH):
        x = x + 0.5 * swiglu(rms_norm(x, norm_ffn1[l]), w_ffn1_gate[l], w_ffn1_up[l], w_ffn1_down[l])

        h = rms_norm(x, norm_mix[l])
        proj = h @ w_in[l]
        q_a, k_a, v_a, q_b, k_b, v_b, f_logit = jnp.split(proj, splits, axis=-1)

        y_a = merge_heads(stick_breaking_attention(
            split_heads(q_a, N_HEADS_SB), split_heads(k_a, N_HEADS_SB), split_heads(v_a, N_HEADS_SB)))

        log_f = jax.nn.log_sigmoid((f_logit + b_forget[l]).astype(jnp.float32))
        log_f_cum = jnp.cumsum(log_f, axis=1).transpose(0, 2, 1)
        y_b = merge_heads(forgetting_attention(
            split_heads(q_b, N_HEADS_FOX), split_heads(k_b, N_HEADS_FOX), split_heads(v_b, N_HEADS_FOX),
            log_f_cum))

        gates = jax.nn.sigmoid(h @ w_gate[l] + b_gate[l])
        g_a, g_b = jnp.split(gates, 2, axis=-1)
        mixed = g_a * (y_a @ w_up_a[l]) + g_b * (y_b @ w_up_b[l])
        x = x + mixed @ w_out[l]

        x = x + 0.5 * swiglu(rms_norm(x, norm_ffn2[l]), w_ffn2_gate[l], w_ffn2_up[l], w_ffn2_down[l])
    return rms_norm(x, norm_final)


import jax as _jax
import jax.numpy as _jnp

TWIN_FORMAT = 'train_step'
FWD_PARAMS = ['x', 'norm_ffn1', 'w_ffn1_gate', 'w_ffn1_up', 'w_ffn1_down', 'norm_mix', 'w_in', 'b_forget', 'w_gate', 'b_gate', 'w_up_a', 'w_up_b', 'w_out', 'norm_ffn2', 'w_ffn2_gate', 'w_ffn2_up', 'w_ffn2_down', 'norm_final']
TWIN_WEIGHTS = ['norm_ffn1', 'w_ffn1_gate', 'w_ffn1_up', 'w_ffn1_down', 'norm_mix', 'w_in', 'b_forget', 'w_gate', 'b_gate', 'w_up_a', 'w_up_b', 'w_out', 'norm_ffn2', 'w_ffn2_gate', 'w_ffn2_up', 'w_ffn2_down', 'norm_final']
TWIN_DIFF_INPUT = 'x'
TWIN_INPUTS = ['x', 'norm_ffn1', 'w_ffn1_gate', 'w_ffn1_up', 'w_ffn1_down', 'norm_mix', 'w_in', 'b_forget', 'w_gate', 'b_gate', 'w_up_a', 'w_up_b', 'w_out', 'norm_ffn2', 'w_ffn2_gate', 'w_ffn2_up', 'w_ffn2_down', 'norm_final', 'loss_target', 'm_norm_ffn1', 'm_w_ffn1_gate', 'm_w_ffn1_up', 'm_w_ffn1_down', 'm_norm_mix', 'm_w_in', 'm_b_forget', 'm_w_gate', 'm_b_gate', 'm_w_up_a', 'm_w_up_b', 'm_w_out', 'm_norm_ffn2', 'm_w_ffn2_gate', 'm_w_ffn2_up', 'm_w_ffn2_down', 'm_norm_final', 'v_norm_ffn1', 'v_w_ffn1_gate', 'v_w_ffn1_up', 'v_w_ffn1_down', 'v_norm_mix', 'v_w_in', 'v_b_forget', 'v_w_gate', 'v_b_gate', 'v_w_up_a', 'v_w_up_b', 'v_w_out', 'v_norm_ffn2', 'v_w_ffn2_gate', 'v_w_ffn2_up', 'v_w_ffn2_down', 'v_norm_final']
TWIN_OUTPUTS = ['loss', 'grad_x', 'grad_norm_ffn1', 'grad_w_ffn1_gate', 'grad_w_ffn1_up', 'grad_w_ffn1_down', 'grad_norm_mix', 'grad_w_in', 'grad_b_forget', 'grad_w_gate', 'grad_b_gate', 'grad_w_up_a', 'grad_w_up_b', 'grad_w_out', 'grad_norm_ffn2', 'grad_w_ffn2_gate', 'grad_w_ffn2_up', 'grad_w_ffn2_down', 'grad_norm_final', 'delta_norm_ffn1', 'delta_w_ffn1_gate', 'delta_w_ffn1_up', 'delta_w_ffn1_down', 'delta_norm_mix', 'delta_w_in', 'delta_b_forget', 'delta_w_gate', 'delta_b_gate', 'delta_w_up_a', 'delta_w_up_b', 'delta_w_out', 'delta_norm_ffn2', 'delta_w_ffn2_gate', 'delta_w_ffn2_up', 'delta_w_ffn2_down', 'delta_norm_final', 'new_m_norm_ffn1', 'new_m_w_ffn1_gate', 'new_m_w_ffn1_up', 'new_m_w_ffn1_down', 'new_m_norm_mix', 'new_m_w_in', 'new_m_b_forget', 'new_m_w_gate', 'new_m_b_gate', 'new_m_w_up_a', 'new_m_w_up_b', 'new_m_w_out', 'new_m_norm_ffn2', 'new_m_w_ffn2_gate', 'new_m_w_ffn2_up', 'new_m_w_ffn2_down', 'new_m_norm_final', 'new_v_norm_ffn1', 'new_v_w_ffn1_gate', 'new_v_w_ffn1_up', 'new_v_w_ffn1_down', 'new_v_norm_mix', 'new_v_w_in', 'new_v_b_forget', 'new_v_w_gate', 'new_v_b_gate', 'new_v_w_up_a', 'new_v_w_up_b', 'new_v_w_out', 'new_v_norm_ffn2', 'new_v_w_ffn2_gate', 'new_v_w_ffn2_up', 'new_v_w_ffn2_down', 'new_v_norm_final']
TWIN_LEAF_KINDS = {'loss': 'loss', 'grad_x': 'grad_x', 'grad_norm_ffn1': 'grad_w', 'grad_w_ffn1_gate': 'grad_w', 'grad_w_ffn1_up': 'grad_w', 'grad_w_ffn1_down': 'grad_w', 'grad_norm_mix': 'grad_w', 'grad_w_in': 'grad_w', 'grad_b_forget': 'grad_w', 'grad_w_gate': 'grad_w', 'grad_b_gate': 'grad_w', 'grad_w_up_a': 'grad_w', 'grad_w_up_b': 'grad_w', 'grad_w_out': 'grad_w', 'grad_norm_ffn2': 'grad_w', 'grad_w_ffn2_gate': 'grad_w', 'grad_w_ffn2_up': 'grad_w', 'grad_w_ffn2_down': 'grad_w', 'grad_norm_final': 'grad_w', 'delta_norm_ffn1': 'delta_w', 'delta_w_ffn1_gate': 'delta_w', 'delta_w_ffn1_up': 'delta_w', 'delta_w_ffn1_down': 'delta_w', 'delta_norm_mix': 'delta_w', 'delta_w_in': 'delta_w', 'delta_b_forget': 'delta_w', 'delta_w_gate': 'delta_w', 'delta_b_gate': 'delta_w', 'delta_w_up_a': 'delta_w', 'delta_w_up_b': 'delta_w', 'delta_w_out': 'delta_w', 'delta_norm_ffn2': 'delta_w', 'delta_w_ffn2_gate': 'delta_w', 'delta_w_ffn2_up': 'delta_w', 'delta_w_ffn2_down': 'delta_w', 'delta_norm_final': 'delta_w', 'new_m_norm_ffn1': 'new_m', 'new_m_w_ffn1_gate': 'new_m', 'new_m_w_ffn1_up': 'new_m', 'new_m_w_ffn1_down': 'new_m', 'new_m_norm_mix': 'new_m', 'new_m_w_in': 'new_m', 'new_m_b_forget': 'new_m', 'new_m_w_gate': 'new_m', 'new_m_b_gate': 'new_m', 'new_m_w_up_a': 'new_m', 'new_m_w_up_b': 'new_m', 'new_m_w_out': 'new_m', 'new_m_norm_ffn2': 'new_m', 'new_m_w_ffn2_gate': 'new_m', 'new_m_w_ffn2_up': 'new_m', 'new_m_w_ffn2_down': 'new_m', 'new_m_norm_final': 'new_m', 'new_v_norm_ffn1': 'new_v', 'new_v_w_ffn1_gate': 'new_v', 'new_v_w_ffn1_up': 'new_v', 'new_v_w_ffn1_down': 'new_v', 'new_v_norm_mix': 'new_v', 'new_v_w_in': 'new_v', 'new_v_b_forget': 'new_v', 'new_v_w_gate': 'new_v', 'new_v_b_gate': 'new_v', 'new_v_w_up_a': 'new_v', 'new_v_w_up_b': 'new_v', 'new_v_w_out': 'new_v', 'new_v_norm_ffn2': 'new_v', 'new_v_w_ffn2_gate': 'new_v', 'new_v_w_ffn2_up': 'new_v', 'new_v_w_ffn2_down': 'new_v', 'new_v_norm_final': 'new_v'}


def _forward(args):
    return _fwd_reference(*[args[k] for k in FWD_PARAMS])


def _output_shape():
    out = _jax.eval_shape(lambda: _forward(_fwd_setup_inputs(0)))
    return out.shape, out.dtype

N_MICROBATCH = 1
ADAM_LR = 0.001
ADAM_B1 = 0.9
ADAM_B2 = 0.999
ADAM_EPS = 1e-08
ADAM_WD = 0.01
ADAM_STEP = 10
PER_EXAMPLE_BATCH_AXIS = {'x': 0, 'loss_target': 0}
SHARED_INPUTS = []
_WEIGHT_DTYPES = {'norm_ffn1': _jnp.float32, 'w_ffn1_gate': _jnp.float32, 'w_ffn1_up': _jnp.float32, 'w_ffn1_down': _jnp.float32, 'norm_mix': _jnp.float32, 'w_in': _jnp.float32, 'b_forget': _jnp.float32, 'w_gate': _jnp.float32, 'b_gate': _jnp.float32, 'w_up_a': _jnp.float32, 'w_up_b': _jnp.float32, 'w_out': _jnp.float32, 'norm_ffn2': _jnp.float32, 'w_ffn2_gate': _jnp.float32, 'w_ffn2_up': _jnp.float32, 'w_ffn2_down': _jnp.float32, 'norm_final': _jnp.float32}
MOMENT_SCALE = {'norm_ffn1': 6.107543e-02, 'w_ffn1_gate': 2.436556e-02, 'w_ffn1_up': 2.356402e-02, 'w_ffn1_down': 3.907909e-02, 'norm_mix': 6.449937e-02, 'w_in': 3.575780e-02, 'b_forget': 1.700650e-01, 'w_gate': 1.329910e-02, 'b_gate': 1.330124e-02, 'w_up_a': 4.024722e-02, 'w_up_b': 2.653108e-02, 'w_out': 4.808451e-02, 'norm_ffn2': 4.608951e-02, 'w_ffn2_gate': 2.005621e-02, 'w_ffn2_up': 1.945334e-02, 'w_ffn2_down': 3.232080e-02, 'norm_final': 1.599792e+01}


def _to_microbatches(a, axis):
    t = _jnp.moveaxis(a, axis, 0)
    t = t.reshape((N_MICROBATCH, t.shape[0] // N_MICROBATCH) + t.shape[1:])
    return _jnp.moveaxis(t, 1, axis + 1)


def setup_inputs(seed: int = 0) -> dict:
    inp = _fwd_setup_inputs(seed)
    key = _jax.random.fold_in(_jax.random.key(seed), 7919)
    shape, _ = _output_shape()
    out = dict(inp)
    out["loss_target"] = _jax.random.normal(_jax.random.fold_in(key, 0), shape, _jnp.float32)
    for i, name in enumerate(TWIN_WEIGHTS):
        w = inp[name].astype(_jnp.float32)
        if MOMENT_SCALE is None:
            s = _jnp.sqrt(_jnp.mean(_jnp.square(w)) + 1e-30)
        else:
            s = MOMENT_SCALE[name]
        km, kv = _jax.random.split(_jax.random.fold_in(key, i + 1))
        out[name] = w
        out["m_" + name] = s * _jax.random.normal(km, w.shape, _jnp.float32)
        out["v_" + name] = (s * s) * _jax.random.uniform(kv, w.shape, _jnp.float32, 0.5, 1.5)
    if N_MICROBATCH > 1:
        for name, axis in PER_EXAMPLE_BATCH_AXIS.items():
            out[name] = _to_microbatches(out[name], axis)
    return {'x': out['x'], 'norm_ffn1': out['norm_ffn1'], 'w_ffn1_gate': out['w_ffn1_gate'], 'w_ffn1_up': out['w_ffn1_up'], 'w_ffn1_down': out['w_ffn1_down'], 'norm_mix': out['norm_mix'], 'w_in': out['w_in'], 'b_forget': out['b_forget'], 'w_gate': out['w_gate'], 'b_gate': out['b_gate'], 'w_up_a': out['w_up_a'], 'w_up_b': out['w_up_b'], 'w_out': out['w_out'], 'norm_ffn2': out['norm_ffn2'], 'w_ffn2_gate': out['w_ffn2_gate'], 'w_ffn2_up': out['w_ffn2_up'], 'w_ffn2_down': out['w_ffn2_down'], 'norm_final': out['norm_final'], 'loss_target': out['loss_target'], 'm_norm_ffn1': out['m_norm_ffn1'], 'm_w_ffn1_gate': out['m_w_ffn1_gate'], 'm_w_ffn1_up': out['m_w_ffn1_up'], 'm_w_ffn1_down': out['m_w_ffn1_down'], 'm_norm_mix': out['m_norm_mix'], 'm_w_in': out['m_w_in'], 'm_b_forget': out['m_b_forget'], 'm_w_gate': out['m_w_gate'], 'm_b_gate': out['m_b_gate'], 'm_w_up_a': out['m_w_up_a'], 'm_w_up_b': out['m_w_up_b'], 'm_w_out': out['m_w_out'], 'm_norm_ffn2': out['m_norm_ffn2'], 'm_w_ffn2_gate': out['m_w_ffn2_gate'], 'm_w_ffn2_up': out['m_w_ffn2_up'], 'm_w_ffn2_down': out['m_w_ffn2_down'], 'm_norm_final': out['m_norm_final'], 'v_norm_ffn1': out['v_norm_ffn1'], 'v_w_ffn1_gate': out['v_w_ffn1_gate'], 'v_w_ffn1_up': out['v_w_ffn1_up'], 'v_w_ffn1_down': out['v_w_ffn1_down'], 'v_norm_mix': out['v_norm_mix'], 'v_w_in': out['v_w_in'], 'v_b_forget': out['v_b_forget'], 'v_w_gate': out['v_w_gate'], 'v_b_gate': out['v_b_gate'], 'v_w_up_a': out['v_w_up_a'], 'v_w_up_b': out['v_w_up_b'], 'v_w_out': out['v_w_out'], 'v_norm_ffn2': out['v_norm_ffn2'], 'v_w_ffn2_gate': out['v_w_ffn2_gate'], 'v_w_ffn2_up': out['v_w_ffn2_up'], 'v_w_ffn2_down': out['v_w_ffn2_down'], 'v_norm_final': out['v_norm_final']}


def _loss(weights, diff, rest, loss_target):
    with _jax.named_scope("forward"):
        args = {**rest, TWIN_DIFF_INPUT: diff, **{k: w.astype(_WEIGHT_DTYPES[k]) for k, w in weights.items()}}
        y = _forward(args)
    with _jax.named_scope("loss_head"):
        err = _jnp.square(y.astype(_jnp.float32) - loss_target)
        return 0.5 * _jnp.sum(_jnp.mean(err, axis=-1)) if err.ndim else 0.5 * err


def _adamw(w, g, m, v):
    m = ADAM_B1 * m + (1.0 - ADAM_B1) * g
    v = ADAM_B2 * v + (1.0 - ADAM_B2) * _jnp.square(g)
    m_hat = m / (1.0 - ADAM_B1 ** ADAM_STEP)
    v_hat = v / (1.0 - ADAM_B2 ** ADAM_STEP)
    delta = -ADAM_LR * (m_hat / (_jnp.sqrt(v_hat) + ADAM_EPS) + ADAM_WD * w)
    return delta, m, v


def reference(x, norm_ffn1, w_ffn1_gate, w_ffn1_up, w_ffn1_down, norm_mix, w_in, b_forget, w_gate, b_gate, w_up_a, w_up_b, w_out, norm_ffn2, w_ffn2_gate, w_ffn2_up, w_ffn2_down, norm_final, loss_target, m_norm_ffn1, m_w_ffn1_gate, m_w_ffn1_up, m_w_ffn1_down, m_norm_mix, m_w_in, m_b_forget, m_w_gate, m_b_gate, m_w_up_a, m_w_up_b, m_w_out, m_norm_ffn2, m_w_ffn2_gate, m_w_ffn2_up, m_w_ffn2_down, m_norm_final, v_norm_ffn1, v_w_ffn1_gate, v_w_ffn1_up, v_w_ffn1_down, v_norm_mix, v_w_in, v_b_forget, v_w_gate, v_b_gate, v_w_up_a, v_w_up_b, v_w_out, v_norm_ffn2, v_w_ffn2_gate, v_w_ffn2_up, v_w_ffn2_down, v_norm_final):
    given = dict(x=x, norm_ffn1=norm_ffn1, w_ffn1_gate=w_ffn1_gate, w_ffn1_up=w_ffn1_up, w_ffn1_down=w_ffn1_down, norm_mix=norm_mix, w_in=w_in, b_forget=b_forget, w_gate=w_gate, b_gate=b_gate, w_up_a=w_up_a, w_up_b=w_up_b, w_out=w_out, norm_ffn2=norm_ffn2, w_ffn2_gate=w_ffn2_gate, w_ffn2_up=w_ffn2_up, w_ffn2_down=w_ffn2_down, norm_final=norm_final, loss_target=loss_target, m_norm_ffn1=m_norm_ffn1, m_w_ffn1_gate=m_w_ffn1_gate, m_w_ffn1_up=m_w_ffn1_up, m_w_ffn1_down=m_w_ffn1_down, m_norm_mix=m_norm_mix, m_w_in=m_w_in, m_b_forget=m_b_forget, m_w_gate=m_w_gate, m_b_gate=m_b_gate, m_w_up_a=m_w_up_a, m_w_up_b=m_w_up_b, m_w_out=m_w_out, m_norm_ffn2=m_norm_ffn2, m_w_ffn2_gate=m_w_ffn2_gate, m_w_ffn2_up=m_w_ffn2_up, m_w_ffn2_down=m_w_ffn2_down, m_norm_final=m_norm_final, v_norm_ffn1=v_norm_ffn1, v_w_ffn1_gate=v_w_ffn1_gate, v_w_ffn1_up=v_w_ffn1_up, v_w_ffn1_down=v_w_ffn1_down, v_norm_mix=v_norm_mix, v_w_in=v_w_in, v_b_forget=v_b_forget, v_w_gate=v_w_gate, v_b_gate=v_b_gate, v_w_up_a=v_w_up_a, v_w_up_b=v_w_up_b, v_w_out=v_w_out, v_norm_ffn2=v_norm_ffn2, v_w_ffn2_gate=v_w_ffn2_gate, v_w_ffn2_up=v_w_ffn2_up, v_w_ffn2_down=v_w_ffn2_down, v_norm_final=v_norm_final)
    weights = {n: given[n] for n in TWIN_WEIGHTS}
    shared = {n: given[n] for n in SHARED_INPUTS}
    per_example = {n: given[n] for n in ['x']}
    grad_fn = _jax.value_and_grad(_loss, argnums=(0, 1))

    def one_microbatch(ex, loss_target):
        ex = dict(ex)
        diff = ex.pop(TWIN_DIFF_INPUT)
        return grad_fn(weights, diff, {**shared, **ex}, loss_target)

    if N_MICROBATCH == 1:
        loss, (grad_w, grad_x) = one_microbatch(per_example, given["loss_target"])
    else:
        def body(carry, xs):
            loss_sum, grad_sum = carry
            l_k, (gw_k, gx_k) = one_microbatch(xs[0], xs[1])
            with _jax.named_scope("update"):
                return (loss_sum + l_k, _jax.tree.map(_jnp.add, grad_sum, gw_k)), gx_k

        init = (_jnp.zeros((), _jnp.float32), _jax.tree.map(_jnp.zeros_like, weights))
        (loss, grad_w), grad_x = _jax.lax.scan(body, init, (per_example, given["loss_target"]))
    with _jax.named_scope("update"):
        delta_w, new_m, new_v = {}, {}, {}
        for n in TWIN_WEIGHTS:
            delta_w[n], new_m[n], new_v[n] = _adamw(weights[n], grad_w[n], given["m_" + n], given["v_" + n])
    return (loss, grad_x, *[grad_w[n] for n in TWIN_WEIGHTS], *[delta_w[n] for n in TWIN_WEIGHTS],
            *[new_m[n] for n in TWIN_WEIGHTS], *[new_v[n] for n in TWIN_WEIGHTS])
```

```python
import functools

import jax
import jax.numpy as jnp
from jax import lax
from jax.experimental import pallas as pl
from jax.experimental.pallas import tpu as pltpu

F32 = jnp.float32
BF16 = jnp.bfloat16

HEAD_DIM = 64
N_HEADS = 8
WIDTH = N_HEADS * HEAD_DIM
QKV_COLS = 6 * WIDTH
RMS_EPS = 1e-6
ATTN_SCALE = HEAD_DIM ** -0.5
N_CHIPS = 4
QB = 128
F_PAD = 256
NEG_BIG = -1e30

ADAM_LR = 0.001
ADAM_B1 = 0.9
ADAM_B2 = 0.999
ADAM_EPS = 1e-08
ADAM_WD = 0.01
ADAM_STEP = 10

VMEM_LIMIT_BYTES = 48 * 1024 * 1024
MESH = pl.DeviceIdType.MESH

NN = ((1,), (0,))
NT = ((1,), (1,))
TN = ((0,), (0,))


def _params(semantics):
    return pltpu.CompilerParams(dimension_semantics=semantics, vmem_limit_bytes=VMEM_LIMIT_BYTES)


def _dot(a, b, contract):
    return lax.dot_general(a.astype(BF16), b.astype(BF16), (contract, ((), ())), preferred_element_type=F32)


def _sigmoid(x):
    return 1.0 / (1.0 + jnp.exp(-x))


def _log1pexp_neg_abs(z):
    return jnp.log(1.0 + jnp.exp(-jnp.abs(z)))


def _split3(x):
    hi = x.astype(BF16)
    r1 = x - hi.astype(F32)
    mid = r1.astype(BF16)
    lo = (r1 - mid.astype(F32)).astype(BF16)
    return hi, mid, lo


def _dot_exact_rhs01(x, m01):
    hi, mid, lo = _split3(x)
    d = lambda p: lax.dot_general(p, m01, (NN, ((), ())), preferred_element_type=F32)
    return d(hi) + d(mid) + d(lo)


def _dot_exact_lhs01(m01, x):
    hi, mid, lo = _split3(x)
    d = lambda p: lax.dot_general(m01, p, (NN, ((), ())), preferred_element_type=F32)
    return d(hi) + d(mid) + d(lo)


def _iota2(shape, dim):
    return lax.broadcasted_iota(jnp.int32, shape, dim)


def _mm(name, pairs, contract, grid, pair_specs, out_shape, out_specs, acc_shape, nk, epilogue,
        extras=(), extra_specs=(), semantics=None):
    n_pairs = len(pairs)
    n_extra = len(extras)
    n_out = len(out_shape)

    def body(*refs):
        ab = refs[:2 * n_pairs]
        ex = refs[2 * n_pairs:2 * n_pairs + n_extra]
        outs = refs[2 * n_pairs + n_extra:2 * n_pairs + n_extra + n_out]
        acc = refs[-1]
        ids = [pl.program_id(i) for i in range(len(grid))]
        k = ids[-1]

        @pl.when(k == 0)
        def _():
            acc[...] = jnp.zeros_like(acc)

        part = _dot(ab[0][...], ab[1][...], contract)
        for p in range(1, n_pairs):
            part += _dot(ab[2 * p][...], ab[2 * p + 1][...], contract)
        acc[...] += part

        @pl.when(k == nk - 1)
        def _():
            epilogue(acc[...], ex, outs, ids)

    operands = [t for pair in pairs for t in pair] + list(extras)
    in_specs = [s for pair in pair_specs for s in pair] + list(extra_specs)
    if semantics is None:
        semantics = ("parallel",) * (len(grid) - 1) + ("arbitrary",)
    return pl.pallas_call(
        body, name=name, grid=grid, in_specs=in_specs, out_specs=list(out_specs), out_shape=list(out_shape),
        scratch_shapes=[pltpu.VMEM(acc_shape, F32)], compiler_params=_params(semantics),
    )(*operands)


def _row_tile(rows, target):
    t = min(rows, target)
    while rows % t:
        t //= 2
    return t


def rms_fwd(name, x, g):
    t, d = x.shape
    tr = _row_tile(t, 256)

    def body(x_ref, g_ref, h_ref, r_ref):
        xv = x_ref[...]
        r = lax.rsqrt(jnp.mean(xv * xv, axis=-1, keepdims=True) + RMS_EPS)
        h_ref[...] = (xv * r * g_ref[...]).astype(BF16)
        r_ref[...] = r

    return pl.pallas_call(
        body, name=name, grid=(t // tr,),
        in_specs=[pl.BlockSpec((tr, d), lambda i: (i, 0)), pl.BlockSpec((1, d), lambda i: (0, 0))],
        out_specs=[pl.BlockSpec((tr, d), lambda i: (i, 0)), pl.BlockSpec((tr, 1), lambda i: (i, 0))],
        out_shape=[jax.ShapeDtypeStruct((t, d), BF16), jax.ShapeDtypeStruct((t, 1), F32)],
        compiler_params=_params(("parallel",)),
    )(x, g)


def ffn_up(name, h, wgu):
    t, d = h.shape
    ns, _, _, f4 = wgu.shape
    tm = _row_tile(t, 512)

    def body(h_ref, wg_ref, wu_ref, a_ref, b_ref, s_ref):
        hv = h_ref[...]
        a = _dot(hv, wg_ref[...], NN)
        b = _dot(hv, wu_ref[...], NN)
        a_ref[...] = a
        b_ref[...] = b
        s_ref[...] = (a * _sigmoid(a) * b).astype(BF16)

    act = lambda dt: jax.ShapeDtypeStruct((ns, t, f4), dt)
    act_spec = pl.BlockSpec((None, tm, f4), lambda j, m: (j, m, 0))
    return pl.pallas_call(
        body, name=name, grid=(ns, t // tm),
        in_specs=[pl.BlockSpec((tm, d), lambda j, m: (m, 0)),
                  pl.BlockSpec((None, None, d, f4), lambda j, m: (j, 0, 0, 0)),
                  pl.BlockSpec((None, None, d, f4), lambda j, m: (j, 1, 0, 0))],
        out_specs=[act_spec, act_spec, act_spec],
        out_shape=[act(F32), act(F32), act(BF16)],
        compiler_params=_params(("parallel", "parallel")),
    )(h, wgu, wgu)


def mm_residual(name, s, w, x, scale):
    nj, t, kdim = s.shape
    n = w.shape[2]
    tm = _row_tile(t, 512)

    def epilogue(acc, ex, outs, ids):
        outs[0][...] = ex[0][...] + scale * acc

    return _mm(
        name, [(s, w)], NN, (t // tm, nj),
        [(pl.BlockSpec((None, tm, kdim), lambda m, j: (j, m, 0)), pl.BlockSpec((None, kdim, n), lambda m, j: (j, 0, 0)))],
        [jax.ShapeDtypeStruct((t, n), F32)], [pl.BlockSpec((tm, n), lambda m, j: (m, 0))], (tm, n), nj, epilogue,
        extras=[x], extra_specs=[pl.BlockSpec((tm, n), lambda m, j: (m, 0))],
    )[0]


def ffn_bwd_act(name, dx, wd, a, b):
    t, d = dx.shape
    ns, f4, _ = wd.shape
    tm = _row_tile(t, 512)

    def body(dx_ref, wd_ref, a_ref, b_ref, da_ref, db_ref):
        ds = _dot(0.5 * dx_ref[...], wd_ref[...], NT)
        av = a_ref[...]
        sig = _sigmoid(av)
        da_ref[...] = (ds * b_ref[...] * (sig * (1.0 + av * (1.0 - sig)))).astype(BF16)
        db_ref[...] = (ds * (av * sig)).astype(BF16)

    act_spec = pl.BlockSpec((None, tm, f4), lambda j, m: (j, m, 0))
    return pl.pallas_call(
        body, name=name, grid=(ns, t // tm),
        in_specs=[pl.BlockSpec((tm, d), lambda j, m: (m, 0)), pl.BlockSpec((None, f4, d), lambda j, m: (j, 0, 0)),
                  act_spec, act_spec],
        out_specs=[act_spec, act_spec],
        out_shape=[jax.ShapeDtypeStruct((ns, t, f4), BF16)] * 2,
        compiler_params=_params(("parallel", "parallel")),
    )(dx, wd, a, b)


def ffn_wgrad_down(name, s, dx):
    ns, t, f4 = s.shape
    d = dx.shape[1]
    tk = _row_tile(t, 512)

    def epilogue(acc, ex, outs, ids):
        outs[0][...] = (0.5 * acc).astype(BF16)

    return _mm(
        name, [(s, dx)], TN, (ns, t // tk),
        [(pl.BlockSpec((None, tk, f4), lambda j, k: (j, k, 0)), pl.BlockSpec((tk, d), lambda j, k: (k, 0)))],
        [jax.ShapeDtypeStruct((ns, 1, f4, d), BF16)], [pl.BlockSpec((None, None, f4, d), lambda j, k: (j, 0, 0, 0))],
        (f4, d), t // tk, epilogue,
    )[0]


def ffn_wgrad_up(name, h, da, db):
    t, d = h.shape
    ns, _, f4 = da.shape
    tk = _row_tile(t, 512)

    def body(h_ref, da_ref, db_ref, o_ref, acc_a, acc_b):
        k = pl.program_id(1)

        @pl.when(k == 0)
        def _():
            acc_a[...] = jnp.zeros_like(acc_a)
            acc_b[...] = jnp.zeros_like(acc_b)

        hv = h_ref[...]
        acc_a[...] += _dot(hv, da_ref[...], TN)
        acc_b[...] += _dot(hv, db_ref[...], TN)

        @pl.when(k == t // tk - 1)
        def _():
            o_ref[0] = acc_a[...].astype(BF16)
            o_ref[1] = acc_b[...].astype(BF16)

    act_spec = pl.BlockSpec((None, tk, f4), lambda j, k: (j, k, 0))
    return pl.pallas_call(
        body, name=name, grid=(ns, t // tk),
        in_specs=[pl.BlockSpec((tk, d), lambda j, k: (k, 0)), act_spec, act_spec],
        out_specs=pl.BlockSpec((None, 2, d, f4), lambda j, k: (j, 0, 0, 0)),
        out_shape=jax.ShapeDtypeStruct((ns, 2, d, f4), BF16),
        scratch_shapes=[pltpu.VMEM((d, f4), F32), pltpu.VMEM((d, f4), F32)],
        compiler_params=_params(("parallel", "arbitrary")),
    )(h, da, db)


def dh_rms_bwd(name, pairs, pair_specs, grid, nk, x, rstd, g, dx_in):
    t, d = x.shape
    tm = t // grid[0]

    def epilogue(acc, ex, outs, ids):
        x_ref, r_ref, g_ref, dxin_ref = ex
        r = r_ref[...]
        xhat = x_ref[...] * r
        dhg = acc * g_ref[...]
        proj = jnp.mean(dhg * xhat, axis=-1, keepdims=True)
        outs[0][...] = dxin_ref[...] + r * (dhg - xhat * proj)
        part = jnp.sum(acc * xhat, axis=0, keepdims=True)

        @pl.when(ids[0] == 0)
        def _():
            outs[1][...] = part

        @pl.when(ids[0] != 0)
        def _():
            outs[1][...] += part

    row = pl.BlockSpec((tm, d), lambda m, k: (m, 0))
    return _mm(
        name, pairs, NT, grid, pair_specs,
        [jax.ShapeDtypeStruct((t, d), F32), jax.ShapeDtypeStruct((1, d), F32)],
        [row, pl.BlockSpec((1, d), lambda m, k: (0, 0))], (tm, d), nk, epilogue,
        extras=[x, rstd, g, dx_in],
        extra_specs=[row, pl.BlockSpec((tm, 1), lambda m, k: (m, 0)), pl.BlockSpec((1, d), lambda m, k: (0, 0)), row],
        semantics=("arbitrary", "arbitrary"),
    )


def ffn_forward(tag, x, g_norm, wgu, wdn):
    h, rstd = rms_fwd(f"{tag}_rms", x, g_norm)
    a, b, s = ffn_up(f"{tag}_up", h, wgu)
    x_out = mm_residual(f"{tag}_down", s, wdn, x, 0.5)
    return x_out, (x, h, rstd, a, b, s)


def ffn_backward(tag, dx, saved, g_norm, wgu, wdn):
    x, h, rstd, a, b, s = saved
    t, d = x.shape
    ns, _, _, f4 = wgu.shape
    tm = _row_tile(t, 512)
    da, db = ffn_bwd_act(f"{tag}_bwd_act", dx, wdn, a, b)
    g_dn = ffn_wgrad_down(f"{tag}_wgrad_down", s, dx)
    g_gu = ffn_wgrad_up(f"{tag}_wgrad_up", h, da, db)
    act_spec = pl.BlockSpec((None, tm, f4), lambda m, j: (j, m, 0))
    w_spec = lambda which: pl.BlockSpec((None, None, d, f4), lambda m, j: (j, which, 0, 0))
    dx_out, g_n = dh_rms_bwd(f"{tag}_dh", [(da, wgu), (db, wgu)], [(act_spec, w_spec(0)), (act_spec, w_spec(1))],
                             (t // tm, ns), ns, x, rstd, g_norm, dx)
    return dx_out, g_n, g_gu, g_dn


def proj_cat(name, h, wcat, bias):
    t, d = h.shape
    n = wcat.shape[1]
    tm = _row_tile(t, 512)
    tn = 768

    def epilogue(acc, ex, outs, ids):
        outs[0][...] = acc + ex[0][...]

    return _mm(
        name, [(h, wcat)], NN, (n // tn, t // tm, 1),
        [(pl.BlockSpec((tm, d), lambda j, m, k: (m, 0)), pl.BlockSpec((d, tn), lambda j, m, k: (0, j)))],
        [jax.ShapeDtypeStruct((t, n), F32)], [pl.BlockSpec((tm, tn), lambda j, m, k: (m, j))], (tm, tn), 1, epilogue,
        extras=[bias], extra_specs=[pl.BlockSpec((1, tn), lambda j, m, k: (0, j))],
    )[0]


def mix_fwd(name, ya, yb, wa, wb, pc):
    t, w = ya.shape
    d = wa.shape[1]
    tm = _row_tile(t, 512)
    tn = 512
    off_a = QKV_COLS // tn
    off_b = (QKV_COLS + d) // tn

    def body(ya_ref, yb_ref, wa_ref, wb_ref, pa_ref, pb_ref, ua_ref, ub_ref, mx_ref):
        ua = _dot(ya_ref[...], wa_ref[...], NN)
        ub = _dot(yb_ref[...], wb_ref[...], NN)
        ua_ref[...] = ua
        ub_ref[...] = ub
        mx_ref[...] = (_sigmoid(pa_ref[...]) * ua + _sigmoid(pb_ref[...]) * ub).astype(BF16)

    y_spec = pl.BlockSpec((tm, w), lambda m, n: (m, 0))
    w_spec = pl.BlockSpec((w, tn), lambda m, n: (0, n))
    o_spec = pl.BlockSpec((tm, tn), lambda m, n: (m, n))
    return pl.pallas_call(
        body, name=name, grid=(t // tm, d // tn),
        in_specs=[y_spec, y_spec, w_spec, w_spec,
                  pl.BlockSpec((tm, tn), lambda m, n: (m, off_a + n)), pl.BlockSpec((tm, tn), lambda m, n: (m, off_b + n))],
        out_specs=[o_spec, o_spec, o_spec],
        out_shape=[jax.ShapeDtypeStruct((t, d), F32), jax.ShapeDtypeStruct((t, d), F32), jax.ShapeDtypeStruct((t, d), BF16)],
        compiler_params=_params(("parallel", "parallel")),
    )(ya, yb, wa, wb, pc, pc)


def mix_bwd(name, dx, wo, pc, ua, ub):
    t, d = dx.shape
    tm = _row_tile(t, 512)
    tn = 512
    off_a = QKV_COLS // tn
    off_b = (QKV_COLS + d) // tn

    def body(dx_ref, wo_ref, pa_ref, pb_ref, ua_ref, ub_ref, dua_ref, dub_ref, dpa_ref, dpb_ref, ba_ref, bb_ref):
        dm = _dot(dx_ref[...], wo_ref[...], NT)
        ga = _sigmoid(pa_ref[...])
        gb = _sigmoid(pb_ref[...])
        dua_ref[...] = (dm * ga).astype(BF16)
        dub_ref[...] = (dm * gb).astype(BF16)
        dpa = dm * ua_ref[...] * ga * (1.0 - ga)
        dpb = dm * ub_ref[...] * gb * (1.0 - gb)
        dpa_ref[...] = dpa.astype(BF16)
        dpb_ref[...] = dpb.astype(BF16)
        sa = jnp.sum(dpa, axis=0, keepdims=True)
        sb = jnp.sum(dpb, axis=0, keepdims=True)

        @pl.when(pl.program_id(1) == 0)
        def _():
            ba_ref[...] = sa
            bb_ref[...] = sb

        @pl.when(pl.program_id(1) != 0)
        def _():
            ba_ref[...] += sa
            bb_ref[...] += sb

    tile = pl.BlockSpec((tm, tn), lambda n, m: (m, n))
    bias = pl.BlockSpec((1, tn), lambda n, m: (0, n))
    return pl.pallas_call(
        body, name=name, grid=(d // tn, t // tm),
        in_specs=[pl.BlockSpec((tm, d), lambda n, m: (m, 0)), pl.BlockSpec((tn, d), lambda n, m: (n, 0)),
                  pl.BlockSpec((tm, tn), lambda n, m: (m, off_a + n)), pl.BlockSpec((tm, tn), lambda n, m: (m, off_b + n)),
                  tile, tile],
        out_specs=[tile, tile, tile, tile, bias, bias],
        out_shape=[jax.ShapeDtypeStruct((t, d), BF16)] * 4 + [jax.ShapeDtypeStruct((1, d), F32)] * 2,
        compiler_params=_params(("parallel", "arbitrary")),
    )(dx, wo, pc, pc, ua, ub)


def mm_plain(name, a, b, contract, out_dtype, tk_target=512):
    if contract == NN:
        m, kdim = a.shape
        n = b.shape[1]
    elif contract == NT:
        m, kdim = a.shape
        n = b.shape[0]
    else:
        kdim, m = a.shape
        n = b.shape[1]
    tm = _row_tile(m, 512)
    tk = _row_tile(kdim, tk_target)
    nk = kdim // tk
    if contract == TN:
        a_spec = pl.BlockSpec((tk, tm), lambda i, k: (k, i))
    else:
        a_spec = pl.BlockSpec((tm, tk), lambda i, k: (i, k))
    if contract == NT:
        b_spec = pl.BlockSpec((n, tk), lambda i, k: (0, k))
    else:
        b_spec = pl.BlockSpec((tk, n), lambda i, k: (k, 0))

    def epilogue(acc, ex, outs, ids):
        outs[0][...] = acc.astype(out_dtype)

    return _mm(name, [(a, b)], contract, (m // tm, nk), [(a_spec, b_spec)],
               [jax.ShapeDtypeStruct((m, n), out_dtype)], [pl.BlockSpec((tm, n), lambda i, k: (i, 0))], (tm, n), nk, epilogue)[0]


def wgrad_cat(name, h, dcat):
    t, d = h.shape
    n = dcat.shape[1]
    tn = 768
    tk = _row_tile(t, 512)

    def epilogue(acc, ex, outs, ids):
        outs[0][...] = acc.astype(BF16)

    return _mm(
        name, [(h, dcat)], TN, (n // tn, t // tk),
        [(pl.BlockSpec((tk, d), lambda j, k: (k, 0)), pl.BlockSpec((tk, tn), lambda j, k: (k, j)))],
        [jax.ShapeDtypeStruct((d, n), BF16)], [pl.BlockSpec((d, tn), lambda j, k: (0, j))], (d, tn), t // tk, epilogue,
    )[0]


def fox_prep(name, f, bias):
    t, lanes = f.shape
    nchunk = t // QB

    def body(f_ref, b_ref, c_ref):
        lower = (_iota2((QB, QB), 1) <= _iota2((QB, QB), 0)).astype(BF16)

        def chunk(n, carry):
            rows = pl.ds(pl.multiple_of(n * QB, QB), QB)
            u = f_ref[rows, :] + b_ref[...]
            lf = jnp.minimum(u, 0.0) - _log1pexp_neg_abs(u)
            c = _dot_exact_lhs01(lower, lf) + carry
            c_ref[rows, :] = c
            return c[QB - 1:QB, :]

        lax.fori_loop(0, nchunk, chunk, jnp.zeros((1, lanes), F32))

    return pl.pallas_call(body, name=name, out_shape=jax.ShapeDtypeStruct((t, lanes), F32),
                          compiler_params=pltpu.CompilerParams(vmem_limit_bytes=VMEM_LIMIT_BYTES))(f, bias)


def fox_gate_bwd(name, dc, f, bias):
    t, lanes = dc.shape
    nchunk = t // QB

    def body(dc_ref, f_ref, b_ref, df_ref, gb_ref):
        upper = (_iota2((QB, QB), 1) >= _iota2((QB, QB), 0)).astype(BF16)

        def chunk(n, carry):
            tail, total = carry
            rows = pl.ds(pl.multiple_of((nchunk - 1 - n) * QB, QB), QB)
            dlf = _dot_exact_lhs01(upper, dc_ref[rows, :]) + tail
            u = f_ref[rows, :] + b_ref[...]
            df = dlf * jnp.exp(jnp.minimum(-u, 0.0) - _log1pexp_neg_abs(u))
            df_ref[rows, :] = df
            return dlf[0:1, :], total + jnp.sum(df, axis=0, keepdims=True)

        zero = jnp.zeros((1, lanes), F32)
        _, total = lax.fori_loop(0, nchunk, chunk, (zero, zero))
        gb_ref[...] = total

    return pl.pallas_call(body, name=name,
                          out_shape=[jax.ShapeDtypeStruct((t, lanes), F32), jax.ShapeDtypeStruct((1, lanes), F32)],
                          compiler_params=pltpu.CompilerParams(vmem_limit_bytes=VMEM_LIMIT_BYTES))(dc, f, bias)


def _rows(i):
    return pl.ds(pl.multiple_of(i * QB, QB), QB)


def _head_spec(t, offset):
    return pl.BlockSpec((None, t, HEAD_DIM), lambda h: (offset + h, 0, 0))


def sb_fwd(name, qkv):
    t = qkv.shape[1]
    nq = t // QB

    def body(q_ref, k_ref, v_ref, o_ref):
        r_io = _iota2((QB, QB), 0)
        c_io = _iota2((QB, QB), 1)
        strict = c_io < r_io
        later = (r_io > c_io).astype(BF16)

        def qblock(i, _):
            q = q_ref[_rows(i), :]

            def tile(j, carry, acc, diag):
                z = _dot(q, k_ref[_rows(j), :], NT) * ATTN_SCALE
                sp = _log1pexp_neg_abs(z)
                lnb = -jnp.maximum(z, 0.0) - sp
                if diag:
                    lnb = jnp.where(strict, lnb, 0.0)
                suffix = _dot_exact_rhs01(lnb, later)
                w = jnp.exp(jnp.minimum(z, 0.0) - sp + suffix + carry)
                if diag:
                    w = jnp.where(strict, w, 0.0)
                acc = acc + _dot(w, v_ref[_rows(j), :], NN)
                return carry + jnp.sum(lnb, axis=1, keepdims=True), acc

            carry, acc = tile(i, jnp.zeros((QB, 1), F32), jnp.zeros((QB, HEAD_DIM), F32), True)
            carry, acc = lax.fori_loop(0, i, lambda n, ca: tile(i - 1 - n, ca[0], ca[1], False), (carry, acc))
            o_ref[_rows(i), :] = acc.astype(BF16)
            return 0

        lax.fori_loop(0, nq, qblock, 0)

    return pl.pallas_call(
        body, name=name, grid=(N_HEADS,),
        in_specs=[_head_spec(t, 0), _head_spec(t, N_HEADS), _head_spec(t, 2 * N_HEADS)],
        out_specs=_head_spec(t, 0), out_shape=jax.ShapeDtypeStruct((N_HEADS, t, HEAD_DIM), BF16),
        compiler_params=_params(("parallel",)),
    )(qkv, qkv, qkv)


def sb_bwd(name, qkv, dy):
    t = qkv.shape[1]
    nq = t // QB

    def body(q_ref, k_ref, v_ref, do_ref, dq_ref, dk_ref, dv_ref, g_s, b_s, dk_acc, dv_acc):
        r_io = _iota2((QB, QB), 0)
        c_io = _iota2((QB, QB), 1)
        strict = c_io < r_io
        later = (r_io > c_io).astype(BF16)
        earlier = (r_io < c_io).astype(BF16)
        dk_acc[...] = jnp.zeros_like(dk_acc)
        dv_acc[...] = jnp.zeros_like(dv_acc)

        def qblock(i, _):
            q = q_ref[_rows(i), :]
            do = do_ref[_rows(i), :]

            def tile1(j, carry, diag):
                z = _dot(q, k_ref[_rows(j), :], NT) * ATTN_SCALE
                sp = _log1pexp_neg_abs(z)
                lnb = -jnp.maximum(z, 0.0) - sp
                lsz = jnp.minimum(z, 0.0) - sp
                if diag:
                    lnb = jnp.where(strict, lnb, 0.0)
                suffix = _dot_exact_rhs01(lnb, later)
                w = jnp.exp(lsz + suffix + carry)
                if diag:
                    w = jnp.where(strict, w, 0.0)
                dw = _dot(do, v_ref[_rows(j), :], NT)
                g_s[j] = w * dw
                b_s[j] = jnp.exp(lsz)
                dv_acc[_rows(j), :] += _dot(w, do, TN)
                return carry + jnp.sum(lnb, axis=1, keepdims=True)

            carry = tile1(i, jnp.zeros((QB, 1), F32), True)
            lax.fori_loop(0, i, lambda n, c: tile1(i - 1 - n, c, False), carry)

            def tile2(j, before, dq, diag):
                g = g_s[j]
                beta = b_s[j]
                prefix = _dot_exact_rhs01(g, earlier) + before
                dz = g * (1.0 - beta) - beta * prefix
                if diag:
                    dz = jnp.where(strict, dz, 0.0)
                dzs = (dz * ATTN_SCALE).astype(BF16)
                dq = dq + _dot(dzs, k_ref[_rows(j), :], NN)
                dk_acc[_rows(j), :] += _dot(dzs, q, TN)
                return before + jnp.sum(g, axis=1, keepdims=True), dq

            before, dq = lax.fori_loop(0, i, lambda j, bd: tile2(j, bd[0], bd[1], False),
                                       (jnp.zeros((QB, 1), F32), jnp.zeros((QB, HEAD_DIM), F32)))
            _, dq = tile2(i, before, dq, True)
            dq_ref[_rows(i), :] = dq.astype(BF16)
            return 0

        lax.fori_loop(0, nq, qblock, 0)
        dk_ref[...] = dk_acc[...].astype(BF16)
        dv_ref[...] = dv_acc[...].astype(BF16)

    out = jax.ShapeDtypeStruct((N_HEADS, t, HEAD_DIM), BF16)
    return pl.pallas_call(
        body, name=name, grid=(N_HEADS,),
        in_specs=[_head_spec(t, 0), _head_spec(t, N_HEADS), _head_spec(t, 2 * N_HEADS), _head_spec(t, 0)],
        out_specs=[_head_spec(t, 0)] * 3, out_shape=[out] * 3,
        scratch_shapes=[pltpu.VMEM((nq, QB, QB), F32), pltpu.VMEM((nq, QB, QB), F32),
                        pltpu.VMEM((t, HEAD_DIM), F32), pltpu.VMEM((t, HEAD_DIM), F32)],
        compiler_params=_params(("parallel",)),
    )(qkv, qkv, qkv, dy)


def _col_spec(t):
    return pl.BlockSpec((None, t, 1), lambda h: (h, 0, 0))


def _row_spec(nq):
    return pl.BlockSpec((None, nq, 1, QB), lambda h: (h, 0, 0, 0))


def fox_fwd(name, qkv, c_col, c_row):
    t = qkv.shape[1]
    nq = t // QB

    def body(q_ref, k_ref, v_ref, cc_ref, cr_ref, o_ref, lse_ref):
        causal = _iota2((QB, QB), 1) <= _iota2((QB, QB), 0)

        def qblock(i, _):
            q = q_ref[_rows(i), :]
            ct = cc_ref[_rows(i), :]

            def tile(j, m, l, acc, diag):
                s = _dot(q, k_ref[_rows(j), :], NT) * ATTN_SCALE + ct - cr_ref[j]
                if diag:
                    s = jnp.where(causal, s, NEG_BIG)
                m_new = jnp.maximum(m, jnp.max(s, axis=1, keepdims=True))
                alpha = jnp.exp(m - m_new)
                p = jnp.exp(s - m_new)
                if diag:
                    p = jnp.where(causal, p, 0.0)
                l = l * alpha + jnp.sum(p, axis=1, keepdims=True)
                acc = acc * alpha + _dot(p, v_ref[_rows(j), :], NN)
                return m_new, l, acc

            m, l, acc = tile(i, jnp.full((QB, 1), NEG_BIG, F32), jnp.zeros((QB, 1), F32),
                             jnp.zeros((QB, HEAD_DIM), F32), True)
            m, l, acc = lax.fori_loop(0, i, lambda n, c: tile(i - 1 - n, c[0], c[1], c[2], False), (m, l, acc))
            o_ref[_rows(i), :] = (acc / l).astype(BF16)
            lse_ref[_rows(i), :] = m + jnp.log(l)
            return 0

        lax.fori_loop(0, nq, qblock, 0)

    return pl.pallas_call(
        body, name=name, grid=(N_HEADS,),
        in_specs=[_head_spec(t, 3 * N_HEADS), _head_spec(t, 4 * N_HEADS), _head_spec(t, 5 * N_HEADS),
                  _col_spec(t), _row_spec(nq)],
        out_specs=[_head_spec(t, 0), _col_spec(t)],
        out_shape=[jax.ShapeDtypeStruct((N_HEADS, t, HEAD_DIM), BF16), jax.ShapeDtypeStruct((N_HEADS, t, 1), F32)],
        compiler_params=_params(("parallel",)),
    )(qkv, qkv, qkv, c_col, c_row)


def fox_bwd(name, qkv, y, dy, lse, c_col, c_row):
    t = qkv.shape[1]
    nq = t // QB

    def body(q_ref, k_ref, v_ref, o_ref, do_ref, lse_ref, cc_ref, cr_ref,
             dq_ref, dk_ref, dv_ref, dcc_ref, dcr_ref, dk_acc, dv_acc, dcr_acc):
        causal = _iota2((QB, QB), 1) <= _iota2((QB, QB), 0)
        dk_acc[...] = jnp.zeros_like(dk_acc)
        dv_acc[...] = jnp.zeros_like(dv_acc)
        dcr_acc[...] = jnp.zeros_like(dcr_acc)

        def qblock(i, _):
            q = q_ref[_rows(i), :]
            do = do_ref[_rows(i), :]
            ct = cc_ref[_rows(i), :]
            lse_i = lse_ref[_rows(i), :]
            delta = jnp.sum(do.astype(F32) * o_ref[_rows(i), :].astype(F32), axis=1, keepdims=True)

            def tile(j, dq, dct, diag):
                s = _dot(q, k_ref[_rows(j), :], NT) * ATTN_SCALE + ct - cr_ref[j]
                p = jnp.exp(s - lse_i)
                if diag:
                    p = jnp.where(causal, p, 0.0)
                dp = _dot(do, v_ref[_rows(j), :], NT)
                ds = p * (dp - delta)
                dv_acc[_rows(j), :] += _dot(p, do, TN)
                dss = (ds * ATTN_SCALE).astype(BF16)
                dq = dq + _dot(dss, k_ref[_rows(j), :], NN)
                dk_acc[_rows(j), :] += _dot(dss, q, TN)
                dcr_acc[j] -= jnp.sum(ds, axis=0, keepdims=True)
                return dq, dct + jnp.sum(ds, axis=1, keepdims=True)

            dq, dct = tile(i, jnp.zeros((QB, HEAD_DIM), F32), jnp.zeros((QB, 1), F32), True)
            dq, dct = lax.fori_loop(0, i, lambda n, c: tile(i - 1 - n, c[0], c[1], False), (dq, dct))
            dq_ref[_rows(i), :] = dq.astype(BF16)
            dcc_ref[_rows(i), :] = dct
            return 0

        lax.fori_loop(0, nq, qblock, 0)
        dk_ref[...] = dk_acc[...].astype(BF16)
        dv_ref[...] = dv_acc[...].astype(BF16)
        dcr_ref[...] = dcr_acc[...]

    out = jax.ShapeDtypeStruct((N_HEADS, t, HEAD_DIM), BF16)
    return pl.pallas_call(
        body, name=name, grid=(N_HEADS,),
        in_specs=[_head_spec(t, 3 * N_HEADS), _head_spec(t, 4 * N_HEADS), _head_spec(t, 5 * N_HEADS),
                  _head_spec(t, 0), _head_spec(t, 0), _col_spec(t), _col_spec(t), _row_spec(nq)],
        out_specs=[_head_spec(t, 0)] * 3 + [_col_spec(t), _row_spec(nq)],
        out_shape=[out] * 3 + [jax.ShapeDtypeStruct((N_HEADS, t, 1), F32), jax.ShapeDtypeStruct((N_HEADS, nq, 1, QB), F32)],
        scratch_shapes=[pltpu.VMEM((t, HEAD_DIM), F32), pltpu.VMEM((t, HEAD_DIM), F32), pltpu.VMEM((nq, 1, QB), F32)],
        compiler_params=_params(("parallel",)),
    )(qkv, qkv, qkv, y, dy, lse, c_col, c_row)


def loss_head(name, x, g, target):
    t, d = x.shape
    tr = _row_tile(t, 256)

    def body(x_ref, g_ref, t_ref, dx_ref, gn_ref, loss_ref):
        xv = x_ref[...]
        r = lax.rsqrt(jnp.mean(xv * xv, axis=-1, keepdims=True) + RMS_EPS)
        xhat = xv * r
        gv = g_ref[...]
        err = xhat * gv - t_ref[...]
        part_loss = 0.5 * jnp.sum(jnp.mean(err * err, axis=-1, keepdims=True), axis=0, keepdims=True)
        dy = err * (1.0 / d)
        dyg = dy * gv
        dx_ref[...] = r * (dyg - xhat * jnp.mean(dyg * xhat, axis=-1, keepdims=True))
        part_g = jnp.sum(dy * xhat, axis=0, keepdims=True)

        @pl.when(pl.program_id(0) == 0)
        def _():
            gn_ref[...] = part_g
            loss_ref[...] = part_loss

        @pl.when(pl.program_id(0) != 0)
        def _():
            gn_ref[...] += part_g
            loss_ref[...] += part_loss

    row = pl.BlockSpec((tr, d), lambda i: (i, 0))
    return pl.pallas_call(
        body, name=name, grid=(t // tr,),
        in_specs=[row, pl.BlockSpec((1, d), lambda i: (0, 0)), row],
        out_specs=[row, pl.BlockSpec((1, d), lambda i: (0, 0)), pl.BlockSpec((1, 1), lambda i: (0, 0))],
        out_shape=[jax.ShapeDtypeStruct((t, d), F32), jax.ShapeDtypeStruct((1, d), F32), jax.ShapeDtypeStruct((1, 1), F32)],
        compiler_params=_params(("arbitrary",)),
    )(x, g, target)


def _place():
    return lax.axis_index("x"), lax.axis_index("y"), lax.axis_index("c")


def _other_chips(x, y):
    return [(1 - x, y), (x, 1 - y), (1 - x, 1 - y)]


def _half(ref, c, rows):
    return ref.at[:, pl.ds(c * (rows // 2), rows // 2), :]


_ANY = pl.BlockSpec(memory_space=pl.ANY)


def gather_weights(name, shards):
    n = len(shards)

    def body(*refs):
        srcs, outs = refs[:n], refs[n:2 * n]
        send_sems, recv_sems, local_sems = refs[2 * n:]
        x, y, c = _place()
        chips = _other_chips(x, y)
        me = 2 * x + y
        sibling = (x, y, 1 - c)
        local, first, passed = [], [], []
        for i in range(n):
            rows = srcs[i].shape[1]
            cp = pltpu.make_async_copy(srcs[i], outs[i].at[me], local_sems.at[i])
            cp.start()
            local.append(cp)
            for j, (qx, qy) in enumerate(chips):
                k = 6 * i + j
                rc = pltpu.make_async_remote_copy(
                    src_ref=_half(srcs[i], c, rows), dst_ref=_half(outs[i].at[me], c, rows),
                    send_sem=send_sems.at[k], recv_sem=recv_sems.at[k], device_id=(qx, qy, c), device_id_type=MESH)
                rc.start()
                first.append(rc)
        for i in range(n):
            rows = srcs[i].shape[1]
            for j, (qx, qy) in enumerate(chips):
                k = 6 * i + j
                block = _half(outs[i].at[2 * qx + qy], c, rows)
                pltpu.make_async_remote_copy(
                    src_ref=block, dst_ref=block, send_sem=send_sems.at[k], recv_sem=recv_sems.at[k],
                    device_id=(qx, qy, c), device_id_type=MESH).wait_recv()
                fw = pltpu.make_async_remote_copy(
                    src_ref=block, dst_ref=block, send_sem=send_sems.at[k + 3], recv_sem=recv_sems.at[k + 3],
                    device_id=sibling, device_id_type=MESH)
                fw.start()
                passed.append(fw)
        for i in range(n):
            rows = srcs[i].shape[1]
            for j, (qx, qy) in enumerate(chips):
                k = 6 * i + j + 3
                block = _half(outs[i].at[2 * qx + qy], 1 - c, rows)
                pltpu.make_async_remote_copy(
                    src_ref=block, dst_ref=block, send_sem=send_sems.at[k], recv_sem=recv_sems.at[k],
                    device_id=sibling, device_id_type=MESH).wait_recv()
        for cp in first + passed:
            cp.wait_send()
        for cp in local:
            cp.wait()

    return pl.pallas_call(
        body, name=name, in_specs=[_ANY] * n, out_specs=[_ANY] * n,
        out_shape=[jax.ShapeDtypeStruct((N_CHIPS,) + s.shape, s.dtype) for s in shards],
        scratch_shapes=[pltpu.SemaphoreType.DMA((6 * n,)), pltpu.SemaphoreType.DMA((6 * n,)), pltpu.SemaphoreType.DMA((n,))],
        compiler_params=pltpu.CompilerParams(has_side_effects=True),
    )(*shards)


def send_half_to_sibling(name, grads):
    n = len(grads)

    def body(*refs):
        srcs, outs = refs[:n], refs[n:2 * n]
        send_sems, recv_sems = refs[2 * n:]
        x, y, c = _place()
        sibling = (x, y, 1 - c)
        copies = []
        for i in range(n):
            rows = srcs[i].shape[2]
            rc = pltpu.make_async_remote_copy(
                src_ref=srcs[i].at[:, :, pl.ds((1 - c) * (rows // 2), rows // 2), :], dst_ref=outs[i],
                send_sem=send_sems.at[i], recv_sem=recv_sems.at[i], device_id=sibling, device_id_type=MESH)
            rc.start()
            copies.append(rc)
        for rc in copies:
            rc.wait()

    def half_shape(g):
        s = g.shape
        return jax.ShapeDtypeStruct((s[0], s[1], s[2] // 2, s[3]), g.dtype)

    return pl.pallas_call(
        body, name=name, in_specs=[_ANY] * n, out_specs=[_ANY] * n, out_shape=[half_shape(g) for g in grads],
        scratch_shapes=[pltpu.SemaphoreType.DMA((n,)), pltpu.SemaphoreType.DMA((n,))],
        compiler_params=pltpu.CompilerParams(has_side_effects=True),
    )(*grads)


def exchange_between_chips(name, parts):
    n = len(parts)

    def body(*refs):
        srcs, outs = refs[:n], refs[n:2 * n]
        send_sems, recv_sems, local_sems = refs[2 * n:]
        x, y, c = _place()
        chips = _other_chips(x, y)
        me = 2 * x + y
        copies, local = [], []
        for i in range(n):
            cp = pltpu.make_async_copy(srcs[i].at[me], outs[i].at[me], local_sems.at[i])
            cp.start()
            local.append(cp)
            for j, (qx, qy) in enumerate(chips):
                k = 3 * i + j
                rc = pltpu.make_async_remote_copy(
                    src_ref=srcs[i].at[2 * qx + qy], dst_ref=outs[i].at[me],
                    send_sem=send_sems.at[k], recv_sem=recv_sems.at[k], device_id=(qx, qy, c), device_id_type=MESH)
                rc.start()
                copies.append(rc)
        for i in range(n):
            for j, (qx, qy) in enumerate(chips):
                k = 3 * i + j
                block = outs[i].at[2 * qx + qy]
                pltpu.make_async_remote_copy(
                    src_ref=block, dst_ref=block, send_sem=send_sems.at[k], recv_sem=recv_sems.at[k],
                    device_id=(qx, qy, c), device_id_type=MESH).wait_recv()
        for rc in copies:
            rc.wait_send()
        for cp in local:
            cp.wait()

    return pl.pallas_call(
        body, name=name, in_specs=[_ANY] * n, out_specs=[_ANY] * n,
        out_shape=[jax.ShapeDtypeStruct(p.shape, p.dtype) for p in parts],
        scratch_shapes=[pltpu.SemaphoreType.DMA((3 * n,)), pltpu.SemaphoreType.DMA((3 * n,)), pltpu.SemaphoreType.DMA((n,))],
        compiler_params=pltpu.CompilerParams(has_side_effects=True),
    )(*parts)


def share_halves(name, halves):
    n = len(halves)

    def body(*refs):
        srcs, outs = refs[:n], refs[n:2 * n]
        send_sems, recv_sems, local_sems = refs[2 * n:]
        x, y, c = _place()
        sibling = (x, y, 1 - c)
        copies, local = [], []
        for i in range(n):
            rows = outs[i].shape[1]
            mine = _half(outs[i], c, rows)
            cp = pltpu.make_async_copy(srcs[i], mine, local_sems.at[i])
            cp.start()
            local.append(cp)
            rc = pltpu.make_async_remote_copy(
                src_ref=srcs[i], dst_ref=mine, send_sem=send_sems.at[i], recv_sem=recv_sems.at[i],
                device_id=sibling, device_id_type=MESH)
            rc.start()
            copies.append(rc)
        for i in range(n):
            rows = outs[i].shape[1]
            theirs = _half(outs[i], 1 - c, rows)
            pltpu.make_async_remote_copy(
                src_ref=theirs, dst_ref=theirs, send_sem=send_sems.at[i], recv_sem=recv_sems.at[i],
                device_id=sibling, device_id_type=MESH).wait_recv()
        for rc in copies:
            rc.wait_send()
        for cp in local:
            cp.wait()

    def full_shape(h):
        s = h.shape
        return jax.ShapeDtypeStruct((s[0], 2 * s[1], s[2]), h.dtype)

    return pl.pallas_call(
        body, name=name, in_specs=[_ANY] * n, out_specs=[_ANY] * n, out_shape=[full_shape(h) for h in halves],
        scratch_shapes=[pltpu.SemaphoreType.DMA((n,)), pltpu.SemaphoreType.DMA((n,)), pltpu.SemaphoreType.DMA((n,))],
        compiler_params=pltpu.CompilerParams(has_side_effects=True),
    )(*halves)


def pair_sum(name, grad, recv, c):
    ns, na, rh, cols = recv.shape
    tr = _row_tile(rh, 256) if rh % 256 == 0 else rh
    nt = rh // tr

    def body(c_ref, g_ref, r_ref, o_ref):
        o_ref[...] = (g_ref[...].astype(F32) + r_ref[...].astype(F32)).astype(BF16)

    blk = (None, None, tr, cols)
    return pl.pallas_call(
        body, name=name,
        grid_spec=pltpu.PrefetchScalarGridSpec(
            num_scalar_prefetch=1, grid=(ns, na, nt),
            in_specs=[pl.BlockSpec(blk, lambda s, a, r, c_ref: (s, a, c_ref[0] * nt + r, 0)),
                      pl.BlockSpec(blk, lambda s, a, r, c_ref: (s, a, r, 0))],
            out_specs=pl.BlockSpec(blk, lambda s, a, r, c_ref: (s, a, r, 0))),
        out_shape=jax.ShapeDtypeStruct(recv.shape, BF16),
        compiler_params=_params(("parallel", "parallel", "parallel")),
    )(c, grad, recv)


def chip_sum(name, parts):
    ns, na, rh, cols = parts.shape
    tr = _row_tile(rh, 256) if rh % 256 == 0 else rh

    def body(p_ref, o_ref):
        total = p_ref[0].astype(F32)
        for s in range(1, ns):
            total = total + p_ref[s].astype(F32)
        o_ref[...] = total

    return pl.pallas_call(
        body, name=name, grid=(na, rh // tr),
        in_specs=[pl.BlockSpec((ns, None, tr, cols), lambda a, r: (0, a, r, 0))],
        out_specs=pl.BlockSpec((None, tr, cols), lambda a, r: (a, r, 0)),
        out_shape=jax.ShapeDtypeStruct((na, rh, cols), F32),
        compiler_params=_params(("parallel", "parallel")),
    )(parts)


def reduce_scatter(tag, grads, c):
    recv = send_half_to_sibling(f"{tag}_to_sibling", grads)
    parts = [pair_sum(f"{tag}_pair_sum{i}", g, r, c) for i, (g, r) in enumerate(zip(grads, recv))]
    landed = exchange_between_chips(f"{tag}_between_chips", parts)
    halves = [chip_sum(f"{tag}_chip_sum{i}", p) for i, p in enumerate(landed)]
    return share_halves(f"{tag}_share", halves)


def _adamw_math(w, g, m, v):
    m = ADAM_B1 * m + (1.0 - ADAM_B1) * g
    v = ADAM_B2 * v + (1.0 - ADAM_B2) * (g * g)
    m_hat = m / (1.0 - ADAM_B1 ** ADAM_STEP)
    v_hat = v / (1.0 - ADAM_B2 ** ADAM_STEP)
    delta = -ADAM_LR * (m_hat / (jnp.sqrt(v_hat) + ADAM_EPS) + ADAM_WD * w)
    return delta, m, v


def adamw(name, w, g, m, v):
    rows, cols = w.shape
    tr = _row_tile(rows, 256) if rows % 256 == 0 else rows // 2

    def body(w_ref, g_ref, m_ref, v_ref, d_ref, mo_ref, vo_ref):
        d_ref[...], mo_ref[...], vo_ref[...] = _adamw_math(w_ref[...], g_ref[...], m_ref[...], v_ref[...])

    blk = pl.BlockSpec((tr, cols), lambda i: (i, 0))
    return pl.pallas_call(
        body, name=name, grid=(rows // tr,), in_specs=[blk] * 4, out_specs=[blk] * 3,
        out_shape=[jax.ShapeDtypeStruct(w.shape, F32)] * 3, compiler_params=_params(("parallel",)),
    )(w, g, m, v)


def small_allreduce_adamw(name, g_part, w, m, v):
    rows, cols = g_part.shape

    def body(g_ref, w_ref, m_ref, v_ref, sum_ref, d_ref, mo_ref, vo_ref, land, send_sems, recv_sems):
        x, y, c = _place()
        me = 4 * x + 2 * y + c
        land[me] = g_ref[...]
        copies = []
        for r in range(1, 8):
            peer = (x ^ (r >> 2), y ^ ((r >> 1) & 1), c ^ (r & 1))
            rc = pltpu.make_async_remote_copy(
                src_ref=g_ref, dst_ref=land.at[me], send_sem=send_sems.at[r - 1], recv_sem=recv_sems.at[r - 1],
                device_id=peer, device_id_type=MESH)
            rc.start()
            copies.append(rc)
        for rc in copies:
            rc.wait()
        total = land[0]
        for s in range(1, 8):
            total = total + land[s]
        sum_ref[...] = total
        d_ref[...], mo_ref[...], vo_ref[...] = _adamw_math(w_ref[...], total, m_ref[...], v_ref[...])

    vmem = pl.BlockSpec(memory_space=pltpu.VMEM)
    return pl.pallas_call(
        body, name=name, in_specs=[vmem] * 4, out_specs=[vmem] * 4,
        out_shape=[jax.ShapeDtypeStruct((rows, cols), F32)] * 4,
        scratch_shapes=[pltpu.VMEM((8, rows, cols), F32), pltpu.SemaphoreType.DMA((7,)), pltpu.SemaphoreType.DMA((7,))],
        compiler_params=pltpu.CompilerParams(has_side_effects=True),
    )(g_part, w, m, v)


def _heads(a):
    t, w = a.shape
    return a.reshape(t, w // HEAD_DIM, HEAD_DIM).transpose(1, 0, 2)


def _unheads(a):
    n, t, _ = a.shape
    return a.transpose(1, 0, 2).reshape(t, n * HEAD_DIM)


def _cols_from_shards(g):
    ns, r, cols = g.shape
    return g.transpose(1, 0, 2).reshape(r, ns * cols)


def _shards_from_cols(a):
    r, cols = a.shape
    return a.reshape(r, N_CHIPS, cols // N_CHIPS).transpose(1, 0, 2)


def kernel(x, norm_ffn1, w_ffn1_gate, w_ffn1_up, w_ffn1_down, norm_mix, w_in, b_forget, w_gate, b_gate, w_up_a, w_up_b, w_out, norm_ffn2, w_ffn2_gate, w_ffn2_up, w_ffn2_down, norm_final, loss_target, m_norm_ffn1, m_w_ffn1_gate, m_w_ffn1_up, m_w_ffn1_down, m_norm_mix, m_w_in, m_b_forget, m_w_gate, m_b_gate, m_w_up_a, m_w_up_b, m_w_out, m_norm_ffn2, m_w_ffn2_gate, m_w_ffn2_up, m_w_ffn2_down, m_norm_final, v_norm_ffn1, v_w_ffn1_gate, v_w_ffn1_up, v_w_ffn1_down, v_norm_mix, v_w_in, v_b_forget, v_w_gate, v_b_gate, v_w_up_a, v_w_up_b, v_w_out, v_norm_ffn2, v_w_ffn2_gate, v_w_ffn2_up, v_w_ffn2_down, v_norm_final):
    t, d = x.shape[1], x.shape[2]
    in4 = w_in.shape[2]
    gate4 = w_gate.shape[2]
    up4 = w_up_a.shape[2]
    in_cols = N_CHIPS * in4
    n_forget = in_cols - QKV_COLS
    assert w_up_a.shape[1] == WIDTH and d == 2 * WIDTH and n_forget == N_HEADS
    nq = t // QB
    c_idx = lax.axis_index("c")
    c_arr = jnp.reshape(c_idx, (1,)).astype(jnp.int32)
    x2d = x[0]
    target = loss_target[0]

    def pack_ffn(wg, wu, wd):
        return jnp.stack([wg[0], wu[0]]).astype(BF16), wd.astype(BF16)

    gu1_sh, dn1_sh = pack_ffn(w_ffn1_gate, w_ffn1_up, w_ffn1_down)
    gu2_sh, dn2_sh = pack_ffn(w_ffn2_gate, w_ffn2_up, w_ffn2_down)
    mx_sh = jnp.concatenate([w_in[0], w_gate[0], jnp.concatenate([w_up_a[0], w_up_b[0]], axis=0)], axis=1).astype(BF16)[None]
    wo_sh = w_out.astype(BF16)
    gu1, dn1 = gather_weights("gather_ffn1", [gu1_sh, dn1_sh])
    mx, wo = gather_weights("gather_mixer", [mx_sh, wo_sh])
    gu2, dn2 = gather_weights("gather_ffn2", [gu2_sh, dn2_sh])
    dn1 = dn1[:, 0]
    dn2 = dn2[:, 0]
    mx = mx[:, 0]
    w_in_full = _cols_from_shards(mx[:, :, :in4])
    w_gate_full = _cols_from_shards(mx[:, :, in4:in4 + gate4])
    w_up_full = _cols_from_shards(mx[:, :, in4 + gate4:])
    w_up_a_full, w_up_b_full = w_up_full[:WIDTH], w_up_full[WIDTH:]
    wcat = jnp.concatenate([w_in_full[:, :QKV_COLS], w_gate_full, w_in_full[:, QKV_COLS:],
                            jnp.zeros((d, F_PAD - n_forget), BF16)], axis=1)
    bias_cat = jnp.concatenate([jnp.zeros((1, QKV_COLS), F32), b_gate, jnp.zeros((1, F_PAD), F32)], axis=1)
    f_off = QKV_COLS + 2 * d
    wo_full = wo.reshape(d, d)
    b_forget_row = jnp.pad(b_forget, ((0, 0), (0, QB - n_forget)))

    x1, saved1 = ffn_forward("ffn1", x2d, norm_ffn1, gu1, dn1)
    h2, rstd2 = rms_fwd("mix_rms", x1, norm_mix)
    pc = proj_cat("mix_proj", h2, wcat, bias_cat)
    qkv = _heads(pc[:, :QKV_COLS].astype(BF16))
    f_logit = pc[:, f_off:f_off + QB]
    c_cum = fox_prep("fox_prep", f_logit, b_forget_row)
    c_heads = c_cum[:, :N_HEADS].T
    c_col = c_heads[:, :, None]
    c_row = c_heads.reshape(N_HEADS, nq, 1, QB)
    ya_h = sb_fwd("sb_fwd", qkv)
    yb_h, lse = fox_fwd("fox_fwd", qkv, c_col, c_row)
    ya = _unheads(ya_h)
    yb = _unheads(yb_h)
    ua, ub, mixed = mix_fwd("mix_fwd", ya, yb, w_up_a_full, w_up_b_full, pc)
    x2 = mm_residual("mix_out", mixed[None], wo_full[None], x1, 1.0)
    x3, saved2 = ffn_forward("ffn2", x2, norm_ffn2, gu2, dn2)
    dx3, gn_final, loss_part = loss_head("loss_head", x3, norm_final[None], target)

    dx2, gn_ffn2, g_gu2, g_dn2 = ffn_backward("ffn2", dx3, saved2, norm_ffn2, gu2, dn2)
    s_gu2, s_dn2 = reduce_scatter("rs_ffn2", [g_gu2, g_dn2], c_arr)

    dua, dub, dpa, dpb, gba, gbb = mix_bwd("mix_bwd", dx2, wo_full, pc, ua, ub)
    dgp = jnp.concatenate([dpa, dpb], axis=1)
    g_bgate = jnp.concatenate([gba, gbb], axis=1)
    g_wo = mm_plain("wgrad_out", mixed, dx2, TN, BF16)
    dya = mm_plain("dya", dua, w_up_a_full, NT, BF16, tk_target=1024)
    dyb = mm_plain("dyb", dub, w_up_b_full, NT, BF16, tk_target=1024)
    g_up_a = mm_plain("wgrad_up_a", ya, dua, TN, BF16)
    g_up_b = mm_plain("wgrad_up_b", yb, dub, TN, BF16)
    dqa, dka, dva = sb_bwd("sb_bwd", qkv, _heads(dya))
    dqb, dkb, dvb, dcc, dcr = fox_bwd("fox_bwd", qkv, yb_h, _heads(dyb), lse, c_col, c_row)
    dc = (dcc[:, :, 0] + dcr.reshape(N_HEADS, t)).T
    df, g_bf = fox_gate_bwd("fox_gate_bwd", jnp.pad(dc, ((0, 0), (0, QB - N_HEADS))), f_logit, b_forget_row)
    dqkv = _unheads(jnp.concatenate([dqa, dka, dva, dqb, dkb, dvb], axis=0))
    dcat = jnp.concatenate([dqkv, dgp, df.astype(BF16), jnp.zeros((t, F_PAD - QB), BF16)], axis=1)
    g_wcat = wgrad_cat("wgrad_cat", h2, dcat)
    tm = _row_tile(t, 512)
    tkc = 768
    nkc = wcat.shape[1] // tkc
    dx1, gn_mix = dh_rms_bwd(
        "mix_dh", [(dcat, wcat)],
        [(pl.BlockSpec((tm, tkc), lambda m, k: (m, k)), pl.BlockSpec((d, tkc), lambda m, k: (0, k)))],
        (t // tm, nkc), nkc, x1, rstd2, norm_mix, dx2)
    g_in = jnp.concatenate([g_wcat[:, :QKV_COLS], g_wcat[:, f_off:f_off + n_forget]], axis=1)
    g_mx = jnp.concatenate([_shards_from_cols(g_in), _shards_from_cols(g_wcat[:, QKV_COLS:f_off]),
                            _shards_from_cols(jnp.concatenate([g_up_a, g_up_b], axis=0))], axis=2)[:, None]
    s_mx, s_wo = reduce_scatter("rs_mixer", [g_mx, g_wo.reshape(N_CHIPS, 1, d // N_CHIPS, d)], c_arr)

    dx0, gn_ffn1, g_gu1, g_dn1 = ffn_backward("ffn1", dx1, saved1, norm_ffn1, gu1, dn1)
    s_gu1, s_dn1 = reduce_scatter("rs_ffn1", [g_gu1, g_dn1], c_arr)

    def pack_small(n1, nm, n2, nf, bg, bf, last):
        return jnp.concatenate([n1, nm, n2, nf, bg.reshape(2, d), jnp.pad(bf, ((0, 0), (0, d - n_forget))), last], axis=0)

    zero_row = jnp.zeros((1, d), F32)
    g_small = pack_small(gn_ffn1, gn_mix, gn_ffn2, gn_final, g_bgate, g_bf[:, :n_forget], jnp.pad(loss_part, ((0, 0), (0, d - 1))))
    w_small = pack_small(norm_ffn1, norm_mix, norm_ffn2, norm_final[None], b_gate, b_forget, zero_row)
    m_small = pack_small(m_norm_ffn1, m_norm_mix, m_norm_ffn2, m_norm_final[None], m_b_gate, m_b_forget, zero_row)
    v_small = pack_small(v_norm_ffn1, v_norm_mix, v_norm_ffn2, v_norm_final[None], v_b_gate, v_b_forget, zero_row)
    smalls = small_allreduce_adamw("small_allreduce_adamw", g_small, w_small, m_small, v_small)

    def unpack_small(p):
        return {"norm_ffn1": p[0:1], "norm_mix": p[1:2], "norm_ffn2": p[2:3], "norm_final": p[3], "b_gate": p[4:6].reshape(1, 2 * d),
                "b_forget": p[6:7, :n_forget]}

    loss = smalls[0][7, 0]
    small_out = [unpack_small(p) for p in smalls]

    grads = {
        "w_ffn1_gate": s_gu1[0], "w_ffn1_up": s_gu1[1], "w_ffn1_down": s_dn1[0],
        "w_in": s_mx[0][:, :in4], "w_gate": s_mx[0][:, in4:in4 + gate4],
        "w_up_a": s_mx[0][:WIDTH, in4 + gate4:], "w_up_b": s_mx[0][WIDTH:, in4 + gate4:], "w_out": s_wo[0],
        "w_ffn2_gate": s_gu2[0], "w_ffn2_up": s_gu2[1], "w_ffn2_down": s_dn2[0],
    }
    weights = {"w_ffn1_gate": (w_ffn1_gate, m_w_ffn1_gate, v_w_ffn1_gate), "w_ffn1_up": (w_ffn1_up, m_w_ffn1_up, v_w_ffn1_up),
               "w_ffn1_down": (w_ffn1_down, m_w_ffn1_down, v_w_ffn1_down), "w_in": (w_in, m_w_in, v_w_in),
               "w_gate": (w_gate, m_w_gate, v_w_gate), "w_up_a": (w_up_a, m_w_up_a, v_w_up_a), "w_up_b": (w_up_b, m_w_up_b, v_w_up_b),
               "w_out": (w_out, m_w_out, v_w_out), "w_ffn2_gate": (w_ffn2_gate, m_w_ffn2_gate, v_w_ffn2_gate),
               "w_ffn2_up": (w_ffn2_up, m_w_ffn2_up, v_w_ffn2_up), "w_ffn2_down": (w_ffn2_down, m_w_ffn2_down, v_w_ffn2_down)}
    big_out = {}
    for wname, (w, m, v) in weights.items():
        g = grads[wname]
        delta, new_m, new_v = adamw(f"adamw_{wname}", w[0], g, m[0], v[0])
        big_out[wname] = (g[None], delta[None], new_m[None], new_v[None])

    order = ["norm_ffn1", "w_ffn1_gate", "w_ffn1_up", "w_ffn1_down", "norm_mix", "w_in", "b_forget", "w_gate", "b_gate",
             "w_up_a", "w_up_b", "w_out", "norm_ffn2", "w_ffn2_gate", "w_ffn2_up", "w_ffn2_down", "norm_final"]
    outs = [loss, dx0[None]]
    for kind in range(4):
        for wname in order:
            outs.append(big_out[wname][kind] if wname in big_out else small_out[kind][wname])
    return tuple(outs)
```

```python
import functools

import jax
import jax.numpy as jnp
from jax import lax
from jax.experimental import pallas as pl
from jax.experimental.pallas import tpu as pltpu

F32 = jnp.float32
BF16 = jnp.bfloat16

HEAD_DIM = 64
N_HEADS = 8
WIDTH = N_HEADS * HEAD_DIM
QKV_COLS = 6 * WIDTH
RMS_EPS = 1e-6
ATTN_SCALE = HEAD_DIM ** -0.5
N_CHIPS = 4
QB = 128
BQ = 512
CS = 256
N_SUB = BQ // CS
F_PAD = 256
NEG_BIG = -1e30

ADAM_LR = 0.001
ADAM_B1 = 0.9
ADAM_B2 = 0.999
ADAM_EPS = 1e-08
ADAM_WD = 0.01
ADAM_STEP = 10

VMEM_LIMIT_BYTES = 48 * 1024 * 1024
MESH = pl.DeviceIdType.MESH

NN = ((1,), (0,))
NT = ((1,), (1,))
TN = ((0,), (0,))


def _params(semantics):
    return pltpu.CompilerParams(dimension_semantics=semantics, vmem_limit_bytes=VMEM_LIMIT_BYTES)


def _dot(a, b, contract):
    return lax.dot_general(a.astype(BF16), b.astype(BF16), (contract, ((), ())), preferred_element_type=F32)


def _sigmoid(x):
    return 1.0 / (1.0 + jnp.exp(-x))


def _log1pexp_neg_abs(z):
    return jnp.log(1.0 + jnp.exp(-jnp.abs(z)))


def _split3(x):
    hi = x.astype(BF16)
    r1 = x - hi.astype(F32)
    mid = r1.astype(BF16)
    lo = (r1 - mid.astype(F32)).astype(BF16)
    return hi, mid, lo


def _dot_exact_rhs01(x, m01):
    hi, mid, lo = _split3(x)
    d = lambda p: lax.dot_general(p, m01, (NN, ((), ())), preferred_element_type=F32)
    return d(hi) + d(mid) + d(lo)


def _dot_exact_lhs01(m01, x):
    hi, mid, lo = _split3(x)
    d = lambda p: lax.dot_general(m01, p, (NN, ((), ())), preferred_element_type=F32)
    return d(hi) + d(mid) + d(lo)


def _iota2(shape, dim):
    return lax.broadcasted_iota(jnp.int32, shape, dim)


def _mm(name, pairs, contract, grid, pair_specs, out_shape, out_specs, acc_shape, nk, epilogue,
        extras=(), extra_specs=(), semantics=None):
    n_pairs = len(pairs)
    n_extra = len(extras)
    n_out = len(out_shape)

    def body(*refs):
        ab = refs[:2 * n_pairs]
        ex = refs[2 * n_pairs:2 * n_pairs + n_extra]
        outs = refs[2 * n_pairs + n_extra:2 * n_pairs + n_extra + n_out]
        acc = refs[-1]
        ids = [pl.program_id(i) for i in range(len(grid))]
        k = ids[-1]

        @pl.when(k == 0)
        def _():
            acc[...] = jnp.zeros_like(acc)

        part = _dot(ab[0][...], ab[1][...], contract)
        for p in range(1, n_pairs):
            part += _dot(ab[2 * p][...], ab[2 * p + 1][...], contract)
        acc[...] += part

        @pl.when(k == nk - 1)
        def _():
            epilogue(acc[...], ex, outs, ids)

    operands = [t for pair in pairs for t in pair] + list(extras)
    in_specs = [s for pair in pair_specs for s in pair] + list(extra_specs)
    if semantics is None:
        semantics = ("parallel",) * (len(grid) - 1) + ("arbitrary",)
    return pl.pallas_call(
        body, name=name, grid=grid, in_specs=in_specs, out_specs=list(out_specs), out_shape=list(out_shape),
        scratch_shapes=[pltpu.VMEM(acc_shape, F32)], compiler_params=_params(semantics),
    )(*operands)


def _row_tile(rows, target):
    t = min(rows, target)
    while rows % t:
        t //= 2
    return t


def rms_fwd(name, x, g):
    t, d = x.shape
    tr = _row_tile(t, 256)

    def body(x_ref, g_ref, h_ref, r_ref):
        xv = x_ref[...]
        r = lax.rsqrt(jnp.mean(xv * xv, axis=-1, keepdims=True) + RMS_EPS)
        h_ref[...] = (xv * r * g_ref[...]).astype(BF16)
        r_ref[...] = r

    return pl.pallas_call(
        body, name=name, grid=(t // tr,),
        in_specs=[pl.BlockSpec((tr, d), lambda i: (i, 0)), pl.BlockSpec((1, d), lambda i: (0, 0))],
        out_specs=[pl.BlockSpec((tr, d), lambda i: (i, 0)), pl.BlockSpec((tr, 1), lambda i: (i, 0))],
        out_shape=[jax.ShapeDtypeStruct((t, d), BF16), jax.ShapeDtypeStruct((t, 1), F32)],
        compiler_params=_params(("parallel",)),
    )(x, g)


def ffn_up(name, h, wgu):
    t, d = h.shape
    ns, _, _, f4 = wgu.shape
    tm = _row_tile(t, 512)

    def body(h_ref, wg_ref, wu_ref, a_ref, b_ref, s_ref):
        hv = h_ref[...]
        a = _dot(hv, wg_ref[...], NN)
        b = _dot(hv, wu_ref[...], NN)
        a_ref[...] = a
        b_ref[...] = b
        s_ref[...] = (a * _sigmoid(a) * b).astype(BF16)

    act = lambda dt: jax.ShapeDtypeStruct((ns, t, f4), dt)
    act_spec = pl.BlockSpec((None, tm, f4), lambda j, m: (j, m, 0))
    return pl.pallas_call(
        body, name=name, grid=(ns, t // tm),
        in_specs=[pl.BlockSpec((tm, d), lambda j, m: (m, 0)),
                  pl.BlockSpec((None, None, d, f4), lambda j, m: (j, 0, 0, 0)),
                  pl.BlockSpec((None, None, d, f4), lambda j, m: (j, 1, 0, 0))],
        out_specs=[act_spec, act_spec, act_spec],
        out_shape=[act(F32), act(F32), act(BF16)],
        compiler_params=_params(("parallel", "parallel")),
    )(h, wgu, wgu)


def mm_residual(name, s, w, x, scale):
    nj, t, kdim = s.shape
    n = w.shape[2]
    tm = _row_tile(t, 512)

    def epilogue(acc, ex, outs, ids):
        outs[0][...] = ex[0][...] + scale * acc

    return _mm(
        name, [(s, w)], NN, (t // tm, nj),
        [(pl.BlockSpec((None, tm, kdim), lambda m, j: (j, m, 0)), pl.BlockSpec((None, kdim, n), lambda m, j: (j, 0, 0)))],
        [jax.ShapeDtypeStruct((t, n), F32)], [pl.BlockSpec((tm, n), lambda m, j: (m, 0))], (tm, n), nj, epilogue,
        extras=[x], extra_specs=[pl.BlockSpec((tm, n), lambda m, j: (m, 0))],
    )[0]


def ffn_bwd_act(name, dx, wd, a, b):
    t, d = dx.shape
    ns, f4, _ = wd.shape
    tm = _row_tile(t, 512)

    def body(dx_ref, wd_ref, a_ref, b_ref, da_ref, db_ref):
        ds = _dot(0.5 * dx_ref[...], wd_ref[...], NT)
        av = a_ref[...]
        sig = _sigmoid(av)
        da_ref[...] = (ds * b_ref[...] * (sig * (1.0 + av * (1.0 - sig)))).astype(BF16)
        db_ref[...] = (ds * (av * sig)).astype(BF16)

    act_spec = pl.BlockSpec((None, tm, f4), lambda j, m: (j, m, 0))
    return pl.pallas_call(
        body, name=name, grid=(ns, t // tm),
        in_specs=[pl.BlockSpec((tm, d), lambda j, m: (m, 0)), pl.BlockSpec((None, f4, d), lambda j, m: (j, 0, 0)),
                  act_spec, act_spec],
        out_specs=[act_spec, act_spec],
        out_shape=[jax.ShapeDtypeStruct((ns, t, f4), BF16)] * 2,
        compiler_params=_params(("parallel", "parallel")),
    )(dx, wd, a, b)


def ffn_wgrad_down(name, s, dx):
    ns, t, f4 = s.shape
    d = dx.shape[1]
    tk = _row_tile(t, 512)

    def epilogue(acc, ex, outs, ids):
        outs[0][...] = (0.5 * acc).astype(BF16)

    return _mm(
        name, [(s, dx)], TN, (ns, t // tk),
        [(pl.BlockSpec((None, tk, f4), lambda j, k: (j, k, 0)), pl.BlockSpec((tk, d), lambda j, k: (k, 0)))],
        [jax.ShapeDtypeStruct((ns, 1, f4, d), BF16)], [pl.BlockSpec((None, None, f4, d), lambda j, k: (j, 0, 0, 0))],
        (f4, d), t // tk, epilogue,
    )[0]


def ffn_wgrad_up(name, h, da, db):
    t, d = h.shape
    ns, _, f4 = da.shape
    tk = _row_tile(t, 512)

    def body(h_ref, da_ref, db_ref, o_ref, acc_a, acc_b):
        k = pl.program_id(1)

        @pl.when(k == 0)
        def _():
            acc_a[...] = jnp.zeros_like(acc_a)
            acc_b[...] = jnp.zeros_like(acc_b)

        hv = h_ref[...]
        acc_a[...] += _dot(hv, da_ref[...], TN)
        acc_b[...] += _dot(hv, db_ref[...], TN)

        @pl.when(k == t // tk - 1)
        def _():
            o_ref[0] = acc_a[...].astype(BF16)
            o_ref[1] = acc_b[...].astype(BF16)

    act_spec = pl.BlockSpec((None, tk, f4), lambda j, k: (j, k, 0))
    return pl.pallas_call(
        body, name=name, grid=(ns, t // tk),
        in_specs=[pl.BlockSpec((tk, d), lambda j, k: (k, 0)), act_spec, act_spec],
        out_specs=pl.BlockSpec((None, 2, d, f4), lambda j, k: (j, 0, 0, 0)),
        out_shape=jax.ShapeDtypeStruct((ns, 2, d, f4), BF16),
        scratch_shapes=[pltpu.VMEM((d, f4), F32), pltpu.VMEM((d, f4), F32)],
        compiler_params=_params(("parallel", "arbitrary")),
    )(h, da, db)


def dh_rms_bwd(name, pairs, pair_specs, grid, nk, x, rstd, g, dx_in):
    t, d = x.shape
    tm = t // grid[0]

    def epilogue(acc, ex, outs, ids):
        x_ref, r_ref, g_ref, dxin_ref = ex
        r = r_ref[...]
        xhat = x_ref[...] * r
        dhg = acc * g_ref[...]
        proj = jnp.mean(dhg * xhat, axis=-1, keepdims=True)
        outs[0][...] = dxin_ref[...] + r * (dhg - xhat * proj)
        part = jnp.sum(acc * xhat, axis=0, keepdims=True)

        @pl.when(ids[0] == 0)
        def _():
            outs[1][...] = part

        @pl.when(ids[0] != 0)
        def _():
            outs[1][...] += part

    row = pl.BlockSpec((tm, d), lambda m, k: (m, 0))
    return _mm(
        name, pairs, NT, grid, pair_specs,
        [jax.ShapeDtypeStruct((t, d), F32), jax.ShapeDtypeStruct((1, d), F32)],
        [row, pl.BlockSpec((1, d), lambda m, k: (0, 0))], (tm, d), nk, epilogue,
        extras=[x, rstd, g, dx_in],
        extra_specs=[row, pl.BlockSpec((tm, 1), lambda m, k: (m, 0)), pl.BlockSpec((1, d), lambda m, k: (0, 0)), row],
        semantics=("arbitrary", "arbitrary"),
    )


def ffn_forward(tag, x, g_norm, wgu, wdn):
    h, rstd = rms_fwd(f"{tag}_rms", x, g_norm)
    a, b, s = ffn_up(f"{tag}_up", h, wgu)
    x_out = mm_residual(f"{tag}_down", s, wdn, x, 0.5)
    return x_out, (x, h, rstd, a, b, s)


def ffn_backward(tag, dx, saved, g_norm, wgu, wdn):
    x, h, rstd, a, b, s = saved
    t, d = x.shape
    ns, _, _, f4 = wgu.shape
    tm = _row_tile(t, 512)
    da, db = ffn_bwd_act(f"{tag}_bwd_act", dx, wdn, a, b)
    g_dn = ffn_wgrad_down(f"{tag}_wgrad_down", s, dx)
    g_gu = ffn_wgrad_up(f"{tag}_wgrad_up", h, da, db)
    act_spec = pl.BlockSpec((None, tm, f4), lambda m, j: (j, m, 0))
    w_spec = lambda which: pl.BlockSpec((None, None, d, f4), lambda m, j: (j, which, 0, 0))
    dx_out, g_n = dh_rms_bwd(f"{tag}_dh", [(da, wgu), (db, wgu)], [(act_spec, w_spec(0)), (act_spec, w_spec(1))],
                             (t // tm, ns), ns, x, rstd, g_norm, dx)
    return dx_out, g_n, g_gu, g_dn


def proj_cat(name, h, wcat, bias):
    t, d = h.shape
    n = wcat.shape[1]
    tm = _row_tile(t, 512)
    tn = 768

    def epilogue(acc, ex, outs, ids):
        outs[0][...] = acc + ex[0][...]

    return _mm(
        name, [(h, wcat)], NN, (n // tn, t // tm, 1),
        [(pl.BlockSpec((tm, d), lambda j, m, k: (m, 0)), pl.BlockSpec((d, tn), lambda j, m, k: (0, j)))],
        [jax.ShapeDtypeStruct((t, n), F32)], [pl.BlockSpec((tm, tn), lambda j, m, k: (m, j))], (tm, tn), 1, epilogue,
        extras=[bias], extra_specs=[pl.BlockSpec((1, tn), lambda j, m, k: (0, j))],
    )[0]


def mix_fwd(name, ya, yb, wa, wb, pc):
    t, w = ya.shape
    d = wa.shape[1]
    tm = _row_tile(t, 512)
    tn = 512
    off_a = QKV_COLS // tn
    off_b = (QKV_COLS + d) // tn

    def body(ya_ref, yb_ref, wa_ref, wb_ref, pa_ref, pb_ref, ua_ref, ub_ref, mx_ref):
        ua = _dot(ya_ref[...], wa_ref[...], NN)
        ub = _dot(yb_ref[...], wb_ref[...], NN)
        ua_ref[...] = ua
        ub_ref[...] = ub
        mx_ref[...] = (_sigmoid(pa_ref[...]) * ua + _sigmoid(pb_ref[...]) * ub).astype(BF16)

    y_spec = pl.BlockSpec((tm, w), lambda m, n: (m, 0))
    w_spec = pl.BlockSpec((w, tn), lambda m, n: (0, n))
    o_spec = pl.BlockSpec((tm, tn), lambda m, n: (m, n))
    return pl.pallas_call(
        body, name=name, grid=(t // tm, d // tn),
        in_specs=[y_spec, y_spec, w_spec, w_spec,
                  pl.BlockSpec((tm, tn), lambda m, n: (m, off_a + n)), pl.BlockSpec((tm, tn), lambda m, n: (m, off_b + n))],
        out_specs=[o_spec, o_spec, o_spec],
        out_shape=[jax.ShapeDtypeStruct((t, d), F32), jax.ShapeDtypeStruct((t, d), F32), jax.ShapeDtypeStruct((t, d), BF16)],
        compiler_params=_params(("parallel", "parallel")),
    )(ya, yb, wa, wb, pc, pc)


def mix_bwd(name, dx, wo, pc, ua, ub):
    t, d = dx.shape
    tm = _row_tile(t, 512)
    tn = 512
    off_a = QKV_COLS // tn
    off_b = (QKV_COLS + d) // tn

    def body(dx_ref, wo_ref, pa_ref, pb_ref, ua_ref, ub_ref, dua_ref, dub_ref, dpa_ref, dpb_ref, ba_ref, bb_ref):
        dm = _dot(dx_ref[...], wo_ref[...], NT)
        ga = _sigmoid(pa_ref[...])
        gb = _sigmoid(pb_ref[...])
        dua_ref[...] = (dm * ga).astype(BF16)
        dub_ref[...] = (dm * gb).astype(BF16)
        dpa = dm * ua_ref[...] * ga * (1.0 - ga)
        dpb = dm * ub_ref[...] * gb * (1.0 - gb)
        dpa_ref[...] = dpa.astype(BF16)
        dpb_ref[...] = dpb.astype(BF16)
        sa = jnp.sum(dpa, axis=0, keepdims=True)
        sb = jnp.sum(dpb, axis=0, keepdims=True)

        @pl.when(pl.program_id(1) == 0)
        def _():
            ba_ref[...] = sa
            bb_ref[...] = sb

        @pl.when(pl.program_id(1) != 0)
        def _():
            ba_ref[...] += sa
            bb_ref[...] += sb

    tile = pl.BlockSpec((tm, tn), lambda n, m: (m, n))
    bias = pl.BlockSpec((1, tn), lambda n, m: (0, n))
    return pl.pallas_call(
        body, name=name, grid=(d // tn, t // tm),
        in_specs=[pl.BlockSpec((tm, d), lambda n, m: (m, 0)), pl.BlockSpec((tn, d), lambda n, m: (n, 0)),
                  pl.BlockSpec((tm, tn), lambda n, m: (m, off_a + n)), pl.BlockSpec((tm, tn), lambda n, m: (m, off_b + n)),
                  tile, tile],
        out_specs=[tile, tile, tile, tile, bias, bias],
        out_shape=[jax.ShapeDtypeStruct((t, d), BF16)] * 4 + [jax.ShapeDtypeStruct((1, d), F32)] * 2,
        compiler_params=_params(("parallel", "arbitrary")),
    )(dx, wo, pc, pc, ua, ub)


def mm_plain(name, a, b, contract, out_dtype, tk_target=512):
    if contract == NN:
        m, kdim = a.shape
        n = b.shape[1]
    elif contract == NT:
        m, kdim = a.shape
        n = b.shape[0]
    else:
        kdim, m = a.shape
        n = b.shape[1]
    tm = _row_tile(m, 512)
    tk = _row_tile(kdim, tk_target)
    nk = kdim // tk
    if contract == TN:
        a_spec = pl.BlockSpec((tk, tm), lambda i, k: (k, i))
    else:
        a_spec = pl.BlockSpec((tm, tk), lambda i, k: (i, k))
    if contract == NT:
        b_spec = pl.BlockSpec((n, tk), lambda i, k: (0, k))
    else:
        b_spec = pl.BlockSpec((tk, n), lambda i, k: (k, 0))

    def epilogue(acc, ex, outs, ids):
        outs[0][...] = acc.astype(out_dtype)

    return _mm(name, [(a, b)], contract, (m // tm, nk), [(a_spec, b_spec)],
               [jax.ShapeDtypeStruct((m, n), out_dtype)], [pl.BlockSpec((tm, n), lambda i, k: (i, 0))], (tm, n), nk, epilogue)[0]


def wgrad_cat(name, h, dcat):
    t, d = h.shape
    n = dcat.shape[1]
    tn = 768
    tk = _row_tile(t, 512)

    def epilogue(acc, ex, outs, ids):
        outs[0][...] = acc.astype(BF16)

    return _mm(
        name, [(h, dcat)], TN, (n // tn, t // tk),
        [(pl.BlockSpec((tk, d), lambda j, k: (k, 0)), pl.BlockSpec((tk, tn), lambda j, k: (k, j)))],
        [jax.ShapeDtypeStruct((d, n), BF16)], [pl.BlockSpec((d, tn), lambda j, k: (0, j))], (d, tn), t // tk, epilogue,
    )[0]


def fox_prep(name, f, bias):
    t, lanes = f.shape
    nchunk = t // QB

    def body(f_ref, b_ref, c_ref):
        lower = (_iota2((QB, QB), 1) <= _iota2((QB, QB), 0)).astype(BF16)

        def chunk(n, carry):
            rows = pl.ds(pl.multiple_of(n * QB, QB), QB)
            u = f_ref[rows, :] + b_ref[...]
            lf = jnp.minimum(u, 0.0) - _log1pexp_neg_abs(u)
            c = _dot_exact_lhs01(lower, lf) + carry
            c_ref[rows, :] = c
            return c[QB - 1:QB, :]

        lax.fori_loop(0, nchunk, chunk, jnp.zeros((1, lanes), F32))

    return pl.pallas_call(body, name=name, out_shape=jax.ShapeDtypeStruct((t, lanes), F32),
                          compiler_params=pltpu.CompilerParams(vmem_limit_bytes=VMEM_LIMIT_BYTES))(f, bias)


def fox_gate_bwd(name, dc, f, bias):
    t, lanes = dc.shape
    nchunk = t // QB

    def body(dc_ref, f_ref, b_ref, df_ref, gb_ref):
        upper = (_iota2((QB, QB), 1) >= _iota2((QB, QB), 0)).astype(BF16)

        def chunk(n, carry):
            tail, total = carry
            rows = pl.ds(pl.multiple_of((nchunk - 1 - n) * QB, QB), QB)
            dlf = _dot_exact_lhs01(upper, dc_ref[rows, :]) + tail
            u = f_ref[rows, :] + b_ref[...]
            df = dlf * jnp.exp(jnp.minimum(-u, 0.0) - _log1pexp_neg_abs(u))
            df_ref[rows, :] = df
            return dlf[0:1, :], total + jnp.sum(df, axis=0, keepdims=True)

        zero = jnp.zeros((1, lanes), F32)
        _, total = lax.fori_loop(0, nchunk, chunk, (zero, zero))
        gb_ref[...] = total

    return pl.pallas_call(body, name=name,
                          out_shape=[jax.ShapeDtypeStruct((t, lanes), F32), jax.ShapeDtypeStruct((1, lanes), F32)],
                          compiler_params=pltpu.CompilerParams(vmem_limit_bytes=VMEM_LIMIT_BYTES))(dc, f, bias)


def _qrows(i):
    return pl.ds(pl.multiple_of(i * BQ, BQ), BQ)


def _krows(kc):
    return pl.ds(pl.multiple_of(kc * CS, CS), CS)


def _head_spec(t, offset):
    return pl.BlockSpec((None, t, HEAD_DIM), lambda h: (offset + h, 0, 0))


def _head_t_spec(nq, offset):
    return pl.BlockSpec((None, nq, HEAD_DIM, BQ), lambda h: (offset + h, 0, 0, 0))


def _chunk_t_spec(nc):
    return pl.BlockSpec((None, nc, HEAD_DIM, CS), lambda h: (h, 0, 0, 0))


def _dot_split2_rhs01(x, m01):
    hi = x.astype(BF16)
    lo = (x - hi.astype(F32)).astype(BF16)
    d = lambda p: lax.dot_general(p, m01, (NN, ((), ())), preferred_element_type=F32)
    return d(hi) + d(lo)


def _diag_mask(dchunk, inclusive):
    r_io = _iota2((BQ, CS), 0)
    c_io = _iota2((BQ, CS), 1) + dchunk * CS
    return c_io <= r_io if inclusive else c_io < r_io


def _walk_chunks(i, step, init, right_to_left):
    order = list(reversed(range(N_SUB))) if right_to_left else list(range(N_SUB))

    def diagonal(state):
        for dchunk in order:
            state = step(i * N_SUB + dchunk, state, dchunk)
        return state

    def group(n, state):
        base = ((i - 1 - n) if right_to_left else n) * N_SUB
        for dchunk in order:
            state = step(base + dchunk, state, None)
        return state

    if right_to_left:
        return lax.fori_loop(0, i, group, diagonal(init))
    return diagonal(lax.fori_loop(0, i, group, init))


def sb_fwd(name, qkv):
    t = qkv.shape[1]

    def body(q_ref, k_ref, v_ref, o_ref):
        later = (_iota2((CS, CS), 0) > _iota2((CS, CS), 1)).astype(BF16)

        def qblock(i, _):
            q = q_ref[_qrows(i), :]

            def step(kc, state, dchunk):
                carry, acc = state
                z = _dot(q, k_ref[_krows(kc), :], NT)
                sp = _log1pexp_neg_abs(z)
                lnb = -jnp.maximum(z, 0.0) - sp
                if dchunk is not None:
                    lnb = jnp.where(_diag_mask(dchunk, False), lnb, 0.0)
                w = jnp.exp(jnp.minimum(z, 0.0) - sp + _dot_split2_rhs01(lnb, later) + carry)
                if dchunk is not None:
                    w = jnp.where(_diag_mask(dchunk, False), w, 0.0)
                acc = acc + _dot(w, v_ref[_krows(kc), :], NN)
                return carry + jnp.sum(lnb, axis=1, keepdims=True), acc

            init = (jnp.zeros((BQ, 1), F32), jnp.zeros((BQ, HEAD_DIM), F32))
            _, acc = _walk_chunks(i, step, init, True)
            o_ref[_qrows(i), :] = acc.astype(BF16)
            return 0

        lax.fori_loop(0, t // BQ, qblock, 0)

    return pl.pallas_call(
        body, name=name, grid=(N_HEADS,),
        in_specs=[_head_spec(t, 0), _head_spec(t, N_HEADS), _head_spec(t, 2 * N_HEADS)],
        out_specs=_head_spec(t, 0), out_shape=jax.ShapeDtypeStruct((N_HEADS, t, HEAD_DIM), BF16),
        compiler_params=_params(("parallel",)),
    )(qkv, qkv, qkv)


def sb_bwd(name, qkv, qt, dy, dyt):
    t = qkv.shape[1]
    nq, nc = t // BQ, t // CS

    def body(q_ref, k_ref, v_ref, qt_ref, do_ref, dot_ref, dq_ref, dkt_ref, dvt_ref, g_s, b_s, dkt_acc, dvt_acc):
        later = (_iota2((CS, CS), 0) > _iota2((CS, CS), 1)).astype(BF16)
        earlier = (_iota2((CS, CS), 0) < _iota2((CS, CS), 1)).astype(BF16)
        dkt_acc[...] = jnp.zeros_like(dkt_acc)
        dvt_acc[...] = jnp.zeros_like(dvt_acc)

        def qblock(i, _):
            q = q_ref[_qrows(i), :]
            do = do_ref[_qrows(i), :]
            q_t = qt_ref[i]
            do_t = dot_ref[i]

            def step1(kc, carry, dchunk):
                z = _dot(q, k_ref[_krows(kc), :], NT)
                sp = _log1pexp_neg_abs(z)
                lnb = -jnp.maximum(z, 0.0) - sp
                lsz = jnp.minimum(z, 0.0) - sp
                if dchunk is not None:
                    lnb = jnp.where(_diag_mask(dchunk, False), lnb, 0.0)
                w = jnp.exp(lsz + _dot_split2_rhs01(lnb, later) + carry)
                if dchunk is not None:
                    w = jnp.where(_diag_mask(dchunk, False), w, 0.0)
                g_s[kc] = w * _dot(do, v_ref[_krows(kc), :], NT)
                b_s[kc] = jnp.exp(lsz)
                dvt_acc[kc] += _dot(do_t, w, NN)
                return carry + jnp.sum(lnb, axis=1, keepdims=True)

            _walk_chunks(i, step1, jnp.zeros((BQ, 1), F32), True)

            def step2(kc, state, dchunk):
                before, dq = state
                g = g_s[kc]
                beta = b_s[kc]
                dz = g * (1.0 - beta) - beta * (_dot_split2_rhs01(g, earlier) + before)
                if dchunk is not None:
                    dz = jnp.where(_diag_mask(dchunk, False), dz, 0.0)
                dzb = dz.astype(BF16)
                dq = dq + _dot(dzb, k_ref[_krows(kc), :], NN)
                dkt_acc[kc] += _dot(q_t, dzb, NN)
                return before + jnp.sum(g, axis=1, keepdims=True), dq

            _, dq = _walk_chunks(i, step2, (jnp.zeros((BQ, 1), F32), jnp.zeros((BQ, HEAD_DIM), F32)), False)
            dq_ref[_qrows(i), :] = (dq * ATTN_SCALE).astype(BF16)
            return 0

        lax.fori_loop(0, nq, qblock, 0)
        dkt_ref[...] = dkt_acc[...].astype(BF16)
        dvt_ref[...] = dvt_acc[...].astype(BF16)

    chunked = jax.ShapeDtypeStruct((N_HEADS, nc, HEAD_DIM, CS), BF16)
    return pl.pallas_call(
        body, name=name, grid=(N_HEADS,),
        in_specs=[_head_spec(t, 0), _head_spec(t, N_HEADS), _head_spec(t, 2 * N_HEADS), _head_t_spec(nq, 0),
                  _head_spec(t, 0), _head_t_spec(nq, 0)],
        out_specs=[_head_spec(t, 0), _chunk_t_spec(nc), _chunk_t_spec(nc)],
        out_shape=[jax.ShapeDtypeStruct((N_HEADS, t, HEAD_DIM), BF16), chunked, chunked],
        scratch_shapes=[pltpu.VMEM((nc, BQ, CS), F32), pltpu.VMEM((nc, BQ, CS), F32),
                        pltpu.VMEM((nc, HEAD_DIM, CS), F32), pltpu.VMEM((nc, HEAD_DIM, CS), F32)],
        compiler_params=_params(("parallel",)),
    )(qkv, qkv, qkv, qt, dy, dyt)


def _col_spec(t):
    return pl.BlockSpec((None, t, 1), lambda h: (h, 0, 0))


def _row_spec(nc):
    return pl.BlockSpec((None, nc, 1, CS), lambda h: (h, 0, 0, 0))


def fox_fwd(name, qkv, c_col, c_row):
    t = qkv.shape[1]

    def body(q_ref, k_ref, v_ref, cc_ref, cr_ref, o_ref, lse_ref):
        def qblock(i, _):
            q = q_ref[_qrows(i), :]
            ct = cc_ref[_qrows(i), :]

            def step(kc, state, dchunk):
                m, l, acc = state
                s = _dot(q, k_ref[_krows(kc), :], NT) + ct - cr_ref[kc]
                if dchunk is not None:
                    s = jnp.where(_diag_mask(dchunk, True), s, NEG_BIG)
                m_new = jnp.maximum(m, jnp.max(s, axis=1, keepdims=True))
                alpha = jnp.exp(m - m_new)
                p = jnp.exp(s - m_new)
                if dchunk is not None:
                    p = jnp.where(_diag_mask(dchunk, True), p, 0.0)
                l = l * alpha + jnp.sum(p, axis=1, keepdims=True)
                acc = acc * alpha + _dot(p, v_ref[_krows(kc), :], NN)
                return m_new, l, acc

            init = (jnp.full((BQ, 1), NEG_BIG, F32), jnp.zeros((BQ, 1), F32), jnp.zeros((BQ, HEAD_DIM), F32))
            m, l, acc = _walk_chunks(i, step, init, False)
            o_ref[_qrows(i), :] = (acc / l).astype(BF16)
            lse_ref[_qrows(i), :] = m + jnp.log(l)
            return 0

        lax.fori_loop(0, t // BQ, qblock, 0)

    return pl.pallas_call(
        body, name=name, grid=(N_HEADS,),
        in_specs=[_head_spec(t, 3 * N_HEADS), _head_spec(t, 4 * N_HEADS), _head_spec(t, 5 * N_HEADS),
                  _col_spec(t), _row_spec(t // CS)],
        out_specs=[_head_spec(t, 0), _col_spec(t)],
        out_shape=[jax.ShapeDtypeStruct((N_HEADS, t, HEAD_DIM), BF16), jax.ShapeDtypeStruct((N_HEADS, t, 1), F32)],
        compiler_params=_params(("parallel",)),
    )(qkv, qkv, qkv, c_col, c_row)


def fox_bwd(name, qkv, qt, y, dy, dyt, lse, c_col, c_row):
    t = qkv.shape[1]
    nq, nc = t // BQ, t // CS

    def body(q_ref, k_ref, v_ref, qt_ref, o_ref, do_ref, dot_ref, lse_ref, cc_ref, cr_ref,
             dq_ref, dkt_ref, dvt_ref, dcc_ref, dcr_ref, dkt_acc, dvt_acc, dcr_acc):
        dkt_acc[...] = jnp.zeros_like(dkt_acc)
        dvt_acc[...] = jnp.zeros_like(dvt_acc)
        dcr_acc[...] = jnp.zeros_like(dcr_acc)

        def qblock(i, _):
            q = q_ref[_qrows(i), :]
            do = do_ref[_qrows(i), :]
            q_t = qt_ref[i]
            do_t = dot_ref[i]
            ct = cc_ref[_qrows(i), :]
            lse_i = lse_ref[_qrows(i), :]
            delta = jnp.sum(do.astype(F32) * o_ref[_qrows(i), :].astype(F32), axis=1, keepdims=True)

            def step(kc, state, dchunk):
                dq, dct = state
                s = _dot(q, k_ref[_krows(kc), :], NT) + ct - cr_ref[kc]
                p = jnp.exp(s - lse_i)
                if dchunk is not None:
                    p = jnp.where(_diag_mask(dchunk, True), p, 0.0)
                ds = p * (_dot(do, v_ref[_krows(kc), :], NT) - delta)
                dvt_acc[kc] += _dot(do_t, p, NN)
                dsb = ds.astype(BF16)
                dq = dq + _dot(dsb, k_ref[_krows(kc), :], NN)
                dkt_acc[kc] += _dot(q_t, dsb, NN)
                dcr_acc[kc] -= jnp.sum(ds, axis=0, keepdims=True)
                return dq, dct + jnp.sum(ds, axis=1, keepdims=True)

            dq, dct = _walk_chunks(i, step, (jnp.zeros((BQ, HEAD_DIM), F32), jnp.zeros((BQ, 1), F32)), False)
            dq_ref[_qrows(i), :] = (dq * ATTN_SCALE).astype(BF16)
            dcc_ref[_qrows(i), :] = dct
            return 0

        lax.fori_loop(0, nq, qblock, 0)
        dkt_ref[...] = dkt_acc[...].astype(BF16)
        dvt_ref[...] = dvt_acc[...].astype(BF16)
        dcr_ref[...] = dcr_acc[...]

    chunked = jax.ShapeDtypeStruct((N_HEADS, nc, HEAD_DIM, CS), BF16)
    return pl.pallas_call(
        body, name=name, grid=(N_HEADS,),
        in_specs=[_head_spec(t, 3 * N_HEADS), _head_spec(t, 4 * N_HEADS), _head_spec(t, 5 * N_HEADS), _head_t_spec(nq, N_HEADS),
                  _head_spec(t, 0), _head_spec(t, 0), _head_t_spec(nq, 0), _col_spec(t), _col_spec(t), _row_spec(nc)],
        out_specs=[_head_spec(t, 0), _chunk_t_spec(nc), _chunk_t_spec(nc), _col_spec(t), _row_spec(nc)],
        out_shape=[jax.ShapeDtypeStruct((N_HEADS, t, HEAD_DIM), BF16), chunked, chunked,
                   jax.ShapeDtypeStruct((N_HEADS, t, 1), F32), jax.ShapeDtypeStruct((N_HEADS, nc, 1, CS), F32)],
        scratch_shapes=[pltpu.VMEM((nc, HEAD_DIM, CS), F32), pltpu.VMEM((nc, HEAD_DIM, CS), F32), pltpu.VMEM((nc, 1, CS), F32)],
        compiler_params=_params(("parallel",)),
    )(qkv, qkv, qkv, qt, y, dy, dyt, lse, c_col, c_row)


def loss_head(name, x, g, target):
    t, d = x.shape
    tr = _row_tile(t, 256)

    def body(x_ref, g_ref, t_ref, dx_ref, gn_ref, loss_ref):
        xv = x_ref[...]
        r = lax.rsqrt(jnp.mean(xv * xv, axis=-1, keepdims=True) + RMS_EPS)
        xhat = xv * r
        gv = g_ref[...]
        err = xhat * gv - t_ref[...]
        part_loss = 0.5 * jnp.sum(jnp.mean(err * err, axis=-1, keepdims=True), axis=0, keepdims=True)
        dy = err * (1.0 / d)
        dyg = dy * gv
        dx_ref[...] = r * (dyg - xhat * jnp.mean(dyg * xhat, axis=-1, keepdims=True))
        part_g = jnp.sum(dy * xhat, axis=0, keepdims=True)

        @pl.when(pl.program_id(0) == 0)
        def _():
            gn_ref[...] = part_g
            loss_ref[...] = part_loss

        @pl.when(pl.program_id(0) != 0)
        def _():
            gn_ref[...] += part_g
            loss_ref[...] += part_loss

    row = pl.BlockSpec((tr, d), lambda i: (i, 0))
    return pl.pallas_call(
        body, name=name, grid=(t // tr,),
        in_specs=[row, pl.BlockSpec((1, d), lambda i: (0, 0)), row],
        out_specs=[row, pl.BlockSpec((1, d), lambda i: (0, 0)), pl.BlockSpec((1, 1), lambda i: (0, 0))],
        out_shape=[jax.ShapeDtypeStruct((t, d), F32), jax.ShapeDtypeStruct((1, d), F32), jax.ShapeDtypeStruct((1, 1), F32)],
        compiler_params=_params(("arbitrary",)),
    )(x, g, target)


def _place():
    return lax.axis_index("x"), lax.axis_index("y"), lax.axis_index("c")


def _other_chips(x, y):
    return [(1 - x, y), (x, 1 - y), (1 - x, 1 - y)]


def _half(ref, c, rows):
    return ref.at[:, pl.ds(c * (rows // 2), rows // 2), :]


_ANY = pl.BlockSpec(memory_space=pl.ANY)


def gather_weights(name, bufs):
    n = len(bufs)

    def body(*refs):
        outs = refs[n:2 * n]
        send_sems, recv_sems = refs[2 * n:]
        x, y, c = _place()
        chips = _other_chips(x, y)
        me = 2 * x + y
        sibling = (x, y, 1 - c)
        first, passed = [], []
        for i in range(n):
            rows = outs[i].shape[2]
            mine = _half(outs[i].at[me], c, rows)
            for j, (qx, qy) in enumerate(chips):
                k = 6 * i + j
                rc = pltpu.make_async_remote_copy(
                    src_ref=mine, dst_ref=mine,
                    send_sem=send_sems.at[k], recv_sem=recv_sems.at[k], device_id=(qx, qy, c), device_id_type=MESH)
                rc.start()
                first.append(rc)
        for i in range(n):
            rows = outs[i].shape[2]
            for j, (qx, qy) in enumerate(chips):
                k = 6 * i + j
                block = _half(outs[i].at[2 * qx + qy], c, rows)
                pltpu.make_async_remote_copy(
                    src_ref=block, dst_ref=block, send_sem=send_sems.at[k], recv_sem=recv_sems.at[k],
                    device_id=(qx, qy, c), device_id_type=MESH).wait_recv()
                fw = pltpu.make_async_remote_copy(
                    src_ref=block, dst_ref=block, send_sem=send_sems.at[k + 3], recv_sem=recv_sems.at[k + 3],
                    device_id=sibling, device_id_type=MESH)
                fw.start()
                passed.append(fw)
        for i in range(n):
            rows = outs[i].shape[2]
            for j, (qx, qy) in enumerate(chips):
                k = 6 * i + j + 3
                block = _half(outs[i].at[2 * qx + qy], 1 - c, rows)
                pltpu.make_async_remote_copy(
                    src_ref=block, dst_ref=block, send_sem=send_sems.at[k], recv_sem=recv_sems.at[k],
                    device_id=sibling, device_id_type=MESH).wait_recv()
        for cp in first + passed:
            cp.wait_send()

    return pl.pallas_call(
        body, name=name, in_specs=[_ANY] * n, out_specs=[_ANY] * n,
        out_shape=[jax.ShapeDtypeStruct(b.shape, b.dtype) for b in bufs],
        input_output_aliases={i: i for i in range(n)},
        scratch_shapes=[pltpu.SemaphoreType.DMA((6 * n,)), pltpu.SemaphoreType.DMA((6 * n,))],
        compiler_params=pltpu.CompilerParams(has_side_effects=True),
    )(*bufs)


def send_half_to_sibling(name, grads):
    n = len(grads)

    def body(*refs):
        srcs, outs = refs[:n], refs[n:2 * n]
        send_sems, recv_sems = refs[2 * n:]
        x, y, c = _place()
        sibling = (x, y, 1 - c)
        copies = []
        for i in range(n):
            rows = srcs[i].shape[2]
            rc = pltpu.make_async_remote_copy(
                src_ref=srcs[i].at[:, :, pl.ds((1 - c) * (rows // 2), rows // 2), :], dst_ref=outs[i],
                send_sem=send_sems.at[i], recv_sem=recv_sems.at[i], device_id=sibling, device_id_type=MESH)
            rc.start()
            copies.append(rc)
        for rc in copies:
            rc.wait()

    def half_shape(g):
        s = g.shape
        return jax.ShapeDtypeStruct((s[0], s[1], s[2] // 2, s[3]), g.dtype)

    return pl.pallas_call(
        body, name=name, in_specs=[_ANY] * n, out_specs=[_ANY] * n, out_shape=[half_shape(g) for g in grads],
        scratch_shapes=[pltpu.SemaphoreType.DMA((n,)), pltpu.SemaphoreType.DMA((n,))],
        compiler_params=pltpu.CompilerParams(has_side_effects=True),
    )(*grads)


def exchange_between_chips(name, parts):
    n = len(parts)

    def body(*refs):
        srcs, outs = refs[:n], refs[n:2 * n]
        send_sems, recv_sems = refs[2 * n:]
        x, y, c = _place()
        chips = _other_chips(x, y)
        copies = []
        for i in range(n):
            for j, (qx, qy) in enumerate(chips):
                k = 3 * i + j
                rc = pltpu.make_async_remote_copy(
                    src_ref=srcs[i].at[2 * qx + qy], dst_ref=outs[i].at[j],
                    send_sem=send_sems.at[k], recv_sem=recv_sems.at[k], device_id=(qx, qy, c), device_id_type=MESH)
                rc.start()
                copies.append(rc)
        for rc in copies:
            rc.wait()

    return pl.pallas_call(
        body, name=name, in_specs=[_ANY] * n, out_specs=[_ANY] * n,
        out_shape=[jax.ShapeDtypeStruct((N_CHIPS - 1,) + p.shape[1:], p.dtype) for p in parts],
        scratch_shapes=[pltpu.SemaphoreType.DMA((3 * n,)), pltpu.SemaphoreType.DMA((3 * n,))],
        compiler_params=pltpu.CompilerParams(has_side_effects=True),
    )(*parts)


def share_halves(name, bufs):
    n = len(bufs)

    def body(*refs):
        outs = refs[n:2 * n]
        send_sems, recv_sems = refs[2 * n:]
        x, y, c = _place()
        copies = []
        for i in range(n):
            mine = _half(outs[i], c, outs[i].shape[1])
            rc = pltpu.make_async_remote_copy(
                src_ref=mine, dst_ref=mine, send_sem=send_sems.at[i], recv_sem=recv_sems.at[i],
                device_id=(x, y, 1 - c), device_id_type=MESH)
            rc.start()
            copies.append(rc)
        for i in range(n):
            theirs = _half(outs[i], 1 - c, outs[i].shape[1])
            pltpu.make_async_remote_copy(
                src_ref=theirs, dst_ref=theirs, send_sem=send_sems.at[i], recv_sem=recv_sems.at[i],
                device_id=(x, y, 1 - c), device_id_type=MESH).wait_recv()
        for rc in copies:
            rc.wait_send()

    return pl.pallas_call(
        body, name=name, in_specs=[_ANY] * n, out_specs=[_ANY] * n,
        out_shape=[jax.ShapeDtypeStruct(b.shape, b.dtype) for b in bufs],
        input_output_aliases={i: i for i in range(n)},
        scratch_shapes=[pltpu.SemaphoreType.DMA((n,)), pltpu.SemaphoreType.DMA((n,))],
        compiler_params=pltpu.CompilerParams(has_side_effects=True),
    )(*bufs)


def pair_sum(name, grad, recv, c):
    ns, na, rh, cols = recv.shape
    tr = _row_tile(rh, 256) if rh % 256 == 0 else rh
    nt = rh // tr

    def body(c_ref, g_ref, r_ref, o_ref):
        o_ref[...] = (g_ref[...].astype(F32) + r_ref[...].astype(F32)).astype(BF16)

    blk = (None, None, tr, cols)
    return pl.pallas_call(
        body, name=name,
        grid_spec=pltpu.PrefetchScalarGridSpec(
            num_scalar_prefetch=1, grid=(ns, na, nt),
            in_specs=[pl.BlockSpec(blk, lambda s, a, r, c_ref: (s, a, c_ref[0] * nt + r, 0)),
                      pl.BlockSpec(blk, lambda s, a, r, c_ref: (s, a, r, 0))],
            out_specs=pl.BlockSpec(blk, lambda s, a, r, c_ref: (s, a, r, 0))),
        out_shape=jax.ShapeDtypeStruct(recv.shape, BF16),
        compiler_params=_params(("parallel", "parallel", "parallel")),
    )(c, grad, recv)


def chip_sum(name, parts, landed, place):
    _, na, rh, cols = parts.shape
    tr = _row_tile(rh, 256) if rh % 256 == 0 else rh
    nt = rh // tr

    def body(place_ref, p_ref, l_ref, o_ref):
        total = p_ref[...].astype(F32)
        for s in range(N_CHIPS - 1):
            total = total + l_ref[s].astype(F32)
        o_ref[...] = total

    return pl.pallas_call(
        body, name=name,
        grid_spec=pltpu.PrefetchScalarGridSpec(
            num_scalar_prefetch=1, grid=(na, nt),
            in_specs=[pl.BlockSpec((None, None, tr, cols), lambda a, r, pr: (pr[1], a, r, 0)),
                      pl.BlockSpec((N_CHIPS - 1, None, tr, cols), lambda a, r, pr: (0, a, r, 0))],
            out_specs=pl.BlockSpec((None, tr, cols), lambda a, r, pr: (a, pr[0] * nt + r, 0))),
        out_shape=jax.ShapeDtypeStruct((na, 2 * rh, cols), F32),
        compiler_params=_params(("parallel", "parallel")),
    )(place, parts, landed)


def reduce_scatter(tag, grads, place):
    recv = send_half_to_sibling(f"{tag}_to_sibling", grads)
    parts = [pair_sum(f"{tag}_pair_sum{i}", g, r, place) for i, (g, r) in enumerate(zip(grads, recv))]
    landed = exchange_between_chips(f"{tag}_between_chips", parts)
    halves = [chip_sum(f"{tag}_chip_sum{i}", p, l, place) for i, (p, l) in enumerate(zip(parts, landed))]
    return share_halves(f"{tag}_share", halves)


def _adamw_math(w, g, m, v):
    m = ADAM_B1 * m + (1.0 - ADAM_B1) * g
    v = ADAM_B2 * v + (1.0 - ADAM_B2) * (g * g)
    m_hat = m / (1.0 - ADAM_B1 ** ADAM_STEP)
    v_hat = v / (1.0 - ADAM_B2 ** ADAM_STEP)
    delta = -ADAM_LR * (m_hat / (jnp.sqrt(v_hat) + ADAM_EPS) + ADAM_WD * w)
    return delta, m, v


def adamw(name, w, g, m, v):
    rows, cols = w.shape
    tr = _row_tile(rows, 256) if rows % 256 == 0 else rows // 2

    def body(w_ref, g_ref, m_ref, v_ref, d_ref, mo_ref, vo_ref):
        d_ref[...], mo_ref[...], vo_ref[...] = _adamw_math(w_ref[...], g_ref[...], m_ref[...], v_ref[...])

    blk = pl.BlockSpec((tr, cols), lambda i: (i, 0))
    return pl.pallas_call(
        body, name=name, grid=(rows // tr,), in_specs=[blk] * 4, out_specs=[blk] * 3,
        out_shape=[jax.ShapeDtypeStruct(w.shape, F32)] * 3, compiler_params=_params(("parallel",)),
    )(w, g, m, v)


def small_allreduce_adamw(name, g_part, w, m, v):
    rows, cols = g_part.shape

    def body(g_ref, w_ref, m_ref, v_ref, sum_ref, d_ref, mo_ref, vo_ref, land, send_sems, recv_sems):
        x, y, c = _place()
        me = 4 * x + 2 * y + c
        land[me] = g_ref[...]
        copies = []
        for r in range(1, 8):
            peer = (x ^ (r >> 2), y ^ ((r >> 1) & 1), c ^ (r & 1))
            rc = pltpu.make_async_remote_copy(
                src_ref=g_ref, dst_ref=land.at[me], send_sem=send_sems.at[r - 1], recv_sem=recv_sems.at[r - 1],
                device_id=peer, device_id_type=MESH)
            rc.start()
            copies.append(rc)
        for rc in copies:
            rc.wait()
        total = land[0]
        for s in range(1, 8):
            total = total + land[s]
        sum_ref[...] = total
        d_ref[...], mo_ref[...], vo_ref[...] = _adamw_math(w_ref[...], total, m_ref[...], v_ref[...])

    vmem = pl.BlockSpec(memory_space=pltpu.VMEM)
    return pl.pallas_call(
        body, name=name, in_specs=[vmem] * 4, out_specs=[vmem] * 4,
        out_shape=[jax.ShapeDtypeStruct((rows, cols), F32)] * 4,
        scratch_shapes=[pltpu.VMEM((8, rows, cols), F32), pltpu.SemaphoreType.DMA((7,)), pltpu.SemaphoreType.DMA((7,))],
        compiler_params=pltpu.CompilerParams(has_side_effects=True),
    )(g_part, w, m, v)


def _heads(a):
    t, w = a.shape
    return a.reshape(t, w // HEAD_DIM, HEAD_DIM).transpose(1, 0, 2)


def _unheads(a):
    n, t, _ = a.shape
    return a.transpose(1, 0, 2).reshape(t, n * HEAD_DIM)


def _cols_from_shards(g):
    ns, r, cols = g.shape
    return g.transpose(1, 0, 2).reshape(r, ns * cols)


def _shards_from_cols(a):
    r, cols = a.shape
    return a.reshape(r, N_CHIPS, cols // N_CHIPS).transpose(1, 0, 2)


def kernel(x, norm_ffn1, w_ffn1_gate, w_ffn1_up, w_ffn1_down, norm_mix, w_in, b_forget, w_gate, b_gate, w_up_a, w_up_b, w_out, norm_ffn2, w_ffn2_gate, w_ffn2_up, w_ffn2_down, norm_final, loss_target, m_norm_ffn1, m_w_ffn1_gate, m_w_ffn1_up, m_w_ffn1_down, m_norm_mix, m_w_in, m_b_forget, m_w_gate, m_b_gate, m_w_up_a, m_w_up_b, m_w_out, m_norm_ffn2, m_w_ffn2_gate, m_w_ffn2_up, m_w_ffn2_down, m_norm_final, v_norm_ffn1, v_w_ffn1_gate, v_w_ffn1_up, v_w_ffn1_down, v_norm_mix, v_w_in, v_b_forget, v_w_gate, v_b_gate, v_w_up_a, v_w_up_b, v_w_out, v_norm_ffn2, v_w_ffn2_gate, v_w_ffn2_up, v_w_ffn2_down, v_norm_final):
    t, d = x.shape[1], x.shape[2]
    in4 = w_in.shape[2]
    gate4 = w_gate.shape[2]
    up4 = w_up_a.shape[2]
    in_cols = N_CHIPS * in4
    n_forget = in_cols - QKV_COLS
    assert w_up_a.shape[1] == WIDTH and d == 2 * WIDTH and n_forget == N_HEADS
    nq = t // BQ
    chip = 2 * lax.axis_index("x") + lax.axis_index("y")
    c_arr = jnp.stack([lax.axis_index("c"), chip]).astype(jnp.int32)
    x2d = x[0]
    target = loss_target[0]

    def slot(shard):
        return lax.dynamic_update_slice(jnp.zeros((N_CHIPS,) + shard.shape, BF16), shard.astype(BF16)[None], (chip, 0, 0, 0))

    def pack_ffn(wg, wu, wd):
        return slot(jnp.stack([wg[0], wu[0]])), slot(wd)

    gu1_sh, dn1_sh = pack_ffn(w_ffn1_gate, w_ffn1_up, w_ffn1_down)
    gu2_sh, dn2_sh = pack_ffn(w_ffn2_gate, w_ffn2_up, w_ffn2_down)
    mx_sh = slot(jnp.concatenate([w_in[0], w_gate[0], jnp.concatenate([w_up_a[0], w_up_b[0]], axis=0)], axis=1)[None])
    wo_sh = slot(w_out)
    gu1, dn1 = gather_weights("gather_ffn1", [gu1_sh, dn1_sh])
    mx, wo = gather_weights("gather_mixer", [mx_sh, wo_sh])
    gu2, dn2 = gather_weights("gather_ffn2", [gu2_sh, dn2_sh])
    dn1 = dn1[:, 0]
    dn2 = dn2[:, 0]
    mx = mx[:, 0]
    w_in_full = _cols_from_shards(mx[:, :, :in4])
    w_gate_full = _cols_from_shards(mx[:, :, in4:in4 + gate4])
    w_up_full = _cols_from_shards(mx[:, :, in4 + gate4:])
    w_up_a_full, w_up_b_full = w_up_full[:WIDTH], w_up_full[WIDTH:]
    wcat = jnp.concatenate([w_in_full[:, :QKV_COLS], w_gate_full, w_in_full[:, QKV_COLS:],
                            jnp.zeros((d, F_PAD - n_forget), BF16)], axis=1)
    bias_cat = jnp.concatenate([jnp.zeros((1, QKV_COLS), F32), b_gate, jnp.zeros((1, F_PAD), F32)], axis=1)
    f_off = QKV_COLS + 2 * d
    wo_full = wo.reshape(d, d)
    b_forget_row = jnp.pad(b_forget, ((0, 0), (0, QB - n_forget)))

    x1, saved1 = ffn_forward("ffn1", x2d, norm_ffn1, gu1, dn1)
    h2, rstd2 = rms_fwd("mix_rms", x1, norm_mix)
    pc = proj_cat("mix_proj", h2, wcat, bias_cat)
    q_scale = jnp.array([ATTN_SCALE, 1.0, 1.0, ATTN_SCALE, 1.0, 1.0], F32)[None, :, None]
    qkv = _heads((pc[:, :QKV_COLS].reshape(t, 6, WIDTH) * q_scale).astype(BF16).reshape(t, QKV_COLS))
    qt = jnp.concatenate([qkv[:N_HEADS], qkv[3 * N_HEADS:4 * N_HEADS]], axis=0)
    qt = qt.reshape(2 * N_HEADS, nq, BQ, HEAD_DIM).transpose(0, 1, 3, 2)
    f_logit = pc[:, f_off:f_off + QB]
    c_cum = fox_prep("fox_prep", f_logit, b_forget_row)
    c_heads = c_cum[:, :N_HEADS].T
    c_col = c_heads[:, :, None]
    c_row = c_heads.reshape(N_HEADS, t // CS, 1, CS)
    ya_h = sb_fwd("sb_fwd", qkv)
    yb_h, lse = fox_fwd("fox_fwd", qkv, c_col, c_row)
    ya = _unheads(ya_h)
    yb = _unheads(yb_h)
    ua, ub, mixed = mix_fwd("mix_fwd", ya, yb, w_up_a_full, w_up_b_full, pc)
    x2 = mm_residual("mix_out", mixed[None], wo_full[None], x1, 1.0)
    x3, saved2 = ffn_forward("ffn2", x2, norm_ffn2, gu2, dn2)
    dx3, gn_final, loss_part = loss_head("loss_head", x3, norm_final[None], target)

    dx2, gn_ffn2, g_gu2, g_dn2 = ffn_backward("ffn2", dx3, saved2, norm_ffn2, gu2, dn2)
    s_gu2, s_dn2 = reduce_scatter("rs_ffn2", [g_gu2, g_dn2], c_arr)

    dua, dub, dpa, dpb, gba, gbb = mix_bwd("mix_bwd", dx2, wo_full, pc, ua, ub)
    dgp = jnp.concatenate([dpa, dpb], axis=1)
    g_bgate = jnp.concatenate([gba, gbb], axis=1)
    g_wo = mm_plain("wgrad_out", mixed, dx2, TN, BF16)
    dya = mm_plain("dya", dua, w_up_a_full, NT, BF16, tk_target=1024)
    dyb = mm_plain("dyb", dub, w_up_b_full, NT, BF16, tk_target=1024)
    g_up_a = mm_plain("wgrad_up_a", ya, dua, TN, BF16)
    g_up_b = mm_plain("wgrad_up_b", yb, dub, TN, BF16)
    def blocks_t(a):
        return a.reshape(N_HEADS, nq, BQ, HEAD_DIM).transpose(0, 1, 3, 2)

    def from_chunks_t(a):
        return a.transpose(1, 3, 0, 2).reshape(t, WIDTH)

    dya_h, dyb_h = _heads(dya), _heads(dyb)
    dqa, dka, dva = sb_bwd("sb_bwd", qkv, qt, dya_h, blocks_t(dya_h))
    dqb, dkb, dvb, dcc, dcr = fox_bwd("fox_bwd", qkv, qt, yb_h, dyb_h, blocks_t(dyb_h), lse, c_col, c_row)
    dc = (dcc[:, :, 0] + dcr.reshape(N_HEADS, t)).T
    df, g_bf = fox_gate_bwd("fox_gate_bwd", jnp.pad(dc, ((0, 0), (0, QB - N_HEADS))), f_logit, b_forget_row)
    dqkv = jnp.concatenate([_unheads(dqa), from_chunks_t(dka), from_chunks_t(dva),
                            _unheads(dqb), from_chunks_t(dkb), from_chunks_t(dvb)], axis=1)
    dcat = jnp.concatenate([dqkv, dgp, df.astype(BF16), jnp.zeros((t, F_PAD - QB), BF16)], axis=1)
    g_wcat = wgrad_cat("wgrad_cat", h2, dcat)
    tm = _row_tile(t, 512)
    tkc = 768
    nkc = wcat.shape[1] // tkc
    dx1, gn_mix = dh_rms_bwd(
        "mix_dh", [(dcat, wcat)],
        [(pl.BlockSpec((tm, tkc), lambda m, k: (m, k)), pl.BlockSpec((d, tkc), lambda m, k: (0, k)))],
        (t // tm, nkc), nkc, x1, rstd2, norm_mix, dx2)
    g_in = jnp.concatenate([g_wcat[:, :QKV_COLS], g_wcat[:, f_off:f_off + n_forget]], axis=1)
    g_mx = jnp.concatenate([_shards_from_cols(g_in), _shards_from_cols(g_wcat[:, QKV_COLS:f_off]),
                            _shards_from_cols(jnp.concatenate([g_up_a, g_up_b], axis=0))], axis=2)[:, None]
    s_mx, s_wo = reduce_scatter("rs_mixer", [g_mx, g_wo.reshape(N_CHIPS, 1, d // N_CHIPS, d)], c_arr)

    dx0, gn_ffn1, g_gu1, g_dn1 = ffn_backward("ffn1", dx1, saved1, norm_ffn1, gu1, dn1)
    s_gu1, s_dn1 = reduce_scatter("rs_ffn1", [g_gu1, g_dn1], c_arr)

    def pack_small(n1, nm, n2, nf, bg, bf, last):
        return jnp.concatenate([n1, nm, n2, nf, bg.reshape(2, d), jnp.pad(bf, ((0, 0), (0, d - n_forget))), last], axis=0)

    zero_row = jnp.zeros((1, d), F32)
    g_small = pack_small(gn_ffn1, gn_mix, gn_ffn2, gn_final, g_bgate, g_bf[:, :n_forget], jnp.pad(loss_part, ((0, 0), (0, d - 1))))
    w_small = pack_small(norm_ffn1, norm_mix, norm_ffn2, norm_final[None], b_gate, b_forget, zero_row)
    m_small = pack_small(m_norm_ffn1, m_norm_mix, m_norm_ffn2, m_norm_final[None], m_b_gate, m_b_forget, zero_row)
    v_small = pack_small(v_norm_ffn1, v_norm_mix, v_norm_ffn2, v_norm_final[None], v_b_gate, v_b_forget, zero_row)
    smalls = small_allreduce_adamw("small_allreduce_adamw", g_small, w_small, m_small, v_small)

    def unpack_small(p):
        return {"norm_ffn1": p[0:1], "norm_mix": p[1:2], "norm_ffn2": p[2:3], "norm_final": p[3], "b_gate": p[4:6].reshape(1, 2 * d),
                "b_forget": p[6:7, :n_forget]}

    loss = smalls[0][7, 0]
    small_out = [unpack_small(p) for p in smalls]

    grads = {
        "w_ffn1_gate": s_gu1[0], "w_ffn1_up": s_gu1[1], "w_ffn1_down": s_dn1[0],
        "w_in": s_mx[0][:, :in4], "w_gate": s_mx[0][:, in4:in4 + gate4],
        "w_up_a": s_mx[0][:WIDTH, in4 + gate4:], "w_up_b": s_mx[0][WIDTH:, in4 + gate4:], "w_out": s_wo[0],
        "w_ffn2_gate": s_gu2[0], "w_ffn2_up": s_gu2[1], "w_ffn2_down": s_dn2[0],
    }
    weights = {"w_ffn1_gate": (w_ffn1_gate, m_w_ffn1_gate, v_w_ffn1_gate), "w_ffn1_up": (w_ffn1_up, m_w_ffn1_up, v_w_ffn1_up),
               "w_ffn1_down": (w_ffn1_down, m_w_ffn1_down, v_w_ffn1_down), "w_in": (w_in, m_w_in, v_w_in),
               "w_gate": (w_gate, m_w_gate, v_w_gate), "w_up_a": (w_up_a, m_w_up_a, v_w_up_a), "w_up_b": (w_up_b, m_w_up_b, v_w_up_b),
               "w_out": (w_out, m_w_out, v_w_out), "w_ffn2_gate": (w_ffn2_gate, m_w_ffn2_gate, v_w_ffn2_gate),
               "w_ffn2_up": (w_ffn2_up, m_w_ffn2_up, v_w_ffn2_up), "w_ffn2_down": (w_ffn2_down, m_w_ffn2_down, v_w_ffn2_down)}
    big_out = {}
    for wname, (w, m, v) in weights.items():
        g = grads[wname]
        delta, new_m, new_v = adamw(f"adamw_{wname}", w[0], g, m[0], v[0])
        big_out[wname] = (g[None], delta[None], new_m[None], new_v[None])

    order = ["norm_ffn1", "w_ffn1_gate", "w_ffn1_up", "w_ffn1_down", "norm_mix", "w_in", "b_forget", "w_gate", "b_gate",
             "w_up_a", "w_up_b", "w_out", "norm_ffn2", "w_ffn2_gate", "w_ffn2_up", "w_ffn2_down", "norm_final"]
    outs = [loss, dx0[None]]
    for kind in range(4):
        for wname in order:
            outs.append(big_out[wname][kind] if wname in big_out else small_out[kind][wname])
    return tuple(outs)
```

```python
import functools

import jax
import jax.numpy as jnp
from jax import lax
from jax.experimental import pallas as pl
from jax.experimental.pallas import tpu as pltpu

F32 = jnp.float32
BF16 = jnp.bfloat16

HEAD_DIM = 64
N_HEADS = 8
WIDTH = N_HEADS * HEAD_DIM
QKV_COLS = 6 * WIDTH
RMS_EPS = 1e-6
ATTN_SCALE = HEAD_DIM ** -0.5
N_CHIPS = 4
QB = 128
BQ = 512
CS = 256
N_SUB = BQ // CS
F_PAD = 256
NEG_BIG = -1e30

ADAM_LR = 0.001
ADAM_B1 = 0.9
ADAM_B2 = 0.999
ADAM_EPS = 1e-08
ADAM_WD = 0.01
ADAM_STEP = 10

VMEM_LIMIT_BYTES = 48 * 1024 * 1024
MESH = pl.DeviceIdType.MESH

NN = ((1,), (0,))
NT = ((1,), (1,))
TN = ((0,), (0,))


def _params(semantics):
    return pltpu.CompilerParams(dimension_semantics=semantics, vmem_limit_bytes=VMEM_LIMIT_BYTES)


def _dot(a, b, contract):
    return lax.dot_general(a.astype(BF16), b.astype(BF16), (contract, ((), ())), preferred_element_type=F32)


def _sigmoid(x):
    return 1.0 / (1.0 + jnp.exp(-x))


def _log1pexp_neg_abs(z):
    return jnp.log(1.0 + jnp.exp(-jnp.abs(z)))


def _split3(x):
    hi = x.astype(BF16)
    r1 = x - hi.astype(F32)
    mid = r1.astype(BF16)
    lo = (r1 - mid.astype(F32)).astype(BF16)
    return hi, mid, lo


def _dot_exact_rhs01(x, m01):
    hi, mid, lo = _split3(x)
    d = lambda p: lax.dot_general(p, m01, (NN, ((), ())), preferred_element_type=F32)
    return d(hi) + d(mid) + d(lo)


def _dot_exact_lhs01(m01, x):
    hi, mid, lo = _split3(x)
    d = lambda p: lax.dot_general(m01, p, (NN, ((), ())), preferred_element_type=F32)
    return d(hi) + d(mid) + d(lo)


def _iota2(shape, dim):
    return lax.broadcasted_iota(jnp.int32, shape, dim)


def _mm(name, pairs, contract, grid, pair_specs, out_shape, out_specs, acc_shape, nk, epilogue,
        extras=(), extra_specs=(), semantics=None):
    n_pairs = len(pairs)
    n_extra = len(extras)
    n_out = len(out_shape)

    def body(*refs):
        ab = refs[:2 * n_pairs]
        ex = refs[2 * n_pairs:2 * n_pairs + n_extra]
        outs = refs[2 * n_pairs + n_extra:2 * n_pairs + n_extra + n_out]
        acc = refs[-1]
        ids = [pl.program_id(i) for i in range(len(grid))]
        k = ids[-1]

        @pl.when(k == 0)
        def _():
            acc[...] = jnp.zeros_like(acc)

        part = _dot(ab[0][...], ab[1][...], contract)
        for p in range(1, n_pairs):
            part += _dot(ab[2 * p][...], ab[2 * p + 1][...], contract)
        acc[...] += part

        @pl.when(k == nk - 1)
        def _():
            epilogue(acc[...], ex, outs, ids)

    operands = [t for pair in pairs for t in pair] + list(extras)
    in_specs = [s for pair in pair_specs for s in pair] + list(extra_specs)
    if semantics is None:
        semantics = ("parallel",) * (len(grid) - 1) + ("arbitrary",)
    return pl.pallas_call(
        body, name=name, grid=grid, in_specs=in_specs, out_specs=list(out_specs), out_shape=list(out_shape),
        scratch_shapes=[pltpu.VMEM(acc_shape, F32)], compiler_params=_params(semantics),
    )(*operands)


def _row_tile(rows, target):
    t = min(rows, target)
    while rows % t:
        t //= 2
    return t


def rms_fwd(name, x, g):
    t, d = x.shape
    tr = _row_tile(t, 256)

    def body(x_ref, g_ref, h_ref, r_ref):
        xv = x_ref[...]
        r = lax.rsqrt(jnp.mean(xv * xv, axis=-1, keepdims=True) + RMS_EPS)
        h_ref[...] = (xv * r * g_ref[...]).astype(BF16)
        r_ref[...] = r

    return pl.pallas_call(
        body, name=name, grid=(t // tr,),
        in_specs=[pl.BlockSpec((tr, d), lambda i: (i, 0)), pl.BlockSpec((1, d), lambda i: (0, 0))],
        out_specs=[pl.BlockSpec((tr, d), lambda i: (i, 0)), pl.BlockSpec((tr, 1), lambda i: (i, 0))],
        out_shape=[jax.ShapeDtypeStruct((t, d), BF16), jax.ShapeDtypeStruct((t, 1), F32)],
        compiler_params=_params(("parallel",)),
    )(x, g)


GATE, UP, DOWN = 0, 1, 2


def _ffn_w_spec(which, f4, d, index_of_j):
    return pl.BlockSpec((None, None, f4, d), lambda *ids: (index_of_j(*ids), which, 0, 0))


def ffn_up(name, h, w3):
    t, d = h.shape
    ns, _, f4, _ = w3.shape
    tm = _row_tile(t, 512)

    def body(h_ref, wg_ref, wu_ref, a_ref, b_ref, s_ref):
        hv = h_ref[...]
        a = _dot(hv, wg_ref[...], NT)
        b = _dot(hv, wu_ref[...], NT)
        a_ref[...] = a.astype(BF16)
        b_ref[...] = b.astype(BF16)
        s_ref[...] = (a * _sigmoid(a) * b).astype(BF16)

    act_spec = pl.BlockSpec((None, tm, f4), lambda j, m: (j, m, 0))
    return pl.pallas_call(
        body, name=name, grid=(ns, t // tm),
        in_specs=[pl.BlockSpec((tm, d), lambda j, m: (m, 0)),
                  _ffn_w_spec(GATE, f4, d, lambda j, m: j), _ffn_w_spec(UP, f4, d, lambda j, m: j)],
        out_specs=[act_spec, act_spec, act_spec],
        out_shape=[jax.ShapeDtypeStruct((ns, t, f4), BF16)] * 3,
        compiler_params=_params(("parallel", "parallel")),
    )(h, w3, w3)


def mm_residual(name, s, w, w_spec, x, scale):
    nj, t, kdim = s.shape
    n = x.shape[1]
    tm = _row_tile(t, 512)

    def epilogue(acc, ex, outs, ids):
        outs[0][...] = ex[0][...] + scale * acc

    return _mm(
        name, [(s, w)], NN, (t // tm, nj),
        [(pl.BlockSpec((None, tm, kdim), lambda m, j: (j, m, 0)), w_spec)],
        [jax.ShapeDtypeStruct((t, n), F32)], [pl.BlockSpec((tm, n), lambda m, j: (m, 0))], (tm, n), nj, epilogue,
        extras=[x], extra_specs=[pl.BlockSpec((tm, n), lambda m, j: (m, 0))],
    )[0]


def ffn_bwd_act(name, dx, w3, a, b):
    t, d = dx.shape
    ns, _, f4, _ = w3.shape
    tm = _row_tile(t, 512)

    def body(dx_ref, wd_ref, a_ref, b_ref, da_ref, db_ref):
        ds = _dot(0.5 * dx_ref[...], wd_ref[...], NT)
        av = a_ref[...].astype(F32)
        sig = _sigmoid(av)
        da_ref[...] = (ds * b_ref[...].astype(F32) * (sig * (1.0 + av * (1.0 - sig)))).astype(BF16)
        db_ref[...] = (ds * (av * sig)).astype(BF16)

    act_spec = pl.BlockSpec((None, tm, f4), lambda j, m: (j, m, 0))
    return pl.pallas_call(
        body, name=name, grid=(ns, t // tm),
        in_specs=[pl.BlockSpec((tm, d), lambda j, m: (m, 0)), _ffn_w_spec(DOWN, f4, d, lambda j, m: j), act_spec, act_spec],
        out_specs=[act_spec, act_spec],
        out_shape=[jax.ShapeDtypeStruct((ns, t, f4), BF16)] * 2,
        compiler_params=_params(("parallel", "parallel")),
    )(dx, w3, a, b)


def ffn_wgrad(name, h, da, db, s, dx):
    t, d = h.shape
    ns, _, f4 = da.shape
    tk = _row_tile(t, 512)
    nk = t // tk

    def body(h_ref, da_ref, db_ref, s_ref, dx_ref, o_ref, acc):
        k = pl.program_id(1)

        @pl.when(k == 0)
        def _():
            acc[...] = jnp.zeros_like(acc)

        hv = h_ref[...]
        acc[GATE] += _dot(da_ref[...], hv, TN)
        acc[UP] += _dot(db_ref[...], hv, TN)
        acc[DOWN] += _dot(s_ref[...], 0.5 * dx_ref[...], TN)

        @pl.when(k == nk - 1)
        def _():
            o_ref[...] = acc[...].astype(BF16)

    act_spec = pl.BlockSpec((None, tk, f4), lambda j, k: (j, k, 0))
    row_spec = pl.BlockSpec((tk, d), lambda j, k: (k, 0))
    return pl.pallas_call(
        body, name=name, grid=(ns, nk),
        in_specs=[row_spec, act_spec, act_spec, act_spec, row_spec],
        out_specs=pl.BlockSpec((None, 3, f4, d), lambda j, k: (j, 0, 0, 0)),
        out_shape=jax.ShapeDtypeStruct((ns, 3, f4, d), BF16),
        scratch_shapes=[pltpu.VMEM((3, f4, d), F32)],
        compiler_params=_params(("parallel", "arbitrary")),
    )(h, da, db, s, dx)


def dh_rms_bwd(name, pairs, pair_specs, contract, grid, nk, x, rstd, g, dx_in):
    t, d = x.shape
    tm = t // grid[0]

    def epilogue(acc, ex, outs, ids):
        x_ref, r_ref, g_ref, dxin_ref = ex
        r = r_ref[...]
        xhat = x_ref[...] * r
        dhg = acc * g_ref[...]
        proj = jnp.mean(dhg * xhat, axis=-1, keepdims=True)
        outs[0][...] = dxin_ref[...] + r * (dhg - xhat * proj)
        part = jnp.sum(acc * xhat, axis=0, keepdims=True)

        @pl.when(ids[0] == 0)
        def _():
            outs[1][...] = part

        @pl.when(ids[0] != 0)
        def _():
            outs[1][...] += part

    row = pl.BlockSpec((tm, d), lambda m, k: (m, 0))
    return _mm(
        name, pairs, contract, grid, pair_specs,
        [jax.ShapeDtypeStruct((t, d), F32), jax.ShapeDtypeStruct((1, d), F32)],
        [row, pl.BlockSpec((1, d), lambda m, k: (0, 0))], (tm, d), nk, epilogue,
        extras=[x, rstd, g, dx_in],
        extra_specs=[row, pl.BlockSpec((tm, 1), lambda m, k: (m, 0)), pl.BlockSpec((1, d), lambda m, k: (0, 0)), row],
        semantics=("arbitrary", "arbitrary"),
    )


def ffn_forward(tag, x, g_norm, w3):
    _, _, f4, d = w3.shape
    h, rstd = rms_fwd(f"{tag}_rms", x, g_norm)
    a, b, s = ffn_up(f"{tag}_up", h, w3)
    x_out = mm_residual(f"{tag}_down", s, w3, _ffn_w_spec(DOWN, f4, d, lambda m, j: j), x, 0.5)
    return x_out, (x, h, rstd, a, b, s)


def ffn_backward(tag, dx, saved, g_norm, w3):
    x, h, rstd, a, b, s = saved
    t, d = x.shape
    ns, _, f4, _ = w3.shape
    tm = _row_tile(t, 512)
    da, db = ffn_bwd_act(f"{tag}_bwd_act", dx, w3, a, b)
    g_w3 = ffn_wgrad(f"{tag}_wgrad", h, da, db, s, dx)
    act_spec = pl.BlockSpec((None, tm, f4), lambda m, j: (j, m, 0))
    w_spec = lambda which: _ffn_w_spec(which, f4, d, lambda m, j: j)
    dx_out, g_n = dh_rms_bwd(f"{tag}_dh", [(da, w3), (db, w3)], [(act_spec, w_spec(GATE)), (act_spec, w_spec(UP))], NN,
                             (t // tm, ns), ns, x, rstd, g_norm, dx)
    return dx_out, g_n, g_w3


def proj_cat(name, h, wcat, bias):
    t, d = h.shape
    n = wcat.shape[1]
    tm = _row_tile(t, 512)
    tn = 768

    def epilogue(acc, ex, outs, ids):
        outs[0][...] = acc + ex[0][...]

    return _mm(
        name, [(h, wcat)], NN, (n // tn, t // tm, 1),
        [(pl.BlockSpec((tm, d), lambda j, m, k: (m, 0)), pl.BlockSpec((d, tn), lambda j, m, k: (0, j)))],
        [jax.ShapeDtypeStruct((t, n), F32)], [pl.BlockSpec((tm, tn), lambda j, m, k: (m, j))], (tm, tn), 1, epilogue,
        extras=[bias], extra_specs=[pl.BlockSpec((1, tn), lambda j, m, k: (0, j))],
    )[0]


def mix_fwd(name, ya, yb, wa, wb, pc):
    t, w = ya.shape
    d = wa.shape[1]
    tm = _row_tile(t, 512)
    tn = 512
    off_a = QKV_COLS // tn
    off_b = (QKV_COLS + d) // tn

    def body(ya_ref, yb_ref, wa_ref, wb_ref, pa_ref, pb_ref, ua_ref, ub_ref, mx_ref):
        ua = _dot(ya_ref[...], wa_ref[...], NN)
        ub = _dot(yb_ref[...], wb_ref[...], NN)
        ua_ref[...] = ua
        ub_ref[...] = ub
        mx_ref[...] = (_sigmoid(pa_ref[...]) * ua + _sigmoid(pb_ref[...]) * ub).astype(BF16)

    y_spec = pl.BlockSpec((tm, w), lambda m, n: (m, 0))
    w_spec = pl.BlockSpec((w, tn), lambda m, n: (0, n))
    o_spec = pl.BlockSpec((tm, tn), lambda m, n: (m, n))
    return pl.pallas_call(
        body, name=name, grid=(t // tm, d // tn),
        in_specs=[y_spec, y_spec, w_spec, w_spec,
                  pl.BlockSpec((tm, tn), lambda m, n: (m, off_a + n)), pl.BlockSpec((tm, tn), lambda m, n: (m, off_b + n))],
        out_specs=[o_spec, o_spec, o_spec],
        out_shape=[jax.ShapeDtypeStruct((t, d), F32), jax.ShapeDtypeStruct((t, d), F32), jax.ShapeDtypeStruct((t, d), BF16)],
        compiler_params=_params(("parallel", "parallel")),
    )(ya, yb, wa, wb, pc, pc)


def mix_bwd(name, dx, wo, pc, ua, ub):
    t, d = dx.shape
    tm = _row_tile(t, 512)
    tn = 512
    off_a = QKV_COLS // tn
    off_b = (QKV_COLS + d) // tn

    def body(dx_ref, wo_ref, pa_ref, pb_ref, ua_ref, ub_ref, dua_ref, dub_ref, dpa_ref, dpb_ref, ba_ref, bb_ref):
        dm = _dot(dx_ref[...], wo_ref[...], NT)
        ga = _sigmoid(pa_ref[...])
        gb = _sigmoid(pb_ref[...])
        dua_ref[...] = (dm * ga).astype(BF16)
        dub_ref[...] = (dm * gb).astype(BF16)
        dpa = dm * ua_ref[...] * ga * (1.0 - ga)
        dpb = dm * ub_ref[...] * gb * (1.0 - gb)
        dpa_ref[...] = dpa.astype(BF16)
        dpb_ref[...] = dpb.astype(BF16)
        sa = jnp.sum(dpa, axis=0, keepdims=True)
        sb = jnp.sum(dpb, axis=0, keepdims=True)

        @pl.when(pl.program_id(1) == 0)
        def _():
            ba_ref[...] = sa
            bb_ref[...] = sb

        @pl.when(pl.program_id(1) != 0)
        def _():
            ba_ref[...] += sa
            bb_ref[...] += sb

    tile = pl.BlockSpec((tm, tn), lambda n, m: (m, n))
    bias = pl.BlockSpec((1, tn), lambda n, m: (0, n))
    return pl.pallas_call(
        body, name=name, grid=(d // tn, t // tm),
        in_specs=[pl.BlockSpec((tm, d), lambda n, m: (m, 0)), pl.BlockSpec((tn, d), lambda n, m: (n, 0)),
                  pl.BlockSpec((tm, tn), lambda n, m: (m, off_a + n)), pl.BlockSpec((tm, tn), lambda n, m: (m, off_b + n)),
                  tile, tile],
        out_specs=[tile, tile, tile, tile, bias, bias],
        out_shape=[jax.ShapeDtypeStruct((t, d), BF16)] * 4 + [jax.ShapeDtypeStruct((1, d), F32)] * 2,
        compiler_params=_params(("parallel", "arbitrary")),
    )(dx, wo, pc, pc, ua, ub)


def mm_plain(name, a, b, contract, out_dtype, tk_target=512):
    if contract == NN:
        m, kdim = a.shape
        n = b.shape[1]
    elif contract == NT:
        m, kdim = a.shape
        n = b.shape[0]
    else:
        kdim, m = a.shape
        n = b.shape[1]
    tm = _row_tile(m, 512)
    tk = _row_tile(kdim, tk_target)
    nk = kdim // tk
    if contract == TN:
        a_spec = pl.BlockSpec((tk, tm), lambda i, k: (k, i))
    else:
        a_spec = pl.BlockSpec((tm, tk), lambda i, k: (i, k))
    if contract == NT:
        b_spec = pl.BlockSpec((n, tk), lambda i, k: (0, k))
    else:
        b_spec = pl.BlockSpec((tk, n), lambda i, k: (k, 0))

    def epilogue(acc, ex, outs, ids):
        outs[0][...] = acc.astype(out_dtype)

    return _mm(name, [(a, b)], contract, (m // tm, nk), [(a_spec, b_spec)],
               [jax.ShapeDtypeStruct((m, n), out_dtype)], [pl.BlockSpec((tm, n), lambda i, k: (i, 0))], (tm, n), nk, epilogue)[0]


def wgrad_cat(name, h, dcat):
    t, d = h.shape
    n = dcat.shape[1]
    tn = 768
    tk = _row_tile(t, 512)

    def epilogue(acc, ex, outs, ids):
        outs[0][...] = acc.astype(BF16)

    return _mm(
        name, [(h, dcat)], TN, (n // tn, t // tk),
        [(pl.BlockSpec((tk, d), lambda j, k: (k, 0)), pl.BlockSpec((tk, tn), lambda j, k: (k, j)))],
        [jax.ShapeDtypeStruct((d, n), BF16)], [pl.BlockSpec((d, tn), lambda j, k: (0, j))], (d, tn), t // tk, epilogue,
    )[0]


def fox_prep(name, f, bias):
    t, lanes = f.shape
    nchunk = t // QB

    def body(f_ref, b_ref, c_ref):
        lower = (_iota2((QB, QB), 1) <= _iota2((QB, QB), 0)).astype(BF16)

        def chunk(n, carry):
            rows = pl.ds(pl.multiple_of(n * QB, QB), QB)
            u = f_ref[rows, :] + b_ref[...]
            lf = jnp.minimum(u, 0.0) - _log1pexp_neg_abs(u)
            c = _dot_exact_lhs01(lower, lf) + carry
            c_ref[rows, :] = c
            return c[QB - 1:QB, :]

        lax.fori_loop(0, nchunk, chunk, jnp.zeros((1, lanes), F32))

    return pl.pallas_call(body, name=name, out_shape=jax.ShapeDtypeStruct((t, lanes), F32),
                          compiler_params=pltpu.CompilerParams(vmem_limit_bytes=VMEM_LIMIT_BYTES))(f, bias)


def fox_gate_bwd(name, dc, f, bias):
    t, lanes = dc.shape
    nchunk = t // QB

    def body(dc_ref, f_ref, b_ref, df_ref, gb_ref):
        upper = (_iota2((QB, QB), 1) >= _iota2((QB, QB), 0)).astype(BF16)

        def chunk(n, carry):
            tail, total = carry
            rows = pl.ds(pl.multiple_of((nchunk - 1 - n) * QB, QB), QB)
            dlf = _dot_exact_lhs01(upper, dc_ref[rows, :]) + tail
            u = f_ref[rows, :] + b_ref[...]
            df = dlf * jnp.exp(jnp.minimum(-u, 0.0) - _log1pexp_neg_abs(u))
            df_ref[rows, :] = df
            return dlf[0:1, :], total + jnp.sum(df, axis=0, keepdims=True)

        zero = jnp.zeros((1, lanes), F32)
        _, total = lax.fori_loop(0, nchunk, chunk, (zero, zero))
        gb_ref[...] = total

    return pl.pallas_call(body, name=name,
                          out_shape=[jax.ShapeDtypeStruct((t, lanes), F32), jax.ShapeDtypeStruct((1, lanes), F32)],
                          compiler_params=pltpu.CompilerParams(vmem_limit_bytes=VMEM_LIMIT_BYTES))(dc, f, bias)


def _qrows(i):
    return pl.ds(pl.multiple_of(i * BQ, BQ), BQ)


def _krows(kc):
    return pl.ds(pl.multiple_of(kc * CS, CS), CS)


def _head_spec(t, offset):
    return pl.BlockSpec((None, t, HEAD_DIM), lambda h: (offset + h, 0, 0))


def _head_t_spec(nq, offset):
    return pl.BlockSpec((None, nq, HEAD_DIM, BQ), lambda h: (offset + h, 0, 0, 0))


def _chunk_t_spec(nc):
    return pl.BlockSpec((None, nc, HEAD_DIM, CS), lambda h: (h, 0, 0, 0))


def _dot_split2_rhs01(x, m01):
    hi = x.astype(BF16)
    lo = (x - hi.astype(F32)).astype(BF16)
    d = lambda p: lax.dot_general(p, m01, (NN, ((), ())), preferred_element_type=F32)
    return d(hi) + d(lo)


def _diag_mask(dchunk, inclusive):
    r_io = _iota2((BQ, CS), 0)
    c_io = _iota2((BQ, CS), 1) + dchunk * CS
    return c_io <= r_io if inclusive else c_io < r_io


def _walk_chunks(i, step, init, right_to_left):
    order = list(reversed(range(N_SUB))) if right_to_left else list(range(N_SUB))

    def diagonal(state):
        for dchunk in order:
            state = step(i * N_SUB + dchunk, state, dchunk)
        return state

    def group(n, state):
        base = ((i - 1 - n) if right_to_left else n) * N_SUB
        for dchunk in order:
            state = step(base + dchunk, state, None)
        return state

    if right_to_left:
        return lax.fori_loop(0, i, group, diagonal(init))
    return diagonal(lax.fori_loop(0, i, group, init))


def sb_fwd(name, qkv):
    t = qkv.shape[1]

    def body(q_ref, k_ref, v_ref, o_ref):
        later = (_iota2((CS, CS), 0) > _iota2((CS, CS), 1)).astype(BF16)

        def qblock(i, _):
            q = q_ref[_qrows(i), :]

            def step(kc, state, dchunk):
                carry, acc = state
                z = _dot(q, k_ref[_krows(kc), :], NT)
                sp = _log1pexp_neg_abs(z)
                lnb = -jnp.maximum(z, 0.0) - sp
                if dchunk is not None:
                    lnb = jnp.where(_diag_mask(dchunk, False), lnb, 0.0)
                w = jnp.exp(jnp.minimum(z, 0.0) - sp + _dot_split2_rhs01(lnb, later) + carry)
                if dchunk is not None:
                    w = jnp.where(_diag_mask(dchunk, False), w, 0.0)
                acc = acc + _dot(w, v_ref[_krows(kc), :], NN)
                return carry + jnp.sum(lnb, axis=1, keepdims=True), acc

            init = (jnp.zeros((BQ, 1), F32), jnp.zeros((BQ, HEAD_DIM), F32))
            _, acc = _walk_chunks(i, step, init, True)
            o_ref[_qrows(i), :] = acc.astype(BF16)
            return 0

        lax.fori_loop(0, t // BQ, qblock, 0)

    return pl.pallas_call(
        body, name=name, grid=(N_HEADS,),
        in_specs=[_head_spec(t, 0), _head_spec(t, N_HEADS), _head_spec(t, 2 * N_HEADS)],
        out_specs=_head_spec(t, 0), out_shape=jax.ShapeDtypeStruct((N_HEADS, t, HEAD_DIM), BF16),
        compiler_params=_params(("parallel",)),
    )(qkv, qkv, qkv)


def sb_bwd(name, qkv, qt, dy, dyt):
    t = qkv.shape[1]
    nq, nc = t // BQ, t // CS

    def body(q_ref, k_ref, v_ref, qt_ref, do_ref, dot_ref, dq_ref, dkt_ref, dvt_ref, g_s, b_s, dkt_acc, dvt_acc):
        later = (_iota2((CS, CS), 0) > _iota2((CS, CS), 1)).astype(BF16)
        earlier = (_iota2((CS, CS), 0) < _iota2((CS, CS), 1)).astype(BF16)
        dkt_acc[...] = jnp.zeros_like(dkt_acc)
        dvt_acc[...] = jnp.zeros_like(dvt_acc)

        def qblock(i, _):
            q = q_ref[_qrows(i), :]
            do = do_ref[_qrows(i), :]
            q_t = qt_ref[i]
            do_t = dot_ref[i]

            def step1(kc, carry, dchunk):
                z = _dot(q, k_ref[_krows(kc), :], NT)
                sp = _log1pexp_neg_abs(z)
                lnb = -jnp.maximum(z, 0.0) - sp
                lsz = jnp.minimum(z, 0.0) - sp
                if dchunk is not None:
                    lnb = jnp.where(_diag_mask(dchunk, False), lnb, 0.0)
                w = jnp.exp(lsz + _dot_split2_rhs01(lnb, later) + carry)
                if dchunk is not None:
                    w = jnp.where(_diag_mask(dchunk, False), w, 0.0)
                g_s[kc] = w * _dot(do, v_ref[_krows(kc), :], NT)
                b_s[kc] = jnp.exp(lsz)
                dvt_acc[kc] += _dot(do_t, w, NN)
                return carry + jnp.sum(lnb, axis=1, keepdims=True)

            _walk_chunks(i, step1, jnp.zeros((BQ, 1), F32), True)

            def step2(kc, state, dchunk):
                before, dq = state
                g = g_s[kc]
                beta = b_s[kc]
                dz = g * (1.0 - beta) - beta * (_dot_split2_rhs01(g, earlier) + before)
                if dchunk is not None:
                    dz = jnp.where(_diag_mask(dchunk, False), dz, 0.0)
                dzb = dz.astype(BF16)
                dq = dq + _dot(dzb, k_ref[_krows(kc), :], NN)
                dkt_acc[kc] += _dot(q_t, dzb, NN)
                return before + jnp.sum(g, axis=1, keepdims=True), dq

            _, dq = _walk_chunks(i, step2, (jnp.zeros((BQ, 1), F32), jnp.zeros((BQ, HEAD_DIM), F32)), False)
            dq_ref[_qrows(i), :] = (dq * ATTN_SCALE).astype(BF16)
            return 0

        lax.fori_loop(0, nq, qblock, 0)
        dkt_ref[...] = dkt_acc[...].astype(BF16)
        dvt_ref[...] = dvt_acc[...].astype(BF16)

    chunked = jax.ShapeDtypeStruct((N_HEADS, nc, HEAD_DIM, CS), BF16)
    return pl.pallas_call(
        body, name=name, grid=(N_HEADS,),
        in_specs=[_head_spec(t, 0), _head_spec(t, N_HEADS), _head_spec(t, 2 * N_HEADS), _head_t_spec(nq, 0),
                  _head_spec(t, 0), _head_t_spec(nq, 0)],
        out_specs=[_head_spec(t, 0), _chunk_t_spec(nc), _chunk_t_spec(nc)],
        out_shape=[jax.ShapeDtypeStruct((N_HEADS, t, HEAD_DIM), BF16), chunked, chunked],
        scratch_shapes=[pltpu.VMEM((nc, BQ, CS), F32), pltpu.VMEM((nc, BQ, CS), F32),
                        pltpu.VMEM((nc, HEAD_DIM, CS), F32), pltpu.VMEM((nc, HEAD_DIM, CS), F32)],
        compiler_params=_params(("parallel",)),
    )(qkv, qkv, qkv, qt, dy, dyt)


def _col_spec(t):
    return pl.BlockSpec((None, t, 1), lambda h: (h, 0, 0))


def _row_spec(nc):
    return pl.BlockSpec((None, nc, 1, CS), lambda h: (h, 0, 0, 0))


def fox_fwd(name, qkv, c_col, c_row):
    t = qkv.shape[1]

    def body(q_ref, k_ref, v_ref, cc_ref, cr_ref, o_ref, lse_ref):
        def qblock(i, _):
            q = q_ref[_qrows(i), :]
            ct = cc_ref[_qrows(i), :]

            def step(kc, state, dchunk):
                m, l, acc = state
                s = _dot(q, k_ref[_krows(kc), :], NT) + ct - cr_ref[kc]
                if dchunk is not None:
                    s = jnp.where(_diag_mask(dchunk, True), s, NEG_BIG)
                m_new = jnp.maximum(m, jnp.max(s, axis=1, keepdims=True))
                alpha = jnp.exp(m - m_new)
                p = jnp.exp(s - m_new)
                if dchunk is not None:
                    p = jnp.where(_diag_mask(dchunk, True), p, 0.0)
                l = l * alpha + jnp.sum(p, axis=1, keepdims=True)
                acc = acc * alpha + _dot(p, v_ref[_krows(kc), :], NN)
                return m_new, l, acc

            init = (jnp.full((BQ, 1), NEG_BIG, F32), jnp.zeros((BQ, 1), F32), jnp.zeros((BQ, HEAD_DIM), F32))
            m, l, acc = _walk_chunks(i, step, init, False)
            o_ref[_qrows(i), :] = (acc / l).astype(BF16)
            lse_ref[_qrows(i), :] = m + jnp.log(l)
            return 0

        lax.fori_loop(0, t // BQ, qblock, 0)

    return pl.pallas_call(
        body, name=name, grid=(N_HEADS,),
        in_specs=[_head_spec(t, 3 * N_HEADS), _head_spec(t, 4 * N_HEADS), _head_spec(t, 5 * N_HEADS),
                  _col_spec(t), _row_spec(t // CS)],
        out_specs=[_head_spec(t, 0), _col_spec(t)],
        out_shape=[jax.ShapeDtypeStruct((N_HEADS, t, HEAD_DIM), BF16), jax.ShapeDtypeStruct((N_HEADS, t, 1), F32)],
        compiler_params=_params(("parallel",)),
    )(qkv, qkv, qkv, c_col, c_row)


def fox_bwd(name, qkv, qt, y, dy, dyt, lse, c_col, c_row):
    t = qkv.shape[1]
    nq, nc = t // BQ, t // CS

    def body(q_ref, k_ref, v_ref, qt_ref, o_ref, do_ref, dot_ref, lse_ref, cc_ref, cr_ref,
             dq_ref, dkt_ref, dvt_ref, dcc_ref, dcr_ref, dkt_acc, dvt_acc, dcr_acc):
        dkt_acc[...] = jnp.zeros_like(dkt_acc)
        dvt_acc[...] = jnp.zeros_like(dvt_acc)
        dcr_acc[...] = jnp.zeros_like(dcr_acc)

        def qblock(i, _):
            q = q_ref[_qrows(i), :]
            do = do_ref[_qrows(i), :]
            q_t = qt_ref[i]
            do_t = dot_ref[i]
            ct = cc_ref[_qrows(i), :]
            lse_i = lse_ref[_qrows(i), :]
            delta = jnp.sum(do.astype(F32) * o_ref[_qrows(i), :].astype(F32), axis=1, keepdims=True)

            def step(kc, state, dchunk):
                dq, dct = state
                s = _dot(q, k_ref[_krows(kc), :], NT) + ct - cr_ref[kc]
                p = jnp.exp(s - lse_i)
                if dchunk is not None:
                    p = jnp.where(_diag_mask(dchunk, True), p, 0.0)
                ds = p * (_dot(do, v_ref[_krows(kc), :], NT) - delta)
                dvt_acc[kc] += _dot(do_t, p, NN)
                dsb = ds.astype(BF16)
                dq = dq + _dot(dsb, k_ref[_krows(kc), :], NN)
                dkt_acc[kc] += _dot(q_t, dsb, NN)
                dcr_acc[kc] -= jnp.sum(ds, axis=0, keepdims=True)
                return dq, dct + jnp.sum(ds, axis=1, keepdims=True)

            dq, dct = _walk_chunks(i, step, (jnp.zeros((BQ, HEAD_DIM), F32), jnp.zeros((BQ, 1), F32)), False)
            dq_ref[_qrows(i), :] = (dq * ATTN_SCALE).astype(BF16)
            dcc_ref[_qrows(i), :] = dct
            return 0

        lax.fori_loop(0, nq, qblock, 0)
        dkt_ref[...] = dkt_acc[...].astype(BF16)
        dvt_ref[...] = dvt_acc[...].astype(BF16)
        dcr_ref[...] = dcr_acc[...]

    chunked = jax.ShapeDtypeStruct((N_HEADS, nc, HEAD_DIM, CS), BF16)
    return pl.pallas_call(
        body, name=name, grid=(N_HEADS,),
        in_specs=[_head_spec(t, 3 * N_HEADS), _head_spec(t, 4 * N_HEADS), _head_spec(t, 5 * N_HEADS), _head_t_spec(nq, N_HEADS),
                  _head_spec(t, 0), _head_spec(t, 0), _head_t_spec(nq, 0), _col_spec(t), _col_spec(t), _row_spec(nc)],
        out_specs=[_head_spec(t, 0), _chunk_t_spec(nc), _chunk_t_spec(nc), _col_spec(t), _row_spec(nc)],
        out_shape=[jax.ShapeDtypeStruct((N_HEADS, t, HEAD_DIM), BF16), chunked, chunked,
                   jax.ShapeDtypeStruct((N_HEADS, t, 1), F32), jax.ShapeDtypeStruct((N_HEADS, nc, 1, CS), F32)],
        scratch_shapes=[pltpu.VMEM((nc, HEAD_DIM, CS), F32), pltpu.VMEM((nc, HEAD_DIM, CS), F32), pltpu.VMEM((nc, 1, CS), F32)],
        compiler_params=_params(("parallel",)),
    )(qkv, qkv, qkv, qt, y, dy, dyt, lse, c_col, c_row)


def loss_head(name, x, g, target):
    t, d = x.shape
    tr = _row_tile(t, 256)

    def body(x_ref, g_ref, t_ref, dx_ref, gn_ref, loss_ref):
        xv = x_ref[...]
        r = lax.rsqrt(jnp.mean(xv * xv, axis=-1, keepdims=True) + RMS_EPS)
        xhat = xv * r
        gv = g_ref[...]
        err = xhat * gv - t_ref[...]
        part_loss = 0.5 * jnp.sum(jnp.mean(err * err, axis=-1, keepdims=True), axis=0, keepdims=True)
        dy = err * (1.0 / d)
        dyg = dy * gv
        dx_ref[...] = r * (dyg - xhat * jnp.mean(dyg * xhat, axis=-1, keepdims=True))
        part_g = jnp.sum(dy * xhat, axis=0, keepdims=True)

        @pl.when(pl.program_id(0) == 0)
        def _():
            gn_ref[...] = part_g
            loss_ref[...] = part_loss

        @pl.when(pl.program_id(0) != 0)
        def _():
            gn_ref[...] += part_g
            loss_ref[...] += part_loss

    row = pl.BlockSpec((tr, d), lambda i: (i, 0))
    return pl.pallas_call(
        body, name=name, grid=(t // tr,),
        in_specs=[row, pl.BlockSpec((1, d), lambda i: (0, 0)), row],
        out_specs=[row, pl.BlockSpec((1, d), lambda i: (0, 0)), pl.BlockSpec((1, 1), lambda i: (0, 0))],
        out_shape=[jax.ShapeDtypeStruct((t, d), F32), jax.ShapeDtypeStruct((1, d), F32), jax.ShapeDtypeStruct((1, 1), F32)],
        compiler_params=_params(("arbitrary",)),
    )(x, g, target)


def _place():
    return lax.axis_index("x"), lax.axis_index("y"), lax.axis_index("c")


def _other_chips(x, y):
    return [(1 - x, y), (x, 1 - y), (1 - x, 1 - y)]


def _half(ref, c, rows):
    return ref.at[:, pl.ds(c * (rows // 2), rows // 2), :]


_ANY = pl.BlockSpec(memory_space=pl.ANY)


def gather_weights(name, bufs):
    n = len(bufs)

    def body(*refs):
        outs = refs[n:2 * n]
        send_sems, recv_sems = refs[2 * n:]
        x, y, c = _place()
        chips = _other_chips(x, y)
        me = 2 * x + y
        sibling = (x, y, 1 - c)
        first, passed = [], []
        for i in range(n):
            rows = outs[i].shape[2]
            mine = _half(outs[i].at[me], c, rows)
            for j, (qx, qy) in enumerate(chips):
                k = 6 * i + j
                rc = pltpu.make_async_remote_copy(
                    src_ref=mine, dst_ref=mine,
                    send_sem=send_sems.at[k], recv_sem=recv_sems.at[k], device_id=(qx, qy, c), device_id_type=MESH)
                rc.start()
                first.append(rc)
        for i in range(n):
            rows = outs[i].shape[2]
            for j, (qx, qy) in enumerate(chips):
                k = 6 * i + j
                block = _half(outs[i].at[2 * qx + qy], c, rows)
                pltpu.make_async_remote_copy(
                    src_ref=block, dst_ref=block, send_sem=send_sems.at[k], recv_sem=recv_sems.at[k],
                    device_id=(qx, qy, c), device_id_type=MESH).wait_recv()
                fw = pltpu.make_async_remote_copy(
                    src_ref=block, dst_ref=block, send_sem=send_sems.at[k + 3], recv_sem=recv_sems.at[k + 3],
                    device_id=sibling, device_id_type=MESH)
                fw.start()
                passed.append(fw)
        for i in range(n):
            rows = outs[i].shape[2]
            for j, (qx, qy) in enumerate(chips):
                k = 6 * i + j + 3
                block = _half(outs[i].at[2 * qx + qy], 1 - c, rows)
                pltpu.make_async_remote_copy(
                    src_ref=block, dst_ref=block, send_sem=send_sems.at[k], recv_sem=recv_sems.at[k],
                    device_id=sibling, device_id_type=MESH).wait_recv()
        for cp in first + passed:
            cp.wait_send()

    return pl.pallas_call(
        body, name=name, in_specs=[_ANY] * n, out_specs=[_ANY] * n,
        out_shape=[jax.ShapeDtypeStruct(b.shape, b.dtype) for b in bufs],
        input_output_aliases={i: i for i in range(n)},
        scratch_shapes=[pltpu.SemaphoreType.DMA((6 * n,)), pltpu.SemaphoreType.DMA((6 * n,))],
        compiler_params=pltpu.CompilerParams(has_side_effects=True),
    )(*bufs)


def send_half_to_sibling(name, grads):
    n = len(grads)

    def body(*refs):
        srcs, outs = refs[:n], refs[n:2 * n]
        send_sems, recv_sems = refs[2 * n:]
        x, y, c = _place()
        sibling = (x, y, 1 - c)
        copies = []
        for i in range(n):
            rows = srcs[i].shape[2]
            rc = pltpu.make_async_remote_copy(
                src_ref=srcs[i].at[:, :, pl.ds((1 - c) * (rows // 2), rows // 2), :], dst_ref=outs[i],
                send_sem=send_sems.at[i], recv_sem=recv_sems.at[i], device_id=sibling, device_id_type=MESH)
            rc.start()
            copies.append(rc)
        for rc in copies:
            rc.wait()

    def half_shape(g):
        s = g.shape
        return jax.ShapeDtypeStruct((s[0], s[1], s[2] // 2, s[3]), g.dtype)

    return pl.pallas_call(
        body, name=name, in_specs=[_ANY] * n, out_specs=[_ANY] * n, out_shape=[half_shape(g) for g in grads],
        scratch_shapes=[pltpu.SemaphoreType.DMA((n,)), pltpu.SemaphoreType.DMA((n,))],
        compiler_params=pltpu.CompilerParams(has_side_effects=True),
    )(*grads)


def exchange_between_chips(name, parts):
    n = len(parts)

    def body(*refs):
        srcs, outs = refs[:n], refs[n:2 * n]
        send_sems, recv_sems = refs[2 * n:]
        x, y, c = _place()
        chips = _other_chips(x, y)
        copies = []
        for i in range(n):
            for j, (qx, qy) in enumerate(chips):
                k = 3 * i + j
                rc = pltpu.make_async_remote_copy(
                    src_ref=srcs[i].at[2 * qx + qy], dst_ref=outs[i].at[j],
                    send_sem=send_sems.at[k], recv_sem=recv_sems.at[k], device_id=(qx, qy, c), device_id_type=MESH)
                rc.start()
                copies.append(rc)
        for rc in copies:
            rc.wait()

    return pl.pallas_call(
        body, name=name, in_specs=[_ANY] * n, out_specs=[_ANY] * n,
        out_shape=[jax.ShapeDtypeStruct((N_CHIPS - 1,) + p.shape[1:], p.dtype) for p in parts],
        scratch_shapes=[pltpu.SemaphoreType.DMA((3 * n,)), pltpu.SemaphoreType.DMA((3 * n,))],
        compiler_params=pltpu.CompilerParams(has_side_effects=True),
    )(*parts)


def share_halves(name, bufs):
    n = len(bufs)

    def body(*refs):
        outs = refs[n:2 * n]
        send_sems, recv_sems = refs[2 * n:]
        x, y, c = _place()
        copies = []
        for i in range(n):
            mine = _half(outs[i], c, outs[i].shape[1])
            rc = pltpu.make_async_remote_copy(
                src_ref=mine, dst_ref=mine, send_sem=send_sems.at[i], recv_sem=recv_sems.at[i],
                device_id=(x, y, 1 - c), device_id_type=MESH)
            rc.start()
            copies.append(rc)
        for i in range(n):
            theirs = _half(outs[i], 1 - c, outs[i].shape[1])
            pltpu.make_async_remote_copy(
                src_ref=theirs, dst_ref=theirs, send_sem=send_sems.at[i], recv_sem=recv_sems.at[i],
                device_id=(x, y, 1 - c), device_id_type=MESH).wait_recv()
        for rc in copies:
            rc.wait_send()

    return pl.pallas_call(
        body, name=name, in_specs=[_ANY] * n, out_specs=[_ANY] * n,
        out_shape=[jax.ShapeDtypeStruct(b.shape, b.dtype) for b in bufs],
        input_output_aliases={i: i for i in range(n)},
        scratch_shapes=[pltpu.SemaphoreType.DMA((n,)), pltpu.SemaphoreType.DMA((n,))],
        compiler_params=pltpu.CompilerParams(has_side_effects=True),
    )(*bufs)


def pair_sum(name, grad, recv, c):
    ns, na, rh, cols = recv.shape
    tr = _row_tile(rh, 256) if rh % 256 == 0 else rh
    nt = rh // tr

    def body(c_ref, g_ref, r_ref, o_ref):
        o_ref[...] = (g_ref[...].astype(F32) + r_ref[...].astype(F32)).astype(BF16)

    blk = (None, None, tr, cols)
    return pl.pallas_call(
        body, name=name,
        grid_spec=pltpu.PrefetchScalarGridSpec(
            num_scalar_prefetch=1, grid=(ns, na, nt),
            in_specs=[pl.BlockSpec(blk, lambda s, a, r, c_ref: (s, a, c_ref[0] * nt + r, 0)),
                      pl.BlockSpec(blk, lambda s, a, r, c_ref: (s, a, r, 0))],
            out_specs=pl.BlockSpec(blk, lambda s, a, r, c_ref: (s, a, r, 0))),
        out_shape=jax.ShapeDtypeStruct(recv.shape, BF16),
        compiler_params=_params(("parallel", "parallel", "parallel")),
    )(c, grad, recv)


def chip_sum(name, parts, landed, place):
    _, na, rh, cols = parts.shape
    tr = _row_tile(rh, 256) if rh % 256 == 0 else rh
    nt = rh // tr

    def body(place_ref, p_ref, l_ref, o_ref):
        total = p_ref[...].astype(F32)
        for s in range(N_CHIPS - 1):
            total = total + l_ref[s].astype(F32)
        o_ref[...] = total

    return pl.pallas_call(
        body, name=name,
        grid_spec=pltpu.PrefetchScalarGridSpec(
            num_scalar_prefetch=1, grid=(na, nt),
            in_specs=[pl.BlockSpec((None, None, tr, cols), lambda a, r, pr: (pr[1], a, r, 0)),
                      pl.BlockSpec((N_CHIPS - 1, None, tr, cols), lambda a, r, pr: (0, a, r, 0))],
            out_specs=pl.BlockSpec((None, tr, cols), lambda a, r, pr: (a, pr[0] * nt + r, 0))),
        out_shape=jax.ShapeDtypeStruct((na, 2 * rh, cols), F32),
        compiler_params=_params(("parallel", "parallel")),
    )(place, parts, landed)


def reduce_scatter(tag, grads, place):
    recv = send_half_to_sibling(f"{tag}_to_sibling", grads)
    parts = [pair_sum(f"{tag}_pair_sum{i}", g, r, place) for i, (g, r) in enumerate(zip(grads, recv))]
    landed = exchange_between_chips(f"{tag}_between_chips", parts)
    halves = [chip_sum(f"{tag}_chip_sum{i}", p, l, place) for i, (p, l) in enumerate(zip(parts, landed))]
    return share_halves(f"{tag}_share", halves)


def _adamw_math(w, g, m, v):
    m = ADAM_B1 * m + (1.0 - ADAM_B1) * g
    v = ADAM_B2 * v + (1.0 - ADAM_B2) * (g * g)
    m_hat = m / (1.0 - ADAM_B1 ** ADAM_STEP)
    v_hat = v / (1.0 - ADAM_B2 ** ADAM_STEP)
    delta = -ADAM_LR * (m_hat / (jnp.sqrt(v_hat) + ADAM_EPS) + ADAM_WD * w)
    return delta, m, v


def adamw(name, w, g, m, v):
    rows, cols = w.shape
    tr = _row_tile(rows, 256) if rows % 256 == 0 else rows // 2

    def body(w_ref, g_ref, m_ref, v_ref, d_ref, mo_ref, vo_ref):
        d_ref[...], mo_ref[...], vo_ref[...] = _adamw_math(w_ref[...], g_ref[...], m_ref[...], v_ref[...])

    blk = pl.BlockSpec((tr, cols), lambda i: (i, 0))
    return pl.pallas_call(
        body, name=name, grid=(rows // tr,), in_specs=[blk] * 4, out_specs=[blk] * 3,
        out_shape=[jax.ShapeDtypeStruct(w.shape, F32)] * 3, compiler_params=_params(("parallel",)),
    )(w, g, m, v)


def adamw_stacked(name, ws, g, ms, vs):
    n = len(ws)
    rows, cols = ws[0].shape
    tr = next(r for r in (128, 88, 64, 32, 16, 8) if rows % r == 0)

    def body(*refs):
        w_refs, m_refs, v_refs, g_ref = refs[:n], refs[n:2 * n], refs[2 * n:3 * n], refs[3 * n]
        outs = refs[3 * n + 1:]
        for i in range(n):
            outs[i][...], outs[n + i][...], outs[2 * n + i][...] = _adamw_math(
                w_refs[i][...], g_ref[i], m_refs[i][...], v_refs[i][...])

    blk = pl.BlockSpec((tr, cols), lambda r: (r, 0))
    res = pl.pallas_call(
        body, name=name, grid=(rows // tr,),
        in_specs=[blk] * (3 * n) + [pl.BlockSpec((n, tr, cols), lambda r: (0, r, 0))], out_specs=[blk] * (3 * n),
        out_shape=[jax.ShapeDtypeStruct((rows, cols), F32)] * (3 * n), compiler_params=_params(("parallel",)),
    )(*ws, *ms, *vs, g)
    return res[:n], res[n:2 * n], res[2 * n:]


def small_allreduce_adamw(name, g_part, w, m, v):
    rows, cols = g_part.shape

    def body(g_ref, w_ref, m_ref, v_ref, sum_ref, d_ref, mo_ref, vo_ref, land, send_sems, recv_sems):
        x, y, c = _place()
        me = 4 * x + 2 * y + c
        land[me] = g_ref[...]
        copies = []
        for r in range(1, 8):
            peer = (x ^ (r >> 2), y ^ ((r >> 1) & 1), c ^ (r & 1))
            rc = pltpu.make_async_remote_copy(
                src_ref=g_ref, dst_ref=land.at[me], send_sem=send_sems.at[r - 1], recv_sem=recv_sems.at[r - 1],
                device_id=peer, device_id_type=MESH)
            rc.start()
            copies.append(rc)
        for rc in copies:
            rc.wait()
        total = land[0]
        for s in range(1, 8):
            total = total + land[s]
        sum_ref[...] = total
        d_ref[...], mo_ref[...], vo_ref[...] = _adamw_math(w_ref[...], total, m_ref[...], v_ref[...])

    vmem = pl.BlockSpec(memory_space=pltpu.VMEM)
    return pl.pallas_call(
        body, name=name, in_specs=[vmem] * 4, out_specs=[vmem] * 4,
        out_shape=[jax.ShapeDtypeStruct((rows, cols), F32)] * 4,
        scratch_shapes=[pltpu.VMEM((8, rows, cols), F32), pltpu.SemaphoreType.DMA((7,)), pltpu.SemaphoreType.DMA((7,))],
        compiler_params=pltpu.CompilerParams(has_side_effects=True),
    )(g_part, w, m, v)


def _heads(a):
    t, w = a.shape
    return a.reshape(t, w // HEAD_DIM, HEAD_DIM).transpose(1, 0, 2)


def _unheads(a):
    n, t, _ = a.shape
    return a.transpose(1, 0, 2).reshape(t, n * HEAD_DIM)


def _cols_from_shards(g):
    ns, r, cols = g.shape
    return g.transpose(1, 0, 2).reshape(r, ns * cols)


def _shards_from_cols(a):
    r, cols = a.shape
    return a.reshape(r, N_CHIPS, cols // N_CHIPS).transpose(1, 0, 2)


def kernel(x, norm_ffn1, w_ffn1_gate, w_ffn1_up, w_ffn1_down, norm_mix, w_in, b_forget, w_gate, b_gate, w_up_a, w_up_b, w_out, norm_ffn2, w_ffn2_gate, w_ffn2_up, w_ffn2_down, norm_final, loss_target, m_norm_ffn1, m_w_ffn1_gate, m_w_ffn1_up, m_w_ffn1_down, m_norm_mix, m_w_in, m_b_forget, m_w_gate, m_b_gate, m_w_up_a, m_w_up_b, m_w_out, m_norm_ffn2, m_w_ffn2_gate, m_w_ffn2_up, m_w_ffn2_down, m_norm_final, v_norm_ffn1, v_w_ffn1_gate, v_w_ffn1_up, v_w_ffn1_down, v_norm_mix, v_w_in, v_b_forget, v_w_gate, v_b_gate, v_w_up_a, v_w_up_b, v_w_out, v_norm_ffn2, v_w_ffn2_gate, v_w_ffn2_up, v_w_ffn2_down, v_norm_final):
    t, d = x.shape[1], x.shape[2]
    in4 = w_in.shape[2]
    gate4 = w_gate.shape[2]
    up4 = w_up_a.shape[2]
    in_cols = N_CHIPS * in4
    n_forget = in_cols - QKV_COLS
    assert w_up_a.shape[1] == WIDTH and d == 2 * WIDTH and n_forget == N_HEADS
    nq = t // BQ
    chip = 2 * lax.axis_index("x") + lax.axis_index("y")
    c_arr = jnp.stack([lax.axis_index("c"), chip]).astype(jnp.int32)
    x2d = x[0]
    target = loss_target[0]

    def slot(shard):
        return lax.dynamic_update_slice(jnp.zeros((N_CHIPS,) + shard.shape, BF16), shard.astype(BF16)[None], (chip, 0, 0, 0))

    def ffn_views(wg, wu, wd):
        return [wg[0].T, wu[0].T, wd[0]]

    ffn1_w, ffn1_m, ffn1_v = (ffn_views(w_ffn1_gate, w_ffn1_up, w_ffn1_down), ffn_views(m_w_ffn1_gate, m_w_ffn1_up, m_w_ffn1_down),
                              ffn_views(v_w_ffn1_gate, v_w_ffn1_up, v_w_ffn1_down))
    ffn2_w, ffn2_m, ffn2_v = (ffn_views(w_ffn2_gate, w_ffn2_up, w_ffn2_down), ffn_views(m_w_ffn2_gate, m_w_ffn2_up, m_w_ffn2_down),
                              ffn_views(v_w_ffn2_gate, v_w_ffn2_up, v_w_ffn2_down))
    mx_sh = slot(jnp.concatenate([w_in[0], w_gate[0], jnp.concatenate([w_up_a[0], w_up_b[0]], axis=0)], axis=1)[None])
    wo_sh = slot(w_out)
    (w3_1,) = gather_weights("gather_ffn1", [slot(jnp.stack(ffn1_w))])
    mx, wo = gather_weights("gather_mixer", [mx_sh, wo_sh])
    (w3_2,) = gather_weights("gather_ffn2", [slot(jnp.stack(ffn2_w))])
    mx = mx[:, 0]
    w_in_full = _cols_from_shards(mx[:, :, :in4])
    w_gate_full = _cols_from_shards(mx[:, :, in4:in4 + gate4])
    w_up_full = _cols_from_shards(mx[:, :, in4 + gate4:])
    w_up_a_full, w_up_b_full = w_up_full[:WIDTH], w_up_full[WIDTH:]
    wcat = jnp.concatenate([w_in_full[:, :QKV_COLS], w_gate_full, w_in_full[:, QKV_COLS:],
                            jnp.zeros((d, F_PAD - n_forget), BF16)], axis=1)
    bias_cat = jnp.concatenate([jnp.zeros((1, QKV_COLS), F32), b_gate, jnp.zeros((1, F_PAD), F32)], axis=1)
    f_off = QKV_COLS + 2 * d
    wo_full = wo.reshape(d, d)
    b_forget_row = jnp.pad(b_forget, ((0, 0), (0, QB - n_forget)))

    x1, saved1 = ffn_forward("ffn1", x2d, norm_ffn1, w3_1)
    h2, rstd2 = rms_fwd("mix_rms", x1, norm_mix)
    pc = proj_cat("mix_proj", h2, wcat, bias_cat)
    q_scale = jnp.array([ATTN_SCALE, 1.0, 1.0, ATTN_SCALE, 1.0, 1.0], F32)[None, :, None]
    qkv = _heads((pc[:, :QKV_COLS].reshape(t, 6, WIDTH) * q_scale).astype(BF16).reshape(t, QKV_COLS))
    qt = jnp.concatenate([qkv[:N_HEADS], qkv[3 * N_HEADS:4 * N_HEADS]], axis=0)
    qt = qt.reshape(2 * N_HEADS, nq, BQ, HEAD_DIM).transpose(0, 1, 3, 2)
    f_logit = pc[:, f_off:f_off + QB]
    c_cum = fox_prep("fox_prep", f_logit, b_forget_row)
    c_heads = c_cum[:, :N_HEADS].T
    c_col = c_heads[:, :, None]
    c_row = c_heads.reshape(N_HEADS, t // CS, 1, CS)
    ya_h = sb_fwd("sb_fwd", qkv)
    yb_h, lse = fox_fwd("fox_fwd", qkv, c_col, c_row)
    ya = _unheads(ya_h)
    yb = _unheads(yb_h)
    ua, ub, mixed = mix_fwd("mix_fwd", ya, yb, w_up_a_full, w_up_b_full, pc)
    x2 = mm_residual("mix_out", mixed[None], wo_full[None], pl.BlockSpec((None, d, d), lambda m, j: (j, 0, 0)), x1, 1.0)
    x3, saved2 = ffn_forward("ffn2", x2, norm_ffn2, w3_2)
    dx3, gn_final, loss_part = loss_head("loss_head", x3, norm_final[None], target)

    dx2, gn_ffn2, g_w3_2 = ffn_backward("ffn2", dx3, saved2, norm_ffn2, w3_2)
    (s_w3_2,) = reduce_scatter("rs_ffn2", [g_w3_2], c_arr)

    dua, dub, dpa, dpb, gba, gbb = mix_bwd("mix_bwd", dx2, wo_full, pc, ua, ub)
    dgp = jnp.concatenate([dpa, dpb], axis=1)
    g_bgate = jnp.concatenate([gba, gbb], axis=1)
    g_wo = mm_plain("wgrad_out", mixed, dx2, TN, BF16)
    dya = mm_plain("dya", dua, w_up_a_full, NT, BF16, tk_target=1024)
    dyb = mm_plain("dyb", dub, w_up_b_full, NT, BF16, tk_target=1024)
    g_up_a = mm_plain("wgrad_up_a", ya, dua, TN, BF16)
    g_up_b = mm_plain("wgrad_up_b", yb, dub, TN, BF16)
    def blocks_t(a):
        return a.reshape(N_HEADS, nq, BQ, HEAD_DIM).transpose(0, 1, 3, 2)

    def from_chunks_t(a):
        return a.transpose(1, 3, 0, 2).reshape(t, WIDTH)

    dya_h, dyb_h = _heads(dya), _heads(dyb)
    dqa, dka, dva = sb_bwd("sb_bwd", qkv, qt, dya_h, blocks_t(dya_h))
    dqb, dkb, dvb, dcc, dcr = fox_bwd("fox_bwd", qkv, qt, yb_h, dyb_h, blocks_t(dyb_h), lse, c_col, c_row)
    dc = (dcc[:, :, 0] + dcr.reshape(N_HEADS, t)).T
    df, g_bf = fox_gate_bwd("fox_gate_bwd", jnp.pad(dc, ((0, 0), (0, QB - N_HEADS))), f_logit, b_forget_row)
    dqkv = jnp.concatenate([_unheads(dqa), from_chunks_t(dka), from_chunks_t(dva),
                            _unheads(dqb), from_chunks_t(dkb), from_chunks_t(dvb)], axis=1)
    dcat = jnp.concatenate([dqkv, dgp, df.astype(BF16), jnp.zeros((t, F_PAD - QB), BF16)], axis=1)
    g_wcat = wgrad_cat("wgrad_cat", h2, dcat)
    tm = _row_tile(t, 512)
    tkc = 768
    nkc = wcat.shape[1] // tkc
    dx1, gn_mix = dh_rms_bwd(
        "mix_dh", [(dcat, wcat)],
        [(pl.BlockSpec((tm, tkc), lambda m, k: (m, k)), pl.BlockSpec((d, tkc), lambda m, k: (0, k)))], NT,
        (t // tm, nkc), nkc, x1, rstd2, norm_mix, dx2)
    g_in = jnp.concatenate([g_wcat[:, :QKV_COLS], g_wcat[:, f_off:f_off + n_forget]], axis=1)
    g_mx = jnp.concatenate([_shards_from_cols(g_in), _shards_from_cols(g_wcat[:, QKV_COLS:f_off]),
                            _shards_from_cols(jnp.concatenate([g_up_a, g_up_b], axis=0))], axis=2)[:, None]
    s_mx, s_wo = reduce_scatter("rs_mixer", [g_mx, g_wo.reshape(N_CHIPS, 1, d // N_CHIPS, d)], c_arr)

    dx0, gn_ffn1, g_w3_1 = ffn_backward("ffn1", dx1, saved1, norm_ffn1, w3_1)
    (s_w3_1,) = reduce_scatter("rs_ffn1", [g_w3_1], c_arr)

    def pack_small(n1, nm, n2, nf, bg, bf, last):
        return jnp.concatenate([n1, nm, n2, nf, bg.reshape(2, d), jnp.pad(bf, ((0, 0), (0, d - n_forget))), last], axis=0)

    zero_row = jnp.zeros((1, d), F32)
    g_small = pack_small(gn_ffn1, gn_mix, gn_ffn2, gn_final, g_bgate, g_bf[:, :n_forget], jnp.pad(loss_part, ((0, 0), (0, d - 1))))
    w_small = pack_small(norm_ffn1, norm_mix, norm_ffn2, norm_final[None], b_gate, b_forget, zero_row)
    m_small = pack_small(m_norm_ffn1, m_norm_mix, m_norm_ffn2, m_norm_final[None], m_b_gate, m_b_forget, zero_row)
    v_small = pack_small(v_norm_ffn1, v_norm_mix, v_norm_ffn2, v_norm_final[None], v_b_gate, v_b_forget, zero_row)
    smalls = small_allreduce_adamw("small_allreduce_adamw", g_small, w_small, m_small, v_small)

    def unpack_small(p):
        return {"norm_ffn1": p[0:1], "norm_mix": p[1:2], "norm_ffn2": p[2:3], "norm_final": p[3], "b_gate": p[4:6].reshape(1, 2 * d),
                "b_forget": p[6:7, :n_forget]}

    loss = smalls[0][7, 0]
    small_out = [unpack_small(p) for p in smalls]

    grads = {
        "w_in": s_mx[0][:, :in4], "w_gate": s_mx[0][:, in4:in4 + gate4],
        "w_up_a": s_mx[0][:WIDTH, in4 + gate4:], "w_up_b": s_mx[0][WIDTH:, in4 + gate4:], "w_out": s_wo[0],
    }
    weights = {"w_in": (w_in, m_w_in, v_w_in), "w_gate": (w_gate, m_w_gate, v_w_gate), "w_up_a": (w_up_a, m_w_up_a, v_w_up_a),
               "w_up_b": (w_up_b, m_w_up_b, v_w_up_b), "w_out": (w_out, m_w_out, v_w_out)}
    big_out = {}
    for wname, (w, m, v) in weights.items():
        g = grads[wname]
        delta, new_m, new_v = adamw(f"adamw_{wname}", w[0], g, m[0], v[0])
        big_out[wname] = (g[None], delta[None], new_m[None], new_v[None])
    for tag, s_w3, ws, ms, vs in (("ffn1", s_w3_1, ffn1_w, ffn1_m, ffn1_v), ("ffn2", s_w3_2, ffn2_w, ffn2_m, ffn2_v)):
        deltas, new_ms, new_vs = adamw_stacked(f"adamw_{tag}", ws, s_w3, ms, vs)
        for which, part in ((GATE, "gate"), (UP, "up"), (DOWN, "down")):
            back = (lambda a: a[None]) if which == DOWN else (lambda a: a.T[None])
            big_out[f"w_{tag}_{part}"] = tuple(back(a) for a in (s_w3[which], deltas[which], new_ms[which], new_vs[which]))

    order = ["norm_ffn1", "w_ffn1_gate", "w_ffn1_up", "w_ffn1_down", "norm_mix", "w_in", "b_forget", "w_gate", "b_gate",
             "w_up_a", "w_up_b", "w_out", "norm_ffn2", "w_ffn2_gate", "w_ffn2_up", "w_ffn2_down", "norm_final"]
    outs = [loss, dx0[None]]
    for kind in range(4):
        for wname in order:
            outs.append(big_out[wname][kind] if wname in big_out else small_out[kind][wname])
    return tuple(outs)
```

```python
import functools

import jax
import jax.numpy as jnp
from jax import lax
from jax.experimental import pallas as pl
from jax.experimental.pallas import tpu as pltpu

F32 = jnp.float32
BF16 = jnp.bfloat16

HEAD_DIM = 64
N_HEADS = 8
WIDTH = N_HEADS * HEAD_DIM
QKV_COLS = 6 * WIDTH
RMS_EPS = 1e-6
ATTN_SCALE = HEAD_DIM ** -0.5
N_CHIPS = 4
QB = 128
BQ = 512
CS = 256
N_SUB = BQ // CS
F_PAD = 256
NEG_BIG = -1e30

ADAM_LR = 0.001
ADAM_B1 = 0.9
ADAM_B2 = 0.999
ADAM_EPS = 1e-08
ADAM_WD = 0.01
ADAM_STEP = 10

VMEM_LIMIT_BYTES = 48 * 1024 * 1024
MESH = pl.DeviceIdType.MESH

NN = ((1,), (0,))
NT = ((1,), (1,))
TN = ((0,), (0,))


def _params(semantics):
    return pltpu.CompilerParams(dimension_semantics=semantics, vmem_limit_bytes=VMEM_LIMIT_BYTES)


def _dot(a, b, contract):
    return lax.dot_general(a.astype(BF16), b.astype(BF16), (contract, ((), ())), preferred_element_type=F32)


def _sigmoid(x):
    return 1.0 / (1.0 + jnp.exp(-x))


def _log1pexp_neg_abs(z):
    return jnp.log(1.0 + jnp.exp(-jnp.abs(z)))


def _split3(x):
    hi = x.astype(BF16)
    r1 = x - hi.astype(F32)
    mid = r1.astype(BF16)
    lo = (r1 - mid.astype(F32)).astype(BF16)
    return hi, mid, lo


def _dot_exact_rhs01(x, m01):
    hi, mid, lo = _split3(x)
    d = lambda p: lax.dot_general(p, m01, (NN, ((), ())), preferred_element_type=F32)
    return d(hi) + d(mid) + d(lo)


def _dot_exact_lhs01(m01, x):
    hi, mid, lo = _split3(x)
    d = lambda p: lax.dot_general(m01, p, (NN, ((), ())), preferred_element_type=F32)
    return d(hi) + d(mid) + d(lo)


def _iota2(shape, dim):
    return lax.broadcasted_iota(jnp.int32, shape, dim)


def _mm(name, pairs, contract, grid, pair_specs, out_shape, out_specs, acc_shape, nk, epilogue,
        extras=(), extra_specs=(), semantics=None):
    n_pairs = len(pairs)
    n_extra = len(extras)
    n_out = len(out_shape)

    def body(*refs):
        ab = refs[:2 * n_pairs]
        ex = refs[2 * n_pairs:2 * n_pairs + n_extra]
        outs = refs[2 * n_pairs + n_extra:2 * n_pairs + n_extra + n_out]
        acc = refs[-1]
        ids = [pl.program_id(i) for i in range(len(grid))]
        k = ids[-1]

        @pl.when(k == 0)
        def _():
            acc[...] = jnp.zeros_like(acc)

        part = _dot(ab[0][...], ab[1][...], contract)
        for p in range(1, n_pairs):
            part += _dot(ab[2 * p][...], ab[2 * p + 1][...], contract)
        acc[...] += part

        @pl.when(k == nk - 1)
        def _():
            epilogue(acc[...], ex, outs, ids)

    operands = [t for pair in pairs for t in pair] + list(extras)
    in_specs = [s for pair in pair_specs for s in pair] + list(extra_specs)
    if semantics is None:
        semantics = ("parallel",) * (len(grid) - 1) + ("arbitrary",)
    return pl.pallas_call(
        body, name=name, grid=grid, in_specs=in_specs, out_specs=list(out_specs), out_shape=list(out_shape),
        scratch_shapes=[pltpu.VMEM(acc_shape, F32)], compiler_params=_params(semantics),
    )(*operands)


def _ordered_after(body, n_in, n_after):
    def wrapped(*refs):
        return body(*refs[:n_in], *refs[n_in + n_after:])
    return wrapped


def _row_tile(rows, target):
    t = min(rows, target)
    while rows % t:
        t //= 2
    return t


def rms_fwd(name, x, g, after=()):
    t, d = x.shape
    tr = _row_tile(t, 256)

    def body(x_ref, g_ref, h_ref, r_ref):
        xv = x_ref[...]
        r = lax.rsqrt(jnp.mean(xv * xv, axis=-1, keepdims=True) + RMS_EPS)
        h_ref[...] = (xv * r * g_ref[...]).astype(BF16)
        r_ref[...] = r

    return pl.pallas_call(
        _ordered_after(body, 2, len(after)), name=name, grid=(t // tr,),
        in_specs=[pl.BlockSpec((tr, d), lambda i: (i, 0)), pl.BlockSpec((1, d), lambda i: (0, 0))] + [_ANY] * len(after),
        out_specs=[pl.BlockSpec((tr, d), lambda i: (i, 0)), pl.BlockSpec((tr, 1), lambda i: (i, 0))],
        out_shape=[jax.ShapeDtypeStruct((t, d), BF16), jax.ShapeDtypeStruct((t, 1), F32)],
        compiler_params=_params(("parallel",)),
    )(x, g, *after)


GATE, UP, DOWN = 0, 1, 2


def _ffn_w_spec(which, f4, d, index_of_j):
    return pl.BlockSpec((None, None, f4, d), lambda *ids: (index_of_j(*ids), which, 0, 0))


def ffn_up(name, h, w3):
    t, d = h.shape
    ns, _, f4, _ = w3.shape
    tm = _row_tile(t, 512)

    def body(h_ref, wg_ref, wu_ref, a_ref, b_ref, s_ref):
        hv = h_ref[...]
        a = _dot(hv, wg_ref[...], NT)
        b = _dot(hv, wu_ref[...], NT)
        a_ref[...] = a.astype(BF16)
        b_ref[...] = b.astype(BF16)
        s_ref[...] = (a * _sigmoid(a) * b).astype(BF16)

    act_spec = pl.BlockSpec((None, tm, f4), lambda j, m: (j, m, 0))
    return pl.pallas_call(
        body, name=name, grid=(ns, t // tm),
        in_specs=[pl.BlockSpec((tm, d), lambda j, m: (m, 0)),
                  _ffn_w_spec(GATE, f4, d, lambda j, m: j), _ffn_w_spec(UP, f4, d, lambda j, m: j)],
        out_specs=[act_spec, act_spec, act_spec],
        out_shape=[jax.ShapeDtypeStruct((ns, t, f4), BF16)] * 3,
        compiler_params=_params(("parallel", "parallel")),
    )(h, w3, w3)


def mm_residual(name, s, w, w_spec, x, scale):
    nj, t, kdim = s.shape
    n = x.shape[1]
    tm = _row_tile(t, 512)

    def epilogue(acc, ex, outs, ids):
        outs[0][...] = ex[0][...] + scale * acc

    return _mm(
        name, [(s, w)], NN, (t // tm, nj),
        [(pl.BlockSpec((None, tm, kdim), lambda m, j: (j, m, 0)), w_spec)],
        [jax.ShapeDtypeStruct((t, n), F32)], [pl.BlockSpec((tm, n), lambda m, j: (m, 0))], (tm, n), nj, epilogue,
        extras=[x], extra_specs=[pl.BlockSpec((tm, n), lambda m, j: (m, 0))],
    )[0]


def ffn_bwd_act(name, dx, w3, a, b, after=()):
    t, d = dx.shape
    ns, _, f4, _ = w3.shape
    tm = _row_tile(t, 512)

    def body(dx_ref, wd_ref, a_ref, b_ref, da_ref, db_ref):
        ds = _dot(0.5 * dx_ref[...], wd_ref[...], NT)
        av = a_ref[...].astype(F32)
        sig = _sigmoid(av)
        da_ref[...] = (ds * b_ref[...].astype(F32) * (sig * (1.0 + av * (1.0 - sig)))).astype(BF16)
        db_ref[...] = (ds * (av * sig)).astype(BF16)

    act_spec = pl.BlockSpec((None, tm, f4), lambda j, m: (j, m, 0))
    return pl.pallas_call(
        _ordered_after(body, 4, len(after)), name=name, grid=(ns, t // tm),
        in_specs=[pl.BlockSpec((tm, d), lambda j, m: (m, 0)), _ffn_w_spec(DOWN, f4, d, lambda j, m: j), act_spec, act_spec]
        + [_ANY] * len(after),
        out_specs=[act_spec, act_spec],
        out_shape=[jax.ShapeDtypeStruct((ns, t, f4), BF16)] * 2,
        compiler_params=_params(("parallel", "parallel")),
    )(dx, w3, a, b, *after)


def ffn_wgrad(name, h, da, db, s, dx):
    t, d = h.shape
    ns, _, f4 = da.shape
    tk = _row_tile(t, 512)
    nk = t // tk

    def body(h_ref, da_ref, db_ref, s_ref, dx_ref, o_ref, acc):
        k = pl.program_id(1)

        @pl.when(k == 0)
        def _():
            acc[...] = jnp.zeros_like(acc)

        hv = h_ref[...]
        acc[GATE] += _dot(da_ref[...], hv, TN)
        acc[UP] += _dot(db_ref[...], hv, TN)
        acc[DOWN] += _dot(s_ref[...], 0.5 * dx_ref[...], TN)

        @pl.when(k == nk - 1)
        def _():
            o_ref[...] = acc[...].astype(BF16)

    act_spec = pl.BlockSpec((None, tk, f4), lambda j, k: (j, k, 0))
    row_spec = pl.BlockSpec((tk, d), lambda j, k: (k, 0))
    return pl.pallas_call(
        body, name=name, grid=(ns, nk),
        in_specs=[row_spec, act_spec, act_spec, act_spec, row_spec],
        out_specs=pl.BlockSpec((None, 3, f4, d), lambda j, k: (j, 0, 0, 0)),
        out_shape=jax.ShapeDtypeStruct((ns, 3, f4, d), BF16),
        scratch_shapes=[pltpu.VMEM((3, f4, d), F32)],
        compiler_params=_params(("parallel", "arbitrary")),
    )(h, da, db, s, dx)


def dh_rms_bwd(name, pairs, pair_specs, contract, grid, nk, x, rstd, g, dx_in):
    t, d = x.shape
    tm = t // grid[0]

    def epilogue(acc, ex, outs, ids):
        x_ref, r_ref, g_ref, dxin_ref = ex
        r = r_ref[...]
        xhat = x_ref[...] * r
        dhg = acc * g_ref[...]
        proj = jnp.mean(dhg * xhat, axis=-1, keepdims=True)
        outs[0][...] = dxin_ref[...] + r * (dhg - xhat * proj)
        part = jnp.sum(acc * xhat, axis=0, keepdims=True)

        @pl.when(ids[0] == 0)
        def _():
            outs[1][...] = part

        @pl.when(ids[0] != 0)
        def _():
            outs[1][...] += part

    row = pl.BlockSpec((tm, d), lambda m, k: (m, 0))
    return _mm(
        name, pairs, contract, grid, pair_specs,
        [jax.ShapeDtypeStruct((t, d), F32), jax.ShapeDtypeStruct((1, d), F32)],
        [row, pl.BlockSpec((1, d), lambda m, k: (0, 0))], (tm, d), nk, epilogue,
        extras=[x, rstd, g, dx_in],
        extra_specs=[row, pl.BlockSpec((tm, 1), lambda m, k: (m, 0)), pl.BlockSpec((1, d), lambda m, k: (0, 0)), row],
        semantics=("arbitrary", "arbitrary"),
    )


def ffn_forward(tag, x, g_norm, w3, after=()):
    _, _, f4, d = w3.shape
    h, rstd = rms_fwd(f"{tag}_rms", x, g_norm, after)
    a, b, s = ffn_up(f"{tag}_up", h, w3)
    x_out = mm_residual(f"{tag}_down", s, w3, _ffn_w_spec(DOWN, f4, d, lambda m, j: j), x, 0.5)
    return x_out, (x, h, rstd, a, b, s)


def ffn_backward(tag, dx, saved, g_norm, w3, after=()):
    x, h, rstd, a, b, s = saved
    t, d = x.shape
    ns, _, f4, _ = w3.shape
    tm = _row_tile(t, 512)
    da, db = ffn_bwd_act(f"{tag}_bwd_act", dx, w3, a, b, after)
    g_w3 = ffn_wgrad(f"{tag}_wgrad", h, da, db, s, dx)
    act_spec = pl.BlockSpec((None, tm, f4), lambda m, j: (j, m, 0))
    w_spec = lambda which: _ffn_w_spec(which, f4, d, lambda m, j: j)
    dx_out, g_n = dh_rms_bwd(f"{tag}_dh", [(da, w3), (db, w3)], [(act_spec, w_spec(GATE)), (act_spec, w_spec(UP))], NN,
                             (t // tm, ns), ns, x, rstd, g_norm, dx)
    return dx_out, g_n, g_w3


def proj_cat(name, h, wcat, bias):
    t, d = h.shape
    n = wcat.shape[1]
    tm = _row_tile(t, 512)
    tn = 768

    def epilogue(acc, ex, outs, ids):
        outs[0][...] = acc + ex[0][...]

    return _mm(
        name, [(h, wcat)], NN, (n // tn, t // tm, 1),
        [(pl.BlockSpec((tm, d), lambda j, m, k: (m, 0)), pl.BlockSpec((d, tn), lambda j, m, k: (0, j)))],
        [jax.ShapeDtypeStruct((t, n), F32)], [pl.BlockSpec((tm, tn), lambda j, m, k: (m, j))], (tm, tn), 1, epilogue,
        extras=[bias], extra_specs=[pl.BlockSpec((1, tn), lambda j, m, k: (0, j))],
    )[0]


def mix_fwd(name, ya, yb, wa, wb, pc):
    t, w = ya.shape
    d = wa.shape[1]
    tm = _row_tile(t, 512)
    tn = 512
    off_a = QKV_COLS // tn
    off_b = (QKV_COLS + d) // tn

    def body(ya_ref, yb_ref, wa_ref, wb_ref, pa_ref, pb_ref, ua_ref, ub_ref, mx_ref):
        ua = _dot(ya_ref[...], wa_ref[...], NN)
        ub = _dot(yb_ref[...], wb_ref[...], NN)
        ua_ref[...] = ua
        ub_ref[...] = ub
        mx_ref[...] = (_sigmoid(pa_ref[...]) * ua + _sigmoid(pb_ref[...]) * ub).astype(BF16)

    y_spec = pl.BlockSpec((tm, w), lambda m, n: (m, 0))
    w_spec = pl.BlockSpec((w, tn), lambda m, n: (0, n))
    o_spec = pl.BlockSpec((tm, tn), lambda m, n: (m, n))
    return pl.pallas_call(
        body, name=name, grid=(t // tm, d // tn),
        in_specs=[y_spec, y_spec, w_spec, w_spec,
                  pl.BlockSpec((tm, tn), lambda m, n: (m, off_a + n)), pl.BlockSpec((tm, tn), lambda m, n: (m, off_b + n))],
        out_specs=[o_spec, o_spec, o_spec],
        out_shape=[jax.ShapeDtypeStruct((t, d), F32), jax.ShapeDtypeStruct((t, d), F32), jax.ShapeDtypeStruct((t, d), BF16)],
        compiler_params=_params(("parallel", "parallel")),
    )(ya, yb, wa, wb, pc, pc)


def mix_bwd(name, dx, wo, pc, ua, ub, after=()):
    t, d = dx.shape
    tm = _row_tile(t, 512)
    tn = 512
    off_a = QKV_COLS // tn
    off_b = (QKV_COLS + d) // tn

    def body(dx_ref, wo_ref, pa_ref, pb_ref, ua_ref, ub_ref, dua_ref, dub_ref, dpa_ref, dpb_ref, ba_ref, bb_ref):
        dm = _dot(dx_ref[...], wo_ref[...], NT)
        ga = _sigmoid(pa_ref[...])
        gb = _sigmoid(pb_ref[...])
        dua_ref[...] = (dm * ga).astype(BF16)
        dub_ref[...] = (dm * gb).astype(BF16)
        dpa = dm * ua_ref[...] * ga * (1.0 - ga)
        dpb = dm * ub_ref[...] * gb * (1.0 - gb)
        dpa_ref[...] = dpa.astype(BF16)
        dpb_ref[...] = dpb.astype(BF16)
        sa = jnp.sum(dpa, axis=0, keepdims=True)
        sb = jnp.sum(dpb, axis=0, keepdims=True)

        @pl.when(pl.program_id(1) == 0)
        def _():
            ba_ref[...] = sa
            bb_ref[...] = sb

        @pl.when(pl.program_id(1) != 0)
        def _():
            ba_ref[...] += sa
            bb_ref[...] += sb

    tile = pl.BlockSpec((tm, tn), lambda n, m: (m, n))
    bias = pl.BlockSpec((1, tn), lambda n, m: (0, n))
    return pl.pallas_call(
        _ordered_after(body, 6, len(after)), name=name, grid=(d // tn, t // tm),
        in_specs=[pl.BlockSpec((tm, d), lambda n, m: (m, 0)), pl.BlockSpec((tn, d), lambda n, m: (n, 0)),
                  pl.BlockSpec((tm, tn), lambda n, m: (m, off_a + n)), pl.BlockSpec((tm, tn), lambda n, m: (m, off_b + n)),
                  tile, tile] + [_ANY] * len(after),
        out_specs=[tile, tile, tile, tile, bias, bias],
        out_shape=[jax.ShapeDtypeStruct((t, d), BF16)] * 4 + [jax.ShapeDtypeStruct((1, d), F32)] * 2,
        compiler_params=_params(("parallel", "arbitrary")),
    )(dx, wo, pc, pc, ua, ub, *after)


def mm_plain(name, a, b, contract, out_dtype, tk_target=512):
    if contract == NN:
        m, kdim = a.shape
        n = b.shape[1]
    elif contract == NT:
        m, kdim = a.shape
        n = b.shape[0]
    else:
        kdim, m = a.shape
        n = b.shape[1]
    tm = _row_tile(m, 512)
    tk = _row_tile(kdim, tk_target)
    nk = kdim // tk
    if contract == TN:
        a_spec = pl.BlockSpec((tk, tm), lambda i, k: (k, i))
    else:
        a_spec = pl.BlockSpec((tm, tk), lambda i, k: (i, k))
    if contract == NT:
        b_spec = pl.BlockSpec((n, tk), lambda i, k: (0, k))
    else:
        b_spec = pl.BlockSpec((tk, n), lambda i, k: (k, 0))

    def epilogue(acc, ex, outs, ids):
        outs[0][...] = acc.astype(out_dtype)

    return _mm(name, [(a, b)], contract, (m // tm, nk), [(a_spec, b_spec)],
               [jax.ShapeDtypeStruct((m, n), out_dtype)], [pl.BlockSpec((tm, n), lambda i, k: (i, 0))], (tm, n), nk, epilogue)[0]


def wgrad_cat(name, h, dcat):
    t, d = h.shape
    n = dcat.shape[1]
    tn = 768
    tk = _row_tile(t, 512)

    def epilogue(acc, ex, outs, ids):
        outs[0][...] = acc.astype(BF16)

    return _mm(
        name, [(h, dcat)], TN, (n // tn, t // tk),
        [(pl.BlockSpec((tk, d), lambda j, k: (k, 0)), pl.BlockSpec((tk, tn), lambda j, k: (k, j)))],
        [jax.ShapeDtypeStruct((d, n), BF16)], [pl.BlockSpec((d, tn), lambda j, k: (0, j))], (d, tn), t // tk, epilogue,
    )[0]


def fox_prep(name, f, bias):
    t, lanes = f.shape
    nchunk = t // QB

    def body(f_ref, b_ref, c_ref):
        lower = (_iota2((QB, QB), 1) <= _iota2((QB, QB), 0)).astype(BF16)

        def chunk(n, carry):
            rows = pl.ds(pl.multiple_of(n * QB, QB), QB)
            u = f_ref[rows, :] + b_ref[...]
            lf = jnp.minimum(u, 0.0) - _log1pexp_neg_abs(u)
            c = _dot_exact_lhs01(lower, lf) + carry
            c_ref[rows, :] = c
            return c[QB - 1:QB, :]

        lax.fori_loop(0, nchunk, chunk, jnp.zeros((1, lanes), F32))

    return pl.pallas_call(body, name=name, out_shape=jax.ShapeDtypeStruct((t, lanes), F32),
                          compiler_params=pltpu.CompilerParams(vmem_limit_bytes=VMEM_LIMIT_BYTES))(f, bias)


def fox_gate_bwd(name, dc, f, bias):
    t, lanes = dc.shape
    nchunk = t // QB

    def body(dc_ref, f_ref, b_ref, df_ref, gb_ref):
        upper = (_iota2((QB, QB), 1) >= _iota2((QB, QB), 0)).astype(BF16)

        def chunk(n, carry):
            tail, total = carry
            rows = pl.ds(pl.multiple_of((nchunk - 1 - n) * QB, QB), QB)
            dlf = _dot_exact_lhs01(upper, dc_ref[rows, :]) + tail
            u = f_ref[rows, :] + b_ref[...]
            df = dlf * jnp.exp(jnp.minimum(-u, 0.0) - _log1pexp_neg_abs(u))
            df_ref[rows, :] = df
            return dlf[0:1, :], total + jnp.sum(df, axis=0, keepdims=True)

        zero = jnp.zeros((1, lanes), F32)
        _, total = lax.fori_loop(0, nchunk, chunk, (zero, zero))
        gb_ref[...] = total

    return pl.pallas_call(body, name=name,
                          out_shape=[jax.ShapeDtypeStruct((t, lanes), F32), jax.ShapeDtypeStruct((1, lanes), F32)],
                          compiler_params=pltpu.CompilerParams(vmem_limit_bytes=VMEM_LIMIT_BYTES))(dc, f, bias)


def _qrows(i):
    return pl.ds(pl.multiple_of(i * BQ, BQ), BQ)


def _krows(kc):
    return pl.ds(pl.multiple_of(kc * CS, CS), CS)


def _head_spec(t, offset):
    return pl.BlockSpec((None, t, HEAD_DIM), lambda h: (offset + h, 0, 0))


def _head_t_spec(nq, offset):
    return pl.BlockSpec((None, nq, HEAD_DIM, BQ), lambda h: (offset + h, 0, 0, 0))


def _chunk_t_spec(nc):
    return pl.BlockSpec((None, nc, HEAD_DIM, CS), lambda h: (h, 0, 0, 0))


def _dot_split2_rhs01(x, m01):
    hi = x.astype(BF16)
    lo = (x - hi.astype(F32)).astype(BF16)
    d = lambda p: lax.dot_general(p, m01, (NN, ((), ())), preferred_element_type=F32)
    return d(hi) + d(lo)


def _diag_mask(dchunk, inclusive):
    r_io = _iota2((BQ, CS), 0)
    c_io = _iota2((BQ, CS), 1) + dchunk * CS
    return c_io <= r_io if inclusive else c_io < r_io


def _walk_chunks(i, step, init, right_to_left):
    order = list(reversed(range(N_SUB))) if right_to_left else list(range(N_SUB))

    def diagonal(state):
        for dchunk in order:
            state = step(i * N_SUB + dchunk, state, dchunk)
        return state

    def group(n, state):
        base = ((i - 1 - n) if right_to_left else n) * N_SUB
        for dchunk in order:
            state = step(base + dchunk, state, None)
        return state

    if right_to_left:
        return lax.fori_loop(0, i, group, diagonal(init))
    return diagonal(lax.fori_loop(0, i, group, init))


def sb_fwd(name, qkv):
    t = qkv.shape[1]

    def body(q_ref, k_ref, v_ref, o_ref):
        later = (_iota2((CS, CS), 0) > _iota2((CS, CS), 1)).astype(BF16)

        def qblock(i, _):
            q = q_ref[_qrows(i), :]

            def step(kc, state, dchunk):
                carry, acc = state
                z = _dot(q, k_ref[_krows(kc), :], NT)
                sp = _log1pexp_neg_abs(z)
                lnb = -jnp.maximum(z, 0.0) - sp
                if dchunk is not None:
                    lnb = jnp.where(_diag_mask(dchunk, False), lnb, 0.0)
                w = jnp.exp(jnp.minimum(z, 0.0) - sp + _dot_split2_rhs01(lnb, later) + carry)
                if dchunk is not None:
                    w = jnp.where(_diag_mask(dchunk, False), w, 0.0)
                acc = acc + _dot(w, v_ref[_krows(kc), :], NN)
                return carry + jnp.sum(lnb, axis=1, keepdims=True), acc

            init = (jnp.zeros((BQ, 1), F32), jnp.zeros((BQ, HEAD_DIM), F32))
            _, acc = _walk_chunks(i, step, init, True)
            o_ref[_qrows(i), :] = acc.astype(BF16)
            return 0

        lax.fori_loop(0, t // BQ, qblock, 0)

    return pl.pallas_call(
        body, name=name, grid=(N_HEADS,),
        in_specs=[_head_spec(t, 0), _head_spec(t, N_HEADS), _head_spec(t, 2 * N_HEADS)],
        out_specs=_head_spec(t, 0), out_shape=jax.ShapeDtypeStruct((N_HEADS, t, HEAD_DIM), BF16),
        compiler_params=_params(("parallel",)),
    )(qkv, qkv, qkv)


def sb_bwd(name, qkv, qt, dy, dyt):
    t = qkv.shape[1]
    nq, nc = t // BQ, t // CS

    def body(q_ref, k_ref, v_ref, qt_ref, do_ref, dot_ref, dq_ref, dkt_ref, dvt_ref, g_s, b_s, dkt_acc, dvt_acc):
        later = (_iota2((CS, CS), 0) > _iota2((CS, CS), 1)).astype(BF16)
        earlier = (_iota2((CS, CS), 0) < _iota2((CS, CS), 1)).astype(BF16)
        dkt_acc[...] = jnp.zeros_like(dkt_acc)
        dvt_acc[...] = jnp.zeros_like(dvt_acc)

        def qblock(i, _):
            q = q_ref[_qrows(i), :]
            do = do_ref[_qrows(i), :]
            q_t = qt_ref[i]
            do_t = dot_ref[i]

            def step1(kc, carry, dchunk):
                z = _dot(q, k_ref[_krows(kc), :], NT)
                sp = _log1pexp_neg_abs(z)
                lnb = -jnp.maximum(z, 0.0) - sp
                lsz = jnp.minimum(z, 0.0) - sp
                if dchunk is not None:
                    lnb = jnp.where(_diag_mask(dchunk, False), lnb, 0.0)
                w = jnp.exp(lsz + _dot_split2_rhs01(lnb, later) + carry)
                if dchunk is not None:
                    w = jnp.where(_diag_mask(dchunk, False), w, 0.0)
                g_s[kc] = w * _dot(do, v_ref[_krows(kc), :], NT)
                b_s[kc] = jnp.exp(lsz)
                dvt_acc[kc] += _dot(do_t, w, NN)
                return carry + jnp.sum(lnb, axis=1, keepdims=True)

            _walk_chunks(i, step1, jnp.zeros((BQ, 1), F32), True)

            def step2(kc, state, dchunk):
                before, dq = state
                g = g_s[kc]
                beta = b_s[kc]
                dz = g * (1.0 - beta) - beta * (_dot_split2_rhs01(g, earlier) + before)
                if dchunk is not None:
                    dz = jnp.where(_diag_mask(dchunk, False), dz, 0.0)
                dzb = dz.astype(BF16)
                dq = dq + _dot(dzb, k_ref[_krows(kc), :], NN)
                dkt_acc[kc] += _dot(q_t, dzb, NN)
                return before + jnp.sum(g, axis=1, keepdims=True), dq

            _, dq = _walk_chunks(i, step2, (jnp.zeros((BQ, 1), F32), jnp.zeros((BQ, HEAD_DIM), F32)), False)
            dq_ref[_qrows(i), :] = (dq * ATTN_SCALE).astype(BF16)
            return 0

        lax.fori_loop(0, nq, qblock, 0)
        dkt_ref[...] = dkt_acc[...].astype(BF16)
        dvt_ref[...] = dvt_acc[...].astype(BF16)

    chunked = jax.ShapeDtypeStruct((N_HEADS, nc, HEAD_DIM, CS), BF16)
    return pl.pallas_call(
        body, name=name, grid=(N_HEADS,),
        in_specs=[_head_spec(t, 0), _head_spec(t, N_HEADS), _head_spec(t, 2 * N_HEADS), _head_t_spec(nq, 0),
                  _head_spec(t, 0), _head_t_spec(nq, 0)],
        out_specs=[_head_spec(t, 0), _chunk_t_spec(nc), _chunk_t_spec(nc)],
        out_shape=[jax.ShapeDtypeStruct((N_HEADS, t, HEAD_DIM), BF16), chunked, chunked],
        scratch_shapes=[pltpu.VMEM((nc, BQ, CS), F32), pltpu.VMEM((nc, BQ, CS), F32),
                        pltpu.VMEM((nc, HEAD_DIM, CS), F32), pltpu.VMEM((nc, HEAD_DIM, CS), F32)],
        compiler_params=_params(("parallel",)),
    )(qkv, qkv, qkv, qt, dy, dyt)


def _col_spec(t):
    return pl.BlockSpec((None, t, 1), lambda h: (h, 0, 0))


def _row_spec(nc):
    return pl.BlockSpec((None, nc, 1, CS), lambda h: (h, 0, 0, 0))


def fox_fwd(name, qkv, c_col, c_row):
    t = qkv.shape[1]

    def body(q_ref, k_ref, v_ref, cc_ref, cr_ref, o_ref, lse_ref):
        def qblock(i, _):
            q = q_ref[_qrows(i), :]
            ct = cc_ref[_qrows(i), :]

            def step(kc, state, dchunk):
                m, l, acc = state
                s = _dot(q, k_ref[_krows(kc), :], NT) + ct - cr_ref[kc]
                if dchunk is not None:
                    s = jnp.where(_diag_mask(dchunk, True), s, NEG_BIG)
                m_new = jnp.maximum(m, jnp.max(s, axis=1, keepdims=True))
                alpha = jnp.exp(m - m_new)
                p = jnp.exp(s - m_new)
                if dchunk is not None:
                    p = jnp.where(_diag_mask(dchunk, True), p, 0.0)
                l = l * alpha + jnp.sum(p, axis=1, keepdims=True)
                acc = acc * alpha + _dot(p, v_ref[_krows(kc), :], NN)
                return m_new, l, acc

            init = (jnp.full((BQ, 1), NEG_BIG, F32), jnp.zeros((BQ, 1), F32), jnp.zeros((BQ, HEAD_DIM), F32))
            m, l, acc = _walk_chunks(i, step, init, False)
            o_ref[_qrows(i), :] = (acc / l).astype(BF16)
            lse_ref[_qrows(i), :] = m + jnp.log(l)
            return 0

        lax.fori_loop(0, t // BQ, qblock, 0)

    return pl.pallas_call(
        body, name=name, grid=(N_HEADS,),
        in_specs=[_head_spec(t, 3 * N_HEADS), _head_spec(t, 4 * N_HEADS), _head_spec(t, 5 * N_HEADS),
                  _col_spec(t), _row_spec(t // CS)],
        out_specs=[_head_spec(t, 0), _col_spec(t)],
        out_shape=[jax.ShapeDtypeStruct((N_HEADS, t, HEAD_DIM), BF16), jax.ShapeDtypeStruct((N_HEADS, t, 1), F32)],
        compiler_params=_params(("parallel",)),
    )(qkv, qkv, qkv, c_col, c_row)


def fox_bwd(name, qkv, qt, y, dy, dyt, lse, c_col, c_row):
    t = qkv.shape[1]
    nq, nc = t // BQ, t // CS

    def body(q_ref, k_ref, v_ref, qt_ref, o_ref, do_ref, dot_ref, lse_ref, cc_ref, cr_ref,
             dq_ref, dkt_ref, dvt_ref, dcc_ref, dcr_ref, dkt_acc, dvt_acc, dcr_acc):
        dkt_acc[...] = jnp.zeros_like(dkt_acc)
        dvt_acc[...] = jnp.zeros_like(dvt_acc)
        dcr_acc[...] = jnp.zeros_like(dcr_acc)

        def qblock(i, _):
            q = q_ref[_qrows(i), :]
            do = do_ref[_qrows(i), :]
            q_t = qt_ref[i]
            do_t = dot_ref[i]
            ct = cc_ref[_qrows(i), :]
            lse_i = lse_ref[_qrows(i), :]
            delta = jnp.sum(do.astype(F32) * o_ref[_qrows(i), :].astype(F32), axis=1, keepdims=True)

            def step(kc, state, dchunk):
                dq, dct = state
                s = _dot(q, k_ref[_krows(kc), :], NT) + ct - cr_ref[kc]
                p = jnp.exp(s - lse_i)
                if dchunk is not None:
                    p = jnp.where(_diag_mask(dchunk, True), p, 0.0)
                ds = p * (_dot(do, v_ref[_krows(kc), :], NT) - delta)
                dvt_acc[kc] += _dot(do_t, p, NN)
                dsb = ds.astype(BF16)
                dq = dq + _dot(dsb, k_ref[_krows(kc), :], NN)
                dkt_acc[kc] += _dot(q_t, dsb, NN)
                dcr_acc[kc] -= jnp.sum(ds, axis=0, keepdims=True)
                return dq, dct + jnp.sum(ds, axis=1, keepdims=True)

            dq, dct = _walk_chunks(i, step, (jnp.zeros((BQ, HEAD_DIM), F32), jnp.zeros((BQ, 1), F32)), False)
            dq_ref[_qrows(i), :] = (dq * ATTN_SCALE).astype(BF16)
            dcc_ref[_qrows(i), :] = dct
            return 0

        lax.fori_loop(0, nq, qblock, 0)
        dkt_ref[...] = dkt_acc[...].astype(BF16)
        dvt_ref[...] = dvt_acc[...].astype(BF16)
        dcr_ref[...] = dcr_acc[...]

    chunked = jax.ShapeDtypeStruct((N_HEADS, nc, HEAD_DIM, CS), BF16)
    return pl.pallas_call(
        body, name=name, grid=(N_HEADS,),
        in_specs=[_head_spec(t, 3 * N_HEADS), _head_spec(t, 4 * N_HEADS), _head_spec(t, 5 * N_HEADS), _head_t_spec(nq, N_HEADS),
                  _head_spec(t, 0), _head_spec(t, 0), _head_t_spec(nq, 0), _col_spec(t), _col_spec(t), _row_spec(nc)],
        out_specs=[_head_spec(t, 0), _chunk_t_spec(nc), _chunk_t_spec(nc), _col_spec(t), _row_spec(nc)],
        out_shape=[jax.ShapeDtypeStruct((N_HEADS, t, HEAD_DIM), BF16), chunked, chunked,
                   jax.ShapeDtypeStruct((N_HEADS, t, 1), F32), jax.ShapeDtypeStruct((N_HEADS, nc, 1, CS), F32)],
        scratch_shapes=[pltpu.VMEM((nc, HEAD_DIM, CS), F32), pltpu.VMEM((nc, HEAD_DIM, CS), F32), pltpu.VMEM((nc, 1, CS), F32)],
        compiler_params=_params(("parallel",)),
    )(qkv, qkv, qkv, qt, y, dy, dyt, lse, c_col, c_row)


def loss_head(name, x, g, target):
    t, d = x.shape
    tr = _row_tile(t, 256)

    def body(x_ref, g_ref, t_ref, dx_ref, gn_ref, loss_ref):
        xv = x_ref[...]
        r = lax.rsqrt(jnp.mean(xv * xv, axis=-1, keepdims=True) + RMS_EPS)
        xhat = xv * r
        gv = g_ref[...]
        err = xhat * gv - t_ref[...]
        part_loss = 0.5 * jnp.sum(jnp.mean(err * err, axis=-1, keepdims=True), axis=0, keepdims=True)
        dy = err * (1.0 / d)
        dyg = dy * gv
        dx_ref[...] = r * (dyg - xhat * jnp.mean(dyg * xhat, axis=-1, keepdims=True))
        part_g = jnp.sum(dy * xhat, axis=0, keepdims=True)

        @pl.when(pl.program_id(0) == 0)
        def _():
            gn_ref[...] = part_g
            loss_ref[...] = part_loss

        @pl.when(pl.program_id(0) != 0)
        def _():
            gn_ref[...] += part_g
            loss_ref[...] += part_loss

    row = pl.BlockSpec((tr, d), lambda i: (i, 0))
    return pl.pallas_call(
        body, name=name, grid=(t // tr,),
        in_specs=[row, pl.BlockSpec((1, d), lambda i: (0, 0)), row],
        out_specs=[row, pl.BlockSpec((1, d), lambda i: (0, 0)), pl.BlockSpec((1, 1), lambda i: (0, 0))],
        out_shape=[jax.ShapeDtypeStruct((t, d), F32), jax.ShapeDtypeStruct((1, d), F32), jax.ShapeDtypeStruct((1, 1), F32)],
        compiler_params=_params(("arbitrary",)),
    )(x, g, target)


def _place():
    return lax.axis_index("x"), lax.axis_index("y"), lax.axis_index("c")


def _other_chips(x, y):
    return [(1 - x, y), (x, 1 - y), (1 - x, 1 - y)]


def _half(ref, c, rows):
    return ref.at[:, pl.ds(c * (rows // 2), rows // 2), :]


_ANY = pl.BlockSpec(memory_space=pl.ANY)


def gather_weights(name, bufs):
    n = len(bufs)

    def body(*refs):
        outs = refs[n:2 * n]
        send_sems, recv_sems = refs[2 * n:]
        x, y, c = _place()
        chips = _other_chips(x, y)
        me = 2 * x + y
        sibling = (x, y, 1 - c)
        first, passed = [], []
        for i in range(n):
            rows = outs[i].shape[2]
            mine = _half(outs[i].at[me], c, rows)
            for j, (qx, qy) in enumerate(chips):
                k = 6 * i + j
                rc = pltpu.make_async_remote_copy(
                    src_ref=mine, dst_ref=mine,
                    send_sem=send_sems.at[k], recv_sem=recv_sems.at[k], device_id=(qx, qy, c), device_id_type=MESH)
                rc.start()
                first.append(rc)
        for i in range(n):
            rows = outs[i].shape[2]
            for j, (qx, qy) in enumerate(chips):
                k = 6 * i + j
                block = _half(outs[i].at[2 * qx + qy], c, rows)
                pltpu.make_async_remote_copy(
                    src_ref=block, dst_ref=block, send_sem=send_sems.at[k], recv_sem=recv_sems.at[k],
                    device_id=(qx, qy, c), device_id_type=MESH).wait_recv()
                fw = pltpu.make_async_remote_copy(
                    src_ref=block, dst_ref=block, send_sem=send_sems.at[k + 3], recv_sem=recv_sems.at[k + 3],
                    device_id=sibling, device_id_type=MESH)
                fw.start()
                passed.append(fw)
        for i in range(n):
            rows = outs[i].shape[2]
            for j, (qx, qy) in enumerate(chips):
                k = 6 * i + j + 3
                block = _half(outs[i].at[2 * qx + qy], 1 - c, rows)
                pltpu.make_async_remote_copy(
                    src_ref=block, dst_ref=block, send_sem=send_sems.at[k], recv_sem=recv_sems.at[k],
                    device_id=sibling, device_id_type=MESH).wait_recv()
        for cp in first + passed:
            cp.wait_send()

    return pl.pallas_call(
        body, name=name, in_specs=[_ANY] * n, out_specs=[_ANY] * n,
        out_shape=[jax.ShapeDtypeStruct(b.shape, b.dtype) for b in bufs],
        input_output_aliases={i: i for i in range(n)},
        scratch_shapes=[pltpu.SemaphoreType.DMA((6 * n,)), pltpu.SemaphoreType.DMA((6 * n,))],
        compiler_params=pltpu.CompilerParams(has_side_effects=True),
    )(*bufs)


_HBM = pl.BlockSpec(memory_space=pltpu.HBM)
_SEM = pl.BlockSpec(memory_space=pltpu.SEMAPHORE)
_DATAFLOW = pltpu.SideEffectType.DATAFLOW_SIDE_EFFECTING


def _in_hbm(a):
    return pltpu.with_memory_space_constraint(a, pltpu.HBM)


def _gather_ici_copies(bufs, send_sems, recv_sems, arrivals):
    x, y, c = _place()
    me = 2 * x + y
    copies = []
    for i, buf in enumerate(bufs):
        rows = buf.shape[2]
        for j, (qx, qy) in enumerate(_other_chips(x, y)):
            block = _half(buf.at[2 * qx + qy if arrivals else me], c, rows)
            copies.append(pltpu.make_async_remote_copy(
                src_ref=block, dst_ref=block, send_sem=send_sems.at[3 * i + j], recv_sem=recv_sems.at[3 * i + j],
                device_id=(qx, qy, c), device_id_type=MESH))
    return copies


def gather_ici_start(name, bufs, after):
    n = len(bufs)

    def body(*refs):
        ins = refs[:n]
        send_sems, recv_sems = refs[n + 1], refs[n + 2]
        token = refs[-1]
        for send in _gather_ici_copies(ins, send_sems, recv_sems, False):
            send.start()
        token[...] = jnp.zeros_like(token)

    res = pl.pallas_call(
        body, name=name,
        out_shape=(pltpu.SemaphoreType.DMA((3 * n,)), pltpu.SemaphoreType.DMA((3 * n,)), *[pltpu.HBM(b.shape, b.dtype) for b in bufs],
                   jax.ShapeDtypeStruct((8, 128), F32)),
        in_specs=[_HBM] * n + [_ANY], out_specs=(_SEM, _SEM, *[_HBM] * n, pl.BlockSpec(memory_space=pltpu.VMEM)),
        input_output_aliases={i: 2 + i for i in range(n)},
        compiler_params=pltpu.CompilerParams(has_side_effects=_DATAFLOW),
    )(*[_in_hbm(b) for b in bufs], after)
    return res[0], res[1], list(res[2:2 + n]), res[-1]


def gather_ici_wait(name, send_sems, recv_sems, bufs, after):
    n = len(bufs)

    def body(*refs):
        ins = refs[:n]
        send_sems_ref, recv_sems_ref = refs[n], refs[n + 1]
        for send in _gather_ici_copies(ins, send_sems_ref, recv_sems_ref, False):
            send.wait_send()
        for recv in _gather_ici_copies(ins, send_sems_ref, recv_sems_ref, True):
            recv.wait_recv()

    return pl.pallas_call(
        body, name=name, out_shape=tuple(pltpu.HBM(b.shape, b.dtype) for b in bufs),
        in_specs=[_HBM] * n + [_SEM, _SEM, _ANY], out_specs=tuple([_HBM] * n),
        input_output_aliases={i: i for i in range(n)},
        compiler_params=pltpu.CompilerParams(has_side_effects=_DATAFLOW),
    )(*bufs, send_sems, recv_sems, after)


def gather_forward(name, bufs):
    n = len(bufs)

    def body(*refs):
        outs = refs[n:2 * n]
        send_sems, recv_sems = refs[2 * n:]
        x, y, c = _place()
        sibling = (x, y, 1 - c)
        sends = []
        for i in range(n):
            rows = outs[i].shape[2]
            for j, (qx, qy) in enumerate(_other_chips(x, y)):
                block = _half(outs[i].at[2 * qx + qy], c, rows)
                fw = pltpu.make_async_remote_copy(
                    src_ref=block, dst_ref=block, send_sem=send_sems.at[3 * i + j], recv_sem=recv_sems.at[3 * i + j],
                    device_id=sibling, device_id_type=MESH)
                fw.start()
                sends.append(fw)
        for i in range(n):
            rows = outs[i].shape[2]
            for j, (qx, qy) in enumerate(_other_chips(x, y)):
                block = _half(outs[i].at[2 * qx + qy], 1 - c, rows)
                pltpu.make_async_remote_copy(
                    src_ref=block, dst_ref=block, send_sem=send_sems.at[3 * i + j], recv_sem=recv_sems.at[3 * i + j],
                    device_id=sibling, device_id_type=MESH).wait_recv()
        for fw in sends:
            fw.wait_send()

    return pl.pallas_call(
        body, name=name, in_specs=[_ANY] * n, out_specs=[_ANY] * n,
        out_shape=[jax.ShapeDtypeStruct(b.shape, b.dtype) for b in bufs],
        input_output_aliases={i: i for i in range(n)},
        scratch_shapes=[pltpu.SemaphoreType.DMA((3 * n,)), pltpu.SemaphoreType.DMA((3 * n,))],
        compiler_params=pltpu.CompilerParams(has_side_effects=True),
    )(*bufs)


def _between_chips_copies(parts, lands, send_sems, recv_sems):
    x, y, c = _place()
    copies = []
    for i, (part, land) in enumerate(zip(parts, lands)):
        for j, (qx, qy) in enumerate(_other_chips(x, y)):
            copies.append(pltpu.make_async_remote_copy(
                src_ref=part.at[2 * qx + qy], dst_ref=land.at[j], send_sem=send_sems.at[3 * i + j], recv_sem=recv_sems.at[3 * i + j],
                device_id=(qx, qy, c), device_id_type=MESH))
    return copies


def between_chips_start(name, parts):
    n = len(parts)
    lands = [lax.empty((N_CHIPS - 1,) + p.shape[1:], p.dtype) for p in parts]

    def body(*refs):
        send_sems, recv_sems = refs[2 * n], refs[2 * n + 1]
        token = refs[-1]
        for cp in _between_chips_copies(refs[:n], refs[n:2 * n], send_sems, recv_sems):
            cp.start()
        token[...] = jnp.zeros_like(token)

    res = pl.pallas_call(
        body, name=name,
        out_shape=(pltpu.SemaphoreType.DMA((3 * n,)), pltpu.SemaphoreType.DMA((3 * n,)),
                   *[pltpu.HBM(a.shape, a.dtype) for a in parts + lands], jax.ShapeDtypeStruct((8, 128), F32)),
        in_specs=[_HBM] * (2 * n), out_specs=(_SEM, _SEM, *[_HBM] * (2 * n), pl.BlockSpec(memory_space=pltpu.VMEM)),
        input_output_aliases={i: 2 + i for i in range(2 * n)},
        compiler_params=pltpu.CompilerParams(has_side_effects=_DATAFLOW),
    )(*[_in_hbm(a) for a in parts + lands])
    return res[0], res[1], list(res[2:2 + n]), list(res[2 + n:2 + 2 * n]), res[-1]


def between_chips_wait(name, send_sems, recv_sems, parts, lands, after):
    n = len(parts)

    def body(*refs):
        for cp in _between_chips_copies(refs[:n], refs[n:2 * n], refs[2 * n], refs[2 * n + 1]):
            cp.wait_send()
            cp.wait_recv()

    res = pl.pallas_call(
        body, name=name, out_shape=tuple(pltpu.HBM(a.shape, a.dtype) for a in parts + lands),
        in_specs=[_HBM] * (2 * n) + [_SEM, _SEM, _ANY], out_specs=tuple([_HBM] * (2 * n)),
        input_output_aliases={i: i for i in range(2 * n)},
        compiler_params=pltpu.CompilerParams(has_side_effects=_DATAFLOW),
    )(*parts, *lands, send_sems, recv_sems, after)
    return list(res[:n]), list(res[n:])


def send_half_to_sibling(name, grads):
    n = len(grads)

    def body(*refs):
        srcs, outs = refs[:n], refs[n:2 * n]
        send_sems, recv_sems = refs[2 * n:]
        x, y, c = _place()
        sibling = (x, y, 1 - c)
        copies = []
        for i in range(n):
            rows = srcs[i].shape[2]
            rc = pltpu.make_async_remote_copy(
                src_ref=srcs[i].at[:, :, pl.ds((1 - c) * (rows // 2), rows // 2), :], dst_ref=outs[i],
                send_sem=send_sems.at[i], recv_sem=recv_sems.at[i], device_id=sibling, device_id_type=MESH)
            rc.start()
            copies.append(rc)
        for rc in copies:
            rc.wait()

    def half_shape(g):
        s = g.shape
        return jax.ShapeDtypeStruct((s[0], s[1], s[2] // 2, s[3]), g.dtype)

    return pl.pallas_call(
        body, name=name, in_specs=[_ANY] * n, out_specs=[_ANY] * n, out_shape=[half_shape(g) for g in grads],
        scratch_shapes=[pltpu.SemaphoreType.DMA((n,)), pltpu.SemaphoreType.DMA((n,))],
        compiler_params=pltpu.CompilerParams(has_side_effects=True),
    )(*grads)


def exchange_between_chips(name, parts):
    n = len(parts)

    def body(*refs):
        srcs, outs = refs[:n], refs[n:2 * n]
        send_sems, recv_sems = refs[2 * n:]
        x, y, c = _place()
        chips = _other_chips(x, y)
        copies = []
        for i in range(n):
            for j, (qx, qy) in enumerate(chips):
                k = 3 * i + j
                rc = pltpu.make_async_remote_copy(
                    src_ref=srcs[i].at[2 * qx + qy], dst_ref=outs[i].at[j],
                    send_sem=send_sems.at[k], recv_sem=recv_sems.at[k], device_id=(qx, qy, c), device_id_type=MESH)
                rc.start()
                copies.append(rc)
        for rc in copies:
            rc.wait()

    return pl.pallas_call(
        body, name=name, in_specs=[_ANY] * n, out_specs=[_ANY] * n,
        out_shape=[jax.ShapeDtypeStruct((N_CHIPS - 1,) + p.shape[1:], p.dtype) for p in parts],
        scratch_shapes=[pltpu.SemaphoreType.DMA((3 * n,)), pltpu.SemaphoreType.DMA((3 * n,))],
        compiler_params=pltpu.CompilerParams(has_side_effects=True),
    )(*parts)


def share_halves(name, bufs):
    n = len(bufs)

    def body(*refs):
        outs = refs[n:2 * n]
        send_sems, recv_sems = refs[2 * n:]
        x, y, c = _place()
        copies = []
        for i in range(n):
            mine = _half(outs[i], c, outs[i].shape[1])
            rc = pltpu.make_async_remote_copy(
                src_ref=mine, dst_ref=mine, send_sem=send_sems.at[i], recv_sem=recv_sems.at[i],
                device_id=(x, y, 1 - c), device_id_type=MESH)
            rc.start()
            copies.append(rc)
        for i in range(n):
            theirs = _half(outs[i], 1 - c, outs[i].shape[1])
            pltpu.make_async_remote_copy(
                src_ref=theirs, dst_ref=theirs, send_sem=send_sems.at[i], recv_sem=recv_sems.at[i],
                device_id=(x, y, 1 - c), device_id_type=MESH).wait_recv()
        for rc in copies:
            rc.wait_send()

    return pl.pallas_call(
        body, name=name, in_specs=[_ANY] * n, out_specs=[_ANY] * n,
        out_shape=[jax.ShapeDtypeStruct(b.shape, b.dtype) for b in bufs],
        input_output_aliases={i: i for i in range(n)},
        scratch_shapes=[pltpu.SemaphoreType.DMA((n,)), pltpu.SemaphoreType.DMA((n,))],
        compiler_params=pltpu.CompilerParams(has_side_effects=True),
    )(*bufs)


def pair_sum(name, grad, recv, c):
    ns, na, rh, cols = recv.shape
    tr = _row_tile(rh, 256) if rh % 256 == 0 else rh
    nt = rh // tr

    def body(c_ref, g_ref, r_ref, o_ref):
        o_ref[...] = (g_ref[...].astype(F32) + r_ref[...].astype(F32)).astype(BF16)

    blk = (None, None, tr, cols)
    return pl.pallas_call(
        body, name=name,
        grid_spec=pltpu.PrefetchScalarGridSpec(
            num_scalar_prefetch=1, grid=(ns, na, nt),
            in_specs=[pl.BlockSpec(blk, lambda s, a, r, c_ref: (s, a, c_ref[0] * nt + r, 0)),
                      pl.BlockSpec(blk, lambda s, a, r, c_ref: (s, a, r, 0))],
            out_specs=pl.BlockSpec(blk, lambda s, a, r, c_ref: (s, a, r, 0))),
        out_shape=jax.ShapeDtypeStruct(recv.shape, BF16),
        compiler_params=_params(("parallel", "parallel", "parallel")),
    )(c, grad, recv)


def chip_sum(name, parts, landed, place):
    _, na, rh, cols = parts.shape
    tr = _row_tile(rh, 256) if rh % 256 == 0 else rh
    nt = rh // tr

    def body(place_ref, p_ref, l_ref, o_ref):
        total = p_ref[...].astype(F32)
        for s in range(N_CHIPS - 1):
            total = total + l_ref[s].astype(F32)
        o_ref[...] = total

    return pl.pallas_call(
        body, name=name,
        grid_spec=pltpu.PrefetchScalarGridSpec(
            num_scalar_prefetch=1, grid=(na, nt),
            in_specs=[pl.BlockSpec((None, None, tr, cols), lambda a, r, pr: (pr[1], a, r, 0)),
                      pl.BlockSpec((N_CHIPS - 1, None, tr, cols), lambda a, r, pr: (0, a, r, 0))],
            out_specs=pl.BlockSpec((None, tr, cols), lambda a, r, pr: (a, pr[0] * nt + r, 0))),
        out_shape=jax.ShapeDtypeStruct((na, 2 * rh, cols), F32),
        compiler_params=_params(("parallel", "parallel")),
    )(place, parts, landed)


def reduce_scatter_begin(tag, grads, place):
    recv = send_half_to_sibling(f"{tag}_to_sibling", grads)
    parts = [pair_sum(f"{tag}_pair_sum{i}", g, r, place) for i, (g, r) in enumerate(zip(grads, recv))]
    send_sems, recv_sems, parts, lands, token = between_chips_start(f"{tag}_between_chips_start", parts)
    return (send_sems, recv_sems, parts, lands), token


def reduce_scatter_end(tag, state, place, after):
    send_sems, recv_sems, parts, lands = state
    parts, landed = between_chips_wait(f"{tag}_between_chips_wait", send_sems, recv_sems, parts, lands, after)
    halves = [chip_sum(f"{tag}_chip_sum{i}", p, l, place) for i, (p, l) in enumerate(zip(parts, landed))]
    return share_halves(f"{tag}_share", halves)


def _adamw_math(w, g, m, v):
    m = ADAM_B1 * m + (1.0 - ADAM_B1) * g
    v = ADAM_B2 * v + (1.0 - ADAM_B2) * (g * g)
    m_hat = m / (1.0 - ADAM_B1 ** ADAM_STEP)
    v_hat = v / (1.0 - ADAM_B2 ** ADAM_STEP)
    delta = -ADAM_LR * (m_hat / (jnp.sqrt(v_hat) + ADAM_EPS) + ADAM_WD * w)
    return delta, m, v


def adamw(name, w, g, m, v, after=()):
    rows, cols = w.shape
    tr = _row_tile(rows, 256) if rows % 256 == 0 else rows // 2

    def body(w_ref, g_ref, m_ref, v_ref, d_ref, mo_ref, vo_ref):
        d_ref[...], mo_ref[...], vo_ref[...] = _adamw_math(w_ref[...], g_ref[...], m_ref[...], v_ref[...])

    blk = pl.BlockSpec((tr, cols), lambda i: (i, 0))
    return pl.pallas_call(
        _ordered_after(body, 4, len(after)), name=name, grid=(rows // tr,), in_specs=[blk] * 4 + [_ANY] * len(after),
        out_specs=[blk] * 3, out_shape=[jax.ShapeDtypeStruct(w.shape, F32)] * 3, compiler_params=_params(("parallel",)),
    )(w, g, m, v, *after)


def adamw_stacked(name, ws, g, ms, vs, after=()):
    n = len(ws)
    rows, cols = ws[0].shape
    tr = next(r for r in (128, 88, 64, 32, 16, 8) if rows % r == 0)

    def body(*refs):
        w_refs, m_refs, v_refs, g_ref = refs[:n], refs[n:2 * n], refs[2 * n:3 * n], refs[3 * n]
        outs = refs[3 * n + 1:]
        for i in range(n):
            outs[i][...], outs[n + i][...], outs[2 * n + i][...] = _adamw_math(
                w_refs[i][...], g_ref[i], m_refs[i][...], v_refs[i][...])

    blk = pl.BlockSpec((tr, cols), lambda r: (r, 0))
    res = pl.pallas_call(
        _ordered_after(body, 3 * n + 1, len(after)), name=name, grid=(rows // tr,),
        in_specs=[blk] * (3 * n) + [pl.BlockSpec((n, tr, cols), lambda r: (0, r, 0))] + [_ANY] * len(after),
        out_specs=[blk] * (3 * n),
        out_shape=[jax.ShapeDtypeStruct((rows, cols), F32)] * (3 * n), compiler_params=_params(("parallel",)),
    )(*ws, *ms, *vs, g, *after)
    return res[:n], res[n:2 * n], res[2 * n:]


def small_allreduce_adamw(name, g_part, w, m, v):
    rows, cols = g_part.shape

    def body(g_ref, w_ref, m_ref, v_ref, sum_ref, d_ref, mo_ref, vo_ref, land, send_sems, recv_sems):
        x, y, c = _place()
        me = 4 * x + 2 * y + c
        land[me] = g_ref[...]
        copies = []
        for r in range(1, 8):
            peer = (x ^ (r >> 2), y ^ ((r >> 1) & 1), c ^ (r & 1))
            rc = pltpu.make_async_remote_copy(
                src_ref=g_ref, dst_ref=land.at[me], send_sem=send_sems.at[r - 1], recv_sem=recv_sems.at[r - 1],
                device_id=peer, device_id_type=MESH)
            rc.start()
            copies.append(rc)
        for rc in copies:
            rc.wait()
        total = land[0]
        for s in range(1, 8):
            total = total + land[s]
        sum_ref[...] = total
        d_ref[...], mo_ref[...], vo_ref[...] = _adamw_math(w_ref[...], total, m_ref[...], v_ref[...])

    vmem = pl.BlockSpec(memory_space=pltpu.VMEM)
    return pl.pallas_call(
        body, name=name, in_specs=[vmem] * 4, out_specs=[vmem] * 4,
        out_shape=[jax.ShapeDtypeStruct((rows, cols), F32)] * 4,
        scratch_shapes=[pltpu.VMEM((8, rows, cols), F32), pltpu.SemaphoreType.DMA((7,)), pltpu.SemaphoreType.DMA((7,))],
        compiler_params=pltpu.CompilerParams(has_side_effects=True),
    )(g_part, w, m, v)


def _heads(a):
    t, w = a.shape
    return a.reshape(t, w // HEAD_DIM, HEAD_DIM).transpose(1, 0, 2)


def _unheads(a):
    n, t, _ = a.shape
    return a.transpose(1, 0, 2).reshape(t, n * HEAD_DIM)


def _cols_from_shards(g):
    ns, r, cols = g.shape
    return g.transpose(1, 0, 2).reshape(r, ns * cols)


def _shards_from_cols(a):
    r, cols = a.shape
    return a.reshape(r, N_CHIPS, cols // N_CHIPS).transpose(1, 0, 2)


def kernel(x, norm_ffn1, w_ffn1_gate, w_ffn1_up, w_ffn1_down, norm_mix, w_in, b_forget, w_gate, b_gate, w_up_a, w_up_b, w_out, norm_ffn2, w_ffn2_gate, w_ffn2_up, w_ffn2_down, norm_final, loss_target, m_norm_ffn1, m_w_ffn1_gate, m_w_ffn1_up, m_w_ffn1_down, m_norm_mix, m_w_in, m_b_forget, m_w_gate, m_b_gate, m_w_up_a, m_w_up_b, m_w_out, m_norm_ffn2, m_w_ffn2_gate, m_w_ffn2_up, m_w_ffn2_down, m_norm_final, v_norm_ffn1, v_w_ffn1_gate, v_w_ffn1_up, v_w_ffn1_down, v_norm_mix, v_w_in, v_b_forget, v_w_gate, v_b_gate, v_w_up_a, v_w_up_b, v_w_out, v_norm_ffn2, v_w_ffn2_gate, v_w_ffn2_up, v_w_ffn2_down, v_norm_final):
    t, d = x.shape[1], x.shape[2]
    in4 = w_in.shape[2]
    gate4 = w_gate.shape[2]
    up4 = w_up_a.shape[2]
    in_cols = N_CHIPS * in4
    n_forget = in_cols - QKV_COLS
    assert w_up_a.shape[1] == WIDTH and d == 2 * WIDTH and n_forget == N_HEADS
    nq = t // BQ
    chip = 2 * lax.axis_index("x") + lax.axis_index("y")
    c_arr = jnp.stack([lax.axis_index("c"), chip]).astype(jnp.int32)
    x2d = x[0]
    target = loss_target[0]

    def slot(shard):
        return lax.dynamic_update_slice(jnp.zeros((N_CHIPS,) + shard.shape, BF16), shard.astype(BF16)[None], (chip, 0, 0, 0))

    def ffn_views(wg, wu, wd):
        return [wg[0].T, wu[0].T, wd[0]]

    ffn1_w, ffn1_m, ffn1_v = (ffn_views(w_ffn1_gate, w_ffn1_up, w_ffn1_down), ffn_views(m_w_ffn1_gate, m_w_ffn1_up, m_w_ffn1_down),
                              ffn_views(v_w_ffn1_gate, v_w_ffn1_up, v_w_ffn1_down))
    ffn2_w, ffn2_m, ffn2_v = (ffn_views(w_ffn2_gate, w_ffn2_up, w_ffn2_down), ffn_views(m_w_ffn2_gate, m_w_ffn2_up, m_w_ffn2_down),
                              ffn_views(v_w_ffn2_gate, v_w_ffn2_up, v_w_ffn2_down))
    mx_sh = slot(jnp.concatenate([w_in[0], w_gate[0], jnp.concatenate([w_up_a[0], w_up_b[0]], axis=0)], axis=1)[None])
    wo_sh = slot(w_out)
    (w3_1,) = gather_weights("gather_ffn1", [slot(jnp.stack(ffn1_w))])
    mx_send, mx_recv, mx_bufs, mx_token = gather_ici_start("gather_mixer_start", [mx_sh, wo_sh], w3_1)
    f2_send, f2_recv, f2_bufs, f2_token = gather_ici_start("gather_ffn2_start", [slot(jnp.stack(ffn2_w))], mx_token)

    x1, saved1 = ffn_forward("ffn1", x2d, norm_ffn1, w3_1, after=(mx_token, f2_token))
    mx, wo = gather_forward("gather_mixer_forward", gather_ici_wait("gather_mixer_wait", mx_send, mx_recv, mx_bufs, x1))
    mx = mx[:, 0]
    w_in_full = _cols_from_shards(mx[:, :, :in4])
    w_gate_full = _cols_from_shards(mx[:, :, in4:in4 + gate4])
    w_up_full = _cols_from_shards(mx[:, :, in4 + gate4:])
    w_up_a_full, w_up_b_full = w_up_full[:WIDTH], w_up_full[WIDTH:]
    wcat = jnp.concatenate([w_in_full[:, :QKV_COLS], w_gate_full, w_in_full[:, QKV_COLS:],
                            jnp.zeros((d, F_PAD - n_forget), BF16)], axis=1)
    bias_cat = jnp.concatenate([jnp.zeros((1, QKV_COLS), F32), b_gate, jnp.zeros((1, F_PAD), F32)], axis=1)
    f_off = QKV_COLS + 2 * d
    wo_full = wo.reshape(d, d)
    b_forget_row = jnp.pad(b_forget, ((0, 0), (0, QB - n_forget)))

    h2, rstd2 = rms_fwd("mix_rms", x1, norm_mix)
    pc = proj_cat("mix_proj", h2, wcat, bias_cat)
    q_scale = jnp.array([ATTN_SCALE, 1.0, 1.0, ATTN_SCALE, 1.0, 1.0], F32)[None, :, None]
    qkv = _heads((pc[:, :QKV_COLS].reshape(t, 6, WIDTH) * q_scale).astype(BF16).reshape(t, QKV_COLS))
    qt = jnp.concatenate([qkv[:N_HEADS], qkv[3 * N_HEADS:4 * N_HEADS]], axis=0)
    qt = qt.reshape(2 * N_HEADS, nq, BQ, HEAD_DIM).transpose(0, 1, 3, 2)
    f_logit = pc[:, f_off:f_off + QB]
    c_cum = fox_prep("fox_prep", f_logit, b_forget_row)
    c_heads = c_cum[:, :N_HEADS].T
    c_col = c_heads[:, :, None]
    c_row = c_heads.reshape(N_HEADS, t // CS, 1, CS)
    ya_h = sb_fwd("sb_fwd", qkv)
    yb_h, lse = fox_fwd("fox_fwd", qkv, c_col, c_row)
    ya = _unheads(ya_h)
    yb = _unheads(yb_h)
    ua, ub, mixed = mix_fwd("mix_fwd", ya, yb, w_up_a_full, w_up_b_full, pc)
    x2 = mm_residual("mix_out", mixed[None], wo_full[None], pl.BlockSpec((None, d, d), lambda m, j: (j, 0, 0)), x1, 1.0)
    (w3_2,) = gather_forward("gather_ffn2_forward", gather_ici_wait("gather_ffn2_wait", f2_send, f2_recv, f2_bufs, x2))
    x3, saved2 = ffn_forward("ffn2", x2, norm_ffn2, w3_2)
    dx3, gn_final, loss_part = loss_head("loss_head", x3, norm_final[None], target)

    dx2, gn_ffn2, g_w3_2 = ffn_backward("ffn2", dx3, saved2, norm_ffn2, w3_2)
    rs_ffn2, rs_ffn2_token = reduce_scatter_begin("rs_ffn2", [g_w3_2], c_arr)

    dua, dub, dpa, dpb, gba, gbb = mix_bwd("mix_bwd", dx2, wo_full, pc, ua, ub, after=(rs_ffn2_token,))
    dgp = jnp.concatenate([dpa, dpb], axis=1)
    g_bgate = jnp.concatenate([gba, gbb], axis=1)
    g_wo = mm_plain("wgrad_out", mixed, dx2, TN, BF16)
    dya = mm_plain("dya", dua, w_up_a_full, NT, BF16, tk_target=1024)
    dyb = mm_plain("dyb", dub, w_up_b_full, NT, BF16, tk_target=1024)
    g_up_a = mm_plain("wgrad_up_a", ya, dua, TN, BF16)
    g_up_b = mm_plain("wgrad_up_b", yb, dub, TN, BF16)
    def blocks_t(a):
        return a.reshape(N_HEADS, nq, BQ, HEAD_DIM).transpose(0, 1, 3, 2)

    def from_chunks_t(a):
        return a.transpose(1, 3, 0, 2).reshape(t, WIDTH)

    dya_h, dyb_h = _heads(dya), _heads(dyb)
    dqa, dka, dva = sb_bwd("sb_bwd", qkv, qt, dya_h, blocks_t(dya_h))
    dqb, dkb, dvb, dcc, dcr = fox_bwd("fox_bwd", qkv, qt, yb_h, dyb_h, blocks_t(dyb_h), lse, c_col, c_row)
    dc = (dcc[:, :, 0] + dcr.reshape(N_HEADS, t)).T
    df, g_bf = fox_gate_bwd("fox_gate_bwd", jnp.pad(dc, ((0, 0), (0, QB - N_HEADS))), f_logit, b_forget_row)
    dqkv = jnp.concatenate([_unheads(dqa), from_chunks_t(dka), from_chunks_t(dva),
                            _unheads(dqb), from_chunks_t(dkb), from_chunks_t(dvb)], axis=1)
    dcat = jnp.concatenate([dqkv, dgp, df.astype(BF16), jnp.zeros((t, F_PAD - QB), BF16)], axis=1)
    g_wcat = wgrad_cat("wgrad_cat", h2, dcat)
    tm = _row_tile(t, 512)
    tkc = 768
    nkc = wcat.shape[1] // tkc
    dx1, gn_mix = dh_rms_bwd(
        "mix_dh", [(dcat, wcat)],
        [(pl.BlockSpec((tm, tkc), lambda m, k: (m, k)), pl.BlockSpec((d, tkc), lambda m, k: (0, k)))], NT,
        (t // tm, nkc), nkc, x1, rstd2, norm_mix, dx2)
    g_in = jnp.concatenate([g_wcat[:, :QKV_COLS], g_wcat[:, f_off:f_off + n_forget]], axis=1)
    g_mx = jnp.concatenate([_shards_from_cols(g_in), _shards_from_cols(g_wcat[:, QKV_COLS:f_off]),
                            _shards_from_cols(jnp.concatenate([g_up_a, g_up_b], axis=0))], axis=2)[:, None]
    rs_mixer, rs_mixer_token = reduce_scatter_begin("rs_mixer", [g_mx, g_wo.reshape(N_CHIPS, 1, d // N_CHIPS, d)], c_arr)
    (s_w3_2,) = reduce_scatter_end("rs_ffn2", rs_ffn2, c_arr, dx1)

    dx0, gn_ffn1, g_w3_1 = ffn_backward("ffn1", dx1, saved1, norm_ffn1, w3_1, after=(rs_mixer_token,))
    rs_ffn1, rs_ffn1_token = reduce_scatter_begin("rs_ffn1", [g_w3_1], c_arr)
    s_mx, s_wo = reduce_scatter_end("rs_mixer", rs_mixer, c_arr, dx0)

    def pack_small(n1, nm, n2, nf, bg, bf, last):
        return jnp.concatenate([n1, nm, n2, nf, bg.reshape(2, d), jnp.pad(bf, ((0, 0), (0, d - n_forget))), last], axis=0)

    zero_row = jnp.zeros((1, d), F32)
    g_small = pack_small(gn_ffn1, gn_mix, gn_ffn2, gn_final, g_bgate, g_bf[:, :n_forget], jnp.pad(loss_part, ((0, 0), (0, d - 1))))
    w_small = pack_small(norm_ffn1, norm_mix, norm_ffn2, norm_final[None], b_gate, b_forget, zero_row)
    m_small = pack_small(m_norm_ffn1, m_norm_mix, m_norm_ffn2, m_norm_final[None], m_b_gate, m_b_forget, zero_row)
    v_small = pack_small(v_norm_ffn1, v_norm_mix, v_norm_ffn2, v_norm_final[None], v_b_gate, v_b_forget, zero_row)
    smalls = small_allreduce_adamw("small_allreduce_adamw", g_small, w_small, m_small, v_small)

    def unpack_small(p):
        return {"norm_ffn1": p[0:1], "norm_mix": p[1:2], "norm_ffn2": p[2:3], "norm_final": p[3], "b_gate": p[4:6].reshape(1, 2 * d),
                "b_forget": p[6:7, :n_forget]}

    loss = smalls[0][7, 0]
    small_out = [unpack_small(p) for p in smalls]

    grads = {
        "w_in": s_mx[0][:, :in4], "w_gate": s_mx[0][:, in4:in4 + gate4],
        "w_up_a": s_mx[0][:WIDTH, in4 + gate4:], "w_up_b": s_mx[0][WIDTH:, in4 + gate4:], "w_out": s_wo[0],
    }
    weights = {"w_in": (w_in, m_w_in, v_w_in), "w_gate": (w_gate, m_w_gate, v_w_gate), "w_up_a": (w_up_a, m_w_up_a, v_w_up_a),
               "w_up_b": (w_up_b, m_w_up_b, v_w_up_b), "w_out": (w_out, m_w_out, v_w_out)}
    big_out = {}
    for wname, (w, m, v) in weights.items():
        g = grads[wname]
        delta, new_m, new_v = adamw(f"adamw_{wname}", w[0], g, m[0], v[0], after=(rs_ffn1_token,))
        big_out[wname] = (g[None], delta[None], new_m[None], new_v[None])

    def adamw_ffn(tag, s_w3, ws, ms, vs, after):
        deltas, new_ms, new_vs = adamw_stacked(f"adamw_{tag}", ws, s_w3, ms, vs, after)
        for which, part in ((GATE, "gate"), (UP, "up"), (DOWN, "down")):
            back = (lambda a: a[None]) if which == DOWN else (lambda a: a.T[None])
            big_out[f"w_{tag}_{part}"] = tuple(back(a) for a in (s_w3[which], deltas[which], new_ms[which], new_vs[which]))
        return deltas[DOWN]

    last = adamw_ffn("ffn2", s_w3_2, ffn2_w, ffn2_m, ffn2_v, (rs_ffn1_token,))
    (s_w3_1,) = reduce_scatter_end("rs_ffn1", rs_ffn1, c_arr, last)
    adamw_ffn("ffn1", s_w3_1, ffn1_w, ffn1_m, ffn1_v, ())

    order = ["norm_ffn1", "w_ffn1_gate", "w_ffn1_up", "w_ffn1_down", "norm_mix", "w_in", "b_forget", "w_gate", "b_gate",
             "w_up_a", "w_up_b", "w_out", "norm_ffn2", "w_ffn2_gate", "w_ffn2_up", "w_ffn2_down", "norm_final"]
    outs = [loss, dx0[None]]
    for kind in range(4):
        for wname in order:
            outs.append(big_out[wname][kind] if wname in big_out else small_out[kind][wname])
    return tuple(outs)
```

```python
import functools

import jax
import jax.numpy as jnp
from jax import lax
from jax.experimental import pallas as pl
from jax.experimental.pallas import tpu as pltpu

F32 = jnp.float32
BF16 = jnp.bfloat16

HEAD_DIM = 64
N_HEADS = 8
WIDTH = N_HEADS * HEAD_DIM
QKV_COLS = 6 * WIDTH
RMS_EPS = 1e-6
ATTN_SCALE = HEAD_DIM ** -0.5
N_CHIPS = 4
QB = 128
BQ = 512
CS = 256
N_SUB = BQ // CS
F_PAD = 256
NEG_BIG = -1e30

ADAM_LR = 0.001
ADAM_B1 = 0.9
ADAM_B2 = 0.999
ADAM_EPS = 1e-08
ADAM_WD = 0.01
ADAM_STEP = 10

VMEM_LIMIT_BYTES = 48 * 1024 * 1024
MESH = pl.DeviceIdType.MESH

NN = ((1,), (0,))
NT = ((1,), (1,))
TN = ((0,), (0,))


def _params(semantics):
    return pltpu.CompilerParams(dimension_semantics=semantics, vmem_limit_bytes=VMEM_LIMIT_BYTES)


def _dot(a, b, contract):
    return lax.dot_general(a.astype(BF16), b.astype(BF16), (contract, ((), ())), preferred_element_type=F32)


def _sigmoid(x):
    return 1.0 / (1.0 + jnp.exp(-x))


def _log1pexp_neg_abs(z):
    return jnp.log(1.0 + jnp.exp(-jnp.abs(z)))


def _split3(x):
    hi = x.astype(BF16)
    r1 = x - hi.astype(F32)
    mid = r1.astype(BF16)
    lo = (r1 - mid.astype(F32)).astype(BF16)
    return hi, mid, lo


def _dot_exact_rhs01(x, m01):
    hi, mid, lo = _split3(x)
    d = lambda p: lax.dot_general(p, m01, (NN, ((), ())), preferred_element_type=F32)
    return d(hi) + d(mid) + d(lo)


def _dot_exact_lhs01(m01, x):
    hi, mid, lo = _split3(x)
    d = lambda p: lax.dot_general(m01, p, (NN, ((), ())), preferred_element_type=F32)
    return d(hi) + d(mid) + d(lo)


def _iota2(shape, dim):
    return lax.broadcasted_iota(jnp.int32, shape, dim)


def _mm(name, pairs, contract, grid, pair_specs, out_shape, out_specs, acc_shape, nk, epilogue,
        extras=(), extra_specs=(), semantics=None):
    n_pairs = len(pairs)
    n_extra = len(extras)
    n_out = len(out_shape)

    def body(*refs):
        ab = refs[:2 * n_pairs]
        ex = refs[2 * n_pairs:2 * n_pairs + n_extra]
        outs = refs[2 * n_pairs + n_extra:2 * n_pairs + n_extra + n_out]
        acc = refs[-1]
        ids = [pl.program_id(i) for i in range(len(grid))]
        k = ids[-1]

        @pl.when(k == 0)
        def _():
            acc[...] = jnp.zeros_like(acc)

        part = _dot(ab[0][...], ab[1][...], contract)
        for p in range(1, n_pairs):
            part += _dot(ab[2 * p][...], ab[2 * p + 1][...], contract)
        acc[...] += part

        @pl.when(k == nk - 1)
        def _():
            epilogue(acc[...], ex, outs, ids)

    operands = [t for pair in pairs for t in pair] + list(extras)
    in_specs = [s for pair in pair_specs for s in pair] + list(extra_specs)
    if semantics is None:
        semantics = ("parallel",) * (len(grid) - 1) + ("arbitrary",)
    return pl.pallas_call(
        body, name=name, grid=grid, in_specs=in_specs, out_specs=list(out_specs), out_shape=list(out_shape),
        scratch_shapes=[pltpu.VMEM(acc_shape, F32)], compiler_params=_params(semantics),
    )(*operands)


def _ordered_after(body, n_in, n_after):
    def wrapped(*refs):
        return body(*refs[:n_in], *refs[n_in + n_after:])
    return wrapped


def _row_tile(rows, target):
    t = min(rows, target)
    while rows % t:
        t //= 2
    return t


def rms_fwd(name, x, g, after=()):
    t, d = x.shape
    tr = _row_tile(t, 256)

    def body(x_ref, g_ref, h_ref, r_ref):
        xv = x_ref[...]
        r = lax.rsqrt(jnp.mean(xv * xv, axis=-1, keepdims=True) + RMS_EPS)
        h_ref[...] = (xv * r * g_ref[...]).astype(BF16)
        r_ref[...] = r

    return pl.pallas_call(
        _ordered_after(body, 2, len(after)), name=name, grid=(t // tr,),
        in_specs=[pl.BlockSpec((tr, d), lambda i: (i, 0)), pl.BlockSpec((1, d), lambda i: (0, 0))] + [_ANY] * len(after),
        out_specs=[pl.BlockSpec((tr, d), lambda i: (i, 0)), pl.BlockSpec((tr, 1), lambda i: (i, 0))],
        out_shape=[jax.ShapeDtypeStruct((t, d), BF16), jax.ShapeDtypeStruct((t, 1), F32)],
        compiler_params=_params(("parallel",)),
    )(x, g, *after)


GATE, UP, DOWN = 0, 1, 2


def _ffn_w_spec(which, f4, d, index_of_j):
    return pl.BlockSpec((None, None, f4, d), lambda *ids: (index_of_j(*ids), which, 0, 0))


def ffn_up(name, h, w3):
    t, d = h.shape
    ns, _, f4, _ = w3.shape
    tm = _row_tile(t, 512)

    def body(h_ref, wg_ref, wu_ref, a_ref, b_ref, s_ref):
        hv = h_ref[...]
        a = _dot(hv, wg_ref[...], NT)
        b = _dot(hv, wu_ref[...], NT)
        a_ref[...] = a.astype(BF16)
        b_ref[...] = b.astype(BF16)
        s_ref[...] = (a * _sigmoid(a) * b).astype(BF16)

    act_spec = pl.BlockSpec((None, tm, f4), lambda j, m: (j, m, 0))
    return pl.pallas_call(
        body, name=name, grid=(ns, t // tm),
        in_specs=[pl.BlockSpec((tm, d), lambda j, m: (m, 0)),
                  _ffn_w_spec(GATE, f4, d, lambda j, m: j), _ffn_w_spec(UP, f4, d, lambda j, m: j)],
        out_specs=[act_spec, act_spec, act_spec],
        out_shape=[jax.ShapeDtypeStruct((ns, t, f4), BF16)] * 3,
        compiler_params=_params(("parallel", "parallel")),
    )(h, w3, w3)


def mm_residual(name, s, w, w_spec, x, scale):
    nj, t, kdim = s.shape
    n = x.shape[1]
    tm = _row_tile(t, 512)

    def epilogue(acc, ex, outs, ids):
        outs[0][...] = ex[0][...] + scale * acc

    return _mm(
        name, [(s, w)], NN, (t // tm, nj),
        [(pl.BlockSpec((None, tm, kdim), lambda m, j: (j, m, 0)), w_spec)],
        [jax.ShapeDtypeStruct((t, n), F32)], [pl.BlockSpec((tm, n), lambda m, j: (m, 0))], (tm, n), nj, epilogue,
        extras=[x], extra_specs=[pl.BlockSpec((tm, n), lambda m, j: (m, 0))],
    )[0]


def ffn_bwd_act(name, dx, w3, a, b, after=()):
    t, d = dx.shape
    ns, _, f4, _ = w3.shape
    tm = _row_tile(t, 512)

    def body(dx_ref, wd_ref, a_ref, b_ref, da_ref, db_ref):
        ds = _dot(0.5 * dx_ref[...], wd_ref[...], NT)
        av = a_ref[...].astype(F32)
        sig = _sigmoid(av)
        da_ref[...] = (ds * b_ref[...].astype(F32) * (sig * (1.0 + av * (1.0 - sig)))).astype(BF16)
        db_ref[...] = (ds * (av * sig)).astype(BF16)

    act_spec = pl.BlockSpec((None, tm, f4), lambda j, m: (j, m, 0))
    return pl.pallas_call(
        _ordered_after(body, 4, len(after)), name=name, grid=(ns, t // tm),
        in_specs=[pl.BlockSpec((tm, d), lambda j, m: (m, 0)), _ffn_w_spec(DOWN, f4, d, lambda j, m: j), act_spec, act_spec]
        + [_ANY] * len(after),
        out_specs=[act_spec, act_spec],
        out_shape=[jax.ShapeDtypeStruct((ns, t, f4), BF16)] * 2,
        compiler_params=_params(("parallel", "parallel")),
    )(dx, w3, a, b, *after)


def ffn_wgrad(name, h, da, db, s, dx):
    t, d = h.shape
    ns, _, f4 = da.shape
    tk = _row_tile(t, 512)
    nk = t // tk

    def body(h_ref, da_ref, db_ref, s_ref, dx_ref, o_ref, acc):
        k = pl.program_id(1)

        @pl.when(k == 0)
        def _():
            acc[...] = jnp.zeros_like(acc)

        hv = h_ref[...]
        acc[GATE] += _dot(da_ref[...], hv, TN)
        acc[UP] += _dot(db_ref[...], hv, TN)
        acc[DOWN] += _dot(s_ref[...], 0.5 * dx_ref[...], TN)

        @pl.when(k == nk - 1)
        def _():
            o_ref[...] = acc[...].astype(BF16)

    act_spec = pl.BlockSpec((None, tk, f4), lambda j, k: (j, k, 0))
    row_spec = pl.BlockSpec((tk, d), lambda j, k: (k, 0))
    return pl.pallas_call(
        body, name=name, grid=(ns, nk),
        in_specs=[row_spec, act_spec, act_spec, act_spec, row_spec],
        out_specs=pl.BlockSpec((None, 3, f4, d), lambda j, k: (j, 0, 0, 0)),
        out_shape=jax.ShapeDtypeStruct((ns, 3, f4, d), BF16),
        scratch_shapes=[pltpu.VMEM((3, f4, d), F32)],
        compiler_params=_params(("parallel", "arbitrary")),
    )(h, da, db, s, dx)


def dh_rms_bwd(name, pairs, pair_specs, contract, grid, nk, x, rstd, g, dx_in):
    t, d = x.shape
    tm = t // grid[0]

    def epilogue(acc, ex, outs, ids):
        x_ref, r_ref, g_ref, dxin_ref = ex
        r = r_ref[...]
        xhat = x_ref[...] * r
        dhg = acc * g_ref[...]
        proj = jnp.mean(dhg * xhat, axis=-1, keepdims=True)
        outs[0][...] = dxin_ref[...] + r * (dhg - xhat * proj)
        part = jnp.sum(acc * xhat, axis=0, keepdims=True)

        @pl.when(ids[0] == 0)
        def _():
            outs[1][...] = part

        @pl.when(ids[0] != 0)
        def _():
            outs[1][...] += part

    row = pl.BlockSpec((tm, d), lambda m, k: (m, 0))
    return _mm(
        name, pairs, contract, grid, pair_specs,
        [jax.ShapeDtypeStruct((t, d), F32), jax.ShapeDtypeStruct((1, d), F32)],
        [row, pl.BlockSpec((1, d), lambda m, k: (0, 0))], (tm, d), nk, epilogue,
        extras=[x, rstd, g, dx_in],
        extra_specs=[row, pl.BlockSpec((tm, 1), lambda m, k: (m, 0)), pl.BlockSpec((1, d), lambda m, k: (0, 0)), row],
        semantics=("arbitrary", "arbitrary"),
    )


def ffn_forward(tag, x, g_norm, w3, after=()):
    _, _, f4, d = w3.shape
    h, rstd = rms_fwd(f"{tag}_rms", x, g_norm, after)
    a, b, s = ffn_up(f"{tag}_up", h, w3)
    x_out = mm_residual(f"{tag}_down", s, w3, _ffn_w_spec(DOWN, f4, d, lambda m, j: j), x, 0.5)
    return x_out, (x, h, rstd, a, b, s)


def ffn_backward(tag, dx, saved, g_norm, w3, after=()):
    x, h, rstd, a, b, s = saved
    t, d = x.shape
    ns, _, f4, _ = w3.shape
    tm = _row_tile(t, 512)
    da, db = ffn_bwd_act(f"{tag}_bwd_act", dx, w3, a, b, after)
    g_w3 = ffn_wgrad(f"{tag}_wgrad", h, da, db, s, dx)
    act_spec = pl.BlockSpec((None, tm, f4), lambda m, j: (j, m, 0))
    w_spec = lambda which: _ffn_w_spec(which, f4, d, lambda m, j: j)
    dx_out, g_n = dh_rms_bwd(f"{tag}_dh", [(da, w3), (db, w3)], [(act_spec, w_spec(GATE)), (act_spec, w_spec(UP))], NN,
                             (t // tm, ns), ns, x, rstd, g_norm, dx)
    return dx_out, g_n, g_w3


def proj(name, h, wcat, bias, first_col, n_cols, tn, out_dtype, scaled_tiles=()):
    t, d = h.shape
    tm = _row_tile(t, 512)
    off = first_col // tn

    def epilogue(acc, ex, outs, ids):
        val = acc + ex[0][...]
        if scaled_tiles:
            hit = functools.reduce(jnp.logical_or, [ids[0] == s for s in scaled_tiles])
            val = val * jnp.where(hit, ATTN_SCALE, 1.0)
        outs[0][...] = val.astype(out_dtype)

    return _mm(
        name, [(h, wcat)], NN, (n_cols // tn, t // tm, 1),
        [(pl.BlockSpec((tm, d), lambda j, m, k: (m, 0)), pl.BlockSpec((d, tn), lambda j, m, k: (0, off + j)))],
        [jax.ShapeDtypeStruct((t, n_cols), out_dtype)], [pl.BlockSpec((tm, tn), lambda j, m, k: (m, j))], (tm, tn), 1, epilogue,
        extras=[bias], extra_specs=[pl.BlockSpec((1, tn), lambda j, m, k: (0, off + j))],
    )[0]


def mix_fwd(name, ya, yb, wa, wb, pc):
    t, w = ya.shape
    d = wa.shape[1]
    tm = _row_tile(t, 512)
    tn = 512
    off_a = 0
    off_b = d // tn

    def body(ya_ref, yb_ref, wa_ref, wb_ref, pa_ref, pb_ref, ua_ref, ub_ref, mx_ref):
        ua = _dot(ya_ref[...], wa_ref[...], NN)
        ub = _dot(yb_ref[...], wb_ref[...], NN)
        ua_ref[...] = ua
        ub_ref[...] = ub
        mx_ref[...] = (_sigmoid(pa_ref[...]) * ua + _sigmoid(pb_ref[...]) * ub).astype(BF16)

    y_spec = pl.BlockSpec((tm, w), lambda m, n: (m, 0))
    w_spec = pl.BlockSpec((w, tn), lambda m, n: (0, n))
    o_spec = pl.BlockSpec((tm, tn), lambda m, n: (m, n))
    return pl.pallas_call(
        body, name=name, grid=(t // tm, d // tn),
        in_specs=[y_spec, y_spec, w_spec, w_spec,
                  pl.BlockSpec((tm, tn), lambda m, n: (m, off_a + n)), pl.BlockSpec((tm, tn), lambda m, n: (m, off_b + n))],
        out_specs=[o_spec, o_spec, o_spec],
        out_shape=[jax.ShapeDtypeStruct((t, d), F32), jax.ShapeDtypeStruct((t, d), F32), jax.ShapeDtypeStruct((t, d), BF16)],
        compiler_params=_params(("parallel", "parallel")),
    )(ya, yb, wa, wb, pc, pc)


def mix_bwd(name, dx, wo, pc, ua, ub, after=()):
    t, d = dx.shape
    tm = _row_tile(t, 512)
    tn = 512
    off_a = 0
    off_b = d // tn

    def body(dx_ref, wo_ref, pa_ref, pb_ref, ua_ref, ub_ref, dua_ref, dub_ref, dpa_ref, dpb_ref, ba_ref, bb_ref):
        dm = _dot(dx_ref[...], wo_ref[...], NT)
        ga = _sigmoid(pa_ref[...])
        gb = _sigmoid(pb_ref[...])
        dua_ref[...] = (dm * ga).astype(BF16)
        dub_ref[...] = (dm * gb).astype(BF16)
        dpa = dm * ua_ref[...] * ga * (1.0 - ga)
        dpb = dm * ub_ref[...] * gb * (1.0 - gb)
        dpa_ref[...] = dpa.astype(BF16)
        dpb_ref[...] = dpb.astype(BF16)
        sa = jnp.sum(dpa, axis=0, keepdims=True)
        sb = jnp.sum(dpb, axis=0, keepdims=True)

        @pl.when(pl.program_id(1) == 0)
        def _():
            ba_ref[...] = sa
            bb_ref[...] = sb

        @pl.when(pl.program_id(1) != 0)
        def _():
            ba_ref[...] += sa
            bb_ref[...] += sb

    tile = pl.BlockSpec((tm, tn), lambda n, m: (m, n))
    bias = pl.BlockSpec((1, tn), lambda n, m: (0, n))
    return pl.pallas_call(
        _ordered_after(body, 6, len(after)), name=name, grid=(d // tn, t // tm),
        in_specs=[pl.BlockSpec((tm, d), lambda n, m: (m, 0)), pl.BlockSpec((tn, d), lambda n, m: (n, 0)),
                  pl.BlockSpec((tm, tn), lambda n, m: (m, off_a + n)), pl.BlockSpec((tm, tn), lambda n, m: (m, off_b + n)),
                  tile, tile] + [_ANY] * len(after),
        out_specs=[tile, tile, tile, tile, bias, bias],
        out_shape=[jax.ShapeDtypeStruct((t, d), BF16)] * 4 + [jax.ShapeDtypeStruct((1, d), F32)] * 2,
        compiler_params=_params(("parallel", "arbitrary")),
    )(dx, wo, pc, pc, ua, ub, *after)


def mm_plain(name, a, b, contract, out_dtype, tk_target=512):
    if contract == NN:
        m, kdim = a.shape
        n = b.shape[1]
    elif contract == NT:
        m, kdim = a.shape
        n = b.shape[0]
    else:
        kdim, m = a.shape
        n = b.shape[1]
    tm = _row_tile(m, 512)
    tk = _row_tile(kdim, tk_target)
    nk = kdim // tk
    if contract == TN:
        a_spec = pl.BlockSpec((tk, tm), lambda i, k: (k, i))
    else:
        a_spec = pl.BlockSpec((tm, tk), lambda i, k: (i, k))
    if contract == NT:
        b_spec = pl.BlockSpec((n, tk), lambda i, k: (0, k))
    else:
        b_spec = pl.BlockSpec((tk, n), lambda i, k: (k, 0))

    def epilogue(acc, ex, outs, ids):
        outs[0][...] = acc.astype(out_dtype)

    return _mm(name, [(a, b)], contract, (m // tm, nk), [(a_spec, b_spec)],
               [jax.ShapeDtypeStruct((m, n), out_dtype)], [pl.BlockSpec((tm, n), lambda i, k: (i, 0))], (tm, n), nk, epilogue)[0]


def wgrad_cat(name, h, dcat):
    t, d = h.shape
    n = dcat.shape[1]
    tn = 768
    tk = _row_tile(t, 512)

    def epilogue(acc, ex, outs, ids):
        outs[0][...] = acc.astype(BF16)

    return _mm(
        name, [(h, dcat)], TN, (n // tn, t // tk),
        [(pl.BlockSpec((tk, d), lambda j, k: (k, 0)), pl.BlockSpec((tk, tn), lambda j, k: (k, j)))],
        [jax.ShapeDtypeStruct((d, n), BF16)], [pl.BlockSpec((d, tn), lambda j, k: (0, j))], (d, tn), t // tk, epilogue,
    )[0]


def fox_prep(name, f, bias):
    t, lanes = f.shape
    nchunk = t // QB

    def body(f_ref, b_ref, c_ref):
        lower = (_iota2((QB, QB), 1) <= _iota2((QB, QB), 0)).astype(BF16)

        def chunk(n, carry):
            rows = pl.ds(pl.multiple_of(n * QB, QB), QB)
            u = f_ref[rows, :] + b_ref[...]
            lf = jnp.minimum(u, 0.0) - _log1pexp_neg_abs(u)
            c = _dot_exact_lhs01(lower, lf) + carry
            c_ref[rows, :] = c
            return c[QB - 1:QB, :]

        lax.fori_loop(0, nchunk, chunk, jnp.zeros((1, lanes), F32))

    return pl.pallas_call(body, name=name, out_shape=jax.ShapeDtypeStruct((t, lanes), F32),
                          compiler_params=pltpu.CompilerParams(vmem_limit_bytes=VMEM_LIMIT_BYTES))(f, bias)


def fox_gate_bwd(name, dc, f, bias):
    t, lanes = dc.shape
    nchunk = t // QB

    def body(dc_ref, f_ref, b_ref, df_ref, gb_ref):
        upper = (_iota2((QB, QB), 1) >= _iota2((QB, QB), 0)).astype(BF16)

        def chunk(n, carry):
            tail, total = carry
            rows = pl.ds(pl.multiple_of((nchunk - 1 - n) * QB, QB), QB)
            dlf = _dot_exact_lhs01(upper, dc_ref[rows, :]) + tail
            u = f_ref[rows, :] + b_ref[...]
            df = dlf * jnp.exp(jnp.minimum(-u, 0.0) - _log1pexp_neg_abs(u))
            df_ref[rows, :] = df
            return dlf[0:1, :], total + jnp.sum(df, axis=0, keepdims=True)

        zero = jnp.zeros((1, lanes), F32)
        _, total = lax.fori_loop(0, nchunk, chunk, (zero, zero))
        gb_ref[...] = total

    return pl.pallas_call(body, name=name,
                          out_shape=[jax.ShapeDtypeStruct((t, lanes), F32), jax.ShapeDtypeStruct((1, lanes), F32)],
                          compiler_params=pltpu.CompilerParams(vmem_limit_bytes=VMEM_LIMIT_BYTES))(dc, f, bias)


def _qrows(i):
    return pl.ds(pl.multiple_of(i * BQ, BQ), BQ)


def _krows(kc):
    return pl.ds(pl.multiple_of(kc * CS, CS), CS)


def _head_spec(t, offset):
    return pl.BlockSpec((None, t, HEAD_DIM), lambda h: (offset + h, 0, 0))


def _head_t_spec(nq, offset):
    return pl.BlockSpec((None, nq, HEAD_DIM, BQ), lambda h: (offset + h, 0, 0, 0))


def _chunk_t_spec(nc):
    return pl.BlockSpec((None, nc, HEAD_DIM, CS), lambda h: (h, 0, 0, 0))


def _dot_split2_rhs01(x, m01):
    hi = x.astype(BF16)
    lo = (x - hi.astype(F32)).astype(BF16)
    d = lambda p: lax.dot_general(p, m01, (NN, ((), ())), preferred_element_type=F32)
    return d(hi) + d(lo)


def _diag_mask(dchunk, inclusive):
    r_io = _iota2((BQ, CS), 0)
    c_io = _iota2((BQ, CS), 1) + dchunk * CS
    return c_io <= r_io if inclusive else c_io < r_io


def _walk_chunks(i, step, init, right_to_left):
    order = list(reversed(range(N_SUB))) if right_to_left else list(range(N_SUB))

    def diagonal(state):
        for dchunk in order:
            state = step(i * N_SUB + dchunk, state, dchunk)
        return state

    def group(n, state):
        base = ((i - 1 - n) if right_to_left else n) * N_SUB
        for dchunk in order:
            state = step(base + dchunk, state, None)
        return state

    if right_to_left:
        return lax.fori_loop(0, i, group, diagonal(init))
    return diagonal(lax.fori_loop(0, i, group, init))


def sb_fwd(name, qkv):
    t = qkv.shape[1]

    def body(q_ref, k_ref, v_ref, o_ref):
        later = (_iota2((CS, CS), 0) > _iota2((CS, CS), 1)).astype(BF16)

        def qblock(i, _):
            q = q_ref[_qrows(i), :]

            def step(kc, state, dchunk):
                carry, acc = state
                z = _dot(q, k_ref[_krows(kc), :], NT)
                sp = _log1pexp_neg_abs(z)
                lnb = -jnp.maximum(z, 0.0) - sp
                if dchunk is not None:
                    lnb = jnp.where(_diag_mask(dchunk, False), lnb, 0.0)
                w = jnp.exp(jnp.minimum(z, 0.0) - sp + _dot_split2_rhs01(lnb, later) + carry)
                if dchunk is not None:
                    w = jnp.where(_diag_mask(dchunk, False), w, 0.0)
                acc = acc + _dot(w, v_ref[_krows(kc), :], NN)
                return carry + jnp.sum(lnb, axis=1, keepdims=True), acc

            init = (jnp.zeros((BQ, 1), F32), jnp.zeros((BQ, HEAD_DIM), F32))
            _, acc = _walk_chunks(i, step, init, True)
            o_ref[_qrows(i), :] = acc.astype(BF16)
            return 0

        lax.fori_loop(0, t // BQ, qblock, 0)

    return pl.pallas_call(
        body, name=name, grid=(N_HEADS,),
        in_specs=[_head_spec(t, 0), _head_spec(t, N_HEADS), _head_spec(t, 2 * N_HEADS)],
        out_specs=_head_spec(t, 0), out_shape=jax.ShapeDtypeStruct((N_HEADS, t, HEAD_DIM), BF16),
        compiler_params=_params(("parallel",)),
    )(qkv, qkv, qkv)


def sb_bwd(name, qkv, qt, dy, dyt):
    t = qkv.shape[1]
    nq, nc = t // BQ, t // CS

    def body(q_ref, k_ref, v_ref, qt_ref, do_ref, dot_ref, dq_ref, dkt_ref, dvt_ref, g_s, b_s, dkt_acc, dvt_acc):
        later = (_iota2((CS, CS), 0) > _iota2((CS, CS), 1)).astype(BF16)
        earlier = (_iota2((CS, CS), 0) < _iota2((CS, CS), 1)).astype(BF16)
        dkt_acc[...] = jnp.zeros_like(dkt_acc)
        dvt_acc[...] = jnp.zeros_like(dvt_acc)

        def qblock(i, _):
            q = q_ref[_qrows(i), :]
            do = do_ref[_qrows(i), :]
            q_t = qt_ref[i]
            do_t = dot_ref[i]

            def step1(kc, carry, dchunk):
                z = _dot(q, k_ref[_krows(kc), :], NT)
                sp = _log1pexp_neg_abs(z)
                lnb = -jnp.maximum(z, 0.0) - sp
                lsz = jnp.minimum(z, 0.0) - sp
                if dchunk is not None:
                    lnb = jnp.where(_diag_mask(dchunk, False), lnb, 0.0)
                w = jnp.exp(lsz + _dot_split2_rhs01(lnb, later) + carry)
                if dchunk is not None:
                    w = jnp.where(_diag_mask(dchunk, False), w, 0.0)
                g_s[kc] = w * _dot(do, v_ref[_krows(kc), :], NT)
                b_s[kc] = jnp.exp(lsz)
                dvt_acc[kc] += _dot(do_t, w, NN)
                return carry + jnp.sum(lnb, axis=1, keepdims=True)

            _walk_chunks(i, step1, jnp.zeros((BQ, 1), F32), True)

            def step2(kc, state, dchunk):
                before, dq = state
                g = g_s[kc]
                beta = b_s[kc]
                dz = g * (1.0 - beta) - beta * (_dot_split2_rhs01(g, earlier) + before)
                if dchunk is not None:
                    dz = jnp.where(_diag_mask(dchunk, False), dz, 0.0)
                dzb = dz.astype(BF16)
                dq = dq + _dot(dzb, k_ref[_krows(kc), :], NN)
                dkt_acc[kc] += _dot(q_t, dzb, NN)
                return before + jnp.sum(g, axis=1, keepdims=True), dq

            _, dq = _walk_chunks(i, step2, (jnp.zeros((BQ, 1), F32), jnp.zeros((BQ, HEAD_DIM), F32)), False)
            dq_ref[_qrows(i), :] = (dq * ATTN_SCALE).astype(BF16)
            return 0

        lax.fori_loop(0, nq, qblock, 0)
        dkt_ref[...] = dkt_acc[...].astype(BF16)
        dvt_ref[...] = dvt_acc[...].astype(BF16)

    chunked = jax.ShapeDtypeStruct((N_HEADS, nc, HEAD_DIM, CS), BF16)
    return pl.pallas_call(
        body, name=name, grid=(N_HEADS,),
        in_specs=[_head_spec(t, 0), _head_spec(t, N_HEADS), _head_spec(t, 2 * N_HEADS), _head_t_spec(nq, 0),
                  _head_spec(t, 0), _head_t_spec(nq, 0)],
        out_specs=[_head_spec(t, 0), _chunk_t_spec(nc), _chunk_t_spec(nc)],
        out_shape=[jax.ShapeDtypeStruct((N_HEADS, t, HEAD_DIM), BF16), chunked, chunked],
        scratch_shapes=[pltpu.VMEM((nc, BQ, CS), F32), pltpu.VMEM((nc, BQ, CS), F32),
                        pltpu.VMEM((nc, HEAD_DIM, CS), F32), pltpu.VMEM((nc, HEAD_DIM, CS), F32)],
        compiler_params=_params(("parallel",)),
    )(qkv, qkv, qkv, qt, dy, dyt)


def _col_spec(t):
    return pl.BlockSpec((None, t, 1), lambda h: (h, 0, 0))


def _row_spec(nc):
    return pl.BlockSpec((None, nc, 1, CS), lambda h: (h, 0, 0, 0))


def fox_fwd(name, qkv, c_col, c_row):
    t = qkv.shape[1]

    def body(q_ref, k_ref, v_ref, cc_ref, cr_ref, o_ref, lse_ref):
        def qblock(i, _):
            q = q_ref[_qrows(i), :]
            ct = cc_ref[_qrows(i), :]

            def step(kc, state, dchunk):
                m, l, acc = state
                s = _dot(q, k_ref[_krows(kc), :], NT) + ct - cr_ref[kc]
                if dchunk is not None:
                    s = jnp.where(_diag_mask(dchunk, True), s, NEG_BIG)
                m_new = jnp.maximum(m, jnp.max(s, axis=1, keepdims=True))
                alpha = jnp.exp(m - m_new)
                p = jnp.exp(s - m_new)
                if dchunk is not None:
                    p = jnp.where(_diag_mask(dchunk, True), p, 0.0)
                l = l * alpha + jnp.sum(p, axis=1, keepdims=True)
                acc = acc * alpha + _dot(p, v_ref[_krows(kc), :], NN)
                return m_new, l, acc

            init = (jnp.full((BQ, 1), NEG_BIG, F32), jnp.zeros((BQ, 1), F32), jnp.zeros((BQ, HEAD_DIM), F32))
            m, l, acc = _walk_chunks(i, step, init, False)
            o_ref[_qrows(i), :] = (acc / l).astype(BF16)
            lse_ref[_qrows(i), :] = m + jnp.log(l)
            return 0

        lax.fori_loop(0, t // BQ, qblock, 0)

    return pl.pallas_call(
        body, name=name, grid=(N_HEADS,),
        in_specs=[_head_spec(t, 3 * N_HEADS), _head_spec(t, 4 * N_HEADS), _head_spec(t, 5 * N_HEADS),
                  _col_spec(t), _row_spec(t // CS)],
        out_specs=[_head_spec(t, 0), _col_spec(t)],
        out_shape=[jax.ShapeDtypeStruct((N_HEADS, t, HEAD_DIM), BF16), jax.ShapeDtypeStruct((N_HEADS, t, 1), F32)],
        compiler_params=_params(("parallel",)),
    )(qkv, qkv, qkv, c_col, c_row)


def fox_bwd(name, qkv, qt, y, dy, dyt, lse, c_col, c_row):
    t = qkv.shape[1]
    nq, nc = t // BQ, t // CS

    def body(q_ref, k_ref, v_ref, qt_ref, o_ref, do_ref, dot_ref, lse_ref, cc_ref, cr_ref,
             dq_ref, dkt_ref, dvt_ref, dcc_ref, dcr_ref, dkt_acc, dvt_acc, dcr_acc):
        dkt_acc[...] = jnp.zeros_like(dkt_acc)
        dvt_acc[...] = jnp.zeros_like(dvt_acc)
        dcr_acc[...] = jnp.zeros_like(dcr_acc)

        def qblock(i, _):
            q = q_ref[_qrows(i), :]
            do = do_ref[_qrows(i), :]
            q_t = qt_ref[i]
            do_t = dot_ref[i]
            ct = cc_ref[_qrows(i), :]
            lse_i = lse_ref[_qrows(i), :]
            delta = jnp.sum(do.astype(F32) * o_ref[_qrows(i), :].astype(F32), axis=1, keepdims=True)

            def step(kc, state, dchunk):
                dq, dct = state
                s = _dot(q, k_ref[_krows(kc), :], NT) + ct - cr_ref[kc]
                p = jnp.exp(s - lse_i)
                if dchunk is not None:
                    p = jnp.where(_diag_mask(dchunk, True), p, 0.0)
                ds = p * (_dot(do, v_ref[_krows(kc), :], NT) - delta)
                dvt_acc[kc] += _dot(do_t, p, NN)
                dsb = ds.astype(BF16)
                dq = dq + _dot(dsb, k_ref[_krows(kc), :], NN)
                dkt_acc[kc] += _dot(q_t, dsb, NN)
                dcr_acc[kc] -= jnp.sum(ds, axis=0, keepdims=True)
                return dq, dct + jnp.sum(ds, axis=1, keepdims=True)

            dq, dct = _walk_chunks(i, step, (jnp.zeros((BQ, HEAD_DIM), F32), jnp.zeros((BQ, 1), F32)), False)
            dq_ref[_qrows(i), :] = (dq * ATTN_SCALE).astype(BF16)
            dcc_ref[_qrows(i), :] = dct
            return 0

        lax.fori_loop(0, nq, qblock, 0)
        dkt_ref[...] = dkt_acc[...].astype(BF16)
        dvt_ref[...] = dvt_acc[...].astype(BF16)
        dcr_ref[...] = dcr_acc[...]

    chunked = jax.ShapeDtypeStruct((N_HEADS, nc, HEAD_DIM, CS), BF16)
    return pl.pallas_call(
        body, name=name, grid=(N_HEADS,),
        in_specs=[_head_spec(t, 3 * N_HEADS), _head_spec(t, 4 * N_HEADS), _head_spec(t, 5 * N_HEADS), _head_t_spec(nq, N_HEADS),
                  _head_spec(t, 0), _head_spec(t, 0), _head_t_spec(nq, 0), _col_spec(t), _col_spec(t), _row_spec(nc)],
        out_specs=[_head_spec(t, 0), _chunk_t_spec(nc), _chunk_t_spec(nc), _col_spec(t), _row_spec(nc)],
        out_shape=[jax.ShapeDtypeStruct((N_HEADS, t, HEAD_DIM), BF16), chunked, chunked,
                   jax.ShapeDtypeStruct((N_HEADS, t, 1), F32), jax.ShapeDtypeStruct((N_HEADS, nc, 1, CS), F32)],
        scratch_shapes=[pltpu.VMEM((nc, HEAD_DIM, CS), F32), pltpu.VMEM((nc, HEAD_DIM, CS), F32), pltpu.VMEM((nc, 1, CS), F32)],
        compiler_params=_params(("parallel",)),
    )(qkv, qkv, qkv, qt, y, dy, dyt, lse, c_col, c_row)


PAIR = 2 * HEAD_DIM
N_PAIRS = N_HEADS // 2


def _pair_spec(t, first_block):
    return pl.BlockSpec((t, PAIR), lambda p, *_: (0, first_block + p))


def _head_lanes(shape):
    lane = _iota2(shape, len(shape) - 1)
    return [lane < HEAD_DIM, lane >= HEAD_DIM]


def _only_head(x, lanes_of_head):
    return jnp.where(lanes_of_head, x, jnp.zeros_like(x))


def _sb_chunk_weights(q_h, k, later, carry, dchunk):
    z = _dot(q_h, k, NT)
    sp = _log1pexp_neg_abs(z)
    lnb = -jnp.maximum(z, 0.0) - sp
    lsz = jnp.minimum(z, 0.0) - sp
    if dchunk is not None:
        lnb = jnp.where(_diag_mask(dchunk, False), lnb, 0.0)
    w = jnp.exp(lsz + _dot_split2_rhs01(lnb, later) + carry)
    if dchunk is not None:
        w = jnp.where(_diag_mask(dchunk, False), w, 0.0)
    return w, lsz, lnb


def sb_pair_fwd(name, qkv):
    t = qkv.shape[0]

    def body(q_ref, k_ref, v_ref, o_ref):
        later = (_iota2((CS, CS), 0) > _iota2((CS, CS), 1)).astype(BF16)
        lanes = _head_lanes((BQ, PAIR))

        def qblock(i, _):
            q = q_ref[_qrows(i), :]
            q_heads = [_only_head(q, lanes[h]) for h in range(2)]

            def step(kc, state, dchunk):
                k = k_ref[_krows(kc), :]
                v = v_ref[_krows(kc), :]
                out = []
                for h in range(2):
                    carry, acc = state[h]
                    w, _, lnb = _sb_chunk_weights(q_heads[h], k, later, carry, dchunk)
                    out.append((carry + jnp.sum(lnb, axis=1, keepdims=True), acc + _dot(w, v, NN)))
                return tuple(out)

            zero = (jnp.zeros((BQ, 1), F32), jnp.zeros((BQ, PAIR), F32))
            (_, acc0), (_, acc1) = _walk_chunks(i, step, (zero, zero), True)
            o_ref[_qrows(i), :] = jnp.where(lanes[0], acc0, acc1).astype(BF16)
            return 0

        lax.fori_loop(0, t // BQ, qblock, 0)

    return pl.pallas_call(
        body, name=name, grid=(N_PAIRS,),
        in_specs=[_pair_spec(t, 0), _pair_spec(t, N_PAIRS), _pair_spec(t, 2 * N_PAIRS)],
        out_specs=_pair_spec(t, 0), out_shape=jax.ShapeDtypeStruct((t, WIDTH), BF16),
        compiler_params=_params(("parallel",)),
    )(qkv, qkv, qkv)


def _emit_dqkv(res, o_ref):
    o_ref[...] = res[pl.program_id(1)]


def _flush_transposed(acc, res, which):
    for kc in range(acc.shape[0]):
        res[which, kc * CS:(kc + 1) * CS, :] = acc[kc].T.astype(BF16)


def sb_pair_bwd(name, qkv, dy, dqkv):
    t = qkv.shape[0]
    nc = t // CS

    def body(q_ref, k_ref, v_ref, do_ref, _, o_ref, g_s, b_s, dkt_acc, dvt_acc, res):
        @pl.when(pl.program_id(1) == 0)
        def _():
            later = (_iota2((CS, CS), 0) > _iota2((CS, CS), 1)).astype(BF16)
            earlier = (_iota2((CS, CS), 0) < _iota2((CS, CS), 1)).astype(BF16)
            lanes = _head_lanes((BQ, PAIR))
            dkt_acc[...] = jnp.zeros_like(dkt_acc)
            dvt_acc[...] = jnp.zeros_like(dvt_acc)

            def qblock(i, _):
                q = q_ref[_qrows(i), :]
                do = do_ref[_qrows(i), :]
                q_heads = [_only_head(q, lanes[h]) for h in range(2)]
                do_heads = [_only_head(do, lanes[h]) for h in range(2)]
                qt_heads = [qh.astype(F32).T.astype(BF16) for qh in q_heads]
                dot_heads = [dh.astype(F32).T.astype(BF16) for dh in do_heads]

                def step1(kc, carries, dchunk):
                    k = k_ref[_krows(kc), :]
                    v = v_ref[_krows(kc), :]
                    out = []
                    for h in range(2):
                        w, lsz, lnb = _sb_chunk_weights(q_heads[h], k, later, carries[h], dchunk)
                        g_s[h, kc] = (w * _dot(do_heads[h], v, NT)).astype(BF16)
                        b_s[h, kc] = jnp.exp(lsz).astype(BF16)
                        dvt_acc[kc] += _dot(dot_heads[h], w, NN)
                        out.append(carries[h] + jnp.sum(lnb, axis=1, keepdims=True))
                    return tuple(out)

                zero = jnp.zeros((BQ, 1), F32)
                _walk_chunks(i, step1, (zero, zero), True)

                def step2(kc, state, dchunk):
                    k = k_ref[_krows(kc), :]
                    out = []
                    for h in range(2):
                        before, dq = state[h]
                        g16 = g_s[h, kc]
                        g = g16.astype(F32)
                        beta = b_s[h, kc].astype(F32)
                        prefix = lax.dot_general(g16, earlier, (NN, ((), ())), preferred_element_type=F32) + before
                        dz = g * (1.0 - beta) - beta * prefix
                        if dchunk is not None:
                            dz = jnp.where(_diag_mask(dchunk, False), dz, 0.0)
                        dzb = dz.astype(BF16)
                        dkt_acc[kc] += _dot(qt_heads[h], dzb, NN)
                        out.append((before + jnp.sum(g, axis=1, keepdims=True), dq + _dot(dzb, k, NN)))
                    return tuple(out)

                start = (zero, jnp.zeros((BQ, PAIR), F32))
                (_, dq0), (_, dq1) = _walk_chunks(i, step2, (start, start), False)
                res[0, _qrows(i), :] = (jnp.where(lanes[0], dq0, dq1) * ATTN_SCALE).astype(BF16)
                return 0

            lax.fori_loop(0, t // BQ, qblock, 0)
            _flush_transposed(dkt_acc, res, 1)
            _flush_transposed(dvt_acc, res, 2)

        _emit_dqkv(res, o_ref)

    return pl.pallas_call(
        body, name=name, grid=(N_PAIRS, 3),
        in_specs=[_pair_spec(t, 0), _pair_spec(t, N_PAIRS), _pair_spec(t, 2 * N_PAIRS), _pair_spec(t, 0), _ANY],
        out_specs=pl.BlockSpec((t, PAIR), lambda p, s: (0, s * N_PAIRS + p)),
        out_shape=jax.ShapeDtypeStruct(dqkv.shape, BF16), input_output_aliases={4: 0},
        scratch_shapes=[pltpu.VMEM((2, nc, BQ, CS), BF16), pltpu.VMEM((2, nc, BQ, CS), BF16),
                        pltpu.VMEM((nc, PAIR, CS), F32), pltpu.VMEM((nc, PAIR, CS), F32), pltpu.VMEM((3, t, PAIR), BF16)],
        compiler_params=_params(("parallel", "arbitrary")),
    )(qkv, qkv, qkv, dy, dqkv)


def _gates_col_spec(t):
    return pl.BlockSpec((2, t, 1), lambda p, *_: (p, 0, 0))


def _gates_row_spec(nc):
    return pl.BlockSpec((2, nc, 1, CS), lambda p, *_: (p, 0, 0, 0))


def fox_pair_fwd(name, qkv, c_col, c_row):
    t = qkv.shape[0]

    def body(q_ref, k_ref, v_ref, cc_ref, cr_ref, o_ref, lse_ref):
        lanes = _head_lanes((BQ, PAIR))

        def qblock(i, _):
            q = q_ref[_qrows(i), :]
            q_heads = [_only_head(q, lanes[h]) for h in range(2)]
            ct = [cc_ref[h, _qrows(i), :] for h in range(2)]

            def step(kc, state, dchunk):
                k = k_ref[_krows(kc), :]
                v = v_ref[_krows(kc), :]
                out = []
                for h in range(2):
                    m, l, acc = state[h]
                    s = _dot(q_heads[h], k, NT) + ct[h] - cr_ref[h, kc]
                    if dchunk is not None:
                        s = jnp.where(_diag_mask(dchunk, True), s, NEG_BIG)
                    m_new = jnp.maximum(m, jnp.max(s, axis=1, keepdims=True))
                    alpha = jnp.exp(m - m_new)
                    p = jnp.exp(s - m_new)
                    if dchunk is not None:
                        p = jnp.where(_diag_mask(dchunk, True), p, 0.0)
                    out.append((m_new, l * alpha + jnp.sum(p, axis=1, keepdims=True), acc * alpha + _dot(p, v, NN)))
                return tuple(out)

            init = (jnp.full((BQ, 1), NEG_BIG, F32), jnp.zeros((BQ, 1), F32), jnp.zeros((BQ, PAIR), F32))
            (m0, l0, acc0), (m1, l1, acc1) = _walk_chunks(i, step, (init, init), False)
            o_ref[_qrows(i), :] = jnp.where(lanes[0], acc0 / l0, acc1 / l1).astype(BF16)
            lse_ref[0, _qrows(i), :] = m0 + jnp.log(l0)
            lse_ref[1, _qrows(i), :] = m1 + jnp.log(l1)
            return 0

        lax.fori_loop(0, t // BQ, qblock, 0)

    return pl.pallas_call(
        body, name=name, grid=(N_PAIRS,),
        in_specs=[_pair_spec(t, 3 * N_PAIRS), _pair_spec(t, 4 * N_PAIRS), _pair_spec(t, 5 * N_PAIRS),
                  _gates_col_spec(t), _gates_row_spec(t // CS)],
        out_specs=[_pair_spec(t, 0), _gates_col_spec(t)],
        out_shape=[jax.ShapeDtypeStruct((t, WIDTH), BF16), jax.ShapeDtypeStruct((N_HEADS, t, 1), F32)],
        compiler_params=_params(("parallel",)),
    )(qkv, qkv, qkv, c_col, c_row)


def fox_pair_bwd(name, qkv, y, dy, lse, c_col, c_row, dqkv):
    t = qkv.shape[0]
    nc = t // CS

    def body(q_ref, k_ref, v_ref, o_in_ref, do_ref, lse_ref, cc_ref, cr_ref, _, o_ref, dcc_ref, dcr_ref,
             dkt_acc, dvt_acc, dcr_acc, res):
        @pl.when(pl.program_id(1) == 0)
        def _():
            lanes = _head_lanes((BQ, PAIR))
            dkt_acc[...] = jnp.zeros_like(dkt_acc)
            dvt_acc[...] = jnp.zeros_like(dvt_acc)
            dcr_acc[...] = jnp.zeros_like(dcr_acc)

            def qblock(i, _):
                q = q_ref[_qrows(i), :]
                do = do_ref[_qrows(i), :]
                q_heads = [_only_head(q, lanes[h]) for h in range(2)]
                do_heads = [_only_head(do, lanes[h]) for h in range(2)]
                qt_heads = [qh.astype(F32).T.astype(BF16) for qh in q_heads]
                dot_heads = [dh.astype(F32).T.astype(BF16) for dh in do_heads]
                prod = do.astype(F32) * o_in_ref[_qrows(i), :].astype(F32)
                delta = [jnp.sum(_only_head(prod, lanes[h]), axis=1, keepdims=True) for h in range(2)]
                ct = [cc_ref[h, _qrows(i), :] for h in range(2)]
                lse_i = [lse_ref[h, _qrows(i), :] for h in range(2)]

                def step(kc, state, dchunk):
                    k = k_ref[_krows(kc), :]
                    v = v_ref[_krows(kc), :]
                    out = []
                    for h in range(2):
                        dq, dct = state[h]
                        s = _dot(q_heads[h], k, NT) + ct[h] - cr_ref[h, kc]
                        p = jnp.exp(s - lse_i[h])
                        if dchunk is not None:
                            p = jnp.where(_diag_mask(dchunk, True), p, 0.0)
                        ds = p * (_dot(do_heads[h], v, NT) - delta[h])
                        dvt_acc[kc] += _dot(dot_heads[h], p, NN)
                        dsb = ds.astype(BF16)
                        dkt_acc[kc] += _dot(qt_heads[h], dsb, NN)
                        dcr_acc[h, kc] -= jnp.sum(ds, axis=0, keepdims=True)
                        out.append((dq + _dot(dsb, k, NN), dct + jnp.sum(ds, axis=1, keepdims=True)))
                    return tuple(out)

                zero = (jnp.zeros((BQ, PAIR), F32), jnp.zeros((BQ, 1), F32))
                (dq0, dct0), (dq1, dct1) = _walk_chunks(i, step, (zero, zero), False)
                res[0, _qrows(i), :] = (jnp.where(lanes[0], dq0, dq1) * ATTN_SCALE).astype(BF16)
                dcc_ref[0, _qrows(i), :] = dct0
                dcc_ref[1, _qrows(i), :] = dct1
                return 0

            lax.fori_loop(0, t // BQ, qblock, 0)
            _flush_transposed(dkt_acc, res, 1)
            _flush_transposed(dvt_acc, res, 2)
            dcr_ref[...] = dcr_acc[...]

        _emit_dqkv(res, o_ref)

    return pl.pallas_call(
        body, name=name, grid=(N_PAIRS, 3),
        in_specs=[_pair_spec(t, 3 * N_PAIRS), _pair_spec(t, 4 * N_PAIRS), _pair_spec(t, 5 * N_PAIRS), _pair_spec(t, 0),
                  _pair_spec(t, 0), _gates_col_spec(t), _gates_col_spec(t), _gates_row_spec(nc), _ANY],
        out_specs=[pl.BlockSpec((t, PAIR), lambda p, s: (0, (3 + s) * N_PAIRS + p)), _gates_col_spec(t), _gates_row_spec(nc)],
        out_shape=[jax.ShapeDtypeStruct(dqkv.shape, BF16), jax.ShapeDtypeStruct((N_HEADS, t, 1), F32),
                   jax.ShapeDtypeStruct((N_HEADS, nc, 1, CS), F32)],
        input_output_aliases={8: 0},
        scratch_shapes=[pltpu.VMEM((nc, PAIR, CS), F32), pltpu.VMEM((nc, PAIR, CS), F32), pltpu.VMEM((2, nc, 1, CS), F32),
                        pltpu.VMEM((3, t, PAIR), BF16)],
        compiler_params=_params(("parallel", "arbitrary")),
    )(qkv, qkv, qkv, y, dy, lse, c_col, c_row, dqkv)


def loss_head(name, x, g, target):
    t, d = x.shape
    tr = _row_tile(t, 256)

    def body(x_ref, g_ref, t_ref, dx_ref, gn_ref, loss_ref):
        xv = x_ref[...]
        r = lax.rsqrt(jnp.mean(xv * xv, axis=-1, keepdims=True) + RMS_EPS)
        xhat = xv * r
        gv = g_ref[...]
        err = xhat * gv - t_ref[...]
        part_loss = 0.5 * jnp.sum(jnp.mean(err * err, axis=-1, keepdims=True), axis=0, keepdims=True)
        dy = err * (1.0 / d)
        dyg = dy * gv
        dx_ref[...] = r * (dyg - xhat * jnp.mean(dyg * xhat, axis=-1, keepdims=True))
        part_g = jnp.sum(dy * xhat, axis=0, keepdims=True)

        @pl.when(pl.program_id(0) == 0)
        def _():
            gn_ref[...] = part_g
            loss_ref[...] = part_loss

        @pl.when(pl.program_id(0) != 0)
        def _():
            gn_ref[...] += part_g
            loss_ref[...] += part_loss

    row = pl.BlockSpec((tr, d), lambda i: (i, 0))
    return pl.pallas_call(
        body, name=name, grid=(t // tr,),
        in_specs=[row, pl.BlockSpec((1, d), lambda i: (0, 0)), row],
        out_specs=[row, pl.BlockSpec((1, d), lambda i: (0, 0)), pl.BlockSpec((1, 1), lambda i: (0, 0))],
        out_shape=[jax.ShapeDtypeStruct((t, d), F32), jax.ShapeDtypeStruct((1, d), F32), jax.ShapeDtypeStruct((1, 1), F32)],
        compiler_params=_params(("arbitrary",)),
    )(x, g, target)


def _place():
    return lax.axis_index("x"), lax.axis_index("y"), lax.axis_index("c")


def _other_chips(x, y):
    return [(1 - x, y), (x, 1 - y), (1 - x, 1 - y)]


def _half(ref, c, rows):
    return ref.at[:, pl.ds(c * (rows // 2), rows // 2), :]


_ANY = pl.BlockSpec(memory_space=pl.ANY)


def gather_weights(name, bufs):
    n = len(bufs)

    def body(*refs):
        outs = refs[n:2 * n]
        send_sems, recv_sems = refs[2 * n:]
        x, y, c = _place()
        chips = _other_chips(x, y)
        me = 2 * x + y
        sibling = (x, y, 1 - c)
        first, passed = [], []
        for i in range(n):
            rows = outs[i].shape[2]
            mine = _half(outs[i].at[me], c, rows)
            for j, (qx, qy) in enumerate(chips):
                k = 6 * i + j
                rc = pltpu.make_async_remote_copy(
                    src_ref=mine, dst_ref=mine,
                    send_sem=send_sems.at[k], recv_sem=recv_sems.at[k], device_id=(qx, qy, c), device_id_type=MESH)
                rc.start()
                first.append(rc)
        for i in range(n):
            rows = outs[i].shape[2]
            for j, (qx, qy) in enumerate(chips):
                k = 6 * i + j
                block = _half(outs[i].at[2 * qx + qy], c, rows)
                pltpu.make_async_remote_copy(
                    src_ref=block, dst_ref=block, send_sem=send_sems.at[k], recv_sem=recv_sems.at[k],
                    device_id=(qx, qy, c), device_id_type=MESH).wait_recv()
                fw = pltpu.make_async_remote_copy(
                    src_ref=block, dst_ref=block, send_sem=send_sems.at[k + 3], recv_sem=recv_sems.at[k + 3],
                    device_id=sibling, device_id_type=MESH)
                fw.start()
                passed.append(fw)
        for i in range(n):
            rows = outs[i].shape[2]
            for j, (qx, qy) in enumerate(chips):
                k = 6 * i + j + 3
                block = _half(outs[i].at[2 * qx + qy], 1 - c, rows)
                pltpu.make_async_remote_copy(
                    src_ref=block, dst_ref=block, send_sem=send_sems.at[k], recv_sem=recv_sems.at[k],
                    device_id=sibling, device_id_type=MESH).wait_recv()
        for cp in first + passed:
            cp.wait_send()

    return pl.pallas_call(
        body, name=name, in_specs=[_ANY] * n, out_specs=[_ANY] * n,
        out_shape=[jax.ShapeDtypeStruct(b.shape, b.dtype) for b in bufs],
        input_output_aliases={i: i for i in range(n)},
        scratch_shapes=[pltpu.SemaphoreType.DMA((6 * n,)), pltpu.SemaphoreType.DMA((6 * n,))],
        compiler_params=pltpu.CompilerParams(has_side_effects=True),
    )(*bufs)


_HBM = pl.BlockSpec(memory_space=pltpu.HBM)
_SEM = pl.BlockSpec(memory_space=pltpu.SEMAPHORE)
_DATAFLOW = pltpu.SideEffectType.DATAFLOW_SIDE_EFFECTING


def _in_hbm(a):
    return pltpu.with_memory_space_constraint(a, pltpu.HBM)


def _gather_ici_copies(bufs, send_sems, recv_sems, arrivals):
    x, y, c = _place()
    me = 2 * x + y
    copies = []
    for i, buf in enumerate(bufs):
        rows = buf.shape[2]
        for j, (qx, qy) in enumerate(_other_chips(x, y)):
            block = _half(buf.at[2 * qx + qy if arrivals else me], c, rows)
            copies.append(pltpu.make_async_remote_copy(
                src_ref=block, dst_ref=block, send_sem=send_sems.at[3 * i + j], recv_sem=recv_sems.at[3 * i + j],
                device_id=(qx, qy, c), device_id_type=MESH))
    return copies


def gather_ici_start(name, bufs, after):
    n = len(bufs)

    def body(*refs):
        ins = refs[:n]
        send_sems, recv_sems = refs[n + 1], refs[n + 2]
        token = refs[-1]
        for send in _gather_ici_copies(ins, send_sems, recv_sems, False):
            send.start()
        token[...] = jnp.zeros_like(token)

    res = pl.pallas_call(
        body, name=name,
        out_shape=(pltpu.SemaphoreType.DMA((3 * n,)), pltpu.SemaphoreType.DMA((3 * n,)), *[pltpu.HBM(b.shape, b.dtype) for b in bufs],
                   jax.ShapeDtypeStruct((8, 128), F32)),
        in_specs=[_HBM] * n + [_ANY], out_specs=(_SEM, _SEM, *[_HBM] * n, pl.BlockSpec(memory_space=pltpu.VMEM)),
        input_output_aliases={i: 2 + i for i in range(n)},
        compiler_params=pltpu.CompilerParams(has_side_effects=_DATAFLOW),
    )(*[_in_hbm(b) for b in bufs], after)
    return res[0], res[1], list(res[2:2 + n]), res[-1]


def gather_ici_wait(name, send_sems, recv_sems, bufs, after):
    n = len(bufs)

    def body(*refs):
        ins = refs[:n]
        send_sems_ref, recv_sems_ref = refs[n], refs[n + 1]
        for send in _gather_ici_copies(ins, send_sems_ref, recv_sems_ref, False):
            send.wait_send()
        for recv in _gather_ici_copies(ins, send_sems_ref, recv_sems_ref, True):
            recv.wait_recv()

    return pl.pallas_call(
        body, name=name, out_shape=tuple(pltpu.HBM(b.shape, b.dtype) for b in bufs),
        in_specs=[_HBM] * n + [_SEM, _SEM, _ANY], out_specs=tuple([_HBM] * n),
        input_output_aliases={i: i for i in range(n)},
        compiler_params=pltpu.CompilerParams(has_side_effects=_DATAFLOW),
    )(*bufs, send_sems, recv_sems, after)


def gather_forward(name, bufs):
    n = len(bufs)

    def body(*refs):
        outs = refs[n:2 * n]
        send_sems, recv_sems = refs[2 * n:]
        x, y, c = _place()
        sibling = (x, y, 1 - c)
        sends = []
        for i in range(n):
            rows = outs[i].shape[2]
            for j, (qx, qy) in enumerate(_other_chips(x, y)):
                block = _half(outs[i].at[2 * qx + qy], c, rows)
                fw = pltpu.make_async_remote_copy(
                    src_ref=block, dst_ref=block, send_sem=send_sems.at[3 * i + j], recv_sem=recv_sems.at[3 * i + j],
                    device_id=sibling, device_id_type=MESH)
                fw.start()
                sends.append(fw)
        for i in range(n):
            rows = outs[i].shape[2]
            for j, (qx, qy) in enumerate(_other_chips(x, y)):
                block = _half(outs[i].at[2 * qx + qy], 1 - c, rows)
                pltpu.make_async_remote_copy(
                    src_ref=block, dst_ref=block, send_sem=send_sems.at[3 * i + j], recv_sem=recv_sems.at[3 * i + j],
                    device_id=sibling, device_id_type=MESH).wait_recv()
        for fw in sends:
            fw.wait_send()

    return pl.pallas_call(
        body, name=name, in_specs=[_ANY] * n, out_specs=[_ANY] * n,
        out_shape=[jax.ShapeDtypeStruct(b.shape, b.dtype) for b in bufs],
        input_output_aliases={i: i for i in range(n)},
        scratch_shapes=[pltpu.SemaphoreType.DMA((3 * n,)), pltpu.SemaphoreType.DMA((3 * n,))],
        compiler_params=pltpu.CompilerParams(has_side_effects=True),
    )(*bufs)


def _between_chips_copies(parts, lands, send_sems, recv_sems):
    x, y, c = _place()
    copies = []
    for i, (part, land) in enumerate(zip(parts, lands)):
        for j, (qx, qy) in enumerate(_other_chips(x, y)):
            copies.append(pltpu.make_async_remote_copy(
                src_ref=part.at[2 * qx + qy], dst_ref=land.at[j], send_sem=send_sems.at[3 * i + j], recv_sem=recv_sems.at[3 * i + j],
                device_id=(qx, qy, c), device_id_type=MESH))
    return copies


def between_chips_start(name, parts):
    n = len(parts)
    lands = [lax.empty((N_CHIPS - 1,) + p.shape[1:], p.dtype) for p in parts]

    def body(*refs):
        send_sems, recv_sems = refs[2 * n], refs[2 * n + 1]
        token = refs[-1]
        for cp in _between_chips_copies(refs[:n], refs[n:2 * n], send_sems, recv_sems):
            cp.start()
        token[...] = jnp.zeros_like(token)

    res = pl.pallas_call(
        body, name=name,
        out_shape=(pltpu.SemaphoreType.DMA((3 * n,)), pltpu.SemaphoreType.DMA((3 * n,)),
                   *[pltpu.HBM(a.shape, a.dtype) for a in parts + lands], jax.ShapeDtypeStruct((8, 128), F32)),
        in_specs=[_HBM] * (2 * n), out_specs=(_SEM, _SEM, *[_HBM] * (2 * n), pl.BlockSpec(memory_space=pltpu.VMEM)),
        input_output_aliases={i: 2 + i for i in range(2 * n)},
        compiler_params=pltpu.CompilerParams(has_side_effects=_DATAFLOW),
    )(*[_in_hbm(a) for a in parts + lands])
    return res[0], res[1], list(res[2:2 + n]), list(res[2 + n:2 + 2 * n]), res[-1]


def between_chips_wait(name, send_sems, recv_sems, parts, lands, after):
    n = len(parts)

    def body(*refs):
        for cp in _between_chips_copies(refs[:n], refs[n:2 * n], refs[2 * n], refs[2 * n + 1]):
            cp.wait_send()
            cp.wait_recv()

    res = pl.pallas_call(
        body, name=name, out_shape=tuple(pltpu.HBM(a.shape, a.dtype) for a in parts + lands),
        in_specs=[_HBM] * (2 * n) + [_SEM, _SEM, _ANY], out_specs=tuple([_HBM] * (2 * n)),
        input_output_aliases={i: i for i in range(2 * n)},
        compiler_params=pltpu.CompilerParams(has_side_effects=_DATAFLOW),
    )(*parts, *lands, send_sems, recv_sems, after)
    return list(res[:n]), list(res[n:])


def send_half_to_sibling(name, grads):
    n = len(grads)

    def body(*refs):
        srcs, outs = refs[:n], refs[n:2 * n]
        send_sems, recv_sems = refs[2 * n:]
        x, y, c = _place()
        sibling = (x, y, 1 - c)
        copies = []
        for i in range(n):
            rows = srcs[i].shape[2]
            rc = pltpu.make_async_remote_copy(
                src_ref=srcs[i].at[:, :, pl.ds((1 - c) * (rows // 2), rows // 2), :], dst_ref=outs[i],
                send_sem=send_sems.at[i], recv_sem=recv_sems.at[i], device_id=sibling, device_id_type=MESH)
            rc.start()
            copies.append(rc)
        for rc in copies:
            rc.wait()

    def half_shape(g):
        s = g.shape
        return jax.ShapeDtypeStruct((s[0], s[1], s[2] // 2, s[3]), g.dtype)

    return pl.pallas_call(
        body, name=name, in_specs=[_ANY] * n, out_specs=[_ANY] * n, out_shape=[half_shape(g) for g in grads],
        scratch_shapes=[pltpu.SemaphoreType.DMA((n,)), pltpu.SemaphoreType.DMA((n,))],
        compiler_params=pltpu.CompilerParams(has_side_effects=True),
    )(*grads)


def exchange_between_chips(name, parts):
    n = len(parts)

    def body(*refs):
        srcs, outs = refs[:n], refs[n:2 * n]
        send_sems, recv_sems = refs[2 * n:]
        x, y, c = _place()
        chips = _other_chips(x, y)
        copies = []
        for i in range(n):
            for j, (qx, qy) in enumerate(chips):
                k = 3 * i + j
                rc = pltpu.make_async_remote_copy(
                    src_ref=srcs[i].at[2 * qx + qy], dst_ref=outs[i].at[j],
                    send_sem=send_sems.at[k], recv_sem=recv_sems.at[k], device_id=(qx, qy, c), device_id_type=MESH)
                rc.start()
                copies.append(rc)
        for rc in copies:
            rc.wait()

    return pl.pallas_call(
        body, name=name, in_specs=[_ANY] * n, out_specs=[_ANY] * n,
        out_shape=[jax.ShapeDtypeStruct((N_CHIPS - 1,) + p.shape[1:], p.dtype) for p in parts],
        scratch_shapes=[pltpu.SemaphoreType.DMA((3 * n,)), pltpu.SemaphoreType.DMA((3 * n,))],
        compiler_params=pltpu.CompilerParams(has_side_effects=True),
    )(*parts)


def share_halves(name, bufs):
    n = len(bufs)

    def body(*refs):
        outs = refs[n:2 * n]
        send_sems, recv_sems = refs[2 * n:]
        x, y, c = _place()
        copies = []
        for i in range(n):
            mine = _half(outs[i], c, outs[i].shape[1])
            rc = pltpu.make_async_remote_copy(
                src_ref=mine, dst_ref=mine, send_sem=send_sems.at[i], recv_sem=recv_sems.at[i],
                device_id=(x, y, 1 - c), device_id_type=MESH)
            rc.start()
            copies.append(rc)
        for i in range(n):
            theirs = _half(outs[i], 1 - c, outs[i].shape[1])
            pltpu.make_async_remote_copy(
                src_ref=theirs, dst_ref=theirs, send_sem=send_sems.at[i], recv_sem=recv_sems.at[i],
                device_id=(x, y, 1 - c), device_id_type=MESH).wait_recv()
        for rc in copies:
            rc.wait_send()

    return pl.pallas_call(
        body, name=name, in_specs=[_ANY] * n, out_specs=[_ANY] * n,
        out_shape=[jax.ShapeDtypeStruct(b.shape, b.dtype) for b in bufs],
        input_output_aliases={i: i for i in range(n)},
        scratch_shapes=[pltpu.SemaphoreType.DMA((n,)), pltpu.SemaphoreType.DMA((n,))],
        compiler_params=pltpu.CompilerParams(has_side_effects=True),
    )(*bufs)


def pair_sum(name, grad, recv, c):
    ns, na, rh, cols = recv.shape
    tr = _row_tile(rh, 256) if rh % 256 == 0 else rh
    nt = rh // tr

    def body(c_ref, g_ref, r_ref, o_ref):
        o_ref[...] = (g_ref[...].astype(F32) + r_ref[...].astype(F32)).astype(BF16)

    blk = (None, None, tr, cols)
    return pl.pallas_call(
        body, name=name,
        grid_spec=pltpu.PrefetchScalarGridSpec(
            num_scalar_prefetch=1, grid=(ns, na, nt),
            in_specs=[pl.BlockSpec(blk, lambda s, a, r, c_ref: (s, a, c_ref[0] * nt + r, 0)),
                      pl.BlockSpec(blk, lambda s, a, r, c_ref: (s, a, r, 0))],
            out_specs=pl.BlockSpec(blk, lambda s, a, r, c_ref: (s, a, r, 0))),
        out_shape=jax.ShapeDtypeStruct(recv.shape, BF16),
        compiler_params=_params(("parallel", "parallel", "parallel")),
    )(c, grad, recv)


def chip_sum(name, parts, landed, place):
    _, na, rh, cols = parts.shape
    tr = _row_tile(rh, 256) if rh % 256 == 0 else rh
    nt = rh // tr

    def body(place_ref, p_ref, l_ref, o_ref):
        total = p_ref[...].astype(F32)
        for s in range(N_CHIPS - 1):
            total = total + l_ref[s].astype(F32)
        o_ref[...] = total

    return pl.pallas_call(
        body, name=name,
        grid_spec=pltpu.PrefetchScalarGridSpec(
            num_scalar_prefetch=1, grid=(na, nt),
            in_specs=[pl.BlockSpec((None, None, tr, cols), lambda a, r, pr: (pr[1], a, r, 0)),
                      pl.BlockSpec((N_CHIPS - 1, None, tr, cols), lambda a, r, pr: (0, a, r, 0))],
            out_specs=pl.BlockSpec((None, tr, cols), lambda a, r, pr: (a, pr[0] * nt + r, 0))),
        out_shape=jax.ShapeDtypeStruct((na, 2 * rh, cols), F32),
        compiler_params=_params(("parallel", "parallel")),
    )(place, parts, landed)


def reduce_scatter_begin(tag, grads, place):
    recv = send_half_to_sibling(f"{tag}_to_sibling", grads)
    parts = [pair_sum(f"{tag}_pair_sum{i}", g, r, place) for i, (g, r) in enumerate(zip(grads, recv))]
    send_sems, recv_sems, parts, lands, token = between_chips_start(f"{tag}_between_chips_start", parts)
    return (send_sems, recv_sems, parts, lands), token


def reduce_scatter_end(tag, state, place, after):
    send_sems, recv_sems, parts, lands = state
    parts, landed = between_chips_wait(f"{tag}_between_chips_wait", send_sems, recv_sems, parts, lands, after)
    halves = [chip_sum(f"{tag}_chip_sum{i}", p, l, place) for i, (p, l) in enumerate(zip(parts, landed))]
    return share_halves(f"{tag}_share", halves)


def _adamw_math(w, g, m, v):
    m = ADAM_B1 * m + (1.0 - ADAM_B1) * g
    v = ADAM_B2 * v + (1.0 - ADAM_B2) * (g * g)
    m_hat = m / (1.0 - ADAM_B1 ** ADAM_STEP)
    v_hat = v / (1.0 - ADAM_B2 ** ADAM_STEP)
    delta = -ADAM_LR * (m_hat / (jnp.sqrt(v_hat) + ADAM_EPS) + ADAM_WD * w)
    return delta, m, v


def adamw(name, w, g, m, v, after=()):
    rows, cols = w.shape
    tr = _row_tile(rows, 256) if rows % 256 == 0 else rows // 2

    def body(w_ref, g_ref, m_ref, v_ref, d_ref, mo_ref, vo_ref):
        d_ref[...], mo_ref[...], vo_ref[...] = _adamw_math(w_ref[...], g_ref[...], m_ref[...], v_ref[...])

    blk = pl.BlockSpec((tr, cols), lambda i: (i, 0))
    return pl.pallas_call(
        _ordered_after(body, 4, len(after)), name=name, grid=(rows // tr,), in_specs=[blk] * 4 + [_ANY] * len(after),
        out_specs=[blk] * 3, out_shape=[jax.ShapeDtypeStruct(w.shape, F32)] * 3, compiler_params=_params(("parallel",)),
    )(w, g, m, v, *after)


def adamw_stacked(name, ws, g, ms, vs, after=()):
    n = len(ws)
    rows, cols = ws[0].shape
    tr = next(r for r in (128, 88, 64, 32, 16, 8) if rows % r == 0)

    def body(*refs):
        w_refs, m_refs, v_refs, g_ref = refs[:n], refs[n:2 * n], refs[2 * n:3 * n], refs[3 * n]
        outs = refs[3 * n + 1:]
        for i in range(n):
            outs[i][...], outs[n + i][...], outs[2 * n + i][...] = _adamw_math(
                w_refs[i][...], g_ref[i], m_refs[i][...], v_refs[i][...])

    blk = pl.BlockSpec((tr, cols), lambda r: (r, 0))
    res = pl.pallas_call(
        _ordered_after(body, 3 * n + 1, len(after)), name=name, grid=(rows // tr,),
        in_specs=[blk] * (3 * n) + [pl.BlockSpec((n, tr, cols), lambda r: (0, r, 0))] + [_ANY] * len(after),
        out_specs=[blk] * (3 * n),
        out_shape=[jax.ShapeDtypeStruct((rows, cols), F32)] * (3 * n), compiler_params=_params(("parallel",)),
    )(*ws, *ms, *vs, g, *after)
    return res[:n], res[n:2 * n], res[2 * n:]


def small_allreduce_adamw(name, g_part, w, m, v):
    rows, cols = g_part.shape

    def body(g_ref, w_ref, m_ref, v_ref, sum_ref, d_ref, mo_ref, vo_ref, land, send_sems, recv_sems):
        x, y, c = _place()
        me = 4 * x + 2 * y + c
        land[me] = g_ref[...]
        copies = []
        for r in range(1, 8):
            peer = (x ^ (r >> 2), y ^ ((r >> 1) & 1), c ^ (r & 1))
            rc = pltpu.make_async_remote_copy(
                src_ref=g_ref, dst_ref=land.at[me], send_sem=send_sems.at[r - 1], recv_sem=recv_sems.at[r - 1],
                device_id=peer, device_id_type=MESH)
            rc.start()
            copies.append(rc)
        for rc in copies:
            rc.wait()
        total = land[0]
        for s in range(1, 8):
            total = total + land[s]
        sum_ref[...] = total
        d_ref[...], mo_ref[...], vo_ref[...] = _adamw_math(w_ref[...], total, m_ref[...], v_ref[...])

    vmem = pl.BlockSpec(memory_space=pltpu.VMEM)
    return pl.pallas_call(
        body, name=name, in_specs=[vmem] * 4, out_specs=[vmem] * 4,
        out_shape=[jax.ShapeDtypeStruct((rows, cols), F32)] * 4,
        scratch_shapes=[pltpu.VMEM((8, rows, cols), F32), pltpu.SemaphoreType.DMA((7,)), pltpu.SemaphoreType.DMA((7,))],
        compiler_params=pltpu.CompilerParams(has_side_effects=True),
    )(g_part, w, m, v)


def _heads(a):
    t, w = a.shape
    return a.reshape(t, w // HEAD_DIM, HEAD_DIM).transpose(1, 0, 2)


def _unheads(a):
    n, t, _ = a.shape
    return a.transpose(1, 0, 2).reshape(t, n * HEAD_DIM)


def _cols_from_shards(g):
    ns, r, cols = g.shape
    return g.transpose(1, 0, 2).reshape(r, ns * cols)


def _shards_from_cols(a):
    r, cols = a.shape
    return a.reshape(r, N_CHIPS, cols // N_CHIPS).transpose(1, 0, 2)


def kernel(x, norm_ffn1, w_ffn1_gate, w_ffn1_up, w_ffn1_down, norm_mix, w_in, b_forget, w_gate, b_gate, w_up_a, w_up_b, w_out, norm_ffn2, w_ffn2_gate, w_ffn2_up, w_ffn2_down, norm_final, loss_target, m_norm_ffn1, m_w_ffn1_gate, m_w_ffn1_up, m_w_ffn1_down, m_norm_mix, m_w_in, m_b_forget, m_w_gate, m_b_gate, m_w_up_a, m_w_up_b, m_w_out, m_norm_ffn2, m_w_ffn2_gate, m_w_ffn2_up, m_w_ffn2_down, m_norm_final, v_norm_ffn1, v_w_ffn1_gate, v_w_ffn1_up, v_w_ffn1_down, v_norm_mix, v_w_in, v_b_forget, v_w_gate, v_b_gate, v_w_up_a, v_w_up_b, v_w_out, v_norm_ffn2, v_w_ffn2_gate, v_w_ffn2_up, v_w_ffn2_down, v_norm_final):
    t, d = x.shape[1], x.shape[2]
    in4 = w_in.shape[2]
    gate4 = w_gate.shape[2]
    up4 = w_up_a.shape[2]
    in_cols = N_CHIPS * in4
    n_forget = in_cols - QKV_COLS
    assert w_up_a.shape[1] == WIDTH and d == 2 * WIDTH and n_forget == N_HEADS
    nq = t // BQ
    chip = 2 * lax.axis_index("x") + lax.axis_index("y")
    c_arr = jnp.stack([lax.axis_index("c"), chip]).astype(jnp.int32)
    x2d = x[0]
    target = loss_target[0]

    def slot(shard):
        return lax.dynamic_update_slice(jnp.zeros((N_CHIPS,) + shard.shape, BF16), shard.astype(BF16)[None], (chip, 0, 0, 0))

    def ffn_views(wg, wu, wd):
        return [wg[0].T, wu[0].T, wd[0]]

    ffn1_w, ffn1_m, ffn1_v = (ffn_views(w_ffn1_gate, w_ffn1_up, w_ffn1_down), ffn_views(m_w_ffn1_gate, m_w_ffn1_up, m_w_ffn1_down),
                              ffn_views(v_w_ffn1_gate, v_w_ffn1_up, v_w_ffn1_down))
    ffn2_w, ffn2_m, ffn2_v = (ffn_views(w_ffn2_gate, w_ffn2_up, w_ffn2_down), ffn_views(m_w_ffn2_gate, m_w_ffn2_up, m_w_ffn2_down),
                              ffn_views(v_w_ffn2_gate, v_w_ffn2_up, v_w_ffn2_down))
    mx_sh = slot(jnp.concatenate([w_in[0], w_gate[0], jnp.concatenate([w_up_a[0], w_up_b[0]], axis=0)], axis=1)[None])
    wo_sh = slot(w_out)
    (w3_1,) = gather_weights("gather_ffn1", [slot(jnp.stack(ffn1_w))])
    mx_send, mx_recv, mx_bufs, mx_token = gather_ici_start("gather_mixer_start", [mx_sh, wo_sh], w3_1)
    f2_send, f2_recv, f2_bufs, f2_token = gather_ici_start("gather_ffn2_start", [slot(jnp.stack(ffn2_w))], mx_token)

    x1, saved1 = ffn_forward("ffn1", x2d, norm_ffn1, w3_1, after=(mx_token, f2_token))
    mx, wo = gather_forward("gather_mixer_forward", gather_ici_wait("gather_mixer_wait", mx_send, mx_recv, mx_bufs, x1))
    mx = mx[:, 0]
    w_in_full = _cols_from_shards(mx[:, :, :in4])
    w_gate_full = _cols_from_shards(mx[:, :, in4:in4 + gate4])
    w_up_full = _cols_from_shards(mx[:, :, in4 + gate4:])
    w_up_a_full, w_up_b_full = w_up_full[:WIDTH], w_up_full[WIDTH:]
    wcat = jnp.concatenate([w_in_full[:, :QKV_COLS], w_gate_full, w_in_full[:, QKV_COLS:],
                            jnp.zeros((d, F_PAD - n_forget), BF16)], axis=1)
    bias_cat = jnp.concatenate([jnp.zeros((1, QKV_COLS), F32), b_gate, jnp.zeros((1, F_PAD), F32)], axis=1)
    f_off = QKV_COLS + 2 * d
    wo_full = wo.reshape(d, d)
    b_forget_row = jnp.pad(b_forget, ((0, 0), (0, QB - n_forget)))

    h2, rstd2 = rms_fwd("mix_rms", x1, norm_mix)
    qkv = proj("mix_proj_qkv", h2, wcat, bias_cat, 0, QKV_COLS, WIDTH, BF16, scaled_tiles=(0, 3))
    pc = proj("mix_proj_gates", h2, wcat, bias_cat, QKV_COLS, 2 * d + F_PAD, 768, F32)
    f_logit = pc[:, 2 * d:2 * d + QB]
    c_cum = fox_prep("fox_prep", f_logit, b_forget_row)
    c_heads = c_cum[:, :N_HEADS].T
    c_col = c_heads[:, :, None]
    c_row = c_heads.reshape(N_HEADS, t // CS, 1, CS)
    ya = sb_pair_fwd("sb_fwd", qkv)
    yb, lse = fox_pair_fwd("fox_fwd", qkv, c_col, c_row)
    ua, ub, mixed = mix_fwd("mix_fwd", ya, yb, w_up_a_full, w_up_b_full, pc)
    x2 = mm_residual("mix_out", mixed[None], wo_full[None], pl.BlockSpec((None, d, d), lambda m, j: (j, 0, 0)), x1, 1.0)
    (w3_2,) = gather_forward("gather_ffn2_forward", gather_ici_wait("gather_ffn2_wait", f2_send, f2_recv, f2_bufs, x2))
    x3, saved2 = ffn_forward("ffn2", x2, norm_ffn2, w3_2)
    dx3, gn_final, loss_part = loss_head("loss_head", x3, norm_final[None], target)

    dx2, gn_ffn2, g_w3_2 = ffn_backward("ffn2", dx3, saved2, norm_ffn2, w3_2)
    rs_ffn2, rs_ffn2_token = reduce_scatter_begin("rs_ffn2", [g_w3_2], c_arr)

    dua, dub, dpa, dpb, gba, gbb = mix_bwd("mix_bwd", dx2, wo_full, pc, ua, ub, after=(rs_ffn2_token,))
    dgp = jnp.concatenate([dpa, dpb], axis=1)
    g_bgate = jnp.concatenate([gba, gbb], axis=1)
    g_wo = mm_plain("wgrad_out", mixed, dx2, TN, BF16)
    dya = mm_plain("dya", dua, w_up_a_full, NT, BF16, tk_target=1024)
    dyb = mm_plain("dyb", dub, w_up_b_full, NT, BF16, tk_target=1024)
    g_up_a = mm_plain("wgrad_up_a", ya, dua, TN, BF16)
    g_up_b = mm_plain("wgrad_up_b", yb, dub, TN, BF16)
    dqkv = sb_pair_bwd("sb_bwd", qkv, dya, lax.empty((t, QKV_COLS), BF16))
    dqkv, dcc, dcr = fox_pair_bwd("fox_bwd", qkv, yb, dyb, lse, c_col, c_row, dqkv)
    dc = (dcc[:, :, 0] + dcr.reshape(N_HEADS, t)).T
    df, g_bf = fox_gate_bwd("fox_gate_bwd", jnp.pad(dc, ((0, 0), (0, QB - N_HEADS))), f_logit, b_forget_row)
    dcat = jnp.concatenate([dqkv, dgp, df.astype(BF16), jnp.zeros((t, F_PAD - QB), BF16)], axis=1)
    g_wcat = wgrad_cat("wgrad_cat", h2, dcat)
    tm = _row_tile(t, 512)
    tkc = 768
    nkc = wcat.shape[1] // tkc
    dx1, gn_mix = dh_rms_bwd(
        "mix_dh", [(dcat, wcat)],
        [(pl.BlockSpec((tm, tkc), lambda m, k: (m, k)), pl.BlockSpec((d, tkc), lambda m, k: (0, k)))], NT,
        (t // tm, nkc), nkc, x1, rstd2, norm_mix, dx2)
    g_in = jnp.concatenate([g_wcat[:, :QKV_COLS], g_wcat[:, f_off:f_off + n_forget]], axis=1)
    g_mx = jnp.concatenate([_shards_from_cols(g_in), _shards_from_cols(g_wcat[:, QKV_COLS:f_off]),
                            _shards_from_cols(jnp.concatenate([g_up_a, g_up_b], axis=0))], axis=2)[:, None]
    rs_mixer, rs_mixer_token = reduce_scatter_begin("rs_mixer", [g_mx, g_wo.reshape(N_CHIPS, 1, d // N_CHIPS, d)], c_arr)
    (s_w3_2,) = reduce_scatter_end("rs_ffn2", rs_ffn2, c_arr, dx1)

    dx0, gn_ffn1, g_w3_1 = ffn_backward("ffn1", dx1, saved1, norm_ffn1, w3_1, after=(rs_mixer_token,))
    rs_ffn1, rs_ffn1_token = reduce_scatter_begin("rs_ffn1", [g_w3_1], c_arr)
    s_mx, s_wo = reduce_scatter_end("rs_mixer", rs_mixer, c_arr, dx0)

    def pack_small(n1, nm, n2, nf, bg, bf, last):
        return jnp.concatenate([n1, nm, n2, nf, bg.reshape(2, d), jnp.pad(bf, ((0, 0), (0, d - n_forget))), last], axis=0)

    zero_row = jnp.zeros((1, d), F32)
    g_small = pack_small(gn_ffn1, gn_mix, gn_ffn2, gn_final, g_bgate, g_bf[:, :n_forget], jnp.pad(loss_part, ((0, 0), (0, d - 1))))
    w_small = pack_small(norm_ffn1, norm_mix, norm_ffn2, norm_final[None], b_gate, b_forget, zero_row)
    m_small = pack_small(m_norm_ffn1, m_norm_mix, m_norm_ffn2, m_norm_final[None], m_b_gate, m_b_forget, zero_row)
    v_small = pack_small(v_norm_ffn1, v_norm_mix, v_norm_ffn2, v_norm_final[None], v_b_gate, v_b_forget, zero_row)
    smalls = small_allreduce_adamw("small_allreduce_adamw", g_small, w_small, m_small, v_small)

    def unpack_small(p):
        return {"norm_ffn1": p[0:1], "norm_mix": p[1:2], "norm_ffn2": p[2:3], "norm_final": p[3], "b_gate": p[4:6].reshape(1, 2 * d),
                "b_forget": p[6:7, :n_forget]}

    loss = smalls[0][7, 0]
    small_out = [unpack_small(p) for p in smalls]

    grads = {
        "w_in": s_mx[0][:, :in4], "w_gate": s_mx[0][:, in4:in4 + gate4],
        "w_up_a": s_mx[0][:WIDTH, in4 + gate4:], "w_up_b": s_mx[0][WIDTH:, in4 + gate4:], "w_out": s_wo[0],
    }
    weights = {"w_in": (w_in, m_w_in, v_w_in), "w_gate": (w_gate, m_w_gate, v_w_gate), "w_up_a": (w_up_a, m_w_up_a, v_w_up_a),
               "w_up_b": (w_up_b, m_w_up_b, v_w_up_b), "w_out": (w_out, m_w_out, v_w_out)}
    big_out = {}
    for wname, (w, m, v) in weights.items():
        g = grads[wname]
        delta, new_m, new_v = adamw(f"adamw_{wname}", w[0], g, m[0], v[0], after=(rs_ffn1_token,))
        big_out[wname] = (g[None], delta[None], new_m[None], new_v[None])

    def adamw_ffn(tag, s_w3, ws, ms, vs, after):
        deltas, new_ms, new_vs = adamw_stacked(f"adamw_{tag}", ws, s_w3, ms, vs, after)
        for which, part in ((GATE, "gate"), (UP, "up"), (DOWN, "down")):
            back = (lambda a: a[None]) if which == DOWN else (lambda a: a.T[None])
            big_out[f"w_{tag}_{part}"] = tuple(back(a) for a in (s_w3[which], deltas[which], new_ms[which], new_vs[which]))
        return deltas[DOWN]

    last = adamw_ffn("ffn2", s_w3_2, ffn2_w, ffn2_m, ffn2_v, (rs_ffn1_token,))
    (s_w3_1,) = reduce_scatter_end("rs_ffn1", rs_ffn1, c_arr, last)
    adamw_ffn("ffn1", s_w3_1, ffn1_w, ffn1_m, ffn1_v, ())

    order = ["norm_ffn1", "w_ffn1_gate", "w_ffn1_up", "w_ffn1_down", "norm_mix", "w_in", "b_forget", "w_gate", "b_gate",
             "w_up_a", "w_up_b", "w_out", "norm_ffn2", "w_ffn2_gate", "w_ffn2_up", "w_ffn2_down", "norm_final"]
    outs = [loss, dx0[None]]
    for kind in range(4):
        for wname in order:
            outs.append(big_out[wname][kind] if wname in big_out else small_out[kind][wname])
    return tuple(outs)
```

```python
import functools

import jax
import jax.numpy as jnp
from jax import lax
from jax.experimental import pallas as pl
from jax.experimental.pallas import tpu as pltpu

F32 = jnp.float32
BF16 = jnp.bfloat16

HEAD_DIM = 64
N_HEADS = 8
WIDTH = N_HEADS * HEAD_DIM
QKV_COLS = 6 * WIDTH
RMS_EPS = 1e-6
ATTN_SCALE = HEAD_DIM ** -0.5
N_CHIPS = 4
QB = 128
BQ = 512
CS = 256
N_SUB = BQ // CS
F_PAD = 256
NEG_BIG = -1e30

ADAM_LR = 0.001
ADAM_B1 = 0.9
ADAM_B2 = 0.999
ADAM_EPS = 1e-08
ADAM_WD = 0.01
ADAM_STEP = 10

VMEM_LIMIT_BYTES = 48 * 1024 * 1024
MESH = pl.DeviceIdType.MESH

NN = ((1,), (0,))
NT = ((1,), (1,))
TN = ((0,), (0,))


def _params(semantics):
    return pltpu.CompilerParams(dimension_semantics=semantics, vmem_limit_bytes=VMEM_LIMIT_BYTES)


def _dot(a, b, contract):
    return lax.dot_general(a.astype(BF16), b.astype(BF16), (contract, ((), ())), preferred_element_type=F32)


def _sigmoid(x):
    return 1.0 / (1.0 + jnp.exp(-x))


def _log1pexp_neg_abs(z):
    return jnp.log(1.0 + jnp.exp(-jnp.abs(z)))


def _split3(x):
    hi = x.astype(BF16)
    r1 = x - hi.astype(F32)
    mid = r1.astype(BF16)
    lo = (r1 - mid.astype(F32)).astype(BF16)
    return hi, mid, lo


def _dot_exact_rhs01(x, m01):
    hi, mid, lo = _split3(x)
    d = lambda p: lax.dot_general(p, m01, (NN, ((), ())), preferred_element_type=F32)
    return d(hi) + d(mid) + d(lo)


def _dot_exact_lhs01(m01, x):
    hi, mid, lo = _split3(x)
    d = lambda p: lax.dot_general(m01, p, (NN, ((), ())), preferred_element_type=F32)
    return d(hi) + d(mid) + d(lo)


def _iota2(shape, dim):
    return lax.broadcasted_iota(jnp.int32, shape, dim)


def _mm(name, pairs, contract, grid, pair_specs, out_shape, out_specs, acc_shape, nk, epilogue,
        extras=(), extra_specs=(), semantics=None):
    n_pairs = len(pairs)
    n_extra = len(extras)
    n_out = len(out_shape)

    def body(*refs):
        ab = refs[:2 * n_pairs]
        ex = refs[2 * n_pairs:2 * n_pairs + n_extra]
        outs = refs[2 * n_pairs + n_extra:2 * n_pairs + n_extra + n_out]
        ids = [pl.program_id(i) for i in range(len(grid))]
        k = ids[-1]
        part = _dot(ab[0][...], ab[1][...], contract)
        for p in range(1, n_pairs):
            part += _dot(ab[2 * p][...], ab[2 * p + 1][...], contract)
        if nk == 1:
            epilogue(part, ex, outs, ids)
            return
        acc = refs[-1]

        @pl.when(k == 0)
        def _():
            acc[...] = part

        @pl.when(k != 0)
        def _():
            acc[...] += part

        @pl.when(k == nk - 1)
        def _():
            epilogue(acc[...], ex, outs, ids)

    operands = [t for pair in pairs for t in pair] + list(extras)
    in_specs = [s for pair in pair_specs for s in pair] + list(extra_specs)
    if semantics is None:
        semantics = ("parallel",) * (len(grid) - 1) + ("arbitrary",)
    return pl.pallas_call(
        body, name=name, grid=grid, in_specs=in_specs, out_specs=list(out_specs), out_shape=list(out_shape),
        scratch_shapes=[] if nk == 1 else [pltpu.VMEM(acc_shape, F32)], compiler_params=_params(semantics),
    )(*operands)


def _ordered_after(body, n_in, n_after):
    def wrapped(*refs):
        return body(*refs[:n_in], *refs[n_in + n_after:])
    return wrapped


def _row_tile(rows, target):
    t = min(rows, target)
    while rows % t:
        t //= 2
    return t


def rms_fwd(name, x, g, after=()):
    t, d = x.shape
    tr = _row_tile(t, 256)

    def body(x_ref, g_ref, h_ref, r_ref):
        xv = x_ref[...]
        r = lax.rsqrt(jnp.mean(xv * xv, axis=-1, keepdims=True) + RMS_EPS)
        h_ref[...] = (xv * r * g_ref[...]).astype(BF16)
        r_ref[...] = r

    return pl.pallas_call(
        _ordered_after(body, 2, len(after)), name=name, grid=(t // tr,),
        in_specs=[pl.BlockSpec((tr, d), lambda i: (i, 0)), pl.BlockSpec((1, d), lambda i: (0, 0))] + [_ANY] * len(after),
        out_specs=[pl.BlockSpec((tr, d), lambda i: (i, 0)), pl.BlockSpec((tr, 1), lambda i: (i, 0))],
        out_shape=[jax.ShapeDtypeStruct((t, d), BF16), jax.ShapeDtypeStruct((t, 1), F32)],
        compiler_params=_params(("parallel",)),
    )(x, g, *after)


GATE, UP, DOWN = 0, 1, 2


def _ffn_w_spec(which, f4, d, index_of_j):
    return pl.BlockSpec((None, None, f4, d), lambda *ids: (index_of_j(*ids), which, 0, 0))


def ffn_up(name, h, w3):
    t, d = h.shape
    ns, _, f4, _ = w3.shape
    tm = _row_tile(t, 512)

    def body(h_ref, wg_ref, wu_ref, a_ref, b_ref, s_ref):
        hv = h_ref[...]
        a = _dot(hv, wg_ref[...], NT)
        b = _dot(hv, wu_ref[...], NT)
        a_ref[...] = a.astype(BF16)
        b_ref[...] = b.astype(BF16)
        s_ref[...] = (a * _sigmoid(a) * b).astype(BF16)

    act_spec = pl.BlockSpec((None, tm, f4), lambda j, m: (j, m, 0))
    return pl.pallas_call(
        body, name=name, grid=(ns, t // tm),
        in_specs=[pl.BlockSpec((tm, d), lambda j, m: (m, 0)),
                  _ffn_w_spec(GATE, f4, d, lambda j, m: j), _ffn_w_spec(UP, f4, d, lambda j, m: j)],
        out_specs=[act_spec, act_spec, act_spec],
        out_shape=[jax.ShapeDtypeStruct((ns, t, f4), BF16)] * 3,
        compiler_params=_params(("parallel", "parallel")),
    )(h, w3, w3)


def mm_residual(name, s, w, w_spec, x, scale):
    nj, t, kdim = s.shape
    n = x.shape[1]
    tm = _row_tile(t, 512)

    def body(s_ref, w_ref, x_ref, o_ref):
        acc = _dot(s_ref[0], w_ref[0], NN)
        for j in range(1, nj):
            acc += _dot(s_ref[j], w_ref[j], NN)
        o_ref[...] = x_ref[...] + scale * acc

    row = pl.BlockSpec((tm, n), lambda m: (m, 0))
    return pl.pallas_call(
        body, name=name, grid=(t // tm,),
        in_specs=[pl.BlockSpec((nj, tm, kdim), lambda m: (0, m, 0)), w_spec, row], out_specs=row,
        out_shape=jax.ShapeDtypeStruct((t, n), F32), compiler_params=_params(("parallel",)),
    )(s, w, x)


def ffn_bwd_act(name, dx, w3, a, b, after=()):
    t, d = dx.shape
    ns, _, f4, _ = w3.shape
    tm = _row_tile(t, 512)

    def body(dx_ref, wd_ref, a_ref, b_ref, da_ref, db_ref):
        ds = _dot(0.5 * dx_ref[...], wd_ref[...], NT)
        av = a_ref[...].astype(F32)
        sig = _sigmoid(av)
        da_ref[...] = (ds * b_ref[...].astype(F32) * (sig * (1.0 + av * (1.0 - sig)))).astype(BF16)
        db_ref[...] = (ds * (av * sig)).astype(BF16)

    act_spec = pl.BlockSpec((None, tm, f4), lambda j, m: (j, m, 0))
    return pl.pallas_call(
        _ordered_after(body, 4, len(after)), name=name, grid=(ns, t // tm),
        in_specs=[pl.BlockSpec((tm, d), lambda j, m: (m, 0)), _ffn_w_spec(DOWN, f4, d, lambda j, m: j), act_spec, act_spec]
        + [_ANY] * len(after),
        out_specs=[act_spec, act_spec],
        out_shape=[jax.ShapeDtypeStruct((ns, t, f4), BF16)] * 2,
        compiler_params=_params(("parallel", "parallel")),
    )(dx, w3, a, b, *after)


def ffn_wgrad(name, h, da, db, s, dx):
    t, d = h.shape
    ns, _, f4 = da.shape
    tk = _row_tile(t, 1024)
    nk = t // tk

    def body(h_ref, da_ref, db_ref, s_ref, dx_ref, o_ref, acc):
        k = pl.program_id(1)

        @pl.when(k == 0)
        def _():
            acc[...] = jnp.zeros_like(acc)

        hv = h_ref[...]
        acc[GATE] += _dot(da_ref[...], hv, TN)
        acc[UP] += _dot(db_ref[...], hv, TN)
        acc[DOWN] += _dot(s_ref[...], 0.5 * dx_ref[...], TN)

        @pl.when(k == nk - 1)
        def _():
            o_ref[...] = acc[...].astype(BF16)

    act_spec = pl.BlockSpec((None, tk, f4), lambda j, k: (j, k, 0))
    row_spec = pl.BlockSpec((tk, d), lambda j, k: (k, 0))
    return pl.pallas_call(
        body, name=name, grid=(ns, nk),
        in_specs=[row_spec, act_spec, act_spec, act_spec, row_spec],
        out_specs=pl.BlockSpec((None, 3, f4, d), lambda j, k: (j, 0, 0, 0)),
        out_shape=jax.ShapeDtypeStruct((ns, 3, f4, d), BF16),
        scratch_shapes=[pltpu.VMEM((3, f4, d), F32)],
        compiler_params=_params(("parallel", "arbitrary")),
    )(h, da, db, s, dx)


def _rms_bwd_tail(dh, x_ref, r_ref, g_ref, dxin_ref, dx_ref, gn_ref, row_tile_index):
    r = r_ref[...]
    xhat = x_ref[...] * r
    dhg = dh * g_ref[...]
    dx_ref[...] = dxin_ref[...] + r * (dhg - xhat * jnp.mean(dhg * xhat, axis=-1, keepdims=True))
    part = jnp.sum(dh * xhat, axis=0, keepdims=True)

    @pl.when(row_tile_index == 0)
    def _():
        gn_ref[...] = part

    @pl.when(row_tile_index != 0)
    def _():
        gn_ref[...] += part


def ffn_dh(name, da, db, w3, x, rstd, g, dx_in):
    ns, t, f4 = da.shape
    d = x.shape[1]
    tm = _row_tile(t, 256)

    def body(da_ref, db_ref, wg_ref, wu_ref, x_ref, r_ref, g_ref, dxin_ref, dx_ref, gn_ref):
        dh = _dot(da_ref[0], wg_ref[0], NN) + _dot(db_ref[0], wu_ref[0], NN)
        for j in range(1, ns):
            dh += _dot(da_ref[j], wg_ref[j], NN) + _dot(db_ref[j], wu_ref[j], NN)
        _rms_bwd_tail(dh, x_ref, r_ref, g_ref, dxin_ref, dx_ref, gn_ref, pl.program_id(0))

    act = pl.BlockSpec((ns, tm, f4), lambda m: (0, m, 0))
    row = pl.BlockSpec((tm, d), lambda m: (m, 0))
    gain = pl.BlockSpec((1, d), lambda m: (0, 0))
    return pl.pallas_call(
        body, name=name, grid=(t // tm,),
        in_specs=[act, act, pl.BlockSpec((ns, None, f4, d), lambda m: (0, GATE, 0, 0)),
                  pl.BlockSpec((ns, None, f4, d), lambda m: (0, UP, 0, 0)), row, pl.BlockSpec((tm, 1), lambda m: (m, 0)), gain, row],
        out_specs=[row, gain], out_shape=[jax.ShapeDtypeStruct((t, d), F32), jax.ShapeDtypeStruct((1, d), F32)],
        compiler_params=_params(("arbitrary",)),
    )(da, db, w3, w3, x, rstd, g, dx_in)


def dh_rms_bwd(name, pairs, pair_specs, contract, grid, nk, x, rstd, g, dx_in):
    t, d = x.shape
    tm = t // grid[0]

    def epilogue(acc, ex, outs, ids):
        _rms_bwd_tail(acc, *ex, *outs, ids[0])

    row = pl.BlockSpec((tm, d), lambda m, k: (m, 0))
    return _mm(
        name, pairs, contract, grid, pair_specs,
        [jax.ShapeDtypeStruct((t, d), F32), jax.ShapeDtypeStruct((1, d), F32)],
        [row, pl.BlockSpec((1, d), lambda m, k: (0, 0))], (tm, d), nk, epilogue,
        extras=[x, rstd, g, dx_in],
        extra_specs=[row, pl.BlockSpec((tm, 1), lambda m, k: (m, 0)), pl.BlockSpec((1, d), lambda m, k: (0, 0)), row],
        semantics=("arbitrary", "arbitrary"),
    )


def ffn_forward(tag, x, g_norm, w3, normed=None):
    _, _, f4, d = w3.shape
    h, rstd = normed if normed is not None else rms_fwd(f"{tag}_rms", x, g_norm)
    a, b, s = ffn_up(f"{tag}_up", h, w3)
    x_out = mm_residual(f"{tag}_down", s, w3, pl.BlockSpec((w3.shape[0], None, f4, d), lambda m: (0, DOWN, 0, 0)), x, 0.5)
    return x_out, (x, h, rstd, a, b, s)


def ffn_backward(tag, dx, saved, g_norm, w3, after=()):
    x, h, rstd, a, b, s = saved
    t, d = x.shape
    da, db = ffn_bwd_act(f"{tag}_bwd_act", dx, w3, a, b, after)
    g_w3 = ffn_wgrad(f"{tag}_wgrad", h, da, db, s, dx)
    dx_out, g_n = ffn_dh(f"{tag}_dh", da, db, w3, x, rstd, g_norm, dx)
    return dx_out, g_n, g_w3


def proj(name, h, wcat, bias, first_col, n_cols, tn, out_dtype, scaled_tiles=()):
    t, d = h.shape
    tm = _row_tile(t, 512)
    off = first_col // tn

    def epilogue(acc, ex, outs, ids):
        val = acc + ex[0][...]
        if scaled_tiles:
            hit = functools.reduce(jnp.logical_or, [ids[0] == s for s in scaled_tiles])
            val = val * jnp.where(hit, ATTN_SCALE, 1.0)
        outs[0][...] = val.astype(out_dtype)

    return _mm(
        name, [(h, wcat)], NN, (n_cols // tn, t // tm, 1),
        [(pl.BlockSpec((tm, d), lambda j, m, k: (m, 0)), pl.BlockSpec((d, tn), lambda j, m, k: (0, off + j)))],
        [jax.ShapeDtypeStruct((t, n_cols), out_dtype)], [pl.BlockSpec((tm, tn), lambda j, m, k: (m, j))], (tm, tn), 1, epilogue,
        extras=[bias], extra_specs=[pl.BlockSpec((1, tn), lambda j, m, k: (0, off + j))],
    )[0]


def mix_fwd(name, ya, yb, wa, wb, pc):
    t, w = ya.shape
    d = wa.shape[1]
    tm = _row_tile(t, 512)
    tn = 512
    off_a = 0
    off_b = d // tn

    def body(ya_ref, yb_ref, wa_ref, wb_ref, pa_ref, pb_ref, ua_ref, ub_ref, mx_ref):
        ua = _dot(ya_ref[...], wa_ref[...], NN)
        ub = _dot(yb_ref[...], wb_ref[...], NN)
        ua_ref[...] = ua
        ub_ref[...] = ub
        mx_ref[...] = (_sigmoid(pa_ref[...]) * ua + _sigmoid(pb_ref[...]) * ub).astype(BF16)

    y_spec = pl.BlockSpec((tm, w), lambda m, n: (m, 0))
    w_spec = pl.BlockSpec((w, tn), lambda m, n: (0, n))
    o_spec = pl.BlockSpec((tm, tn), lambda m, n: (m, n))
    return pl.pallas_call(
        body, name=name, grid=(t // tm, d // tn),
        in_specs=[y_spec, y_spec, w_spec, w_spec,
                  pl.BlockSpec((tm, tn), lambda m, n: (m, off_a + n)), pl.BlockSpec((tm, tn), lambda m, n: (m, off_b + n))],
        out_specs=[o_spec, o_spec, o_spec],
        out_shape=[jax.ShapeDtypeStruct((t, d), F32), jax.ShapeDtypeStruct((t, d), F32), jax.ShapeDtypeStruct((t, d), BF16)],
        compiler_params=_params(("parallel", "parallel")),
    )(ya, yb, wa, wb, pc, pc)


def mix_bwd(name, dx, wo, pc, ua, ub, after=()):
    t, d = dx.shape
    tm = _row_tile(t, 512)
    tn = 512
    off_a = 0
    off_b = d // tn

    def body(dx_ref, wo_ref, pa_ref, pb_ref, ua_ref, ub_ref, dua_ref, dub_ref, dpa_ref, dpb_ref, ba_ref, bb_ref):
        dm = _dot(dx_ref[...], wo_ref[...], NT)
        ga = _sigmoid(pa_ref[...])
        gb = _sigmoid(pb_ref[...])
        dua_ref[...] = (dm * ga).astype(BF16)
        dub_ref[...] = (dm * gb).astype(BF16)
        dpa = dm * ua_ref[...] * ga * (1.0 - ga)
        dpb = dm * ub_ref[...] * gb * (1.0 - gb)
        dpa_ref[...] = dpa.astype(BF16)
        dpb_ref[...] = dpb.astype(BF16)
        sa = jnp.sum(dpa, axis=0, keepdims=True)
        sb = jnp.sum(dpb, axis=0, keepdims=True)

        @pl.when(pl.program_id(1) == 0)
        def _():
            ba_ref[...] = sa
            bb_ref[...] = sb

        @pl.when(pl.program_id(1) != 0)
        def _():
            ba_ref[...] += sa
            bb_ref[...] += sb

    tile = pl.BlockSpec((tm, tn), lambda n, m: (m, n))
    bias = pl.BlockSpec((1, tn), lambda n, m: (0, n))
    return pl.pallas_call(
        _ordered_after(body, 6, len(after)), name=name, grid=(d // tn, t // tm),
        in_specs=[pl.BlockSpec((tm, d), lambda n, m: (m, 0)), pl.BlockSpec((tn, d), lambda n, m: (n, 0)),
                  pl.BlockSpec((tm, tn), lambda n, m: (m, off_a + n)), pl.BlockSpec((tm, tn), lambda n, m: (m, off_b + n)),
                  tile, tile] + [_ANY] * len(after),
        out_specs=[tile, tile, tile, tile, bias, bias],
        out_shape=[jax.ShapeDtypeStruct((t, d), BF16)] * 4 + [jax.ShapeDtypeStruct((1, d), F32)] * 2,
        compiler_params=_params(("parallel", "arbitrary")),
    )(dx, wo, pc, pc, ua, ub, *after)


def mm_plain(name, a, b, contract, out_dtype, tk_target=512):
    if contract == NN:
        m, kdim = a.shape
        n = b.shape[1]
    elif contract == NT:
        m, kdim = a.shape
        n = b.shape[0]
    else:
        kdim, m = a.shape
        n = b.shape[1]
    tm = _row_tile(m, 512)
    tk = _row_tile(kdim, tk_target)
    nk = kdim // tk
    if contract == TN:
        a_spec = pl.BlockSpec((tk, tm), lambda i, k: (k, i))
    else:
        a_spec = pl.BlockSpec((tm, tk), lambda i, k: (i, k))
    if contract == NT:
        b_spec = pl.BlockSpec((n, tk), lambda i, k: (0, k))
    else:
        b_spec = pl.BlockSpec((tk, n), lambda i, k: (k, 0))

    def epilogue(acc, ex, outs, ids):
        outs[0][...] = acc.astype(out_dtype)

    return _mm(name, [(a, b)], contract, (m // tm, nk), [(a_spec, b_spec)],
               [jax.ShapeDtypeStruct((m, n), out_dtype)], [pl.BlockSpec((tm, n), lambda i, k: (i, 0))], (tm, n), nk, epilogue)[0]


def wgrad_cat(name, h, dcat):
    t, d = h.shape
    n = dcat.shape[1]
    tn = 768
    tk = _row_tile(t, 2048)

    def epilogue(acc, ex, outs, ids):
        outs[0][...] = acc.astype(BF16)

    return _mm(
        name, [(h, dcat)], TN, (n // tn, t // tk),
        [(pl.BlockSpec((tk, d), lambda j, k: (k, 0)), pl.BlockSpec((tk, tn), lambda j, k: (k, j)))],
        [jax.ShapeDtypeStruct((d, n), BF16)], [pl.BlockSpec((d, tn), lambda j, k: (0, j))], (d, tn), t // tk, epilogue,
    )[0]


def fox_prep(name, f, bias):
    t, lanes = f.shape
    nchunk = t // QB

    def body(f_ref, b_ref, c_ref):
        lower = (_iota2((QB, QB), 1) <= _iota2((QB, QB), 0)).astype(BF16)

        def chunk(n, carry):
            rows = pl.ds(pl.multiple_of(n * QB, QB), QB)
            u = f_ref[rows, :] + b_ref[...]
            lf = jnp.minimum(u, 0.0) - _log1pexp_neg_abs(u)
            c = _dot_exact_lhs01(lower, lf) + carry
            c_ref[rows, :] = c
            return c[QB - 1:QB, :]

        lax.fori_loop(0, nchunk, chunk, jnp.zeros((1, lanes), F32))

    return pl.pallas_call(body, name=name, out_shape=jax.ShapeDtypeStruct((t, lanes), F32),
                          compiler_params=pltpu.CompilerParams(vmem_limit_bytes=VMEM_LIMIT_BYTES))(f, bias)


def fox_gate_bwd(name, dc, f, bias):
    t, lanes = dc.shape
    nchunk = t // QB

    def body(dc_ref, f_ref, b_ref, df_ref, gb_ref):
        upper = (_iota2((QB, QB), 1) >= _iota2((QB, QB), 0)).astype(BF16)

        def chunk(n, carry):
            tail, total = carry
            rows = pl.ds(pl.multiple_of((nchunk - 1 - n) * QB, QB), QB)
            dlf = _dot_exact_lhs01(upper, dc_ref[rows, :]) + tail
            u = f_ref[rows, :] + b_ref[...]
            df = dlf * jnp.exp(jnp.minimum(-u, 0.0) - _log1pexp_neg_abs(u))
            df_ref[rows, :] = df
            return dlf[0:1, :], total + jnp.sum(df, axis=0, keepdims=True)

        zero = jnp.zeros((1, lanes), F32)
        _, total = lax.fori_loop(0, nchunk, chunk, (zero, zero))
        gb_ref[...] = total

    return pl.pallas_call(body, name=name,
                          out_shape=[jax.ShapeDtypeStruct((t, lanes), F32), jax.ShapeDtypeStruct((1, lanes), F32)],
                          compiler_params=pltpu.CompilerParams(vmem_limit_bytes=VMEM_LIMIT_BYTES))(dc, f, bias)


def _qrows(i):
    return pl.ds(pl.multiple_of(i * BQ, BQ), BQ)


def _krows(kc):
    return pl.ds(pl.multiple_of(kc * CS, CS), CS)


def _head_spec(t, offset):
    return pl.BlockSpec((None, t, HEAD_DIM), lambda h: (offset + h, 0, 0))


def _head_t_spec(nq, offset):
    return pl.BlockSpec((None, nq, HEAD_DIM, BQ), lambda h: (offset + h, 0, 0, 0))


def _chunk_t_spec(nc):
    return pl.BlockSpec((None, nc, HEAD_DIM, CS), lambda h: (h, 0, 0, 0))


def _dot_split2_rhs01(x, m01):
    hi = x.astype(BF16)
    lo = (x - hi.astype(F32)).astype(BF16)
    d = lambda p: lax.dot_general(p, m01, (NN, ((), ())), preferred_element_type=F32)
    return d(hi) + d(lo)


def _diag_mask(dchunk, inclusive):
    r_io = _iota2((BQ, CS), 0)
    c_io = _iota2((BQ, CS), 1) + dchunk * CS
    return c_io <= r_io if inclusive else c_io < r_io


def _walk_chunks(i, step, init, right_to_left):
    order = list(reversed(range(N_SUB))) if right_to_left else list(range(N_SUB))

    def diagonal(state):
        for dchunk in order:
            state = step(i * N_SUB + dchunk, state, dchunk)
        return state

    def group(n, state):
        base = ((i - 1 - n) if right_to_left else n) * N_SUB
        for dchunk in order:
            state = step(base + dchunk, state, None)
        return state

    if right_to_left:
        return lax.fori_loop(0, i, group, diagonal(init))
    return diagonal(lax.fori_loop(0, i, group, init))


def sb_fwd(name, qkv):
    t = qkv.shape[1]

    def body(q_ref, k_ref, v_ref, o_ref):
        later = (_iota2((CS, CS), 0) > _iota2((CS, CS), 1)).astype(BF16)

        def qblock(i, _):
            q = q_ref[_qrows(i), :]

            def step(kc, state, dchunk):
                carry, acc = state
                z = _dot(q, k_ref[_krows(kc), :], NT)
                sp = _log1pexp_neg_abs(z)
                lnb = -jnp.maximum(z, 0.0) - sp
                if dchunk is not None:
                    lnb = jnp.where(_diag_mask(dchunk, False), lnb, 0.0)
                w = jnp.exp(jnp.minimum(z, 0.0) - sp + _dot_split2_rhs01(lnb, later) + carry)
                if dchunk is not None:
                    w = jnp.where(_diag_mask(dchunk, False), w, 0.0)
                acc = acc + _dot(w, v_ref[_krows(kc), :], NN)
                return carry + jnp.sum(lnb, axis=1, keepdims=True), acc

            init = (jnp.zeros((BQ, 1), F32), jnp.zeros((BQ, HEAD_DIM), F32))
            _, acc = _walk_chunks(i, step, init, True)
            o_ref[_qrows(i), :] = acc.astype(BF16)
            return 0

        lax.fori_loop(0, t // BQ, qblock, 0)

    return pl.pallas_call(
        body, name=name, grid=(N_HEADS,),
        in_specs=[_head_spec(t, 0), _head_spec(t, N_HEADS), _head_spec(t, 2 * N_HEADS)],
        out_specs=_head_spec(t, 0), out_shape=jax.ShapeDtypeStruct((N_HEADS, t, HEAD_DIM), BF16),
        compiler_params=_params(("parallel",)),
    )(qkv, qkv, qkv)


def sb_bwd(name, qkv, qt, dy, dyt):
    t = qkv.shape[1]
    nq, nc = t // BQ, t // CS

    def body(q_ref, k_ref, v_ref, qt_ref, do_ref, dot_ref, dq_ref, dkt_ref, dvt_ref, g_s, b_s, dkt_acc, dvt_acc):
        later = (_iota2((CS, CS), 0) > _iota2((CS, CS), 1)).astype(BF16)
        earlier = (_iota2((CS, CS), 0) < _iota2((CS, CS), 1)).astype(BF16)
        dkt_acc[...] = jnp.zeros_like(dkt_acc)
        dvt_acc[...] = jnp.zeros_like(dvt_acc)

        def qblock(i, _):
            q = q_ref[_qrows(i), :]
            do = do_ref[_qrows(i), :]
            q_t = qt_ref[i]
            do_t = dot_ref[i]

            def step1(kc, carry, dchunk):
                z = _dot(q, k_ref[_krows(kc), :], NT)
                sp = _log1pexp_neg_abs(z)
                lnb = -jnp.maximum(z, 0.0) - sp
                lsz = jnp.minimum(z, 0.0) - sp
                if dchunk is not None:
                    lnb = jnp.where(_diag_mask(dchunk, False), lnb, 0.0)
                w = jnp.exp(lsz + _dot_split2_rhs01(lnb, later) + carry)
                if dchunk is not None:
                    w = jnp.where(_diag_mask(dchunk, False), w, 0.0)
                g_s[kc] = w * _dot(do, v_ref[_krows(kc), :], NT)
                b_s[kc] = jnp.exp(lsz)
                dvt_acc[kc] += _dot(do_t, w, NN)
                return carry + jnp.sum(lnb, axis=1, keepdims=True)

            _walk_chunks(i, step1, jnp.zeros((BQ, 1), F32), True)

            def step2(kc, state, dchunk):
                before, dq = state
                g = g_s[kc]
                beta = b_s[kc]
                dz = g * (1.0 - beta) - beta * (_dot_split2_rhs01(g, earlier) + before)
                if dchunk is not None:
                    dz = jnp.where(_diag_mask(dchunk, False), dz, 0.0)
                dzb = dz.astype(BF16)
                dq = dq + _dot(dzb, k_ref[_krows(kc), :], NN)
                dkt_acc[kc] += _dot(q_t, dzb, NN)
                return before + jnp.sum(g, axis=1, keepdims=True), dq

            _, dq = _walk_chunks(i, step2, (jnp.zeros((BQ, 1), F32), jnp.zeros((BQ, HEAD_DIM), F32)), False)
            dq_ref[_qrows(i), :] = (dq * ATTN_SCALE).astype(BF16)
            return 0

        lax.fori_loop(0, nq, qblock, 0)
        dkt_ref[...] = dkt_acc[...].astype(BF16)
        dvt_ref[...] = dvt_acc[...].astype(BF16)

    chunked = jax.ShapeDtypeStruct((N_HEADS, nc, HEAD_DIM, CS), BF16)
    return pl.pallas_call(
        body, name=name, grid=(N_HEADS,),
        in_specs=[_head_spec(t, 0), _head_spec(t, N_HEADS), _head_spec(t, 2 * N_HEADS), _head_t_spec(nq, 0),
                  _head_spec(t, 0), _head_t_spec(nq, 0)],
        out_specs=[_head_spec(t, 0), _chunk_t_spec(nc), _chunk_t_spec(nc)],
        out_shape=[jax.ShapeDtypeStruct((N_HEADS, t, HEAD_DIM), BF16), chunked, chunked],
        scratch_shapes=[pltpu.VMEM((nc, BQ, CS), F32), pltpu.VMEM((nc, BQ, CS), F32),
                        pltpu.VMEM((nc, HEAD_DIM, CS), F32), pltpu.VMEM((nc, HEAD_DIM, CS), F32)],
        compiler_params=_params(("parallel",)),
    )(qkv, qkv, qkv, qt, dy, dyt)


def _col_spec(t):
    return pl.BlockSpec((None, t, 1), lambda h: (h, 0, 0))


def _row_spec(nc):
    return pl.BlockSpec((None, nc, 1, CS), lambda h: (h, 0, 0, 0))


def fox_fwd(name, qkv, c_col, c_row):
    t = qkv.shape[1]

    def body(q_ref, k_ref, v_ref, cc_ref, cr_ref, o_ref, lse_ref):
        def qblock(i, _):
            q = q_ref[_qrows(i), :]
            ct = cc_ref[_qrows(i), :]

            def step(kc, state, dchunk):
                m, l, acc = state
                s = _dot(q, k_ref[_krows(kc), :], NT) + ct - cr_ref[kc]
                if dchunk is not None:
                    s = jnp.where(_diag_mask(dchunk, True), s, NEG_BIG)
                m_new = jnp.maximum(m, jnp.max(s, axis=1, keepdims=True))
                alpha = jnp.exp(m - m_new)
                p = jnp.exp(s - m_new)
                if dchunk is not None:
                    p = jnp.where(_diag_mask(dchunk, True), p, 0.0)
                l = l * alpha + jnp.sum(p, axis=1, keepdims=True)
                acc = acc * alpha + _dot(p, v_ref[_krows(kc), :], NN)
                return m_new, l, acc

            init = (jnp.full((BQ, 1), NEG_BIG, F32), jnp.zeros((BQ, 1), F32), jnp.zeros((BQ, HEAD_DIM), F32))
            m, l, acc = _walk_chunks(i, step, init, False)
            o_ref[_qrows(i), :] = (acc / l).astype(BF16)
            lse_ref[_qrows(i), :] = m + jnp.log(l)
            return 0

        lax.fori_loop(0, t // BQ, qblock, 0)

    return pl.pallas_call(
        body, name=name, grid=(N_HEADS,),
        in_specs=[_head_spec(t, 3 * N_HEADS), _head_spec(t, 4 * N_HEADS), _head_spec(t, 5 * N_HEADS),
                  _col_spec(t), _row_spec(t // CS)],
        out_specs=[_head_spec(t, 0), _col_spec(t)],
        out_shape=[jax.ShapeDtypeStruct((N_HEADS, t, HEAD_DIM), BF16), jax.ShapeDtypeStruct((N_HEADS, t, 1), F32)],
        compiler_params=_params(("parallel",)),
    )(qkv, qkv, qkv, c_col, c_row)


def fox_bwd(name, qkv, qt, y, dy, dyt, lse, c_col, c_row):
    t = qkv.shape[1]
    nq, nc = t // BQ, t // CS

    def body(q_ref, k_ref, v_ref, qt_ref, o_ref, do_ref, dot_ref, lse_ref, cc_ref, cr_ref,
             dq_ref, dkt_ref, dvt_ref, dcc_ref, dcr_ref, dkt_acc, dvt_acc, dcr_acc):
        dkt_acc[...] = jnp.zeros_like(dkt_acc)
        dvt_acc[...] = jnp.zeros_like(dvt_acc)
        dcr_acc[...] = jnp.zeros_like(dcr_acc)

        def qblock(i, _):
            q = q_ref[_qrows(i), :]
            do = do_ref[_qrows(i), :]
            q_t = qt_ref[i]
            do_t = dot_ref[i]
            ct = cc_ref[_qrows(i), :]
            lse_i = lse_ref[_qrows(i), :]
            delta = jnp.sum(do.astype(F32) * o_ref[_qrows(i), :].astype(F32), axis=1, keepdims=True)

            def step(kc, state, dchunk):
                dq, dct = state
                s = _dot(q, k_ref[_krows(kc), :], NT) + ct - cr_ref[kc]
                p = jnp.exp(s - lse_i)
                if dchunk is not None:
                    p = jnp.where(_diag_mask(dchunk, True), p, 0.0)
                ds = p * (_dot(do, v_ref[_krows(kc), :], NT) - delta)
                dvt_acc[kc] += _dot(do_t, p, NN)
                dsb = ds.astype(BF16)
                dq = dq + _dot(dsb, k_ref[_krows(kc), :], NN)
                dkt_acc[kc] += _dot(q_t, dsb, NN)
                dcr_acc[kc] -= jnp.sum(ds, axis=0, keepdims=True)
                return dq, dct + jnp.sum(ds, axis=1, keepdims=True)

            dq, dct = _walk_chunks(i, step, (jnp.zeros((BQ, HEAD_DIM), F32), jnp.zeros((BQ, 1), F32)), False)
            dq_ref[_qrows(i), :] = (dq * ATTN_SCALE).astype(BF16)
            dcc_ref[_qrows(i), :] = dct
            return 0

        lax.fori_loop(0, nq, qblock, 0)
        dkt_ref[...] = dkt_acc[...].astype(BF16)
        dvt_ref[...] = dvt_acc[...].astype(BF16)
        dcr_ref[...] = dcr_acc[...]

    chunked = jax.ShapeDtypeStruct((N_HEADS, nc, HEAD_DIM, CS), BF16)
    return pl.pallas_call(
        body, name=name, grid=(N_HEADS,),
        in_specs=[_head_spec(t, 3 * N_HEADS), _head_spec(t, 4 * N_HEADS), _head_spec(t, 5 * N_HEADS), _head_t_spec(nq, N_HEADS),
                  _head_spec(t, 0), _head_spec(t, 0), _head_t_spec(nq, 0), _col_spec(t), _col_spec(t), _row_spec(nc)],
        out_specs=[_head_spec(t, 0), _chunk_t_spec(nc), _chunk_t_spec(nc), _col_spec(t), _row_spec(nc)],
        out_shape=[jax.ShapeDtypeStruct((N_HEADS, t, HEAD_DIM), BF16), chunked, chunked,
                   jax.ShapeDtypeStruct((N_HEADS, t, 1), F32), jax.ShapeDtypeStruct((N_HEADS, nc, 1, CS), F32)],
        scratch_shapes=[pltpu.VMEM((nc, HEAD_DIM, CS), F32), pltpu.VMEM((nc, HEAD_DIM, CS), F32), pltpu.VMEM((nc, 1, CS), F32)],
        compiler_params=_params(("parallel",)),
    )(qkv, qkv, qkv, qt, y, dy, dyt, lse, c_col, c_row)


PAIR = 2 * HEAD_DIM
N_PAIRS = N_HEADS // 2


def _pair_spec(t, first_block):
    return pl.BlockSpec((t, PAIR), lambda p, *_: (0, first_block + p))


def _head_lanes(shape):
    lane = _iota2(shape, len(shape) - 1)
    return [lane < HEAD_DIM, lane >= HEAD_DIM]


def _only_head(x, lanes_of_head):
    return jnp.where(lanes_of_head, x, jnp.zeros_like(x))


def _sb_chunk_weights(q_h, k, later, carry, dchunk):
    z = _dot(q_h, k, NT)
    sp = _log1pexp_neg_abs(z)
    lnb = -jnp.maximum(z, 0.0) - sp
    lsz = jnp.minimum(z, 0.0) - sp
    if dchunk is not None:
        lnb = jnp.where(_diag_mask(dchunk, False), lnb, 0.0)
    w = jnp.exp(lsz + _dot_split2_rhs01(lnb, later) + carry)
    if dchunk is not None:
        w = jnp.where(_diag_mask(dchunk, False), w, 0.0)
    return w, lsz, lnb


def sb_pair_fwd(name, qkv):
    t = qkv.shape[0]

    def body(q_ref, k_ref, v_ref, o_ref):
        later = (_iota2((CS, CS), 0) > _iota2((CS, CS), 1)).astype(BF16)
        lanes = _head_lanes((BQ, PAIR))

        def qblock(i, _):
            q = q_ref[_qrows(i), :]
            q_heads = [_only_head(q, lanes[h]) for h in range(2)]

            def step(kc, state, dchunk):
                k = k_ref[_krows(kc), :]
                v = v_ref[_krows(kc), :]
                out = []
                for h in range(2):
                    carry, acc = state[h]
                    w, _, lnb = _sb_chunk_weights(q_heads[h], k, later, carry, dchunk)
                    out.append((carry + jnp.sum(lnb, axis=1, keepdims=True), acc + _dot(w, v, NN)))
                return tuple(out)

            zero = (jnp.zeros((BQ, 1), F32), jnp.zeros((BQ, PAIR), F32))
            (_, acc0), (_, acc1) = _walk_chunks(i, step, (zero, zero), True)
            o_ref[_qrows(i), :] = jnp.where(lanes[0], acc0, acc1).astype(BF16)
            return 0

        lax.fori_loop(0, t // BQ, qblock, 0)

    return pl.pallas_call(
        body, name=name, grid=(N_PAIRS,),
        in_specs=[_pair_spec(t, 0), _pair_spec(t, N_PAIRS), _pair_spec(t, 2 * N_PAIRS)],
        out_specs=_pair_spec(t, 0), out_shape=jax.ShapeDtypeStruct((t, WIDTH), BF16),
        compiler_params=_params(("parallel",)),
    )(qkv, qkv, qkv)


def _emit_dqkv(res, o_ref):
    o_ref[...] = res[pl.program_id(1)]


def _flush_transposed(acc, res, which):
    for kc in range(acc.shape[0]):
        res[which, kc * CS:(kc + 1) * CS, :] = acc[kc].T.astype(BF16)


def sb_pair_bwd(name, qkv, dy, dqkv):
    t = qkv.shape[0]
    nc = t // CS

    def body(q_ref, k_ref, v_ref, do_ref, _, o_ref, g_s, b_s, dkt_acc, dvt_acc, res):
        @pl.when(pl.program_id(1) == 0)
        def _():
            later = (_iota2((CS, CS), 0) > _iota2((CS, CS), 1)).astype(BF16)
            earlier = (_iota2((CS, CS), 0) < _iota2((CS, CS), 1)).astype(BF16)
            lanes = _head_lanes((BQ, PAIR))
            dkt_acc[...] = jnp.zeros_like(dkt_acc)
            dvt_acc[...] = jnp.zeros_like(dvt_acc)

            def qblock(i, _):
                q = q_ref[_qrows(i), :]
                do = do_ref[_qrows(i), :]
                q_heads = [_only_head(q, lanes[h]) for h in range(2)]
                do_heads = [_only_head(do, lanes[h]) for h in range(2)]
                qt_heads = [qh.astype(F32).T.astype(BF16) for qh in q_heads]
                dot_heads = [dh.astype(F32).T.astype(BF16) for dh in do_heads]

                def step1(kc, carries, dchunk):
                    k = k_ref[_krows(kc), :]
                    v = v_ref[_krows(kc), :]
                    out = []
                    for h in range(2):
                        w, lsz, lnb = _sb_chunk_weights(q_heads[h], k, later, carries[h], dchunk)
                        g_s[h, kc] = (w * _dot(do_heads[h], v, NT)).astype(BF16)
                        b_s[h, kc] = jnp.exp(lsz).astype(BF16)
                        dvt_acc[kc] += _dot(dot_heads[h], w, NN)
                        out.append(carries[h] + jnp.sum(lnb, axis=1, keepdims=True))
                    return tuple(out)

                zero = jnp.zeros((BQ, 1), F32)
                _walk_chunks(i, step1, (zero, zero), True)

                def step2(kc, state, dchunk):
                    k = k_ref[_krows(kc), :]
                    out = []
                    for h in range(2):
                        before, dq = state[h]
                        g16 = g_s[h, kc]
                        g = g16.astype(F32)
                        beta = b_s[h, kc].astype(F32)
                        prefix = lax.dot_general(g16, earlier, (NN, ((), ())), preferred_element_type=F32) + before
                        dz = g * (1.0 - beta) - beta * prefix
                        if dchunk is not None:
                            dz = jnp.where(_diag_mask(dchunk, False), dz, 0.0)
                        dzb = dz.astype(BF16)
                        dkt_acc[kc] += _dot(qt_heads[h], dzb, NN)
                        out.append((before + jnp.sum(g, axis=1, keepdims=True), dq + _dot(dzb, k, NN)))
                    return tuple(out)

                start = (zero, jnp.zeros((BQ, PAIR), F32))
                (_, dq0), (_, dq1) = _walk_chunks(i, step2, (start, start), False)
                res[0, _qrows(i), :] = (jnp.where(lanes[0], dq0, dq1) * ATTN_SCALE).astype(BF16)
                return 0

            lax.fori_loop(0, t // BQ, qblock, 0)
            _flush_transposed(dkt_acc, res, 1)
            _flush_transposed(dvt_acc, res, 2)

        _emit_dqkv(res, o_ref)

    return pl.pallas_call(
        body, name=name, grid=(N_PAIRS, 3),
        in_specs=[_pair_spec(t, 0), _pair_spec(t, N_PAIRS), _pair_spec(t, 2 * N_PAIRS), _pair_spec(t, 0), _ANY],
        out_specs=pl.BlockSpec((t, PAIR), lambda p, s: (0, s * N_PAIRS + p)),
        out_shape=jax.ShapeDtypeStruct(dqkv.shape, BF16), input_output_aliases={4: 0},
        scratch_shapes=[pltpu.VMEM((2, nc, BQ, CS), BF16), pltpu.VMEM((2, nc, BQ, CS), BF16),
                        pltpu.VMEM((nc, PAIR, CS), F32), pltpu.VMEM((nc, PAIR, CS), F32), pltpu.VMEM((3, t, PAIR), BF16)],
        compiler_params=_params(("parallel", "arbitrary")),
    )(qkv, qkv, qkv, dy, dqkv)


def _gates_col_spec(t):
    return pl.BlockSpec((2, t, 1), lambda p, *_: (p, 0, 0))


def _gates_row_spec(nc):
    return pl.BlockSpec((2, nc, 1, CS), lambda p, *_: (p, 0, 0, 0))


def fox_pair_fwd(name, qkv, c_col, c_row):
    t = qkv.shape[0]

    def body(q_ref, k_ref, v_ref, cc_ref, cr_ref, o_ref, lse_ref):
        lanes = _head_lanes((BQ, PAIR))

        def qblock(i, _):
            q = q_ref[_qrows(i), :]
            q_heads = [_only_head(q, lanes[h]) for h in range(2)]
            ct = [cc_ref[h, _qrows(i), :] for h in range(2)]

            def step(kc, state, dchunk):
                k = k_ref[_krows(kc), :]
                v = v_ref[_krows(kc), :]
                out = []
                for h in range(2):
                    m, l, acc = state[h]
                    s = _dot(q_heads[h], k, NT) + ct[h] - cr_ref[h, kc]
                    if dchunk is not None:
                        s = jnp.where(_diag_mask(dchunk, True), s, NEG_BIG)
                    m_new = jnp.maximum(m, jnp.max(s, axis=1, keepdims=True))
                    alpha = jnp.exp(m - m_new)
                    p = jnp.exp(s - m_new)
                    if dchunk is not None:
                        p = jnp.where(_diag_mask(dchunk, True), p, 0.0)
                    out.append((m_new, l * alpha + jnp.sum(p, axis=1, keepdims=True), acc * alpha + _dot(p, v, NN)))
                return tuple(out)

            init = (jnp.full((BQ, 1), NEG_BIG, F32), jnp.zeros((BQ, 1), F32), jnp.zeros((BQ, PAIR), F32))
            (m0, l0, acc0), (m1, l1, acc1) = _walk_chunks(i, step, (init, init), False)
            o_ref[_qrows(i), :] = jnp.where(lanes[0], acc0 / l0, acc1 / l1).astype(BF16)
            lse_ref[0, _qrows(i), :] = m0 + jnp.log(l0)
            lse_ref[1, _qrows(i), :] = m1 + jnp.log(l1)
            return 0

        lax.fori_loop(0, t // BQ, qblock, 0)

    return pl.pallas_call(
        body, name=name, grid=(N_PAIRS,),
        in_specs=[_pair_spec(t, 3 * N_PAIRS), _pair_spec(t, 4 * N_PAIRS), _pair_spec(t, 5 * N_PAIRS),
                  _gates_col_spec(t), _gates_row_spec(t // CS)],
        out_specs=[_pair_spec(t, 0), _gates_col_spec(t)],
        out_shape=[jax.ShapeDtypeStruct((t, WIDTH), BF16), jax.ShapeDtypeStruct((N_HEADS, t, 1), F32)],
        compiler_params=_params(("parallel",)),
    )(qkv, qkv, qkv, c_col, c_row)


def fox_pair_bwd(name, qkv, y, dy, lse, c_col, c_row, dqkv):
    t = qkv.shape[0]
    nc = t // CS

    def body(q_ref, k_ref, v_ref, o_in_ref, do_ref, lse_ref, cc_ref, cr_ref, _, o_ref, dcc_ref, dcr_ref,
             dkt_acc, dvt_acc, dcr_acc, res):
        @pl.when(pl.program_id(1) == 0)
        def _():
            lanes = _head_lanes((BQ, PAIR))
            dkt_acc[...] = jnp.zeros_like(dkt_acc)
            dvt_acc[...] = jnp.zeros_like(dvt_acc)
            dcr_acc[...] = jnp.zeros_like(dcr_acc)

            def qblock(i, _):
                q = q_ref[_qrows(i), :]
                do = do_ref[_qrows(i), :]
                q_heads = [_only_head(q, lanes[h]) for h in range(2)]
                do_heads = [_only_head(do, lanes[h]) for h in range(2)]
                qt_heads = [qh.astype(F32).T.astype(BF16) for qh in q_heads]
                dot_heads = [dh.astype(F32).T.astype(BF16) for dh in do_heads]
                prod = do.astype(F32) * o_in_ref[_qrows(i), :].astype(F32)
                delta = [jnp.sum(_only_head(prod, lanes[h]), axis=1, keepdims=True) for h in range(2)]
                ct = [cc_ref[h, _qrows(i), :] for h in range(2)]
                lse_i = [lse_ref[h, _qrows(i), :] for h in range(2)]

                def step(kc, state, dchunk):
                    k = k_ref[_krows(kc), :]
                    v = v_ref[_krows(kc), :]
                    out = []
                    for h in range(2):
                        dq, dct = state[h]
                        s = _dot(q_heads[h], k, NT) + ct[h] - cr_ref[h, kc]
                        p = jnp.exp(s - lse_i[h])
                        if dchunk is not None:
                            p = jnp.where(_diag_mask(dchunk, True), p, 0.0)
                        ds = p * (_dot(do_heads[h], v, NT) - delta[h])
                        dvt_acc[kc] += _dot(dot_heads[h], p, NN)
                        dsb = ds.astype(BF16)
                        dkt_acc[kc] += _dot(qt_heads[h], dsb, NN)
                        dcr_acc[h, kc] -= jnp.sum(ds, axis=0, keepdims=True)
                        out.append((dq + _dot(dsb, k, NN), dct + jnp.sum(ds, axis=1, keepdims=True)))
                    return tuple(out)

                zero = (jnp.zeros((BQ, PAIR), F32), jnp.zeros((BQ, 1), F32))
                (dq0, dct0), (dq1, dct1) = _walk_chunks(i, step, (zero, zero), False)
                res[0, _qrows(i), :] = (jnp.where(lanes[0], dq0, dq1) * ATTN_SCALE).astype(BF16)
                dcc_ref[0, _qrows(i), :] = dct0
                dcc_ref[1, _qrows(i), :] = dct1
                return 0

            lax.fori_loop(0, t // BQ, qblock, 0)
            _flush_transposed(dkt_acc, res, 1)
            _flush_transposed(dvt_acc, res, 2)
            dcr_ref[...] = dcr_acc[...]

        _emit_dqkv(res, o_ref)

    return pl.pallas_call(
        body, name=name, grid=(N_PAIRS, 3),
        in_specs=[_pair_spec(t, 3 * N_PAIRS), _pair_spec(t, 4 * N_PAIRS), _pair_spec(t, 5 * N_PAIRS), _pair_spec(t, 0),
                  _pair_spec(t, 0), _gates_col_spec(t), _gates_col_spec(t), _gates_row_spec(nc), _ANY],
        out_specs=[pl.BlockSpec((t, PAIR), lambda p, s: (0, (3 + s) * N_PAIRS + p)), _gates_col_spec(t), _gates_row_spec(nc)],
        out_shape=[jax.ShapeDtypeStruct(dqkv.shape, BF16), jax.ShapeDtypeStruct((N_HEADS, t, 1), F32),
                   jax.ShapeDtypeStruct((N_HEADS, nc, 1, CS), F32)],
        input_output_aliases={8: 0},
        scratch_shapes=[pltpu.VMEM((nc, PAIR, CS), F32), pltpu.VMEM((nc, PAIR, CS), F32), pltpu.VMEM((2, nc, 1, CS), F32),
                        pltpu.VMEM((3, t, PAIR), BF16)],
        compiler_params=_params(("parallel", "arbitrary")),
    )(qkv, qkv, qkv, y, dy, lse, c_col, c_row, dqkv)


def loss_head(name, x, g, target):
    t, d = x.shape
    tr = _row_tile(t, 256)

    def body(x_ref, g_ref, t_ref, dx_ref, gn_ref, loss_ref):
        xv = x_ref[...]
        r = lax.rsqrt(jnp.mean(xv * xv, axis=-1, keepdims=True) + RMS_EPS)
        xhat = xv * r
        gv = g_ref[...]
        err = xhat * gv - t_ref[...]
        part_loss = 0.5 * jnp.sum(jnp.mean(err * err, axis=-1, keepdims=True), axis=0, keepdims=True)
        dy = err * (1.0 / d)
        dyg = dy * gv
        dx_ref[...] = r * (dyg - xhat * jnp.mean(dyg * xhat, axis=-1, keepdims=True))
        part_g = jnp.sum(dy * xhat, axis=0, keepdims=True)

        @pl.when(pl.program_id(0) == 0)
        def _():
            gn_ref[...] = part_g
            loss_ref[...] = part_loss

        @pl.when(pl.program_id(0) != 0)
        def _():
            gn_ref[...] += part_g
            loss_ref[...] += part_loss

    row = pl.BlockSpec((tr, d), lambda i: (i, 0))
    return pl.pallas_call(
        body, name=name, grid=(t // tr,),
        in_specs=[row, pl.BlockSpec((1, d), lambda i: (0, 0)), row],
        out_specs=[row, pl.BlockSpec((1, d), lambda i: (0, 0)), pl.BlockSpec((1, 1), lambda i: (0, 0))],
        out_shape=[jax.ShapeDtypeStruct((t, d), F32), jax.ShapeDtypeStruct((1, d), F32), jax.ShapeDtypeStruct((1, 1), F32)],
        compiler_params=_params(("arbitrary",)),
    )(x, g, target)


def _place():
    return lax.axis_index("x"), lax.axis_index("y"), lax.axis_index("c")


def _other_chips(x, y):
    return [(1 - x, y), (x, 1 - y), (1 - x, 1 - y)]


def _half(ref, c, rows):
    return ref.at[:, pl.ds(c * (rows // 2), rows // 2), :]


_ANY = pl.BlockSpec(memory_space=pl.ANY)


def gather_weights(name, bufs):
    n = len(bufs)

    def body(*refs):
        outs = refs[n:2 * n]
        send_sems, recv_sems = refs[2 * n:]
        x, y, c = _place()
        chips = _other_chips(x, y)
        me = 2 * x + y
        sibling = (x, y, 1 - c)
        first, passed = [], []
        for i in range(n):
            rows = outs[i].shape[2]
            mine = _half(outs[i].at[me], c, rows)
            for j, (qx, qy) in enumerate(chips):
                k = 6 * i + j
                rc = pltpu.make_async_remote_copy(
                    src_ref=mine, dst_ref=mine,
                    send_sem=send_sems.at[k], recv_sem=recv_sems.at[k], device_id=(qx, qy, c), device_id_type=MESH)
                rc.start()
                first.append(rc)
        for i in range(n):
            rows = outs[i].shape[2]
            for j, (qx, qy) in enumerate(chips):
                k = 6 * i + j
                block = _half(outs[i].at[2 * qx + qy], c, rows)
                pltpu.make_async_remote_copy(
                    src_ref=block, dst_ref=block, send_sem=send_sems.at[k], recv_sem=recv_sems.at[k],
                    device_id=(qx, qy, c), device_id_type=MESH).wait_recv()
                fw = pltpu.make_async_remote_copy(
                    src_ref=block, dst_ref=block, send_sem=send_sems.at[k + 3], recv_sem=recv_sems.at[k + 3],
                    device_id=sibling, device_id_type=MESH)
                fw.start()
                passed.append(fw)
        for i in range(n):
            rows = outs[i].shape[2]
            for j, (qx, qy) in enumerate(chips):
                k = 6 * i + j + 3
                block = _half(outs[i].at[2 * qx + qy], 1 - c, rows)
                pltpu.make_async_remote_copy(
                    src_ref=block, dst_ref=block, send_sem=send_sems.at[k], recv_sem=recv_sems.at[k],
                    device_id=sibling, device_id_type=MESH).wait_recv()
        for cp in first + passed:
            cp.wait_send()

    return pl.pallas_call(
        body, name=name, in_specs=[_ANY] * n, out_specs=[_ANY] * n,
        out_shape=[jax.ShapeDtypeStruct(b.shape, b.dtype) for b in bufs],
        input_output_aliases={i: i for i in range(n)},
        scratch_shapes=[pltpu.SemaphoreType.DMA((6 * n,)), pltpu.SemaphoreType.DMA((6 * n,))],
        compiler_params=pltpu.CompilerParams(has_side_effects=True),
    )(*bufs)


_HBM = pl.BlockSpec(memory_space=pltpu.HBM)
_SEM = pl.BlockSpec(memory_space=pltpu.SEMAPHORE)
_DATAFLOW = pltpu.SideEffectType.DATAFLOW_SIDE_EFFECTING


def _in_hbm(a):
    return pltpu.with_memory_space_constraint(a, pltpu.HBM)


def _gather_ici_copies(bufs, send_sems, recv_sems, arrivals):
    x, y, c = _place()
    me = 2 * x + y
    copies = []
    for i, buf in enumerate(bufs):
        rows = buf.shape[2]
        for j, (qx, qy) in enumerate(_other_chips(x, y)):
            block = _half(buf.at[2 * qx + qy if arrivals else me], c, rows)
            copies.append(pltpu.make_async_remote_copy(
                src_ref=block, dst_ref=block, send_sem=send_sems.at[3 * i + j], recv_sem=recv_sems.at[3 * i + j],
                device_id=(qx, qy, c), device_id_type=MESH))
    return copies


def gather_ici_start(name, bufs, after):
    n = len(bufs)

    def body(*refs):
        ins = refs[:n]
        send_sems, recv_sems = refs[n + 1], refs[n + 2]
        token = refs[-1]
        for send in _gather_ici_copies(ins, send_sems, recv_sems, False):
            send.start()
        token[...] = jnp.zeros_like(token)

    res = pl.pallas_call(
        body, name=name,
        out_shape=(pltpu.SemaphoreType.DMA((3 * n,)), pltpu.SemaphoreType.DMA((3 * n,)), *[pltpu.HBM(b.shape, b.dtype) for b in bufs],
                   jax.ShapeDtypeStruct((8, 128), F32)),
        in_specs=[_HBM] * n + [_ANY], out_specs=(_SEM, _SEM, *[_HBM] * n, pl.BlockSpec(memory_space=pltpu.VMEM)),
        input_output_aliases={i: 2 + i for i in range(n)},
        compiler_params=pltpu.CompilerParams(has_side_effects=_DATAFLOW),
    )(*[_in_hbm(b) for b in bufs], after)
    return res[0], res[1], list(res[2:2 + n]), res[-1]


def gather_ici_wait(name, send_sems, recv_sems, bufs, after):
    n = len(bufs)

    def body(*refs):
        ins = refs[:n]
        send_sems_ref, recv_sems_ref = refs[n], refs[n + 1]
        for send in _gather_ici_copies(ins, send_sems_ref, recv_sems_ref, False):
            send.wait_send()
        for recv in _gather_ici_copies(ins, send_sems_ref, recv_sems_ref, True):
            recv.wait_recv()

    return pl.pallas_call(
        body, name=name, out_shape=tuple(pltpu.HBM(b.shape, b.dtype) for b in bufs),
        in_specs=[_HBM] * n + [_SEM, _SEM, _ANY], out_specs=tuple([_HBM] * n),
        input_output_aliases={i: i for i in range(n)},
        compiler_params=pltpu.CompilerParams(has_side_effects=_DATAFLOW),
    )(*bufs, send_sems, recv_sems, after)


def gather_forward(name, bufs):
    n = len(bufs)

    def body(*refs):
        outs = refs[n:2 * n]
        send_sems, recv_sems = refs[2 * n:]
        x, y, c = _place()
        sibling = (x, y, 1 - c)
        sends = []
        for i in range(n):
            rows = outs[i].shape[2]
            for j, (qx, qy) in enumerate(_other_chips(x, y)):
                block = _half(outs[i].at[2 * qx + qy], c, rows)
                fw = pltpu.make_async_remote_copy(
                    src_ref=block, dst_ref=block, send_sem=send_sems.at[3 * i + j], recv_sem=recv_sems.at[3 * i + j],
                    device_id=sibling, device_id_type=MESH)
                fw.start()
                sends.append(fw)
        for i in range(n):
            rows = outs[i].shape[2]
            for j, (qx, qy) in enumerate(_other_chips(x, y)):
                block = _half(outs[i].at[2 * qx + qy], 1 - c, rows)
                pltpu.make_async_remote_copy(
                    src_ref=block, dst_ref=block, send_sem=send_sems.at[3 * i + j], recv_sem=recv_sems.at[3 * i + j],
                    device_id=sibling, device_id_type=MESH).wait_recv()
        for fw in sends:
            fw.wait_send()

    return pl.pallas_call(
        body, name=name, in_specs=[_ANY] * n, out_specs=[_ANY] * n,
        out_shape=[jax.ShapeDtypeStruct(b.shape, b.dtype) for b in bufs],
        input_output_aliases={i: i for i in range(n)},
        scratch_shapes=[pltpu.SemaphoreType.DMA((3 * n,)), pltpu.SemaphoreType.DMA((3 * n,))],
        compiler_params=pltpu.CompilerParams(has_side_effects=True),
    )(*bufs)


def _between_chips_copies(parts, lands, send_sems, recv_sems):
    x, y, c = _place()
    copies = []
    for i, (part, land) in enumerate(zip(parts, lands)):
        for j, (qx, qy) in enumerate(_other_chips(x, y)):
            copies.append(pltpu.make_async_remote_copy(
                src_ref=part.at[2 * qx + qy], dst_ref=land.at[j], send_sem=send_sems.at[3 * i + j], recv_sem=recv_sems.at[3 * i + j],
                device_id=(qx, qy, c), device_id_type=MESH))
    return copies


def between_chips_start(name, parts):
    n = len(parts)
    lands = [lax.empty((N_CHIPS - 1,) + p.shape[1:], p.dtype) for p in parts]

    def body(*refs):
        send_sems, recv_sems = refs[2 * n], refs[2 * n + 1]
        token = refs[-1]
        for cp in _between_chips_copies(refs[:n], refs[n:2 * n], send_sems, recv_sems):
            cp.start()
        token[...] = jnp.zeros_like(token)

    res = pl.pallas_call(
        body, name=name,
        out_shape=(pltpu.SemaphoreType.DMA((3 * n,)), pltpu.SemaphoreType.DMA((3 * n,)),
                   *[pltpu.HBM(a.shape, a.dtype) for a in parts + lands], jax.ShapeDtypeStruct((8, 128), F32)),
        in_specs=[_HBM] * (2 * n), out_specs=(_SEM, _SEM, *[_HBM] * (2 * n), pl.BlockSpec(memory_space=pltpu.VMEM)),
        input_output_aliases={i: 2 + i for i in range(2 * n)},
        compiler_params=pltpu.CompilerParams(has_side_effects=_DATAFLOW),
    )(*[_in_hbm(a) for a in parts + lands])
    return res[0], res[1], list(res[2:2 + n]), list(res[2 + n:2 + 2 * n]), res[-1]


def between_chips_wait(name, send_sems, recv_sems, parts, lands, after):
    n = len(parts)

    def body(*refs):
        for cp in _between_chips_copies(refs[:n], refs[n:2 * n], refs[2 * n], refs[2 * n + 1]):
            cp.wait_send()
            cp.wait_recv()

    res = pl.pallas_call(
        body, name=name, out_shape=tuple(pltpu.HBM(a.shape, a.dtype) for a in parts + lands),
        in_specs=[_HBM] * (2 * n) + [_SEM, _SEM, _ANY], out_specs=tuple([_HBM] * (2 * n)),
        input_output_aliases={i: i for i in range(2 * n)},
        compiler_params=pltpu.CompilerParams(has_side_effects=_DATAFLOW),
    )(*parts, *lands, send_sems, recv_sems, after)
    return list(res[:n]), list(res[n:])


def send_half_to_sibling(name, grads):
    n = len(grads)

    def body(*refs):
        srcs, outs = refs[:n], refs[n:2 * n]
        send_sems, recv_sems = refs[2 * n:]
        x, y, c = _place()
        sibling = (x, y, 1 - c)
        copies = []
        for i in range(n):
            rows = srcs[i].shape[2]
            rc = pltpu.make_async_remote_copy(
                src_ref=srcs[i].at[:, :, pl.ds((1 - c) * (rows // 2), rows // 2), :], dst_ref=outs[i],
                send_sem=send_sems.at[i], recv_sem=recv_sems.at[i], device_id=sibling, device_id_type=MESH)
            rc.start()
            copies.append(rc)
        for rc in copies:
            rc.wait()

    def half_shape(g):
        s = g.shape
        return jax.ShapeDtypeStruct((s[0], s[1], s[2] // 2, s[3]), g.dtype)

    return pl.pallas_call(
        body, name=name, in_specs=[_ANY] * n, out_specs=[_ANY] * n, out_shape=[half_shape(g) for g in grads],
        scratch_shapes=[pltpu.SemaphoreType.DMA((n,)), pltpu.SemaphoreType.DMA((n,))],
        compiler_params=pltpu.CompilerParams(has_side_effects=True),
    )(*grads)


def exchange_between_chips(name, parts):
    n = len(parts)

    def body(*refs):
        srcs, outs = refs[:n], refs[n:2 * n]
        send_sems, recv_sems = refs[2 * n:]
        x, y, c = _place()
        chips = _other_chips(x, y)
        copies = []
        for i in range(n):
            for j, (qx, qy) in enumerate(chips):
                k = 3 * i + j
                rc = pltpu.make_async_remote_copy(
                    src_ref=srcs[i].at[2 * qx + qy], dst_ref=outs[i].at[j],
                    send_sem=send_sems.at[k], recv_sem=recv_sems.at[k], device_id=(qx, qy, c), device_id_type=MESH)
                rc.start()
                copies.append(rc)
        for rc in copies:
            rc.wait()

    return pl.pallas_call(
        body, name=name, in_specs=[_ANY] * n, out_specs=[_ANY] * n,
        out_shape=[jax.ShapeDtypeStruct((N_CHIPS - 1,) + p.shape[1:], p.dtype) for p in parts],
        scratch_shapes=[pltpu.SemaphoreType.DMA((3 * n,)), pltpu.SemaphoreType.DMA((3 * n,))],
        compiler_params=pltpu.CompilerParams(has_side_effects=True),
    )(*parts)


def share_halves(name, bufs):
    n = len(bufs)

    def body(*refs):
        outs = refs[n:2 * n]
        send_sems, recv_sems = refs[2 * n:]
        x, y, c = _place()
        copies = []
        for i in range(n):
            mine = _half(outs[i], c, outs[i].shape[1])
            rc = pltpu.make_async_remote_copy(
                src_ref=mine, dst_ref=mine, send_sem=send_sems.at[i], recv_sem=recv_sems.at[i],
                device_id=(x, y, 1 - c), device_id_type=MESH)
            rc.start()
            copies.append(rc)
        for i in range(n):
            theirs = _half(outs[i], 1 - c, outs[i].shape[1])
            pltpu.make_async_remote_copy(
                src_ref=theirs, dst_ref=theirs, send_sem=send_sems.at[i], recv_sem=recv_sems.at[i],
                device_id=(x, y, 1 - c), device_id_type=MESH).wait_recv()
        for rc in copies:
            rc.wait_send()

    return pl.pallas_call(
        body, name=name, in_specs=[_ANY] * n, out_specs=[_ANY] * n,
        out_shape=[jax.ShapeDtypeStruct(b.shape, b.dtype) for b in bufs],
        input_output_aliases={i: i for i in range(n)},
        scratch_shapes=[pltpu.SemaphoreType.DMA((n,)), pltpu.SemaphoreType.DMA((n,))],
        compiler_params=pltpu.CompilerParams(has_side_effects=True),
    )(*bufs)


def pair_sum(name, grad, recv, c):
    ns, na, rh, cols = recv.shape
    tr = _row_tile(rh, 256) if rh % 256 == 0 else rh
    nt = rh // tr

    def body(c_ref, g_ref, r_ref, o_ref):
        o_ref[...] = (g_ref[...].astype(F32) + r_ref[...].astype(F32)).astype(BF16)

    blk = (None, None, tr, cols)
    return pl.pallas_call(
        body, name=name,
        grid_spec=pltpu.PrefetchScalarGridSpec(
            num_scalar_prefetch=1, grid=(ns, na, nt),
            in_specs=[pl.BlockSpec(blk, lambda s, a, r, c_ref: (s, a, c_ref[0] * nt + r, 0)),
                      pl.BlockSpec(blk, lambda s, a, r, c_ref: (s, a, r, 0))],
            out_specs=pl.BlockSpec(blk, lambda s, a, r, c_ref: (s, a, r, 0))),
        out_shape=jax.ShapeDtypeStruct(recv.shape, BF16),
        compiler_params=_params(("parallel", "parallel", "parallel")),
    )(c, grad, recv)


def chip_sum(name, parts, landed, place):
    _, na, rh, cols = parts.shape
    tr = _row_tile(rh, 256) if rh % 256 == 0 else rh
    nt = rh // tr

    def body(place_ref, p_ref, l_ref, o_ref):
        total = p_ref[...].astype(F32)
        for s in range(N_CHIPS - 1):
            total = total + l_ref[s].astype(F32)
        o_ref[...] = total

    return pl.pallas_call(
        body, name=name,
        grid_spec=pltpu.PrefetchScalarGridSpec(
            num_scalar_prefetch=1, grid=(na, nt),
            in_specs=[pl.BlockSpec((None, None, tr, cols), lambda a, r, pr: (pr[1], a, r, 0)),
                      pl.BlockSpec((N_CHIPS - 1, None, tr, cols), lambda a, r, pr: (0, a, r, 0))],
            out_specs=pl.BlockSpec((None, tr, cols), lambda a, r, pr: (a, pr[0] * nt + r, 0))),
        out_shape=jax.ShapeDtypeStruct((na, 2 * rh, cols), F32),
        compiler_params=_params(("parallel", "parallel")),
    )(place, parts, landed)


def reduce_scatter_begin(tag, grads, place):
    recv = send_half_to_sibling(f"{tag}_to_sibling", grads)
    parts = [pair_sum(f"{tag}_pair_sum{i}", g, r, place) for i, (g, r) in enumerate(zip(grads, recv))]
    send_sems, recv_sems, parts, lands, token = between_chips_start(f"{tag}_between_chips_start", parts)
    return (send_sems, recv_sems, parts, lands), token


def reduce_scatter_end(tag, state, place, after):
    send_sems, recv_sems, parts, lands = state
    parts, landed = between_chips_wait(f"{tag}_between_chips_wait", send_sems, recv_sems, parts, lands, after)
    halves = [chip_sum(f"{tag}_chip_sum{i}", p, l, place) for i, (p, l) in enumerate(zip(parts, landed))]
    return share_halves(f"{tag}_share", halves)


def _adamw_math(w, g, m, v):
    m = ADAM_B1 * m + (1.0 - ADAM_B1) * g
    v = ADAM_B2 * v + (1.0 - ADAM_B2) * (g * g)
    m_hat = m / (1.0 - ADAM_B1 ** ADAM_STEP)
    v_hat = v / (1.0 - ADAM_B2 ** ADAM_STEP)
    delta = -ADAM_LR * (m_hat / (jnp.sqrt(v_hat) + ADAM_EPS) + ADAM_WD * w)
    return delta, m, v


def adamw(name, w, g, m, v, after=()):
    rows, cols = w.shape
    tr = _row_tile(rows, 256) if rows % 256 == 0 else rows // 2

    def body(w_ref, g_ref, m_ref, v_ref, d_ref, mo_ref, vo_ref):
        d_ref[...], mo_ref[...], vo_ref[...] = _adamw_math(w_ref[...], g_ref[...], m_ref[...], v_ref[...])

    blk = pl.BlockSpec((tr, cols), lambda i: (i, 0))
    return pl.pallas_call(
        _ordered_after(body, 4, len(after)), name=name, grid=(rows // tr,), in_specs=[blk] * 4 + [_ANY] * len(after),
        out_specs=[blk] * 3, out_shape=[jax.ShapeDtypeStruct(w.shape, F32)] * 3, compiler_params=_params(("parallel",)),
    )(w, g, m, v, *after)


def adamw_stacked(name, ws, g, ms, vs, after=()):
    n = len(ws)
    rows, cols = ws[0].shape
    tr = next(r for r in (128, 88, 64, 32, 16, 8) if rows % r == 0)

    def body(*refs):
        w_refs, m_refs, v_refs, g_ref = refs[:n], refs[n:2 * n], refs[2 * n:3 * n], refs[3 * n]
        outs = refs[3 * n + 1:]
        for i in range(n):
            outs[i][...], outs[n + i][...], outs[2 * n + i][...] = _adamw_math(
                w_refs[i][...], g_ref[i], m_refs[i][...], v_refs[i][...])

    blk = pl.BlockSpec((tr, cols), lambda r: (r, 0))
    res = pl.pallas_call(
        _ordered_after(body, 3 * n + 1, len(after)), name=name, grid=(rows // tr,),
        in_specs=[blk] * (3 * n) + [pl.BlockSpec((n, tr, cols), lambda r: (0, r, 0))] + [_ANY] * len(after),
        out_specs=[blk] * (3 * n),
        out_shape=[jax.ShapeDtypeStruct((rows, cols), F32)] * (3 * n), compiler_params=_params(("parallel",)),
    )(*ws, *ms, *vs, g, *after)
    return res[:n], res[n:2 * n], res[2 * n:]


def small_allreduce_adamw(name, g_part, w, m, v):
    rows, cols = g_part.shape

    def body(g_ref, w_ref, m_ref, v_ref, sum_ref, d_ref, mo_ref, vo_ref, land, send_sems, recv_sems):
        x, y, c = _place()
        me = 4 * x + 2 * y + c
        land[me] = g_ref[...]
        copies = []
        for r in range(1, 8):
            peer = (x ^ (r >> 2), y ^ ((r >> 1) & 1), c ^ (r & 1))
            rc = pltpu.make_async_remote_copy(
                src_ref=g_ref, dst_ref=land.at[me], send_sem=send_sems.at[r - 1], recv_sem=recv_sems.at[r - 1],
                device_id=peer, device_id_type=MESH)
            rc.start()
            copies.append(rc)
        for rc in copies:
            rc.wait()
        total = land[0]
        for s in range(1, 8):
            total = total + land[s]
        sum_ref[...] = total
        d_ref[...], mo_ref[...], vo_ref[...] = _adamw_math(w_ref[...], total, m_ref[...], v_ref[...])

    vmem = pl.BlockSpec(memory_space=pltpu.VMEM)
    return pl.pallas_call(
        body, name=name, in_specs=[vmem] * 4, out_specs=[vmem] * 4,
        out_shape=[jax.ShapeDtypeStruct((rows, cols), F32)] * 4,
        scratch_shapes=[pltpu.VMEM((8, rows, cols), F32), pltpu.SemaphoreType.DMA((7,)), pltpu.SemaphoreType.DMA((7,))],
        compiler_params=pltpu.CompilerParams(has_side_effects=True),
    )(g_part, w, m, v)


def _heads(a):
    t, w = a.shape
    return a.reshape(t, w // HEAD_DIM, HEAD_DIM).transpose(1, 0, 2)


def _unheads(a):
    n, t, _ = a.shape
    return a.transpose(1, 0, 2).reshape(t, n * HEAD_DIM)


def _cols_from_shards(g):
    ns, r, cols = g.shape
    return g.transpose(1, 0, 2).reshape(r, ns * cols)


def _shards_from_cols(a):
    r, cols = a.shape
    return a.reshape(r, N_CHIPS, cols // N_CHIPS).transpose(1, 0, 2)


def kernel(x, norm_ffn1, w_ffn1_gate, w_ffn1_up, w_ffn1_down, norm_mix, w_in, b_forget, w_gate, b_gate, w_up_a, w_up_b, w_out, norm_ffn2, w_ffn2_gate, w_ffn2_up, w_ffn2_down, norm_final, loss_target, m_norm_ffn1, m_w_ffn1_gate, m_w_ffn1_up, m_w_ffn1_down, m_norm_mix, m_w_in, m_b_forget, m_w_gate, m_b_gate, m_w_up_a, m_w_up_b, m_w_out, m_norm_ffn2, m_w_ffn2_gate, m_w_ffn2_up, m_w_ffn2_down, m_norm_final, v_norm_ffn1, v_w_ffn1_gate, v_w_ffn1_up, v_w_ffn1_down, v_norm_mix, v_w_in, v_b_forget, v_w_gate, v_b_gate, v_w_up_a, v_w_up_b, v_w_out, v_norm_ffn2, v_w_ffn2_gate, v_w_ffn2_up, v_w_ffn2_down, v_norm_final):
    t, d = x.shape[1], x.shape[2]
    in4 = w_in.shape[2]
    gate4 = w_gate.shape[2]
    up4 = w_up_a.shape[2]
    in_cols = N_CHIPS * in4
    n_forget = in_cols - QKV_COLS
    assert w_up_a.shape[1] == WIDTH and d == 2 * WIDTH and n_forget == N_HEADS
    nq = t // BQ
    chip = 2 * lax.axis_index("x") + lax.axis_index("y")
    c_arr = jnp.stack([lax.axis_index("c"), chip]).astype(jnp.int32)
    x2d = x[0]
    target = loss_target[0]

    def slot(shard):
        return lax.dynamic_update_slice(jnp.zeros((N_CHIPS,) + shard.shape, BF16), shard.astype(BF16)[None], (chip, 0, 0, 0))

    def ffn_views(wg, wu, wd):
        return [wg[0].T, wu[0].T, wd[0]]

    ffn1_w, ffn1_m, ffn1_v = (ffn_views(w_ffn1_gate, w_ffn1_up, w_ffn1_down), ffn_views(m_w_ffn1_gate, m_w_ffn1_up, m_w_ffn1_down),
                              ffn_views(v_w_ffn1_gate, v_w_ffn1_up, v_w_ffn1_down))
    ffn2_w, ffn2_m, ffn2_v = (ffn_views(w_ffn2_gate, w_ffn2_up, w_ffn2_down), ffn_views(m_w_ffn2_gate, m_w_ffn2_up, m_w_ffn2_down),
                              ffn_views(v_w_ffn2_gate, v_w_ffn2_up, v_w_ffn2_down))
    mx_sh = slot(jnp.concatenate([w_in[0], w_gate[0], jnp.concatenate([w_up_a[0], w_up_b[0]], axis=0)], axis=1)[None])
    wo_sh = slot(w_out)
    f1_send, f1_recv, f1_bufs, f1_token = gather_ici_start("gather_ffn1_start", [slot(jnp.stack(ffn1_w))], norm_ffn1)
    mx_send, mx_recv, mx_bufs, mx_token = gather_ici_start("gather_mixer_start", [mx_sh, wo_sh], f1_token)
    f2_send, f2_recv, f2_bufs, f2_token = gather_ici_start("gather_ffn2_start", [slot(jnp.stack(ffn2_w))], mx_token)

    normed1 = rms_fwd("ffn1_rms", x2d, norm_ffn1, after=(f2_token,))
    (w3_1,) = gather_forward("gather_ffn1_forward", gather_ici_wait("gather_ffn1_wait", f1_send, f1_recv, f1_bufs, normed1[0]))
    x1, saved1 = ffn_forward("ffn1", x2d, norm_ffn1, w3_1, normed=normed1)
    mx, wo = gather_forward("gather_mixer_forward", gather_ici_wait("gather_mixer_wait", mx_send, mx_recv, mx_bufs, x1))
    mx = mx[:, 0]
    w_in_full = _cols_from_shards(mx[:, :, :in4])
    w_gate_full = _cols_from_shards(mx[:, :, in4:in4 + gate4])
    w_up_full = _cols_from_shards(mx[:, :, in4 + gate4:])
    w_up_a_full, w_up_b_full = w_up_full[:WIDTH], w_up_full[WIDTH:]
    wcat = jnp.concatenate([w_in_full[:, :QKV_COLS], w_gate_full, w_in_full[:, QKV_COLS:],
                            jnp.zeros((d, F_PAD - n_forget), BF16)], axis=1)
    bias_cat = jnp.concatenate([jnp.zeros((1, QKV_COLS), F32), b_gate, jnp.zeros((1, F_PAD), F32)], axis=1)
    f_off = QKV_COLS + 2 * d
    wo_full = wo.reshape(d, d)
    b_forget_row = jnp.pad(b_forget, ((0, 0), (0, QB - n_forget)))

    h2, rstd2 = rms_fwd("mix_rms", x1, norm_mix)
    qkv = proj("mix_proj_qkv", h2, wcat, bias_cat, 0, QKV_COLS, WIDTH, BF16, scaled_tiles=(0, 3))
    pc = proj("mix_proj_gates", h2, wcat, bias_cat, QKV_COLS, 2 * d + F_PAD, 768, F32)
    f_logit = pc[:, 2 * d:2 * d + QB]
    c_cum = fox_prep("fox_prep", f_logit, b_forget_row)
    c_heads = c_cum[:, :N_HEADS].T
    c_col = c_heads[:, :, None]
    c_row = c_heads.reshape(N_HEADS, t // CS, 1, CS)
    ya = sb_pair_fwd("sb_fwd", qkv)
    yb, lse = fox_pair_fwd("fox_fwd", qkv, c_col, c_row)
    ua, ub, mixed = mix_fwd("mix_fwd", ya, yb, w_up_a_full, w_up_b_full, pc)
    x2 = mm_residual("mix_out", mixed[None], wo_full[None], pl.BlockSpec((1, d, d), lambda m: (0, 0, 0)), x1, 1.0)
    (w3_2,) = gather_forward("gather_ffn2_forward", gather_ici_wait("gather_ffn2_wait", f2_send, f2_recv, f2_bufs, x2))
    x3, saved2 = ffn_forward("ffn2", x2, norm_ffn2, w3_2)
    dx3, gn_final, loss_part = loss_head("loss_head", x3, norm_final[None], target)

    dx2, gn_ffn2, g_w3_2 = ffn_backward("ffn2", dx3, saved2, norm_ffn2, w3_2)
    rs_ffn2, rs_ffn2_token = reduce_scatter_begin("rs_ffn2", [g_w3_2], c_arr)

    dua, dub, dpa, dpb, gba, gbb = mix_bwd("mix_bwd", dx2, wo_full, pc, ua, ub, after=(rs_ffn2_token,))
    dgp = jnp.concatenate([dpa, dpb], axis=1)
    g_bgate = jnp.concatenate([gba, gbb], axis=1)
    g_wo = mm_plain("wgrad_out", mixed, dx2, TN, BF16, tk_target=1024)
    dya = mm_plain("dya", dua, w_up_a_full, NT, BF16, tk_target=1024)
    dyb = mm_plain("dyb", dub, w_up_b_full, NT, BF16, tk_target=1024)
    g_up_a = mm_plain("wgrad_up_a", ya, dua, TN, BF16, tk_target=2048)
    g_up_b = mm_plain("wgrad_up_b", yb, dub, TN, BF16, tk_target=2048)
    dqkv = sb_pair_bwd("sb_bwd", qkv, dya, lax.empty((t, QKV_COLS), BF16))
    dqkv, dcc, dcr = fox_pair_bwd("fox_bwd", qkv, yb, dyb, lse, c_col, c_row, dqkv)
    dc = (dcc[:, :, 0] + dcr.reshape(N_HEADS, t)).T
    df, g_bf = fox_gate_bwd("fox_gate_bwd", jnp.pad(dc, ((0, 0), (0, QB - N_HEADS))), f_logit, b_forget_row)
    dcat = jnp.concatenate([dqkv, dgp, df.astype(BF16), jnp.zeros((t, F_PAD - QB), BF16)], axis=1)
    g_wcat = wgrad_cat("wgrad_cat", h2, dcat)
    tm = _row_tile(t, 512)
    tkc = 1792
    nkc = wcat.shape[1] // tkc
    dx1, gn_mix = dh_rms_bwd(
        "mix_dh", [(dcat, wcat)],
        [(pl.BlockSpec((tm, tkc), lambda m, k: (m, k)), pl.BlockSpec((d, tkc), lambda m, k: (0, k)))], NT,
        (t // tm, nkc), nkc, x1, rstd2, norm_mix, dx2)
    g_in = jnp.concatenate([g_wcat[:, :QKV_COLS], g_wcat[:, f_off:f_off + n_forget]], axis=1)
    g_mx = jnp.concatenate([_shards_from_cols(g_in), _shards_from_cols(g_wcat[:, QKV_COLS:f_off]),
                            _shards_from_cols(jnp.concatenate([g_up_a, g_up_b], axis=0))], axis=2)[:, None]
    rs_mixer, rs_mixer_token = reduce_scatter_begin("rs_mixer", [g_mx, g_wo.reshape(N_CHIPS, 1, d // N_CHIPS, d)], c_arr)
    (s_w3_2,) = reduce_scatter_end("rs_ffn2", rs_ffn2, c_arr, dx1)

    dx0, gn_ffn1, g_w3_1 = ffn_backward("ffn1", dx1, saved1, norm_ffn1, w3_1, after=(rs_mixer_token,))
    rs_ffn1, rs_ffn1_token = reduce_scatter_begin("rs_ffn1", [g_w3_1], c_arr)
    s_mx, s_wo = reduce_scatter_end("rs_mixer", rs_mixer, c_arr, dx0)

    def pack_small(n1, nm, n2, nf, bg, bf, last):
        return jnp.concatenate([n1, nm, n2, nf, bg.reshape(2, d), jnp.pad(bf, ((0, 0), (0, d - n_forget))), last], axis=0)

    zero_row = jnp.zeros((1, d), F32)
    g_small = pack_small(gn_ffn1, gn_mix, gn_ffn2, gn_final, g_bgate, g_bf[:, :n_forget], jnp.pad(loss_part, ((0, 0), (0, d - 1))))
    w_small = pack_small(norm_ffn1, norm_mix, norm_ffn2, norm_final[None], b_gate, b_forget, zero_row)
    m_small = pack_small(m_norm_ffn1, m_norm_mix, m_norm_ffn2, m_norm_final[None], m_b_gate, m_b_forget, zero_row)
    v_small = pack_small(v_norm_ffn1, v_norm_mix, v_norm_ffn2, v_norm_final[None], v_b_gate, v_b_forget, zero_row)
    smalls = small_allreduce_adamw("small_allreduce_adamw", g_small, w_small, m_small, v_small)

    def unpack_small(p):
        return {"norm_ffn1": p[0:1], "norm_mix": p[1:2], "norm_ffn2": p[2:3], "norm_final": p[3], "b_gate": p[4:6].reshape(1, 2 * d),
                "b_forget": p[6:7, :n_forget]}

    loss = smalls[0][7, 0]
    small_out = [unpack_small(p) for p in smalls]

    grads = {
        "w_in": s_mx[0][:, :in4], "w_gate": s_mx[0][:, in4:in4 + gate4],
        "w_up_a": s_mx[0][:WIDTH, in4 + gate4:], "w_up_b": s_mx[0][WIDTH:, in4 + gate4:], "w_out": s_wo[0],
    }
    weights = {"w_in": (w_in, m_w_in, v_w_in), "w_gate": (w_gate, m_w_gate, v_w_gate), "w_up_a": (w_up_a, m_w_up_a, v_w_up_a),
               "w_up_b": (w_up_b, m_w_up_b, v_w_up_b), "w_out": (w_out, m_w_out, v_w_out)}
    big_out = {}
    for wname, (w, m, v) in weights.items():
        g = grads[wname]
        delta, new_m, new_v = adamw(f"adamw_{wname}", w[0], g, m[0], v[0], after=(rs_ffn1_token,))
        big_out[wname] = (g[None], delta[None], new_m[None], new_v[None])

    def adamw_ffn(tag, s_w3, ws, ms, vs, after):
        deltas, new_ms, new_vs = adamw_stacked(f"adamw_{tag}", ws, s_w3, ms, vs, after)
        for which, part in ((GATE, "gate"), (UP, "up"), (DOWN, "down")):
            back = (lambda a: a[None]) if which == DOWN else (lambda a: a.T[None])
            big_out[f"w_{tag}_{part}"] = tuple(back(a) for a in (s_w3[which], deltas[which], new_ms[which], new_vs[which]))
        return deltas[DOWN]

    last = adamw_ffn("ffn2", s_w3_2, ffn2_w, ffn2_m, ffn2_v, (rs_ffn1_token,))
    (s_w3_1,) = reduce_scatter_end("rs_ffn1", rs_ffn1, c_arr, last)
    adamw_ffn("ffn1", s_w3_1, ffn1_w, ffn1_m, ffn1_v, ())

    order = ["norm_ffn1", "w_ffn1_gate", "w_ffn1_up", "w_ffn1_down", "norm_mix", "w_in", "b_forget", "w_gate", "b_gate",
             "w_up_a", "w_up_b", "w_out", "norm_ffn2", "w_ffn2_gate", "w_ffn2_up", "w_ffn2_down", "norm_final"]
    outs = [loss, dx0[None]]
    for kind in range(4):
        for wname in order:
            outs.append(big_out[wname][kind] if wname in big_out else small_out[kind][wname])
    return tuple(outs)
```

```python
import functools

import jax
import jax.numpy as jnp
from jax import lax
from jax.experimental import pallas as pl
from jax.experimental.pallas import tpu as pltpu

F32 = jnp.float32
BF16 = jnp.bfloat16

HEAD_DIM = 64
N_HEADS = 8
WIDTH = N_HEADS * HEAD_DIM
QKV_COLS = 6 * WIDTH
RMS_EPS = 1e-6
ATTN_SCALE = HEAD_DIM ** -0.5
N_CHIPS = 4
QB = 128
BQ = 512
CS = 256
N_SUB = BQ // CS
F_PAD = 256
NEG_BIG = -1e30

ADAM_LR = 0.001
ADAM_B1 = 0.9
ADAM_B2 = 0.999
ADAM_EPS = 1e-08
ADAM_WD = 0.01
ADAM_STEP = 10

VMEM_LIMIT_BYTES = 48 * 1024 * 1024
MESH = pl.DeviceIdType.MESH

NN = ((1,), (0,))
NT = ((1,), (1,))
TN = ((0,), (0,))


def _params(semantics):
    return pltpu.CompilerParams(dimension_semantics=semantics, vmem_limit_bytes=VMEM_LIMIT_BYTES)


def _dot(a, b, contract):
    return lax.dot_general(a.astype(BF16), b.astype(BF16), (contract, ((), ())), preferred_element_type=F32)


def _sigmoid(x):
    return 1.0 / (1.0 + jnp.exp(-x))


def _log1pexp_neg_abs(z):
    return jnp.log(1.0 + jnp.exp(-jnp.abs(z)))


def _split3(x):
    hi = x.astype(BF16)
    r1 = x - hi.astype(F32)
    mid = r1.astype(BF16)
    lo = (r1 - mid.astype(F32)).astype(BF16)
    return hi, mid, lo


def _dot_exact_rhs01(x, m01):
    hi, mid, lo = _split3(x)
    d = lambda p: lax.dot_general(p, m01, (NN, ((), ())), preferred_element_type=F32)
    return d(hi) + d(mid) + d(lo)


def _dot_exact_lhs01(m01, x):
    hi, mid, lo = _split3(x)
    d = lambda p: lax.dot_general(m01, p, (NN, ((), ())), preferred_element_type=F32)
    return d(hi) + d(mid) + d(lo)


def _iota2(shape, dim):
    return lax.broadcasted_iota(jnp.int32, shape, dim)


def _mm(name, pairs, contract, grid, pair_specs, out_shape, out_specs, acc_shape, nk, epilogue,
        extras=(), extra_specs=(), semantics=None):
    n_pairs = len(pairs)
    n_extra = len(extras)
    n_out = len(out_shape)

    def body(*refs):
        ab = refs[:2 * n_pairs]
        ex = refs[2 * n_pairs:2 * n_pairs + n_extra]
        outs = refs[2 * n_pairs + n_extra:2 * n_pairs + n_extra + n_out]
        ids = [pl.program_id(i) for i in range(len(grid))]
        k = ids[-1]
        part = _dot(ab[0][...], ab[1][...], contract)
        for p in range(1, n_pairs):
            part += _dot(ab[2 * p][...], ab[2 * p + 1][...], contract)
        if nk == 1:
            epilogue(part, ex, outs, ids)
            return
        acc = refs[-1]

        @pl.when(k == 0)
        def _():
            acc[...] = part

        @pl.when(k != 0)
        def _():
            acc[...] += part

        @pl.when(k == nk - 1)
        def _():
            epilogue(acc[...], ex, outs, ids)

    operands = [t for pair in pairs for t in pair] + list(extras)
    in_specs = [s for pair in pair_specs for s in pair] + list(extra_specs)
    if semantics is None:
        semantics = ("parallel",) * (len(grid) - 1) + ("arbitrary",)
    return pl.pallas_call(
        body, name=name, grid=grid, in_specs=in_specs, out_specs=list(out_specs), out_shape=list(out_shape),
        scratch_shapes=[] if nk == 1 else [pltpu.VMEM(acc_shape, F32)], compiler_params=_params(semantics),
    )(*operands)


def _ordered_after(body, n_in, n_after):
    def wrapped(*refs):
        return body(*refs[:n_in], *refs[n_in + n_after:])
    return wrapped


def _row_tile(rows, target):
    t = min(rows, target)
    while rows % t:
        t //= 2
    return t


def rms_fwd(name, x, g, after=()):
    t, d = x.shape
    tr = _row_tile(t, 256)

    def body(x_ref, g_ref, h_ref, r_ref):
        xv = x_ref[...]
        r = lax.rsqrt(jnp.mean(xv * xv, axis=-1, keepdims=True) + RMS_EPS)
        h_ref[...] = (xv * r * g_ref[...]).astype(BF16)
        r_ref[...] = r

    return pl.pallas_call(
        _ordered_after(body, 2, len(after)), name=name, grid=(t // tr,),
        in_specs=[pl.BlockSpec((tr, d), lambda i: (i, 0)), pl.BlockSpec((1, d), lambda i: (0, 0))] + [_ANY] * len(after),
        out_specs=[pl.BlockSpec((tr, d), lambda i: (i, 0)), pl.BlockSpec((tr, 1), lambda i: (i, 0))],
        out_shape=[jax.ShapeDtypeStruct((t, d), BF16), jax.ShapeDtypeStruct((t, 1), F32)],
        compiler_params=_params(("parallel",)),
    )(x, g, *after)


GATE, UP, DOWN = 0, 1, 2


def _ffn_w_spec(which, f4, d, index_of_j):
    return pl.BlockSpec((None, None, f4, d), lambda *ids: (index_of_j(*ids), which, 0, 0))


def ffn_up(name, h, w3):
    t, d = h.shape
    ns, _, f4, _ = w3.shape
    tm = _row_tile(t, 512)

    def body(h_ref, wg_ref, wu_ref, a_ref, b_ref, s_ref):
        hv = h_ref[...]
        a = _dot(hv, wg_ref[...], NT)
        b = _dot(hv, wu_ref[...], NT)
        a_ref[...] = a.astype(BF16)
        b_ref[...] = b.astype(BF16)
        s_ref[...] = (a * _sigmoid(a) * b).astype(BF16)

    act_spec = pl.BlockSpec((None, tm, f4), lambda j, m: (j, m, 0))
    return pl.pallas_call(
        body, name=name, grid=(ns, t // tm),
        in_specs=[pl.BlockSpec((tm, d), lambda j, m: (m, 0)),
                  _ffn_w_spec(GATE, f4, d, lambda j, m: j), _ffn_w_spec(UP, f4, d, lambda j, m: j)],
        out_specs=[act_spec, act_spec, act_spec],
        out_shape=[jax.ShapeDtypeStruct((ns, t, f4), BF16)] * 3,
        compiler_params=_params(("parallel", "parallel")),
    )(h, w3, w3)


def mm_residual(name, s, w, w_spec, x, scale):
    nj, t, kdim = s.shape
    n = x.shape[1]
    tm = _row_tile(t, 512)

    def body(s_ref, w_ref, x_ref, o_ref):
        acc = _dot(s_ref[0], w_ref[0], NN)
        for j in range(1, nj):
            acc += _dot(s_ref[j], w_ref[j], NN)
        o_ref[...] = x_ref[...] + scale * acc

    row = pl.BlockSpec((tm, n), lambda m: (m, 0))
    return pl.pallas_call(
        body, name=name, grid=(t // tm,),
        in_specs=[pl.BlockSpec((nj, tm, kdim), lambda m: (0, m, 0)), w_spec, row], out_specs=row,
        out_shape=jax.ShapeDtypeStruct((t, n), F32), compiler_params=_params(("parallel",)),
    )(s, w, x)


def ffn_bwd_act(name, dx, w3, a, b, after=()):
    t, d = dx.shape
    ns, _, f4, _ = w3.shape
    tm = _row_tile(t, 512)

    def body(dx_ref, wd_ref, a_ref, b_ref, da_ref, db_ref):
        ds = _dot(0.5 * dx_ref[...], wd_ref[...], NT)
        av = a_ref[...].astype(F32)
        sig = _sigmoid(av)
        da_ref[...] = (ds * b_ref[...].astype(F32) * (sig * (1.0 + av * (1.0 - sig)))).astype(BF16)
        db_ref[...] = (ds * (av * sig)).astype(BF16)

    act_spec = pl.BlockSpec((None, tm, f4), lambda j, m: (j, m, 0))
    return pl.pallas_call(
        _ordered_after(body, 4, len(after)), name=name, grid=(ns, t // tm),
        in_specs=[pl.BlockSpec((tm, d), lambda j, m: (m, 0)), _ffn_w_spec(DOWN, f4, d, lambda j, m: j), act_spec, act_spec]
        + [_ANY] * len(after),
        out_specs=[act_spec, act_spec],
        out_shape=[jax.ShapeDtypeStruct((ns, t, f4), BF16)] * 2,
        compiler_params=_params(("parallel", "parallel")),
    )(dx, w3, a, b, *after)


def ffn_wgrad(name, h, da, db, s, dx):
    t, d = h.shape
    ns, _, f4 = da.shape
    tk = _row_tile(t, 1024)
    nk = t // tk

    def body(h_ref, da_ref, db_ref, s_ref, dx_ref, o_ref, acc):
        k = pl.program_id(1)

        @pl.when(k == 0)
        def _():
            acc[...] = jnp.zeros_like(acc)

        hv = h_ref[...]
        acc[GATE] += _dot(da_ref[...], hv, TN)
        acc[UP] += _dot(db_ref[...], hv, TN)
        acc[DOWN] += _dot(s_ref[...], 0.5 * dx_ref[...], TN)

        @pl.when(k == nk - 1)
        def _():
            o_ref[...] = acc[...].astype(BF16)

    act_spec = pl.BlockSpec((None, tk, f4), lambda j, k: (j, k, 0))
    row_spec = pl.BlockSpec((tk, d), lambda j, k: (k, 0))
    return pl.pallas_call(
        body, name=name, grid=(ns, nk),
        in_specs=[row_spec, act_spec, act_spec, act_spec, row_spec],
        out_specs=pl.BlockSpec((None, 3, f4, d), lambda j, k: (j, 0, 0, 0)),
        out_shape=jax.ShapeDtypeStruct((ns, 3, f4, d), BF16),
        scratch_shapes=[pltpu.VMEM((3, f4, d), F32)],
        compiler_params=_params(("parallel", "arbitrary")),
    )(h, da, db, s, dx)


def _rms_bwd_tail(dh, x_ref, r_ref, g_ref, dxin_ref, dx_ref, gn_ref, row_tile_index):
    r = r_ref[...]
    xhat = x_ref[...] * r
    dhg = dh * g_ref[...]
    dx_ref[...] = dxin_ref[...] + r * (dhg - xhat * jnp.mean(dhg * xhat, axis=-1, keepdims=True))
    part = jnp.sum(dh * xhat, axis=0, keepdims=True)

    @pl.when(row_tile_index == 0)
    def _():
        gn_ref[...] = part

    @pl.when(row_tile_index != 0)
    def _():
        gn_ref[...] += part


def ffn_dh(name, da, db, w3, x, rstd, g, dx_in):
    ns, t, f4 = da.shape
    d = x.shape[1]
    tm = _row_tile(t, 256)

    def body(da_ref, db_ref, wg_ref, wu_ref, x_ref, r_ref, g_ref, dxin_ref, dx_ref, gn_ref):
        dh = _dot(da_ref[0], wg_ref[0], NN) + _dot(db_ref[0], wu_ref[0], NN)
        for j in range(1, ns):
            dh += _dot(da_ref[j], wg_ref[j], NN) + _dot(db_ref[j], wu_ref[j], NN)
        _rms_bwd_tail(dh, x_ref, r_ref, g_ref, dxin_ref, dx_ref, gn_ref, pl.program_id(0))

    act = pl.BlockSpec((ns, tm, f4), lambda m: (0, m, 0))
    row = pl.BlockSpec((tm, d), lambda m: (m, 0))
    gain = pl.BlockSpec((1, d), lambda m: (0, 0))
    return pl.pallas_call(
        body, name=name, grid=(t // tm,),
        in_specs=[act, act, pl.BlockSpec((ns, None, f4, d), lambda m: (0, GATE, 0, 0)),
                  pl.BlockSpec((ns, None, f4, d), lambda m: (0, UP, 0, 0)), row, pl.BlockSpec((tm, 1), lambda m: (m, 0)), gain, row],
        out_specs=[row, gain], out_shape=[jax.ShapeDtypeStruct((t, d), F32), jax.ShapeDtypeStruct((1, d), F32)],
        compiler_params=_params(("arbitrary",)),
    )(da, db, w3, w3, x, rstd, g, dx_in)


def dh_rms_bwd(name, pairs, pair_specs, contract, grid, nk, x, rstd, g, dx_in):
    t, d = x.shape
    tm = t // grid[0]

    def epilogue(acc, ex, outs, ids):
        _rms_bwd_tail(acc, *ex, *outs, ids[0])

    row = pl.BlockSpec((tm, d), lambda m, k: (m, 0))
    return _mm(
        name, pairs, contract, grid, pair_specs,
        [jax.ShapeDtypeStruct((t, d), F32), jax.ShapeDtypeStruct((1, d), F32)],
        [row, pl.BlockSpec((1, d), lambda m, k: (0, 0))], (tm, d), nk, epilogue,
        extras=[x, rstd, g, dx_in],
        extra_specs=[row, pl.BlockSpec((tm, 1), lambda m, k: (m, 0)), pl.BlockSpec((1, d), lambda m, k: (0, 0)), row],
        semantics=("arbitrary", "arbitrary"),
    )


def ffn_forward(tag, x, g_norm, w3, normed=None):
    _, _, f4, d = w3.shape
    h, rstd = normed if normed is not None else rms_fwd(f"{tag}_rms", x, g_norm)
    a, b, s = ffn_up(f"{tag}_up", h, w3)
    x_out = mm_residual(f"{tag}_down", s, w3, pl.BlockSpec((w3.shape[0], None, f4, d), lambda m: (0, DOWN, 0, 0)), x, 0.5)
    return x_out, (x, h, rstd, a, b, s)


def ffn_backward(tag, dx, saved, g_norm, w3, after=()):
    x, h, rstd, a, b, s = saved
    t, d = x.shape
    da, db = ffn_bwd_act(f"{tag}_bwd_act", dx, w3, a, b, after)
    g_w3 = ffn_wgrad(f"{tag}_wgrad", h, da, db, s, dx)
    dx_out, g_n = ffn_dh(f"{tag}_dh", da, db, w3, x, rstd, g_norm, dx)
    return dx_out, g_n, g_w3


def proj(name, h, wcat_t, bias, first_col, n_cols, tn, out_dtype, scaled_tiles=()):
    t, d = h.shape
    tm = _row_tile(t, 512)
    off = first_col // tn

    def epilogue(acc, ex, outs, ids):
        val = acc + ex[0][...]
        if scaled_tiles:
            hit = functools.reduce(jnp.logical_or, [ids[0] == s for s in scaled_tiles])
            val = val * jnp.where(hit, ATTN_SCALE, 1.0)
        outs[0][...] = val.astype(out_dtype)

    return _mm(
        name, [(h, wcat_t)], NT, (n_cols // tn, t // tm, 1),
        [(pl.BlockSpec((tm, d), lambda j, m, k: (m, 0)), pl.BlockSpec((tn, d), lambda j, m, k: (off + j, 0)))],
        [jax.ShapeDtypeStruct((t, n_cols), out_dtype)], [pl.BlockSpec((tm, tn), lambda j, m, k: (m, j))], (tm, tn), 1, epilogue,
        extras=[bias], extra_specs=[pl.BlockSpec((1, tn), lambda j, m, k: (0, off + j))],
    )[0]


def mix_fwd(name, ya, yb, wup, pc):
    t, w = ya.shape
    ns, _, tn = wup.shape
    d = ns * tn
    tm = _row_tile(t, 512)

    def body(ya_ref, yb_ref, wa_ref, wb_ref, pa_ref, pb_ref, ua_ref, ub_ref, mx_ref):
        ua = _dot(ya_ref[...], wa_ref[...], NN)
        ub = _dot(yb_ref[...], wb_ref[...], NN)
        ua_ref[...] = ua
        ub_ref[...] = ub
        mx_ref[...] = (_sigmoid(pa_ref[...]) * ua + _sigmoid(pb_ref[...]) * ub).astype(BF16)

    y_spec = pl.BlockSpec((tm, w), lambda m, n: (m, 0))
    o_spec = pl.BlockSpec((tm, tn), lambda m, n: (m, n))
    return pl.pallas_call(
        body, name=name, grid=(t // tm, ns),
        in_specs=[y_spec, y_spec, pl.BlockSpec((None, w, tn), lambda m, n: (n, 0, 0)), pl.BlockSpec((None, w, tn), lambda m, n: (n, 1, 0)),
                  o_spec, pl.BlockSpec((tm, tn), lambda m, n: (m, ns + n))],
        out_specs=[o_spec, o_spec, o_spec],
        out_shape=[jax.ShapeDtypeStruct((t, d), F32), jax.ShapeDtypeStruct((t, d), F32), jax.ShapeDtypeStruct((t, d), BF16)],
        compiler_params=_params(("parallel", "parallel")),
    )(ya, yb, wup, wup, pc, pc)


def up_bwd(name, du, wup, branch):
    t, d = du.shape
    ns, w2, tn = wup.shape
    w = w2 // 2
    tm = _row_tile(t, 512)

    def body(du_ref, w_ref, o_ref):
        acc = _dot(du_ref[:, 0:tn], w_ref[0], NT)
        for j in range(1, ns):
            acc += _dot(du_ref[:, j * tn:(j + 1) * tn], w_ref[j], NT)
        o_ref[...] = acc.astype(BF16)

    return pl.pallas_call(
        body, name=name, grid=(t // tm,),
        in_specs=[pl.BlockSpec((tm, d), lambda m: (m, 0)), pl.BlockSpec((ns, w, tn), lambda m: (0, branch, 0))],
        out_specs=pl.BlockSpec((tm, w), lambda m: (m, 0)), out_shape=jax.ShapeDtypeStruct((t, w), BF16),
        compiler_params=_params(("parallel",)),
    )(du, wup)


def up_wgrad(name, ya, yb, dua, dub):
    t, w = ya.shape
    d = dua.shape[1]
    tn = d // N_CHIPS

    def body(ya_ref, yb_ref, dua_ref, dub_ref, o_ref):
        o_ref[0:w, :] = _dot(ya_ref[...], dua_ref[...], TN).astype(BF16)
        o_ref[w:2 * w, :] = _dot(yb_ref[...], dub_ref[...], TN).astype(BF16)

    y_spec = pl.BlockSpec((t, w), lambda j: (0, 0))
    du_spec = pl.BlockSpec((t, tn), lambda j: (0, j))
    return pl.pallas_call(
        body, name=name, grid=(N_CHIPS,), in_specs=[y_spec, y_spec, du_spec, du_spec],
        out_specs=pl.BlockSpec((None, 2 * w, tn), lambda j: (j, 0, 0)),
        out_shape=jax.ShapeDtypeStruct((N_CHIPS, 2 * w, tn), BF16), compiler_params=_params(("parallel",)),
    )(ya, yb, dua, dub)


def mix_bwd(name, dx, wo, pc, ua, ub, after=()):
    t, d = dx.shape
    tm = _row_tile(t, 512)
    tn = 512
    off_a = 0
    off_b = d // tn

    def body(dx_ref, wo_ref, pa_ref, pb_ref, ua_ref, ub_ref, dua_ref, dub_ref, dpa_ref, dpb_ref, ba_ref, bb_ref):
        dm = _dot(dx_ref[...], wo_ref[...], NT)
        ga = _sigmoid(pa_ref[...])
        gb = _sigmoid(pb_ref[...])
        dua_ref[...] = (dm * ga).astype(BF16)
        dub_ref[...] = (dm * gb).astype(BF16)
        dpa = dm * ua_ref[...] * ga * (1.0 - ga)
        dpb = dm * ub_ref[...] * gb * (1.0 - gb)
        dpa_ref[...] = dpa.astype(BF16)
        dpb_ref[...] = dpb.astype(BF16)
        sa = jnp.sum(dpa, axis=0, keepdims=True)
        sb = jnp.sum(dpb, axis=0, keepdims=True)

        @pl.when(pl.program_id(1) == 0)
        def _():
            ba_ref[...] = sa
            bb_ref[...] = sb

        @pl.when(pl.program_id(1) != 0)
        def _():
            ba_ref[...] += sa
            bb_ref[...] += sb

    tile = pl.BlockSpec((tm, tn), lambda n, m: (m, n))
    bias = pl.BlockSpec((1, tn), lambda n, m: (0, n))
    return pl.pallas_call(
        _ordered_after(body, 6, len(after)), name=name, grid=(d // tn, t // tm),
        in_specs=[pl.BlockSpec((tm, d), lambda n, m: (m, 0)), pl.BlockSpec((tn, d), lambda n, m: (n, 0)),
                  pl.BlockSpec((tm, tn), lambda n, m: (m, off_a + n)), pl.BlockSpec((tm, tn), lambda n, m: (m, off_b + n)),
                  tile, tile] + [_ANY] * len(after),
        out_specs=[tile, tile, tile, tile, bias, bias],
        out_shape=[jax.ShapeDtypeStruct((t, d), BF16)] * 4 + [jax.ShapeDtypeStruct((1, d), F32)] * 2,
        compiler_params=_params(("parallel", "arbitrary")),
    )(dx, wo, pc, pc, ua, ub, *after)


def mm_plain(name, a, b, contract, out_dtype, tk_target=512):
    if contract == NN:
        m, kdim = a.shape
        n = b.shape[1]
    elif contract == NT:
        m, kdim = a.shape
        n = b.shape[0]
    else:
        kdim, m = a.shape
        n = b.shape[1]
    tm = _row_tile(m, 512)
    tk = _row_tile(kdim, tk_target)
    nk = kdim // tk
    if contract == TN:
        a_spec = pl.BlockSpec((tk, tm), lambda i, k: (k, i))
    else:
        a_spec = pl.BlockSpec((tm, tk), lambda i, k: (i, k))
    if contract == NT:
        b_spec = pl.BlockSpec((n, tk), lambda i, k: (0, k))
    else:
        b_spec = pl.BlockSpec((tk, n), lambda i, k: (k, 0))

    def epilogue(acc, ex, outs, ids):
        outs[0][...] = acc.astype(out_dtype)

    return _mm(name, [(a, b)], contract, (m // tm, nk), [(a_spec, b_spec)],
               [jax.ShapeDtypeStruct((m, n), out_dtype)], [pl.BlockSpec((tm, n), lambda i, k: (i, 0))], (tm, n), nk, epilogue)[0]


def wgrad_cat(name, h, dcat):
    t, d = h.shape
    n = dcat.shape[1]
    tn = 768
    tk = _row_tile(t, 2048)

    def epilogue(acc, ex, outs, ids):
        outs[0][...] = acc.astype(BF16)

    return _mm(
        name, [(dcat, h)], TN, (n // tn, t // tk),
        [(pl.BlockSpec((tk, tn), lambda j, k: (k, j)), pl.BlockSpec((tk, d), lambda j, k: (k, 0)))],
        [jax.ShapeDtypeStruct((n, d), BF16)], [pl.BlockSpec((tn, d), lambda j, k: (j, 0))], (tn, d), t // tk, epilogue,
    )[0]


def fox_prep(name, f, bias):
    t, lanes = f.shape
    nchunk = t // QB

    def body(f_ref, b_ref, c_ref):
        lower = (_iota2((QB, QB), 1) <= _iota2((QB, QB), 0)).astype(BF16)

        def chunk(n, carry):
            rows = pl.ds(pl.multiple_of(n * QB, QB), QB)
            u = f_ref[rows, :] + b_ref[...]
            lf = jnp.minimum(u, 0.0) - _log1pexp_neg_abs(u)
            c = _dot_exact_lhs01(lower, lf) + carry
            c_ref[rows, :] = c
            return c[QB - 1:QB, :]

        lax.fori_loop(0, nchunk, chunk, jnp.zeros((1, lanes), F32))

    return pl.pallas_call(body, name=name, out_shape=jax.ShapeDtypeStruct((t, lanes), F32),
                          compiler_params=pltpu.CompilerParams(vmem_limit_bytes=VMEM_LIMIT_BYTES))(f, bias)


def fox_gate_bwd(name, dc, f, bias):
    t, lanes = dc.shape
    nchunk = t // QB

    def body(dc_ref, f_ref, b_ref, df_ref, gb_ref):
        upper = (_iota2((QB, QB), 1) >= _iota2((QB, QB), 0)).astype(BF16)

        def chunk(n, carry):
            tail, total = carry
            rows = pl.ds(pl.multiple_of((nchunk - 1 - n) * QB, QB), QB)
            dlf = _dot_exact_lhs01(upper, dc_ref[rows, :]) + tail
            u = f_ref[rows, :] + b_ref[...]
            df = dlf * jnp.exp(jnp.minimum(-u, 0.0) - _log1pexp_neg_abs(u))
            df_ref[rows, :] = df
            return dlf[0:1, :], total + jnp.sum(df, axis=0, keepdims=True)

        zero = jnp.zeros((1, lanes), F32)
        _, total = lax.fori_loop(0, nchunk, chunk, (zero, zero))
        gb_ref[...] = total

    return pl.pallas_call(body, name=name,
                          out_shape=[jax.ShapeDtypeStruct((t, lanes), F32), jax.ShapeDtypeStruct((1, lanes), F32)],
                          compiler_params=pltpu.CompilerParams(vmem_limit_bytes=VMEM_LIMIT_BYTES))(dc, f, bias)


def _qrows(i):
    return pl.ds(pl.multiple_of(i * BQ, BQ), BQ)


def _krows(kc):
    return pl.ds(pl.multiple_of(kc * CS, CS), CS)


def _head_spec(t, offset):
    return pl.BlockSpec((None, t, HEAD_DIM), lambda h: (offset + h, 0, 0))


def _head_t_spec(nq, offset):
    return pl.BlockSpec((None, nq, HEAD_DIM, BQ), lambda h: (offset + h, 0, 0, 0))


def _chunk_t_spec(nc):
    return pl.BlockSpec((None, nc, HEAD_DIM, CS), lambda h: (h, 0, 0, 0))


def _dot_split2_rhs01(x, m01):
    hi = x.astype(BF16)
    lo = (x - hi.astype(F32)).astype(BF16)
    d = lambda p: lax.dot_general(p, m01, (NN, ((), ())), preferred_element_type=F32)
    return d(hi) + d(lo)


def _diag_mask(dchunk, inclusive):
    r_io = _iota2((BQ, CS), 0)
    c_io = _iota2((BQ, CS), 1) + dchunk * CS
    return c_io <= r_io if inclusive else c_io < r_io


def _walk_chunks(i, step, init, right_to_left):
    order = list(reversed(range(N_SUB))) if right_to_left else list(range(N_SUB))

    def diagonal(state):
        for dchunk in order:
            state = step(i * N_SUB + dchunk, state, dchunk)
        return state

    def group(n, state):
        base = ((i - 1 - n) if right_to_left else n) * N_SUB
        for dchunk in order:
            state = step(base + dchunk, state, None)
        return state

    if right_to_left:
        return lax.fori_loop(0, i, group, diagonal(init))
    return diagonal(lax.fori_loop(0, i, group, init))


def sb_fwd(name, qkv):
    t = qkv.shape[1]

    def body(q_ref, k_ref, v_ref, o_ref):
        later = (_iota2((CS, CS), 0) > _iota2((CS, CS), 1)).astype(BF16)

        def qblock(i, _):
            q = q_ref[_qrows(i), :]

            def step(kc, state, dchunk):
                carry, acc = state
                z = _dot(q, k_ref[_krows(kc), :], NT)
                sp = _log1pexp_neg_abs(z)
                lnb = -jnp.maximum(z, 0.0) - sp
                if dchunk is not None:
                    lnb = jnp.where(_diag_mask(dchunk, False), lnb, 0.0)
                w = jnp.exp(jnp.minimum(z, 0.0) - sp + _dot_split2_rhs01(lnb, later) + carry)
                if dchunk is not None:
                    w = jnp.where(_diag_mask(dchunk, False), w, 0.0)
                acc = acc + _dot(w, v_ref[_krows(kc), :], NN)
                return carry + jnp.sum(lnb, axis=1, keepdims=True), acc

            init = (jnp.zeros((BQ, 1), F32), jnp.zeros((BQ, HEAD_DIM), F32))
            _, acc = _walk_chunks(i, step, init, True)
            o_ref[_qrows(i), :] = acc.astype(BF16)
            return 0

        lax.fori_loop(0, t // BQ, qblock, 0)

    return pl.pallas_call(
        body, name=name, grid=(N_HEADS,),
        in_specs=[_head_spec(t, 0), _head_spec(t, N_HEADS), _head_spec(t, 2 * N_HEADS)],
        out_specs=_head_spec(t, 0), out_shape=jax.ShapeDtypeStruct((N_HEADS, t, HEAD_DIM), BF16),
        compiler_params=_params(("parallel",)),
    )(qkv, qkv, qkv)


def sb_bwd(name, qkv, qt, dy, dyt):
    t = qkv.shape[1]
    nq, nc = t // BQ, t // CS

    def body(q_ref, k_ref, v_ref, qt_ref, do_ref, dot_ref, dq_ref, dkt_ref, dvt_ref, g_s, b_s, dkt_acc, dvt_acc):
        later = (_iota2((CS, CS), 0) > _iota2((CS, CS), 1)).astype(BF16)
        earlier = (_iota2((CS, CS), 0) < _iota2((CS, CS), 1)).astype(BF16)
        dkt_acc[...] = jnp.zeros_like(dkt_acc)
        dvt_acc[...] = jnp.zeros_like(dvt_acc)

        def qblock(i, _):
            q = q_ref[_qrows(i), :]
            do = do_ref[_qrows(i), :]
            q_t = qt_ref[i]
            do_t = dot_ref[i]

            def step1(kc, carry, dchunk):
                z = _dot(q, k_ref[_krows(kc), :], NT)
                sp = _log1pexp_neg_abs(z)
                lnb = -jnp.maximum(z, 0.0) - sp
                lsz = jnp.minimum(z, 0.0) - sp
                if dchunk is not None:
                    lnb = jnp.where(_diag_mask(dchunk, False), lnb, 0.0)
                w = jnp.exp(lsz + _dot_split2_rhs01(lnb, later) + carry)
                if dchunk is not None:
                    w = jnp.where(_diag_mask(dchunk, False), w, 0.0)
                g_s[kc] = w * _dot(do, v_ref[_krows(kc), :], NT)
                b_s[kc] = jnp.exp(lsz)
                dvt_acc[kc] += _dot(do_t, w, NN)
                return carry + jnp.sum(lnb, axis=1, keepdims=True)

            _walk_chunks(i, step1, jnp.zeros((BQ, 1), F32), True)

            def step2(kc, state, dchunk):
                before, dq = state
                g = g_s[kc]
                beta = b_s[kc]
                dz = g * (1.0 - beta) - beta * (_dot_split2_rhs01(g, earlier) + before)
                if dchunk is not None:
                    dz = jnp.where(_diag_mask(dchunk, False), dz, 0.0)
                dzb = dz.astype(BF16)
                dq = dq + _dot(dzb, k_ref[_krows(kc), :], NN)
                dkt_acc[kc] += _dot(q_t, dzb, NN)
                return before + jnp.sum(g, axis=1, keepdims=True), dq

            _, dq = _walk_chunks(i, step2, (jnp.zeros((BQ, 1), F32), jnp.zeros((BQ, HEAD_DIM), F32)), False)
            dq_ref[_qrows(i), :] = (dq * ATTN_SCALE).astype(BF16)
            return 0

        lax.fori_loop(0, nq, qblock, 0)
        dkt_ref[...] = dkt_acc[...].astype(BF16)
        dvt_ref[...] = dvt_acc[...].astype(BF16)

    chunked = jax.ShapeDtypeStruct((N_HEADS, nc, HEAD_DIM, CS), BF16)
    return pl.pallas_call(
        body, name=name, grid=(N_HEADS,),
        in_specs=[_head_spec(t, 0), _head_spec(t, N_HEADS), _head_spec(t, 2 * N_HEADS), _head_t_spec(nq, 0),
                  _head_spec(t, 0), _head_t_spec(nq, 0)],
        out_specs=[_head_spec(t, 0), _chunk_t_spec(nc), _chunk_t_spec(nc)],
        out_shape=[jax.ShapeDtypeStruct((N_HEADS, t, HEAD_DIM), BF16), chunked, chunked],
        scratch_shapes=[pltpu.VMEM((nc, BQ, CS), F32), pltpu.VMEM((nc, BQ, CS), F32),
                        pltpu.VMEM((nc, HEAD_DIM, CS), F32), pltpu.VMEM((nc, HEAD_DIM, CS), F32)],
        compiler_params=_params(("parallel",)),
    )(qkv, qkv, qkv, qt, dy, dyt)


def _col_spec(t):
    return pl.BlockSpec((None, t, 1), lambda h: (h, 0, 0))


def _row_spec(nc):
    return pl.BlockSpec((None, nc, 1, CS), lambda h: (h, 0, 0, 0))


def fox_fwd(name, qkv, c_col, c_row):
    t = qkv.shape[1]

    def body(q_ref, k_ref, v_ref, cc_ref, cr_ref, o_ref, lse_ref):
        def qblock(i, _):
            q = q_ref[_qrows(i), :]
            ct = cc_ref[_qrows(i), :]

            def step(kc, state, dchunk):
                m, l, acc = state
                s = _dot(q, k_ref[_krows(kc), :], NT) + ct - cr_ref[kc]
                if dchunk is not None:
                    s = jnp.where(_diag_mask(dchunk, True), s, NEG_BIG)
                m_new = jnp.maximum(m, jnp.max(s, axis=1, keepdims=True))
                alpha = jnp.exp(m - m_new)
                p = jnp.exp(s - m_new)
                if dchunk is not None:
                    p = jnp.where(_diag_mask(dchunk, True), p, 0.0)
                l = l * alpha + jnp.sum(p, axis=1, keepdims=True)
                acc = acc * alpha + _dot(p, v_ref[_krows(kc), :], NN)
                return m_new, l, acc

            init = (jnp.full((BQ, 1), NEG_BIG, F32), jnp.zeros((BQ, 1), F32), jnp.zeros((BQ, HEAD_DIM), F32))
            m, l, acc = _walk_chunks(i, step, init, False)
            o_ref[_qrows(i), :] = (acc / l).astype(BF16)
            lse_ref[_qrows(i), :] = m + jnp.log(l)
            return 0

        lax.fori_loop(0, t // BQ, qblock, 0)

    return pl.pallas_call(
        body, name=name, grid=(N_HEADS,),
        in_specs=[_head_spec(t, 3 * N_HEADS), _head_spec(t, 4 * N_HEADS), _head_spec(t, 5 * N_HEADS),
                  _col_spec(t), _row_spec(t // CS)],
        out_specs=[_head_spec(t, 0), _col_spec(t)],
        out_shape=[jax.ShapeDtypeStruct((N_HEADS, t, HEAD_DIM), BF16), jax.ShapeDtypeStruct((N_HEADS, t, 1), F32)],
        compiler_params=_params(("parallel",)),
    )(qkv, qkv, qkv, c_col, c_row)


def fox_bwd(name, qkv, qt, y, dy, dyt, lse, c_col, c_row):
    t = qkv.shape[1]
    nq, nc = t // BQ, t // CS

    def body(q_ref, k_ref, v_ref, qt_ref, o_ref, do_ref, dot_ref, lse_ref, cc_ref, cr_ref,
             dq_ref, dkt_ref, dvt_ref, dcc_ref, dcr_ref, dkt_acc, dvt_acc, dcr_acc):
        dkt_acc[...] = jnp.zeros_like(dkt_acc)
        dvt_acc[...] = jnp.zeros_like(dvt_acc)
        dcr_acc[...] = jnp.zeros_like(dcr_acc)

        def qblock(i, _):
            q = q_ref[_qrows(i), :]
            do = do_ref[_qrows(i), :]
            q_t = qt_ref[i]
            do_t = dot_ref[i]
            ct = cc_ref[_qrows(i), :]
            lse_i = lse_ref[_qrows(i), :]
            delta = jnp.sum(do.astype(F32) * o_ref[_qrows(i), :].astype(F32), axis=1, keepdims=True)

            def step(kc, state, dchunk):
                dq, dct = state
                s = _dot(q, k_ref[_krows(kc), :], NT) + ct - cr_ref[kc]
                p = jnp.exp(s - lse_i)
                if dchunk is not None:
                    p = jnp.where(_diag_mask(dchunk, True), p, 0.0)
                ds = p * (_dot(do, v_ref[_krows(kc), :], NT) - delta)
                dvt_acc[kc] += _dot(do_t, p, NN)
                dsb = ds.astype(BF16)
                dq = dq + _dot(dsb, k_ref[_krows(kc), :], NN)
                dkt_acc[kc] += _dot(q_t, dsb, NN)
                dcr_acc[kc] -= jnp.sum(ds, axis=0, keepdims=True)
                return dq, dct + jnp.sum(ds, axis=1, keepdims=True)

            dq, dct = _walk_chunks(i, step, (jnp.zeros((BQ, HEAD_DIM), F32), jnp.zeros((BQ, 1), F32)), False)
            dq_ref[_qrows(i), :] = (dq * ATTN_SCALE).astype(BF16)
            dcc_ref[_qrows(i), :] = dct
            return 0

        lax.fori_loop(0, nq, qblock, 0)
        dkt_ref[...] = dkt_acc[...].astype(BF16)
        dvt_ref[...] = dvt_acc[...].astype(BF16)
        dcr_ref[...] = dcr_acc[...]

    chunked = jax.ShapeDtypeStruct((N_HEADS, nc, HEAD_DIM, CS), BF16)
    return pl.pallas_call(
        body, name=name, grid=(N_HEADS,),
        in_specs=[_head_spec(t, 3 * N_HEADS), _head_spec(t, 4 * N_HEADS), _head_spec(t, 5 * N_HEADS), _head_t_spec(nq, N_HEADS),
                  _head_spec(t, 0), _head_spec(t, 0), _head_t_spec(nq, 0), _col_spec(t), _col_spec(t), _row_spec(nc)],
        out_specs=[_head_spec(t, 0), _chunk_t_spec(nc), _chunk_t_spec(nc), _col_spec(t), _row_spec(nc)],
        out_shape=[jax.ShapeDtypeStruct((N_HEADS, t, HEAD_DIM), BF16), chunked, chunked,
                   jax.ShapeDtypeStruct((N_HEADS, t, 1), F32), jax.ShapeDtypeStruct((N_HEADS, nc, 1, CS), F32)],
        scratch_shapes=[pltpu.VMEM((nc, HEAD_DIM, CS), F32), pltpu.VMEM((nc, HEAD_DIM, CS), F32), pltpu.VMEM((nc, 1, CS), F32)],
        compiler_params=_params(("parallel",)),
    )(qkv, qkv, qkv, qt, y, dy, dyt, lse, c_col, c_row)


PAIR = 2 * HEAD_DIM
N_PAIRS = N_HEADS // 2


def _pair_spec(t, first_block):
    return pl.BlockSpec((t, PAIR), lambda p, *_: (0, first_block + p))


def _head_lanes(shape):
    lane = _iota2(shape, len(shape) - 1)
    return [lane < HEAD_DIM, lane >= HEAD_DIM]


def _only_head(x, lanes_of_head):
    return jnp.where(lanes_of_head, x, jnp.zeros_like(x))


def _sb_chunk_weights(q_h, k, later, carry, dchunk):
    z = _dot(q_h, k, NT)
    sp = _log1pexp_neg_abs(z)
    lnb = -jnp.maximum(z, 0.0) - sp
    lsz = jnp.minimum(z, 0.0) - sp
    if dchunk is not None:
        lnb = jnp.where(_diag_mask(dchunk, False), lnb, 0.0)
    w = jnp.exp(lsz + _dot_split2_rhs01(lnb, later) + carry)
    if dchunk is not None:
        w = jnp.where(_diag_mask(dchunk, False), w, 0.0)
    return w, lsz, lnb


def sb_pair_fwd(name, qkv):
    t = qkv.shape[0]

    def body(q_ref, k_ref, v_ref, o_ref):
        later = (_iota2((CS, CS), 0) > _iota2((CS, CS), 1)).astype(BF16)
        lanes = _head_lanes((BQ, PAIR))

        def qblock(i, _):
            q = q_ref[_qrows(i), :]
            q_heads = [_only_head(q, lanes[h]) for h in range(2)]

            def step(kc, state, dchunk):
                k = k_ref[_krows(kc), :]
                v = v_ref[_krows(kc), :]
                out = []
                for h in range(2):
                    carry, acc = state[h]
                    w, _, lnb = _sb_chunk_weights(q_heads[h], k, later, carry, dchunk)
                    out.append((carry + jnp.sum(lnb, axis=1, keepdims=True), acc + _dot(w, v, NN)))
                return tuple(out)

            zero = (jnp.zeros((BQ, 1), F32), jnp.zeros((BQ, PAIR), F32))
            (_, acc0), (_, acc1) = _walk_chunks(i, step, (zero, zero), True)
            o_ref[_qrows(i), :] = jnp.where(lanes[0], acc0, acc1).astype(BF16)
            return 0

        lax.fori_loop(0, t // BQ, qblock, 0)

    return pl.pallas_call(
        body, name=name, grid=(N_PAIRS,),
        in_specs=[_pair_spec(t, 0), _pair_spec(t, N_PAIRS), _pair_spec(t, 2 * N_PAIRS)],
        out_specs=_pair_spec(t, 0), out_shape=jax.ShapeDtypeStruct((t, WIDTH), BF16),
        compiler_params=_params(("parallel",)),
    )(qkv, qkv, qkv)


def _emit_dqkv(res, o_ref):
    o_ref[...] = res[pl.program_id(1)]


def _flush_transposed(acc, res, which):
    for kc in range(acc.shape[0]):
        res[which, kc * CS:(kc + 1) * CS, :] = acc[kc].T.astype(BF16)


def sb_pair_bwd(name, qkv, dy, dqkv):
    t = qkv.shape[0]
    nc = t // CS

    def body(q_ref, k_ref, v_ref, do_ref, _, o_ref, g_s, b_s, dkt_acc, dvt_acc, res):
        @pl.when(pl.program_id(1) == 0)
        def _():
            later = (_iota2((CS, CS), 0) > _iota2((CS, CS), 1)).astype(BF16)
            earlier = (_iota2((CS, CS), 0) < _iota2((CS, CS), 1)).astype(BF16)
            lanes = _head_lanes((BQ, PAIR))
            dkt_acc[...] = jnp.zeros_like(dkt_acc)
            dvt_acc[...] = jnp.zeros_like(dvt_acc)

            def qblock(i, _):
                q = q_ref[_qrows(i), :]
                do = do_ref[_qrows(i), :]
                q_heads = [_only_head(q, lanes[h]) for h in range(2)]
                do_heads = [_only_head(do, lanes[h]) for h in range(2)]
                qt_heads = [qh.astype(F32).T.astype(BF16) for qh in q_heads]
                dot_heads = [dh.astype(F32).T.astype(BF16) for dh in do_heads]

                def step1(kc, carries, dchunk):
                    k = k_ref[_krows(kc), :]
                    v = v_ref[_krows(kc), :]
                    out = []
                    for h in range(2):
                        w, lsz, lnb = _sb_chunk_weights(q_heads[h], k, later, carries[h], dchunk)
                        g_s[h, kc] = (w * _dot(do_heads[h], v, NT)).astype(BF16)
                        b_s[h, kc] = jnp.exp(lsz).astype(BF16)
                        dvt_acc[kc] += _dot(dot_heads[h], w, NN)
                        out.append(carries[h] + jnp.sum(lnb, axis=1, keepdims=True))
                    return tuple(out)

                zero = jnp.zeros((BQ, 1), F32)
                _walk_chunks(i, step1, (zero, zero), True)

                def step2(kc, state, dchunk):
                    k = k_ref[_krows(kc), :]
                    out = []
                    for h in range(2):
                        before, dq = state[h]
                        g16 = g_s[h, kc]
                        g = g16.astype(F32)
                        beta = b_s[h, kc].astype(F32)
                        prefix = lax.dot_general(g16, earlier, (NN, ((), ())), preferred_element_type=F32) + before
                        dz = g * (1.0 - beta) - beta * prefix
                        if dchunk is not None:
                            dz = jnp.where(_diag_mask(dchunk, False), dz, 0.0)
                        dzb = dz.astype(BF16)
                        dkt_acc[kc] += _dot(qt_heads[h], dzb, NN)
                        out.append((before + jnp.sum(g, axis=1, keepdims=True), dq + _dot(dzb, k, NN)))
                    return tuple(out)

                start = (zero, jnp.zeros((BQ, PAIR), F32))
                (_, dq0), (_, dq1) = _walk_chunks(i, step2, (start, start), False)
                res[0, _qrows(i), :] = (jnp.where(lanes[0], dq0, dq1) * ATTN_SCALE).astype(BF16)
                return 0

            lax.fori_loop(0, t // BQ, qblock, 0)
            _flush_transposed(dkt_acc, res, 1)
            _flush_transposed(dvt_acc, res, 2)

        _emit_dqkv(res, o_ref)

    return pl.pallas_call(
        body, name=name, grid=(N_PAIRS, 3),
        in_specs=[_pair_spec(t, 0), _pair_spec(t, N_PAIRS), _pair_spec(t, 2 * N_PAIRS), _pair_spec(t, 0), _ANY],
        out_specs=pl.BlockSpec((t, PAIR), lambda p, s: (0, s * N_PAIRS + p)),
        out_shape=jax.ShapeDtypeStruct(dqkv.shape, BF16), input_output_aliases={4: 0},
        scratch_shapes=[pltpu.VMEM((2, nc, BQ, CS), BF16), pltpu.VMEM((2, nc, BQ, CS), BF16),
                        pltpu.VMEM((nc, PAIR, CS), F32), pltpu.VMEM((nc, PAIR, CS), F32), pltpu.VMEM((3, t, PAIR), BF16)],
        compiler_params=_params(("parallel", "arbitrary")),
    )(qkv, qkv, qkv, dy, dqkv)


def _gates_col_spec(t):
    return pl.BlockSpec((2, t, 1), lambda p, *_: (p, 0, 0))


def _gates_row_spec(nc):
    return pl.BlockSpec((2, nc, 1, CS), lambda p, *_: (p, 0, 0, 0))


def fox_pair_fwd(name, qkv, c_col, c_row):
    t = qkv.shape[0]

    def body(q_ref, k_ref, v_ref, cc_ref, cr_ref, o_ref, lse_ref):
        lanes = _head_lanes((BQ, PAIR))

        def qblock(i, _):
            q = q_ref[_qrows(i), :]
            q_heads = [_only_head(q, lanes[h]) for h in range(2)]
            ct = [cc_ref[h, _qrows(i), :] for h in range(2)]

            def step(kc, state, dchunk):
                k = k_ref[_krows(kc), :]
                v = v_ref[_krows(kc), :]
                out = []
                for h in range(2):
                    m, l, acc = state[h]
                    s = _dot(q_heads[h], k, NT) + ct[h] - cr_ref[h, kc]
                    if dchunk is not None:
                        s = jnp.where(_diag_mask(dchunk, True), s, NEG_BIG)
                    m_new = jnp.maximum(m, jnp.max(s, axis=1, keepdims=True))
                    alpha = jnp.exp(m - m_new)
                    p = jnp.exp(s - m_new)
                    if dchunk is not None:
                        p = jnp.where(_diag_mask(dchunk, True), p, 0.0)
                    out.append((m_new, l * alpha + jnp.sum(p, axis=1, keepdims=True), acc * alpha + _dot(p, v, NN)))
                return tuple(out)

            init = (jnp.full((BQ, 1), NEG_BIG, F32), jnp.zeros((BQ, 1), F32), jnp.zeros((BQ, PAIR), F32))
            (m0, l0, acc0), (m1, l1, acc1) = _walk_chunks(i, step, (init, init), False)
            o_ref[_qrows(i), :] = jnp.where(lanes[0], acc0 / l0, acc1 / l1).astype(BF16)
            lse_ref[0, _qrows(i), :] = m0 + jnp.log(l0)
            lse_ref[1, _qrows(i), :] = m1 + jnp.log(l1)
            return 0

        lax.fori_loop(0, t // BQ, qblock, 0)

    return pl.pallas_call(
        body, name=name, grid=(N_PAIRS,),
        in_specs=[_pair_spec(t, 3 * N_PAIRS), _pair_spec(t, 4 * N_PAIRS), _pair_spec(t, 5 * N_PAIRS),
                  _gates_col_spec(t), _gates_row_spec(t // CS)],
        out_specs=[_pair_spec(t, 0), _gates_col_spec(t)],
        out_shape=[jax.ShapeDtypeStruct((t, WIDTH), BF16), jax.ShapeDtypeStruct((N_HEADS, t, 1), F32)],
        compiler_params=_params(("parallel",)),
    )(qkv, qkv, qkv, c_col, c_row)


def fox_pair_bwd(name, qkv, y, dy, lse, c_col, c_row, dqkv):
    t = qkv.shape[0]
    nc = t // CS

    def body(q_ref, k_ref, v_ref, o_in_ref, do_ref, lse_ref, cc_ref, cr_ref, _, o_ref, dcc_ref, dcr_ref,
             dkt_acc, dvt_acc, dcr_acc, res):
        @pl.when(pl.program_id(1) == 0)
        def _():
            lanes = _head_lanes((BQ, PAIR))
            dkt_acc[...] = jnp.zeros_like(dkt_acc)
            dvt_acc[...] = jnp.zeros_like(dvt_acc)
            dcr_acc[...] = jnp.zeros_like(dcr_acc)

            def qblock(i, _):
                q = q_ref[_qrows(i), :]
                do = do_ref[_qrows(i), :]
                q_heads = [_only_head(q, lanes[h]) for h in range(2)]
                do_heads = [_only_head(do, lanes[h]) for h in range(2)]
                qt_heads = [qh.astype(F32).T.astype(BF16) for qh in q_heads]
                dot_heads = [dh.astype(F32).T.astype(BF16) for dh in do_heads]
                prod = do.astype(F32) * o_in_ref[_qrows(i), :].astype(F32)
                delta = [jnp.sum(_only_head(prod, lanes[h]), axis=1, keepdims=True) for h in range(2)]
                ct = [cc_ref[h, _qrows(i), :] for h in range(2)]
                lse_i = [lse_ref[h, _qrows(i), :] for h in range(2)]

                def step(kc, state, dchunk):
                    k = k_ref[_krows(kc), :]
                    v = v_ref[_krows(kc), :]
                    out = []
                    for h in range(2):
                        dq, dct = state[h]
                        s = _dot(q_heads[h], k, NT) + ct[h] - cr_ref[h, kc]
                        p = jnp.exp(s - lse_i[h])
                        if dchunk is not None:
                            p = jnp.where(_diag_mask(dchunk, True), p, 0.0)
                        ds = p * (_dot(do_heads[h], v, NT) - delta[h])
                        dvt_acc[kc] += _dot(dot_heads[h], p, NN)
                        dsb = ds.astype(BF16)
                        dkt_acc[kc] += _dot(qt_heads[h], dsb, NN)
                        dcr_acc[h, kc] -= jnp.sum(ds, axis=0, keepdims=True)
                        out.append((dq + _dot(dsb, k, NN), dct + jnp.sum(ds, axis=1, keepdims=True)))
                    return tuple(out)

                zero = (jnp.zeros((BQ, PAIR), F32), jnp.zeros((BQ, 1), F32))
                (dq0, dct0), (dq1, dct1) = _walk_chunks(i, step, (zero, zero), False)
                res[0, _qrows(i), :] = (jnp.where(lanes[0], dq0, dq1) * ATTN_SCALE).astype(BF16)
                dcc_ref[0, _qrows(i), :] = dct0
                dcc_ref[1, _qrows(i), :] = dct1
                return 0

            lax.fori_loop(0, t // BQ, qblock, 0)
            _flush_transposed(dkt_acc, res, 1)
            _flush_transposed(dvt_acc, res, 2)
            dcr_ref[...] = dcr_acc[...]

        _emit_dqkv(res, o_ref)

    return pl.pallas_call(
        body, name=name, grid=(N_PAIRS, 3),
        in_specs=[_pair_spec(t, 3 * N_PAIRS), _pair_spec(t, 4 * N_PAIRS), _pair_spec(t, 5 * N_PAIRS), _pair_spec(t, 0),
                  _pair_spec(t, 0), _gates_col_spec(t), _gates_col_spec(t), _gates_row_spec(nc), _ANY],
        out_specs=[pl.BlockSpec((t, PAIR), lambda p, s: (0, (3 + s) * N_PAIRS + p)), _gates_col_spec(t), _gates_row_spec(nc)],
        out_shape=[jax.ShapeDtypeStruct(dqkv.shape, BF16), jax.ShapeDtypeStruct((N_HEADS, t, 1), F32),
                   jax.ShapeDtypeStruct((N_HEADS, nc, 1, CS), F32)],
        input_output_aliases={8: 0},
        scratch_shapes=[pltpu.VMEM((nc, PAIR, CS), F32), pltpu.VMEM((nc, PAIR, CS), F32), pltpu.VMEM((2, nc, 1, CS), F32),
                        pltpu.VMEM((3, t, PAIR), BF16)],
        compiler_params=_params(("parallel", "arbitrary")),
    )(qkv, qkv, qkv, y, dy, lse, c_col, c_row, dqkv)


def loss_head(name, x, g, target):
    t, d = x.shape
    tr = _row_tile(t, 256)

    def body(x_ref, g_ref, t_ref, dx_ref, gn_ref, loss_ref):
        xv = x_ref[...]
        r = lax.rsqrt(jnp.mean(xv * xv, axis=-1, keepdims=True) + RMS_EPS)
        xhat = xv * r
        gv = g_ref[...]
        err = xhat * gv - t_ref[...]
        part_loss = 0.5 * jnp.sum(jnp.mean(err * err, axis=-1, keepdims=True), axis=0, keepdims=True)
        dy = err * (1.0 / d)
        dyg = dy * gv
        dx_ref[...] = r * (dyg - xhat * jnp.mean(dyg * xhat, axis=-1, keepdims=True))
        part_g = jnp.sum(dy * xhat, axis=0, keepdims=True)

        @pl.when(pl.program_id(0) == 0)
        def _():
            gn_ref[...] = part_g
            loss_ref[...] = part_loss

        @pl.when(pl.program_id(0) != 0)
        def _():
            gn_ref[...] += part_g
            loss_ref[...] += part_loss

    row = pl.BlockSpec((tr, d), lambda i: (i, 0))
    return pl.pallas_call(
        body, name=name, grid=(t // tr,),
        in_specs=[row, pl.BlockSpec((1, d), lambda i: (0, 0)), row],
        out_specs=[row, pl.BlockSpec((1, d), lambda i: (0, 0)), pl.BlockSpec((1, 1), lambda i: (0, 0))],
        out_shape=[jax.ShapeDtypeStruct((t, d), F32), jax.ShapeDtypeStruct((1, d), F32), jax.ShapeDtypeStruct((1, 1), F32)],
        compiler_params=_params(("arbitrary",)),
    )(x, g, target)


def _place():
    return lax.axis_index("x"), lax.axis_index("y"), lax.axis_index("c")


def _other_chips(x, y):
    return [(1 - x, y), (x, 1 - y), (1 - x, 1 - y)]


def _half(ref, c, rows):
    return ref.at[:, pl.ds(c * (rows // 2), rows // 2), :]


_ANY = pl.BlockSpec(memory_space=pl.ANY)


def gather_weights(name, bufs):
    n = len(bufs)

    def body(*refs):
        outs = refs[n:2 * n]
        send_sems, recv_sems = refs[2 * n:]
        x, y, c = _place()
        chips = _other_chips(x, y)
        me = 2 * x + y
        sibling = (x, y, 1 - c)
        first, passed = [], []
        for i in range(n):
            rows = outs[i].shape[2]
            mine = _half(outs[i].at[me], c, rows)
            for j, (qx, qy) in enumerate(chips):
                k = 6 * i + j
                rc = pltpu.make_async_remote_copy(
                    src_ref=mine, dst_ref=mine,
                    send_sem=send_sems.at[k], recv_sem=recv_sems.at[k], device_id=(qx, qy, c), device_id_type=MESH)
                rc.start()
                first.append(rc)
        for i in range(n):
            rows = outs[i].shape[2]
            for j, (qx, qy) in enumerate(chips):
                k = 6 * i + j
                block = _half(outs[i].at[2 * qx + qy], c, rows)
                pltpu.make_async_remote_copy(
                    src_ref=block, dst_ref=block, send_sem=send_sems.at[k], recv_sem=recv_sems.at[k],
                    device_id=(qx, qy, c), device_id_type=MESH).wait_recv()
                fw = pltpu.make_async_remote_copy(
                    src_ref=block, dst_ref=block, send_sem=send_sems.at[k + 3], recv_sem=recv_sems.at[k + 3],
                    device_id=sibling, device_id_type=MESH)
                fw.start()
                passed.append(fw)
        for i in range(n):
            rows = outs[i].shape[2]
            for j, (qx, qy) in enumerate(chips):
                k = 6 * i + j + 3
                block = _half(outs[i].at[2 * qx + qy], 1 - c, rows)
                pltpu.make_async_remote_copy(
                    src_ref=block, dst_ref=block, send_sem=send_sems.at[k], recv_sem=recv_sems.at[k],
                    device_id=sibling, device_id_type=MESH).wait_recv()
        for cp in first + passed:
            cp.wait_send()

    return pl.pallas_call(
        body, name=name, in_specs=[_ANY] * n, out_specs=[_ANY] * n,
        out_shape=[jax.ShapeDtypeStruct(b.shape, b.dtype) for b in bufs],
        input_output_aliases={i: i for i in range(n)},
        scratch_shapes=[pltpu.SemaphoreType.DMA((6 * n,)), pltpu.SemaphoreType.DMA((6 * n,))],
        compiler_params=pltpu.CompilerParams(has_side_effects=True),
    )(*bufs)


_HBM = pl.BlockSpec(memory_space=pltpu.HBM)
_SEM = pl.BlockSpec(memory_space=pltpu.SEMAPHORE)
_DATAFLOW = pltpu.SideEffectType.DATAFLOW_SIDE_EFFECTING


def _in_hbm(a):
    return pltpu.with_memory_space_constraint(a, pltpu.HBM)


def _gather_ici_copies(bufs, send_sems, recv_sems, arrivals):
    x, y, c = _place()
    me = 2 * x + y
    copies = []
    for i, buf in enumerate(bufs):
        rows = buf.shape[2]
        for j, (qx, qy) in enumerate(_other_chips(x, y)):
            block = _half(buf.at[2 * qx + qy if arrivals else me], c, rows)
            copies.append(pltpu.make_async_remote_copy(
                src_ref=block, dst_ref=block, send_sem=send_sems.at[3 * i + j], recv_sem=recv_sems.at[3 * i + j],
                device_id=(qx, qy, c), device_id_type=MESH))
    return copies


def gather_ici_start(name, bufs, after):
    n = len(bufs)

    def body(*refs):
        ins = refs[:n]
        send_sems, recv_sems = refs[n + 1], refs[n + 2]
        token = refs[-1]
        for send in _gather_ici_copies(ins, send_sems, recv_sems, False):
            send.start()
        token[...] = jnp.zeros_like(token)

    res = pl.pallas_call(
        body, name=name,
        out_shape=(pltpu.SemaphoreType.DMA((3 * n,)), pltpu.SemaphoreType.DMA((3 * n,)), *[pltpu.HBM(b.shape, b.dtype) for b in bufs],
                   jax.ShapeDtypeStruct((8, 128), F32)),
        in_specs=[_HBM] * n + [_ANY], out_specs=(_SEM, _SEM, *[_HBM] * n, pl.BlockSpec(memory_space=pltpu.VMEM)),
        input_output_aliases={i: 2 + i for i in range(n)},
        compiler_params=pltpu.CompilerParams(has_side_effects=_DATAFLOW),
    )(*[_in_hbm(b) for b in bufs], after)
    return res[0], res[1], list(res[2:2 + n]), res[-1]


def gather_ici_wait(name, send_sems, recv_sems, bufs, after):
    n = len(bufs)

    def body(*refs):
        ins = refs[:n]
        send_sems_ref, recv_sems_ref = refs[n], refs[n + 1]
        for send in _gather_ici_copies(ins, send_sems_ref, recv_sems_ref, False):
            send.wait_send()
        for recv in _gather_ici_copies(ins, send_sems_ref, recv_sems_ref, True):
            recv.wait_recv()

    return pl.pallas_call(
        body, name=name, out_shape=tuple(pltpu.HBM(b.shape, b.dtype) for b in bufs),
        in_specs=[_HBM] * n + [_SEM, _SEM, _ANY], out_specs=tuple([_HBM] * n),
        input_output_aliases={i: i for i in range(n)},
        compiler_params=pltpu.CompilerParams(has_side_effects=_DATAFLOW),
    )(*bufs, send_sems, recv_sems, after)


def gather_forward(name, bufs):
    n = len(bufs)

    def body(*refs):
        outs = refs[n:2 * n]
        send_sems, recv_sems = refs[2 * n:]
        x, y, c = _place()
        sibling = (x, y, 1 - c)
        sends = []
        for i in range(n):
            rows = outs[i].shape[2]
            for j, (qx, qy) in enumerate(_other_chips(x, y)):
                block = _half(outs[i].at[2 * qx + qy], c, rows)
                fw = pltpu.make_async_remote_copy(
                    src_ref=block, dst_ref=block, send_sem=send_sems.at[3 * i + j], recv_sem=recv_sems.at[3 * i + j],
                    device_id=sibling, device_id_type=MESH)
                fw.start()
                sends.append(fw)
        for i in range(n):
            rows = outs[i].shape[2]
            for j, (qx, qy) in enumerate(_other_chips(x, y)):
                block = _half(outs[i].at[2 * qx + qy], 1 - c, rows)
                pltpu.make_async_remote_copy(
                    src_ref=block, dst_ref=block, send_sem=send_sems.at[3 * i + j], recv_sem=recv_sems.at[3 * i + j],
                    device_id=sibling, device_id_type=MESH).wait_recv()
        for fw in sends:
            fw.wait_send()

    return pl.pallas_call(
        body, name=name, in_specs=[_ANY] * n, out_specs=[_ANY] * n,
        out_shape=[jax.ShapeDtypeStruct(b.shape, b.dtype) for b in bufs],
        input_output_aliases={i: i for i in range(n)},
        scratch_shapes=[pltpu.SemaphoreType.DMA((3 * n,)), pltpu.SemaphoreType.DMA((3 * n,))],
        compiler_params=pltpu.CompilerParams(has_side_effects=True),
    )(*bufs)


def _between_chips_copies(parts, lands, send_sems, recv_sems):
    x, y, c = _place()
    copies = []
    for i, (part, land) in enumerate(zip(parts, lands)):
        for j, (qx, qy) in enumerate(_other_chips(x, y)):
            copies.append(pltpu.make_async_remote_copy(
                src_ref=part.at[2 * qx + qy], dst_ref=land.at[j], send_sem=send_sems.at[3 * i + j], recv_sem=recv_sems.at[3 * i + j],
                device_id=(qx, qy, c), device_id_type=MESH))
    return copies


def between_chips_start(name, parts):
    n = len(parts)
    lands = [lax.empty((N_CHIPS - 1,) + p.shape[1:], p.dtype) for p in parts]

    def body(*refs):
        send_sems, recv_sems = refs[2 * n], refs[2 * n + 1]
        token = refs[-1]
        for cp in _between_chips_copies(refs[:n], refs[n:2 * n], send_sems, recv_sems):
            cp.start()
        token[...] = jnp.zeros_like(token)

    res = pl.pallas_call(
        body, name=name,
        out_shape=(pltpu.SemaphoreType.DMA((3 * n,)), pltpu.SemaphoreType.DMA((3 * n,)),
                   *[pltpu.HBM(a.shape, a.dtype) for a in parts + lands], jax.ShapeDtypeStruct((8, 128), F32)),
        in_specs=[_HBM] * (2 * n), out_specs=(_SEM, _SEM, *[_HBM] * (2 * n), pl.BlockSpec(memory_space=pltpu.VMEM)),
        input_output_aliases={i: 2 + i for i in range(2 * n)},
        compiler_params=pltpu.CompilerParams(has_side_effects=_DATAFLOW),
    )(*[_in_hbm(a) for a in parts + lands])
    return res[0], res[1], list(res[2:2 + n]), list(res[2 + n:2 + 2 * n]), res[-1]


def between_chips_wait(name, send_sems, recv_sems, parts, lands, after):
    n = len(parts)

    def body(*refs):
        for cp in _between_chips_copies(refs[:n], refs[n:2 * n], refs[2 * n], refs[2 * n + 1]):
            cp.wait_send()
            cp.wait_recv()

    res = pl.pallas_call(
        body, name=name, out_shape=tuple(pltpu.HBM(a.shape, a.dtype) for a in parts + lands),
        in_specs=[_HBM] * (2 * n) + [_SEM, _SEM, _ANY], out_specs=tuple([_HBM] * (2 * n)),
        input_output_aliases={i: i for i in range(2 * n)},
        compiler_params=pltpu.CompilerParams(has_side_effects=_DATAFLOW),
    )(*parts, *lands, send_sems, recv_sems, after)
    return list(res[:n]), list(res[n:])


def send_half_to_sibling(name, grads):
    n = len(grads)

    def body(*refs):
        srcs, outs = refs[:n], refs[n:2 * n]
        send_sems, recv_sems = refs[2 * n:]
        x, y, c = _place()
        sibling = (x, y, 1 - c)
        copies = []
        for i in range(n):
            rows = srcs[i].shape[2]
            rc = pltpu.make_async_remote_copy(
                src_ref=srcs[i].at[:, :, pl.ds((1 - c) * (rows // 2), rows // 2), :], dst_ref=outs[i],
                send_sem=send_sems.at[i], recv_sem=recv_sems.at[i], device_id=sibling, device_id_type=MESH)
            rc.start()
            copies.append(rc)
        for rc in copies:
            rc.wait()

    def half_shape(g):
        s = g.shape
        return jax.ShapeDtypeStruct((s[0], s[1], s[2] // 2, s[3]), g.dtype)

    return pl.pallas_call(
        body, name=name, in_specs=[_ANY] * n, out_specs=[_ANY] * n, out_shape=[half_shape(g) for g in grads],
        scratch_shapes=[pltpu.SemaphoreType.DMA((n,)), pltpu.SemaphoreType.DMA((n,))],
        compiler_params=pltpu.CompilerParams(has_side_effects=True),
    )(*grads)


def exchange_between_chips(name, parts):
    n = len(parts)

    def body(*refs):
        srcs, outs = refs[:n], refs[n:2 * n]
        send_sems, recv_sems = refs[2 * n:]
        x, y, c = _place()
        chips = _other_chips(x, y)
        copies = []
        for i in range(n):
            for j, (qx, qy) in enumerate(chips):
                k = 3 * i + j
                rc = pltpu.make_async_remote_copy(
                    src_ref=srcs[i].at[2 * qx + qy], dst_ref=outs[i].at[j],
                    send_sem=send_sems.at[k], recv_sem=recv_sems.at[k], device_id=(qx, qy, c), device_id_type=MESH)
                rc.start()
                copies.append(rc)
        for rc in copies:
            rc.wait()

    return pl.pallas_call(
        body, name=name, in_specs=[_ANY] * n, out_specs=[_ANY] * n,
        out_shape=[jax.ShapeDtypeStruct((N_CHIPS - 1,) + p.shape[1:], p.dtype) for p in parts],
        scratch_shapes=[pltpu.SemaphoreType.DMA((3 * n,)), pltpu.SemaphoreType.DMA((3 * n,))],
        compiler_params=pltpu.CompilerParams(has_side_effects=True),
    )(*parts)


def share_halves(name, bufs):
    n = len(bufs)

    def body(*refs):
        outs = refs[n:2 * n]
        send_sems, recv_sems = refs[2 * n:]
        x, y, c = _place()
        copies = []
        for i in range(n):
            mine = _half(outs[i], c, outs[i].shape[1])
            rc = pltpu.make_async_remote_copy(
                src_ref=mine, dst_ref=mine, send_sem=send_sems.at[i], recv_sem=recv_sems.at[i],
                device_id=(x, y, 1 - c), device_id_type=MESH)
            rc.start()
            copies.append(rc)
        for i in range(n):
            theirs = _half(outs[i], 1 - c, outs[i].shape[1])
            pltpu.make_async_remote_copy(
                src_ref=theirs, dst_ref=theirs, send_sem=send_sems.at[i], recv_sem=recv_sems.at[i],
                device_id=(x, y, 1 - c), device_id_type=MESH).wait_recv()
        for rc in copies:
            rc.wait_send()

    return pl.pallas_call(
        body, name=name, in_specs=[_ANY] * n, out_specs=[_ANY] * n,
        out_shape=[jax.ShapeDtypeStruct(b.shape, b.dtype) for b in bufs],
        input_output_aliases={i: i for i in range(n)},
        scratch_shapes=[pltpu.SemaphoreType.DMA((n,)), pltpu.SemaphoreType.DMA((n,))],
        compiler_params=pltpu.CompilerParams(has_side_effects=True),
    )(*bufs)


def pair_sum(name, grad, recv, c):
    ns, na, rh, cols = recv.shape
    tr = _row_tile(rh, 256) if rh % 256 == 0 else rh
    nt = rh // tr

    def body(c_ref, g_ref, r_ref, o_ref):
        o_ref[...] = (g_ref[...].astype(F32) + r_ref[...].astype(F32)).astype(BF16)

    blk = (None, None, tr, cols)
    return pl.pallas_call(
        body, name=name,
        grid_spec=pltpu.PrefetchScalarGridSpec(
            num_scalar_prefetch=1, grid=(ns, na, nt),
            in_specs=[pl.BlockSpec(blk, lambda s, a, r, c_ref: (s, a, c_ref[0] * nt + r, 0)),
                      pl.BlockSpec(blk, lambda s, a, r, c_ref: (s, a, r, 0))],
            out_specs=pl.BlockSpec(blk, lambda s, a, r, c_ref: (s, a, r, 0))),
        out_shape=jax.ShapeDtypeStruct(recv.shape, BF16),
        compiler_params=_params(("parallel", "parallel", "parallel")),
    )(c, grad, recv)


def chip_sum(name, parts, landed, place):
    _, na, rh, cols = parts.shape
    tr = _row_tile(rh, 256) if rh % 256 == 0 else rh
    nt = rh // tr

    def body(place_ref, p_ref, l_ref, o_ref):
        total = p_ref[...].astype(F32)
        for s in range(N_CHIPS - 1):
            total = total + l_ref[s].astype(F32)
        o_ref[...] = total

    return pl.pallas_call(
        body, name=name,
        grid_spec=pltpu.PrefetchScalarGridSpec(
            num_scalar_prefetch=1, grid=(na, nt),
            in_specs=[pl.BlockSpec((None, None, tr, cols), lambda a, r, pr: (pr[1], a, r, 0)),
                      pl.BlockSpec((N_CHIPS - 1, None, tr, cols), lambda a, r, pr: (0, a, r, 0))],
            out_specs=pl.BlockSpec((None, tr, cols), lambda a, r, pr: (a, pr[0] * nt + r, 0))),
        out_shape=jax.ShapeDtypeStruct((na, 2 * rh, cols), F32),
        compiler_params=_params(("parallel", "parallel")),
    )(place, parts, landed)


def reduce_scatter_begin(tag, grads, place):
    recv = send_half_to_sibling(f"{tag}_to_sibling", grads)
    parts = [pair_sum(f"{tag}_pair_sum{i}", g, r, place) for i, (g, r) in enumerate(zip(grads, recv))]
    send_sems, recv_sems, parts, lands, token = between_chips_start(f"{tag}_between_chips_start", parts)
    return (send_sems, recv_sems, parts, lands), token


def reduce_scatter_end(tag, state, place, after):
    send_sems, recv_sems, parts, lands = state
    parts, landed = between_chips_wait(f"{tag}_between_chips_wait", send_sems, recv_sems, parts, lands, after)
    halves = [chip_sum(f"{tag}_chip_sum{i}", p, l, place) for i, (p, l) in enumerate(zip(parts, landed))]
    return share_halves(f"{tag}_share", halves)


def _adamw_math(w, g, m, v):
    m = ADAM_B1 * m + (1.0 - ADAM_B1) * g
    v = ADAM_B2 * v + (1.0 - ADAM_B2) * (g * g)
    m_hat = m / (1.0 - ADAM_B1 ** ADAM_STEP)
    v_hat = v / (1.0 - ADAM_B2 ** ADAM_STEP)
    delta = -ADAM_LR * (m_hat / (jnp.sqrt(v_hat) + ADAM_EPS) + ADAM_WD * w)
    return delta, m, v


def adamw(name, w, g, m, v, after=()):
    rows, cols = w.shape
    tr = _row_tile(rows, 256) if rows % 256 == 0 else rows // 2

    def body(w_ref, g_ref, m_ref, v_ref, d_ref, mo_ref, vo_ref):
        d_ref[...], mo_ref[...], vo_ref[...] = _adamw_math(w_ref[...], g_ref[...], m_ref[...], v_ref[...])

    blk = pl.BlockSpec((tr, cols), lambda i: (i, 0))
    return pl.pallas_call(
        _ordered_after(body, 4, len(after)), name=name, grid=(rows // tr,), in_specs=[blk] * 4 + [_ANY] * len(after),
        out_specs=[blk] * 3, out_shape=[jax.ShapeDtypeStruct(w.shape, F32)] * 3, compiler_params=_params(("parallel",)),
    )(w, g, m, v, *after)


def adamw_stacked(name, ws, g, ms, vs, after=()):
    n = len(ws)
    rows, cols = ws[0].shape
    tr = next(r for r in (128, 88, 64, 32, 16, 8) if rows % r == 0)

    def body(*refs):
        w_refs, m_refs, v_refs, g_ref = refs[:n], refs[n:2 * n], refs[2 * n:3 * n], refs[3 * n]
        outs = refs[3 * n + 1:]
        for i in range(n):
            outs[i][...], outs[n + i][...], outs[2 * n + i][...] = _adamw_math(
                w_refs[i][...], g_ref[i], m_refs[i][...], v_refs[i][...])

    blk = pl.BlockSpec((tr, cols), lambda r: (r, 0))
    res = pl.pallas_call(
        _ordered_after(body, 3 * n + 1, len(after)), name=name, grid=(rows // tr,),
        in_specs=[blk] * (3 * n) + [pl.BlockSpec((n, tr, cols), lambda r: (0, r, 0))] + [_ANY] * len(after),
        out_specs=[blk] * (3 * n),
        out_shape=[jax.ShapeDtypeStruct((rows, cols), F32)] * (3 * n), compiler_params=_params(("parallel",)),
    )(*ws, *ms, *vs, g, *after)
    return res[:n], res[n:2 * n], res[2 * n:]


def small_allreduce_adamw(name, g_part, w, m, v):
    rows, cols = g_part.shape

    def body(g_ref, w_ref, m_ref, v_ref, sum_ref, d_ref, mo_ref, vo_ref, land, send_sems, recv_sems):
        x, y, c = _place()
        me = 4 * x + 2 * y + c
        land[me] = g_ref[...]
        copies = []
        for r in range(1, 8):
            peer = (x ^ (r >> 2), y ^ ((r >> 1) & 1), c ^ (r & 1))
            rc = pltpu.make_async_remote_copy(
                src_ref=g_ref, dst_ref=land.at[me], send_sem=send_sems.at[r - 1], recv_sem=recv_sems.at[r - 1],
                device_id=peer, device_id_type=MESH)
            rc.start()
            copies.append(rc)
        for rc in copies:
            rc.wait()
        total = land[0]
        for s in range(1, 8):
            total = total + land[s]
        sum_ref[...] = total
        d_ref[...], mo_ref[...], vo_ref[...] = _adamw_math(w_ref[...], total, m_ref[...], v_ref[...])

    vmem = pl.BlockSpec(memory_space=pltpu.VMEM)
    return pl.pallas_call(
        body, name=name, in_specs=[vmem] * 4, out_specs=[vmem] * 4,
        out_shape=[jax.ShapeDtypeStruct((rows, cols), F32)] * 4,
        scratch_shapes=[pltpu.VMEM((8, rows, cols), F32), pltpu.SemaphoreType.DMA((7,)), pltpu.SemaphoreType.DMA((7,))],
        compiler_params=pltpu.CompilerParams(has_side_effects=True),
    )(g_part, w, m, v)


def _heads(a):
    t, w = a.shape
    return a.reshape(t, w // HEAD_DIM, HEAD_DIM).transpose(1, 0, 2)


def _unheads(a):
    n, t, _ = a.shape
    return a.transpose(1, 0, 2).reshape(t, n * HEAD_DIM)


def _cols_from_shards(g):
    ns, r, cols = g.shape
    return g.transpose(1, 0, 2).reshape(r, ns * cols)


def _shards_from_cols(a):
    r, cols = a.shape
    return a.reshape(r, N_CHIPS, cols // N_CHIPS).transpose(1, 0, 2)


def kernel(x, norm_ffn1, w_ffn1_gate, w_ffn1_up, w_ffn1_down, norm_mix, w_in, b_forget, w_gate, b_gate, w_up_a, w_up_b, w_out, norm_ffn2, w_ffn2_gate, w_ffn2_up, w_ffn2_down, norm_final, loss_target, m_norm_ffn1, m_w_ffn1_gate, m_w_ffn1_up, m_w_ffn1_down, m_norm_mix, m_w_in, m_b_forget, m_w_gate, m_b_gate, m_w_up_a, m_w_up_b, m_w_out, m_norm_ffn2, m_w_ffn2_gate, m_w_ffn2_up, m_w_ffn2_down, m_norm_final, v_norm_ffn1, v_w_ffn1_gate, v_w_ffn1_up, v_w_ffn1_down, v_norm_mix, v_w_in, v_b_forget, v_w_gate, v_b_gate, v_w_up_a, v_w_up_b, v_w_out, v_norm_ffn2, v_w_ffn2_gate, v_w_ffn2_up, v_w_ffn2_down, v_norm_final):
    t, d = x.shape[1], x.shape[2]
    in4 = w_in.shape[2]
    gate4 = w_gate.shape[2]
    up4 = w_up_a.shape[2]
    in_cols = N_CHIPS * in4
    n_forget = in_cols - QKV_COLS
    assert w_up_a.shape[1] == WIDTH and d == 2 * WIDTH and n_forget == N_HEADS
    nq = t // BQ
    chip = 2 * lax.axis_index("x") + lax.axis_index("y")
    c_arr = jnp.stack([lax.axis_index("c"), chip]).astype(jnp.int32)
    x2d = x[0]
    target = loss_target[0]

    def slot(shard):
        return lax.dynamic_update_slice(jnp.zeros((N_CHIPS,) + shard.shape, BF16), shard.astype(BF16)[None], (chip, 0, 0, 0))

    def ffn_views(wg, wu, wd):
        return [wg[0].T, wu[0].T, wd[0]]

    ffn1_w, ffn1_m, ffn1_v = (ffn_views(w_ffn1_gate, w_ffn1_up, w_ffn1_down), ffn_views(m_w_ffn1_gate, m_w_ffn1_up, m_w_ffn1_down),
                              ffn_views(v_w_ffn1_gate, v_w_ffn1_up, v_w_ffn1_down))
    ffn2_w, ffn2_m, ffn2_v = (ffn_views(w_ffn2_gate, w_ffn2_up, w_ffn2_down), ffn_views(m_w_ffn2_gate, m_w_ffn2_up, m_w_ffn2_down),
                              ffn_views(v_w_ffn2_gate, v_w_ffn2_up, v_w_ffn2_down))
    in_pad = -(-in4 // 32) * 32
    ig_sh = slot(jnp.concatenate([jnp.pad(w_in[0].T, ((0, in_pad - in4), (0, 0))), w_gate[0].T], axis=0)[None])
    up_sh = slot(jnp.concatenate([w_up_a[0], w_up_b[0]], axis=0)[None])
    wo_sh = slot(w_out)
    f1_send, f1_recv, f1_bufs, f1_token = gather_ici_start("gather_ffn1_start", [slot(jnp.stack(ffn1_w))], norm_ffn1)
    mx_send, mx_recv, mx_bufs, mx_token = gather_ici_start("gather_mixer_start", [ig_sh, up_sh, wo_sh], f1_token)
    f2_send, f2_recv, f2_bufs, f2_token = gather_ici_start("gather_ffn2_start", [slot(jnp.stack(ffn2_w))], mx_token)

    normed1 = rms_fwd("ffn1_rms", x2d, norm_ffn1, after=(f2_token,))
    (w3_1,) = gather_forward("gather_ffn1_forward", gather_ici_wait("gather_ffn1_wait", f1_send, f1_recv, f1_bufs, normed1[0]))
    x1, saved1 = ffn_forward("ffn1", x2d, norm_ffn1, w3_1, normed=normed1)
    ig, wup, wo = gather_forward("gather_mixer_forward", gather_ici_wait("gather_mixer_wait", mx_send, mx_recv, mx_bufs, x1))
    wup = wup[:, 0]
    w_in_t = ig[:, 0, :in4].reshape(in_cols, d)
    wcat = jnp.concatenate([w_in_t[:QKV_COLS], ig[:, 0, in_pad:].reshape(2 * d, d), w_in_t[QKV_COLS:],
                            jnp.zeros((F_PAD - n_forget, d), BF16)], axis=0)
    bias_cat = jnp.concatenate([jnp.zeros((1, QKV_COLS), F32), b_gate, jnp.zeros((1, F_PAD), F32)], axis=1)
    f_off = QKV_COLS + 2 * d
    wo_full = wo.reshape(d, d)
    b_forget_row = jnp.pad(b_forget, ((0, 0), (0, QB - n_forget)))

    h2, rstd2 = rms_fwd("mix_rms", x1, norm_mix)
    qkv = proj("mix_proj_qkv", h2, wcat, bias_cat, 0, QKV_COLS, WIDTH, BF16, scaled_tiles=(0, 3))
    pc = proj("mix_proj_gates", h2, wcat, bias_cat, QKV_COLS, 2 * d + F_PAD, 768, F32)
    f_logit = pc[:, 2 * d:2 * d + QB]
    c_cum = fox_prep("fox_prep", f_logit, b_forget_row)
    c_heads = c_cum[:, :N_HEADS].T
    c_col = c_heads[:, :, None]
    c_row = c_heads.reshape(N_HEADS, t // CS, 1, CS)
    ya = sb_pair_fwd("sb_fwd", qkv)
    yb, lse = fox_pair_fwd("fox_fwd", qkv, c_col, c_row)
    ua, ub, mixed = mix_fwd("mix_fwd", ya, yb, wup, pc)
    x2 = mm_residual("mix_out", mixed[None], wo_full[None], pl.BlockSpec((1, d, d), lambda m: (0, 0, 0)), x1, 1.0)
    (w3_2,) = gather_forward("gather_ffn2_forward", gather_ici_wait("gather_ffn2_wait", f2_send, f2_recv, f2_bufs, x2))
    x3, saved2 = ffn_forward("ffn2", x2, norm_ffn2, w3_2)
    dx3, gn_final, loss_part = loss_head("loss_head", x3, norm_final[None], target)

    dx2, gn_ffn2, g_w3_2 = ffn_backward("ffn2", dx3, saved2, norm_ffn2, w3_2)
    rs_ffn2, rs_ffn2_token = reduce_scatter_begin("rs_ffn2", [g_w3_2], c_arr)

    dua, dub, dpa, dpb, gba, gbb = mix_bwd("mix_bwd", dx2, wo_full, pc, ua, ub, after=(rs_ffn2_token,))
    dgp = jnp.concatenate([dpa, dpb], axis=1)
    g_bgate = jnp.concatenate([gba, gbb], axis=1)
    g_wo = mm_plain("wgrad_out", mixed, dx2, TN, BF16, tk_target=1024)
    dya = up_bwd("dya", dua, wup, 0)
    dyb = up_bwd("dyb", dub, wup, 1)
    g_up = up_wgrad("wgrad_up", ya, yb, dua, dub)
    dqkv = sb_pair_bwd("sb_bwd", qkv, dya, lax.empty((t, QKV_COLS), BF16))
    dqkv, dcc, dcr = fox_pair_bwd("fox_bwd", qkv, yb, dyb, lse, c_col, c_row, dqkv)
    dc = (dcc[:, :, 0] + dcr.reshape(N_HEADS, t)).T
    df, g_bf = fox_gate_bwd("fox_gate_bwd", jnp.pad(dc, ((0, 0), (0, QB - N_HEADS))), f_logit, b_forget_row)
    dcat = jnp.concatenate([dqkv, dgp, df.astype(BF16), jnp.zeros((t, F_PAD - QB), BF16)], axis=1)
    g_wcat = wgrad_cat("wgrad_cat", h2, dcat)
    tm = _row_tile(t, 512)
    tkc = 1792
    nkc = wcat.shape[0] // tkc
    dx1, gn_mix = dh_rms_bwd(
        "mix_dh", [(dcat, wcat)],
        [(pl.BlockSpec((tm, tkc), lambda m, k: (m, k)), pl.BlockSpec((tkc, d), lambda m, k: (k, 0)))], NN,
        (t // tm, nkc), nkc, x1, rstd2, norm_mix, dx2)
    g_in_t = jnp.concatenate([g_wcat[:QKV_COLS], g_wcat[f_off:f_off + n_forget]], axis=0).reshape(N_CHIPS, in4, d)
    g_ig = jnp.concatenate([jnp.pad(g_in_t, ((0, 0), (0, in_pad - in4), (0, 0))),
                            g_wcat[QKV_COLS:f_off].reshape(N_CHIPS, gate4, d)], axis=1)[:, None]
    rs_mixer, rs_mixer_token = reduce_scatter_begin(
        "rs_mixer", [g_ig, g_up[:, None], g_wo.reshape(N_CHIPS, 1, d // N_CHIPS, d)], c_arr)
    (s_w3_2,) = reduce_scatter_end("rs_ffn2", rs_ffn2, c_arr, dx1)

    dx0, gn_ffn1, g_w3_1 = ffn_backward("ffn1", dx1, saved1, norm_ffn1, w3_1, after=(rs_mixer_token,))
    rs_ffn1, rs_ffn1_token = reduce_scatter_begin("rs_ffn1", [g_w3_1], c_arr)
    s_ig, s_up, s_wo = reduce_scatter_end("rs_mixer", rs_mixer, c_arr, dx0)

    def pack_small(n1, nm, n2, nf, bg, bf, last):
        return jnp.concatenate([n1, nm, n2, nf, bg.reshape(2, d), jnp.pad(bf, ((0, 0), (0, d - n_forget))), last], axis=0)

    zero_row = jnp.zeros((1, d), F32)
    g_small = pack_small(gn_ffn1, gn_mix, gn_ffn2, gn_final, g_bgate, g_bf[:, :n_forget], jnp.pad(loss_part, ((0, 0), (0, d - 1))))
    w_small = pack_small(norm_ffn1, norm_mix, norm_ffn2, norm_final[None], b_gate, b_forget, zero_row)
    m_small = pack_small(m_norm_ffn1, m_norm_mix, m_norm_ffn2, m_norm_final[None], m_b_gate, m_b_forget, zero_row)
    v_small = pack_small(v_norm_ffn1, v_norm_mix, v_norm_ffn2, v_norm_final[None], v_b_gate, v_b_forget, zero_row)
    smalls = small_allreduce_adamw("small_allreduce_adamw", g_small, w_small, m_small, v_small)

    def unpack_small(p):
        return {"norm_ffn1": p[0:1], "norm_mix": p[1:2], "norm_ffn2": p[2:3], "norm_final": p[3], "b_gate": p[4:6].reshape(1, 2 * d),
                "b_forget": p[6:7, :n_forget]}

    loss = smalls[0][7, 0]
    small_out = [unpack_small(p) for p in smalls]

    grads = {
        "w_in": s_ig[0, :in4].T, "w_gate": s_ig[0, in_pad:].T,
        "w_up_a": s_up[0, :WIDTH], "w_up_b": s_up[0, WIDTH:], "w_out": s_wo[0],
    }
    weights = {"w_in": (w_in, m_w_in, v_w_in), "w_gate": (w_gate, m_w_gate, v_w_gate), "w_up_a": (w_up_a, m_w_up_a, v_w_up_a),
               "w_up_b": (w_up_b, m_w_up_b, v_w_up_b), "w_out": (w_out, m_w_out, v_w_out)}
    big_out = {}
    for wname, (w, m, v) in weights.items():
        g = grads[wname]
        delta, new_m, new_v = adamw(f"adamw_{wname}", w[0], g, m[0], v[0], after=(rs_ffn1_token,))
        big_out[wname] = (g[None], delta[None], new_m[None], new_v[None])

    def adamw_ffn(tag, s_w3, ws, ms, vs, after):
        deltas, new_ms, new_vs = adamw_stacked(f"adamw_{tag}", ws, s_w3, ms, vs, after)
        for which, part in ((GATE, "gate"), (UP, "up"), (DOWN, "down")):
            back = (lambda a: a[None]) if which == DOWN else (lambda a: a.T[None])
            big_out[f"w_{tag}_{part}"] = tuple(back(a) for a in (s_w3[which], deltas[which], new_ms[which], new_vs[which]))
        return deltas[DOWN]

    last = adamw_ffn("ffn2", s_w3_2, ffn2_w, ffn2_m, ffn2_v, (rs_ffn1_token,))
    (s_w3_1,) = reduce_scatter_end("rs_ffn1", rs_ffn1, c_arr, last)
    adamw_ffn("ffn1", s_w3_1, ffn1_w, ffn1_m, ffn1_v, ())

    order = ["norm_ffn1", "w_ffn1_gate", "w_ffn1_up", "w_ffn1_down", "norm_mix", "w_in", "b_forget", "w_gate", "b_gate",
             "w_up_a", "w_up_b", "w_out", "norm_ffn2", "w_ffn2_gate", "w_ffn2_up", "w_ffn2_down", "norm_final"]
    outs = [loss, dx0[None]]
    for kind in range(4):
        for wname in order:
            outs.append(big_out[wname][kind] if wname in big_out else small_out[kind][wname])
    return tuple(outs)
```

```python
import functools

import jax
import jax.numpy as jnp
from jax import lax
from jax.experimental import pallas as pl
from jax.experimental.pallas import tpu as pltpu

F32 = jnp.float32
BF16 = jnp.bfloat16

HEAD_DIM = 64
N_HEADS = 8
WIDTH = N_HEADS * HEAD_DIM
QKV_COLS = 6 * WIDTH
RMS_EPS = 1e-6
ATTN_SCALE = HEAD_DIM ** -0.5
N_CHIPS = 4
QB = 128
BQ = 512
CS = 256
N_SUB = BQ // CS
F_PAD = 256
NEG_BIG = -1e30

ADAM_LR = 0.001
ADAM_B1 = 0.9
ADAM_B2 = 0.999
ADAM_EPS = 1e-08
ADAM_WD = 0.01
ADAM_STEP = 10

VMEM_LIMIT_BYTES = 48 * 1024 * 1024
MESH = pl.DeviceIdType.MESH

NN = ((1,), (0,))
NT = ((1,), (1,))
TN = ((0,), (0,))


def _params(semantics):
    return pltpu.CompilerParams(dimension_semantics=semantics, vmem_limit_bytes=VMEM_LIMIT_BYTES)


def _dot(a, b, contract):
    return lax.dot_general(a.astype(BF16), b.astype(BF16), (contract, ((), ())), preferred_element_type=F32)


def _sigmoid(x):
    return 1.0 / (1.0 + jnp.exp(-x))


def _log1pexp_neg_abs(z):
    return jnp.log(1.0 + jnp.exp(-jnp.abs(z)))


def _split3(x):
    hi = x.astype(BF16)
    r1 = x - hi.astype(F32)
    mid = r1.astype(BF16)
    lo = (r1 - mid.astype(F32)).astype(BF16)
    return hi, mid, lo


def _dot_exact_rhs01(x, m01):
    hi, mid, lo = _split3(x)
    d = lambda p: lax.dot_general(p, m01, (NN, ((), ())), preferred_element_type=F32)
    return d(hi) + d(mid) + d(lo)


def _dot_exact_lhs01(m01, x):
    hi, mid, lo = _split3(x)
    d = lambda p: lax.dot_general(m01, p, (NN, ((), ())), preferred_element_type=F32)
    return d(hi) + d(mid) + d(lo)


def _iota2(shape, dim):
    return lax.broadcasted_iota(jnp.int32, shape, dim)


def _mm(name, pairs, contract, grid, pair_specs, out_shape, out_specs, acc_shape, nk, epilogue,
        extras=(), extra_specs=(), semantics=None):
    n_pairs = len(pairs)
    n_extra = len(extras)
    n_out = len(out_shape)

    def body(*refs):
        ab = refs[:2 * n_pairs]
        ex = refs[2 * n_pairs:2 * n_pairs + n_extra]
        outs = refs[2 * n_pairs + n_extra:2 * n_pairs + n_extra + n_out]
        ids = [pl.program_id(i) for i in range(len(grid))]
        k = ids[-1]
        part = _dot(ab[0][...], ab[1][...], contract)
        for p in range(1, n_pairs):
            part += _dot(ab[2 * p][...], ab[2 * p + 1][...], contract)
        if nk == 1:
            epilogue(part, ex, outs, ids)
            return
        acc = refs[-1]

        @pl.when(k == 0)
        def _():
            acc[...] = part

        @pl.when(k != 0)
        def _():
            acc[...] += part

        @pl.when(k == nk - 1)
        def _():
            epilogue(acc[...], ex, outs, ids)

    operands = [t for pair in pairs for t in pair] + list(extras)
    in_specs = [s for pair in pair_specs for s in pair] + list(extra_specs)
    if semantics is None:
        semantics = ("parallel",) * (len(grid) - 1) + ("arbitrary",)
    return pl.pallas_call(
        body, name=name, grid=grid, in_specs=in_specs, out_specs=list(out_specs), out_shape=list(out_shape),
        scratch_shapes=[] if nk == 1 else [pltpu.VMEM(acc_shape, F32)], compiler_params=_params(semantics),
    )(*operands)


def _ordered_after(body, n_in, n_after):
    def wrapped(*refs):
        return body(*refs[:n_in], *refs[n_in + n_after:])
    return wrapped


def _row_tile(rows, target):
    t = min(rows, target)
    while rows % t:
        t //= 2
    return t


def rms_fwd(name, x, g, after=()):
    t, d = x.shape
    tr = _row_tile(t, 256)

    def body(x_ref, g_ref, h_ref, r_ref):
        xv = x_ref[...]
        r = lax.rsqrt(jnp.mean(xv * xv, axis=-1, keepdims=True) + RMS_EPS)
        h_ref[...] = (xv * r * g_ref[...]).astype(BF16)
        r_ref[...] = r

    return pl.pallas_call(
        _ordered_after(body, 2, len(after)), name=name, grid=(t // tr,),
        in_specs=[pl.BlockSpec((tr, d), lambda i: (i, 0)), pl.BlockSpec((1, d), lambda i: (0, 0))] + [_ANY] * len(after),
        out_specs=[pl.BlockSpec((tr, d), lambda i: (i, 0)), pl.BlockSpec((tr, 1), lambda i: (i, 0))],
        out_shape=[jax.ShapeDtypeStruct((t, d), BF16), jax.ShapeDtypeStruct((t, 1), F32)],
        compiler_params=_params(("parallel",)),
    )(x, g, *after)


GATE, UP, DOWN = 0, 1, 2


def _ffn_w_spec(which, f4, d, index_of_j):
    return pl.BlockSpec((None, None, f4, d), lambda *ids: (index_of_j(*ids), which, 0, 0))


def ffn_up(name, h, w3):
    t, d = h.shape
    ns, _, f4, _ = w3.shape
    tm = _row_tile(t, 512)

    def body(h_ref, wg_ref, wu_ref, a_ref, b_ref, s_ref):
        hv = h_ref[...]
        a = _dot(hv, wg_ref[...], NT)
        b = _dot(hv, wu_ref[...], NT)
        a_ref[...] = a.astype(BF16)
        b_ref[...] = b.astype(BF16)
        s_ref[...] = (a * _sigmoid(a) * b).astype(BF16)

    act_spec = pl.BlockSpec((None, tm, f4), lambda j, m: (j, m, 0))
    return pl.pallas_call(
        body, name=name, grid=(ns, t // tm),
        in_specs=[pl.BlockSpec((tm, d), lambda j, m: (m, 0)),
                  _ffn_w_spec(GATE, f4, d, lambda j, m: j), _ffn_w_spec(UP, f4, d, lambda j, m: j)],
        out_specs=[act_spec, act_spec, act_spec],
        out_shape=[jax.ShapeDtypeStruct((ns, t, f4), BF16)] * 3,
        compiler_params=_params(("parallel", "parallel")),
    )(h, w3, w3)


def mm_residual(name, s, w, w_spec, x, scale):
    nj, t, kdim = s.shape
    n = x.shape[1]
    tm = _row_tile(t, 512)

    def body(s_ref, w_ref, x_ref, o_ref):
        acc = _dot(s_ref[0], w_ref[0], NN)
        for j in range(1, nj):
            acc += _dot(s_ref[j], w_ref[j], NN)
        o_ref[...] = x_ref[...] + scale * acc

    row = pl.BlockSpec((tm, n), lambda m: (m, 0))
    return pl.pallas_call(
        body, name=name, grid=(t // tm,),
        in_specs=[pl.BlockSpec((nj, tm, kdim), lambda m: (0, m, 0)), w_spec, row], out_specs=row,
        out_shape=jax.ShapeDtypeStruct((t, n), F32), compiler_params=_params(("parallel",)),
    )(s, w, x)


def ffn_bwd_act(name, dx, w3, a, b, after=()):
    t, d = dx.shape
    ns, _, f4, _ = w3.shape
    tm = _row_tile(t, 512)

    def body(dx_ref, wd_ref, a_ref, b_ref, da_ref, db_ref):
        ds = _dot(0.5 * dx_ref[...], wd_ref[...], NT)
        av = a_ref[...].astype(F32)
        sig = _sigmoid(av)
        da_ref[...] = (ds * b_ref[...].astype(F32) * (sig * (1.0 + av * (1.0 - sig)))).astype(BF16)
        db_ref[...] = (ds * (av * sig)).astype(BF16)

    act_spec = pl.BlockSpec((None, tm, f4), lambda j, m: (j, m, 0))
    return pl.pallas_call(
        _ordered_after(body, 4, len(after)), name=name, grid=(ns, t // tm),
        in_specs=[pl.BlockSpec((tm, d), lambda j, m: (m, 0)), _ffn_w_spec(DOWN, f4, d, lambda j, m: j), act_spec, act_spec]
        + [_ANY] * len(after),
        out_specs=[act_spec, act_spec],
        out_shape=[jax.ShapeDtypeStruct((ns, t, f4), BF16)] * 2,
        compiler_params=_params(("parallel", "parallel")),
    )(dx, w3, a, b, *after)


def ffn_wgrad(name, h, da, db, s, dx):
    t, d = h.shape
    ns, _, f4 = da.shape
    tk = _row_tile(t, 1024)
    nk = t // tk

    def body(h_ref, da_ref, db_ref, s_ref, dx_ref, o_ref, acc):
        k = pl.program_id(1)

        @pl.when(k == 0)
        def _():
            acc[...] = jnp.zeros_like(acc)

        hv = h_ref[...]
        acc[GATE] += _dot(da_ref[...], hv, TN)
        acc[UP] += _dot(db_ref[...], hv, TN)
        acc[DOWN] += _dot(s_ref[...], 0.5 * dx_ref[...], TN)

        @pl.when(k == nk - 1)
        def _():
            o_ref[...] = acc[...].astype(BF16)

    act_spec = pl.BlockSpec((None, tk, f4), lambda j, k: (j, k, 0))
    row_spec = pl.BlockSpec((tk, d), lambda j, k: (k, 0))
    return pl.pallas_call(
        body, name=name, grid=(ns, nk),
        in_specs=[row_spec, act_spec, act_spec, act_spec, row_spec],
        out_specs=pl.BlockSpec((None, 3, f4, d), lambda j, k: (j, 0, 0, 0)),
        out_shape=jax.ShapeDtypeStruct((ns, 3, f4, d), BF16),
        scratch_shapes=[pltpu.VMEM((3, f4, d), F32)],
        compiler_params=_params(("parallel", "arbitrary")),
    )(h, da, db, s, dx)


def _rms_bwd_tail(dh, x_ref, r_ref, g_ref, dxin_ref, dx_ref, gn_ref, row_tile_index):
    r = r_ref[...]
    xhat = x_ref[...] * r
    dhg = dh * g_ref[...]
    dx_ref[...] = dxin_ref[...] + r * (dhg - xhat * jnp.mean(dhg * xhat, axis=-1, keepdims=True))
    part = jnp.sum(dh * xhat, axis=0, keepdims=True)

    @pl.when(row_tile_index == 0)
    def _():
        gn_ref[...] = part

    @pl.when(row_tile_index != 0)
    def _():
        gn_ref[...] += part


def ffn_dh(name, da, db, w3, x, rstd, g, dx_in, after=()):
    ns, t, f4 = da.shape
    d = x.shape[1]
    tm = _row_tile(t, 256)

    def body(da_ref, db_ref, wg_ref, wu_ref, x_ref, r_ref, g_ref, dxin_ref, dx_ref, gn_ref):
        dh = _dot(da_ref[0], wg_ref[0], NN) + _dot(db_ref[0], wu_ref[0], NN)
        for j in range(1, ns):
            dh += _dot(da_ref[j], wg_ref[j], NN) + _dot(db_ref[j], wu_ref[j], NN)
        _rms_bwd_tail(dh, x_ref, r_ref, g_ref, dxin_ref, dx_ref, gn_ref, pl.program_id(0))

    act = pl.BlockSpec((ns, tm, f4), lambda m: (0, m, 0))
    row = pl.BlockSpec((tm, d), lambda m: (m, 0))
    gain = pl.BlockSpec((1, d), lambda m: (0, 0))
    return pl.pallas_call(
        _ordered_after(body, 8, len(after)), name=name, grid=(t // tm,),
        in_specs=[act, act, pl.BlockSpec((ns, None, f4, d), lambda m: (0, GATE, 0, 0)),
                  pl.BlockSpec((ns, None, f4, d), lambda m: (0, UP, 0, 0)), row, pl.BlockSpec((tm, 1), lambda m: (m, 0)), gain, row]
        + [_ANY] * len(after),
        out_specs=[row, gain], out_shape=[jax.ShapeDtypeStruct((t, d), F32), jax.ShapeDtypeStruct((1, d), F32)],
        compiler_params=_params(("arbitrary",)),
    )(da, db, w3, w3, x, rstd, g, dx_in, *after)


def dh_rms_bwd(name, pairs, pair_specs, contract, grid, nk, x, rstd, g, dx_in, after=()):
    t, d = x.shape
    tm = t // grid[0]

    def epilogue(acc, ex, outs, ids):
        _rms_bwd_tail(acc, *ex[:4], *outs, ids[0])

    row = pl.BlockSpec((tm, d), lambda m, k: (m, 0))
    return _mm(
        name, pairs, contract, grid, pair_specs,
        [jax.ShapeDtypeStruct((t, d), F32), jax.ShapeDtypeStruct((1, d), F32)],
        [row, pl.BlockSpec((1, d), lambda m, k: (0, 0))], (tm, d), nk, epilogue,
        extras=[x, rstd, g, dx_in, *after],
        extra_specs=[row, pl.BlockSpec((tm, 1), lambda m, k: (m, 0)), pl.BlockSpec((1, d), lambda m, k: (0, 0)), row]
        + [_ANY] * len(after),
        semantics=("arbitrary", "arbitrary"),
    )


def ffn_forward(tag, x, g_norm, w3, normed=None):
    _, _, f4, d = w3.shape
    h, rstd = normed if normed is not None else rms_fwd(f"{tag}_rms", x, g_norm)
    a, b, s = ffn_up(f"{tag}_up", h, w3)
    x_out = mm_residual(f"{tag}_down", s, w3, pl.BlockSpec((w3.shape[0], None, f4, d), lambda m: (0, DOWN, 0, 0)), x, 0.5)
    return x_out, (x, h, rstd, a, b, s)


def ffn_backward_weights(tag, dx, saved, w3, after=()):
    x, h, rstd, a, b, s = saved
    da, db = ffn_bwd_act(f"{tag}_bwd_act", dx, w3, a, b, after)
    return ffn_wgrad(f"{tag}_wgrad", h, da, db, s, dx), (da, db)


def ffn_backward_input(tag, dx, saved, dab, g_norm, w3, after=()):
    x, h, rstd, a, b, s = saved
    return ffn_dh(f"{tag}_dh", dab[0], dab[1], w3, x, rstd, g_norm, dx, after)


def proj(name, h, wcat_t, bias, first_col, n_cols, tn, out_dtype, scaled_tiles=()):
    t, d = h.shape
    tm = _row_tile(t, 512)
    off = first_col // tn

    def epilogue(acc, ex, outs, ids):
        val = acc + ex[0][...]
        if scaled_tiles:
            hit = functools.reduce(jnp.logical_or, [ids[0] == s for s in scaled_tiles])
            val = val * jnp.where(hit, ATTN_SCALE, 1.0)
        outs[0][...] = val.astype(out_dtype)

    return _mm(
        name, [(h, wcat_t)], NT, (n_cols // tn, t // tm, 1),
        [(pl.BlockSpec((tm, d), lambda j, m, k: (m, 0)), pl.BlockSpec((tn, d), lambda j, m, k: (off + j, 0)))],
        [jax.ShapeDtypeStruct((t, n_cols), out_dtype)], [pl.BlockSpec((tm, tn), lambda j, m, k: (m, j))], (tm, tn), 1, epilogue,
        extras=[bias], extra_specs=[pl.BlockSpec((1, tn), lambda j, m, k: (0, off + j))],
    )[0]


def mix_fwd(name, ya, yb, wup, pc):
    t, w = ya.shape
    ns, _, tn = wup.shape
    d = ns * tn
    tm = _row_tile(t, 512)

    def body(ya_ref, yb_ref, wa_ref, wb_ref, pa_ref, pb_ref, ua_ref, ub_ref, mx_ref):
        ua = _dot(ya_ref[...], wa_ref[...], NN)
        ub = _dot(yb_ref[...], wb_ref[...], NN)
        ua_ref[...] = ua
        ub_ref[...] = ub
        mx_ref[...] = (_sigmoid(pa_ref[...]) * ua + _sigmoid(pb_ref[...]) * ub).astype(BF16)

    y_spec = pl.BlockSpec((tm, w), lambda m, n: (m, 0))
    o_spec = pl.BlockSpec((tm, tn), lambda m, n: (m, n))
    return pl.pallas_call(
        body, name=name, grid=(t // tm, ns),
        in_specs=[y_spec, y_spec, pl.BlockSpec((None, w, tn), lambda m, n: (n, 0, 0)), pl.BlockSpec((None, w, tn), lambda m, n: (n, 1, 0)),
                  o_spec, pl.BlockSpec((tm, tn), lambda m, n: (m, ns + n))],
        out_specs=[o_spec, o_spec, o_spec],
        out_shape=[jax.ShapeDtypeStruct((t, d), F32), jax.ShapeDtypeStruct((t, d), F32), jax.ShapeDtypeStruct((t, d), BF16)],
        compiler_params=_params(("parallel", "parallel")),
    )(ya, yb, wup, wup, pc, pc)


def up_bwd(name, du, wup, branch):
    t, d = du.shape
    ns, w2, tn = wup.shape
    w = w2 // 2
    tm = _row_tile(t, 512)

    def body(du_ref, w_ref, o_ref):
        acc = _dot(du_ref[:, 0:tn], w_ref[0], NT)
        for j in range(1, ns):
            acc += _dot(du_ref[:, j * tn:(j + 1) * tn], w_ref[j], NT)
        o_ref[...] = acc.astype(BF16)

    return pl.pallas_call(
        body, name=name, grid=(t // tm,),
        in_specs=[pl.BlockSpec((tm, d), lambda m: (m, 0)), pl.BlockSpec((ns, w, tn), lambda m: (0, branch, 0))],
        out_specs=pl.BlockSpec((tm, w), lambda m: (m, 0)), out_shape=jax.ShapeDtypeStruct((t, w), BF16),
        compiler_params=_params(("parallel",)),
    )(du, wup)


def up_wgrad(name, ya, yb, dua, dub):
    t, w = ya.shape
    d = dua.shape[1]
    tn = d // N_CHIPS

    def body(ya_ref, yb_ref, dua_ref, dub_ref, o_ref):
        o_ref[0:w, :] = _dot(ya_ref[...], dua_ref[...], TN).astype(BF16)
        o_ref[w:2 * w, :] = _dot(yb_ref[...], dub_ref[...], TN).astype(BF16)

    y_spec = pl.BlockSpec((t, w), lambda j: (0, 0))
    du_spec = pl.BlockSpec((t, tn), lambda j: (0, j))
    return pl.pallas_call(
        body, name=name, grid=(N_CHIPS,), in_specs=[y_spec, y_spec, du_spec, du_spec],
        out_specs=pl.BlockSpec((None, 2 * w, tn), lambda j: (j, 0, 0)),
        out_shape=jax.ShapeDtypeStruct((N_CHIPS, 2 * w, tn), BF16), compiler_params=_params(("parallel",)),
    )(ya, yb, dua, dub)


def mix_bwd(name, dx, wo, pc, ua, ub, after=()):
    t, d = dx.shape
    tm = _row_tile(t, 512)
    tn = 512
    off_a = 0
    off_b = d // tn

    def body(dx_ref, wo_ref, pa_ref, pb_ref, ua_ref, ub_ref, dua_ref, dub_ref, dpa_ref, dpb_ref, ba_ref, bb_ref):
        dm = _dot(dx_ref[...], wo_ref[...], NT)
        ga = _sigmoid(pa_ref[...])
        gb = _sigmoid(pb_ref[...])
        dua_ref[...] = (dm * ga).astype(BF16)
        dub_ref[...] = (dm * gb).astype(BF16)
        dpa = dm * ua_ref[...] * ga * (1.0 - ga)
        dpb = dm * ub_ref[...] * gb * (1.0 - gb)
        dpa_ref[...] = dpa.astype(BF16)
        dpb_ref[...] = dpb.astype(BF16)
        sa = jnp.sum(dpa, axis=0, keepdims=True)
        sb = jnp.sum(dpb, axis=0, keepdims=True)

        @pl.when(pl.program_id(1) == 0)
        def _():
            ba_ref[...] = sa
            bb_ref[...] = sb

        @pl.when(pl.program_id(1) != 0)
        def _():
            ba_ref[...] += sa
            bb_ref[...] += sb

    tile = pl.BlockSpec((tm, tn), lambda n, m: (m, n))
    bias = pl.BlockSpec((1, tn), lambda n, m: (0, n))
    return pl.pallas_call(
        _ordered_after(body, 6, len(after)), name=name, grid=(d // tn, t // tm),
        in_specs=[pl.BlockSpec((tm, d), lambda n, m: (m, 0)), pl.BlockSpec((tn, d), lambda n, m: (n, 0)),
                  pl.BlockSpec((tm, tn), lambda n, m: (m, off_a + n)), pl.BlockSpec((tm, tn), lambda n, m: (m, off_b + n)),
                  tile, tile] + [_ANY] * len(after),
        out_specs=[tile, tile, tile, tile, bias, bias],
        out_shape=[jax.ShapeDtypeStruct((t, d), BF16)] * 4 + [jax.ShapeDtypeStruct((1, d), F32)] * 2,
        compiler_params=_params(("parallel", "arbitrary")),
    )(dx, wo, pc, pc, ua, ub, *after)


def mm_plain(name, a, b, contract, out_dtype, tk_target=512):
    if contract == NN:
        m, kdim = a.shape
        n = b.shape[1]
    elif contract == NT:
        m, kdim = a.shape
        n = b.shape[0]
    else:
        kdim, m = a.shape
        n = b.shape[1]
    tm = _row_tile(m, 512)
    tk = _row_tile(kdim, tk_target)
    nk = kdim // tk
    if contract == TN:
        a_spec = pl.BlockSpec((tk, tm), lambda i, k: (k, i))
    else:
        a_spec = pl.BlockSpec((tm, tk), lambda i, k: (i, k))
    if contract == NT:
        b_spec = pl.BlockSpec((n, tk), lambda i, k: (0, k))
    else:
        b_spec = pl.BlockSpec((tk, n), lambda i, k: (k, 0))

    def epilogue(acc, ex, outs, ids):
        outs[0][...] = acc.astype(out_dtype)

    return _mm(name, [(a, b)], contract, (m // tm, nk), [(a_spec, b_spec)],
               [jax.ShapeDtypeStruct((m, n), out_dtype)], [pl.BlockSpec((tm, n), lambda i, k: (i, 0))], (tm, n), nk, epilogue)[0]


def wgrad_cat(name, h, dcat):
    t, d = h.shape
    n = dcat.shape[1]
    tn = 768
    tk = _row_tile(t, 2048)

    def epilogue(acc, ex, outs, ids):
        outs[0][...] = acc.astype(BF16)

    return _mm(
        name, [(dcat, h)], TN, (n // tn, t // tk),
        [(pl.BlockSpec((tk, tn), lambda j, k: (k, j)), pl.BlockSpec((tk, d), lambda j, k: (k, 0)))],
        [jax.ShapeDtypeStruct((n, d), BF16)], [pl.BlockSpec((tn, d), lambda j, k: (j, 0))], (tn, d), t // tk, epilogue,
    )[0]


def fox_prep(name, f, bias):
    t, lanes = f.shape
    nchunk = t // QB

    def body(f_ref, b_ref, c_ref):
        lower = (_iota2((QB, QB), 1) <= _iota2((QB, QB), 0)).astype(BF16)

        def chunk(n, carry):
            rows = pl.ds(pl.multiple_of(n * QB, QB), QB)
            u = f_ref[rows, :] + b_ref[...]
            lf = jnp.minimum(u, 0.0) - _log1pexp_neg_abs(u)
            c = _dot_exact_lhs01(lower, lf) + carry
            c_ref[rows, :] = c
            return c[QB - 1:QB, :]

        lax.fori_loop(0, nchunk, chunk, jnp.zeros((1, lanes), F32))

    return pl.pallas_call(body, name=name, out_shape=jax.ShapeDtypeStruct((t, lanes), F32),
                          compiler_params=pltpu.CompilerParams(vmem_limit_bytes=VMEM_LIMIT_BYTES))(f, bias)


def fox_gate_bwd(name, dc, f, bias):
    t, lanes = dc.shape
    nchunk = t // QB

    def body(dc_ref, f_ref, b_ref, df_ref, gb_ref):
        upper = (_iota2((QB, QB), 1) >= _iota2((QB, QB), 0)).astype(BF16)

        def chunk(n, carry):
            tail, total = carry
            rows = pl.ds(pl.multiple_of((nchunk - 1 - n) * QB, QB), QB)
            dlf = _dot_exact_lhs01(upper, dc_ref[rows, :]) + tail
            u = f_ref[rows, :] + b_ref[...]
            df = dlf * jnp.exp(jnp.minimum(-u, 0.0) - _log1pexp_neg_abs(u))
            df_ref[rows, :] = df
            return dlf[0:1, :], total + jnp.sum(df, axis=0, keepdims=True)

        zero = jnp.zeros((1, lanes), F32)
        _, total = lax.fori_loop(0, nchunk, chunk, (zero, zero))
        gb_ref[...] = total

    return pl.pallas_call(body, name=name,
                          out_shape=[jax.ShapeDtypeStruct((t, lanes), F32), jax.ShapeDtypeStruct((1, lanes), F32)],
                          compiler_params=pltpu.CompilerParams(vmem_limit_bytes=VMEM_LIMIT_BYTES))(dc, f, bias)


def _qrows(i):
    return pl.ds(pl.multiple_of(i * BQ, BQ), BQ)


def _krows(kc):
    return pl.ds(pl.multiple_of(kc * CS, CS), CS)


def _head_spec(t, offset):
    return pl.BlockSpec((None, t, HEAD_DIM), lambda h: (offset + h, 0, 0))


def _head_t_spec(nq, offset):
    return pl.BlockSpec((None, nq, HEAD_DIM, BQ), lambda h: (offset + h, 0, 0, 0))


def _chunk_t_spec(nc):
    return pl.BlockSpec((None, nc, HEAD_DIM, CS), lambda h: (h, 0, 0, 0))


def _dot_split2_rhs01(x, m01):
    hi = x.astype(BF16)
    lo = (x - hi.astype(F32)).astype(BF16)
    d = lambda p: lax.dot_general(p, m01, (NN, ((), ())), preferred_element_type=F32)
    return d(hi) + d(lo)


def _diag_mask(dchunk, inclusive):
    r_io = _iota2((BQ, CS), 0)
    c_io = _iota2((BQ, CS), 1) + dchunk * CS
    return c_io <= r_io if inclusive else c_io < r_io


def _walk_chunks(i, step, init, right_to_left):
    order = list(reversed(range(N_SUB))) if right_to_left else list(range(N_SUB))

    def diagonal(state):
        for dchunk in order:
            state = step(i * N_SUB + dchunk, state, dchunk)
        return state

    def group(n, state):
        base = ((i - 1 - n) if right_to_left else n) * N_SUB
        for dchunk in order:
            state = step(base + dchunk, state, None)
        return state

    if right_to_left:
        return lax.fori_loop(0, i, group, diagonal(init))
    return diagonal(lax.fori_loop(0, i, group, init))


def sb_fwd(name, qkv):
    t = qkv.shape[1]

    def body(q_ref, k_ref, v_ref, o_ref):
        later = (_iota2((CS, CS), 0) > _iota2((CS, CS), 1)).astype(BF16)

        def qblock(i, _):
            q = q_ref[_qrows(i), :]

            def step(kc, state, dchunk):
                carry, acc = state
                z = _dot(q, k_ref[_krows(kc), :], NT)
                sp = _log1pexp_neg_abs(z)
                lnb = -jnp.maximum(z, 0.0) - sp
                if dchunk is not None:
                    lnb = jnp.where(_diag_mask(dchunk, False), lnb, 0.0)
                w = jnp.exp(jnp.minimum(z, 0.0) - sp + _dot_split2_rhs01(lnb, later) + carry)
                if dchunk is not None:
                    w = jnp.where(_diag_mask(dchunk, False), w, 0.0)
                acc = acc + _dot(w, v_ref[_krows(kc), :], NN)
                return carry + jnp.sum(lnb, axis=1, keepdims=True), acc

            init = (jnp.zeros((BQ, 1), F32), jnp.zeros((BQ, HEAD_DIM), F32))
            _, acc = _walk_chunks(i, step, init, True)
            o_ref[_qrows(i), :] = acc.astype(BF16)
            return 0

        lax.fori_loop(0, t // BQ, qblock, 0)

    return pl.pallas_call(
        body, name=name, grid=(N_HEADS,),
        in_specs=[_head_spec(t, 0), _head_spec(t, N_HEADS), _head_spec(t, 2 * N_HEADS)],
        out_specs=_head_spec(t, 0), out_shape=jax.ShapeDtypeStruct((N_HEADS, t, HEAD_DIM), BF16),
        compiler_params=_params(("parallel",)),
    )(qkv, qkv, qkv)


def sb_bwd(name, qkv, qt, dy, dyt):
    t = qkv.shape[1]
    nq, nc = t // BQ, t // CS

    def body(q_ref, k_ref, v_ref, qt_ref, do_ref, dot_ref, dq_ref, dkt_ref, dvt_ref, g_s, b_s, dkt_acc, dvt_acc):
        later = (_iota2((CS, CS), 0) > _iota2((CS, CS), 1)).astype(BF16)
        earlier = (_iota2((CS, CS), 0) < _iota2((CS, CS), 1)).astype(BF16)
        dkt_acc[...] = jnp.zeros_like(dkt_acc)
        dvt_acc[...] = jnp.zeros_like(dvt_acc)

        def qblock(i, _):
            q = q_ref[_qrows(i), :]
            do = do_ref[_qrows(i), :]
            q_t = qt_ref[i]
            do_t = dot_ref[i]

            def step1(kc, carry, dchunk):
                z = _dot(q, k_ref[_krows(kc), :], NT)
                sp = _log1pexp_neg_abs(z)
                lnb = -jnp.maximum(z, 0.0) - sp
                lsz = jnp.minimum(z, 0.0) - sp
                if dchunk is not None:
                    lnb = jnp.where(_diag_mask(dchunk, False), lnb, 0.0)
                w = jnp.exp(lsz + _dot_split2_rhs01(lnb, later) + carry)
                if dchunk is not None:
                    w = jnp.where(_diag_mask(dchunk, False), w, 0.0)
                g_s[kc] = w * _dot(do, v_ref[_krows(kc), :], NT)
                b_s[kc] = jnp.exp(lsz)
                dvt_acc[kc] += _dot(do_t, w, NN)
                return carry + jnp.sum(lnb, axis=1, keepdims=True)

            _walk_chunks(i, step1, jnp.zeros((BQ, 1), F32), True)

            def step2(kc, state, dchunk):
                before, dq = state
                g = g_s[kc]
                beta = b_s[kc]
                dz = g * (1.0 - beta) - beta * (_dot_split2_rhs01(g, earlier) + before)
                if dchunk is not None:
                    dz = jnp.where(_diag_mask(dchunk, False), dz, 0.0)
                dzb = dz.astype(BF16)
                dq = dq + _dot(dzb, k_ref[_krows(kc), :], NN)
                dkt_acc[kc] += _dot(q_t, dzb, NN)
                return before + jnp.sum(g, axis=1, keepdims=True), dq

            _, dq = _walk_chunks(i, step2, (jnp.zeros((BQ, 1), F32), jnp.zeros((BQ, HEAD_DIM), F32)), False)
            dq_ref[_qrows(i), :] = (dq * ATTN_SCALE).astype(BF16)
            return 0

        lax.fori_loop(0, nq, qblock, 0)
        dkt_ref[...] = dkt_acc[...].astype(BF16)
        dvt_ref[...] = dvt_acc[...].astype(BF16)

    chunked = jax.ShapeDtypeStruct((N_HEADS, nc, HEAD_DIM, CS), BF16)
    return pl.pallas_call(
        body, name=name, grid=(N_HEADS,),
        in_specs=[_head_spec(t, 0), _head_spec(t, N_HEADS), _head_spec(t, 2 * N_HEADS), _head_t_spec(nq, 0),
                  _head_spec(t, 0), _head_t_spec(nq, 0)],
        out_specs=[_head_spec(t, 0), _chunk_t_spec(nc), _chunk_t_spec(nc)],
        out_shape=[jax.ShapeDtypeStruct((N_HEADS, t, HEAD_DIM), BF16), chunked, chunked],
        scratch_shapes=[pltpu.VMEM((nc, BQ, CS), F32), pltpu.VMEM((nc, BQ, CS), F32),
                        pltpu.VMEM((nc, HEAD_DIM, CS), F32), pltpu.VMEM((nc, HEAD_DIM, CS), F32)],
        compiler_params=_params(("parallel",)),
    )(qkv, qkv, qkv, qt, dy, dyt)


def _col_spec(t):
    return pl.BlockSpec((None, t, 1), lambda h: (h, 0, 0))


def _row_spec(nc):
    return pl.BlockSpec((None, nc, 1, CS), lambda h: (h, 0, 0, 0))


def fox_fwd(name, qkv, c_col, c_row):
    t = qkv.shape[1]

    def body(q_ref, k_ref, v_ref, cc_ref, cr_ref, o_ref, lse_ref):
        def qblock(i, _):
            q = q_ref[_qrows(i), :]
            ct = cc_ref[_qrows(i), :]

            def step(kc, state, dchunk):
                m, l, acc = state
                s = _dot(q, k_ref[_krows(kc), :], NT) + ct - cr_ref[kc]
                if dchunk is not None:
                    s = jnp.where(_diag_mask(dchunk, True), s, NEG_BIG)
                m_new = jnp.maximum(m, jnp.max(s, axis=1, keepdims=True))
                alpha = jnp.exp(m - m_new)
                p = jnp.exp(s - m_new)
                if dchunk is not None:
                    p = jnp.where(_diag_mask(dchunk, True), p, 0.0)
                l = l * alpha + jnp.sum(p, axis=1, keepdims=True)
                acc = acc * alpha + _dot(p, v_ref[_krows(kc), :], NN)
                return m_new, l, acc

            init = (jnp.full((BQ, 1), NEG_BIG, F32), jnp.zeros((BQ, 1), F32), jnp.zeros((BQ, HEAD_DIM), F32))
            m, l, acc = _walk_chunks(i, step, init, False)
            o_ref[_qrows(i), :] = (acc / l).astype(BF16)
            lse_ref[_qrows(i), :] = m + jnp.log(l)
            return 0

        lax.fori_loop(0, t // BQ, qblock, 0)

    return pl.pallas_call(
        body, name=name, grid=(N_HEADS,),
        in_specs=[_head_spec(t, 3 * N_HEADS), _head_spec(t, 4 * N_HEADS), _head_spec(t, 5 * N_HEADS),
                  _col_spec(t), _row_spec(t // CS)],
        out_specs=[_head_spec(t, 0), _col_spec(t)],
        out_shape=[jax.ShapeDtypeStruct((N_HEADS, t, HEAD_DIM), BF16), jax.ShapeDtypeStruct((N_HEADS, t, 1), F32)],
        compiler_params=_params(("parallel",)),
    )(qkv, qkv, qkv, c_col, c_row)


def fox_bwd(name, qkv, qt, y, dy, dyt, lse, c_col, c_row):
    t = qkv.shape[1]
    nq, nc = t // BQ, t // CS

    def body(q_ref, k_ref, v_ref, qt_ref, o_ref, do_ref, dot_ref, lse_ref, cc_ref, cr_ref,
             dq_ref, dkt_ref, dvt_ref, dcc_ref, dcr_ref, dkt_acc, dvt_acc, dcr_acc):
        dkt_acc[...] = jnp.zeros_like(dkt_acc)
        dvt_acc[...] = jnp.zeros_like(dvt_acc)
        dcr_acc[...] = jnp.zeros_like(dcr_acc)

        def qblock(i, _):
            q = q_ref[_qrows(i), :]
            do = do_ref[_qrows(i), :]
            q_t = qt_ref[i]
            do_t = dot_ref[i]
            ct = cc_ref[_qrows(i), :]
            lse_i = lse_ref[_qrows(i), :]
            delta = jnp.sum(do.astype(F32) * o_ref[_qrows(i), :].astype(F32), axis=1, keepdims=True)

            def step(kc, state, dchunk):
                dq, dct = state
                s = _dot(q, k_ref[_krows(kc), :], NT) + ct - cr_ref[kc]
                p = jnp.exp(s - lse_i)
                if dchunk is not None:
                    p = jnp.where(_diag_mask(dchunk, True), p, 0.0)
                ds = p * (_dot(do, v_ref[_krows(kc), :], NT) - delta)
                dvt_acc[kc] += _dot(do_t, p, NN)
                dsb = ds.astype(BF16)
                dq = dq + _dot(dsb, k_ref[_krows(kc), :], NN)
                dkt_acc[kc] += _dot(q_t, dsb, NN)
                dcr_acc[kc] -= jnp.sum(ds, axis=0, keepdims=True)
                return dq, dct + jnp.sum(ds, axis=1, keepdims=True)

            dq, dct = _walk_chunks(i, step, (jnp.zeros((BQ, HEAD_DIM), F32), jnp.zeros((BQ, 1), F32)), False)
            dq_ref[_qrows(i), :] = (dq * ATTN_SCALE).astype(BF16)
            dcc_ref[_qrows(i), :] = dct
            return 0

        lax.fori_loop(0, nq, qblock, 0)
        dkt_ref[...] = dkt_acc[...].astype(BF16)
        dvt_ref[...] = dvt_acc[...].astype(BF16)
        dcr_ref[...] = dcr_acc[...]

    chunked = jax.ShapeDtypeStruct((N_HEADS, nc, HEAD_DIM, CS), BF16)
    return pl.pallas_call(
        body, name=name, grid=(N_HEADS,),
        in_specs=[_head_spec(t, 3 * N_HEADS), _head_spec(t, 4 * N_HEADS), _head_spec(t, 5 * N_HEADS), _head_t_spec(nq, N_HEADS),
                  _head_spec(t, 0), _head_spec(t, 0), _head_t_spec(nq, 0), _col_spec(t), _col_spec(t), _row_spec(nc)],
        out_specs=[_head_spec(t, 0), _chunk_t_spec(nc), _chunk_t_spec(nc), _col_spec(t), _row_spec(nc)],
        out_shape=[jax.ShapeDtypeStruct((N_HEADS, t, HEAD_DIM), BF16), chunked, chunked,
                   jax.ShapeDtypeStruct((N_HEADS, t, 1), F32), jax.ShapeDtypeStruct((N_HEADS, nc, 1, CS), F32)],
        scratch_shapes=[pltpu.VMEM((nc, HEAD_DIM, CS), F32), pltpu.VMEM((nc, HEAD_DIM, CS), F32), pltpu.VMEM((nc, 1, CS), F32)],
        compiler_params=_params(("parallel",)),
    )(qkv, qkv, qkv, qt, y, dy, dyt, lse, c_col, c_row)


PAIR = 2 * HEAD_DIM
N_PAIRS = N_HEADS // 2


def _pair_spec(t, first_block):
    return pl.BlockSpec((t, PAIR), lambda p, *_: (0, first_block + p))


def _head_lanes(shape):
    lane = _iota2(shape, len(shape) - 1)
    return [lane < HEAD_DIM, lane >= HEAD_DIM]


def _only_head(x, lanes_of_head):
    return jnp.where(lanes_of_head, x, jnp.zeros_like(x))


def _sb_chunk_weights(q_h, k, later, carry, dchunk):
    z = _dot(q_h, k, NT)
    sp = _log1pexp_neg_abs(z)
    lnb = -jnp.maximum(z, 0.0) - sp
    lsz = jnp.minimum(z, 0.0) - sp
    if dchunk is not None:
        lnb = jnp.where(_diag_mask(dchunk, False), lnb, 0.0)
    w = jnp.exp(lsz + _dot_split2_rhs01(lnb, later) + carry)
    if dchunk is not None:
        w = jnp.where(_diag_mask(dchunk, False), w, 0.0)
    return w, lsz, lnb


def sb_pair_fwd(name, qkv):
    t = qkv.shape[0]

    def body(q_ref, k_ref, v_ref, o_ref):
        later = (_iota2((CS, CS), 0) > _iota2((CS, CS), 1)).astype(BF16)
        lanes = _head_lanes((BQ, PAIR))

        def qblock(i, _):
            q = q_ref[_qrows(i), :]
            q_heads = [_only_head(q, lanes[h]) for h in range(2)]

            def step(kc, state, dchunk):
                k = k_ref[_krows(kc), :]
                v = v_ref[_krows(kc), :]
                out = []
                for h in range(2):
                    carry, acc = state[h]
                    w, _, lnb = _sb_chunk_weights(q_heads[h], k, later, carry, dchunk)
                    out.append((carry + jnp.sum(lnb, axis=1, keepdims=True), acc + _dot(w, v, NN)))
                return tuple(out)

            zero = (jnp.zeros((BQ, 1), F32), jnp.zeros((BQ, PAIR), F32))
            (_, acc0), (_, acc1) = _walk_chunks(i, step, (zero, zero), True)
            o_ref[_qrows(i), :] = jnp.where(lanes[0], acc0, acc1).astype(BF16)
            return 0

        lax.fori_loop(0, t // BQ, qblock, 0)

    return pl.pallas_call(
        body, name=name, grid=(N_PAIRS,),
        in_specs=[_pair_spec(t, 0), _pair_spec(t, N_PAIRS), _pair_spec(t, 2 * N_PAIRS)],
        out_specs=_pair_spec(t, 0), out_shape=jax.ShapeDtypeStruct((t, WIDTH), BF16),
        compiler_params=_params(("parallel",)),
    )(qkv, qkv, qkv)


def _emit_dqkv(res, o_ref):
    o_ref[...] = res[pl.program_id(1)]


def _flush_transposed(acc, res, which):
    for kc in range(acc.shape[0]):
        res[which, kc * CS:(kc + 1) * CS, :] = acc[kc].T.astype(BF16)


def sb_pair_bwd(name, qkv, dy, dqkv):
    t = qkv.shape[0]
    nc = t // CS

    def body(q_ref, k_ref, v_ref, do_ref, _, o_ref, g_s, b_s, dkt_acc, dvt_acc, res):
        @pl.when(pl.program_id(1) == 0)
        def _():
            later = (_iota2((CS, CS), 0) > _iota2((CS, CS), 1)).astype(BF16)
            earlier = (_iota2((CS, CS), 0) < _iota2((CS, CS), 1)).astype(BF16)
            lanes = _head_lanes((BQ, PAIR))
            dkt_acc[...] = jnp.zeros_like(dkt_acc)
            dvt_acc[...] = jnp.zeros_like(dvt_acc)

            def qblock(i, _):
                q = q_ref[_qrows(i), :]
                do = do_ref[_qrows(i), :]
                q_heads = [_only_head(q, lanes[h]) for h in range(2)]
                do_heads = [_only_head(do, lanes[h]) for h in range(2)]
                qt_heads = [qh.astype(F32).T.astype(BF16) for qh in q_heads]
                dot_heads = [dh.astype(F32).T.astype(BF16) for dh in do_heads]

                def step1(kc, carries, dchunk):
                    k = k_ref[_krows(kc), :]
                    v = v_ref[_krows(kc), :]
                    out = []
                    for h in range(2):
                        w, lsz, lnb = _sb_chunk_weights(q_heads[h], k, later, carries[h], dchunk)
                        g_s[h, kc] = (w * _dot(do_heads[h], v, NT)).astype(BF16)
                        b_s[h, kc] = jnp.exp(lsz).astype(BF16)
                        dvt_acc[kc] += _dot(dot_heads[h], w, NN)
                        out.append(carries[h] + jnp.sum(lnb, axis=1, keepdims=True))
                    return tuple(out)

                zero = jnp.zeros((BQ, 1), F32)
                _walk_chunks(i, step1, (zero, zero), True)

                def step2(kc, state, dchunk):
                    k = k_ref[_krows(kc), :]
                    out = []
                    for h in range(2):
                        before, dq = state[h]
                        g16 = g_s[h, kc]
                        g = g16.astype(F32)
                        beta = b_s[h, kc].astype(F32)
                        prefix = lax.dot_general(g16, earlier, (NN, ((), ())), preferred_element_type=F32) + before
                        dz = g * (1.0 - beta) - beta * prefix
                        if dchunk is not None:
                            dz = jnp.where(_diag_mask(dchunk, False), dz, 0.0)
                        dzb = dz.astype(BF16)
                        dkt_acc[kc] += _dot(qt_heads[h], dzb, NN)
                        out.append((before + jnp.sum(g, axis=1, keepdims=True), dq + _dot(dzb, k, NN)))
                    return tuple(out)

                start = (zero, jnp.zeros((BQ, PAIR), F32))
                (_, dq0), (_, dq1) = _walk_chunks(i, step2, (start, start), False)
                res[0, _qrows(i), :] = (jnp.where(lanes[0], dq0, dq1) * ATTN_SCALE).astype(BF16)
                return 0

            lax.fori_loop(0, t // BQ, qblock, 0)
            _flush_transposed(dkt_acc, res, 1)
            _flush_transposed(dvt_acc, res, 2)

        _emit_dqkv(res, o_ref)

    return pl.pallas_call(
        body, name=name, grid=(N_PAIRS, 3),
        in_specs=[_pair_spec(t, 0), _pair_spec(t, N_PAIRS), _pair_spec(t, 2 * N_PAIRS), _pair_spec(t, 0), _ANY],
        out_specs=pl.BlockSpec((t, PAIR), lambda p, s: (0, s * N_PAIRS + p)),
        out_shape=jax.ShapeDtypeStruct(dqkv.shape, BF16), input_output_aliases={4: 0},
        scratch_shapes=[pltpu.VMEM((2, nc, BQ, CS), BF16), pltpu.VMEM((2, nc, BQ, CS), BF16),
                        pltpu.VMEM((nc, PAIR, CS), F32), pltpu.VMEM((nc, PAIR, CS), F32), pltpu.VMEM((3, t, PAIR), BF16)],
        compiler_params=_params(("parallel", "arbitrary")),
    )(qkv, qkv, qkv, dy, dqkv)


def _gates_col_spec(t):
    return pl.BlockSpec((2, t, 1), lambda p, *_: (p, 0, 0))


def _gates_row_spec(nc):
    return pl.BlockSpec((2, nc, 1, CS), lambda p, *_: (p, 0, 0, 0))


def fox_pair_fwd(name, qkv, c_col, c_row):
    t = qkv.shape[0]

    def body(q_ref, k_ref, v_ref, cc_ref, cr_ref, o_ref, lse_ref):
        lanes = _head_lanes((BQ, PAIR))

        def qblock(i, _):
            q = q_ref[_qrows(i), :]
            q_heads = [_only_head(q, lanes[h]) for h in range(2)]
            ct = [cc_ref[h, _qrows(i), :] for h in range(2)]

            def step(kc, state, dchunk):
                k = k_ref[_krows(kc), :]
                v = v_ref[_krows(kc), :]
                out = []
                for h in range(2):
                    m, l, acc = state[h]
                    s = _dot(q_heads[h], k, NT) + ct[h] - cr_ref[h, kc]
                    if dchunk is not None:
                        s = jnp.where(_diag_mask(dchunk, True), s, NEG_BIG)
                    m_new = jnp.maximum(m, jnp.max(s, axis=1, keepdims=True))
                    alpha = jnp.exp(m - m_new)
                    p = jnp.exp(s - m_new)
                    if dchunk is not None:
                        p = jnp.where(_diag_mask(dchunk, True), p, 0.0)
                    out.append((m_new, l * alpha + jnp.sum(p, axis=1, keepdims=True), acc * alpha + _dot(p, v, NN)))
                return tuple(out)

            init = (jnp.full((BQ, 1), NEG_BIG, F32), jnp.zeros((BQ, 1), F32), jnp.zeros((BQ, PAIR), F32))
            (m0, l0, acc0), (m1, l1, acc1) = _walk_chunks(i, step, (init, init), False)
            o_ref[_qrows(i), :] = jnp.where(lanes[0], acc0 / l0, acc1 / l1).astype(BF16)
            lse_ref[0, _qrows(i), :] = m0 + jnp.log(l0)
            lse_ref[1, _qrows(i), :] = m1 + jnp.log(l1)
            return 0

        lax.fori_loop(0, t // BQ, qblock, 0)

    return pl.pallas_call(
        body, name=name, grid=(N_PAIRS,),
        in_specs=[_pair_spec(t, 3 * N_PAIRS), _pair_spec(t, 4 * N_PAIRS), _pair_spec(t, 5 * N_PAIRS),
                  _gates_col_spec(t), _gates_row_spec(t // CS)],
        out_specs=[_pair_spec(t, 0), _gates_col_spec(t)],
        out_shape=[jax.ShapeDtypeStruct((t, WIDTH), BF16), jax.ShapeDtypeStruct((N_HEADS, t, 1), F32)],
        compiler_params=_params(("parallel",)),
    )(qkv, qkv, qkv, c_col, c_row)


def fox_pair_bwd(name, qkv, y, dy, lse, c_col, c_row, dqkv):
    t = qkv.shape[0]
    nc = t // CS

    def body(q_ref, k_ref, v_ref, o_in_ref, do_ref, lse_ref, cc_ref, cr_ref, _, o_ref, dcc_ref, dcr_ref,
             dkt_acc, dvt_acc, dcr_acc, res):
        @pl.when(pl.program_id(1) == 0)
        def _():
            lanes = _head_lanes((BQ, PAIR))
            dkt_acc[...] = jnp.zeros_like(dkt_acc)
            dvt_acc[...] = jnp.zeros_like(dvt_acc)
            dcr_acc[...] = jnp.zeros_like(dcr_acc)

            def qblock(i, _):
                q = q_ref[_qrows(i), :]
                do = do_ref[_qrows(i), :]
                q_heads = [_only_head(q, lanes[h]) for h in range(2)]
                do_heads = [_only_head(do, lanes[h]) for h in range(2)]
                qt_heads = [qh.astype(F32).T.astype(BF16) for qh in q_heads]
                dot_heads = [dh.astype(F32).T.astype(BF16) for dh in do_heads]
                prod = do.astype(F32) * o_in_ref[_qrows(i), :].astype(F32)
                delta = [jnp.sum(_only_head(prod, lanes[h]), axis=1, keepdims=True) for h in range(2)]
                ct = [cc_ref[h, _qrows(i), :] for h in range(2)]
                lse_i = [lse_ref[h, _qrows(i), :] for h in range(2)]

                def step(kc, state, dchunk):
                    k = k_ref[_krows(kc), :]
                    v = v_ref[_krows(kc), :]
                    out = []
                    for h in range(2):
                        dq, dct = state[h]
                        s = _dot(q_heads[h], k, NT) + ct[h] - cr_ref[h, kc]
                        p = jnp.exp(s - lse_i[h])
                        if dchunk is not None:
                            p = jnp.where(_diag_mask(dchunk, True), p, 0.0)
                        ds = p * (_dot(do_heads[h], v, NT) - delta[h])
                        dvt_acc[kc] += _dot(dot_heads[h], p, NN)
                        dsb = ds.astype(BF16)
                        dkt_acc[kc] += _dot(qt_heads[h], dsb, NN)
                        dcr_acc[h, kc] -= jnp.sum(ds, axis=0, keepdims=True)
                        out.append((dq + _dot(dsb, k, NN), dct + jnp.sum(ds, axis=1, keepdims=True)))
                    return tuple(out)

                zero = (jnp.zeros((BQ, PAIR), F32), jnp.zeros((BQ, 1), F32))
                (dq0, dct0), (dq1, dct1) = _walk_chunks(i, step, (zero, zero), False)
                res[0, _qrows(i), :] = (jnp.where(lanes[0], dq0, dq1) * ATTN_SCALE).astype(BF16)
                dcc_ref[0, _qrows(i), :] = dct0
                dcc_ref[1, _qrows(i), :] = dct1
                return 0

            lax.fori_loop(0, t // BQ, qblock, 0)
            _flush_transposed(dkt_acc, res, 1)
            _flush_transposed(dvt_acc, res, 2)
            dcr_ref[...] = dcr_acc[...]

        _emit_dqkv(res, o_ref)

    return pl.pallas_call(
        body, name=name, grid=(N_PAIRS, 3),
        in_specs=[_pair_spec(t, 3 * N_PAIRS), _pair_spec(t, 4 * N_PAIRS), _pair_spec(t, 5 * N_PAIRS), _pair_spec(t, 0),
                  _pair_spec(t, 0), _gates_col_spec(t), _gates_col_spec(t), _gates_row_spec(nc), _ANY],
        out_specs=[pl.BlockSpec((t, PAIR), lambda p, s: (0, (3 + s) * N_PAIRS + p)), _gates_col_spec(t), _gates_row_spec(nc)],
        out_shape=[jax.ShapeDtypeStruct(dqkv.shape, BF16), jax.ShapeDtypeStruct((N_HEADS, t, 1), F32),
                   jax.ShapeDtypeStruct((N_HEADS, nc, 1, CS), F32)],
        input_output_aliases={8: 0},
        scratch_shapes=[pltpu.VMEM((nc, PAIR, CS), F32), pltpu.VMEM((nc, PAIR, CS), F32), pltpu.VMEM((2, nc, 1, CS), F32),
                        pltpu.VMEM((3, t, PAIR), BF16)],
        compiler_params=_params(("parallel", "arbitrary")),
    )(qkv, qkv, qkv, y, dy, lse, c_col, c_row, dqkv)


def loss_head(name, x, g, target):
    t, d = x.shape
    tr = _row_tile(t, 256)

    def body(x_ref, g_ref, t_ref, dx_ref, gn_ref, loss_ref):
        xv = x_ref[...]
        r = lax.rsqrt(jnp.mean(xv * xv, axis=-1, keepdims=True) + RMS_EPS)
        xhat = xv * r
        gv = g_ref[...]
        err = xhat * gv - t_ref[...]
        part_loss = 0.5 * jnp.sum(jnp.mean(err * err, axis=-1, keepdims=True), axis=0, keepdims=True)
        dy = err * (1.0 / d)
        dyg = dy * gv
        dx_ref[...] = r * (dyg - xhat * jnp.mean(dyg * xhat, axis=-1, keepdims=True))
        part_g = jnp.sum(dy * xhat, axis=0, keepdims=True)

        @pl.when(pl.program_id(0) == 0)
        def _():
            gn_ref[...] = part_g
            loss_ref[...] = part_loss

        @pl.when(pl.program_id(0) != 0)
        def _():
            gn_ref[...] += part_g
            loss_ref[...] += part_loss

    row = pl.BlockSpec((tr, d), lambda i: (i, 0))
    return pl.pallas_call(
        body, name=name, grid=(t // tr,),
        in_specs=[row, pl.BlockSpec((1, d), lambda i: (0, 0)), row],
        out_specs=[row, pl.BlockSpec((1, d), lambda i: (0, 0)), pl.BlockSpec((1, 1), lambda i: (0, 0))],
        out_shape=[jax.ShapeDtypeStruct((t, d), F32), jax.ShapeDtypeStruct((1, d), F32), jax.ShapeDtypeStruct((1, 1), F32)],
        compiler_params=_params(("arbitrary",)),
    )(x, g, target)


def _place():
    return lax.axis_index("x"), lax.axis_index("y"), lax.axis_index("c")


def _other_chips(x, y):
    return [(1 - x, y), (x, 1 - y), (1 - x, 1 - y)]


def _half(ref, c, rows):
    return ref.at[:, pl.ds(c * (rows // 2), rows // 2), :]


_ANY = pl.BlockSpec(memory_space=pl.ANY)


def gather_weights(name, bufs):
    n = len(bufs)

    def body(*refs):
        outs = refs[n:2 * n]
        send_sems, recv_sems = refs[2 * n:]
        x, y, c = _place()
        chips = _other_chips(x, y)
        me = 2 * x + y
        sibling = (x, y, 1 - c)
        first, passed = [], []
        for i in range(n):
            rows = outs[i].shape[2]
            mine = _half(outs[i].at[me], c, rows)
            for j, (qx, qy) in enumerate(chips):
                k = 6 * i + j
                rc = pltpu.make_async_remote_copy(
                    src_ref=mine, dst_ref=mine,
                    send_sem=send_sems.at[k], recv_sem=recv_sems.at[k], device_id=(qx, qy, c), device_id_type=MESH)
                rc.start()
                first.append(rc)
        for i in range(n):
            rows = outs[i].shape[2]
            for j, (qx, qy) in enumerate(chips):
                k = 6 * i + j
                block = _half(outs[i].at[2 * qx + qy], c, rows)
                pltpu.make_async_remote_copy(
                    src_ref=block, dst_ref=block, send_sem=send_sems.at[k], recv_sem=recv_sems.at[k],
                    device_id=(qx, qy, c), device_id_type=MESH).wait_recv()
                fw = pltpu.make_async_remote_copy(
                    src_ref=block, dst_ref=block, send_sem=send_sems.at[k + 3], recv_sem=recv_sems.at[k + 3],
                    device_id=sibling, device_id_type=MESH)
                fw.start()
                passed.append(fw)
        for i in range(n):
            rows = outs[i].shape[2]
            for j, (qx, qy) in enumerate(chips):
                k = 6 * i + j + 3
                block = _half(outs[i].at[2 * qx + qy], 1 - c, rows)
                pltpu.make_async_remote_copy(
                    src_ref=block, dst_ref=block, send_sem=send_sems.at[k], recv_sem=recv_sems.at[k],
                    device_id=sibling, device_id_type=MESH).wait_recv()
        for cp in first + passed:
            cp.wait_send()

    return pl.pallas_call(
        body, name=name, in_specs=[_ANY] * n, out_specs=[_ANY] * n,
        out_shape=[jax.ShapeDtypeStruct(b.shape, b.dtype) for b in bufs],
        input_output_aliases={i: i for i in range(n)},
        scratch_shapes=[pltpu.SemaphoreType.DMA((6 * n,)), pltpu.SemaphoreType.DMA((6 * n,))],
        compiler_params=pltpu.CompilerParams(has_side_effects=True),
    )(*bufs)


_HBM = pl.BlockSpec(memory_space=pltpu.HBM)
_SEM = pl.BlockSpec(memory_space=pltpu.SEMAPHORE)
_DATAFLOW = pltpu.SideEffectType.DATAFLOW_SIDE_EFFECTING


def _in_hbm(a):
    return pltpu.with_memory_space_constraint(a, pltpu.HBM)


def _gather_ici_copies(bufs, send_sems, recv_sems, arrivals):
    x, y, c = _place()
    me = 2 * x + y
    copies = []
    for i, buf in enumerate(bufs):
        rows = buf.shape[2]
        for j, (qx, qy) in enumerate(_other_chips(x, y)):
            block = _half(buf.at[2 * qx + qy if arrivals else me], c, rows)
            copies.append(pltpu.make_async_remote_copy(
                src_ref=block, dst_ref=block, send_sem=send_sems.at[3 * i + j], recv_sem=recv_sems.at[3 * i + j],
                device_id=(qx, qy, c), device_id_type=MESH))
    return copies


def gather_ici_start(name, bufs, after):
    n = len(bufs)

    def body(*refs):
        ins = refs[:n]
        send_sems, recv_sems = refs[n + 1], refs[n + 2]
        token = refs[-1]
        for send in _gather_ici_copies(ins, send_sems, recv_sems, False):
            send.start()
        token[...] = jnp.zeros_like(token)

    res = pl.pallas_call(
        body, name=name,
        out_shape=(pltpu.SemaphoreType.DMA((3 * n,)), pltpu.SemaphoreType.DMA((3 * n,)), *[pltpu.HBM(b.shape, b.dtype) for b in bufs],
                   jax.ShapeDtypeStruct((8, 128), F32)),
        in_specs=[_HBM] * n + [_ANY], out_specs=(_SEM, _SEM, *[_HBM] * n, pl.BlockSpec(memory_space=pltpu.VMEM)),
        input_output_aliases={i: 2 + i for i in range(n)},
        compiler_params=pltpu.CompilerParams(has_side_effects=_DATAFLOW),
    )(*[_in_hbm(b) for b in bufs], after)
    return res[0], res[1], list(res[2:2 + n]), res[-1]


def gather_ici_wait(name, send_sems, recv_sems, bufs, after):
    n = len(bufs)

    def body(*refs):
        ins = refs[:n]
        send_sems_ref, recv_sems_ref = refs[n], refs[n + 1]
        for send in _gather_ici_copies(ins, send_sems_ref, recv_sems_ref, False):
            send.wait_send()
        for recv in _gather_ici_copies(ins, send_sems_ref, recv_sems_ref, True):
            recv.wait_recv()

    return pl.pallas_call(
        body, name=name, out_shape=tuple(pltpu.HBM(b.shape, b.dtype) for b in bufs),
        in_specs=[_HBM] * n + [_SEM, _SEM, _ANY], out_specs=tuple([_HBM] * n),
        input_output_aliases={i: i for i in range(n)},
        compiler_params=pltpu.CompilerParams(has_side_effects=_DATAFLOW),
    )(*bufs, send_sems, recv_sems, after)


def gather_forward(name, bufs):
    n = len(bufs)

    def body(*refs):
        outs = refs[n:2 * n]
        send_sems, recv_sems = refs[2 * n:]
        x, y, c = _place()
        sibling = (x, y, 1 - c)
        sends = []
        for i in range(n):
            rows = outs[i].shape[2]
            for j, (qx, qy) in enumerate(_other_chips(x, y)):
                block = _half(outs[i].at[2 * qx + qy], c, rows)
                fw = pltpu.make_async_remote_copy(
                    src_ref=block, dst_ref=block, send_sem=send_sems.at[3 * i + j], recv_sem=recv_sems.at[3 * i + j],
                    device_id=sibling, device_id_type=MESH)
                fw.start()
                sends.append(fw)
        for i in range(n):
            rows = outs[i].shape[2]
            for j, (qx, qy) in enumerate(_other_chips(x, y)):
                block = _half(outs[i].at[2 * qx + qy], 1 - c, rows)
                pltpu.make_async_remote_copy(
                    src_ref=block, dst_ref=block, send_sem=send_sems.at[3 * i + j], recv_sem=recv_sems.at[3 * i + j],
                    device_id=sibling, device_id_type=MESH).wait_recv()
        for fw in sends:
            fw.wait_send()

    return pl.pallas_call(
        body, name=name, in_specs=[_ANY] * n, out_specs=[_ANY] * n,
        out_shape=[jax.ShapeDtypeStruct(b.shape, b.dtype) for b in bufs],
        input_output_aliases={i: i for i in range(n)},
        scratch_shapes=[pltpu.SemaphoreType.DMA((3 * n,)), pltpu.SemaphoreType.DMA((3 * n,))],
        compiler_params=pltpu.CompilerParams(has_side_effects=True),
    )(*bufs)


def _between_chips_copies(parts, lands, send_sems, recv_sems):
    x, y, c = _place()
    copies = []
    for i, (part, land) in enumerate(zip(parts, lands)):
        for j, (qx, qy) in enumerate(_other_chips(x, y)):
            copies.append(pltpu.make_async_remote_copy(
                src_ref=part.at[2 * qx + qy], dst_ref=land.at[j], send_sem=send_sems.at[3 * i + j], recv_sem=recv_sems.at[3 * i + j],
                device_id=(qx, qy, c), device_id_type=MESH))
    return copies


def between_chips_start(name, parts):
    n = len(parts)
    lands = [lax.empty((N_CHIPS - 1,) + p.shape[1:], p.dtype) for p in parts]

    def body(*refs):
        send_sems, recv_sems = refs[2 * n], refs[2 * n + 1]
        token = refs[-1]
        for cp in _between_chips_copies(refs[:n], refs[n:2 * n], send_sems, recv_sems):
            cp.start()
        token[...] = jnp.zeros_like(token)

    res = pl.pallas_call(
        body, name=name,
        out_shape=(pltpu.SemaphoreType.DMA((3 * n,)), pltpu.SemaphoreType.DMA((3 * n,)),
                   *[pltpu.HBM(a.shape, a.dtype) for a in parts + lands], jax.ShapeDtypeStruct((8, 128), F32)),
        in_specs=[_HBM] * (2 * n), out_specs=(_SEM, _SEM, *[_HBM] * (2 * n), pl.BlockSpec(memory_space=pltpu.VMEM)),
        input_output_aliases={i: 2 + i for i in range(2 * n)},
        compiler_params=pltpu.CompilerParams(has_side_effects=_DATAFLOW),
    )(*[_in_hbm(a) for a in parts + lands])
    return res[0], res[1], list(res[2:2 + n]), list(res[2 + n:2 + 2 * n]), res[-1]


def between_chips_wait(name, send_sems, recv_sems, parts, lands, after):
    n = len(parts)

    def body(*refs):
        for cp in _between_chips_copies(refs[:n], refs[n:2 * n], refs[2 * n], refs[2 * n + 1]):
            cp.wait_send()
            cp.wait_recv()

    res = pl.pallas_call(
        body, name=name, out_shape=tuple(pltpu.HBM(a.shape, a.dtype) for a in parts + lands),
        in_specs=[_HBM] * (2 * n) + [_SEM, _SEM, _ANY], out_specs=tuple([_HBM] * (2 * n)),
        input_output_aliases={i: i for i in range(2 * n)},
        compiler_params=pltpu.CompilerParams(has_side_effects=_DATAFLOW),
    )(*parts, *lands, send_sems, recv_sems, after)
    return list(res[:n]), list(res[n:])


def exchange_start(name, arrays, n_copies, copies, after=()):
    n = len(arrays)

    def body(*refs):
        send_sems, recv_sems = refs[n + len(after)], refs[n + len(after) + 1]
        for cp in copies(refs[:n], send_sems, recv_sems):
            cp.start()
        refs[-1][...] = jnp.zeros_like(refs[-1])

    res = pl.pallas_call(
        body, name=name,
        out_shape=(pltpu.SemaphoreType.DMA((n_copies,)), pltpu.SemaphoreType.DMA((n_copies,)),
                   *[pltpu.HBM(a.shape, a.dtype) for a in arrays], jax.ShapeDtypeStruct((8, 128), F32)),
        in_specs=[_HBM] * n + [_ANY] * len(after),
        out_specs=(_SEM, _SEM, *[_HBM] * n, pl.BlockSpec(memory_space=pltpu.VMEM)),
        input_output_aliases={i: 2 + i for i in range(n)},
        compiler_params=pltpu.CompilerParams(has_side_effects=_DATAFLOW),
    )(*[_in_hbm(a) for a in arrays], *after)
    return res[0], res[1], list(res[2:2 + n]), res[-1]


def exchange_wait(name, send_sems, recv_sems, arrays, copies, after):
    n = len(arrays)

    def body(*refs):
        for cp in copies(refs[:n], refs[n], refs[n + 1]):
            cp.wait_send()
            cp.wait_recv()

    return list(pl.pallas_call(
        body, name=name, out_shape=tuple(pltpu.HBM(a.shape, a.dtype) for a in arrays),
        in_specs=[_HBM] * n + [_SEM, _SEM, _ANY], out_specs=tuple([_HBM] * n),
        input_output_aliases={i: i for i in range(n)},
        compiler_params=pltpu.CompilerParams(has_side_effects=_DATAFLOW),
    )(*arrays, send_sems, recv_sems, after))


def _to_sibling_copies(n):
    def copies(refs, send_sems, recv_sems):
        x, y, c = _place()
        out = []
        for i in range(n):
            rows = refs[i].shape[2]
            out.append(pltpu.make_async_remote_copy(
                src_ref=refs[i].at[:, :, pl.ds((1 - c) * (rows // 2), rows // 2), :], dst_ref=refs[n + i],
                send_sem=send_sems.at[i], recv_sem=recv_sems.at[i], device_id=(x, y, 1 - c), device_id_type=MESH))
        return out
    return copies


def _share_copies(n):
    def copies(refs, send_sems, recv_sems):
        x, y, c = _place()
        out = []
        for i in range(n):
            mine = _half(refs[i], c, refs[i].shape[1])
            out.append(pltpu.make_async_remote_copy(
                src_ref=mine, dst_ref=mine, send_sem=send_sems.at[i], recv_sem=recv_sems.at[i],
                device_id=(x, y, 1 - c), device_id_type=MESH))
        return out
    return copies


def send_half_to_sibling(name, grads):
    n = len(grads)

    def body(*refs):
        srcs, outs = refs[:n], refs[n:2 * n]
        send_sems, recv_sems = refs[2 * n:]
        x, y, c = _place()
        sibling = (x, y, 1 - c)
        copies = []
        for i in range(n):
            rows = srcs[i].shape[2]
            rc = pltpu.make_async_remote_copy(
                src_ref=srcs[i].at[:, :, pl.ds((1 - c) * (rows // 2), rows // 2), :], dst_ref=outs[i],
                send_sem=send_sems.at[i], recv_sem=recv_sems.at[i], device_id=sibling, device_id_type=MESH)
            rc.start()
            copies.append(rc)
        for rc in copies:
            rc.wait()

    def half_shape(g):
        s = g.shape
        return jax.ShapeDtypeStruct((s[0], s[1], s[2] // 2, s[3]), g.dtype)

    return pl.pallas_call(
        body, name=name, in_specs=[_ANY] * n, out_specs=[_ANY] * n, out_shape=[half_shape(g) for g in grads],
        scratch_shapes=[pltpu.SemaphoreType.DMA((n,)), pltpu.SemaphoreType.DMA((n,))],
        compiler_params=pltpu.CompilerParams(has_side_effects=True),
    )(*grads)


def exchange_between_chips(name, parts):
    n = len(parts)

    def body(*refs):
        srcs, outs = refs[:n], refs[n:2 * n]
        send_sems, recv_sems = refs[2 * n:]
        x, y, c = _place()
        chips = _other_chips(x, y)
        copies = []
        for i in range(n):
            for j, (qx, qy) in enumerate(chips):
                k = 3 * i + j
                rc = pltpu.make_async_remote_copy(
                    src_ref=srcs[i].at[2 * qx + qy], dst_ref=outs[i].at[j],
                    send_sem=send_sems.at[k], recv_sem=recv_sems.at[k], device_id=(qx, qy, c), device_id_type=MESH)
                rc.start()
                copies.append(rc)
        for rc in copies:
            rc.wait()

    return pl.pallas_call(
        body, name=name, in_specs=[_ANY] * n, out_specs=[_ANY] * n,
        out_shape=[jax.ShapeDtypeStruct((N_CHIPS - 1,) + p.shape[1:], p.dtype) for p in parts],
        scratch_shapes=[pltpu.SemaphoreType.DMA((3 * n,)), pltpu.SemaphoreType.DMA((3 * n,))],
        compiler_params=pltpu.CompilerParams(has_side_effects=True),
    )(*parts)


def share_halves(name, bufs):
    n = len(bufs)

    def body(*refs):
        outs = refs[n:2 * n]
        send_sems, recv_sems = refs[2 * n:]
        x, y, c = _place()
        copies = []
        for i in range(n):
            mine = _half(outs[i], c, outs[i].shape[1])
            rc = pltpu.make_async_remote_copy(
                src_ref=mine, dst_ref=mine, send_sem=send_sems.at[i], recv_sem=recv_sems.at[i],
                device_id=(x, y, 1 - c), device_id_type=MESH)
            rc.start()
            copies.append(rc)
        for i in range(n):
            theirs = _half(outs[i], 1 - c, outs[i].shape[1])
            pltpu.make_async_remote_copy(
                src_ref=theirs, dst_ref=theirs, send_sem=send_sems.at[i], recv_sem=recv_sems.at[i],
                device_id=(x, y, 1 - c), device_id_type=MESH).wait_recv()
        for rc in copies:
            rc.wait_send()

    return pl.pallas_call(
        body, name=name, in_specs=[_ANY] * n, out_specs=[_ANY] * n,
        out_shape=[jax.ShapeDtypeStruct(b.shape, b.dtype) for b in bufs],
        input_output_aliases={i: i for i in range(n)},
        scratch_shapes=[pltpu.SemaphoreType.DMA((n,)), pltpu.SemaphoreType.DMA((n,))],
        compiler_params=pltpu.CompilerParams(has_side_effects=True),
    )(*bufs)


def pair_sum(name, grad, recv, c):
    ns, na, rh, cols = recv.shape
    tr = _row_tile(rh, 256) if rh % 256 == 0 else rh
    nt = rh // tr

    def body(c_ref, g_ref, r_ref, o_ref):
        o_ref[...] = (g_ref[...].astype(F32) + r_ref[...].astype(F32)).astype(BF16)

    blk = (None, None, tr, cols)
    return pl.pallas_call(
        body, name=name,
        grid_spec=pltpu.PrefetchScalarGridSpec(
            num_scalar_prefetch=1, grid=(ns, na, nt),
            in_specs=[pl.BlockSpec(blk, lambda s, a, r, c_ref: (s, a, c_ref[0] * nt + r, 0)),
                      pl.BlockSpec(blk, lambda s, a, r, c_ref: (s, a, r, 0))],
            out_specs=pl.BlockSpec(blk, lambda s, a, r, c_ref: (s, a, r, 0))),
        out_shape=jax.ShapeDtypeStruct(recv.shape, BF16),
        compiler_params=_params(("parallel", "parallel", "parallel")),
    )(c, grad, recv)


def chip_sum(name, parts, landed, place):
    _, na, rh, cols = parts.shape
    tr = _row_tile(rh, 256) if rh % 256 == 0 else rh
    nt = rh // tr

    def body(place_ref, p_ref, l_ref, o_ref):
        total = p_ref[...].astype(F32)
        for s in range(N_CHIPS - 1):
            total = total + l_ref[s].astype(F32)
        o_ref[...] = total

    return pl.pallas_call(
        body, name=name,
        grid_spec=pltpu.PrefetchScalarGridSpec(
            num_scalar_prefetch=1, grid=(na, nt),
            in_specs=[pl.BlockSpec((None, None, tr, cols), lambda a, r, pr: (pr[1], a, r, 0)),
                      pl.BlockSpec((N_CHIPS - 1, None, tr, cols), lambda a, r, pr: (0, a, r, 0))],
            out_specs=pl.BlockSpec((None, tr, cols), lambda a, r, pr: (a, pr[0] * nt + r, 0))),
        out_shape=jax.ShapeDtypeStruct((na, 2 * rh, cols), F32),
        compiler_params=_params(("parallel", "parallel")),
    )(place, parts, landed)


def reduce_scatter_1(tag, grads):
    n = len(grads)
    lands = [lax.empty((g.shape[0], g.shape[1], g.shape[2] // 2, g.shape[3]), g.dtype) for g in grads]
    send_sems, recv_sems, arrays, token = exchange_start(f"{tag}_to_sibling_start", list(grads) + lands, n, _to_sibling_copies(n))
    return (send_sems, recv_sems, arrays), token


def reduce_scatter_2(tag, state, place, after):
    send_sems, recv_sems, arrays = state
    n = len(arrays) // 2
    arrays = exchange_wait(f"{tag}_to_sibling_wait", send_sems, recv_sems, arrays, _to_sibling_copies(n), after)
    parts = [pair_sum(f"{tag}_pair_sum{i}", g, r, place) for i, (g, r) in enumerate(zip(arrays[:n], arrays[n:]))]
    send_sems, recv_sems, parts, lands, token = between_chips_start(f"{tag}_between_chips_start", parts)
    return (send_sems, recv_sems, parts, lands), token


def reduce_scatter_3(tag, state, place, after):
    send_sems, recv_sems, parts, lands = state
    parts, landed = between_chips_wait(f"{tag}_between_chips_wait", send_sems, recv_sems, parts, lands, after)
    halves = [chip_sum(f"{tag}_chip_sum{i}", p, l, place) for i, (p, l) in enumerate(zip(parts, landed))]
    send_sems, recv_sems, halves, token = exchange_start(f"{tag}_share_start", halves, len(halves), _share_copies(len(halves)))
    return (send_sems, recv_sems, halves), token


def reduce_scatter_4(tag, state, after):
    send_sems, recv_sems, halves = state
    return exchange_wait(f"{tag}_share_wait", send_sems, recv_sems, halves, _share_copies(len(halves)), after)


def _adamw_math(w, g, m, v):
    m = ADAM_B1 * m + (1.0 - ADAM_B1) * g
    v = ADAM_B2 * v + (1.0 - ADAM_B2) * (g * g)
    m_hat = m / (1.0 - ADAM_B1 ** ADAM_STEP)
    v_hat = v / (1.0 - ADAM_B2 ** ADAM_STEP)
    delta = -ADAM_LR * (m_hat / (jnp.sqrt(v_hat) + ADAM_EPS) + ADAM_WD * w)
    return delta, m, v


def adamw(name, w, g, m, v, after=()):
    rows, cols = w.shape
    tr = _row_tile(rows, 256) if rows % 256 == 0 else rows // 2

    def body(w_ref, g_ref, m_ref, v_ref, d_ref, mo_ref, vo_ref):
        d_ref[...], mo_ref[...], vo_ref[...] = _adamw_math(w_ref[...], g_ref[...], m_ref[...], v_ref[...])

    blk = pl.BlockSpec((tr, cols), lambda i: (i, 0))
    return pl.pallas_call(
        _ordered_after(body, 4, len(after)), name=name, grid=(rows // tr,), in_specs=[blk] * 4 + [_ANY] * len(after),
        out_specs=[blk] * 3, out_shape=[jax.ShapeDtypeStruct(w.shape, F32)] * 3, compiler_params=_params(("parallel",)),
    )(w, g, m, v, *after)


def adamw_rows(name, w, g, m, v, after=()):
    rows, _, cols = w.shape
    tr = next(r for r in (128, 110, 64, 32, 16, 8, 1) if rows % r == 0)

    def body(w_ref, g_ref, m_ref, v_ref, d_ref, mo_ref, vo_ref):
        d_ref[...], mo_ref[...], vo_ref[...] = _adamw_math(w_ref[...], g_ref[...], m_ref[...], v_ref[...])

    blk = pl.BlockSpec((tr, 1, cols), lambda i: (i, 0, 0))
    return pl.pallas_call(
        _ordered_after(body, 4, len(after)), name=name, grid=(rows // tr,), in_specs=[blk] * 4 + [_ANY] * len(after),
        out_specs=[blk] * 3, out_shape=[jax.ShapeDtypeStruct(w.shape, F32)] * 3, compiler_params=_params(("parallel",)),
    )(w, g, m, v, *after)


def adamw_stacked(name, ws, g, ms, vs, after=()):
    n = len(ws)
    rows, cols = ws[0].shape
    tr = next(r for r in (128, 88, 64, 32, 16, 8) if rows % r == 0)

    def body(*refs):
        w_refs, m_refs, v_refs, g_ref = refs[:n], refs[n:2 * n], refs[2 * n:3 * n], refs[3 * n]
        outs = refs[3 * n + 1:]
        for i in range(n):
            outs[i][...], outs[n + i][...], outs[2 * n + i][...] = _adamw_math(
                w_refs[i][...], g_ref[i], m_refs[i][...], v_refs[i][...])

    blk = pl.BlockSpec((tr, cols), lambda r: (r, 0))
    res = pl.pallas_call(
        _ordered_after(body, 3 * n + 1, len(after)), name=name, grid=(rows // tr,),
        in_specs=[blk] * (3 * n) + [pl.BlockSpec((n, tr, cols), lambda r: (0, r, 0))] + [_ANY] * len(after),
        out_specs=[blk] * (3 * n),
        out_shape=[jax.ShapeDtypeStruct((rows, cols), F32)] * (3 * n), compiler_params=_params(("parallel",)),
    )(*ws, *ms, *vs, g, *after)
    return res[:n], res[n:2 * n], res[2 * n:]


def small_allreduce_adamw(name, g_part, w, m, v):
    rows, cols = g_part.shape

    def body(g_ref, w_ref, m_ref, v_ref, sum_ref, d_ref, mo_ref, vo_ref, land, send_sems, recv_sems):
        x, y, c = _place()
        me = 4 * x + 2 * y + c
        land[me] = g_ref[...]
        copies = []
        for r in range(1, 8):
            peer = (x ^ (r >> 2), y ^ ((r >> 1) & 1), c ^ (r & 1))
            rc = pltpu.make_async_remote_copy(
                src_ref=g_ref, dst_ref=land.at[me], send_sem=send_sems.at[r - 1], recv_sem=recv_sems.at[r - 1],
                device_id=peer, device_id_type=MESH)
            rc.start()
            copies.append(rc)
        for rc in copies:
            rc.wait()
        total = land[0]
        for s in range(1, 8):
            total = total + land[s]
        sum_ref[...] = total
        d_ref[...], mo_ref[...], vo_ref[...] = _adamw_math(w_ref[...], total, m_ref[...], v_ref[...])

    vmem = pl.BlockSpec(memory_space=pltpu.VMEM)
    return pl.pallas_call(
        body, name=name, in_specs=[vmem] * 4, out_specs=[vmem] * 4,
        out_shape=[jax.ShapeDtypeStruct((rows, cols), F32)] * 4,
        scratch_shapes=[pltpu.VMEM((8, rows, cols), F32), pltpu.SemaphoreType.DMA((7,)), pltpu.SemaphoreType.DMA((7,))],
        compiler_params=pltpu.CompilerParams(has_side_effects=True),
    )(g_part, w, m, v)


def _heads(a):
    t, w = a.shape
    return a.reshape(t, w // HEAD_DIM, HEAD_DIM).transpose(1, 0, 2)


def _unheads(a):
    n, t, _ = a.shape
    return a.transpose(1, 0, 2).reshape(t, n * HEAD_DIM)


def _cols_from_shards(g):
    ns, r, cols = g.shape
    return g.transpose(1, 0, 2).reshape(r, ns * cols)


def _shards_from_cols(a):
    r, cols = a.shape
    return a.reshape(r, N_CHIPS, cols // N_CHIPS).transpose(1, 0, 2)


def kernel(x, norm_ffn1, w_ffn1_gate, w_ffn1_up, w_ffn1_down, norm_mix, w_in, b_forget, w_gate, b_gate, w_up_a, w_up_b, w_out, norm_ffn2, w_ffn2_gate, w_ffn2_up, w_ffn2_down, norm_final, loss_target, m_norm_ffn1, m_w_ffn1_gate, m_w_ffn1_up, m_w_ffn1_down, m_norm_mix, m_w_in, m_b_forget, m_w_gate, m_b_gate, m_w_up_a, m_w_up_b, m_w_out, m_norm_ffn2, m_w_ffn2_gate, m_w_ffn2_up, m_w_ffn2_down, m_norm_final, v_norm_ffn1, v_w_ffn1_gate, v_w_ffn1_up, v_w_ffn1_down, v_norm_mix, v_w_in, v_b_forget, v_w_gate, v_b_gate, v_w_up_a, v_w_up_b, v_w_out, v_norm_ffn2, v_w_ffn2_gate, v_w_ffn2_up, v_w_ffn2_down, v_norm_final):
    t, d = x.shape[1], x.shape[2]
    in4 = w_in.shape[2]
    gate4 = w_gate.shape[2]
    up4 = w_up_a.shape[2]
    in_cols = N_CHIPS * in4
    n_forget = in_cols - QKV_COLS
    assert w_up_a.shape[1] == WIDTH and d == 2 * WIDTH and n_forget == N_HEADS
    nq = t // BQ
    chip = 2 * lax.axis_index("x") + lax.axis_index("y")
    c_arr = jnp.stack([lax.axis_index("c"), chip]).astype(jnp.int32)
    x2d = x[0]
    target = loss_target[0]

    def slot(shard):
        return lax.dynamic_update_slice(lax.empty((N_CHIPS,) + shard.shape, BF16), shard.astype(BF16)[None], (chip, 0, 0, 0))

    def ffn_views(wg, wu, wd):
        return [wg[0].T, wu[0].T, wd[0]]

    ffn1_w, ffn1_m, ffn1_v = (ffn_views(w_ffn1_gate, w_ffn1_up, w_ffn1_down), ffn_views(m_w_ffn1_gate, m_w_ffn1_up, m_w_ffn1_down),
                              ffn_views(v_w_ffn1_gate, v_w_ffn1_up, v_w_ffn1_down))
    ffn2_w, ffn2_m, ffn2_v = (ffn_views(w_ffn2_gate, w_ffn2_up, w_ffn2_down), ffn_views(m_w_ffn2_gate, m_w_ffn2_up, m_w_ffn2_down),
                              ffn_views(v_w_ffn2_gate, v_w_ffn2_up, v_w_ffn2_down))
    in_pad = -(-in4 // 32) * 32
    ig_sh = slot(jnp.concatenate([jnp.pad(w_in[0].T, ((0, in_pad - in4), (0, 0))), w_gate[0].T], axis=0)[None])
    up_sh = slot(jnp.concatenate([w_up_a[0], w_up_b[0]], axis=0)[None])
    wo_sh = slot(w_out)
    f1_send, f1_recv, f1_bufs, f1_token = gather_ici_start("gather_ffn1_start", [slot(jnp.stack(ffn1_w))], norm_ffn1)
    mx_send, mx_recv, mx_bufs, mx_token = gather_ici_start("gather_mixer_start", [ig_sh, up_sh, wo_sh], f1_token)
    f2_send, f2_recv, f2_bufs, f2_token = gather_ici_start("gather_ffn2_start", [slot(jnp.stack(ffn2_w))], mx_token)

    normed1 = rms_fwd("ffn1_rms", x2d, norm_ffn1, after=(f2_token,))
    (w3_1,) = gather_forward("gather_ffn1_forward", gather_ici_wait("gather_ffn1_wait", f1_send, f1_recv, f1_bufs, normed1[0]))
    x1, saved1 = ffn_forward("ffn1", x2d, norm_ffn1, w3_1, normed=normed1)
    ig, wup, wo = gather_forward("gather_mixer_forward", gather_ici_wait("gather_mixer_wait", mx_send, mx_recv, mx_bufs, x1))
    wup = wup[:, 0]
    w_in_t = ig[:, 0, :in4].reshape(in_cols, d)
    wcat = jnp.concatenate([w_in_t[:QKV_COLS], ig[:, 0, in_pad:].reshape(2 * d, d), w_in_t[QKV_COLS:],
                            jnp.zeros((F_PAD - n_forget, d), BF16)], axis=0)
    bias_cat = jnp.concatenate([jnp.zeros((1, QKV_COLS), F32), b_gate, jnp.zeros((1, F_PAD), F32)], axis=1)
    f_off = QKV_COLS + 2 * d
    wo_full = wo.reshape(d, d)
    b_forget_row = jnp.pad(b_forget, ((0, 0), (0, QB - n_forget)))

    h2, rstd2 = rms_fwd("mix_rms", x1, norm_mix)
    qkv = proj("mix_proj_qkv", h2, wcat, bias_cat, 0, QKV_COLS, WIDTH, BF16, scaled_tiles=(0, 3))
    pc = proj("mix_proj_gates", h2, wcat, bias_cat, QKV_COLS, 2 * d + F_PAD, 768, F32)
    f_logit = pc[:, 2 * d:2 * d + QB]
    c_cum = fox_prep("fox_prep", f_logit, b_forget_row)
    c_heads = c_cum[:, :N_HEADS].T
    c_col = c_heads[:, :, None]
    c_row = c_heads.reshape(N_HEADS, t // CS, 1, CS)
    ya = sb_pair_fwd("sb_fwd", qkv)
    yb, lse = fox_pair_fwd("fox_fwd", qkv, c_col, c_row)
    ua, ub, mixed = mix_fwd("mix_fwd", ya, yb, wup, pc)
    x2 = mm_residual("mix_out", mixed[None], wo_full[None], pl.BlockSpec((1, d, d), lambda m: (0, 0, 0)), x1, 1.0)
    (w3_2,) = gather_forward("gather_ffn2_forward", gather_ici_wait("gather_ffn2_wait", f2_send, f2_recv, f2_bufs, x2))
    x3, saved2 = ffn_forward("ffn2", x2, norm_ffn2, w3_2)
    dx3, gn_final, loss_part = loss_head("loss_head", x3, norm_final[None], target)

    g_w3_2, dab2 = ffn_backward_weights("ffn2", dx3, saved2, w3_2)
    rs_ffn2, token = reduce_scatter_1("rs_ffn2", [g_w3_2])
    dx2, gn_ffn2 = ffn_backward_input("ffn2", dx3, saved2, dab2, norm_ffn2, w3_2, after=(token,))
    rs_ffn2, rs_ffn2_token = reduce_scatter_2("rs_ffn2", rs_ffn2, c_arr, dx2)

    dua, dub, dpa, dpb, gba, gbb = mix_bwd("mix_bwd", dx2, wo_full, pc, ua, ub, after=(rs_ffn2_token,))
    dgp = jnp.concatenate([dpa, dpb], axis=1)
    g_bgate = jnp.concatenate([gba, gbb], axis=1)
    g_wo = mm_plain("wgrad_out", mixed, dx2, TN, BF16, tk_target=1024)
    dya = up_bwd("dya", dua, wup, 0)
    dyb = up_bwd("dyb", dub, wup, 1)
    g_up = up_wgrad("wgrad_up", ya, yb, dua, dub)
    dqkv = sb_pair_bwd("sb_bwd", qkv, dya, lax.empty((t, QKV_COLS), BF16))
    dqkv, dcc, dcr = fox_pair_bwd("fox_bwd", qkv, yb, dyb, lse, c_col, c_row, dqkv)
    dc = (dcc[:, :, 0] + dcr.reshape(N_HEADS, t)).T
    df, g_bf = fox_gate_bwd("fox_gate_bwd", jnp.pad(dc, ((0, 0), (0, QB - N_HEADS))), f_logit, b_forget_row)
    dcat = jnp.concatenate([dqkv, dgp, df.astype(BF16), jnp.zeros((t, F_PAD - QB), BF16)], axis=1)
    g_wcat = wgrad_cat("wgrad_cat", h2, dcat)
    g_in_t = jnp.concatenate([g_wcat[:QKV_COLS], g_wcat[f_off:f_off + n_forget]], axis=0).reshape(N_CHIPS, in4, d)
    g_ig = jnp.concatenate([jnp.pad(g_in_t, ((0, 0), (0, in_pad - in4), (0, 0))),
                            g_wcat[QKV_COLS:f_off].reshape(N_CHIPS, gate4, d)], axis=1)[:, None]
    rs_mixer, token = reduce_scatter_1("rs_mixer", [g_ig, g_up[:, None], g_wo.reshape(N_CHIPS, 1, d // N_CHIPS, d)])
    tm = _row_tile(t, 512)
    tkc = 1792
    nkc = wcat.shape[0] // tkc
    dx1, gn_mix = dh_rms_bwd(
        "mix_dh", [(dcat, wcat)],
        [(pl.BlockSpec((tm, tkc), lambda m, k: (m, k)), pl.BlockSpec((tkc, d), lambda m, k: (k, 0)))], NN,
        (t // tm, nkc), nkc, x1, rstd2, norm_mix, dx2, after=(token,))
    rs_mixer, rs_mixer_token = reduce_scatter_2("rs_mixer", rs_mixer, c_arr, dx1)
    rs_ffn2, rs_ffn2_token = reduce_scatter_3("rs_ffn2", rs_ffn2, c_arr, dx1)

    g_w3_1, dab1 = ffn_backward_weights("ffn1", dx1, saved1, w3_1, after=(rs_mixer_token, rs_ffn2_token))
    rs_ffn1, token = reduce_scatter_1("rs_ffn1", [g_w3_1])
    dx0, gn_ffn1 = ffn_backward_input("ffn1", dx1, saved1, dab1, norm_ffn1, w3_1, after=(token,))
    rs_ffn1, rs_ffn1_token = reduce_scatter_2("rs_ffn1", rs_ffn1, c_arr, dx0)
    rs_mixer, rs_mixer_token = reduce_scatter_3("rs_mixer", rs_mixer, c_arr, dx0)
    (s_w3_2,) = reduce_scatter_4("rs_ffn2", rs_ffn2, dx0)

    def pack_small(n1, nm, n2, nf, bg, bf, last):
        return jnp.concatenate([n1, nm, n2, nf, bg.reshape(2, d), jnp.pad(bf, ((0, 0), (0, d - n_forget))), last], axis=0)

    zero_row = jnp.zeros((1, d), F32)
    g_small = pack_small(gn_ffn1, gn_mix, gn_ffn2, gn_final, g_bgate, g_bf[:, :n_forget], jnp.pad(loss_part, ((0, 0), (0, d - 1))))
    w_small = pack_small(norm_ffn1, norm_mix, norm_ffn2, norm_final[None], b_gate, b_forget, zero_row)
    m_small = pack_small(m_norm_ffn1, m_norm_mix, m_norm_ffn2, m_norm_final[None], m_b_gate, m_b_forget, zero_row)
    v_small = pack_small(v_norm_ffn1, v_norm_mix, v_norm_ffn2, v_norm_final[None], v_b_gate, v_b_forget, zero_row)
    smalls = small_allreduce_adamw("small_allreduce_adamw", g_small, w_small, m_small, v_small)

    def unpack_small(p):
        return {"norm_ffn1": p[0:1], "norm_mix": p[1:2], "norm_ffn2": p[2:3], "norm_final": p[3], "b_gate": p[4:6].reshape(1, 2 * d),
                "b_forget": p[6:7, :n_forget]}

    loss = smalls[0][7, 0]
    small_out = [unpack_small(p) for p in smalls]

    big_out = {}

    def adamw_ffn(tag, s_w3, ws, ms, vs, after):
        deltas, new_ms, new_vs = adamw_stacked(f"adamw_{tag}", ws, s_w3, ms, vs, after)
        for which, part in ((GATE, "gate"), (UP, "up"), (DOWN, "down")):
            back = (lambda a: a[None]) if which == DOWN else (lambda a: a.T[None])
            big_out[f"w_{tag}_{part}"] = tuple(back(a) for a in (s_w3[which], deltas[which], new_ms[which], new_vs[which]))
        return deltas[DOWN]

    last = adamw_ffn("ffn2", s_w3_2, ffn2_w, ffn2_m, ffn2_v, (rs_ffn1_token, rs_mixer_token))
    s_ig, s_up, s_wo = reduce_scatter_4("rs_mixer", rs_mixer, last)
    grads = {
        "w_gate": s_ig[0, in_pad:].T, "w_up_a": s_up[0, :WIDTH], "w_up_b": s_up[0, WIDTH:], "w_out": s_wo[0],
    }
    weights = {"w_gate": (w_gate, m_w_gate, v_w_gate), "w_up_a": (w_up_a, m_w_up_a, v_w_up_a),
               "w_up_b": (w_up_b, m_w_up_b, v_w_up_b), "w_out": (w_out, m_w_out, v_w_out)}
    for wname, (w, m, v) in weights.items():
        g = grads[wname]
        delta, new_m, new_v = adamw(f"adamw_{wname}", w[0], g, m[0], v[0])
        big_out[wname] = (g[None], delta[None], new_m[None], new_v[None])
    rows_of = lambda a: jnp.transpose(a, (2, 0, 1))
    g_in_rows = s_ig[0, :in4][:, None, :]
    in_rows = adamw_rows("adamw_w_in", rows_of(w_in), g_in_rows, rows_of(m_w_in), rows_of(v_w_in))
    big_out["w_in"] = tuple(jnp.transpose(a, (1, 2, 0)) for a in (g_in_rows, *in_rows))

    rs_ffn1, token = reduce_scatter_3("rs_ffn1", rs_ffn1, c_arr, in_rows[0])
    (s_w3_1,) = reduce_scatter_4("rs_ffn1", rs_ffn1, token)
    adamw_ffn("ffn1", s_w3_1, ffn1_w, ffn1_m, ffn1_v, ())

    order = ["norm_ffn1", "w_ffn1_gate", "w_ffn1_up", "w_ffn1_down", "norm_mix", "w_in", "b_forget", "w_gate", "b_gate",
             "w_up_a", "w_up_b", "w_out", "norm_ffn2", "w_ffn2_gate", "w_ffn2_up", "w_ffn2_down", "norm_final"]
    outs = [loss, dx0[None]]
    for kind in range(4):
        for wname in order:
            outs.append(big_out[wname][kind] if wname in big_out else small_out[kind][wname])
    return tuple(outs)
```

```python
import functools

import jax
import jax.numpy as jnp
from jax import lax
from jax.experimental import pallas as pl
from jax.experimental.pallas import tpu as pltpu

F32 = jnp.float32
BF16 = jnp.bfloat16

HEAD_DIM = 64
N_HEADS = 8
WIDTH = N_HEADS * HEAD_DIM
QKV_COLS = 6 * WIDTH
RMS_EPS = 1e-6
ATTN_SCALE = HEAD_DIM ** -0.5
N_CHIPS = 4
QB = 128
BQ = 512
CS = 256
N_SUB = BQ // CS
F_PAD = 256
NEG_BIG = -1e30

ADAM_LR = 0.001
ADAM_B1 = 0.9
ADAM_B2 = 0.999
ADAM_EPS = 1e-08
ADAM_WD = 0.01
ADAM_STEP = 10

VMEM_LIMIT_BYTES = 48 * 1024 * 1024
MESH = pl.DeviceIdType.MESH

NN = ((1,), (0,))
NT = ((1,), (1,))
TN = ((0,), (0,))


def _params(semantics):
    return pltpu.CompilerParams(dimension_semantics=semantics, vmem_limit_bytes=VMEM_LIMIT_BYTES)


def _dot(a, b, contract):
    return lax.dot_general(a.astype(BF16), b.astype(BF16), (contract, ((), ())), preferred_element_type=F32)


def _sigmoid(x):
    return 1.0 / (1.0 + jnp.exp(-x))


def _log1pexp_neg_abs(z):
    return jnp.log(1.0 + jnp.exp(-jnp.abs(z)))


def _split3(x):
    hi = x.astype(BF16)
    r1 = x - hi.astype(F32)
    mid = r1.astype(BF16)
    lo = (r1 - mid.astype(F32)).astype(BF16)
    return hi, mid, lo


def _dot_exact_rhs01(x, m01):
    hi, mid, lo = _split3(x)
    d = lambda p: lax.dot_general(p, m01, (NN, ((), ())), preferred_element_type=F32)
    return d(hi) + d(mid) + d(lo)


def _dot_exact_lhs01(m01, x):
    hi, mid, lo = _split3(x)
    d = lambda p: lax.dot_general(m01, p, (NN, ((), ())), preferred_element_type=F32)
    return d(hi) + d(mid) + d(lo)


def _iota2(shape, dim):
    return lax.broadcasted_iota(jnp.int32, shape, dim)


def _mm(name, pairs, contract, grid, pair_specs, out_shape, out_specs, acc_shape, nk, epilogue,
        extras=(), extra_specs=(), semantics=None):
    n_pairs = len(pairs)
    n_extra = len(extras)
    n_out = len(out_shape)

    def body(*refs):
        ab = refs[:2 * n_pairs]
        ex = refs[2 * n_pairs:2 * n_pairs + n_extra]
        outs = refs[2 * n_pairs + n_extra:2 * n_pairs + n_extra + n_out]
        ids = [pl.program_id(i) for i in range(len(grid))]
        k = ids[-1]
        part = _dot(ab[0][...], ab[1][...], contract)
        for p in range(1, n_pairs):
            part += _dot(ab[2 * p][...], ab[2 * p + 1][...], contract)
        if nk == 1:
            epilogue(part, ex, outs, ids)
            return
        acc = refs[-1]

        @pl.when(k == 0)
        def _():
            acc[...] = part

        @pl.when(k != 0)
        def _():
            acc[...] += part

        @pl.when(k == nk - 1)
        def _():
            epilogue(acc[...], ex, outs, ids)

    operands = [t for pair in pairs for t in pair] + list(extras)
    in_specs = [s for pair in pair_specs for s in pair] + list(extra_specs)
    if semantics is None:
        semantics = ("parallel",) * (len(grid) - 1) + ("arbitrary",)
    return pl.pallas_call(
        body, name=name, grid=grid, in_specs=in_specs, out_specs=list(out_specs), out_shape=list(out_shape),
        scratch_shapes=[] if nk == 1 else [pltpu.VMEM(acc_shape, F32)], compiler_params=_params(semantics),
    )(*operands)


def _ordered_after(body, n_in, n_after):
    def wrapped(*refs):
        return body(*refs[:n_in], *refs[n_in + n_after:])
    return wrapped


def _row_halves(rows):
    return (slice(0, rows // 2), slice(rows // 2, rows)) if rows % 32 == 0 else (slice(0, rows),)


def _row_tile(rows, target):
    t = min(rows, target)
    while rows % t:
        t //= 2
    return t


def rms_fwd(name, x, g, after=()):
    t, d = x.shape
    tr = _row_tile(t, 256)

    def body(x_ref, g_ref, h_ref, r_ref):
        xv = x_ref[...]
        r = lax.rsqrt(jnp.mean(xv * xv, axis=-1, keepdims=True) + RMS_EPS)
        h_ref[...] = (xv * r * g_ref[...]).astype(BF16)
        r_ref[...] = r

    return pl.pallas_call(
        _ordered_after(body, 2, len(after)), name=name, grid=(t // tr,),
        in_specs=[pl.BlockSpec((tr, d), lambda i: (i, 0)), pl.BlockSpec((1, d), lambda i: (0, 0))] + [_ANY] * len(after),
        out_specs=[pl.BlockSpec((tr, d), lambda i: (i, 0)), pl.BlockSpec((tr, 1), lambda i: (i, 0))],
        out_shape=[jax.ShapeDtypeStruct((t, d), BF16), jax.ShapeDtypeStruct((t, 1), F32)],
        compiler_params=_params(("parallel",)),
    )(x, g, *after)


GATE, UP, DOWN = 0, 1, 2


def _ffn_w_spec(which, f4, d, index_of_j):
    return pl.BlockSpec((None, None, f4, d), lambda *ids: (index_of_j(*ids), which, 0, 0))


def ffn_up(name, h, w3):
    t, d = h.shape
    ns, _, f4, _ = w3.shape
    tm = _row_tile(t, 512)

    def body(h_ref, wg_ref, wu_ref, a_ref, b_ref, s_ref):
        for rows in _row_halves(tm):
            hv = h_ref[rows, :]
            a = _dot(hv, wg_ref[...], NT)
            b = _dot(hv, wu_ref[...], NT)
            a_ref[rows, :] = a.astype(BF16)
            b_ref[rows, :] = b.astype(BF16)
            s_ref[rows, :] = (a * _sigmoid(a) * b).astype(BF16)

    act_spec = pl.BlockSpec((None, tm, f4), lambda j, m: (j, m, 0))
    return pl.pallas_call(
        body, name=name, grid=(ns, t // tm),
        in_specs=[pl.BlockSpec((tm, d), lambda j, m: (m, 0)),
                  _ffn_w_spec(GATE, f4, d, lambda j, m: j), _ffn_w_spec(UP, f4, d, lambda j, m: j)],
        out_specs=[act_spec, act_spec, act_spec],
        out_shape=[jax.ShapeDtypeStruct((ns, t, f4), BF16)] * 3,
        compiler_params=_params(("parallel", "parallel")),
    )(h, w3, w3)


def mm_residual(name, s, w, w_spec, x, scale):
    nj, t, kdim = s.shape
    n = x.shape[1]
    tm = _row_tile(t, 512)

    def body(s_ref, w_ref, x_ref, o_ref):
        acc = _dot(s_ref[0], w_ref[0], NN)
        for j in range(1, nj):
            acc += _dot(s_ref[j], w_ref[j], NN)
        o_ref[...] = x_ref[...] + scale * acc

    row = pl.BlockSpec((tm, n), lambda m: (m, 0))
    return pl.pallas_call(
        body, name=name, grid=(t // tm,),
        in_specs=[pl.BlockSpec((nj, tm, kdim), lambda m: (0, m, 0)), w_spec, row], out_specs=row,
        out_shape=jax.ShapeDtypeStruct((t, n), F32), compiler_params=_params(("parallel",)),
    )(s, w, x)


def ffn_bwd_act(name, dx, w3, a, b, after=()):
    t, d = dx.shape
    ns, _, f4, _ = w3.shape
    tm = _row_tile(t, 512)

    def body(dx_ref, wd_ref, a_ref, b_ref, da_ref, db_ref):
        for rows in _row_halves(tm):
            ds = _dot(0.5 * dx_ref[rows, :], wd_ref[...], NT)
            av = a_ref[rows, :].astype(F32)
            sig = _sigmoid(av)
            da_ref[rows, :] = (ds * b_ref[rows, :].astype(F32) * (sig * (1.0 + av * (1.0 - sig)))).astype(BF16)
            db_ref[rows, :] = (ds * (av * sig)).astype(BF16)

    act_spec = pl.BlockSpec((None, tm, f4), lambda j, m: (j, m, 0))
    return pl.pallas_call(
        _ordered_after(body, 4, len(after)), name=name, grid=(ns, t // tm),
        in_specs=[pl.BlockSpec((tm, d), lambda j, m: (m, 0)), _ffn_w_spec(DOWN, f4, d, lambda j, m: j), act_spec, act_spec]
        + [_ANY] * len(after),
        out_specs=[act_spec, act_spec],
        out_shape=[jax.ShapeDtypeStruct((ns, t, f4), BF16)] * 2,
        compiler_params=_params(("parallel", "parallel")),
    )(dx, w3, a, b, *after)


def ffn_wgrad(name, h, da, db, s, dx):
    t, d = h.shape
    ns, _, f4 = da.shape
    tk = _row_tile(t, 1024)
    nk = t // tk

    def body(h_ref, da_ref, db_ref, s_ref, dx_ref, o_ref, acc):
        k = pl.program_id(1)

        @pl.when(k == 0)
        def _():
            acc[...] = jnp.zeros_like(acc)

        hv = h_ref[...]
        acc[GATE] += _dot(da_ref[...], hv, TN)
        acc[UP] += _dot(db_ref[...], hv, TN)
        acc[DOWN] += _dot(s_ref[...], 0.5 * dx_ref[...], TN)

        @pl.when(k == nk - 1)
        def _():
            o_ref[...] = acc[...].astype(BF16)

    act_spec = pl.BlockSpec((None, tk, f4), lambda j, k: (j, k, 0))
    row_spec = pl.BlockSpec((tk, d), lambda j, k: (k, 0))
    return pl.pallas_call(
        body, name=name, grid=(ns, nk),
        in_specs=[row_spec, act_spec, act_spec, act_spec, row_spec],
        out_specs=pl.BlockSpec((None, 3, f4, d), lambda j, k: (j, 0, 0, 0)),
        out_shape=jax.ShapeDtypeStruct((ns, 3, f4, d), BF16),
        scratch_shapes=[pltpu.VMEM((3, f4, d), F32)],
        compiler_params=_params(("parallel", "arbitrary")),
    )(h, da, db, s, dx)


def _rms_bwd_tail(dh, x_ref, r_ref, g_ref, dxin_ref, dx_ref, gn_ref, row_tile_index):
    r = r_ref[...]
    xhat = x_ref[...] * r
    dhg = dh * g_ref[...]
    dx_ref[...] = dxin_ref[...] + r * (dhg - xhat * jnp.mean(dhg * xhat, axis=-1, keepdims=True))
    part = jnp.sum(dh * xhat, axis=0, keepdims=True)

    @pl.when(row_tile_index == 0)
    def _():
        gn_ref[...] = part

    @pl.when(row_tile_index != 0)
    def _():
        gn_ref[...] += part


def ffn_dh(name, da, db, w3, x, rstd, g, dx_in, after=()):
    ns, t, f4 = da.shape
    d = x.shape[1]
    tm = _row_tile(t, 256)

    def body(da_ref, db_ref, wg_ref, wu_ref, x_ref, r_ref, g_ref, dxin_ref, dx_ref, gn_ref):
        dh = _dot(da_ref[0], wg_ref[0], NN) + _dot(db_ref[0], wu_ref[0], NN)
        for j in range(1, ns):
            dh += _dot(da_ref[j], wg_ref[j], NN) + _dot(db_ref[j], wu_ref[j], NN)
        _rms_bwd_tail(dh, x_ref, r_ref, g_ref, dxin_ref, dx_ref, gn_ref, pl.program_id(0))

    act = pl.BlockSpec((ns, tm, f4), lambda m: (0, m, 0))
    row = pl.BlockSpec((tm, d), lambda m: (m, 0))
    gain = pl.BlockSpec((1, d), lambda m: (0, 0))
    return pl.pallas_call(
        _ordered_after(body, 8, len(after)), name=name, grid=(t // tm,),
        in_specs=[act, act, pl.BlockSpec((ns, None, f4, d), lambda m: (0, GATE, 0, 0)),
                  pl.BlockSpec((ns, None, f4, d), lambda m: (0, UP, 0, 0)), row, pl.BlockSpec((tm, 1), lambda m: (m, 0)), gain, row]
        + [_ANY] * len(after),
        out_specs=[row, gain], out_shape=[jax.ShapeDtypeStruct((t, d), F32), jax.ShapeDtypeStruct((1, d), F32)],
        compiler_params=_params(("arbitrary",)),
    )(da, db, w3, w3, x, rstd, g, dx_in, *after)


def dh_rms_bwd(name, pairs, pair_specs, contract, grid, nk, x, rstd, g, dx_in, after=()):
    t, d = x.shape
    tm = t // grid[0]

    def epilogue(acc, ex, outs, ids):
        _rms_bwd_tail(acc, *ex[:4], *outs, ids[0])

    row = pl.BlockSpec((tm, d), lambda m, k: (m, 0))
    return _mm(
        name, pairs, contract, grid, pair_specs,
        [jax.ShapeDtypeStruct((t, d), F32), jax.ShapeDtypeStruct((1, d), F32)],
        [row, pl.BlockSpec((1, d), lambda m, k: (0, 0))], (tm, d), nk, epilogue,
        extras=[x, rstd, g, dx_in, *after],
        extra_specs=[row, pl.BlockSpec((tm, 1), lambda m, k: (m, 0)), pl.BlockSpec((1, d), lambda m, k: (0, 0)), row]
        + [_ANY] * len(after),
        semantics=("arbitrary", "arbitrary"),
    )


def ffn_forward(tag, x, g_norm, w3, normed=None):
    _, _, f4, d = w3.shape
    h, rstd = normed if normed is not None else rms_fwd(f"{tag}_rms", x, g_norm)
    a, b, s = ffn_up(f"{tag}_up", h, w3)
    x_out = mm_residual(f"{tag}_down", s, w3, pl.BlockSpec((w3.shape[0], None, f4, d), lambda m: (0, DOWN, 0, 0)), x, 0.5)
    return x_out, (x, h, rstd, a, b, s)


def ffn_backward_weights(tag, dx, saved, w3, after=()):
    x, h, rstd, a, b, s = saved
    da, db = ffn_bwd_act(f"{tag}_bwd_act", dx, w3, a, b, after)
    return ffn_wgrad(f"{tag}_wgrad", h, da, db, s, dx), (da, db)


def ffn_backward_input(tag, dx, saved, dab, g_norm, w3, after=()):
    x, h, rstd, a, b, s = saved
    return ffn_dh(f"{tag}_dh", dab[0], dab[1], w3, x, rstd, g_norm, dx, after)


def proj(name, h, wcat_t, bias, first_col, n_cols, tn, out_dtype, scaled_tiles=()):
    t, d = h.shape
    tm = _row_tile(t, 512)
    off = first_col // tn

    def epilogue(acc, ex, outs, ids):
        val = acc + ex[0][...]
        if scaled_tiles:
            hit = functools.reduce(jnp.logical_or, [ids[0] == s for s in scaled_tiles])
            val = val * jnp.where(hit, ATTN_SCALE, 1.0)
        outs[0][...] = val.astype(out_dtype)

    return _mm(
        name, [(h, wcat_t)], NT, (n_cols // tn, t // tm, 1),
        [(pl.BlockSpec((tm, d), lambda j, m, k: (m, 0)), pl.BlockSpec((tn, d), lambda j, m, k: (off + j, 0)))],
        [jax.ShapeDtypeStruct((t, n_cols), out_dtype)], [pl.BlockSpec((tm, tn), lambda j, m, k: (m, j))], (tm, tn), 1, epilogue,
        extras=[bias], extra_specs=[pl.BlockSpec((1, tn), lambda j, m, k: (0, off + j))],
    )[0]


def mix_fwd(name, ya, yb, wup, pc):
    t, w = ya.shape
    ns, _, tn = wup.shape
    d = ns * tn
    tm = _row_tile(t, 512)

    def body(ya_ref, yb_ref, wa_ref, wb_ref, pa_ref, pb_ref, ua_ref, ub_ref, mx_ref):
        ua = _dot(ya_ref[...], wa_ref[...], NN)
        ub = _dot(yb_ref[...], wb_ref[...], NN)
        ua_ref[...] = ua
        ub_ref[...] = ub
        mx_ref[...] = (_sigmoid(pa_ref[...]) * ua + _sigmoid(pb_ref[...]) * ub).astype(BF16)

    y_spec = pl.BlockSpec((tm, w), lambda m, n: (m, 0))
    o_spec = pl.BlockSpec((tm, tn), lambda m, n: (m, n))
    return pl.pallas_call(
        body, name=name, grid=(t // tm, ns),
        in_specs=[y_spec, y_spec, pl.BlockSpec((None, w, tn), lambda m, n: (n, 0, 0)), pl.BlockSpec((None, w, tn), lambda m, n: (n, 1, 0)),
                  o_spec, pl.BlockSpec((tm, tn), lambda m, n: (m, ns + n))],
        out_specs=[o_spec, o_spec, o_spec],
        out_shape=[jax.ShapeDtypeStruct((t, d), F32), jax.ShapeDtypeStruct((t, d), F32), jax.ShapeDtypeStruct((t, d), BF16)],
        compiler_params=_params(("parallel", "parallel")),
    )(ya, yb, wup, wup, pc, pc)


def up_bwd(name, du, wup, branch):
    t, d = du.shape
    ns, w2, tn = wup.shape
    w = w2 // 2
    tm = _row_tile(t, 512)

    def body(du_ref, w_ref, o_ref):
        acc = _dot(du_ref[:, 0:tn], w_ref[0], NT)
        for j in range(1, ns):
            acc += _dot(du_ref[:, j * tn:(j + 1) * tn], w_ref[j], NT)
        o_ref[...] = acc.astype(BF16)

    return pl.pallas_call(
        body, name=name, grid=(t // tm,),
        in_specs=[pl.BlockSpec((tm, d), lambda m: (m, 0)), pl.BlockSpec((ns, w, tn), lambda m: (0, branch, 0))],
        out_specs=pl.BlockSpec((tm, w), lambda m: (m, 0)), out_shape=jax.ShapeDtypeStruct((t, w), BF16),
        compiler_params=_params(("parallel",)),
    )(du, wup)


def up_wgrad(name, ya, yb, dua, dub):
    t, w = ya.shape
    d = dua.shape[1]
    tn = d // N_CHIPS

    def body(ya_ref, yb_ref, dua_ref, dub_ref, o_ref):
        o_ref[0:w, :] = _dot(ya_ref[...], dua_ref[...], TN).astype(BF16)
        o_ref[w:2 * w, :] = _dot(yb_ref[...], dub_ref[...], TN).astype(BF16)

    y_spec = pl.BlockSpec((t, w), lambda j: (0, 0))
    du_spec = pl.BlockSpec((t, tn), lambda j: (0, j))
    return pl.pallas_call(
        body, name=name, grid=(N_CHIPS,), in_specs=[y_spec, y_spec, du_spec, du_spec],
        out_specs=pl.BlockSpec((None, 2 * w, tn), lambda j: (j, 0, 0)),
        out_shape=jax.ShapeDtypeStruct((N_CHIPS, 2 * w, tn), BF16), compiler_params=_params(("parallel",)),
    )(ya, yb, dua, dub)


def mix_bwd(name, dx, wo, pc, ua, ub, after=()):
    t, d = dx.shape
    tm = _row_tile(t, 512)
    tn = 512
    off_a = 0
    off_b = d // tn

    def body(dx_ref, wo_ref, pa_ref, pb_ref, ua_ref, ub_ref, dua_ref, dub_ref, dpa_ref, dpb_ref, ba_ref, bb_ref):
        dm = _dot(dx_ref[...], wo_ref[...], NT)
        ga = _sigmoid(pa_ref[...])
        gb = _sigmoid(pb_ref[...])
        dua_ref[...] = (dm * ga).astype(BF16)
        dub_ref[...] = (dm * gb).astype(BF16)
        dpa = dm * ua_ref[...] * ga * (1.0 - ga)
        dpb = dm * ub_ref[...] * gb * (1.0 - gb)
        dpa_ref[...] = dpa.astype(BF16)
        dpb_ref[...] = dpb.astype(BF16)
        sa = jnp.sum(dpa, axis=0, keepdims=True)
        sb = jnp.sum(dpb, axis=0, keepdims=True)

        @pl.when(pl.program_id(1) == 0)
        def _():
            ba_ref[...] = sa
            bb_ref[...] = sb

        @pl.when(pl.program_id(1) != 0)
        def _():
            ba_ref[...] += sa
            bb_ref[...] += sb

    tile = pl.BlockSpec((tm, tn), lambda n, m: (m, n))
    bias = pl.BlockSpec((1, tn), lambda n, m: (0, n))
    return pl.pallas_call(
        _ordered_after(body, 6, len(after)), name=name, grid=(d // tn, t // tm),
        in_specs=[pl.BlockSpec((tm, d), lambda n, m: (m, 0)), pl.BlockSpec((tn, d), lambda n, m: (n, 0)),
                  pl.BlockSpec((tm, tn), lambda n, m: (m, off_a + n)), pl.BlockSpec((tm, tn), lambda n, m: (m, off_b + n)),
                  tile, tile] + [_ANY] * len(after),
        out_specs=[tile, tile, tile, tile, bias, bias],
        out_shape=[jax.ShapeDtypeStruct((t, d), BF16)] * 4 + [jax.ShapeDtypeStruct((1, d), F32)] * 2,
        compiler_params=_params(("parallel", "arbitrary")),
    )(dx, wo, pc, pc, ua, ub, *after)


def mm_plain(name, a, b, contract, out_dtype, tk_target=512):
    if contract == NN:
        m, kdim = a.shape
        n = b.shape[1]
    elif contract == NT:
        m, kdim = a.shape
        n = b.shape[0]
    else:
        kdim, m = a.shape
        n = b.shape[1]
    tm = _row_tile(m, 512)
    tk = _row_tile(kdim, tk_target)
    nk = kdim // tk
    if contract == TN:
        a_spec = pl.BlockSpec((tk, tm), lambda i, k: (k, i))
    else:
        a_spec = pl.BlockSpec((tm, tk), lambda i, k: (i, k))
    if contract == NT:
        b_spec = pl.BlockSpec((n, tk), lambda i, k: (0, k))
    else:
        b_spec = pl.BlockSpec((tk, n), lambda i, k: (k, 0))

    def epilogue(acc, ex, outs, ids):
        outs[0][...] = acc.astype(out_dtype)

    return _mm(name, [(a, b)], contract, (m // tm, nk), [(a_spec, b_spec)],
               [jax.ShapeDtypeStruct((m, n), out_dtype)], [pl.BlockSpec((tm, n), lambda i, k: (i, 0))], (tm, n), nk, epilogue)[0]


def wgrad_cat(name, h, dcat):
    t, d = h.shape
    n = dcat.shape[1]
    tn = 768
    tk = _row_tile(t, 2048)

    def epilogue(acc, ex, outs, ids):
        outs[0][...] = acc.astype(BF16)

    return _mm(
        name, [(dcat, h)], TN, (n // tn, t // tk),
        [(pl.BlockSpec((tk, tn), lambda j, k: (k, j)), pl.BlockSpec((tk, d), lambda j, k: (k, 0)))],
        [jax.ShapeDtypeStruct((n, d), BF16)], [pl.BlockSpec((tn, d), lambda j, k: (j, 0))], (tn, d), t // tk, epilogue,
    )[0]


def fox_prep(name, f, bias):
    t, lanes = f.shape
    nchunk = t // QB

    def body(f_ref, b_ref, c_ref):
        lower = (_iota2((QB, QB), 1) <= _iota2((QB, QB), 0)).astype(BF16)

        def chunk(n, carry):
            rows = pl.ds(pl.multiple_of(n * QB, QB), QB)
            u = f_ref[rows, :] + b_ref[...]
            lf = jnp.minimum(u, 0.0) - _log1pexp_neg_abs(u)
            c = _dot_exact_lhs01(lower, lf) + carry
            c_ref[rows, :] = c
            return c[QB - 1:QB, :]

        lax.fori_loop(0, nchunk, chunk, jnp.zeros((1, lanes), F32))

    return pl.pallas_call(body, name=name, out_shape=jax.ShapeDtypeStruct((t, lanes), F32),
                          compiler_params=pltpu.CompilerParams(vmem_limit_bytes=VMEM_LIMIT_BYTES))(f, bias)


def fox_gate_bwd(name, dc, f, bias):
    t, lanes = dc.shape
    nchunk = t // QB

    def body(dc_ref, f_ref, b_ref, df_ref, gb_ref):
        upper = (_iota2((QB, QB), 1) >= _iota2((QB, QB), 0)).astype(BF16)

        def chunk(n, carry):
            tail, total = carry
            rows = pl.ds(pl.multiple_of((nchunk - 1 - n) * QB, QB), QB)
            dlf = _dot_exact_lhs01(upper, dc_ref[rows, :]) + tail
            u = f_ref[rows, :] + b_ref[...]
            df = dlf * jnp.exp(jnp.minimum(-u, 0.0) - _log1pexp_neg_abs(u))
            df_ref[rows, :] = df
            return dlf[0:1, :], total + jnp.sum(df, axis=0, keepdims=True)

        zero = jnp.zeros((1, lanes), F32)
        _, total = lax.fori_loop(0, nchunk, chunk, (zero, zero))
        gb_ref[...] = total

    return pl.pallas_call(body, name=name,
                          out_shape=[jax.ShapeDtypeStruct((t, lanes), F32), jax.ShapeDtypeStruct((1, lanes), F32)],
                          compiler_params=pltpu.CompilerParams(vmem_limit_bytes=VMEM_LIMIT_BYTES))(dc, f, bias)


def _qrows(i):
    return pl.ds(pl.multiple_of(i * BQ, BQ), BQ)


def _krows(kc):
    return pl.ds(pl.multiple_of(kc * CS, CS), CS)


def _head_spec(t, offset):
    return pl.BlockSpec((None, t, HEAD_DIM), lambda h: (offset + h, 0, 0))


def _head_t_spec(nq, offset):
    return pl.BlockSpec((None, nq, HEAD_DIM, BQ), lambda h: (offset + h, 0, 0, 0))


def _chunk_t_spec(nc):
    return pl.BlockSpec((None, nc, HEAD_DIM, CS), lambda h: (h, 0, 0, 0))


def _dot_split2_rhs01(x, m01):
    hi = x.astype(BF16)
    lo = (x - hi.astype(F32)).astype(BF16)
    d = lambda p: lax.dot_general(p, m01, (NN, ((), ())), preferred_element_type=F32)
    return d(hi) + d(lo)


def _live_rows(dchunk):
    return 0 if dchunk is None else dchunk * CS


def _diag_mask(dchunk, inclusive):
    shape = (BQ - _live_rows(dchunk), CS)
    return _iota2(shape, 1) <= _iota2(shape, 0) if inclusive else _iota2(shape, 1) < _iota2(shape, 0)


def _tail(x, r0, axis=0):
    return x if r0 == 0 else (x[r0:] if axis == 0 else x[:, r0:])


def _with_tail(old, tail, r0):
    return tail if r0 == 0 else jnp.concatenate([old[:r0], tail], axis=0)


def _walk_chunks(i, step, init, right_to_left):
    order = list(reversed(range(N_SUB))) if right_to_left else list(range(N_SUB))

    def diagonal(state):
        for dchunk in order:
            state = step(i * N_SUB + dchunk, state, dchunk)
        return state

    def group(n, state):
        base = ((i - 1 - n) if right_to_left else n) * N_SUB
        for dchunk in order:
            state = step(base + dchunk, state, None)
        return state

    if right_to_left:
        return lax.fori_loop(0, i, group, diagonal(init))
    return diagonal(lax.fori_loop(0, i, group, init))


def sb_fwd(name, qkv):
    t = qkv.shape[1]

    def body(q_ref, k_ref, v_ref, o_ref):
        later = (_iota2((CS, CS), 0) > _iota2((CS, CS), 1)).astype(BF16)

        def qblock(i, _):
            q = q_ref[_qrows(i), :]

            def step(kc, state, dchunk):
                carry, acc = state
                z = _dot(q, k_ref[_krows(kc), :], NT)
                sp = _log1pexp_neg_abs(z)
                lnb = -jnp.maximum(z, 0.0) - sp
                if dchunk is not None:
                    lnb = jnp.where(_diag_mask(dchunk, False), lnb, 0.0)
                w = jnp.exp(jnp.minimum(z, 0.0) - sp + _dot_split2_rhs01(lnb, later) + carry)
                if dchunk is not None:
                    w = jnp.where(_diag_mask(dchunk, False), w, 0.0)
                acc = acc + _dot(w, v_ref[_krows(kc), :], NN)
                return carry + jnp.sum(lnb, axis=1, keepdims=True), acc

            init = (jnp.zeros((BQ, 1), F32), jnp.zeros((BQ, HEAD_DIM), F32))
            _, acc = _walk_chunks(i, step, init, True)
            o_ref[_qrows(i), :] = acc.astype(BF16)
            return 0

        lax.fori_loop(0, t // BQ, qblock, 0)

    return pl.pallas_call(
        body, name=name, grid=(N_HEADS,),
        in_specs=[_head_spec(t, 0), _head_spec(t, N_HEADS), _head_spec(t, 2 * N_HEADS)],
        out_specs=_head_spec(t, 0), out_shape=jax.ShapeDtypeStruct((N_HEADS, t, HEAD_DIM), BF16),
        compiler_params=_params(("parallel",)),
    )(qkv, qkv, qkv)


def sb_bwd(name, qkv, qt, dy, dyt):
    t = qkv.shape[1]
    nq, nc = t // BQ, t // CS

    def body(q_ref, k_ref, v_ref, qt_ref, do_ref, dot_ref, dq_ref, dkt_ref, dvt_ref, g_s, b_s, dkt_acc, dvt_acc):
        later = (_iota2((CS, CS), 0) > _iota2((CS, CS), 1)).astype(BF16)
        earlier = (_iota2((CS, CS), 0) < _iota2((CS, CS), 1)).astype(BF16)
        dkt_acc[...] = jnp.zeros_like(dkt_acc)
        dvt_acc[...] = jnp.zeros_like(dvt_acc)

        def qblock(i, _):
            q = q_ref[_qrows(i), :]
            do = do_ref[_qrows(i), :]
            q_t = qt_ref[i]
            do_t = dot_ref[i]

            def step1(kc, carry, dchunk):
                z = _dot(q, k_ref[_krows(kc), :], NT)
                sp = _log1pexp_neg_abs(z)
                lnb = -jnp.maximum(z, 0.0) - sp
                lsz = jnp.minimum(z, 0.0) - sp
                if dchunk is not None:
                    lnb = jnp.where(_diag_mask(dchunk, False), lnb, 0.0)
                w = jnp.exp(lsz + _dot_split2_rhs01(lnb, later) + carry)
                if dchunk is not None:
                    w = jnp.where(_diag_mask(dchunk, False), w, 0.0)
                g_s[kc] = w * _dot(do, v_ref[_krows(kc), :], NT)
                b_s[kc] = jnp.exp(lsz)
                dvt_acc[kc] += _dot(do_t, w, NN)
                return carry + jnp.sum(lnb, axis=1, keepdims=True)

            _walk_chunks(i, step1, jnp.zeros((BQ, 1), F32), True)

            def step2(kc, state, dchunk):
                before, dq = state
                g = g_s[kc]
                beta = b_s[kc]
                dz = g * (1.0 - beta) - beta * (_dot_split2_rhs01(g, earlier) + before)
                if dchunk is not None:
                    dz = jnp.where(_diag_mask(dchunk, False), dz, 0.0)
                dzb = dz.astype(BF16)
                dq = dq + _dot(dzb, k_ref[_krows(kc), :], NN)
                dkt_acc[kc] += _dot(q_t, dzb, NN)
                return before + jnp.sum(g, axis=1, keepdims=True), dq

            _, dq = _walk_chunks(i, step2, (jnp.zeros((BQ, 1), F32), jnp.zeros((BQ, HEAD_DIM), F32)), False)
            dq_ref[_qrows(i), :] = (dq * ATTN_SCALE).astype(BF16)
            return 0

        lax.fori_loop(0, nq, qblock, 0)
        dkt_ref[...] = dkt_acc[...].astype(BF16)
        dvt_ref[...] = dvt_acc[...].astype(BF16)

    chunked = jax.ShapeDtypeStruct((N_HEADS, nc, HEAD_DIM, CS), BF16)
    return pl.pallas_call(
        body, name=name, grid=(N_HEADS,),
        in_specs=[_head_spec(t, 0), _head_spec(t, N_HEADS), _head_spec(t, 2 * N_HEADS), _head_t_spec(nq, 0),
                  _head_spec(t, 0), _head_t_spec(nq, 0)],
        out_specs=[_head_spec(t, 0), _chunk_t_spec(nc), _chunk_t_spec(nc)],
        out_shape=[jax.ShapeDtypeStruct((N_HEADS, t, HEAD_DIM), BF16), chunked, chunked],
        scratch_shapes=[pltpu.VMEM((nc, BQ, CS), F32), pltpu.VMEM((nc, BQ, CS), F32),
                        pltpu.VMEM((nc, HEAD_DIM, CS), F32), pltpu.VMEM((nc, HEAD_DIM, CS), F32)],
        compiler_params=_params(("parallel",)),
    )(qkv, qkv, qkv, qt, dy, dyt)


def _col_spec(t):
    return pl.BlockSpec((None, t, 1), lambda h: (h, 0, 0))


def _row_spec(nc):
    return pl.BlockSpec((None, nc, 1, CS), lambda h: (h, 0, 0, 0))


def fox_fwd(name, qkv, c_col, c_row):
    t = qkv.shape[1]

    def body(q_ref, k_ref, v_ref, cc_ref, cr_ref, o_ref, lse_ref):
        def qblock(i, _):
            q = q_ref[_qrows(i), :]
            ct = cc_ref[_qrows(i), :]

            def step(kc, state, dchunk):
                m, l, acc = state
                s = _dot(q, k_ref[_krows(kc), :], NT) + ct - cr_ref[kc]
                if dchunk is not None:
                    s = jnp.where(_diag_mask(dchunk, True), s, NEG_BIG)
                m_new = jnp.maximum(m, jnp.max(s, axis=1, keepdims=True))
                alpha = jnp.exp(m - m_new)
                p = jnp.exp(s - m_new)
                if dchunk is not None:
                    p = jnp.where(_diag_mask(dchunk, True), p, 0.0)
                l = l * alpha + jnp.sum(p, axis=1, keepdims=True)
                acc = acc * alpha + _dot(p, v_ref[_krows(kc), :], NN)
                return m_new, l, acc

            init = (jnp.full((BQ, 1), NEG_BIG, F32), jnp.zeros((BQ, 1), F32), jnp.zeros((BQ, HEAD_DIM), F32))
            m, l, acc = _walk_chunks(i, step, init, False)
            o_ref[_qrows(i), :] = (acc / l).astype(BF16)
            lse_ref[_qrows(i), :] = m + jnp.log(l)
            return 0

        lax.fori_loop(0, t // BQ, qblock, 0)

    return pl.pallas_call(
        body, name=name, grid=(N_HEADS,),
        in_specs=[_head_spec(t, 3 * N_HEADS), _head_spec(t, 4 * N_HEADS), _head_spec(t, 5 * N_HEADS),
                  _col_spec(t), _row_spec(t // CS)],
        out_specs=[_head_spec(t, 0), _col_spec(t)],
        out_shape=[jax.ShapeDtypeStruct((N_HEADS, t, HEAD_DIM), BF16), jax.ShapeDtypeStruct((N_HEADS, t, 1), F32)],
        compiler_params=_params(("parallel",)),
    )(qkv, qkv, qkv, c_col, c_row)


def fox_bwd(name, qkv, qt, y, dy, dyt, lse, c_col, c_row):
    t = qkv.shape[1]
    nq, nc = t // BQ, t // CS

    def body(q_ref, k_ref, v_ref, qt_ref, o_ref, do_ref, dot_ref, lse_ref, cc_ref, cr_ref,
             dq_ref, dkt_ref, dvt_ref, dcc_ref, dcr_ref, dkt_acc, dvt_acc, dcr_acc):
        dkt_acc[...] = jnp.zeros_like(dkt_acc)
        dvt_acc[...] = jnp.zeros_like(dvt_acc)
        dcr_acc[...] = jnp.zeros_like(dcr_acc)

        def qblock(i, _):
            q = q_ref[_qrows(i), :]
            do = do_ref[_qrows(i), :]
            q_t = qt_ref[i]
            do_t = dot_ref[i]
            ct = cc_ref[_qrows(i), :]
            lse_i = lse_ref[_qrows(i), :]
            delta = jnp.sum(do.astype(F32) * o_ref[_qrows(i), :].astype(F32), axis=1, keepdims=True)

            def step(kc, state, dchunk):
                dq, dct = state
                s = _dot(q, k_ref[_krows(kc), :], NT) + ct - cr_ref[kc]
                p = jnp.exp(s - lse_i)
                if dchunk is not None:
                    p = jnp.where(_diag_mask(dchunk, True), p, 0.0)
                ds = p * (_dot(do, v_ref[_krows(kc), :], NT) - delta)
                dvt_acc[kc] += _dot(do_t, p, NN)
                dsb = ds.astype(BF16)
                dq = dq + _dot(dsb, k_ref[_krows(kc), :], NN)
                dkt_acc[kc] += _dot(q_t, dsb, NN)
                dcr_acc[kc] -= jnp.sum(ds, axis=0, keepdims=True)
                return dq, dct + jnp.sum(ds, axis=1, keepdims=True)

            dq, dct = _walk_chunks(i, step, (jnp.zeros((BQ, HEAD_DIM), F32), jnp.zeros((BQ, 1), F32)), False)
            dq_ref[_qrows(i), :] = (dq * ATTN_SCALE).astype(BF16)
            dcc_ref[_qrows(i), :] = dct
            return 0

        lax.fori_loop(0, nq, qblock, 0)
        dkt_ref[...] = dkt_acc[...].astype(BF16)
        dvt_ref[...] = dvt_acc[...].astype(BF16)
        dcr_ref[...] = dcr_acc[...]

    chunked = jax.ShapeDtypeStruct((N_HEADS, nc, HEAD_DIM, CS), BF16)
    return pl.pallas_call(
        body, name=name, grid=(N_HEADS,),
        in_specs=[_head_spec(t, 3 * N_HEADS), _head_spec(t, 4 * N_HEADS), _head_spec(t, 5 * N_HEADS), _head_t_spec(nq, N_HEADS),
                  _head_spec(t, 0), _head_spec(t, 0), _head_t_spec(nq, 0), _col_spec(t), _col_spec(t), _row_spec(nc)],
        out_specs=[_head_spec(t, 0), _chunk_t_spec(nc), _chunk_t_spec(nc), _col_spec(t), _row_spec(nc)],
        out_shape=[jax.ShapeDtypeStruct((N_HEADS, t, HEAD_DIM), BF16), chunked, chunked,
                   jax.ShapeDtypeStruct((N_HEADS, t, 1), F32), jax.ShapeDtypeStruct((N_HEADS, nc, 1, CS), F32)],
        scratch_shapes=[pltpu.VMEM((nc, HEAD_DIM, CS), F32), pltpu.VMEM((nc, HEAD_DIM, CS), F32), pltpu.VMEM((nc, 1, CS), F32)],
        compiler_params=_params(("parallel",)),
    )(qkv, qkv, qkv, qt, y, dy, dyt, lse, c_col, c_row)


PAIR = 2 * HEAD_DIM
N_PAIRS = N_HEADS // 2


def _pair_spec(t, first_block):
    return pl.BlockSpec((t, PAIR), lambda p, *_: (0, first_block + p))


def _head_lanes(shape):
    lane = _iota2(shape, len(shape) - 1)
    return [lane < HEAD_DIM, lane >= HEAD_DIM]


def _only_head(x, lanes_of_head):
    return jnp.where(lanes_of_head, x, jnp.zeros_like(x))


LOG2_E = 1.4426950408889634


def _sb_chunk_weights(q_h, k, later, carry, dchunk):
    z = _dot(q_h, k, NT) * LOG2_E
    lnb = -jnp.maximum(z, 0.0) - jnp.log2(1.0 + jnp.exp2(-jnp.abs(z)))
    lsz = lnb + z
    if dchunk is not None:
        lnb = jnp.where(_diag_mask(dchunk, False), lnb, 0.0)
    w = jnp.exp2(lsz + _dot_split2_rhs01(lnb, later) + carry)
    if dchunk is not None:
        w = jnp.where(_diag_mask(dchunk, False), w, 0.0)
    return w, lsz, lnb


def sb_pair_fwd(name, qkv):
    t = qkv.shape[0]

    def body(q_ref, k_ref, v_ref, o_ref):
        later = (_iota2((CS, CS), 0) > _iota2((CS, CS), 1)).astype(BF16)
        lanes = _head_lanes((BQ, PAIR))

        def qblock(i, _):
            q = q_ref[_qrows(i), :]
            q_heads = [_only_head(q, lanes[h]) for h in range(2)]

            def step(kc, state, dchunk):
                k = k_ref[_krows(kc), :]
                v = v_ref[_krows(kc), :]
                out = []
                r0 = _live_rows(dchunk)
                for h in range(2):
                    carry, acc = state[h]
                    w, _, lnb = _sb_chunk_weights(_tail(q_heads[h], r0), k, later, _tail(carry, r0), dchunk)
                    out.append((_with_tail(carry, _tail(carry, r0) + jnp.sum(lnb, axis=1, keepdims=True), r0),
                                _with_tail(acc, _tail(acc, r0) + _dot(w, v, NN), r0)))
                return tuple(out)

            zero = (jnp.zeros((BQ, 1), F32), jnp.zeros((BQ, PAIR), F32))
            (_, acc0), (_, acc1) = _walk_chunks(i, step, (zero, zero), True)
            o_ref[_qrows(i), :] = jnp.where(lanes[0], acc0, acc1).astype(BF16)
            return 0

        lax.fori_loop(0, t // BQ, qblock, 0)

    return pl.pallas_call(
        body, name=name, grid=(N_PAIRS,),
        in_specs=[_pair_spec(t, 0), _pair_spec(t, N_PAIRS), _pair_spec(t, 2 * N_PAIRS)],
        out_specs=_pair_spec(t, 0), out_shape=jax.ShapeDtypeStruct((t, WIDTH), BF16),
        compiler_params=_params(("parallel",)),
    )(qkv, qkv, qkv)


def _emit_dqkv(res, o_ref):
    o_ref[...] = res[pl.program_id(1)]


def _flush_transposed(acc, res, which):
    for kc in range(acc.shape[0]):
        res[which, kc * CS:(kc + 1) * CS, :] = acc[kc].T.astype(BF16)


def sb_pair_bwd(name, qkv, dy, dqkv):
    t = qkv.shape[0]
    nc = t // CS

    def body(q_ref, k_ref, v_ref, do_ref, _, o_ref, g_s, b_s, dkt_acc, dvt_acc, res):
        @pl.when(pl.program_id(1) == 0)
        def _():
            later = (_iota2((CS, CS), 0) > _iota2((CS, CS), 1)).astype(BF16)
            earlier = (_iota2((CS, CS), 0) < _iota2((CS, CS), 1)).astype(BF16)
            lanes = _head_lanes((BQ, PAIR))
            dkt_acc[...] = jnp.zeros_like(dkt_acc)
            dvt_acc[...] = jnp.zeros_like(dvt_acc)

            def qblock(i, _):
                q = q_ref[_qrows(i), :]
                do = do_ref[_qrows(i), :]
                q_heads = [_only_head(q, lanes[h]) for h in range(2)]
                do_heads = [_only_head(do, lanes[h]) for h in range(2)]
                qt_heads = [qh.astype(F32).T.astype(BF16) for qh in q_heads]
                dot_heads = [dh.astype(F32).T.astype(BF16) for dh in do_heads]

                def step1(kc, carries, dchunk):
                    k = k_ref[_krows(kc), :]
                    v = v_ref[_krows(kc), :]
                    out = []
                    r0 = _live_rows(dchunk)
                    for h in range(2):
                        carry = _tail(carries[h], r0)
                        w, lsz, lnb = _sb_chunk_weights(_tail(q_heads[h], r0), k, later, carry, dchunk)
                        g_s[h, kc, r0:, :] = (w * _dot(_tail(do_heads[h], r0), v, NT)).astype(BF16)
                        b_s[h, kc, r0:, :] = jnp.exp2(lsz).astype(BF16)
                        dvt_acc[kc] += _dot(_tail(dot_heads[h], r0, axis=1), w, NN)
                        out.append(_with_tail(carries[h], carry + jnp.sum(lnb, axis=1, keepdims=True), r0))
                    return tuple(out)

                zero = jnp.zeros((BQ, 1), F32)
                _walk_chunks(i, step1, (zero, zero), True)

                def step2(kc, state, dchunk):
                    k = k_ref[_krows(kc), :]
                    out = []
                    r0 = _live_rows(dchunk)
                    for h in range(2):
                        before, dq = state[h]
                        g16 = g_s[h, kc, r0:, :]
                        g = g16.astype(F32)
                        beta = b_s[h, kc, r0:, :].astype(F32)
                        prefix = lax.dot_general(g16, earlier, (NN, ((), ())), preferred_element_type=F32) + _tail(before, r0)
                        dz = g * (1.0 - beta) - beta * prefix
                        if dchunk is not None:
                            dz = jnp.where(_diag_mask(dchunk, False), dz, 0.0)
                        dzb = dz.astype(BF16)
                        dkt_acc[kc] += _dot(_tail(qt_heads[h], r0, axis=1), dzb, NN)
                        out.append((_with_tail(before, _tail(before, r0) + jnp.sum(g, axis=1, keepdims=True), r0),
                                    _with_tail(dq, _tail(dq, r0) + _dot(dzb, k, NN), r0)))
                    return tuple(out)

                start = (zero, jnp.zeros((BQ, PAIR), F32))
                (_, dq0), (_, dq1) = _walk_chunks(i, step2, (start, start), False)
                res[0, _qrows(i), :] = (jnp.where(lanes[0], dq0, dq1) * ATTN_SCALE).astype(BF16)
                return 0

            lax.fori_loop(0, t // BQ, qblock, 0)
            _flush_transposed(dkt_acc, res, 1)
            _flush_transposed(dvt_acc, res, 2)

        _emit_dqkv(res, o_ref)

    return pl.pallas_call(
        body, name=name, grid=(N_PAIRS, 3),
        in_specs=[_pair_spec(t, 0), _pair_spec(t, N_PAIRS), _pair_spec(t, 2 * N_PAIRS), _pair_spec(t, 0), _ANY],
        out_specs=pl.BlockSpec((t, PAIR), lambda p, s: (0, s * N_PAIRS + p)),
        out_shape=jax.ShapeDtypeStruct(dqkv.shape, BF16), input_output_aliases={4: 0},
        scratch_shapes=[pltpu.VMEM((2, nc, BQ, CS), BF16), pltpu.VMEM((2, nc, BQ, CS), BF16),
                        pltpu.VMEM((nc, PAIR, CS), F32), pltpu.VMEM((nc, PAIR, CS), F32), pltpu.VMEM((3, t, PAIR), BF16)],
        compiler_params=_params(("parallel", "arbitrary")),
    )(qkv, qkv, qkv, dy, dqkv)


def _gates_col_spec(t):
    return pl.BlockSpec((2, t, 1), lambda p, *_: (p, 0, 0))


def _gates_row_spec(nc):
    return pl.BlockSpec((2, nc, 1, CS), lambda p, *_: (p, 0, 0, 0))


def fox_pair_fwd(name, qkv, c_col, c_row):
    t = qkv.shape[0]

    def body(q_ref, k_ref, v_ref, cc_ref, cr_ref, o_ref, lse_ref):
        lanes = _head_lanes((BQ, PAIR))

        def qblock(i, _):
            q = q_ref[_qrows(i), :]
            q_heads = [_only_head(q, lanes[h]) for h in range(2)]
            ct = [cc_ref[h, _qrows(i), :] for h in range(2)]

            def step(kc, state, dchunk):
                k = k_ref[_krows(kc), :]
                v = v_ref[_krows(kc), :]
                out = []
                r0 = _live_rows(dchunk)
                for h in range(2):
                    m, l, acc = (_tail(a, r0) for a in state[h])
                    s = _dot(_tail(q_heads[h], r0), k, NT) + _tail(ct[h], r0) - cr_ref[h, kc]
                    if dchunk is not None:
                        s = jnp.where(_diag_mask(dchunk, True), s, NEG_BIG)
                    m_new = jnp.maximum(m, jnp.max(s, axis=1, keepdims=True))
                    alpha = jnp.exp(m - m_new)
                    p = jnp.exp(s - m_new)
                    if dchunk is not None:
                        p = jnp.where(_diag_mask(dchunk, True), p, 0.0)
                    new = (m_new, l * alpha + jnp.sum(p, axis=1, keepdims=True), acc * alpha + _dot(p, v, NN))
                    out.append(tuple(_with_tail(old, tail, r0) for old, tail in zip(state[h], new)))
                return tuple(out)

            init = (jnp.full((BQ, 1), NEG_BIG, F32), jnp.zeros((BQ, 1), F32), jnp.zeros((BQ, PAIR), F32))
            (m0, l0, acc0), (m1, l1, acc1) = _walk_chunks(i, step, (init, init), False)
            o_ref[_qrows(i), :] = jnp.where(lanes[0], acc0 / l0, acc1 / l1).astype(BF16)
            lse_ref[0, _qrows(i), :] = m0 + jnp.log(l0)
            lse_ref[1, _qrows(i), :] = m1 + jnp.log(l1)
            return 0

        lax.fori_loop(0, t // BQ, qblock, 0)

    return pl.pallas_call(
        body, name=name, grid=(N_PAIRS,),
        in_specs=[_pair_spec(t, 3 * N_PAIRS), _pair_spec(t, 4 * N_PAIRS), _pair_spec(t, 5 * N_PAIRS),
                  _gates_col_spec(t), _gates_row_spec(t // CS)],
        out_specs=[_pair_spec(t, 0), _gates_col_spec(t)],
        out_shape=[jax.ShapeDtypeStruct((t, WIDTH), BF16), jax.ShapeDtypeStruct((N_HEADS, t, 1), F32)],
        compiler_params=_params(("parallel",)),
    )(qkv, qkv, qkv, c_col, c_row)


def fox_pair_bwd(name, qkv, y, dy, lse, c_col, c_row, dqkv):
    t = qkv.shape[0]
    nc = t // CS

    def body(q_ref, k_ref, v_ref, o_in_ref, do_ref, lse_ref, cc_ref, cr_ref, _, o_ref, dcc_ref, dcr_ref,
             dkt_acc, dvt_acc, dcr_acc, res):
        @pl.when(pl.program_id(1) == 0)
        def _():
            lanes = _head_lanes((BQ, PAIR))
            dkt_acc[...] = jnp.zeros_like(dkt_acc)
            dvt_acc[...] = jnp.zeros_like(dvt_acc)
            dcr_acc[...] = jnp.zeros_like(dcr_acc)

            def qblock(i, _):
                q = q_ref[_qrows(i), :]
                do = do_ref[_qrows(i), :]
                q_heads = [_only_head(q, lanes[h]) for h in range(2)]
                do_heads = [_only_head(do, lanes[h]) for h in range(2)]
                qt_heads = [qh.astype(F32).T.astype(BF16) for qh in q_heads]
                dot_heads = [dh.astype(F32).T.astype(BF16) for dh in do_heads]
                prod = do.astype(F32) * o_in_ref[_qrows(i), :].astype(F32)
                delta = [jnp.sum(_only_head(prod, lanes[h]), axis=1, keepdims=True) for h in range(2)]
                ct = [cc_ref[h, _qrows(i), :] for h in range(2)]
                lse_i = [lse_ref[h, _qrows(i), :] for h in range(2)]

                def step(kc, state, dchunk):
                    k = k_ref[_krows(kc), :]
                    v = v_ref[_krows(kc), :]
                    out = []
                    r0 = _live_rows(dchunk)
                    for h in range(2):
                        dq, dct = state[h]
                        s = _dot(_tail(q_heads[h], r0), k, NT) + _tail(ct[h], r0) - cr_ref[h, kc]
                        p = jnp.exp(s - _tail(lse_i[h], r0))
                        if dchunk is not None:
                            p = jnp.where(_diag_mask(dchunk, True), p, 0.0)
                        ds = p * (_dot(_tail(do_heads[h], r0), v, NT) - _tail(delta[h], r0))
                        dvt_acc[kc] += _dot(_tail(dot_heads[h], r0, axis=1), p, NN)
                        dsb = ds.astype(BF16)
                        dkt_acc[kc] += _dot(_tail(qt_heads[h], r0, axis=1), dsb, NN)
                        dcr_acc[h, kc] -= jnp.sum(ds, axis=0, keepdims=True)
                        out.append((_with_tail(dq, _tail(dq, r0) + _dot(dsb, k, NN), r0),
                                    _with_tail(dct, _tail(dct, r0) + jnp.sum(ds, axis=1, keepdims=True), r0)))
                    return tuple(out)

                zero = (jnp.zeros((BQ, PAIR), F32), jnp.zeros((BQ, 1), F32))
                (dq0, dct0), (dq1, dct1) = _walk_chunks(i, step, (zero, zero), False)
                res[0, _qrows(i), :] = (jnp.where(lanes[0], dq0, dq1) * ATTN_SCALE).astype(BF16)
                dcc_ref[0, _qrows(i), :] = dct0
                dcc_ref[1, _qrows(i), :] = dct1
                return 0

            lax.fori_loop(0, t // BQ, qblock, 0)
            _flush_transposed(dkt_acc, res, 1)
            _flush_transposed(dvt_acc, res, 2)
            dcr_ref[...] = dcr_acc[...]

        _emit_dqkv(res, o_ref)

    return pl.pallas_call(
        body, name=name, grid=(N_PAIRS, 3),
        in_specs=[_pair_spec(t, 3 * N_PAIRS), _pair_spec(t, 4 * N_PAIRS), _pair_spec(t, 5 * N_PAIRS), _pair_spec(t, 0),
                  _pair_spec(t, 0), _gates_col_spec(t), _gates_col_spec(t), _gates_row_spec(nc), _ANY],
        out_specs=[pl.BlockSpec((t, PAIR), lambda p, s: (0, (3 + s) * N_PAIRS + p)), _gates_col_spec(t), _gates_row_spec(nc)],
        out_shape=[jax.ShapeDtypeStruct(dqkv.shape, BF16), jax.ShapeDtypeStruct((N_HEADS, t, 1), F32),
                   jax.ShapeDtypeStruct((N_HEADS, nc, 1, CS), F32)],
        input_output_aliases={8: 0},
        scratch_shapes=[pltpu.VMEM((nc, PAIR, CS), F32), pltpu.VMEM((nc, PAIR, CS), F32), pltpu.VMEM((2, nc, 1, CS), F32),
                        pltpu.VMEM((3, t, PAIR), BF16)],
        compiler_params=_params(("parallel", "arbitrary")),
    )(qkv, qkv, qkv, y, dy, lse, c_col, c_row, dqkv)


def loss_head(name, x, g, target):
    t, d = x.shape
    tr = _row_tile(t, 256)

    def body(x_ref, g_ref, t_ref, dx_ref, gn_ref, loss_ref):
        xv = x_ref[...]
        r = lax.rsqrt(jnp.mean(xv * xv, axis=-1, keepdims=True) + RMS_EPS)
        xhat = xv * r
        gv = g_ref[...]
        err = xhat * gv - t_ref[...]
        part_loss = 0.5 * jnp.sum(jnp.mean(err * err, axis=-1, keepdims=True), axis=0, keepdims=True)
        dy = err * (1.0 / d)
        dyg = dy * gv
        dx_ref[...] = r * (dyg - xhat * jnp.mean(dyg * xhat, axis=-1, keepdims=True))
        part_g = jnp.sum(dy * xhat, axis=0, keepdims=True)

        @pl.when(pl.program_id(0) == 0)
        def _():
            gn_ref[...] = part_g
            loss_ref[...] = part_loss

        @pl.when(pl.program_id(0) != 0)
        def _():
            gn_ref[...] += part_g
            loss_ref[...] += part_loss

    row = pl.BlockSpec((tr, d), lambda i: (i, 0))
    return pl.pallas_call(
        body, name=name, grid=(t // tr,),
        in_specs=[row, pl.BlockSpec((1, d), lambda i: (0, 0)), row],
        out_specs=[row, pl.BlockSpec((1, d), lambda i: (0, 0)), pl.BlockSpec((1, 1), lambda i: (0, 0))],
        out_shape=[jax.ShapeDtypeStruct((t, d), F32), jax.ShapeDtypeStruct((1, d), F32), jax.ShapeDtypeStruct((1, 1), F32)],
        compiler_params=_params(("arbitrary",)),
    )(x, g, target)


def _place():
    return lax.axis_index("x"), lax.axis_index("y"), lax.axis_index("c")


def _other_chips(x, y):
    return [(1 - x, y), (x, 1 - y), (1 - x, 1 - y)]


def _half(ref, c, rows):
    return ref.at[:, pl.ds(c * (rows // 2), rows // 2), :]


_ANY = pl.BlockSpec(memory_space=pl.ANY)


def gather_weights(name, bufs):
    n = len(bufs)

    def body(*refs):
        outs = refs[n:2 * n]
        send_sems, recv_sems = refs[2 * n:]
        x, y, c = _place()
        chips = _other_chips(x, y)
        me = 2 * x + y
        sibling = (x, y, 1 - c)
        first, passed = [], []
        for i in range(n):
            rows = outs[i].shape[2]
            mine = _half(outs[i].at[me], c, rows)
            for j, (qx, qy) in enumerate(chips):
                k = 6 * i + j
                rc = pltpu.make_async_remote_copy(
                    src_ref=mine, dst_ref=mine,
                    send_sem=send_sems.at[k], recv_sem=recv_sems.at[k], device_id=(qx, qy, c), device_id_type=MESH)
                rc.start()
                first.append(rc)
        for i in range(n):
            rows = outs[i].shape[2]
            for j, (qx, qy) in enumerate(chips):
                k = 6 * i + j
                block = _half(outs[i].at[2 * qx + qy], c, rows)
                pltpu.make_async_remote_copy(
                    src_ref=block, dst_ref=block, send_sem=send_sems.at[k], recv_sem=recv_sems.at[k],
                    device_id=(qx, qy, c), device_id_type=MESH).wait_recv()
                fw = pltpu.make_async_remote_copy(
                    src_ref=block, dst_ref=block, send_sem=send_sems.at[k + 3], recv_sem=recv_sems.at[k + 3],
                    device_id=sibling, device_id_type=MESH)
                fw.start()
                passed.append(fw)
        for i in range(n):
            rows = outs[i].shape[2]
            for j, (qx, qy) in enumerate(chips):
                k = 6 * i + j + 3
                block = _half(outs[i].at[2 * qx + qy], 1 - c, rows)
                pltpu.make_async_remote_copy(
                    src_ref=block, dst_ref=block, send_sem=send_sems.at[k], recv_sem=recv_sems.at[k],
                    device_id=sibling, device_id_type=MESH).wait_recv()
        for cp in first + passed:
            cp.wait_send()

    return pl.pallas_call(
        body, name=name, in_specs=[_ANY] * n, out_specs=[_ANY] * n,
        out_shape=[jax.ShapeDtypeStruct(b.shape, b.dtype) for b in bufs],
        input_output_aliases={i: i for i in range(n)},
        scratch_shapes=[pltpu.SemaphoreType.DMA((6 * n,)), pltpu.SemaphoreType.DMA((6 * n,))],
        compiler_params=pltpu.CompilerParams(has_side_effects=True),
    )(*bufs)


_HBM = pl.BlockSpec(memory_space=pltpu.HBM)
_SEM = pl.BlockSpec(memory_space=pltpu.SEMAPHORE)
_DATAFLOW = pltpu.SideEffectType.DATAFLOW_SIDE_EFFECTING


def _in_hbm(a):
    return pltpu.with_memory_space_constraint(a, pltpu.HBM)


def _gather_ici_copies(bufs, send_sems, recv_sems, arrivals):
    x, y, c = _place()
    me = 2 * x + y
    copies = []
    for i, buf in enumerate(bufs):
        rows = buf.shape[2]
        for j, (qx, qy) in enumerate(_other_chips(x, y)):
            block = _half(buf.at[2 * qx + qy if arrivals else me], c, rows)
            copies.append(pltpu.make_async_remote_copy(
                src_ref=block, dst_ref=block, send_sem=send_sems.at[3 * i + j], recv_sem=recv_sems.at[3 * i + j],
                device_id=(qx, qy, c), device_id_type=MESH))
    return copies


def gather_ici_start(name, bufs, after):
    n = len(bufs)

    def body(*refs):
        ins = refs[:n]
        send_sems, recv_sems = refs[n + 1], refs[n + 2]
        token = refs[-1]
        for send in _gather_ici_copies(ins, send_sems, recv_sems, False):
            send.start()
        token[...] = jnp.zeros_like(token)

    res = pl.pallas_call(
        body, name=name,
        out_shape=(pltpu.SemaphoreType.DMA((3 * n,)), pltpu.SemaphoreType.DMA((3 * n,)), *[pltpu.HBM(b.shape, b.dtype) for b in bufs],
                   jax.ShapeDtypeStruct((8, 128), F32)),
        in_specs=[_HBM] * n + [_ANY], out_specs=(_SEM, _SEM, *[_HBM] * n, pl.BlockSpec(memory_space=pltpu.VMEM)),
        input_output_aliases={i: 2 + i for i in range(n)},
        compiler_params=pltpu.CompilerParams(has_side_effects=_DATAFLOW),
    )(*[_in_hbm(b) for b in bufs], after)
    return res[0], res[1], list(res[2:2 + n]), res[-1]


def gather_ici_wait(name, send_sems, recv_sems, bufs, after):
    n = len(bufs)

    def body(*refs):
        ins = refs[:n]
        send_sems_ref, recv_sems_ref = refs[n], refs[n + 1]
        for send in _gather_ici_copies(ins, send_sems_ref, recv_sems_ref, False):
            send.wait_send()
        for recv in _gather_ici_copies(ins, send_sems_ref, recv_sems_ref, True):
            recv.wait_recv()

    return pl.pallas_call(
        body, name=name, out_shape=tuple(pltpu.HBM(b.shape, b.dtype) for b in bufs),
        in_specs=[_HBM] * n + [_SEM, _SEM, _ANY], out_specs=tuple([_HBM] * n),
        input_output_aliases={i: i for i in range(n)},
        compiler_params=pltpu.CompilerParams(has_side_effects=_DATAFLOW),
    )(*bufs, send_sems, recv_sems, after)


def gather_forward(name, bufs):
    n = len(bufs)

    def body(*refs):
        outs = refs[n:2 * n]
        send_sems, recv_sems = refs[2 * n:]
        x, y, c = _place()
        sibling = (x, y, 1 - c)
        sends = []
        for i in range(n):
            rows = outs[i].shape[2]
            for j, (qx, qy) in enumerate(_other_chips(x, y)):
                block = _half(outs[i].at[2 * qx + qy], c, rows)
                fw = pltpu.make_async_remote_copy(
                    src_ref=block, dst_ref=block, send_sem=send_sems.at[3 * i + j], recv_sem=recv_sems.at[3 * i + j],
                    device_id=sibling, device_id_type=MESH)
                fw.start()
                sends.append(fw)
        for i in range(n):
            rows = outs[i].shape[2]
            for j, (qx, qy) in enumerate(_other_chips(x, y)):
                block = _half(outs[i].at[2 * qx + qy], 1 - c, rows)
                pltpu.make_async_remote_copy(
                    src_ref=block, dst_ref=block, send_sem=send_sems.at[3 * i + j], recv_sem=recv_sems.at[3 * i + j],
                    device_id=sibling, device_id_type=MESH).wait_recv()
        for fw in sends:
            fw.wait_send()

    return pl.pallas_call(
        body, name=name, in_specs=[_ANY] * n, out_specs=[_ANY] * n,
        out_shape=[jax.ShapeDtypeStruct(b.shape, b.dtype) for b in bufs],
        input_output_aliases={i: i for i in range(n)},
        scratch_shapes=[pltpu.SemaphoreType.DMA((3 * n,)), pltpu.SemaphoreType.DMA((3 * n,))],
        compiler_params=pltpu.CompilerParams(has_side_effects=True),
    )(*bufs)


def _between_chips_copies(parts, lands, send_sems, recv_sems):
    x, y, c = _place()
    copies = []
    for i, (part, land) in enumerate(zip(parts, lands)):
        for j, (qx, qy) in enumerate(_other_chips(x, y)):
            copies.append(pltpu.make_async_remote_copy(
                src_ref=part.at[2 * qx + qy], dst_ref=land.at[j], send_sem=send_sems.at[3 * i + j], recv_sem=recv_sems.at[3 * i + j],
                device_id=(qx, qy, c), device_id_type=MESH))
    return copies


def between_chips_start(name, parts):
    n = len(parts)
    lands = [lax.empty((N_CHIPS - 1,) + p.shape[1:], p.dtype) for p in parts]

    def body(*refs):
        send_sems, recv_sems = refs[2 * n], refs[2 * n + 1]
        token = refs[-1]
        for cp in _between_chips_copies(refs[:n], refs[n:2 * n], send_sems, recv_sems):
            cp.start()
        token[...] = jnp.zeros_like(token)

    res = pl.pallas_call(
        body, name=name,
        out_shape=(pltpu.SemaphoreType.DMA((3 * n,)), pltpu.SemaphoreType.DMA((3 * n,)),
                   *[pltpu.HBM(a.shape, a.dtype) for a in parts + lands], jax.ShapeDtypeStruct((8, 128), F32)),
        in_specs=[_HBM] * (2 * n), out_specs=(_SEM, _SEM, *[_HBM] * (2 * n), pl.BlockSpec(memory_space=pltpu.VMEM)),
        input_output_aliases={i: 2 + i for i in range(2 * n)},
        compiler_params=pltpu.CompilerParams(has_side_effects=_DATAFLOW),
    )(*[_in_hbm(a) for a in parts + lands])
    return res[0], res[1], list(res[2:2 + n]), list(res[2 + n:2 + 2 * n]), res[-1]


def between_chips_wait(name, send_sems, recv_sems, parts, lands, after):
    n = len(parts)

    def body(*refs):
        for cp in _between_chips_copies(refs[:n], refs[n:2 * n], refs[2 * n], refs[2 * n + 1]):
            cp.wait_send()
            cp.wait_recv()

    res = pl.pallas_call(
        body, name=name, out_shape=tuple(pltpu.HBM(a.shape, a.dtype) for a in parts + lands),
        in_specs=[_HBM] * (2 * n) + [_SEM, _SEM, _ANY], out_specs=tuple([_HBM] * (2 * n)),
        input_output_aliases={i: i for i in range(2 * n)},
        compiler_params=pltpu.CompilerParams(has_side_effects=_DATAFLOW),
    )(*parts, *lands, send_sems, recv_sems, after)
    return list(res[:n]), list(res[n:])


def exchange_start(name, arrays, n_copies, copies, after=()):
    n = len(arrays)

    def body(*refs):
        send_sems, recv_sems = refs[n + len(after)], refs[n + len(after) + 1]
        for cp in copies(refs[:n], send_sems, recv_sems):
            cp.start()
        refs[-1][...] = jnp.zeros_like(refs[-1])

    res = pl.pallas_call(
        body, name=name,
        out_shape=(pltpu.SemaphoreType.DMA((n_copies,)), pltpu.SemaphoreType.DMA((n_copies,)),
                   *[pltpu.HBM(a.shape, a.dtype) for a in arrays], jax.ShapeDtypeStruct((8, 128), F32)),
        in_specs=[_HBM] * n + [_ANY] * len(after),
        out_specs=(_SEM, _SEM, *[_HBM] * n, pl.BlockSpec(memory_space=pltpu.VMEM)),
        input_output_aliases={i: 2 + i for i in range(n)},
        compiler_params=pltpu.CompilerParams(has_side_effects=_DATAFLOW),
    )(*[_in_hbm(a) for a in arrays], *after)
    return res[0], res[1], list(res[2:2 + n]), res[-1]


def exchange_wait(name, send_sems, recv_sems, arrays, copies, after):
    n = len(arrays)

    def body(*refs):
        for cp in copies(refs[:n], refs[n], refs[n + 1]):
            cp.wait_send()
            cp.wait_recv()

    return list(pl.pallas_call(
        body, name=name, out_shape=tuple(pltpu.HBM(a.shape, a.dtype) for a in arrays),
        in_specs=[_HBM] * n + [_SEM, _SEM, _ANY], out_specs=tuple([_HBM] * n),
        input_output_aliases={i: i for i in range(n)},
        compiler_params=pltpu.CompilerParams(has_side_effects=_DATAFLOW),
    )(*arrays, send_sems, recv_sems, after))


def _to_sibling_copies(n):
    def copies(refs, send_sems, recv_sems):
        x, y, c = _place()
        out = []
        for i in range(n):
            rows = refs[i].shape[2]
            out.append(pltpu.make_async_remote_copy(
                src_ref=refs[i].at[:, :, pl.ds((1 - c) * (rows // 2), rows // 2), :], dst_ref=refs[n + i],
                send_sem=send_sems.at[i], recv_sem=recv_sems.at[i], device_id=(x, y, 1 - c), device_id_type=MESH))
        return out
    return copies


def _share_copies(n):
    def copies(refs, send_sems, recv_sems):
        x, y, c = _place()
        out = []
        for i in range(n):
            mine = _half(refs[i], c, refs[i].shape[1])
            out.append(pltpu.make_async_remote_copy(
                src_ref=mine, dst_ref=mine, send_sem=send_sems.at[i], recv_sem=recv_sems.at[i],
                device_id=(x, y, 1 - c), device_id_type=MESH))
        return out
    return copies


def send_half_to_sibling(name, grads):
    n = len(grads)

    def body(*refs):
        srcs, outs = refs[:n], refs[n:2 * n]
        send_sems, recv_sems = refs[2 * n:]
        x, y, c = _place()
        sibling = (x, y, 1 - c)
        copies = []
        for i in range(n):
            rows = srcs[i].shape[2]
            rc = pltpu.make_async_remote_copy(
                src_ref=srcs[i].at[:, :, pl.ds((1 - c) * (rows // 2), rows // 2), :], dst_ref=outs[i],
                send_sem=send_sems.at[i], recv_sem=recv_sems.at[i], device_id=sibling, device_id_type=MESH)
            rc.start()
            copies.append(rc)
        for rc in copies:
            rc.wait()

    def half_shape(g):
        s = g.shape
        return jax.ShapeDtypeStruct((s[0], s[1], s[2] // 2, s[3]), g.dtype)

    return pl.pallas_call(
        body, name=name, in_specs=[_ANY] * n, out_specs=[_ANY] * n, out_shape=[half_shape(g) for g in grads],
        scratch_shapes=[pltpu.SemaphoreType.DMA((n,)), pltpu.SemaphoreType.DMA((n,))],
        compiler_params=pltpu.CompilerParams(has_side_effects=True),
    )(*grads)


def exchange_between_chips(name, parts):
    n = len(parts)

    def body(*refs):
        srcs, outs = refs[:n], refs[n:2 * n]
        send_sems, recv_sems = refs[2 * n:]
        x, y, c = _place()
        chips = _other_chips(x, y)
        copies = []
        for i in range(n):
            for j, (qx, qy) in enumerate(chips):
                k = 3 * i + j
                rc = pltpu.make_async_remote_copy(
                    src_ref=srcs[i].at[2 * qx + qy], dst_ref=outs[i].at[j],
                    send_sem=send_sems.at[k], recv_sem=recv_sems.at[k], device_id=(qx, qy, c), device_id_type=MESH)
                rc.start()
                copies.append(rc)
        for rc in copies:
            rc.wait()

    return pl.pallas_call(
        body, name=name, in_specs=[_ANY] * n, out_specs=[_ANY] * n,
        out_shape=[jax.ShapeDtypeStruct((N_CHIPS - 1,) + p.shape[1:], p.dtype) for p in parts],
        scratch_shapes=[pltpu.SemaphoreType.DMA((3 * n,)), pltpu.SemaphoreType.DMA((3 * n,))],
        compiler_params=pltpu.CompilerParams(has_side_effects=True),
    )(*parts)


def share_halves(name, bufs):
    n = len(bufs)

    def body(*refs):
        outs = refs[n:2 * n]
        send_sems, recv_sems = refs[2 * n:]
        x, y, c = _place()
        copies = []
        for i in range(n):
            mine = _half(outs[i], c, outs[i].shape[1])
            rc = pltpu.make_async_remote_copy(
                src_ref=mine, dst_ref=mine, send_sem=send_sems.at[i], recv_sem=recv_sems.at[i],
                device_id=(x, y, 1 - c), device_id_type=MESH)
            rc.start()
            copies.append(rc)
        for i in range(n):
            theirs = _half(outs[i], 1 - c, outs[i].shape[1])
            pltpu.make_async_remote_copy(
                src_ref=theirs, dst_ref=theirs, send_sem=send_sems.at[i], recv_sem=recv_sems.at[i],
                device_id=(x, y, 1 - c), device_id_type=MESH).wait_recv()
        for rc in copies:
            rc.wait_send()

    return pl.pallas_call(
        body, name=name, in_specs=[_ANY] * n, out_specs=[_ANY] * n,
        out_shape=[jax.ShapeDtypeStruct(b.shape, b.dtype) for b in bufs],
        input_output_aliases={i: i for i in range(n)},
        scratch_shapes=[pltpu.SemaphoreType.DMA((n,)), pltpu.SemaphoreType.DMA((n,))],
        compiler_params=pltpu.CompilerParams(has_side_effects=True),
    )(*bufs)


def pair_sum(name, grad, recv, c):
    ns, na, rh, cols = recv.shape
    tr = _row_tile(rh, 256) if rh % 256 == 0 else rh
    nt = rh // tr

    def body(c_ref, g_ref, r_ref, o_ref):
        o_ref[...] = (g_ref[...].astype(F32) + r_ref[...].astype(F32)).astype(BF16)

    blk = (None, None, tr, cols)
    return pl.pallas_call(
        body, name=name,
        grid_spec=pltpu.PrefetchScalarGridSpec(
            num_scalar_prefetch=1, grid=(ns, na, nt),
            in_specs=[pl.BlockSpec(blk, lambda s, a, r, c_ref: (s, a, c_ref[0] * nt + r, 0)),
                      pl.BlockSpec(blk, lambda s, a, r, c_ref: (s, a, r, 0))],
            out_specs=pl.BlockSpec(blk, lambda s, a, r, c_ref: (s, a, r, 0))),
        out_shape=jax.ShapeDtypeStruct(recv.shape, BF16),
        compiler_params=_params(("parallel", "parallel", "parallel")),
    )(c, grad, recv)


def chip_sum(name, parts, landed, place):
    _, na, rh, cols = parts.shape
    tr = _row_tile(rh, 256) if rh % 256 == 0 else rh
    nt = rh // tr

    def body(place_ref, p_ref, l_ref, o_ref):
        total = p_ref[...].astype(F32)
        for s in range(N_CHIPS - 1):
            total = total + l_ref[s].astype(F32)
        o_ref[...] = total

    return pl.pallas_call(
        body, name=name,
        grid_spec=pltpu.PrefetchScalarGridSpec(
            num_scalar_prefetch=1, grid=(na, nt),
            in_specs=[pl.BlockSpec((None, None, tr, cols), lambda a, r, pr: (pr[1], a, r, 0)),
                      pl.BlockSpec((N_CHIPS - 1, None, tr, cols), lambda a, r, pr: (0, a, r, 0))],
            out_specs=pl.BlockSpec((None, tr, cols), lambda a, r, pr: (a, pr[0] * nt + r, 0))),
        out_shape=jax.ShapeDtypeStruct((na, 2 * rh, cols), F32),
        compiler_params=_params(("parallel", "parallel")),
    )(place, parts, landed)


def reduce_scatter_1(tag, grads):
    n = len(grads)
    lands = [lax.empty((g.shape[0], g.shape[1], g.shape[2] // 2, g.shape[3]), g.dtype) for g in grads]
    send_sems, recv_sems, arrays, token = exchange_start(f"{tag}_to_sibling_start", list(grads) + lands, n, _to_sibling_copies(n))
    return (send_sems, recv_sems, arrays), token


def reduce_scatter_2(tag, state, place, after):
    send_sems, recv_sems, arrays = state
    n = len(arrays) // 2
    arrays = exchange_wait(f"{tag}_to_sibling_wait", send_sems, recv_sems, arrays, _to_sibling_copies(n), after)
    parts = [pair_sum(f"{tag}_pair_sum{i}", g, r, place) for i, (g, r) in enumerate(zip(arrays[:n], arrays[n:]))]
    send_sems, recv_sems, parts, lands, token = between_chips_start(f"{tag}_between_chips_start", parts)
    return (send_sems, recv_sems, parts, lands), token


def reduce_scatter_3(tag, state, place, after):
    send_sems, recv_sems, parts, lands = state
    parts, landed = between_chips_wait(f"{tag}_between_chips_wait", send_sems, recv_sems, parts, lands, after)
    halves = [chip_sum(f"{tag}_chip_sum{i}", p, l, place) for i, (p, l) in enumerate(zip(parts, landed))]
    send_sems, recv_sems, halves, token = exchange_start(f"{tag}_share_start", halves, len(halves), _share_copies(len(halves)))
    return (send_sems, recv_sems, halves), token


def reduce_scatter_4(tag, state, after):
    send_sems, recv_sems, halves = state
    return exchange_wait(f"{tag}_share_wait", send_sems, recv_sems, halves, _share_copies(len(halves)), after)


def _adamw_math(w, g, m, v):
    m = ADAM_B1 * m + (1.0 - ADAM_B1) * g
    v = ADAM_B2 * v + (1.0 - ADAM_B2) * (g * g)
    m_hat = m / (1.0 - ADAM_B1 ** ADAM_STEP)
    v_hat = v / (1.0 - ADAM_B2 ** ADAM_STEP)
    delta = -ADAM_LR * (m_hat / (jnp.sqrt(v_hat) + ADAM_EPS) + ADAM_WD * w)
    return delta, m, v


def adamw(name, w, g, m, v, after=()):
    rows, cols = w.shape
    tr = _row_tile(rows, 256) if rows % 256 == 0 else rows // 2

    def body(w_ref, g_ref, m_ref, v_ref, d_ref, mo_ref, vo_ref):
        d_ref[...], mo_ref[...], vo_ref[...] = _adamw_math(w_ref[...], g_ref[...], m_ref[...], v_ref[...])

    blk = pl.BlockSpec((tr, cols), lambda i: (i, 0))
    return pl.pallas_call(
        _ordered_after(body, 4, len(after)), name=name, grid=(rows // tr,), in_specs=[blk] * 4 + [_ANY] * len(after),
        out_specs=[blk] * 3, out_shape=[jax.ShapeDtypeStruct(w.shape, F32)] * 3, compiler_params=_params(("parallel",)),
    )(w, g, m, v, *after)


def adamw_rows(name, w, g, m, v, after=()):
    rows, _, cols = w.shape
    tr = next(r for r in (128, 110, 64, 32, 16, 8, 1) if rows % r == 0)

    def body(w_ref, g_ref, m_ref, v_ref, d_ref, mo_ref, vo_ref):
        d_ref[...], mo_ref[...], vo_ref[...] = _adamw_math(w_ref[...], g_ref[...], m_ref[...], v_ref[...])

    blk = pl.BlockSpec((tr, 1, cols), lambda i: (i, 0, 0))
    return pl.pallas_call(
        _ordered_after(body, 4, len(after)), name=name, grid=(rows // tr,), in_specs=[blk] * 4 + [_ANY] * len(after),
        out_specs=[blk] * 3, out_shape=[jax.ShapeDtypeStruct(w.shape, F32)] * 3, compiler_params=_params(("parallel",)),
    )(w, g, m, v, *after)


def adamw_stacked(name, ws, g, ms, vs, after=()):
    n = len(ws)
    rows, cols = ws[0].shape
    tr = next(r for r in (128, 88, 64, 32, 16, 8) if rows % r == 0)

    def body(*refs):
        w_refs, m_refs, v_refs, g_ref = refs[:n], refs[n:2 * n], refs[2 * n:3 * n], refs[3 * n]
        outs = refs[3 * n + 1:]
        for i in range(n):
            outs[i][...], outs[n + i][...], outs[2 * n + i][...] = _adamw_math(
                w_refs[i][...], g_ref[i], m_refs[i][...], v_refs[i][...])

    blk = pl.BlockSpec((tr, cols), lambda r: (r, 0))
    res = pl.pallas_call(
        _ordered_after(body, 3 * n + 1, len(after)), name=name, grid=(rows // tr,),
        in_specs=[blk] * (3 * n) + [pl.BlockSpec((n, tr, cols), lambda r: (0, r, 0))] + [_ANY] * len(after),
        out_specs=[blk] * (3 * n),
        out_shape=[jax.ShapeDtypeStruct((rows, cols), F32)] * (3 * n), compiler_params=_params(("parallel",)),
    )(*ws, *ms, *vs, g, *after)
    return res[:n], res[n:2 * n], res[2 * n:]


def small_allreduce_adamw(name, g_part, w, m, v):
    rows, cols = g_part.shape

    def body(g_ref, w_ref, m_ref, v_ref, sum_ref, d_ref, mo_ref, vo_ref, land, send_sems, recv_sems):
        x, y, c = _place()
        me = 4 * x + 2 * y + c
        land[me] = g_ref[...]
        copies = []
        for r in range(1, 8):
            peer = (x ^ (r >> 2), y ^ ((r >> 1) & 1), c ^ (r & 1))
            rc = pltpu.make_async_remote_copy(
                src_ref=g_ref, dst_ref=land.at[me], send_sem=send_sems.at[r - 1], recv_sem=recv_sems.at[r - 1],
                device_id=peer, device_id_type=MESH)
            rc.start()
            copies.append(rc)
        for rc in copies:
            rc.wait()
        total = land[0]
        for s in range(1, 8):
            total = total + land[s]
        sum_ref[...] = total
        d_ref[...], mo_ref[...], vo_ref[...] = _adamw_math(w_ref[...], total, m_ref[...], v_ref[...])

    vmem = pl.BlockSpec(memory_space=pltpu.VMEM)
    return pl.pallas_call(
        body, name=name, in_specs=[vmem] * 4, out_specs=[vmem] * 4,
        out_shape=[jax.ShapeDtypeStruct((rows, cols), F32)] * 4,
        scratch_shapes=[pltpu.VMEM((8, rows, cols), F32), pltpu.SemaphoreType.DMA((7,)), pltpu.SemaphoreType.DMA((7,))],
        compiler_params=pltpu.CompilerParams(has_side_effects=True),
    )(g_part, w, m, v)


def _heads(a):
    t, w = a.shape
    return a.reshape(t, w // HEAD_DIM, HEAD_DIM).transpose(1, 0, 2)


def _unheads(a):
    n, t, _ = a.shape
    return a.transpose(1, 0, 2).reshape(t, n * HEAD_DIM)


def _cols_from_shards(g):
    ns, r, cols = g.shape
    return g.transpose(1, 0, 2).reshape(r, ns * cols)


def _shards_from_cols(a):
    r, cols = a.shape
    return a.reshape(r, N_CHIPS, cols // N_CHIPS).transpose(1, 0, 2)


def kernel(x, norm_ffn1, w_ffn1_gate, w_ffn1_up, w_ffn1_down, norm_mix, w_in, b_forget, w_gate, b_gate, w_up_a, w_up_b, w_out, norm_ffn2, w_ffn2_gate, w_ffn2_up, w_ffn2_down, norm_final, loss_target, m_norm_ffn1, m_w_ffn1_gate, m_w_ffn1_up, m_w_ffn1_down, m_norm_mix, m_w_in, m_b_forget, m_w_gate, m_b_gate, m_w_up_a, m_w_up_b, m_w_out, m_norm_ffn2, m_w_ffn2_gate, m_w_ffn2_up, m_w_ffn2_down, m_norm_final, v_norm_ffn1, v_w_ffn1_gate, v_w_ffn1_up, v_w_ffn1_down, v_norm_mix, v_w_in, v_b_forget, v_w_gate, v_b_gate, v_w_up_a, v_w_up_b, v_w_out, v_norm_ffn2, v_w_ffn2_gate, v_w_ffn2_up, v_w_ffn2_down, v_norm_final):
    t, d = x.shape[1], x.shape[2]
    in4 = w_in.shape[2]
    gate4 = w_gate.shape[2]
    up4 = w_up_a.shape[2]
    in_cols = N_CHIPS * in4
    n_forget = in_cols - QKV_COLS
    assert w_up_a.shape[1] == WIDTH and d == 2 * WIDTH and n_forget == N_HEADS
    nq = t // BQ
    chip = 2 * lax.axis_index("x") + lax.axis_index("y")
    c_arr = jnp.stack([lax.axis_index("c"), chip]).astype(jnp.int32)
    x2d = x[0]
    target = loss_target[0]

    def slot(shard):
        return lax.dynamic_update_slice(lax.empty((N_CHIPS,) + shard.shape, BF16), shard.astype(BF16)[None], (chip, 0, 0, 0))

    def ffn_views(wg, wu, wd):
        return [wg[0].T, wu[0].T, wd[0]]

    ffn1_w, ffn1_m, ffn1_v = (ffn_views(w_ffn1_gate, w_ffn1_up, w_ffn1_down), ffn_views(m_w_ffn1_gate, m_w_ffn1_up, m_w_ffn1_down),
                              ffn_views(v_w_ffn1_gate, v_w_ffn1_up, v_w_ffn1_down))
    ffn2_w, ffn2_m, ffn2_v = (ffn_views(w_ffn2_gate, w_ffn2_up, w_ffn2_down), ffn_views(m_w_ffn2_gate, m_w_ffn2_up, m_w_ffn2_down),
                              ffn_views(v_w_ffn2_gate, v_w_ffn2_up, v_w_ffn2_down))
    in_pad = -(-in4 // 32) * 32
    ig_sh = slot(jnp.concatenate([jnp.pad(w_in[0].T, ((0, in_pad - in4), (0, 0))), w_gate[0].T], axis=0)[None])
    up_sh = slot(jnp.concatenate([w_up_a[0], w_up_b[0]], axis=0)[None])
    wo_sh = slot(w_out)
    f1_send, f1_recv, f1_bufs, f1_token = gather_ici_start("gather_ffn1_start", [slot(jnp.stack(ffn1_w))], norm_ffn1)
    mx_send, mx_recv, mx_bufs, mx_token = gather_ici_start("gather_mixer_start", [ig_sh, up_sh, wo_sh], f1_token)
    f2_send, f2_recv, f2_bufs, f2_token = gather_ici_start("gather_ffn2_start", [slot(jnp.stack(ffn2_w))], mx_token)

    normed1 = rms_fwd("ffn1_rms", x2d, norm_ffn1, after=(f2_token,))
    (w3_1,) = gather_forward("gather_ffn1_forward", gather_ici_wait("gather_ffn1_wait", f1_send, f1_recv, f1_bufs, normed1[0]))
    x1, saved1 = ffn_forward("ffn1", x2d, norm_ffn1, w3_1, normed=normed1)
    ig, wup, wo = gather_forward("gather_mixer_forward", gather_ici_wait("gather_mixer_wait", mx_send, mx_recv, mx_bufs, x1))
    wup = wup[:, 0]
    w_in_t = ig[:, 0, :in4].reshape(in_cols, d)
    wcat = jnp.concatenate([w_in_t[:QKV_COLS], ig[:, 0, in_pad:].reshape(2 * d, d), w_in_t[QKV_COLS:],
                            jnp.zeros((F_PAD - n_forget, d), BF16)], axis=0)
    bias_cat = jnp.concatenate([jnp.zeros((1, QKV_COLS), F32), b_gate, jnp.zeros((1, F_PAD), F32)], axis=1)
    f_off = QKV_COLS + 2 * d
    wo_full = wo.reshape(d, d)
    b_forget_row = jnp.pad(b_forget, ((0, 0), (0, QB - n_forget)))

    h2, rstd2 = rms_fwd("mix_rms", x1, norm_mix)
    qkv = proj("mix_proj_qkv", h2, wcat, bias_cat, 0, QKV_COLS, WIDTH, BF16, scaled_tiles=(0, 3))
    pc = proj("mix_proj_gates", h2, wcat, bias_cat, QKV_COLS, 2 * d + F_PAD, 768, F32)
    f_logit = pc[:, 2 * d:2 * d + QB]
    c_cum = fox_prep("fox_prep", f_logit, b_forget_row)
    c_heads = c_cum[:, :N_HEADS].T
    c_col = c_heads[:, :, None]
    c_row = c_heads.reshape(N_HEADS, t // CS, 1, CS)
    ya = sb_pair_fwd("sb_fwd", qkv)
    yb, lse = fox_pair_fwd("fox_fwd", qkv, c_col, c_row)
    ua, ub, mixed = mix_fwd("mix_fwd", ya, yb, wup, pc)
    x2 = mm_residual("mix_out", mixed[None], wo_full[None], pl.BlockSpec((1, d, d), lambda m: (0, 0, 0)), x1, 1.0)
    (w3_2,) = gather_forward("gather_ffn2_forward", gather_ici_wait("gather_ffn2_wait", f2_send, f2_recv, f2_bufs, x2))
    x3, saved2 = ffn_forward("ffn2", x2, norm_ffn2, w3_2)
    dx3, gn_final, loss_part = loss_head("loss_head", x3, norm_final[None], target)

    g_w3_2, dab2 = ffn_backward_weights("ffn2", dx3, saved2, w3_2)
    rs_ffn2, token = reduce_scatter_1("rs_ffn2", [g_w3_2])
    dx2, gn_ffn2 = ffn_backward_input("ffn2", dx3, saved2, dab2, norm_ffn2, w3_2, after=(token,))
    rs_ffn2, rs_ffn2_token = reduce_scatter_2("rs_ffn2", rs_ffn2, c_arr, dx2)

    dua, dub, dpa, dpb, gba, gbb = mix_bwd("mix_bwd", dx2, wo_full, pc, ua, ub, after=(rs_ffn2_token,))
    dgp = jnp.concatenate([dpa, dpb], axis=1)
    g_bgate = jnp.concatenate([gba, gbb], axis=1)
    g_wo = mm_plain("wgrad_out", mixed, dx2, TN, BF16, tk_target=1024)
    dya = up_bwd("dya", dua, wup, 0)
    dyb = up_bwd("dyb", dub, wup, 1)
    g_up = up_wgrad("wgrad_up", ya, yb, dua, dub)
    dqkv = sb_pair_bwd("sb_bwd", qkv, dya, lax.empty((t, QKV_COLS), BF16))
    dqkv, dcc, dcr = fox_pair_bwd("fox_bwd", qkv, yb, dyb, lse, c_col, c_row, dqkv)
    dc = (dcc[:, :, 0] + dcr.reshape(N_HEADS, t)).T
    df, g_bf = fox_gate_bwd("fox_gate_bwd", jnp.pad(dc, ((0, 0), (0, QB - N_HEADS))), f_logit, b_forget_row)
    dcat = jnp.concatenate([dqkv, dgp, df.astype(BF16), jnp.zeros((t, F_PAD - QB), BF16)], axis=1)
    g_wcat = wgrad_cat("wgrad_cat", h2, dcat)
    g_in_t = jnp.concatenate([g_wcat[:QKV_COLS], g_wcat[f_off:f_off + n_forget]], axis=0).reshape(N_CHIPS, in4, d)
    g_ig = jnp.concatenate([jnp.pad(g_in_t, ((0, 0), (0, in_pad - in4), (0, 0))),
                            g_wcat[QKV_COLS:f_off].reshape(N_CHIPS, gate4, d)], axis=1)[:, None]
    rs_mixer, token = reduce_scatter_1("rs_mixer", [g_ig, g_up[:, None], g_wo.reshape(N_CHIPS, 1, d // N_CHIPS, d)])
    tm = _row_tile(t, 512)
    tkc = 1792
    nkc = wcat.shape[0] // tkc
    dx1, gn_mix = dh_rms_bwd(
        "mix_dh", [(dcat, wcat)],
        [(pl.BlockSpec((tm, tkc), lambda m, k: (m, k)), pl.BlockSpec((tkc, d), lambda m, k: (k, 0)))], NN,
        (t // tm, nkc), nkc, x1, rstd2, norm_mix, dx2, after=(token,))
    rs_mixer, rs_mixer_token = reduce_scatter_2("rs_mixer", rs_mixer, c_arr, dx1)
    rs_ffn2, rs_ffn2_token = reduce_scatter_3("rs_ffn2", rs_ffn2, c_arr, dx1)

    g_w3_1, dab1 = ffn_backward_weights("ffn1", dx1, saved1, w3_1, after=(rs_mixer_token, rs_ffn2_token))
    rs_ffn1, token = reduce_scatter_1("rs_ffn1", [g_w3_1])
    dx0, gn_ffn1 = ffn_backward_input("ffn1", dx1, saved1, dab1, norm_ffn1, w3_1, after=(token,))
    rs_ffn1, rs_ffn1_token = reduce_scatter_2("rs_ffn1", rs_ffn1, c_arr, dx0)
    rs_mixer, rs_mixer_token = reduce_scatter_3("rs_mixer", rs_mixer, c_arr, dx0)
    (s_w3_2,) = reduce_scatter_4("rs_ffn2", rs_ffn2, dx0)

    def pack_small(n1, nm, n2, nf, bg, bf, last):
        return jnp.concatenate([n1, nm, n2, nf, bg.reshape(2, d), jnp.pad(bf, ((0, 0), (0, d - n_forget))), last], axis=0)

    zero_row = jnp.zeros((1, d), F32)
    g_small = pack_small(gn_ffn1, gn_mix, gn_ffn2, gn_final, g_bgate, g_bf[:, :n_forget], jnp.pad(loss_part, ((0, 0), (0, d - 1))))
    w_small = pack_small(norm_ffn1, norm_mix, norm_ffn2, norm_final[None], b_gate, b_forget, zero_row)
    m_small = pack_small(m_norm_ffn1, m_norm_mix, m_norm_ffn2, m_norm_final[None], m_b_gate, m_b_forget, zero_row)
    v_small = pack_small(v_norm_ffn1, v_norm_mix, v_norm_ffn2, v_norm_final[None], v_b_gate, v_b_forget, zero_row)
    smalls = small_allreduce_adamw("small_allreduce_adamw", g_small, w_small, m_small, v_small)

    def unpack_small(p):
        return {"norm_ffn1": p[0:1], "norm_mix": p[1:2], "norm_ffn2": p[2:3], "norm_final": p[3], "b_gate": p[4:6].reshape(1, 2 * d),
                "b_forget": p[6:7, :n_forget]}

    loss = smalls[0][7, 0]
    small_out = [unpack_small(p) for p in smalls]

    big_out = {}

    def adamw_ffn(tag, s_w3, ws, ms, vs, after):
        deltas, new_ms, new_vs = adamw_stacked(f"adamw_{tag}", ws, s_w3, ms, vs, after)
        for which, part in ((GATE, "gate"), (UP, "up"), (DOWN, "down")):
            back = (lambda a: a[None]) if which == DOWN else (lambda a: a.T[None])
            big_out[f"w_{tag}_{part}"] = tuple(back(a) for a in (s_w3[which], deltas[which], new_ms[which], new_vs[which]))
        return deltas[DOWN]

    last = adamw_ffn("ffn2", s_w3_2, ffn2_w, ffn2_m, ffn2_v, (rs_ffn1_token, rs_mixer_token))
    s_ig, s_up, s_wo = reduce_scatter_4("rs_mixer", rs_mixer, last)
    grads = {
        "w_gate": s_ig[0, in_pad:].T, "w_up_a": s_up[0, :WIDTH], "w_up_b": s_up[0, WIDTH:], "w_out": s_wo[0],
    }
    weights = {"w_gate": (w_gate, m_w_gate, v_w_gate), "w_up_a": (w_up_a, m_w_up_a, v_w_up_a),
               "w_up_b": (w_up_b, m_w_up_b, v_w_up_b), "w_out": (w_out, m_w_out, v_w_out)}
    for wname, (w, m, v) in weights.items():
        g = grads[wname]
        delta, new_m, new_v = adamw(f"adamw_{wname}", w[0], g, m[0], v[0])
        big_out[wname] = (g[None], delta[None], new_m[None], new_v[None])
    rows_of = lambda a: jnp.transpose(a, (2, 0, 1))
    g_in_rows = s_ig[0, :in4][:, None, :]
    in_rows = adamw_rows("adamw_w_in", rows_of(w_in), g_in_rows, rows_of(m_w_in), rows_of(v_w_in))
    big_out["w_in"] = tuple(jnp.transpose(a, (1, 2, 0)) for a in (g_in_rows, *in_rows))

    rs_ffn1, token = reduce_scatter_3("rs_ffn1", rs_ffn1, c_arr, in_rows[0])
    (s_w3_1,) = reduce_scatter_4("rs_ffn1", rs_ffn1, token)
    adamw_ffn("ffn1", s_w3_1, ffn1_w, ffn1_m, ffn1_v, ())

    order = ["norm_ffn1", "w_ffn1_gate", "w_ffn1_up", "w_ffn1_down", "norm_mix", "w_in", "b_forget", "w_gate", "b_gate",
             "w_up_a", "w_up_b", "w_out", "norm_ffn2", "w_ffn2_gate", "w_ffn2_up", "w_ffn2_down", "norm_final"]
    outs = [loss, dx0[None]]
    for kind in range(4):
        for wname in order:
            outs.append(big_out[wname][kind] if wname in big_out else small_out[kind][wname])
    return tuple(outs)
```

```python
import functools

import jax
import jax.numpy as jnp
from jax import lax
from jax.experimental import pallas as pl
from jax.experimental.pallas import tpu as pltpu

F32 = jnp.float32
BF16 = jnp.bfloat16

HEAD_DIM = 64
N_HEADS = 8
WIDTH = N_HEADS * HEAD_DIM
QKV_COLS = 6 * WIDTH
RMS_EPS = 1e-6
ATTN_SCALE = HEAD_DIM ** -0.5
N_CHIPS = 4
QB = 128
BQ = 1024
CS = 256
N_SUB = BQ // CS
F_PAD = 256
NEG_BIG = -1e30

ADAM_LR = 0.001
ADAM_B1 = 0.9
ADAM_B2 = 0.999
ADAM_EPS = 1e-08
ADAM_WD = 0.01
ADAM_STEP = 10

VMEM_LIMIT_BYTES = 48 * 1024 * 1024
MESH = pl.DeviceIdType.MESH

NN = ((1,), (0,))
NT = ((1,), (1,))
TN = ((0,), (0,))


def _params(semantics):
    return pltpu.CompilerParams(dimension_semantics=semantics, vmem_limit_bytes=VMEM_LIMIT_BYTES)


def _dot(a, b, contract):
    return lax.dot_general(a.astype(BF16), b.astype(BF16), (contract, ((), ())), preferred_element_type=F32)


def _sigmoid(x):
    return 1.0 / (1.0 + jnp.exp(-x))


def _log1pexp_neg_abs(z):
    return jnp.log(1.0 + jnp.exp(-jnp.abs(z)))


def _split3(x):
    hi = x.astype(BF16)
    r1 = x - hi.astype(F32)
    mid = r1.astype(BF16)
    lo = (r1 - mid.astype(F32)).astype(BF16)
    return hi, mid, lo


def _dot_exact_rhs01(x, m01):
    hi, mid, lo = _split3(x)
    d = lambda p: lax.dot_general(p, m01, (NN, ((), ())), preferred_element_type=F32)
    return d(hi) + d(mid) + d(lo)


def _dot_exact_lhs01(m01, x):
    hi, mid, lo = _split3(x)
    d = lambda p: lax.dot_general(m01, p, (NN, ((), ())), preferred_element_type=F32)
    return d(hi) + d(mid) + d(lo)


def _iota2(shape, dim):
    return lax.broadcasted_iota(jnp.int32, shape, dim)


def _mm(name, pairs, contract, grid, pair_specs, out_shape, out_specs, acc_shape, nk, epilogue,
        extras=(), extra_specs=(), semantics=None):
    n_pairs = len(pairs)
    n_extra = len(extras)
    n_out = len(out_shape)

    def body(*refs):
        ab = refs[:2 * n_pairs]
        ex = refs[2 * n_pairs:2 * n_pairs + n_extra]
        outs = refs[2 * n_pairs + n_extra:2 * n_pairs + n_extra + n_out]
        ids = [pl.program_id(i) for i in range(len(grid))]
        k = ids[-1]
        part = _dot(ab[0][...], ab[1][...], contract)
        for p in range(1, n_pairs):
            part += _dot(ab[2 * p][...], ab[2 * p + 1][...], contract)
        if nk == 1:
            epilogue(part, ex, outs, ids)
            return
        acc = refs[-1]

        @pl.when(k == 0)
        def _():
            acc[...] = part

        @pl.when(k != 0)
        def _():
            acc[...] += part

        @pl.when(k == nk - 1)
        def _():
            epilogue(acc[...], ex, outs, ids)

    operands = [t for pair in pairs for t in pair] + list(extras)
    in_specs = [s for pair in pair_specs for s in pair] + list(extra_specs)
    if semantics is None:
        semantics = ("parallel",) * (len(grid) - 1) + ("arbitrary",)
    return pl.pallas_call(
        body, name=name, grid=grid, in_specs=in_specs, out_specs=list(out_specs), out_shape=list(out_shape),
        scratch_shapes=[] if nk == 1 else [pltpu.VMEM(acc_shape, F32)], compiler_params=_params(semantics),
    )(*operands)


def _ordered_after(body, n_in, n_after):
    def wrapped(*refs):
        return body(*refs[:n_in], *refs[n_in + n_after:])
    return wrapped


def _row_halves(rows):
    return (slice(0, rows // 2), slice(rows // 2, rows)) if rows % 32 == 0 else (slice(0, rows),)


def _row_tile(rows, target):
    t = min(rows, target)
    while rows % t:
        t //= 2
    return t


def rms_fwd(name, x, g, after=()):
    t, d = x.shape
    tr = _row_tile(t, 256)

    def body(x_ref, g_ref, h_ref, r_ref):
        xv = x_ref[...]
        r = lax.rsqrt(jnp.mean(xv * xv, axis=-1, keepdims=True) + RMS_EPS)
        h_ref[...] = (xv * r * g_ref[...]).astype(BF16)
        r_ref[...] = r

    return pl.pallas_call(
        _ordered_after(body, 2, len(after)), name=name, grid=(t // tr,),
        in_specs=[pl.BlockSpec((tr, d), lambda i: (i, 0)), pl.BlockSpec((1, d), lambda i: (0, 0))] + [_ANY] * len(after),
        out_specs=[pl.BlockSpec((tr, d), lambda i: (i, 0)), pl.BlockSpec((tr, 1), lambda i: (i, 0))],
        out_shape=[jax.ShapeDtypeStruct((t, d), BF16), jax.ShapeDtypeStruct((t, 1), F32)],
        compiler_params=_params(("parallel",)),
    )(x, g, *after)


GATE, UP, DOWN = 0, 1, 2


def _ffn_w_spec(which, f4, d, index_of_j):
    return pl.BlockSpec((None, None, f4, d), lambda *ids: (index_of_j(*ids), which, 0, 0))


def ffn_up(name, h, w3):
    t, d = h.shape
    ns, _, f4, _ = w3.shape
    tm = _row_tile(t, 512)

    def body(h_ref, wg_ref, wu_ref, a_ref, b_ref, s_ref):
        for rows in _row_halves(tm):
            hv = h_ref[rows, :]
            a = _dot(hv, wg_ref[...], NT)
            b = _dot(hv, wu_ref[...], NT)
            a_ref[rows, :] = a.astype(BF16)
            b_ref[rows, :] = b.astype(BF16)
            s_ref[rows, :] = (a * _sigmoid(a) * b).astype(BF16)

    act_spec = pl.BlockSpec((None, tm, f4), lambda j, m: (j, m, 0))
    return pl.pallas_call(
        body, name=name, grid=(ns, t // tm),
        in_specs=[pl.BlockSpec((tm, d), lambda j, m: (m, 0)),
                  _ffn_w_spec(GATE, f4, d, lambda j, m: j), _ffn_w_spec(UP, f4, d, lambda j, m: j)],
        out_specs=[act_spec, act_spec, act_spec],
        out_shape=[jax.ShapeDtypeStruct((ns, t, f4), BF16)] * 3,
        compiler_params=_params(("parallel", "parallel")),
    )(h, w3, w3)


def mm_residual(name, s, w, w_spec, x, scale):
    nj, t, kdim = s.shape
    n = x.shape[1]
    tm = _row_tile(t, 512)

    def body(s_ref, w_ref, x_ref, o_ref):
        acc = _dot(s_ref[0], w_ref[0], NN)
        for j in range(1, nj):
            acc += _dot(s_ref[j], w_ref[j], NN)
        o_ref[...] = x_ref[...] + scale * acc

    row = pl.BlockSpec((tm, n), lambda m: (m, 0))
    return pl.pallas_call(
        body, name=name, grid=(t // tm,),
        in_specs=[pl.BlockSpec((nj, tm, kdim), lambda m: (0, m, 0)), w_spec, row], out_specs=row,
        out_shape=jax.ShapeDtypeStruct((t, n), F32), compiler_params=_params(("parallel",)),
    )(s, w, x)


def ffn_bwd_act(name, dx, w3, a, b, after=()):
    t, d = dx.shape
    ns, _, f4, _ = w3.shape
    tm = _row_tile(t, 512)

    def body(dx_ref, wd_ref, a_ref, b_ref, da_ref, db_ref):
        for rows in _row_halves(tm):
            ds = _dot(0.5 * dx_ref[rows, :], wd_ref[...], NT)
            av = a_ref[rows, :].astype(F32)
            sig = _sigmoid(av)
            da_ref[rows, :] = (ds * b_ref[rows, :].astype(F32) * (sig * (1.0 + av * (1.0 - sig)))).astype(BF16)
            db_ref[rows, :] = (ds * (av * sig)).astype(BF16)

    act_spec = pl.BlockSpec((None, tm, f4), lambda j, m: (j, m, 0))
    return pl.pallas_call(
        _ordered_after(body, 4, len(after)), name=name, grid=(ns, t // tm),
        in_specs=[pl.BlockSpec((tm, d), lambda j, m: (m, 0)), _ffn_w_spec(DOWN, f4, d, lambda j, m: j), act_spec, act_spec]
        + [_ANY] * len(after),
        out_specs=[act_spec, act_spec],
        out_shape=[jax.ShapeDtypeStruct((ns, t, f4), BF16)] * 2,
        compiler_params=_params(("parallel", "parallel")),
    )(dx, w3, a, b, *after)


def ffn_wgrad(name, h, da, db, s, dx):
    t, d = h.shape
    ns, _, f4 = da.shape
    tk = _row_tile(t, 1024)
    nk = t // tk

    def body(h_ref, da_ref, db_ref, s_ref, dx_ref, o_ref, acc):
        k = pl.program_id(1)

        @pl.when(k == 0)
        def _():
            acc[...] = jnp.zeros_like(acc)

        hv = h_ref[...]
        acc[GATE] += _dot(da_ref[...], hv, TN)
        acc[UP] += _dot(db_ref[...], hv, TN)
        acc[DOWN] += _dot(s_ref[...], 0.5 * dx_ref[...], TN)

        @pl.when(k == nk - 1)
        def _():
            o_ref[...] = acc[...].astype(BF16)

    act_spec = pl.BlockSpec((None, tk, f4), lambda j, k: (j, k, 0))
    row_spec = pl.BlockSpec((tk, d), lambda j, k: (k, 0))
    return pl.pallas_call(
        body, name=name, grid=(ns, nk),
        in_specs=[row_spec, act_spec, act_spec, act_spec, row_spec],
        out_specs=pl.BlockSpec((None, 3, f4, d), lambda j, k: (j, 0, 0, 0)),
        out_shape=jax.ShapeDtypeStruct((ns, 3, f4, d), BF16),
        scratch_shapes=[pltpu.VMEM((3, f4, d), F32)],
        compiler_params=_params(("parallel", "arbitrary")),
    )(h, da, db, s, dx)


def _rms_bwd_tail(dh, x_ref, r_ref, g_ref, dxin_ref, dx_ref, gn_ref, row_tile_index):
    r = r_ref[...]
    xhat = x_ref[...] * r
    dhg = dh * g_ref[...]
    dx_ref[...] = dxin_ref[...] + r * (dhg - xhat * jnp.mean(dhg * xhat, axis=-1, keepdims=True))
    part = jnp.sum(dh * xhat, axis=0, keepdims=True)

    @pl.when(row_tile_index == 0)
    def _():
        gn_ref[...] = part

    @pl.when(row_tile_index != 0)
    def _():
        gn_ref[...] += part


def ffn_dh(name, da, db, w3, x, rstd, g, dx_in, after=()):
    ns, t, f4 = da.shape
    d = x.shape[1]
    tm = _row_tile(t, 256)

    def body(da_ref, db_ref, wg_ref, wu_ref, x_ref, r_ref, g_ref, dxin_ref, dx_ref, gn_ref):
        dh = _dot(da_ref[0], wg_ref[0], NN) + _dot(db_ref[0], wu_ref[0], NN)
        for j in range(1, ns):
            dh += _dot(da_ref[j], wg_ref[j], NN) + _dot(db_ref[j], wu_ref[j], NN)
        _rms_bwd_tail(dh, x_ref, r_ref, g_ref, dxin_ref, dx_ref, gn_ref, pl.program_id(0))

    act = pl.BlockSpec((ns, tm, f4), lambda m: (0, m, 0))
    row = pl.BlockSpec((tm, d), lambda m: (m, 0))
    gain = pl.BlockSpec((1, d), lambda m: (0, 0))
    return pl.pallas_call(
        _ordered_after(body, 8, len(after)), name=name, grid=(t // tm,),
        in_specs=[act, act, pl.BlockSpec((ns, None, f4, d), lambda m: (0, GATE, 0, 0)),
                  pl.BlockSpec((ns, None, f4, d), lambda m: (0, UP, 0, 0)), row, pl.BlockSpec((tm, 1), lambda m: (m, 0)), gain, row]
        + [_ANY] * len(after),
        out_specs=[row, gain], out_shape=[jax.ShapeDtypeStruct((t, d), F32), jax.ShapeDtypeStruct((1, d), F32)],
        compiler_params=_params(("arbitrary",)),
    )(da, db, w3, w3, x, rstd, g, dx_in, *after)


def dh_rms_bwd(name, pairs, pair_specs, contract, grid, nk, x, rstd, g, dx_in, after=()):
    t, d = x.shape
    tm = t // grid[0]

    def epilogue(acc, ex, outs, ids):
        _rms_bwd_tail(acc, *ex[:4], *outs, ids[0])

    row = pl.BlockSpec((tm, d), lambda m, k: (m, 0))
    return _mm(
        name, pairs, contract, grid, pair_specs,
        [jax.ShapeDtypeStruct((t, d), F32), jax.ShapeDtypeStruct((1, d), F32)],
        [row, pl.BlockSpec((1, d), lambda m, k: (0, 0))], (tm, d), nk, epilogue,
        extras=[x, rstd, g, dx_in, *after],
        extra_specs=[row, pl.BlockSpec((tm, 1), lambda m, k: (m, 0)), pl.BlockSpec((1, d), lambda m, k: (0, 0)), row]
        + [_ANY] * len(after),
        semantics=("arbitrary", "arbitrary"),
    )


def ffn_forward(tag, x, g_norm, w3, normed=None):
    _, _, f4, d = w3.shape
    h, rstd = normed if normed is not None else rms_fwd(f"{tag}_rms", x, g_norm)
    a, b, s = ffn_up(f"{tag}_up", h, w3)
    x_out = mm_residual(f"{tag}_down", s, w3, pl.BlockSpec((w3.shape[0], None, f4, d), lambda m: (0, DOWN, 0, 0)), x, 0.5)
    return x_out, (x, h, rstd, a, b, s)


def ffn_backward_weights(tag, dx, saved, w3, after=()):
    x, h, rstd, a, b, s = saved
    da, db = ffn_bwd_act(f"{tag}_bwd_act", dx, w3, a, b, after)
    return ffn_wgrad(f"{tag}_wgrad", h, da, db, s, dx), (da, db)


def ffn_backward_input(tag, dx, saved, dab, g_norm, w3, after=()):
    x, h, rstd, a, b, s = saved
    return ffn_dh(f"{tag}_dh", dab[0], dab[1], w3, x, rstd, g_norm, dx, after)


def proj(name, h, wcat_t, bias, first_col, n_cols, tn, out_dtype, scaled_tiles=()):
    t, d = h.shape
    tm = _row_tile(t, 512)
    off = first_col // tn

    def epilogue(acc, ex, outs, ids):
        val = acc + ex[0][...]
        if scaled_tiles:
            hit = functools.reduce(jnp.logical_or, [ids[0] == s for s in scaled_tiles])
            val = val * jnp.where(hit, ATTN_SCALE, 1.0)
        outs[0][...] = val.astype(out_dtype)

    return _mm(
        name, [(h, wcat_t)], NT, (n_cols // tn, t // tm, 1),
        [(pl.BlockSpec((tm, d), lambda j, m, k: (m, 0)), pl.BlockSpec((tn, d), lambda j, m, k: (off + j, 0)))],
        [jax.ShapeDtypeStruct((t, n_cols), out_dtype)], [pl.BlockSpec((tm, tn), lambda j, m, k: (m, j))], (tm, tn), 1, epilogue,
        extras=[bias], extra_specs=[pl.BlockSpec((1, tn), lambda j, m, k: (0, off + j))],
    )[0]


def mix_fwd(name, ya, yb, wup, pc):
    t, w = ya.shape
    ns, _, tn = wup.shape
    d = ns * tn
    tm = _row_tile(t, 512)

    def body(ya_ref, yb_ref, wa_ref, wb_ref, pa_ref, pb_ref, ua_ref, ub_ref, mx_ref):
        ua = _dot(ya_ref[...], wa_ref[...], NN)
        ub = _dot(yb_ref[...], wb_ref[...], NN)
        ua_ref[...] = ua
        ub_ref[...] = ub
        mx_ref[...] = (_sigmoid(pa_ref[...]) * ua + _sigmoid(pb_ref[...]) * ub).astype(BF16)

    y_spec = pl.BlockSpec((tm, w), lambda m, n: (m, 0))
    o_spec = pl.BlockSpec((tm, tn), lambda m, n: (m, n))
    return pl.pallas_call(
        body, name=name, grid=(t // tm, ns),
        in_specs=[y_spec, y_spec, pl.BlockSpec((None, w, tn), lambda m, n: (n, 0, 0)), pl.BlockSpec((None, w, tn), lambda m, n: (n, 1, 0)),
                  o_spec, pl.BlockSpec((tm, tn), lambda m, n: (m, ns + n))],
        out_specs=[o_spec, o_spec, o_spec],
        out_shape=[jax.ShapeDtypeStruct((t, d), F32), jax.ShapeDtypeStruct((t, d), F32), jax.ShapeDtypeStruct((t, d), BF16)],
        compiler_params=_params(("parallel", "parallel")),
    )(ya, yb, wup, wup, pc, pc)


def up_bwd(name, du, wup, branch):
    t, d = du.shape
    ns, w2, tn = wup.shape
    w = w2 // 2
    tm = _row_tile(t, 512)

    def body(du_ref, w_ref, o_ref):
        acc = _dot(du_ref[:, 0:tn], w_ref[0], NT)
        for j in range(1, ns):
            acc += _dot(du_ref[:, j * tn:(j + 1) * tn], w_ref[j], NT)
        o_ref[...] = acc.astype(BF16)

    return pl.pallas_call(
        body, name=name, grid=(t // tm,),
        in_specs=[pl.BlockSpec((tm, d), lambda m: (m, 0)), pl.BlockSpec((ns, w, tn), lambda m: (0, branch, 0))],
        out_specs=pl.BlockSpec((tm, w), lambda m: (m, 0)), out_shape=jax.ShapeDtypeStruct((t, w), BF16),
        compiler_params=_params(("parallel",)),
    )(du, wup)


def up_wgrad(name, ya, yb, dua, dub):
    t, w = ya.shape
    d = dua.shape[1]
    tn = d // N_CHIPS

    def body(ya_ref, yb_ref, dua_ref, dub_ref, o_ref):
        o_ref[0:w, :] = _dot(ya_ref[...], dua_ref[...], TN).astype(BF16)
        o_ref[w:2 * w, :] = _dot(yb_ref[...], dub_ref[...], TN).astype(BF16)

    y_spec = pl.BlockSpec((t, w), lambda j: (0, 0))
    du_spec = pl.BlockSpec((t, tn), lambda j: (0, j))
    return pl.pallas_call(
        body, name=name, grid=(N_CHIPS,), in_specs=[y_spec, y_spec, du_spec, du_spec],
        out_specs=pl.BlockSpec((None, 2 * w, tn), lambda j: (j, 0, 0)),
        out_shape=jax.ShapeDtypeStruct((N_CHIPS, 2 * w, tn), BF16), compiler_params=_params(("parallel",)),
    )(ya, yb, dua, dub)


def mix_bwd(name, dx, wo, pc, ua, ub, after=()):
    t, d = dx.shape
    tm = _row_tile(t, 512)
    tn = 512
    off_a = 0
    off_b = d // tn

    def body(dx_ref, wo_ref, pa_ref, pb_ref, ua_ref, ub_ref, dua_ref, dub_ref, dpa_ref, dpb_ref, ba_ref, bb_ref):
        dm = _dot(dx_ref[...], wo_ref[...], NT)
        ga = _sigmoid(pa_ref[...])
        gb = _sigmoid(pb_ref[...])
        dua_ref[...] = (dm * ga).astype(BF16)
        dub_ref[...] = (dm * gb).astype(BF16)
        dpa = dm * ua_ref[...] * ga * (1.0 - ga)
        dpb = dm * ub_ref[...] * gb * (1.0 - gb)
        dpa_ref[...] = dpa.astype(BF16)
        dpb_ref[...] = dpb.astype(BF16)
        sa = jnp.sum(dpa, axis=0, keepdims=True)
        sb = jnp.sum(dpb, axis=0, keepdims=True)

        @pl.when(pl.program_id(1) == 0)
        def _():
            ba_ref[...] = sa
            bb_ref[...] = sb

        @pl.when(pl.program_id(1) != 0)
        def _():
            ba_ref[...] += sa
            bb_ref[...] += sb

    tile = pl.BlockSpec((tm, tn), lambda n, m: (m, n))
    bias = pl.BlockSpec((1, tn), lambda n, m: (0, n))
    return pl.pallas_call(
        _ordered_after(body, 6, len(after)), name=name, grid=(d // tn, t // tm),
        in_specs=[pl.BlockSpec((tm, d), lambda n, m: (m, 0)), pl.BlockSpec((tn, d), lambda n, m: (n, 0)),
                  pl.BlockSpec((tm, tn), lambda n, m: (m, off_a + n)), pl.BlockSpec((tm, tn), lambda n, m: (m, off_b + n)),
                  tile, tile] + [_ANY] * len(after),
        out_specs=[tile, tile, tile, tile, bias, bias],
        out_shape=[jax.ShapeDtypeStruct((t, d), BF16)] * 4 + [jax.ShapeDtypeStruct((1, d), F32)] * 2,
        compiler_params=_params(("parallel", "arbitrary")),
    )(dx, wo, pc, pc, ua, ub, *after)


def mm_plain(name, a, b, contract, out_dtype, tk_target=512):
    if contract == NN:
        m, kdim = a.shape
        n = b.shape[1]
    elif contract == NT:
        m, kdim = a.shape
        n = b.shape[0]
    else:
        kdim, m = a.shape
        n = b.shape[1]
    tm = _row_tile(m, 512)
    tk = _row_tile(kdim, tk_target)
    nk = kdim // tk
    if contract == TN:
        a_spec = pl.BlockSpec((tk, tm), lambda i, k: (k, i))
    else:
        a_spec = pl.BlockSpec((tm, tk), lambda i, k: (i, k))
    if contract == NT:
        b_spec = pl.BlockSpec((n, tk), lambda i, k: (0, k))
    else:
        b_spec = pl.BlockSpec((tk, n), lambda i, k: (k, 0))

    def epilogue(acc, ex, outs, ids):
        outs[0][...] = acc.astype(out_dtype)

    return _mm(name, [(a, b)], contract, (m // tm, nk), [(a_spec, b_spec)],
               [jax.ShapeDtypeStruct((m, n), out_dtype)], [pl.BlockSpec((tm, n), lambda i, k: (i, 0))], (tm, n), nk, epilogue)[0]


def wgrad_cat(name, h, dcat):
    t, d = h.shape
    n = dcat.shape[1]
    tn = 768
    tk = _row_tile(t, 2048)

    def epilogue(acc, ex, outs, ids):
        outs[0][...] = acc.astype(BF16)

    return _mm(
        name, [(dcat, h)], TN, (n // tn, t // tk),
        [(pl.BlockSpec((tk, tn), lambda j, k: (k, j)), pl.BlockSpec((tk, d), lambda j, k: (k, 0)))],
        [jax.ShapeDtypeStruct((n, d), BF16)], [pl.BlockSpec((tn, d), lambda j, k: (j, 0))], (tn, d), t // tk, epilogue,
    )[0]


def fox_prep(name, f, bias):
    t, lanes = f.shape
    nchunk = t // QB

    def body(f_ref, b_ref, c_ref):
        lower = (_iota2((QB, QB), 1) <= _iota2((QB, QB), 0)).astype(BF16)

        def chunk(n, carry):
            rows = pl.ds(pl.multiple_of(n * QB, QB), QB)
            u = f_ref[rows, :] + b_ref[...]
            lf = jnp.minimum(u, 0.0) - _log1pexp_neg_abs(u)
            c = _dot_exact_lhs01(lower, lf) + carry
            c_ref[rows, :] = c
            return c[QB - 1:QB, :]

        lax.fori_loop(0, nchunk, chunk, jnp.zeros((1, lanes), F32))

    return pl.pallas_call(body, name=name, out_shape=jax.ShapeDtypeStruct((t, lanes), F32),
                          compiler_params=pltpu.CompilerParams(vmem_limit_bytes=VMEM_LIMIT_BYTES))(f, bias)


def fox_gate_bwd(name, dc, f, bias):
    t, lanes = dc.shape
    nchunk = t // QB

    def body(dc_ref, f_ref, b_ref, df_ref, gb_ref):
        upper = (_iota2((QB, QB), 1) >= _iota2((QB, QB), 0)).astype(BF16)

        def chunk(n, carry):
            tail, total = carry
            rows = pl.ds(pl.multiple_of((nchunk - 1 - n) * QB, QB), QB)
            dlf = _dot_exact_lhs01(upper, dc_ref[rows, :]) + tail
            u = f_ref[rows, :] + b_ref[...]
            df = dlf * jnp.exp(jnp.minimum(-u, 0.0) - _log1pexp_neg_abs(u))
            df_ref[rows, :] = df
            return dlf[0:1, :], total + jnp.sum(df, axis=0, keepdims=True)

        zero = jnp.zeros((1, lanes), F32)
        _, total = lax.fori_loop(0, nchunk, chunk, (zero, zero))
        gb_ref[...] = total

    return pl.pallas_call(body, name=name,
                          out_shape=[jax.ShapeDtypeStruct((t, lanes), F32), jax.ShapeDtypeStruct((1, lanes), F32)],
                          compiler_params=pltpu.CompilerParams(vmem_limit_bytes=VMEM_LIMIT_BYTES))(dc, f, bias)


def _qrows(i):
    return pl.ds(pl.multiple_of(i * BQ, BQ), BQ)


def _krows(kc):
    return pl.ds(pl.multiple_of(kc * CS, CS), CS)


def _head_spec(t, offset):
    return pl.BlockSpec((None, t, HEAD_DIM), lambda h: (offset + h, 0, 0))


def _head_t_spec(nq, offset):
    return pl.BlockSpec((None, nq, HEAD_DIM, BQ), lambda h: (offset + h, 0, 0, 0))


def _chunk_t_spec(nc):
    return pl.BlockSpec((None, nc, HEAD_DIM, CS), lambda h: (h, 0, 0, 0))


def _dot_split2_rhs01(x, m01):
    hi = x.astype(BF16)
    lo = (x - hi.astype(F32)).astype(BF16)
    d = lambda p: lax.dot_general(p, m01, (NN, ((), ())), preferred_element_type=F32)
    return d(hi) + d(lo)


def _live_rows(dchunk):
    return 0 if dchunk is None else dchunk * CS


def _diag_mask(dchunk, inclusive):
    shape = (BQ - _live_rows(dchunk), CS)
    return _iota2(shape, 1) <= _iota2(shape, 0) if inclusive else _iota2(shape, 1) < _iota2(shape, 0)


def _tail(x, r0, axis=0):
    return x if r0 == 0 else (x[r0:] if axis == 0 else x[:, r0:])


def _with_tail(old, tail, r0):
    return tail if r0 == 0 else jnp.concatenate([old[:r0], tail], axis=0)


def _walk_chunks(i, step, init, right_to_left):
    order = list(reversed(range(N_SUB))) if right_to_left else list(range(N_SUB))

    def diagonal(state):
        for dchunk in order:
            state = step(i * N_SUB + dchunk, state, dchunk)
        return state

    def group(n, state):
        base = ((i - 1 - n) if right_to_left else n) * N_SUB
        for dchunk in order:
            state = step(base + dchunk, state, None)
        return state

    if right_to_left:
        return lax.fori_loop(0, i, group, diagonal(init))
    return diagonal(lax.fori_loop(0, i, group, init))


def sb_fwd(name, qkv):
    t = qkv.shape[1]

    def body(q_ref, k_ref, v_ref, o_ref):
        later = (_iota2((CS, CS), 0) > _iota2((CS, CS), 1)).astype(BF16)

        def qblock(i, _):
            q = q_ref[_qrows(i), :]

            def step(kc, state, dchunk):
                carry, acc = state
                z = _dot(q, k_ref[_krows(kc), :], NT)
                sp = _log1pexp_neg_abs(z)
                lnb = -jnp.maximum(z, 0.0) - sp
                if dchunk is not None:
                    lnb = jnp.where(_diag_mask(dchunk, False), lnb, 0.0)
                w = jnp.exp(jnp.minimum(z, 0.0) - sp + _dot_split2_rhs01(lnb, later) + carry)
                if dchunk is not None:
                    w = jnp.where(_diag_mask(dchunk, False), w, 0.0)
                acc = acc + _dot(w, v_ref[_krows(kc), :], NN)
                return carry + jnp.sum(lnb, axis=1, keepdims=True), acc

            init = (jnp.zeros((BQ, 1), F32), jnp.zeros((BQ, HEAD_DIM), F32))
            _, acc = _walk_chunks(i, step, init, True)
            o_ref[_qrows(i), :] = acc.astype(BF16)
            return 0

        lax.fori_loop(0, t // BQ, qblock, 0)

    return pl.pallas_call(
        body, name=name, grid=(N_HEADS,),
        in_specs=[_head_spec(t, 0), _head_spec(t, N_HEADS), _head_spec(t, 2 * N_HEADS)],
        out_specs=_head_spec(t, 0), out_shape=jax.ShapeDtypeStruct((N_HEADS, t, HEAD_DIM), BF16),
        compiler_params=_params(("parallel",)),
    )(qkv, qkv, qkv)


def sb_bwd(name, qkv, qt, dy, dyt):
    t = qkv.shape[1]
    nq, nc = t // BQ, t // CS

    def body(q_ref, k_ref, v_ref, qt_ref, do_ref, dot_ref, dq_ref, dkt_ref, dvt_ref, g_s, b_s, dkt_acc, dvt_acc):
        later = (_iota2((CS, CS), 0) > _iota2((CS, CS), 1)).astype(BF16)
        earlier = (_iota2((CS, CS), 0) < _iota2((CS, CS), 1)).astype(BF16)
        dkt_acc[...] = jnp.zeros_like(dkt_acc)
        dvt_acc[...] = jnp.zeros_like(dvt_acc)

        def qblock(i, _):
            q = q_ref[_qrows(i), :]
            do = do_ref[_qrows(i), :]
            q_t = qt_ref[i]
            do_t = dot_ref[i]

            def step1(kc, carry, dchunk):
                z = _dot(q, k_ref[_krows(kc), :], NT)
                sp = _log1pexp_neg_abs(z)
                lnb = -jnp.maximum(z, 0.0) - sp
                lsz = jnp.minimum(z, 0.0) - sp
                if dchunk is not None:
                    lnb = jnp.where(_diag_mask(dchunk, False), lnb, 0.0)
                w = jnp.exp(lsz + _dot_split2_rhs01(lnb, later) + carry)
                if dchunk is not None:
                    w = jnp.where(_diag_mask(dchunk, False), w, 0.0)
                g_s[kc] = w * _dot(do, v_ref[_krows(kc), :], NT)
                b_s[kc] = jnp.exp(lsz)
                dvt_acc[kc] += _dot(do_t, w, NN)
                return carry + jnp.sum(lnb, axis=1, keepdims=True)

            _walk_chunks(i, step1, jnp.zeros((BQ, 1), F32), True)

            def step2(kc, state, dchunk):
                before, dq = state
                g = g_s[kc]
                beta = b_s[kc]
                dz = g * (1.0 - beta) - beta * (_dot_split2_rhs01(g, earlier) + before)
                if dchunk is not None:
                    dz = jnp.where(_diag_mask(dchunk, False), dz, 0.0)
                dzb = dz.astype(BF16)
                dq = dq + _dot(dzb, k_ref[_krows(kc), :], NN)
                dkt_acc[kc] += _dot(q_t, dzb, NN)
                return before + jnp.sum(g, axis=1, keepdims=True), dq

            _, dq = _walk_chunks(i, step2, (jnp.zeros((BQ, 1), F32), jnp.zeros((BQ, HEAD_DIM), F32)), False)
            dq_ref[_qrows(i), :] = (dq * ATTN_SCALE).astype(BF16)
            return 0

        lax.fori_loop(0, nq, qblock, 0)
        dkt_ref[...] = dkt_acc[...].astype(BF16)
        dvt_ref[...] = dvt_acc[...].astype(BF16)

    chunked = jax.ShapeDtypeStruct((N_HEADS, nc, HEAD_DIM, CS), BF16)
    return pl.pallas_call(
        body, name=name, grid=(N_HEADS,),
        in_specs=[_head_spec(t, 0), _head_spec(t, N_HEADS), _head_spec(t, 2 * N_HEADS), _head_t_spec(nq, 0),
                  _head_spec(t, 0), _head_t_spec(nq, 0)],
        out_specs=[_head_spec(t, 0), _chunk_t_spec(nc), _chunk_t_spec(nc)],
        out_shape=[jax.ShapeDtypeStruct((N_HEADS, t, HEAD_DIM), BF16), chunked, chunked],
        scratch_shapes=[pltpu.VMEM((nc, BQ, CS), F32), pltpu.VMEM((nc, BQ, CS), F32),
                        pltpu.VMEM((nc, HEAD_DIM, CS), F32), pltpu.VMEM((nc, HEAD_DIM, CS), F32)],
        compiler_params=_params(("parallel",)),
    )(qkv, qkv, qkv, qt, dy, dyt)


def _col_spec(t):
    return pl.BlockSpec((None, t, 1), lambda h: (h, 0, 0))


def _row_spec(nc):
    return pl.BlockSpec((None, nc, 1, CS), lambda h: (h, 0, 0, 0))


def fox_fwd(name, qkv, c_col, c_row):
    t = qkv.shape[1]

    def body(q_ref, k_ref, v_ref, cc_ref, cr_ref, o_ref, lse_ref):
        def qblock(i, _):
            q = q_ref[_qrows(i), :]
            ct = cc_ref[_qrows(i), :]

            def step(kc, state, dchunk):
                m, l, acc = state
                s = _dot(q, k_ref[_krows(kc), :], NT) + ct - cr_ref[kc]
                if dchunk is not None:
                    s = jnp.where(_diag_mask(dchunk, True), s, NEG_BIG)
                m_new = jnp.maximum(m, jnp.max(s, axis=1, keepdims=True))
                alpha = jnp.exp(m - m_new)
                p = jnp.exp(s - m_new)
                if dchunk is not None:
                    p = jnp.where(_diag_mask(dchunk, True), p, 0.0)
                l = l * alpha + jnp.sum(p, axis=1, keepdims=True)
                acc = acc * alpha + _dot(p, v_ref[_krows(kc), :], NN)
                return m_new, l, acc

            init = (jnp.full((BQ, 1), NEG_BIG, F32), jnp.zeros((BQ, 1), F32), jnp.zeros((BQ, HEAD_DIM), F32))
            m, l, acc = _walk_chunks(i, step, init, False)
            o_ref[_qrows(i), :] = (acc / l).astype(BF16)
            lse_ref[_qrows(i), :] = m + jnp.log(l)
            return 0

        lax.fori_loop(0, t // BQ, qblock, 0)

    return pl.pallas_call(
        body, name=name, grid=(N_HEADS,),
        in_specs=[_head_spec(t, 3 * N_HEADS), _head_spec(t, 4 * N_HEADS), _head_spec(t, 5 * N_HEADS),
                  _col_spec(t), _row_spec(t // CS)],
        out_specs=[_head_spec(t, 0), _col_spec(t)],
        out_shape=[jax.ShapeDtypeStruct((N_HEADS, t, HEAD_DIM), BF16), jax.ShapeDtypeStruct((N_HEADS, t, 1), F32)],
        compiler_params=_params(("parallel",)),
    )(qkv, qkv, qkv, c_col, c_row)


def fox_bwd(name, qkv, qt, y, dy, dyt, lse, c_col, c_row):
    t = qkv.shape[1]
    nq, nc = t // BQ, t // CS

    def body(q_ref, k_ref, v_ref, qt_ref, o_ref, do_ref, dot_ref, lse_ref, cc_ref, cr_ref,
             dq_ref, dkt_ref, dvt_ref, dcc_ref, dcr_ref, dkt_acc, dvt_acc, dcr_acc):
        dkt_acc[...] = jnp.zeros_like(dkt_acc)
        dvt_acc[...] = jnp.zeros_like(dvt_acc)
        dcr_acc[...] = jnp.zeros_like(dcr_acc)

        def qblock(i, _):
            q = q_ref[_qrows(i), :]
            do = do_ref[_qrows(i), :]
            q_t = qt_ref[i]
            do_t = dot_ref[i]
            ct = cc_ref[_qrows(i), :]
            lse_i = lse_ref[_qrows(i), :]
            delta = jnp.sum(do.astype(F32) * o_ref[_qrows(i), :].astype(F32), axis=1, keepdims=True)

            def step(kc, state, dchunk):
                dq, dct = state
                s = _dot(q, k_ref[_krows(kc), :], NT) + ct - cr_ref[kc]
                p = jnp.exp(s - lse_i)
                if dchunk is not None:
                    p = jnp.where(_diag_mask(dchunk, True), p, 0.0)
                ds = p * (_dot(do, v_ref[_krows(kc), :], NT) - delta)
                dvt_acc[kc] += _dot(do_t, p, NN)
                dsb = ds.astype(BF16)
                dq = dq + _dot(dsb, k_ref[_krows(kc), :], NN)
                dkt_acc[kc] += _dot(q_t, dsb, NN)
                dcr_acc[kc] -= jnp.sum(ds, axis=0, keepdims=True)
                return dq, dct + jnp.sum(ds, axis=1, keepdims=True)

            dq, dct = _walk_chunks(i, step, (jnp.zeros((BQ, HEAD_DIM), F32), jnp.zeros((BQ, 1), F32)), False)
            dq_ref[_qrows(i), :] = (dq * ATTN_SCALE).astype(BF16)
            dcc_ref[_qrows(i), :] = dct
            return 0

        lax.fori_loop(0, nq, qblock, 0)
        dkt_ref[...] = dkt_acc[...].astype(BF16)
        dvt_ref[...] = dvt_acc[...].astype(BF16)
        dcr_ref[...] = dcr_acc[...]

    chunked = jax.ShapeDtypeStruct((N_HEADS, nc, HEAD_DIM, CS), BF16)
    return pl.pallas_call(
        body, name=name, grid=(N_HEADS,),
        in_specs=[_head_spec(t, 3 * N_HEADS), _head_spec(t, 4 * N_HEADS), _head_spec(t, 5 * N_HEADS), _head_t_spec(nq, N_HEADS),
                  _head_spec(t, 0), _head_spec(t, 0), _head_t_spec(nq, 0), _col_spec(t), _col_spec(t), _row_spec(nc)],
        out_specs=[_head_spec(t, 0), _chunk_t_spec(nc), _chunk_t_spec(nc), _col_spec(t), _row_spec(nc)],
        out_shape=[jax.ShapeDtypeStruct((N_HEADS, t, HEAD_DIM), BF16), chunked, chunked,
                   jax.ShapeDtypeStruct((N_HEADS, t, 1), F32), jax.ShapeDtypeStruct((N_HEADS, nc, 1, CS), F32)],
        scratch_shapes=[pltpu.VMEM((nc, HEAD_DIM, CS), F32), pltpu.VMEM((nc, HEAD_DIM, CS), F32), pltpu.VMEM((nc, 1, CS), F32)],
        compiler_params=_params(("parallel",)),
    )(qkv, qkv, qkv, qt, y, dy, dyt, lse, c_col, c_row)


PAIR = 2 * HEAD_DIM
N_PAIRS = N_HEADS // 2


def _pair_spec(t, first_block):
    return pl.BlockSpec((t, PAIR), lambda p, *_: (0, first_block + p))


def _head_lanes(shape):
    lane = _iota2(shape, len(shape) - 1)
    return [lane < HEAD_DIM, lane >= HEAD_DIM]


def _only_head(x, lanes_of_head):
    return jnp.where(lanes_of_head, x, jnp.zeros_like(x))


LOG2_E = 1.4426950408889634


def _sb_chunk_weights(q_h, k, later, carry, dchunk):
    z = _dot(q_h, k, NT) * LOG2_E
    lnb = -jnp.maximum(z, 0.0) - jnp.log2(1.0 + jnp.exp2(-jnp.abs(z)))
    lsz = lnb + z
    if dchunk is not None:
        lnb = jnp.where(_diag_mask(dchunk, False), lnb, 0.0)
    w = jnp.exp2(lsz + _dot_split2_rhs01(lnb, later) + carry)
    if dchunk is not None:
        w = jnp.where(_diag_mask(dchunk, False), w, 0.0)
    return w, lsz, lnb


def sb_pair_fwd(name, qkv):
    t = qkv.shape[0]

    def body(q_ref, k_ref, v_ref, o_ref):
        later = (_iota2((CS, CS), 0) > _iota2((CS, CS), 1)).astype(BF16)
        lanes = _head_lanes((BQ, PAIR))

        def qblock(i, _):
            q = q_ref[_qrows(i), :]
            q_heads = [_only_head(q, lanes[h]) for h in range(2)]

            def step(kc, state, dchunk):
                k = k_ref[_krows(kc), :]
                v = v_ref[_krows(kc), :]
                out = []
                r0 = _live_rows(dchunk)
                for h in range(2):
                    carry, acc = state[h]
                    w, _, lnb = _sb_chunk_weights(_tail(q_heads[h], r0), k, later, _tail(carry, r0), dchunk)
                    out.append((_with_tail(carry, _tail(carry, r0) + jnp.sum(lnb, axis=1, keepdims=True), r0),
                                _with_tail(acc, _tail(acc, r0) + _dot(w, v, NN), r0)))
                return tuple(out)

            zero = (jnp.zeros((BQ, 1), F32), jnp.zeros((BQ, PAIR), F32))
            (_, acc0), (_, acc1) = _walk_chunks(i, step, (zero, zero), True)
            o_ref[_qrows(i), :] = jnp.where(lanes[0], acc0, acc1).astype(BF16)
            return 0

        lax.fori_loop(0, t // BQ, qblock, 0)

    return pl.pallas_call(
        body, name=name, grid=(N_PAIRS,),
        in_specs=[_pair_spec(t, 0), _pair_spec(t, N_PAIRS), _pair_spec(t, 2 * N_PAIRS)],
        out_specs=_pair_spec(t, 0), out_shape=jax.ShapeDtypeStruct((t, WIDTH), BF16),
        compiler_params=_params(("parallel",)),
    )(qkv, qkv, qkv)


def _emit_dqkv(res, o_ref):
    o_ref[...] = res[pl.program_id(1)]


def _flush_transposed(acc, res, which):
    for kc in range(acc.shape[0]):
        res[which, kc * CS:(kc + 1) * CS, :] = acc[kc].T.astype(BF16)


def sb_pair_bwd(name, qkv, dy, dqkv):
    t = qkv.shape[0]
    nc = t // CS

    def body(q_ref, k_ref, v_ref, do_ref, _, o_ref, g_s, b_s, dkt_acc, dvt_acc, res):
        @pl.when(pl.program_id(1) == 0)
        def _():
            later = (_iota2((CS, CS), 0) > _iota2((CS, CS), 1)).astype(BF16)
            earlier = (_iota2((CS, CS), 0) < _iota2((CS, CS), 1)).astype(BF16)
            lanes = _head_lanes((BQ, PAIR))
            dkt_acc[...] = jnp.zeros_like(dkt_acc)
            dvt_acc[...] = jnp.zeros_like(dvt_acc)

            def qblock(i, _):
                q = q_ref[_qrows(i), :]
                do = do_ref[_qrows(i), :]
                q_heads = [_only_head(q, lanes[h]) for h in range(2)]
                do_heads = [_only_head(do, lanes[h]) for h in range(2)]
                qt_heads = [qh.astype(F32).T.astype(BF16) for qh in q_heads]
                dot_heads = [dh.astype(F32).T.astype(BF16) for dh in do_heads]

                def step1(kc, carries, dchunk):
                    k = k_ref[_krows(kc), :]
                    v = v_ref[_krows(kc), :]
                    out = []
                    r0 = _live_rows(dchunk)
                    for h in range(2):
                        carry = _tail(carries[h], r0)
                        w, lsz, lnb = _sb_chunk_weights(_tail(q_heads[h], r0), k, later, carry, dchunk)
                        g_s[h, kc, r0:, :] = (w * _dot(_tail(do_heads[h], r0), v, NT)).astype(BF16)
                        b_s[h, kc, r0:, :] = jnp.exp2(lsz).astype(BF16)
                        dvt_acc[kc] += _dot(_tail(dot_heads[h], r0, axis=1), w, NN)
                        out.append(_with_tail(carries[h], carry + jnp.sum(lnb, axis=1, keepdims=True), r0))
                    return tuple(out)

                zero = jnp.zeros((BQ, 1), F32)
                _walk_chunks(i, step1, (zero, zero), True)

                def step2(kc, state, dchunk):
                    k = k_ref[_krows(kc), :]
                    out = []
                    r0 = _live_rows(dchunk)
                    for h in range(2):
                        before, dq = state[h]
                        g16 = g_s[h, kc, r0:, :]
                        g = g16.astype(F32)
                        beta = b_s[h, kc, r0:, :].astype(F32)
                        prefix = lax.dot_general(g16, earlier, (NN, ((), ())), preferred_element_type=F32) + _tail(before, r0)
                        dz = g * (1.0 - beta) - beta * prefix
                        if dchunk is not None:
                            dz = jnp.where(_diag_mask(dchunk, False), dz, 0.0)
                        dzb = dz.astype(BF16)
                        dkt_acc[kc] += _dot(_tail(qt_heads[h], r0, axis=1), dzb, NN)
                        out.append((_with_tail(before, _tail(before, r0) + jnp.sum(g, axis=1, keepdims=True), r0),
                                    _with_tail(dq, _tail(dq, r0) + _dot(dzb, k, NN), r0)))
                    return tuple(out)

                start = (zero, jnp.zeros((BQ, PAIR), F32))
                (_, dq0), (_, dq1) = _walk_chunks(i, step2, (start, start), False)
                res[0, _qrows(i), :] = (jnp.where(lanes[0], dq0, dq1) * ATTN_SCALE).astype(BF16)
                return 0

            lax.fori_loop(0, t // BQ, qblock, 0)
            _flush_transposed(dkt_acc, res, 1)
            _flush_transposed(dvt_acc, res, 2)

        _emit_dqkv(res, o_ref)

    return pl.pallas_call(
        body, name=name, grid=(N_PAIRS, 3),
        in_specs=[_pair_spec(t, 0), _pair_spec(t, N_PAIRS), _pair_spec(t, 2 * N_PAIRS), _pair_spec(t, 0), _ANY],
        out_specs=pl.BlockSpec((t, PAIR), lambda p, s: (0, s * N_PAIRS + p)),
        out_shape=jax.ShapeDtypeStruct(dqkv.shape, BF16), input_output_aliases={4: 0},
        scratch_shapes=[pltpu.VMEM((2, nc, BQ, CS), BF16), pltpu.VMEM((2, nc, BQ, CS), BF16),
                        pltpu.VMEM((nc, PAIR, CS), F32), pltpu.VMEM((nc, PAIR, CS), F32), pltpu.VMEM((3, t, PAIR), BF16)],
        compiler_params=_params(("parallel", "arbitrary")),
    )(qkv, qkv, qkv, dy, dqkv)


def _gates_col_spec(t):
    return pl.BlockSpec((2, t, 1), lambda p, *_: (p, 0, 0))


def _gates_row_spec(nc):
    return pl.BlockSpec((2, nc, 1, CS), lambda p, *_: (p, 0, 0, 0))


def fox_pair_fwd(name, qkv, c_col, c_row):
    t = qkv.shape[0]

    def body(q_ref, k_ref, v_ref, cc_ref, cr_ref, o_ref, lse_ref):
        lanes = _head_lanes((BQ, PAIR))

        def qblock(i, _):
            q = q_ref[_qrows(i), :]
            q_heads = [_only_head(q, lanes[h]) for h in range(2)]
            ct = [cc_ref[h, _qrows(i), :] for h in range(2)]

            def step(kc, state, dchunk):
                k = k_ref[_krows(kc), :]
                v = v_ref[_krows(kc), :]
                out = []
                r0 = _live_rows(dchunk)
                for h in range(2):
                    m, l, acc = (_tail(a, r0) for a in state[h])
                    s = _dot(_tail(q_heads[h], r0), k, NT) + _tail(ct[h], r0) - cr_ref[h, kc]
                    if dchunk is not None:
                        s = jnp.where(_diag_mask(dchunk, True), s, NEG_BIG)
                    m_new = jnp.maximum(m, jnp.max(s, axis=1, keepdims=True))
                    alpha = jnp.exp(m - m_new)
                    p = jnp.exp(s - m_new)
                    if dchunk is not None:
                        p = jnp.where(_diag_mask(dchunk, True), p, 0.0)
                    new = (m_new, l * alpha + jnp.sum(p, axis=1, keepdims=True), acc * alpha + _dot(p, v, NN))
                    out.append(tuple(_with_tail(old, tail, r0) for old, tail in zip(state[h], new)))
                return tuple(out)

            init = (jnp.full((BQ, 1), NEG_BIG, F32), jnp.zeros((BQ, 1), F32), jnp.zeros((BQ, PAIR), F32))
            (m0, l0, acc0), (m1, l1, acc1) = _walk_chunks(i, step, (init, init), False)
            o_ref[_qrows(i), :] = jnp.where(lanes[0], acc0 / l0, acc1 / l1).astype(BF16)
            lse_ref[0, _qrows(i), :] = m0 + jnp.log(l0)
            lse_ref[1, _qrows(i), :] = m1 + jnp.log(l1)
            return 0

        lax.fori_loop(0, t // BQ, qblock, 0)

    return pl.pallas_call(
        body, name=name, grid=(N_PAIRS,),
        in_specs=[_pair_spec(t, 3 * N_PAIRS), _pair_spec(t, 4 * N_PAIRS), _pair_spec(t, 5 * N_PAIRS),
                  _gates_col_spec(t), _gates_row_spec(t // CS)],
        out_specs=[_pair_spec(t, 0), _gates_col_spec(t)],
        out_shape=[jax.ShapeDtypeStruct((t, WIDTH), BF16), jax.ShapeDtypeStruct((N_HEADS, t, 1), F32)],
        compiler_params=_params(("parallel",)),
    )(qkv, qkv, qkv, c_col, c_row)


def fox_pair_bwd(name, qkv, y, dy, lse, c_col, c_row, dqkv):
    t = qkv.shape[0]
    nc = t // CS

    def body(q_ref, k_ref, v_ref, o_in_ref, do_ref, lse_ref, cc_ref, cr_ref, _, o_ref, dcc_ref, dcr_ref,
             dkt_acc, dvt_acc, dcr_acc, res):
        @pl.when(pl.program_id(1) == 0)
        def _():
            lanes = _head_lanes((BQ, PAIR))
            dkt_acc[...] = jnp.zeros_like(dkt_acc)
            dvt_acc[...] = jnp.zeros_like(dvt_acc)
            dcr_acc[...] = jnp.zeros_like(dcr_acc)

            def qblock(i, _):
                q = q_ref[_qrows(i), :]
                do = do_ref[_qrows(i), :]
                q_heads = [_only_head(q, lanes[h]) for h in range(2)]
                do_heads = [_only_head(do, lanes[h]) for h in range(2)]
                qt_heads = [qh.astype(F32).T.astype(BF16) for qh in q_heads]
                dot_heads = [dh.astype(F32).T.astype(BF16) for dh in do_heads]
                prod = do.astype(F32) * o_in_ref[_qrows(i), :].astype(F32)
                delta = [jnp.sum(_only_head(prod, lanes[h]), axis=1, keepdims=True) for h in range(2)]
                ct = [cc_ref[h, _qrows(i), :] for h in range(2)]
                lse_i = [lse_ref[h, _qrows(i), :] for h in range(2)]

                def step(kc, state, dchunk):
                    k = k_ref[_krows(kc), :]
                    v = v_ref[_krows(kc), :]
                    out = []
                    r0 = _live_rows(dchunk)
                    for h in range(2):
                        dq, dct = state[h]
                        s = _dot(_tail(q_heads[h], r0), k, NT) + _tail(ct[h], r0) - cr_ref[h, kc]
                        p = jnp.exp(s - _tail(lse_i[h], r0))
                        if dchunk is not None:
                            p = jnp.where(_diag_mask(dchunk, True), p, 0.0)
                        ds = p * (_dot(_tail(do_heads[h], r0), v, NT) - _tail(delta[h], r0))
                        dvt_acc[kc] += _dot(_tail(dot_heads[h], r0, axis=1), p, NN)
                        dsb = ds.astype(BF16)
                        dkt_acc[kc] += _dot(_tail(qt_heads[h], r0, axis=1), dsb, NN)
                        dcr_acc[h, kc] -= jnp.sum(ds, axis=0, keepdims=True)
                        out.append((_with_tail(dq, _tail(dq, r0) + _dot(dsb, k, NN), r0),
                                    _with_tail(dct, _tail(dct, r0) + jnp.sum(ds, axis=1, keepdims=True), r0)))
                    return tuple(out)

                zero = (jnp.zeros((BQ, PAIR), F32), jnp.zeros((BQ, 1), F32))
                (dq0, dct0), (dq1, dct1) = _walk_chunks(i, step, (zero, zero), False)
                res[0, _qrows(i), :] = (jnp.where(lanes[0], dq0, dq1) * ATTN_SCALE).astype(BF16)
                dcc_ref[0, _qrows(i), :] = dct0
                dcc_ref[1, _qrows(i), :] = dct1
                return 0

            lax.fori_loop(0, t // BQ, qblock, 0)
            _flush_transposed(dkt_acc, res, 1)
            _flush_transposed(dvt_acc, res, 2)
            dcr_ref[...] = dcr_acc[...]

        _emit_dqkv(res, o_ref)

    return pl.pallas_call(
        body, name=name, grid=(N_PAIRS, 3),
        in_specs=[_pair_spec(t, 3 * N_PAIRS), _pair_spec(t, 4 * N_PAIRS), _pair_spec(t, 5 * N_PAIRS), _pair_spec(t, 0),
                  _pair_spec(t, 0), _gates_col_spec(t), _gates_col_spec(t), _gates_row_spec(nc), _ANY],
        out_specs=[pl.BlockSpec((t, PAIR), lambda p, s: (0, (3 + s) * N_PAIRS + p)), _gates_col_spec(t), _gates_row_spec(nc)],
        out_shape=[jax.ShapeDtypeStruct(dqkv.shape, BF16), jax.ShapeDtypeStruct((N_HEADS, t, 1), F32),
                   jax.ShapeDtypeStruct((N_HEADS, nc, 1, CS), F32)],
        input_output_aliases={8: 0},
        scratch_shapes=[pltpu.VMEM((nc, PAIR, CS), F32), pltpu.VMEM((nc, PAIR, CS), F32), pltpu.VMEM((2, nc, 1, CS), F32),
                        pltpu.VMEM((3, t, PAIR), BF16)],
        compiler_params=_params(("parallel", "arbitrary")),
    )(qkv, qkv, qkv, y, dy, lse, c_col, c_row, dqkv)


def loss_head(name, x, g, target):
    t, d = x.shape
    tr = _row_tile(t, 256)

    def body(x_ref, g_ref, t_ref, dx_ref, gn_ref, loss_ref):
        xv = x_ref[...]
        r = lax.rsqrt(jnp.mean(xv * xv, axis=-1, keepdims=True) + RMS_EPS)
        xhat = xv * r
        gv = g_ref[...]
        err = xhat * gv - t_ref[...]
        part_loss = 0.5 * jnp.sum(jnp.mean(err * err, axis=-1, keepdims=True), axis=0, keepdims=True)
        dy = err * (1.0 / d)
        dyg = dy * gv
        dx_ref[...] = r * (dyg - xhat * jnp.mean(dyg * xhat, axis=-1, keepdims=True))
        part_g = jnp.sum(dy * xhat, axis=0, keepdims=True)

        @pl.when(pl.program_id(0) == 0)
        def _():
            gn_ref[...] = part_g
            loss_ref[...] = part_loss

        @pl.when(pl.program_id(0) != 0)
        def _():
            gn_ref[...] += part_g
            loss_ref[...] += part_loss

    row = pl.BlockSpec((tr, d), lambda i: (i, 0))
    return pl.pallas_call(
        body, name=name, grid=(t // tr,),
        in_specs=[row, pl.BlockSpec((1, d), lambda i: (0, 0)), row],
        out_specs=[row, pl.BlockSpec((1, d), lambda i: (0, 0)), pl.BlockSpec((1, 1), lambda i: (0, 0))],
        out_shape=[jax.ShapeDtypeStruct((t, d), F32), jax.ShapeDtypeStruct((1, d), F32), jax.ShapeDtypeStruct((1, 1), F32)],
        compiler_params=_params(("arbitrary",)),
    )(x, g, target)


def _place():
    return lax.axis_index("x"), lax.axis_index("y"), lax.axis_index("c")


def _other_chips(x, y):
    return [(1 - x, y), (x, 1 - y), (1 - x, 1 - y)]


def _half(ref, c, rows):
    return ref.at[:, pl.ds(c * (rows // 2), rows // 2), :]


_ANY = pl.BlockSpec(memory_space=pl.ANY)


def gather_weights(name, bufs):
    n = len(bufs)

    def body(*refs):
        outs = refs[n:2 * n]
        send_sems, recv_sems = refs[2 * n:]
        x, y, c = _place()
        chips = _other_chips(x, y)
        me = 2 * x + y
        sibling = (x, y, 1 - c)
        first, passed = [], []
        for i in range(n):
            rows = outs[i].shape[2]
            mine = _half(outs[i].at[me], c, rows)
            for j, (qx, qy) in enumerate(chips):
                k = 6 * i + j
                rc = pltpu.make_async_remote_copy(
                    src_ref=mine, dst_ref=mine,
                    send_sem=send_sems.at[k], recv_sem=recv_sems.at[k], device_id=(qx, qy, c), device_id_type=MESH)
                rc.start()
                first.append(rc)
        for i in range(n):
            rows = outs[i].shape[2]
            for j, (qx, qy) in enumerate(chips):
                k = 6 * i + j
                block = _half(outs[i].at[2 * qx + qy], c, rows)
                pltpu.make_async_remote_copy(
                    src_ref=block, dst_ref=block, send_sem=send_sems.at[k], recv_sem=recv_sems.at[k],
                    device_id=(qx, qy, c), device_id_type=MESH).wait_recv()
                fw = pltpu.make_async_remote_copy(
                    src_ref=block, dst_ref=block, send_sem=send_sems.at[k + 3], recv_sem=recv_sems.at[k + 3],
                    device_id=sibling, device_id_type=MESH)
                fw.start()
                passed.append(fw)
        for i in range(n):
            rows = outs[i].shape[2]
            for j, (qx, qy) in enumerate(chips):
                k = 6 * i + j + 3
                block = _half(outs[i].at[2 * qx + qy], 1 - c, rows)
                pltpu.make_async_remote_copy(
                    src_ref=block, dst_ref=block, send_sem=send_sems.at[k], recv_sem=recv_sems.at[k],
                    device_id=sibling, device_id_type=MESH).wait_recv()
        for cp in first + passed:
            cp.wait_send()

    return pl.pallas_call(
        body, name=name, in_specs=[_ANY] * n, out_specs=[_ANY] * n,
        out_shape=[jax.ShapeDtypeStruct(b.shape, b.dtype) for b in bufs],
        input_output_aliases={i: i for i in range(n)},
        scratch_shapes=[pltpu.SemaphoreType.DMA((6 * n,)), pltpu.SemaphoreType.DMA((6 * n,))],
        compiler_params=pltpu.CompilerParams(has_side_effects=True),
    )(*bufs)


_HBM = pl.BlockSpec(memory_space=pltpu.HBM)
_SEM = pl.BlockSpec(memory_space=pltpu.SEMAPHORE)
_DATAFLOW = pltpu.SideEffectType.DATAFLOW_SIDE_EFFECTING


def _in_hbm(a):
    return pltpu.with_memory_space_constraint(a, pltpu.HBM)


def _gather_ici_copies(bufs, send_sems, recv_sems, arrivals):
    x, y, c = _place()
    me = 2 * x + y
    copies = []
    for i, buf in enumerate(bufs):
        rows = buf.shape[2]
        for j, (qx, qy) in enumerate(_other_chips(x, y)):
            block = _half(buf.at[2 * qx + qy if arrivals else me], c, rows)
            copies.append(pltpu.make_async_remote_copy(
                src_ref=block, dst_ref=block, send_sem=send_sems.at[3 * i + j], recv_sem=recv_sems.at[3 * i + j],
                device_id=(qx, qy, c), device_id_type=MESH))
    return copies


def gather_ici_start(name, bufs, after):
    n = len(bufs)

    def body(*refs):
        ins = refs[:n]
        send_sems, recv_sems = refs[n + 1], refs[n + 2]
        token = refs[-1]
        for send in _gather_ici_copies(ins, send_sems, recv_sems, False):
            send.start()
        token[...] = jnp.zeros_like(token)

    res = pl.pallas_call(
        body, name=name,
        out_shape=(pltpu.SemaphoreType.DMA((3 * n,)), pltpu.SemaphoreType.DMA((3 * n,)), *[pltpu.HBM(b.shape, b.dtype) for b in bufs],
                   jax.ShapeDtypeStruct((8, 128), F32)),
        in_specs=[_HBM] * n + [_ANY], out_specs=(_SEM, _SEM, *[_HBM] * n, pl.BlockSpec(memory_space=pltpu.VMEM)),
        input_output_aliases={i: 2 + i for i in range(n)},
        compiler_params=pltpu.CompilerParams(has_side_effects=_DATAFLOW),
    )(*[_in_hbm(b) for b in bufs], after)
    return res[0], res[1], list(res[2:2 + n]), res[-1]


def gather_ici_wait(name, send_sems, recv_sems, bufs, after):
    n = len(bufs)

    def body(*refs):
        ins = refs[:n]
        send_sems_ref, recv_sems_ref = refs[n], refs[n + 1]
        for send in _gather_ici_copies(ins, send_sems_ref, recv_sems_ref, False):
            send.wait_send()
        for recv in _gather_ici_copies(ins, send_sems_ref, recv_sems_ref, True):
            recv.wait_recv()

    return pl.pallas_call(
        body, name=name, out_shape=tuple(pltpu.HBM(b.shape, b.dtype) for b in bufs),
        in_specs=[_HBM] * n + [_SEM, _SEM, _ANY], out_specs=tuple([_HBM] * n),
        input_output_aliases={i: i for i in range(n)},
        compiler_params=pltpu.CompilerParams(has_side_effects=_DATAFLOW),
    )(*bufs, send_sems, recv_sems, after)


def gather_forward(name, bufs):
    n = len(bufs)

    def body(*refs):
        outs = refs[n:2 * n]
        send_sems, recv_sems = refs[2 * n:]
        x, y, c = _place()
        sibling = (x, y, 1 - c)
        sends = []
        for i in range(n):
            rows = outs[i].shape[2]
            for j, (qx, qy) in enumerate(_other_chips(x, y)):
                block = _half(outs[i].at[2 * qx + qy], c, rows)
                fw = pltpu.make_async_remote_copy(
                    src_ref=block, dst_ref=block, send_sem=send_sems.at[3 * i + j], recv_sem=recv_sems.at[3 * i + j],
                    device_id=sibling, device_id_type=MESH)
                fw.start()
                sends.append(fw)
        for i in range(n):
            rows = outs[i].shape[2]
            for j, (qx, qy) in enumerate(_other_chips(x, y)):
                block = _half(outs[i].at[2 * qx + qy], 1 - c, rows)
                pltpu.make_async_remote_copy(
                    src_ref=block, dst_ref=block, send_sem=send_sems.at[3 * i + j], recv_sem=recv_sems.at[3 * i + j],
                    device_id=sibling, device_id_type=MESH).wait_recv()
        for fw in sends:
            fw.wait_send()

    return pl.pallas_call(
        body, name=name, in_specs=[_ANY] * n, out_specs=[_ANY] * n,
        out_shape=[jax.ShapeDtypeStruct(b.shape, b.dtype) for b in bufs],
        input_output_aliases={i: i for i in range(n)},
        scratch_shapes=[pltpu.SemaphoreType.DMA((3 * n,)), pltpu.SemaphoreType.DMA((3 * n,))],
        compiler_params=pltpu.CompilerParams(has_side_effects=True),
    )(*bufs)


def _between_chips_copies(parts, lands, send_sems, recv_sems):
    x, y, c = _place()
    copies = []
    for i, (part, land) in enumerate(zip(parts, lands)):
        for j, (qx, qy) in enumerate(_other_chips(x, y)):
            copies.append(pltpu.make_async_remote_copy(
                src_ref=part.at[2 * qx + qy], dst_ref=land.at[j], send_sem=send_sems.at[3 * i + j], recv_sem=recv_sems.at[3 * i + j],
                device_id=(qx, qy, c), device_id_type=MESH))
    return copies


def between_chips_start(name, parts):
    n = len(parts)
    lands = [lax.empty((N_CHIPS - 1,) + p.shape[1:], p.dtype) for p in parts]

    def body(*refs):
        send_sems, recv_sems = refs[2 * n], refs[2 * n + 1]
        token = refs[-1]
        for cp in _between_chips_copies(refs[:n], refs[n:2 * n], send_sems, recv_sems):
            cp.start()
        token[...] = jnp.zeros_like(token)

    res = pl.pallas_call(
        body, name=name,
        out_shape=(pltpu.SemaphoreType.DMA((3 * n,)), pltpu.SemaphoreType.DMA((3 * n,)),
                   *[pltpu.HBM(a.shape, a.dtype) for a in parts + lands], jax.ShapeDtypeStruct((8, 128), F32)),
        in_specs=[_HBM] * (2 * n), out_specs=(_SEM, _SEM, *[_HBM] * (2 * n), pl.BlockSpec(memory_space=pltpu.VMEM)),
        input_output_aliases={i: 2 + i for i in range(2 * n)},
        compiler_params=pltpu.CompilerParams(has_side_effects=_DATAFLOW),
    )(*[_in_hbm(a) for a in parts + lands])
    return res[0], res[1], list(res[2:2 + n]), list(res[2 + n:2 + 2 * n]), res[-1]


def between_chips_wait(name, send_sems, recv_sems, parts, lands, after):
    n = len(parts)

    def body(*refs):
        for cp in _between_chips_copies(refs[:n], refs[n:2 * n], refs[2 * n], refs[2 * n + 1]):
            cp.wait_send()
            cp.wait_recv()

    res = pl.pallas_call(
        body, name=name, out_shape=tuple(pltpu.HBM(a.shape, a.dtype) for a in parts + lands),
        in_specs=[_HBM] * (2 * n) + [_SEM, _SEM, _ANY], out_specs=tuple([_HBM] * (2 * n)),
        input_output_aliases={i: i for i in range(2 * n)},
        compiler_params=pltpu.CompilerParams(has_side_effects=_DATAFLOW),
    )(*parts, *lands, send_sems, recv_sems, after)
    return list(res[:n]), list(res[n:])


def exchange_start(name, arrays, n_copies, copies, after=()):
    n = len(arrays)

    def body(*refs):
        send_sems, recv_sems = refs[n + len(after)], refs[n + len(after) + 1]
        for cp in copies(refs[:n], send_sems, recv_sems):
            cp.start()
        refs[-1][...] = jnp.zeros_like(refs[-1])

    res = pl.pallas_call(
        body, name=name,
        out_shape=(pltpu.SemaphoreType.DMA((n_copies,)), pltpu.SemaphoreType.DMA((n_copies,)),
                   *[pltpu.HBM(a.shape, a.dtype) for a in arrays], jax.ShapeDtypeStruct((8, 128), F32)),
        in_specs=[_HBM] * n + [_ANY] * len(after),
        out_specs=(_SEM, _SEM, *[_HBM] * n, pl.BlockSpec(memory_space=pltpu.VMEM)),
        input_output_aliases={i: 2 + i for i in range(n)},
        compiler_params=pltpu.CompilerParams(has_side_effects=_DATAFLOW),
    )(*[_in_hbm(a) for a in arrays], *after)
    return res[0], res[1], list(res[2:2 + n]), res[-1]


def exchange_wait(name, send_sems, recv_sems, arrays, copies, after):
    n = len(arrays)

    def body(*refs):
        for cp in copies(refs[:n], refs[n], refs[n + 1]):
            cp.wait_send()
            cp.wait_recv()

    return list(pl.pallas_call(
        body, name=name, out_shape=tuple(pltpu.HBM(a.shape, a.dtype) for a in arrays),
        in_specs=[_HBM] * n + [_SEM, _SEM, _ANY], out_specs=tuple([_HBM] * n),
        input_output_aliases={i: i for i in range(n)},
        compiler_params=pltpu.CompilerParams(has_side_effects=_DATAFLOW),
    )(*arrays, send_sems, recv_sems, after))


def _to_sibling_copies(n):
    def copies(refs, send_sems, recv_sems):
        x, y, c = _place()
        out = []
        for i in range(n):
            rows = refs[i].shape[2]
            out.append(pltpu.make_async_remote_copy(
                src_ref=refs[i].at[:, :, pl.ds((1 - c) * (rows // 2), rows // 2), :], dst_ref=refs[n + i],
                send_sem=send_sems.at[i], recv_sem=recv_sems.at[i], device_id=(x, y, 1 - c), device_id_type=MESH))
        return out
    return copies


def _share_copies(n):
    def copies(refs, send_sems, recv_sems):
        x, y, c = _place()
        out = []
        for i in range(n):
            mine = _half(refs[i], c, refs[i].shape[1])
            out.append(pltpu.make_async_remote_copy(
                src_ref=mine, dst_ref=mine, send_sem=send_sems.at[i], recv_sem=recv_sems.at[i],
                device_id=(x, y, 1 - c), device_id_type=MESH))
        return out
    return copies


def send_half_to_sibling(name, grads):
    n = len(grads)

    def body(*refs):
        srcs, outs = refs[:n], refs[n:2 * n]
        send_sems, recv_sems = refs[2 * n:]
        x, y, c = _place()
        sibling = (x, y, 1 - c)
        copies = []
        for i in range(n):
            rows = srcs[i].shape[2]
            rc = pltpu.make_async_remote_copy(
                src_ref=srcs[i].at[:, :, pl.ds((1 - c) * (rows // 2), rows // 2), :], dst_ref=outs[i],
                send_sem=send_sems.at[i], recv_sem=recv_sems.at[i], device_id=sibling, device_id_type=MESH)
            rc.start()
            copies.append(rc)
        for rc in copies:
            rc.wait()

    def half_shape(g):
        s = g.shape
        return jax.ShapeDtypeStruct((s[0], s[1], s[2] // 2, s[3]), g.dtype)

    return pl.pallas_call(
        body, name=name, in_specs=[_ANY] * n, out_specs=[_ANY] * n, out_shape=[half_shape(g) for g in grads],
        scratch_shapes=[pltpu.SemaphoreType.DMA((n,)), pltpu.SemaphoreType.DMA((n,))],
        compiler_params=pltpu.CompilerParams(has_side_effects=True),
    )(*grads)


def exchange_between_chips(name, parts):
    n = len(parts)

    def body(*refs):
        srcs, outs = refs[:n], refs[n:2 * n]
        send_sems, recv_sems = refs[2 * n:]
        x, y, c = _place()
        chips = _other_chips(x, y)
        copies = []
        for i in range(n):
            for j, (qx, qy) in enumerate(chips):
                k = 3 * i + j
                rc = pltpu.make_async_remote_copy(
                    src_ref=srcs[i].at[2 * qx + qy], dst_ref=outs[i].at[j],
                    send_sem=send_sems.at[k], recv_sem=recv_sems.at[k], device_id=(qx, qy, c), device_id_type=MESH)
                rc.start()
                copies.append(rc)
        for rc in copies:
            rc.wait()

    return pl.pallas_call(
        body, name=name, in_specs=[_ANY] * n, out_specs=[_ANY] * n,
        out_shape=[jax.ShapeDtypeStruct((N_CHIPS - 1,) + p.shape[1:], p.dtype) for p in parts],
        scratch_shapes=[pltpu.SemaphoreType.DMA((3 * n,)), pltpu.SemaphoreType.DMA((3 * n,))],
        compiler_params=pltpu.CompilerParams(has_side_effects=True),
    )(*parts)


def share_halves(name, bufs):
    n = len(bufs)

    def body(*refs):
        outs = refs[n:2 * n]
        send_sems, recv_sems = refs[2 * n:]
        x, y, c = _place()
        copies = []
        for i in range(n):
            mine = _half(outs[i], c, outs[i].shape[1])
            rc = pltpu.make_async_remote_copy(
                src_ref=mine, dst_ref=mine, send_sem=send_sems.at[i], recv_sem=recv_sems.at[i],
                device_id=(x, y, 1 - c), device_id_type=MESH)
            rc.start()
            copies.append(rc)
        for i in range(n):
            theirs = _half(outs[i], 1 - c, outs[i].shape[1])
            pltpu.make_async_remote_copy(
                src_ref=theirs, dst_ref=theirs, send_sem=send_sems.at[i], recv_sem=recv_sems.at[i],
                device_id=(x, y, 1 - c), device_id_type=MESH).wait_recv()
        for rc in copies:
            rc.wait_send()

    return pl.pallas_call(
        body, name=name, in_specs=[_ANY] * n, out_specs=[_ANY] * n,
        out_shape=[jax.ShapeDtypeStruct(b.shape, b.dtype) for b in bufs],
        input_output_aliases={i: i for i in range(n)},
        scratch_shapes=[pltpu.SemaphoreType.DMA((n,)), pltpu.SemaphoreType.DMA((n,))],
        compiler_params=pltpu.CompilerParams(has_side_effects=True),
    )(*bufs)


def pair_sum(name, grad, recv, c):
    ns, na, rh, cols = recv.shape
    tr = _row_tile(rh, 256) if rh % 256 == 0 else rh
    nt = rh // tr

    def body(c_ref, g_ref, r_ref, o_ref):
        o_ref[...] = (g_ref[...].astype(F32) + r_ref[...].astype(F32)).astype(BF16)

    blk = (None, None, tr, cols)
    return pl.pallas_call(
        body, name=name,
        grid_spec=pltpu.PrefetchScalarGridSpec(
            num_scalar_prefetch=1, grid=(ns, na, nt),
            in_specs=[pl.BlockSpec(blk, lambda s, a, r, c_ref: (s, a, c_ref[0] * nt + r, 0)),
                      pl.BlockSpec(blk, lambda s, a, r, c_ref: (s, a, r, 0))],
            out_specs=pl.BlockSpec(blk, lambda s, a, r, c_ref: (s, a, r, 0))),
        out_shape=jax.ShapeDtypeStruct(recv.shape, BF16),
        compiler_params=_params(("parallel", "parallel", "parallel")),
    )(c, grad, recv)


def chip_sum(name, parts, landed, place):
    _, na, rh, cols = parts.shape
    tr = _row_tile(rh, 256) if rh % 256 == 0 else rh
    nt = rh // tr

    def body(place_ref, p_ref, l_ref, o_ref):
        total = p_ref[...].astype(F32)
        for s in range(N_CHIPS - 1):
            total = total + l_ref[s].astype(F32)
        o_ref[...] = total

    return pl.pallas_call(
        body, name=name,
        grid_spec=pltpu.PrefetchScalarGridSpec(
            num_scalar_prefetch=1, grid=(na, nt),
            in_specs=[pl.BlockSpec((None, None, tr, cols), lambda a, r, pr: (pr[1], a, r, 0)),
                      pl.BlockSpec((N_CHIPS - 1, None, tr, cols), lambda a, r, pr: (0, a, r, 0))],
            out_specs=pl.BlockSpec((None, tr, cols), lambda a, r, pr: (a, pr[0] * nt + r, 0))),
        out_shape=jax.ShapeDtypeStruct((na, 2 * rh, cols), F32),
        compiler_params=_params(("parallel", "parallel")),
    )(place, parts, landed)


def reduce_scatter_1(tag, grads):
    n = len(grads)
    lands = [lax.empty((g.shape[0], g.shape[1], g.shape[2] // 2, g.shape[3]), g.dtype) for g in grads]
    send_sems, recv_sems, arrays, token = exchange_start(f"{tag}_to_sibling_start", list(grads) + lands, n, _to_sibling_copies(n))
    return (send_sems, recv_sems, arrays), token


def reduce_scatter_2(tag, state, place, after):
    send_sems, recv_sems, arrays = state
    n = len(arrays) // 2
    arrays = exchange_wait(f"{tag}_to_sibling_wait", send_sems, recv_sems, arrays, _to_sibling_copies(n), after)
    parts = [pair_sum(f"{tag}_pair_sum{i}", g, r, place) for i, (g, r) in enumerate(zip(arrays[:n], arrays[n:]))]
    send_sems, recv_sems, parts, lands, token = between_chips_start(f"{tag}_between_chips_start", parts)
    return (send_sems, recv_sems, parts, lands), token


def reduce_scatter_3(tag, state, place, after):
    send_sems, recv_sems, parts, lands = state
    parts, landed = between_chips_wait(f"{tag}_between_chips_wait", send_sems, recv_sems, parts, lands, after)
    halves = [chip_sum(f"{tag}_chip_sum{i}", p, l, place) for i, (p, l) in enumerate(zip(parts, landed))]
    send_sems, recv_sems, halves, token = exchange_start(f"{tag}_share_start", halves, len(halves), _share_copies(len(halves)))
    return (send_sems, recv_sems, halves), token


def reduce_scatter_4(tag, state, after):
    send_sems, recv_sems, halves = state
    return exchange_wait(f"{tag}_share_wait", send_sems, recv_sems, halves, _share_copies(len(halves)), after)


def _adamw_math(w, g, m, v):
    m = ADAM_B1 * m + (1.0 - ADAM_B1) * g
    v = ADAM_B2 * v + (1.0 - ADAM_B2) * (g * g)
    m_hat = m / (1.0 - ADAM_B1 ** ADAM_STEP)
    v_hat = v / (1.0 - ADAM_B2 ** ADAM_STEP)
    delta = -ADAM_LR * (m_hat / (jnp.sqrt(v_hat) + ADAM_EPS) + ADAM_WD * w)
    return delta, m, v


def adamw(name, w, g, m, v, after=()):
    rows, cols = w.shape
    tr = _row_tile(rows, 256) if rows % 256 == 0 else rows // 2

    def body(w_ref, g_ref, m_ref, v_ref, d_ref, mo_ref, vo_ref):
        d_ref[...], mo_ref[...], vo_ref[...] = _adamw_math(w_ref[...], g_ref[...], m_ref[...], v_ref[...])

    blk = pl.BlockSpec((tr, cols), lambda i: (i, 0))
    return pl.pallas_call(
        _ordered_after(body, 4, len(after)), name=name, grid=(rows // tr,), in_specs=[blk] * 4 + [_ANY] * len(after),
        out_specs=[blk] * 3, out_shape=[jax.ShapeDtypeStruct(w.shape, F32)] * 3, compiler_params=_params(("parallel",)),
    )(w, g, m, v, *after)


def adamw_rows(name, w, g, m, v, after=()):
    rows, _, cols = w.shape
    tr = next(r for r in (128, 110, 64, 32, 16, 8, 1) if rows % r == 0)

    def body(w_ref, g_ref, m_ref, v_ref, d_ref, mo_ref, vo_ref):
        d_ref[...], mo_ref[...], vo_ref[...] = _adamw_math(w_ref[...], g_ref[...], m_ref[...], v_ref[...])

    blk = pl.BlockSpec((tr, 1, cols), lambda i: (i, 0, 0))
    return pl.pallas_call(
        _ordered_after(body, 4, len(after)), name=name, grid=(rows // tr,), in_specs=[blk] * 4 + [_ANY] * len(after),
        out_specs=[blk] * 3, out_shape=[jax.ShapeDtypeStruct(w.shape, F32)] * 3, compiler_params=_params(("parallel",)),
    )(w, g, m, v, *after)


def adamw_stacked(name, ws, g, ms, vs, after=()):
    n = len(ws)
    rows, cols = ws[0].shape
    tr = next(r for r in (128, 88, 64, 32, 16, 8) if rows % r == 0)

    def body(*refs):
        w_refs, m_refs, v_refs, g_ref = refs[:n], refs[n:2 * n], refs[2 * n:3 * n], refs[3 * n]
        outs = refs[3 * n + 1:]
        for i in range(n):
            outs[i][...], outs[n + i][...], outs[2 * n + i][...] = _adamw_math(
                w_refs[i][...], g_ref[i], m_refs[i][...], v_refs[i][...])

    blk = pl.BlockSpec((tr, cols), lambda r: (r, 0))
    res = pl.pallas_call(
        _ordered_after(body, 3 * n + 1, len(after)), name=name, grid=(rows // tr,),
        in_specs=[blk] * (3 * n) + [pl.BlockSpec((n, tr, cols), lambda r: (0, r, 0))] + [_ANY] * len(after),
        out_specs=[blk] * (3 * n),
        out_shape=[jax.ShapeDtypeStruct((rows, cols), F32)] * (3 * n), compiler_params=_params(("parallel",)),
    )(*ws, *ms, *vs, g, *after)
    return res[:n], res[n:2 * n], res[2 * n:]


def small_allreduce_adamw(name, g_part, w, m, v):
    rows, cols = g_part.shape

    def body(g_ref, w_ref, m_ref, v_ref, sum_ref, d_ref, mo_ref, vo_ref, land, send_sems, recv_sems):
        x, y, c = _place()
        me = 4 * x + 2 * y + c
        land[me] = g_ref[...]
        copies = []
        for r in range(1, 8):
            peer = (x ^ (r >> 2), y ^ ((r >> 1) & 1), c ^ (r & 1))
            rc = pltpu.make_async_remote_copy(
                src_ref=g_ref, dst_ref=land.at[me], send_sem=send_sems.at[r - 1], recv_sem=recv_sems.at[r - 1],
                device_id=peer, device_id_type=MESH)
            rc.start()
            copies.append(rc)
        for rc in copies:
            rc.wait()
        total = land[0]
        for s in range(1, 8):
            total = total + land[s]
        sum_ref[...] = total
        d_ref[...], mo_ref[...], vo_ref[...] = _adamw_math(w_ref[...], total, m_ref[...], v_ref[...])

    vmem = pl.BlockSpec(memory_space=pltpu.VMEM)
    return pl.pallas_call(
        body, name=name, in_specs=[vmem] * 4, out_specs=[vmem] * 4,
        out_shape=[jax.ShapeDtypeStruct((rows, cols), F32)] * 4,
        scratch_shapes=[pltpu.VMEM((8, rows, cols), F32), pltpu.SemaphoreType.DMA((7,)), pltpu.SemaphoreType.DMA((7,))],
        compiler_params=pltpu.CompilerParams(has_side_effects=True),
    )(g_part, w, m, v)


def _heads(a):
    t, w = a.shape
    return a.reshape(t, w // HEAD_DIM, HEAD_DIM).transpose(1, 0, 2)


def _unheads(a):
    n, t, _ = a.shape
    return a.transpose(1, 0, 2).reshape(t, n * HEAD_DIM)


def _cols_from_shards(g):
    ns, r, cols = g.shape
    return g.transpose(1, 0, 2).reshape(r, ns * cols)


def _shards_from_cols(a):
    r, cols = a.shape
    return a.reshape(r, N_CHIPS, cols // N_CHIPS).transpose(1, 0, 2)


def kernel(x, norm_ffn1, w_ffn1_gate, w_ffn1_up, w_ffn1_down, norm_mix, w_in, b_forget, w_gate, b_gate, w_up_a, w_up_b, w_out, norm_ffn2, w_ffn2_gate, w_ffn2_up, w_ffn2_down, norm_final, loss_target, m_norm_ffn1, m_w_ffn1_gate, m_w_ffn1_up, m_w_ffn1_down, m_norm_mix, m_w_in, m_b_forget, m_w_gate, m_b_gate, m_w_up_a, m_w_up_b, m_w_out, m_norm_ffn2, m_w_ffn2_gate, m_w_ffn2_up, m_w_ffn2_down, m_norm_final, v_norm_ffn1, v_w_ffn1_gate, v_w_ffn1_up, v_w_ffn1_down, v_norm_mix, v_w_in, v_b_forget, v_w_gate, v_b_gate, v_w_up_a, v_w_up_b, v_w_out, v_norm_ffn2, v_w_ffn2_gate, v_w_ffn2_up, v_w_ffn2_down, v_norm_final):
    t, d = x.shape[1], x.shape[2]
    in4 = w_in.shape[2]
    gate4 = w_gate.shape[2]
    up4 = w_up_a.shape[2]
    in_cols = N_CHIPS * in4
    n_forget = in_cols - QKV_COLS
    assert w_up_a.shape[1] == WIDTH and d == 2 * WIDTH and n_forget == N_HEADS
    nq = t // BQ
    chip = 2 * lax.axis_index("x") + lax.axis_index("y")
    c_arr = jnp.stack([lax.axis_index("c"), chip]).astype(jnp.int32)
    x2d = x[0]
    target = loss_target[0]

    def slot(shard):
        return lax.dynamic_update_slice(lax.empty((N_CHIPS,) + shard.shape, BF16), shard.astype(BF16)[None], (chip, 0, 0, 0))

    def ffn_views(wg, wu, wd):
        return [wg[0].T, wu[0].T, wd[0]]

    ffn1_w, ffn1_m, ffn1_v = (ffn_views(w_ffn1_gate, w_ffn1_up, w_ffn1_down), ffn_views(m_w_ffn1_gate, m_w_ffn1_up, m_w_ffn1_down),
                              ffn_views(v_w_ffn1_gate, v_w_ffn1_up, v_w_ffn1_down))
    ffn2_w, ffn2_m, ffn2_v = (ffn_views(w_ffn2_gate, w_ffn2_up, w_ffn2_down), ffn_views(m_w_ffn2_gate, m_w_ffn2_up, m_w_ffn2_down),
                              ffn_views(v_w_ffn2_gate, v_w_ffn2_up, v_w_ffn2_down))
    in_pad = -(-in4 // 32) * 32
    ig_sh = slot(jnp.concatenate([jnp.pad(w_in[0].T, ((0, in_pad - in4), (0, 0))), w_gate[0].T], axis=0)[None])
    up_sh = slot(jnp.concatenate([w_up_a[0], w_up_b[0]], axis=0)[None])
    wo_sh = slot(w_out)
    f1_send, f1_recv, f1_bufs, f1_token = gather_ici_start("gather_ffn1_start", [slot(jnp.stack(ffn1_w))], norm_ffn1)
    mx_send, mx_recv, mx_bufs, mx_token = gather_ici_start("gather_mixer_start", [ig_sh, up_sh, wo_sh], f1_token)
    f2_send, f2_recv, f2_bufs, f2_token = gather_ici_start("gather_ffn2_start", [slot(jnp.stack(ffn2_w))], mx_token)

    normed1 = rms_fwd("ffn1_rms", x2d, norm_ffn1, after=(f2_token,))
    (w3_1,) = gather_forward("gather_ffn1_forward", gather_ici_wait("gather_ffn1_wait", f1_send, f1_recv, f1_bufs, normed1[0]))
    x1, saved1 = ffn_forward("ffn1", x2d, norm_ffn1, w3_1, normed=normed1)
    ig, wup, wo = gather_forward("gather_mixer_forward", gather_ici_wait("gather_mixer_wait", mx_send, mx_recv, mx_bufs, x1))
    wup = wup[:, 0]
    w_in_t = ig[:, 0, :in4].reshape(in_cols, d)
    wcat = jnp.concatenate([w_in_t[:QKV_COLS], ig[:, 0, in_pad:].reshape(2 * d, d), w_in_t[QKV_COLS:],
                            jnp.zeros((F_PAD - n_forget, d), BF16)], axis=0)
    bias_cat = jnp.concatenate([jnp.zeros((1, QKV_COLS), F32), b_gate, jnp.zeros((1, F_PAD), F32)], axis=1)
    f_off = QKV_COLS + 2 * d
    wo_full = wo.reshape(d, d)
    b_forget_row = jnp.pad(b_forget, ((0, 0), (0, QB - n_forget)))

    h2, rstd2 = rms_fwd("mix_rms", x1, norm_mix)
    qkv = proj("mix_proj_qkv", h2, wcat, bias_cat, 0, QKV_COLS, WIDTH, BF16, scaled_tiles=(0, 3))
    pc = proj("mix_proj_gates", h2, wcat, bias_cat, QKV_COLS, 2 * d + F_PAD, 768, F32)
    f_logit = pc[:, 2 * d:2 * d + QB]
    c_cum = fox_prep("fox_prep", f_logit, b_forget_row)
    c_heads = c_cum[:, :N_HEADS].T
    c_col = c_heads[:, :, None]
    c_row = c_heads.reshape(N_HEADS, t // CS, 1, CS)
    ya = sb_pair_fwd("sb_fwd", qkv)
    yb, lse = fox_pair_fwd("fox_fwd", qkv, c_col, c_row)
    ua, ub, mixed = mix_fwd("mix_fwd", ya, yb, wup, pc)
    x2 = mm_residual("mix_out", mixed[None], wo_full[None], pl.BlockSpec((1, d, d), lambda m: (0, 0, 0)), x1, 1.0)
    (w3_2,) = gather_forward("gather_ffn2_forward", gather_ici_wait("gather_ffn2_wait", f2_send, f2_recv, f2_bufs, x2))
    x3, saved2 = ffn_forward("ffn2", x2, norm_ffn2, w3_2)
    dx3, gn_final, loss_part = loss_head("loss_head", x3, norm_final[None], target)

    g_w3_2, dab2 = ffn_backward_weights("ffn2", dx3, saved2, w3_2)
    rs_ffn2, token = reduce_scatter_1("rs_ffn2", [g_w3_2])
    dx2, gn_ffn2 = ffn_backward_input("ffn2", dx3, saved2, dab2, norm_ffn2, w3_2, after=(token,))
    rs_ffn2, rs_ffn2_token = reduce_scatter_2("rs_ffn2", rs_ffn2, c_arr, dx2)

    dua, dub, dpa, dpb, gba, gbb = mix_bwd("mix_bwd", dx2, wo_full, pc, ua, ub, after=(rs_ffn2_token,))
    dgp = jnp.concatenate([dpa, dpb], axis=1)
    g_bgate = jnp.concatenate([gba, gbb], axis=1)
    g_wo = mm_plain("wgrad_out", mixed, dx2, TN, BF16, tk_target=1024)
    dya = up_bwd("dya", dua, wup, 0)
    dyb = up_bwd("dyb", dub, wup, 1)
    g_up = up_wgrad("wgrad_up", ya, yb, dua, dub)
    dqkv = sb_pair_bwd("sb_bwd", qkv, dya, lax.empty((t, QKV_COLS), BF16))
    dqkv, dcc, dcr = fox_pair_bwd("fox_bwd", qkv, yb, dyb, lse, c_col, c_row, dqkv)
    dc = (dcc[:, :, 0] + dcr.reshape(N_HEADS, t)).T
    df, g_bf = fox_gate_bwd("fox_gate_bwd", jnp.pad(dc, ((0, 0), (0, QB - N_HEADS))), f_logit, b_forget_row)
    dcat = jnp.concatenate([dqkv, dgp, df.astype(BF16), jnp.zeros((t, F_PAD - QB), BF16)], axis=1)
    g_wcat = wgrad_cat("wgrad_cat", h2, dcat)
    g_in_t = jnp.concatenate([g_wcat[:QKV_COLS], g_wcat[f_off:f_off + n_forget]], axis=0).reshape(N_CHIPS, in4, d)
    g_ig = jnp.concatenate([jnp.pad(g_in_t, ((0, 0), (0, in_pad - in4), (0, 0))),
                            g_wcat[QKV_COLS:f_off].reshape(N_CHIPS, gate4, d)], axis=1)[:, None]
    rs_mixer, token = reduce_scatter_1("rs_mixer", [g_ig, g_up[:, None], g_wo.reshape(N_CHIPS, 1, d // N_CHIPS, d)])
    tm = _row_tile(t, 512)
    tkc = 1792
    nkc = wcat.shape[0] // tkc
    dx1, gn_mix = dh_rms_bwd(
        "mix_dh", [(dcat, wcat)],
        [(pl.BlockSpec((tm, tkc), lambda m, k: (m, k)), pl.BlockSpec((tkc, d), lambda m, k: (k, 0)))], NN,
        (t // tm, nkc), nkc, x1, rstd2, norm_mix, dx2, after=(token,))
    rs_mixer, rs_mixer_token = reduce_scatter_2("rs_mixer", rs_mixer, c_arr, dx1)
    rs_ffn2, rs_ffn2_token = reduce_scatter_3("rs_ffn2", rs_ffn2, c_arr, dx1)

    g_w3_1, dab1 = ffn_backward_weights("ffn1", dx1, saved1, w3_1, after=(rs_mixer_token, rs_ffn2_token))
    rs_ffn1, token = reduce_scatter_1("rs_ffn1", [g_w3_1])
    dx0, gn_ffn1 = ffn_backward_input("ffn1", dx1, saved1, dab1, norm_ffn1, w3_1, after=(token,))
    rs_ffn1, rs_ffn1_token = reduce_scatter_2("rs_ffn1", rs_ffn1, c_arr, dx0)
    rs_mixer, rs_mixer_token = reduce_scatter_3("rs_mixer", rs_mixer, c_arr, dx0)
    (s_w3_2,) = reduce_scatter_4("rs_ffn2", rs_ffn2, dx0)

    def pack_small(n1, nm, n2, nf, bg, bf, last):
        return jnp.concatenate([n1, nm, n2, nf, bg.reshape(2, d), jnp.pad(bf, ((0, 0), (0, d - n_forget))), last], axis=0)

    zero_row = jnp.zeros((1, d), F32)
    g_small = pack_small(gn_ffn1, gn_mix, gn_ffn2, gn_final, g_bgate, g_bf[:, :n_forget], jnp.pad(loss_part, ((0, 0), (0, d - 1))))
    w_small = pack_small(norm_ffn1, norm_mix, norm_ffn2, norm_final[None], b_gate, b_forget, zero_row)
    m_small = pack_small(m_norm_ffn1, m_norm_mix, m_norm_ffn2, m_norm_final[None], m_b_gate, m_b_forget, zero_row)
    v_small = pack_small(v_norm_ffn1, v_norm_mix, v_norm_ffn2, v_norm_final[None], v_b_gate, v_b_forget, zero_row)
    smalls = small_allreduce_adamw("small_allreduce_adamw", g_small, w_small, m_small, v_small)

    def unpack_small(p):
        return {"norm_ffn1": p[0:1], "norm_mix": p[1:2], "norm_ffn2": p[2:3], "norm_final": p[3], "b_gate": p[4:6].reshape(1, 2 * d),
                "b_forget": p[6:7, :n_forget]}

    loss = smalls[0][7, 0]
    small_out = [unpack_small(p) for p in smalls]

    big_out = {}

    def adamw_ffn(tag, s_w3, ws, ms, vs, after):
        deltas, new_ms, new_vs = adamw_stacked(f"adamw_{tag}", ws, s_w3, ms, vs, after)
        for which, part in ((GATE, "gate"), (UP, "up"), (DOWN, "down")):
            back = (lambda a: a[None]) if which == DOWN else (lambda a: a.T[None])
            big_out[f"w_{tag}_{part}"] = tuple(back(a) for a in (s_w3[which], deltas[which], new_ms[which], new_vs[which]))
        return deltas[DOWN]

    last = adamw_ffn("ffn2", s_w3_2, ffn2_w, ffn2_m, ffn2_v, (rs_ffn1_token, rs_mixer_token))
    s_ig, s_up, s_wo = reduce_scatter_4("rs_mixer", rs_mixer, last)
    grads = {
        "w_gate": s_ig[0, in_pad:].T, "w_up_a": s_up[0, :WIDTH], "w_up_b": s_up[0, WIDTH:], "w_out": s_wo[0],
    }
    weights = {"w_gate": (w_gate, m_w_gate, v_w_gate), "w_up_a": (w_up_a, m_w_up_a, v_w_up_a),
               "w_up_b": (w_up_b, m_w_up_b, v_w_up_b), "w_out": (w_out, m_w_out, v_w_out)}
    for wname, (w, m, v) in weights.items():
        g = grads[wname]
        delta, new_m, new_v = adamw(f"adamw_{wname}", w[0], g, m[0], v[0])
        big_out[wname] = (g[None], delta[None], new_m[None], new_v[None])
    rows_of = lambda a: jnp.transpose(a, (2, 0, 1))
    g_in_rows = s_ig[0, :in4][:, None, :]
    in_rows = adamw_rows("adamw_w_in", rows_of(w_in), g_in_rows, rows_of(m_w_in), rows_of(v_w_in))
    big_out["w_in"] = tuple(jnp.transpose(a, (1, 2, 0)) for a in (g_in_rows, *in_rows))

    rs_ffn1, token = reduce_scatter_3("rs_ffn1", rs_ffn1, c_arr, in_rows[0])
    (s_w3_1,) = reduce_scatter_4("rs_ffn1", rs_ffn1, token)
    adamw_ffn("ffn1", s_w3_1, ffn1_w, ffn1_m, ffn1_v, ())

    order = ["norm_ffn1", "w_ffn1_gate", "w_ffn1_up", "w_ffn1_down", "norm_mix", "w_in", "b_forget", "w_gate", "b_gate",
             "w_up_a", "w_up_b", "w_out", "norm_ffn2", "w_ffn2_gate", "w_ffn2_up", "w_ffn2_down", "norm_final"]
    outs = [loss, dx0[None]]
    for kind in range(4):
        for wname in order:
            outs.append(big_out[wname][kind] if wname in big_out else small_out[kind][wname])
    return tuple(outs)
```

```python
import functools

import jax
import jax.numpy as jnp
from jax import lax
from jax.experimental import pallas as pl
from jax.experimental.pallas import tpu as pltpu

F32 = jnp.float32
BF16 = jnp.bfloat16

HEAD_DIM = 64
N_HEADS = 8
WIDTH = N_HEADS * HEAD_DIM
QKV_COLS = 6 * WIDTH
RMS_EPS = 1e-6
ATTN_SCALE = HEAD_DIM ** -0.5
N_CHIPS = 4
QB = 128
BQ = 1024
CS = 256
N_SUB = BQ // CS
F_PAD = 256
NEG_BIG = -1e30

ADAM_LR = 0.001
ADAM_B1 = 0.9
ADAM_B2 = 0.999
ADAM_EPS = 1e-08
ADAM_WD = 0.01
ADAM_STEP = 10

VMEM_LIMIT_BYTES = 48 * 1024 * 1024
MESH = pl.DeviceIdType.MESH

NN = ((1,), (0,))
NT = ((1,), (1,))
TN = ((0,), (0,))


def _params(semantics):
    return pltpu.CompilerParams(dimension_semantics=semantics, vmem_limit_bytes=VMEM_LIMIT_BYTES)


def _dot(a, b, contract):
    return lax.dot_general(a.astype(BF16), b.astype(BF16), (contract, ((), ())), preferred_element_type=F32)


def _sigmoid(x):
    return 1.0 / (1.0 + jnp.exp(-x))


def _log1pexp_neg_abs(z):
    return jnp.log(1.0 + jnp.exp(-jnp.abs(z)))


def _split3(x):
    hi = x.astype(BF16)
    r1 = x - hi.astype(F32)
    mid = r1.astype(BF16)
    lo = (r1 - mid.astype(F32)).astype(BF16)
    return hi, mid, lo


def _dot_exact_rhs01(x, m01):
    hi, mid, lo = _split3(x)
    d = lambda p: lax.dot_general(p, m01, (NN, ((), ())), preferred_element_type=F32)
    return d(hi) + d(mid) + d(lo)


def _dot_exact_lhs01(m01, x):
    hi, mid, lo = _split3(x)
    d = lambda p: lax.dot_general(m01, p, (NN, ((), ())), preferred_element_type=F32)
    return d(hi) + d(mid) + d(lo)


def _iota2(shape, dim):
    return lax.broadcasted_iota(jnp.int32, shape, dim)


def _mm(name, pairs, contract, grid, pair_specs, out_shape, out_specs, acc_shape, nk, epilogue,
        extras=(), extra_specs=(), semantics=None):
    n_pairs = len(pairs)
    n_extra = len(extras)
    n_out = len(out_shape)

    def body(*refs):
        ab = refs[:2 * n_pairs]
        ex = refs[2 * n_pairs:2 * n_pairs + n_extra]
        outs = refs[2 * n_pairs + n_extra:2 * n_pairs + n_extra + n_out]
        ids = [pl.program_id(i) for i in range(len(grid))]
        k = ids[-1]
        part = _dot(ab[0][...], ab[1][...], contract)
        for p in range(1, n_pairs):
            part += _dot(ab[2 * p][...], ab[2 * p + 1][...], contract)
        if nk == 1:
            epilogue(part, ex, outs, ids)
            return
        acc = refs[-1]

        @pl.when(k == 0)
        def _():
            acc[...] = part

        @pl.when(k != 0)
        def _():
            acc[...] += part

        @pl.when(k == nk - 1)
        def _():
            epilogue(acc[...], ex, outs, ids)

    operands = [t for pair in pairs for t in pair] + list(extras)
    in_specs = [s for pair in pair_specs for s in pair] + list(extra_specs)
    if semantics is None:
        semantics = ("parallel",) * (len(grid) - 1) + ("arbitrary",)
    return pl.pallas_call(
        body, name=name, grid=grid, in_specs=in_specs, out_specs=list(out_specs), out_shape=list(out_shape),
        scratch_shapes=[] if nk == 1 else [pltpu.VMEM(acc_shape, F32)], compiler_params=_params(semantics),
    )(*operands)


def _ordered_after(body, n_in, n_after):
    def wrapped(*refs):
        return body(*refs[:n_in], *refs[n_in + n_after:])
    return wrapped


def _row_halves(rows):
    return (slice(0, rows // 2), slice(rows // 2, rows)) if rows % 32 == 0 else (slice(0, rows),)


def _row_tile(rows, target):
    t = min(rows, target)
    while rows % t:
        t //= 2
    return t


def rms_fwd(name, x, g, after=()):
    t, d = x.shape
    tr = _row_tile(t, 256)

    def body(x_ref, g_ref, h_ref, r_ref):
        xv = x_ref[...]
        r = lax.rsqrt(jnp.mean(xv * xv, axis=-1, keepdims=True) + RMS_EPS)
        h_ref[...] = (xv * r * g_ref[...]).astype(BF16)
        r_ref[...] = r

    return pl.pallas_call(
        _ordered_after(body, 2, len(after)), name=name, grid=(t // tr,),
        in_specs=[pl.BlockSpec((tr, d), lambda i: (i, 0)), pl.BlockSpec((1, d), lambda i: (0, 0))] + [_ANY] * len(after),
        out_specs=[pl.BlockSpec((tr, d), lambda i: (i, 0)), pl.BlockSpec((tr, 1), lambda i: (i, 0))],
        out_shape=[jax.ShapeDtypeStruct((t, d), BF16), jax.ShapeDtypeStruct((t, 1), F32)],
        compiler_params=_params(("parallel",)),
    )(x, g, *after)


GATE, UP, DOWN = 0, 1, 2


def _ffn_w_spec(which, f4, d, index_of_j):
    return pl.BlockSpec((None, None, f4, d), lambda *ids: (index_of_j(*ids), which, 0, 0))


def ffn_up(name, h, w3):
    t, d = h.shape
    ns, _, f4, _ = w3.shape
    tm = _row_tile(t, 512)

    def body(h_ref, wg_ref, wu_ref, a_ref, b_ref, s_ref):
        for rows in _row_halves(tm):
            hv = h_ref[rows, :]
            a = _dot(hv, wg_ref[...], NT)
            b = _dot(hv, wu_ref[...], NT)
            a_ref[rows, :] = a.astype(BF16)
            b_ref[rows, :] = b.astype(BF16)
            s_ref[rows, :] = (a * _sigmoid(a) * b).astype(BF16)

    act_spec = pl.BlockSpec((None, tm, f4), lambda j, m: (j, m, 0))
    return pl.pallas_call(
        body, name=name, grid=(ns, t // tm),
        in_specs=[pl.BlockSpec((tm, d), lambda j, m: (m, 0)),
                  _ffn_w_spec(GATE, f4, d, lambda j, m: j), _ffn_w_spec(UP, f4, d, lambda j, m: j)],
        out_specs=[act_spec, act_spec, act_spec],
        out_shape=[jax.ShapeDtypeStruct((ns, t, f4), BF16)] * 3,
        compiler_params=_params(("parallel", "parallel")),
    )(h, w3, w3)


def mm_residual(name, s, w, w_spec, x, scale):
    nj, t, kdim = s.shape
    n = x.shape[1]
    tm = _row_tile(t, 512)

    def body(s_ref, w_ref, x_ref, o_ref):
        acc = _dot(s_ref[0], w_ref[0], NN)
        for j in range(1, nj):
            acc += _dot(s_ref[j], w_ref[j], NN)
        o_ref[...] = x_ref[...] + scale * acc

    row = pl.BlockSpec((tm, n), lambda m: (m, 0))
    return pl.pallas_call(
        body, name=name, grid=(t // tm,),
        in_specs=[pl.BlockSpec((nj, tm, kdim), lambda m: (0, m, 0)), w_spec, row], out_specs=row,
        out_shape=jax.ShapeDtypeStruct((t, n), F32), compiler_params=_params(("parallel",)),
    )(s, w, x)


def ffn_bwd_act(name, dx, w3, a, b, after=()):
    t, d = dx.shape
    ns, _, f4, _ = w3.shape
    tm = _row_tile(t, 512)

    def body(dx_ref, wd_ref, a_ref, b_ref, da_ref, db_ref):
        for rows in _row_halves(tm):
            ds = _dot(0.5 * dx_ref[rows, :], wd_ref[...], NT)
            av = a_ref[rows, :].astype(F32)
            sig = _sigmoid(av)
            da_ref[rows, :] = (ds * b_ref[rows, :].astype(F32) * (sig * (1.0 + av * (1.0 - sig)))).astype(BF16)
            db_ref[rows, :] = (ds * (av * sig)).astype(BF16)

    act_spec = pl.BlockSpec((None, tm, f4), lambda j, m: (j, m, 0))
    return pl.pallas_call(
        _ordered_after(body, 4, len(after)), name=name, grid=(ns, t // tm),
        in_specs=[pl.BlockSpec((tm, d), lambda j, m: (m, 0)), _ffn_w_spec(DOWN, f4, d, lambda j, m: j), act_spec, act_spec]
        + [_ANY] * len(after),
        out_specs=[act_spec, act_spec],
        out_shape=[jax.ShapeDtypeStruct((ns, t, f4), BF16)] * 2,
        compiler_params=_params(("parallel", "parallel")),
    )(dx, w3, a, b, *after)


def ffn_wgrad(name, h, da, db, s, dx):
    t, d = h.shape
    ns, _, f4 = da.shape
    tk = _row_tile(t, 1024)
    nk = t // tk

    def body(h_ref, da_ref, db_ref, s_ref, dx_ref, o_ref, acc):
        k = pl.program_id(1)

        @pl.when(k == 0)
        def _():
            acc[...] = jnp.zeros_like(acc)

        hv = h_ref[...]
        acc[GATE] += _dot(da_ref[...], hv, TN)
        acc[UP] += _dot(db_ref[...], hv, TN)
        acc[DOWN] += _dot(s_ref[...], 0.5 * dx_ref[...], TN)

        @pl.when(k == nk - 1)
        def _():
            o_ref[...] = acc[...].astype(BF16)

    act_spec = pl.BlockSpec((None, tk, f4), lambda j, k: (j, k, 0))
    row_spec = pl.BlockSpec((tk, d), lambda j, k: (k, 0))
    return pl.pallas_call(
        body, name=name, grid=(ns, nk),
        in_specs=[row_spec, act_spec, act_spec, act_spec, row_spec],
        out_specs=pl.BlockSpec((None, 3, f4, d), lambda j, k: (j, 0, 0, 0)),
        out_shape=jax.ShapeDtypeStruct((ns, 3, f4, d), BF16),
        scratch_shapes=[pltpu.VMEM((3, f4, d), F32)],
        compiler_params=_params(("parallel", "arbitrary")),
    )(h, da, db, s, dx)


def _rms_bwd_tail(dh, x_ref, r_ref, g_ref, dxin_ref, dx_ref, gn_ref, row_tile_index):
    r = r_ref[...]
    xhat = x_ref[...] * r
    dhg = dh * g_ref[...]
    dx_ref[...] = dxin_ref[...] + r * (dhg - xhat * jnp.mean(dhg * xhat, axis=-1, keepdims=True))
    part = jnp.sum(dh * xhat, axis=0, keepdims=True)

    @pl.when(row_tile_index == 0)
    def _():
        gn_ref[...] = part

    @pl.when(row_tile_index != 0)
    def _():
        gn_ref[...] += part


def ffn_dh(name, da, db, w3, x, rstd, g, dx_in, after=()):
    ns, t, f4 = da.shape
    d = x.shape[1]
    tm = _row_tile(t, 256)

    def body(da_ref, db_ref, wg_ref, wu_ref, x_ref, r_ref, g_ref, dxin_ref, dx_ref, gn_ref):
        dh = _dot(da_ref[0], wg_ref[0], NN) + _dot(db_ref[0], wu_ref[0], NN)
        for j in range(1, ns):
            dh += _dot(da_ref[j], wg_ref[j], NN) + _dot(db_ref[j], wu_ref[j], NN)
        _rms_bwd_tail(dh, x_ref, r_ref, g_ref, dxin_ref, dx_ref, gn_ref, pl.program_id(0))

    act = pl.BlockSpec((ns, tm, f4), lambda m: (0, m, 0))
    row = pl.BlockSpec((tm, d), lambda m: (m, 0))
    gain = pl.BlockSpec((1, d), lambda m: (0, 0))
    return pl.pallas_call(
        _ordered_after(body, 8, len(after)), name=name, grid=(t // tm,),
        in_specs=[act, act, pl.BlockSpec((ns, None, f4, d), lambda m: (0, GATE, 0, 0)),
                  pl.BlockSpec((ns, None, f4, d), lambda m: (0, UP, 0, 0)), row, pl.BlockSpec((tm, 1), lambda m: (m, 0)), gain, row]
        + [_ANY] * len(after),
        out_specs=[row, gain], out_shape=[jax.ShapeDtypeStruct((t, d), F32), jax.ShapeDtypeStruct((1, d), F32)],
        compiler_params=_params(("arbitrary",)),
    )(da, db, w3, w3, x, rstd, g, dx_in, *after)


def mixer_dh(name, parts, x, rstd, g, dx_in, after=()):
    t, d = x.shape
    n = len(parts)
    tm = _row_tile(t, 256)

    def body(*refs):
        dh = _dot(refs[0][...], refs[n][...], NN)
        for i in range(1, n):
            dh += _dot(refs[i][...], refs[n + i][...], NN)
        _rms_bwd_tail(dh, *refs[2 * n:2 * n + 6], pl.program_id(0))

    row = pl.BlockSpec((tm, d), lambda m: (m, 0))
    gain = pl.BlockSpec((1, d), lambda m: (0, 0))
    act_specs = [pl.BlockSpec((tm, a.shape[1]), lambda m: (m, 0)) for a, _, _ in parts]
    w_specs = [pl.BlockSpec((a.shape[1], d), functools.partial(lambda m, blk: (blk, 0), blk=blk)) for a, _, blk in parts]
    return pl.pallas_call(
        _ordered_after(body, 2 * n + 4, len(after)), name=name, grid=(t // tm,),
        in_specs=act_specs + w_specs + [row, pl.BlockSpec((tm, 1), lambda m: (m, 0)), gain, row] + [_ANY] * len(after),
        out_specs=[row, gain], out_shape=[jax.ShapeDtypeStruct((t, d), F32), jax.ShapeDtypeStruct((1, d), F32)],
        compiler_params=_params(("arbitrary",)),
    )(*[a for a, _, _ in parts], *[w for _, w, _ in parts], x, rstd, g, dx_in, *after)


def dh_rms_bwd(name, pairs, pair_specs, contract, grid, nk, x, rstd, g, dx_in, after=()):
    t, d = x.shape
    tm = t // grid[0]

    def epilogue(acc, ex, outs, ids):
        _rms_bwd_tail(acc, *ex[:4], *outs, ids[0])

    row = pl.BlockSpec((tm, d), lambda m, k: (m, 0))
    return _mm(
        name, pairs, contract, grid, pair_specs,
        [jax.ShapeDtypeStruct((t, d), F32), jax.ShapeDtypeStruct((1, d), F32)],
        [row, pl.BlockSpec((1, d), lambda m, k: (0, 0))], (tm, d), nk, epilogue,
        extras=[x, rstd, g, dx_in, *after],
        extra_specs=[row, pl.BlockSpec((tm, 1), lambda m, k: (m, 0)), pl.BlockSpec((1, d), lambda m, k: (0, 0)), row]
        + [_ANY] * len(after),
        semantics=("arbitrary", "arbitrary"),
    )


def ffn_forward(tag, x, g_norm, w3, normed=None):
    _, _, f4, d = w3.shape
    h, rstd = normed if normed is not None else rms_fwd(f"{tag}_rms", x, g_norm)
    a, b, s = ffn_up(f"{tag}_up", h, w3)
    x_out = mm_residual(f"{tag}_down", s, w3, pl.BlockSpec((w3.shape[0], None, f4, d), lambda m: (0, DOWN, 0, 0)), x, 0.5)
    return x_out, (x, h, rstd, a, b, s)


def ffn_backward_weights(tag, dx, saved, w3, after=()):
    x, h, rstd, a, b, s = saved
    da, db = ffn_bwd_act(f"{tag}_bwd_act", dx, w3, a, b, after)
    return ffn_wgrad(f"{tag}_wgrad", h, da, db, s, dx), (da, db)


def ffn_backward_input(tag, dx, saved, dab, g_norm, w3, after=()):
    x, h, rstd, a, b, s = saved
    return ffn_dh(f"{tag}_dh", dab[0], dab[1], w3, x, rstd, g_norm, dx, after)


def proj(name, h, wcat_t, bias, first_col, n_cols, tn, out_dtype, scaled_tiles=()):
    t, d = h.shape
    tm = _row_tile(t, 512)
    off = first_col // tn

    def epilogue(acc, ex, outs, ids):
        val = acc + ex[0][...]
        if scaled_tiles:
            hit = functools.reduce(jnp.logical_or, [ids[0] == s for s in scaled_tiles])
            val = val * jnp.where(hit, ATTN_SCALE, 1.0)
        outs[0][...] = val.astype(out_dtype)

    return _mm(
        name, [(h, wcat_t)], NT, (n_cols // tn, t // tm, 1),
        [(pl.BlockSpec((tm, d), lambda j, m, k: (m, 0)), pl.BlockSpec((tn, d), lambda j, m, k: (off + j, 0)))],
        [jax.ShapeDtypeStruct((t, n_cols), out_dtype)], [pl.BlockSpec((tm, tn), lambda j, m, k: (m, j))], (tm, tn), 1, epilogue,
        extras=[bias], extra_specs=[pl.BlockSpec((1, tn), lambda j, m, k: (0, off + j))],
    )[0]


def mix_fwd(name, ya, yb, wup, pc):
    t, w = ya.shape
    ns, _, tn = wup.shape
    d = ns * tn
    tm = _row_tile(t, 512)

    def body(ya_ref, yb_ref, wa_ref, wb_ref, pa_ref, pb_ref, ua_ref, ub_ref, mx_ref):
        ua = _dot(ya_ref[...], wa_ref[...], NN)
        ub = _dot(yb_ref[...], wb_ref[...], NN)
        ua_ref[...] = ua
        ub_ref[...] = ub
        mx_ref[...] = (_sigmoid(pa_ref[...]) * ua + _sigmoid(pb_ref[...]) * ub).astype(BF16)

    y_spec = pl.BlockSpec((tm, w), lambda m, n: (m, 0))
    o_spec = pl.BlockSpec((tm, tn), lambda m, n: (m, n))
    return pl.pallas_call(
        body, name=name, grid=(t // tm, ns),
        in_specs=[y_spec, y_spec, pl.BlockSpec((None, w, tn), lambda m, n: (n, 0, 0)), pl.BlockSpec((None, w, tn), lambda m, n: (n, 1, 0)),
                  o_spec, pl.BlockSpec((tm, tn), lambda m, n: (m, ns + n))],
        out_specs=[o_spec, o_spec, o_spec],
        out_shape=[jax.ShapeDtypeStruct((t, d), F32), jax.ShapeDtypeStruct((t, d), F32), jax.ShapeDtypeStruct((t, d), BF16)],
        compiler_params=_params(("parallel", "parallel")),
    )(ya, yb, wup, wup, pc, pc)


def up_bwd(name, du, wup, branch):
    t, d = du.shape
    ns, w2, tn = wup.shape
    w = w2 // 2
    tm = _row_tile(t, 512)

    def body(du_ref, w_ref, o_ref):
        acc = _dot(du_ref[:, 0:tn], w_ref[0], NT)
        for j in range(1, ns):
            acc += _dot(du_ref[:, j * tn:(j + 1) * tn], w_ref[j], NT)
        o_ref[...] = acc.astype(BF16)

    return pl.pallas_call(
        body, name=name, grid=(t // tm,),
        in_specs=[pl.BlockSpec((tm, d), lambda m: (m, 0)), pl.BlockSpec((ns, w, tn), lambda m: (0, branch, 0))],
        out_specs=pl.BlockSpec((tm, w), lambda m: (m, 0)), out_shape=jax.ShapeDtypeStruct((t, w), BF16),
        compiler_params=_params(("parallel",)),
    )(du, wup)


def up_wgrad(name, ya, yb, dua, dub):
    t, w = ya.shape
    d = dua.shape[1]
    tn = d // N_CHIPS

    def body(ya_ref, yb_ref, dua_ref, dub_ref, o_ref):
        o_ref[0:w, :] = _dot(ya_ref[...], dua_ref[...], TN).astype(BF16)
        o_ref[w:2 * w, :] = _dot(yb_ref[...], dub_ref[...], TN).astype(BF16)

    y_spec = pl.BlockSpec((t, w), lambda j: (0, 0))
    du_spec = pl.BlockSpec((t, tn), lambda j: (0, j))
    return pl.pallas_call(
        body, name=name, grid=(N_CHIPS,), in_specs=[y_spec, y_spec, du_spec, du_spec],
        out_specs=pl.BlockSpec((None, 2 * w, tn), lambda j: (j, 0, 0)),
        out_shape=jax.ShapeDtypeStruct((N_CHIPS, 2 * w, tn), BF16), compiler_params=_params(("parallel",)),
    )(ya, yb, dua, dub)


def mix_bwd(name, dx, wo, pc, ua, ub, after=()):
    t, d = dx.shape
    tm = _row_tile(t, 512)
    tn = 512
    off_a = 0
    off_b = d // tn

    def body(dx_ref, wo_ref, pa_ref, pb_ref, ua_ref, ub_ref, dua_ref, dub_ref, dpa_ref, dpb_ref, ba_ref, bb_ref):
        dm = _dot(dx_ref[...], wo_ref[...], NT)
        ga = _sigmoid(pa_ref[...])
        gb = _sigmoid(pb_ref[...])
        dua_ref[...] = (dm * ga).astype(BF16)
        dub_ref[...] = (dm * gb).astype(BF16)
        dpa = dm * ua_ref[...] * ga * (1.0 - ga)
        dpb = dm * ub_ref[...] * gb * (1.0 - gb)
        dpa_ref[...] = dpa.astype(BF16)
        dpb_ref[...] = dpb.astype(BF16)
        sa = jnp.sum(dpa, axis=0, keepdims=True)
        sb = jnp.sum(dpb, axis=0, keepdims=True)

        @pl.when(pl.program_id(1) == 0)
        def _():
            ba_ref[...] = sa
            bb_ref[...] = sb

        @pl.when(pl.program_id(1) != 0)
        def _():
            ba_ref[...] += sa
            bb_ref[...] += sb

    tile = pl.BlockSpec((tm, tn), lambda n, m: (m, n))
    bias = pl.BlockSpec((1, tn), lambda n, m: (0, n))
    return pl.pallas_call(
        _ordered_after(body, 6, len(after)), name=name, grid=(d // tn, t // tm),
        in_specs=[pl.BlockSpec((tm, d), lambda n, m: (m, 0)), pl.BlockSpec((tn, d), lambda n, m: (n, 0)),
                  pl.BlockSpec((tm, tn), lambda n, m: (m, off_a + n)), pl.BlockSpec((tm, tn), lambda n, m: (m, off_b + n)),
                  tile, tile] + [_ANY] * len(after),
        out_specs=[tile, tile, tile, tile, bias, bias],
        out_shape=[jax.ShapeDtypeStruct((t, d), BF16)] * 4 + [jax.ShapeDtypeStruct((1, d), F32)] * 2,
        compiler_params=_params(("parallel", "arbitrary")),
    )(dx, wo, pc, pc, ua, ub, *after)


def mm_plain(name, a, b, contract, out_dtype, tk_target=512):
    if contract == NN:
        m, kdim = a.shape
        n = b.shape[1]
    elif contract == NT:
        m, kdim = a.shape
        n = b.shape[0]
    else:
        kdim, m = a.shape
        n = b.shape[1]
    tm = _row_tile(m, 512)
    tk = _row_tile(kdim, tk_target)
    nk = kdim // tk
    if contract == TN:
        a_spec = pl.BlockSpec((tk, tm), lambda i, k: (k, i))
    else:
        a_spec = pl.BlockSpec((tm, tk), lambda i, k: (i, k))
    if contract == NT:
        b_spec = pl.BlockSpec((n, tk), lambda i, k: (0, k))
    else:
        b_spec = pl.BlockSpec((tk, n), lambda i, k: (k, 0))

    def epilogue(acc, ex, outs, ids):
        outs[0][...] = acc.astype(out_dtype)

    return _mm(name, [(a, b)], contract, (m // tm, nk), [(a_spec, b_spec)],
               [jax.ShapeDtypeStruct((m, n), out_dtype)], [pl.BlockSpec((tm, n), lambda i, k: (i, 0))], (tm, n), nk, epilogue)[0]


def wgrad_cat(name, h, dcat):
    t, d = h.shape
    n = dcat.shape[1]
    tn = next(c for c in (768, 512, 256, 128) if n % c == 0)
    tk = _row_tile(t, 2048)

    def epilogue(acc, ex, outs, ids):
        outs[0][...] = acc.astype(BF16)

    return _mm(
        name, [(dcat, h)], TN, (n // tn, t // tk),
        [(pl.BlockSpec((tk, tn), lambda j, k: (k, j)), pl.BlockSpec((tk, d), lambda j, k: (k, 0)))],
        [jax.ShapeDtypeStruct((n, d), BF16)], [pl.BlockSpec((tn, d), lambda j, k: (j, 0))], (tn, d), t // tk, epilogue,
    )[0]


def fox_prep(name, f, bias):
    t, lanes = f.shape
    nchunk = t // QB

    def body(f_ref, b_ref, c_ref):
        lower = (_iota2((QB, QB), 1) <= _iota2((QB, QB), 0)).astype(BF16)

        def chunk(n, carry):
            rows = pl.ds(pl.multiple_of(n * QB, QB), QB)
            u = f_ref[rows, :] + b_ref[...]
            lf = jnp.minimum(u, 0.0) - _log1pexp_neg_abs(u)
            c = _dot_exact_lhs01(lower, lf) + carry
            c_ref[rows, :] = c
            return c[QB - 1:QB, :]

        lax.fori_loop(0, nchunk, chunk, jnp.zeros((1, lanes), F32))

    return pl.pallas_call(body, name=name, out_shape=jax.ShapeDtypeStruct((t, lanes), F32),
                          compiler_params=pltpu.CompilerParams(vmem_limit_bytes=VMEM_LIMIT_BYTES))(f, bias)


def fox_gate_bwd(name, dc, f, bias):
    t, lanes = dc.shape
    nchunk = t // QB

    def body(dc_ref, f_ref, b_ref, df_ref, gb_ref):
        upper = (_iota2((QB, QB), 1) >= _iota2((QB, QB), 0)).astype(BF16)

        def chunk(n, carry):
            tail, total = carry
            rows = pl.ds(pl.multiple_of((nchunk - 1 - n) * QB, QB), QB)
            dlf = _dot_exact_lhs01(upper, dc_ref[rows, :]) + tail
            u = f_ref[rows, :] + b_ref[...]
            df = dlf * jnp.exp(jnp.minimum(-u, 0.0) - _log1pexp_neg_abs(u))
            df_ref[rows, :] = df
            return dlf[0:1, :], total + jnp.sum(df, axis=0, keepdims=True)

        zero = jnp.zeros((1, lanes), F32)
        _, total = lax.fori_loop(0, nchunk, chunk, (zero, zero))
        gb_ref[...] = total

    return pl.pallas_call(body, name=name,
                          out_shape=[jax.ShapeDtypeStruct((t, lanes), F32), jax.ShapeDtypeStruct((1, lanes), F32)],
                          compiler_params=pltpu.CompilerParams(vmem_limit_bytes=VMEM_LIMIT_BYTES))(dc, f, bias)


def _qrows(i):
    return pl.ds(pl.multiple_of(i * BQ, BQ), BQ)


def _krows(kc):
    return pl.ds(pl.multiple_of(kc * CS, CS), CS)


def _head_spec(t, offset):
    return pl.BlockSpec((None, t, HEAD_DIM), lambda h: (offset + h, 0, 0))


def _head_t_spec(nq, offset):
    return pl.BlockSpec((None, nq, HEAD_DIM, BQ), lambda h: (offset + h, 0, 0, 0))


def _chunk_t_spec(nc):
    return pl.BlockSpec((None, nc, HEAD_DIM, CS), lambda h: (h, 0, 0, 0))


def _dot_split2_rhs01(x, m01):
    hi = x.astype(BF16)
    lo = (x - hi.astype(F32)).astype(BF16)
    d = lambda p: lax.dot_general(p, m01, (NN, ((), ())), preferred_element_type=F32)
    return d(hi) + d(lo)


def _live_rows(dchunk):
    return 0 if dchunk is None else dchunk * CS


def _diag_mask(dchunk, inclusive):
    shape = (BQ - _live_rows(dchunk), CS)
    return _iota2(shape, 1) <= _iota2(shape, 0) if inclusive else _iota2(shape, 1) < _iota2(shape, 0)


def _tail(x, r0, axis=0):
    return x if r0 == 0 else (x[r0:] if axis == 0 else x[:, r0:])


def _with_tail(old, tail, r0):
    return tail if r0 == 0 else jnp.concatenate([old[:r0], tail], axis=0)


def _walk_chunks(i, step, init, right_to_left):
    order = list(reversed(range(N_SUB))) if right_to_left else list(range(N_SUB))

    def diagonal(state):
        for dchunk in order:
            state = step(i * N_SUB + dchunk, state, dchunk)
        return state

    def group(n, state):
        base = ((i - 1 - n) if right_to_left else n) * N_SUB
        for dchunk in order:
            state = step(base + dchunk, state, None)
        return state

    if right_to_left:
        return lax.fori_loop(0, i, group, diagonal(init))
    return diagonal(lax.fori_loop(0, i, group, init))


def sb_fwd(name, qkv):
    t = qkv.shape[1]

    def body(q_ref, k_ref, v_ref, o_ref):
        later = (_iota2((CS, CS), 0) > _iota2((CS, CS), 1)).astype(BF16)

        def qblock(i, _):
            q = q_ref[_qrows(i), :]

            def step(kc, state, dchunk):
                carry, acc = state
                z = _dot(q, k_ref[_krows(kc), :], NT)
                sp = _log1pexp_neg_abs(z)
                lnb = -jnp.maximum(z, 0.0) - sp
                if dchunk is not None:
                    lnb = jnp.where(_diag_mask(dchunk, False), lnb, 0.0)
                w = jnp.exp(jnp.minimum(z, 0.0) - sp + _dot_split2_rhs01(lnb, later) + carry)
                if dchunk is not None:
                    w = jnp.where(_diag_mask(dchunk, False), w, 0.0)
                acc = acc + _dot(w, v_ref[_krows(kc), :], NN)
                return carry + jnp.sum(lnb, axis=1, keepdims=True), acc

            init = (jnp.zeros((BQ, 1), F32), jnp.zeros((BQ, HEAD_DIM), F32))
            _, acc = _walk_chunks(i, step, init, True)
            o_ref[_qrows(i), :] = acc.astype(BF16)
            return 0

        lax.fori_loop(0, t // BQ, qblock, 0)

    return pl.pallas_call(
        body, name=name, grid=(N_HEADS,),
        in_specs=[_head_spec(t, 0), _head_spec(t, N_HEADS), _head_spec(t, 2 * N_HEADS)],
        out_specs=_head_spec(t, 0), out_shape=jax.ShapeDtypeStruct((N_HEADS, t, HEAD_DIM), BF16),
        compiler_params=_params(("parallel",)),
    )(qkv, qkv, qkv)


def sb_bwd(name, qkv, qt, dy, dyt):
    t = qkv.shape[1]
    nq, nc = t // BQ, t // CS

    def body(q_ref, k_ref, v_ref, qt_ref, do_ref, dot_ref, dq_ref, dkt_ref, dvt_ref, g_s, b_s, dkt_acc, dvt_acc):
        later = (_iota2((CS, CS), 0) > _iota2((CS, CS), 1)).astype(BF16)
        earlier = (_iota2((CS, CS), 0) < _iota2((CS, CS), 1)).astype(BF16)
        dkt_acc[...] = jnp.zeros_like(dkt_acc)
        dvt_acc[...] = jnp.zeros_like(dvt_acc)

        def qblock(i, _):
            q = q_ref[_qrows(i), :]
            do = do_ref[_qrows(i), :]
            q_t = qt_ref[i]
            do_t = dot_ref[i]

            def step1(kc, carry, dchunk):
                z = _dot(q, k_ref[_krows(kc), :], NT)
                sp = _log1pexp_neg_abs(z)
                lnb = -jnp.maximum(z, 0.0) - sp
                lsz = jnp.minimum(z, 0.0) - sp
                if dchunk is not None:
                    lnb = jnp.where(_diag_mask(dchunk, False), lnb, 0.0)
                w = jnp.exp(lsz + _dot_split2_rhs01(lnb, later) + carry)
                if dchunk is not None:
                    w = jnp.where(_diag_mask(dchunk, False), w, 0.0)
                g_s[kc] = w * _dot(do, v_ref[_krows(kc), :], NT)
                b_s[kc] = jnp.exp(lsz)
                dvt_acc[kc] += _dot(do_t, w, NN)
                return carry + jnp.sum(lnb, axis=1, keepdims=True)

            _walk_chunks(i, step1, jnp.zeros((BQ, 1), F32), True)

            def step2(kc, state, dchunk):
                before, dq = state
                g = g_s[kc]
                beta = b_s[kc]
                dz = g * (1.0 - beta) - beta * (_dot_split2_rhs01(g, earlier) + before)
                if dchunk is not None:
                    dz = jnp.where(_diag_mask(dchunk, False), dz, 0.0)
                dzb = dz.astype(BF16)
                dq = dq + _dot(dzb, k_ref[_krows(kc), :], NN)
                dkt_acc[kc] += _dot(q_t, dzb, NN)
                return before + jnp.sum(g, axis=1, keepdims=True), dq

            _, dq = _walk_chunks(i, step2, (jnp.zeros((BQ, 1), F32), jnp.zeros((BQ, HEAD_DIM), F32)), False)
            dq_ref[_qrows(i), :] = (dq * ATTN_SCALE).astype(BF16)
            return 0

        lax.fori_loop(0, nq, qblock, 0)
        dkt_ref[...] = dkt_acc[...].astype(BF16)
        dvt_ref[...] = dvt_acc[...].astype(BF16)

    chunked = jax.ShapeDtypeStruct((N_HEADS, nc, HEAD_DIM, CS), BF16)
    return pl.pallas_call(
        body, name=name, grid=(N_HEADS,),
        in_specs=[_head_spec(t, 0), _head_spec(t, N_HEADS), _head_spec(t, 2 * N_HEADS), _head_t_spec(nq, 0),
                  _head_spec(t, 0), _head_t_spec(nq, 0)],
        out_specs=[_head_spec(t, 0), _chunk_t_spec(nc), _chunk_t_spec(nc)],
        out_shape=[jax.ShapeDtypeStruct((N_HEADS, t, HEAD_DIM), BF16), chunked, chunked],
        scratch_shapes=[pltpu.VMEM((nc, BQ, CS), F32), pltpu.VMEM((nc, BQ, CS), F32),
                        pltpu.VMEM((nc, HEAD_DIM, CS), F32), pltpu.VMEM((nc, HEAD_DIM, CS), F32)],
        compiler_params=_params(("parallel",)),
    )(qkv, qkv, qkv, qt, dy, dyt)


def _col_spec(t):
    return pl.BlockSpec((None, t, 1), lambda h: (h, 0, 0))


def _row_spec(nc):
    return pl.BlockSpec((None, nc, 1, CS), lambda h: (h, 0, 0, 0))


def fox_fwd(name, qkv, c_col, c_row):
    t = qkv.shape[1]

    def body(q_ref, k_ref, v_ref, cc_ref, cr_ref, o_ref, lse_ref):
        def qblock(i, _):
            q = q_ref[_qrows(i), :]
            ct = cc_ref[_qrows(i), :]

            def step(kc, state, dchunk):
                m, l, acc = state
                s = _dot(q, k_ref[_krows(kc), :], NT) + ct - cr_ref[kc]
                if dchunk is not None:
                    s = jnp.where(_diag_mask(dchunk, True), s, NEG_BIG)
                m_new = jnp.maximum(m, jnp.max(s, axis=1, keepdims=True))
                alpha = jnp.exp(m - m_new)
                p = jnp.exp(s - m_new)
                if dchunk is not None:
                    p = jnp.where(_diag_mask(dchunk, True), p, 0.0)
                l = l * alpha + jnp.sum(p, axis=1, keepdims=True)
                acc = acc * alpha + _dot(p, v_ref[_krows(kc), :], NN)
                return m_new, l, acc

            init = (jnp.full((BQ, 1), NEG_BIG, F32), jnp.zeros((BQ, 1), F32), jnp.zeros((BQ, HEAD_DIM), F32))
            m, l, acc = _walk_chunks(i, step, init, False)
            o_ref[_qrows(i), :] = (acc / l).astype(BF16)
            lse_ref[_qrows(i), :] = m + jnp.log(l)
            return 0

        lax.fori_loop(0, t // BQ, qblock, 0)

    return pl.pallas_call(
        body, name=name, grid=(N_HEADS,),
        in_specs=[_head_spec(t, 3 * N_HEADS), _head_spec(t, 4 * N_HEADS), _head_spec(t, 5 * N_HEADS),
                  _col_spec(t), _row_spec(t // CS)],
        out_specs=[_head_spec(t, 0), _col_spec(t)],
        out_shape=[jax.ShapeDtypeStruct((N_HEADS, t, HEAD_DIM), BF16), jax.ShapeDtypeStruct((N_HEADS, t, 1), F32)],
        compiler_params=_params(("parallel",)),
    )(qkv, qkv, qkv, c_col, c_row)


def fox_bwd(name, qkv, qt, y, dy, dyt, lse, c_col, c_row):
    t = qkv.shape[1]
    nq, nc = t // BQ, t // CS

    def body(q_ref, k_ref, v_ref, qt_ref, o_ref, do_ref, dot_ref, lse_ref, cc_ref, cr_ref,
             dq_ref, dkt_ref, dvt_ref, dcc_ref, dcr_ref, dkt_acc, dvt_acc, dcr_acc):
        dkt_acc[...] = jnp.zeros_like(dkt_acc)
        dvt_acc[...] = jnp.zeros_like(dvt_acc)
        dcr_acc[...] = jnp.zeros_like(dcr_acc)

        def qblock(i, _):
            q = q_ref[_qrows(i), :]
            do = do_ref[_qrows(i), :]
            q_t = qt_ref[i]
            do_t = dot_ref[i]
            ct = cc_ref[_qrows(i), :]
            lse_i = lse_ref[_qrows(i), :]
            delta = jnp.sum(do.astype(F32) * o_ref[_qrows(i), :].astype(F32), axis=1, keepdims=True)

            def step(kc, state, dchunk):
                dq, dct = state
                s = _dot(q, k_ref[_krows(kc), :], NT) + ct - cr_ref[kc]
                p = jnp.exp(s - lse_i)
                if dchunk is not None:
                    p = jnp.where(_diag_mask(dchunk, True), p, 0.0)
                ds = p * (_dot(do, v_ref[_krows(kc), :], NT) - delta)
                dvt_acc[kc] += _dot(do_t, p, NN)
                dsb = ds.astype(BF16)
                dq = dq + _dot(dsb, k_ref[_krows(kc), :], NN)
                dkt_acc[kc] += _dot(q_t, dsb, NN)
                dcr_acc[kc] -= jnp.sum(ds, axis=0, keepdims=True)
                return dq, dct + jnp.sum(ds, axis=1, keepdims=True)

            dq, dct = _walk_chunks(i, step, (jnp.zeros((BQ, HEAD_DIM), F32), jnp.zeros((BQ, 1), F32)), False)
            dq_ref[_qrows(i), :] = (dq * ATTN_SCALE).astype(BF16)
            dcc_ref[_qrows(i), :] = dct
            return 0

        lax.fori_loop(0, nq, qblock, 0)
        dkt_ref[...] = dkt_acc[...].astype(BF16)
        dvt_ref[...] = dvt_acc[...].astype(BF16)
        dcr_ref[...] = dcr_acc[...]

    chunked = jax.ShapeDtypeStruct((N_HEADS, nc, HEAD_DIM, CS), BF16)
    return pl.pallas_call(
        body, name=name, grid=(N_HEADS,),
        in_specs=[_head_spec(t, 3 * N_HEADS), _head_spec(t, 4 * N_HEADS), _head_spec(t, 5 * N_HEADS), _head_t_spec(nq, N_HEADS),
                  _head_spec(t, 0), _head_spec(t, 0), _head_t_spec(nq, 0), _col_spec(t), _col_spec(t), _row_spec(nc)],
        out_specs=[_head_spec(t, 0), _chunk_t_spec(nc), _chunk_t_spec(nc), _col_spec(t), _row_spec(nc)],
        out_shape=[jax.ShapeDtypeStruct((N_HEADS, t, HEAD_DIM), BF16), chunked, chunked,
                   jax.ShapeDtypeStruct((N_HEADS, t, 1), F32), jax.ShapeDtypeStruct((N_HEADS, nc, 1, CS), F32)],
        scratch_shapes=[pltpu.VMEM((nc, HEAD_DIM, CS), F32), pltpu.VMEM((nc, HEAD_DIM, CS), F32), pltpu.VMEM((nc, 1, CS), F32)],
        compiler_params=_params(("parallel",)),
    )(qkv, qkv, qkv, qt, y, dy, dyt, lse, c_col, c_row)


PAIR = 2 * HEAD_DIM
N_PAIRS = N_HEADS // 2


def _pair_spec(t, first_block):
    return pl.BlockSpec((t, PAIR), lambda p, *_: (0, first_block + p))


def _head_lanes(shape):
    lane = _iota2(shape, len(shape) - 1)
    return [lane < HEAD_DIM, lane >= HEAD_DIM]


def _only_head(x, lanes_of_head):
    return jnp.where(lanes_of_head, x, jnp.zeros_like(x))


LOG2_E = 1.4426950408889634


def _sb_chunk_weights(q_h, k, later, carry, dchunk):
    z = _dot(q_h, k, NT) * LOG2_E
    lnb = -jnp.maximum(z, 0.0) - jnp.log2(1.0 + jnp.exp2(-jnp.abs(z)))
    lsz = lnb + z
    if dchunk is not None:
        lnb = jnp.where(_diag_mask(dchunk, False), lnb, 0.0)
    w = jnp.exp2(lsz + _dot_split2_rhs01(lnb, later) + carry)
    if dchunk is not None:
        w = jnp.where(_diag_mask(dchunk, False), w, 0.0)
    return w, lsz, lnb


def sb_pair_fwd(name, qkv):
    t = qkv.shape[0]

    def body(q_ref, k_ref, v_ref, o_ref):
        later = (_iota2((CS, CS), 0) > _iota2((CS, CS), 1)).astype(BF16)
        lanes = _head_lanes((BQ, PAIR))

        def qblock(i, _):
            q = q_ref[_qrows(i), :]
            q_heads = [_only_head(q, lanes[h]) for h in range(2)]

            def step(kc, state, dchunk):
                k = k_ref[_krows(kc), :]
                v = v_ref[_krows(kc), :]
                out = []
                r0 = _live_rows(dchunk)
                for h in range(2):
                    carry, acc = state[h]
                    w, _, lnb = _sb_chunk_weights(_tail(q_heads[h], r0), k, later, _tail(carry, r0), dchunk)
                    out.append((_with_tail(carry, _tail(carry, r0) + jnp.sum(lnb, axis=1, keepdims=True), r0),
                                _with_tail(acc, _tail(acc, r0) + _dot(w, v, NN), r0)))
                return tuple(out)

            zero = (jnp.zeros((BQ, 1), F32), jnp.zeros((BQ, PAIR), F32))
            (_, acc0), (_, acc1) = _walk_chunks(i, step, (zero, zero), True)
            o_ref[_qrows(i), :] = jnp.where(lanes[0], acc0, acc1).astype(BF16)
            return 0

        lax.fori_loop(0, t // BQ, qblock, 0)

    return pl.pallas_call(
        body, name=name, grid=(N_PAIRS,),
        in_specs=[_pair_spec(t, 0), _pair_spec(t, N_PAIRS), _pair_spec(t, 2 * N_PAIRS)],
        out_specs=_pair_spec(t, 0), out_shape=jax.ShapeDtypeStruct((t, WIDTH), BF16),
        compiler_params=_params(("parallel",)),
    )(qkv, qkv, qkv)


def _emit_dqkv(res, o_ref):
    o_ref[...] = res[pl.program_id(1)]


def _flush_transposed(acc, res, which):
    for kc in range(acc.shape[0]):
        res[which, kc * CS:(kc + 1) * CS, :] = acc[kc].T.astype(BF16)


def sb_pair_bwd(name, qkv, dy, dqkv):
    t = qkv.shape[0]
    nc = t // CS

    def body(q_ref, k_ref, v_ref, do_ref, _, o_ref, g_s, b_s, dkt_acc, dvt_acc, res):
        @pl.when(pl.program_id(1) == 0)
        def _():
            later = (_iota2((CS, CS), 0) > _iota2((CS, CS), 1)).astype(BF16)
            earlier = (_iota2((CS, CS), 0) < _iota2((CS, CS), 1)).astype(BF16)
            lanes = _head_lanes((BQ, PAIR))
            dkt_acc[...] = jnp.zeros_like(dkt_acc)
            dvt_acc[...] = jnp.zeros_like(dvt_acc)

            def qblock(i, _):
                q = q_ref[_qrows(i), :]
                do = do_ref[_qrows(i), :]
                q_heads = [_only_head(q, lanes[h]) for h in range(2)]
                do_heads = [_only_head(do, lanes[h]) for h in range(2)]
                qt_heads = [qh.astype(F32).T.astype(BF16) for qh in q_heads]
                dot_heads = [dh.astype(F32).T.astype(BF16) for dh in do_heads]

                def step1(kc, carries, dchunk):
                    k = k_ref[_krows(kc), :]
                    v = v_ref[_krows(kc), :]
                    out = []
                    r0 = _live_rows(dchunk)
                    for h in range(2):
                        carry = _tail(carries[h], r0)
                        w, lsz, lnb = _sb_chunk_weights(_tail(q_heads[h], r0), k, later, carry, dchunk)
                        g_s[h, kc, r0:, :] = (w * _dot(_tail(do_heads[h], r0), v, NT)).astype(BF16)
                        b_s[h, kc, r0:, :] = jnp.exp2(lsz).astype(BF16)
                        dvt_acc[kc] += _dot(_tail(dot_heads[h], r0, axis=1), w, NN)
                        out.append(_with_tail(carries[h], carry + jnp.sum(lnb, axis=1, keepdims=True), r0))
                    return tuple(out)

                zero = jnp.zeros((BQ, 1), F32)
                _walk_chunks(i, step1, (zero, zero), True)

                def step2(kc, state, dchunk):
                    k = k_ref[_krows(kc), :]
                    out = []
                    r0 = _live_rows(dchunk)
                    for h in range(2):
                        before, dq = state[h]
                        g16 = g_s[h, kc, r0:, :]
                        g = g16.astype(F32)
                        beta = b_s[h, kc, r0:, :].astype(F32)
                        prefix = lax.dot_general(g16, earlier, (NN, ((), ())), preferred_element_type=F32) + _tail(before, r0)
                        dz = g * (1.0 - beta) - beta * prefix
                        if dchunk is not None:
                            dz = jnp.where(_diag_mask(dchunk, False), dz, 0.0)
                        dzb = dz.astype(BF16)
                        dkt_acc[kc] += _dot(_tail(qt_heads[h], r0, axis=1), dzb, NN)
                        out.append((_with_tail(before, _tail(before, r0) + jnp.sum(g, axis=1, keepdims=True), r0),
                                    _with_tail(dq, _tail(dq, r0) + _dot(dzb, k, NN), r0)))
                    return tuple(out)

                start = (zero, jnp.zeros((BQ, PAIR), F32))
                (_, dq0), (_, dq1) = _walk_chunks(i, step2, (start, start), False)
                res[0, _qrows(i), :] = (jnp.where(lanes[0], dq0, dq1) * ATTN_SCALE).astype(BF16)
                return 0

            lax.fori_loop(0, t // BQ, qblock, 0)
            _flush_transposed(dkt_acc, res, 1)
            _flush_transposed(dvt_acc, res, 2)

        _emit_dqkv(res, o_ref)

    return pl.pallas_call(
        body, name=name, grid=(N_PAIRS, 3),
        in_specs=[_pair_spec(t, 0), _pair_spec(t, N_PAIRS), _pair_spec(t, 2 * N_PAIRS), _pair_spec(t, 0), _ANY],
        out_specs=pl.BlockSpec((t, PAIR), lambda p, s: (0, s * N_PAIRS + p)),
        out_shape=jax.ShapeDtypeStruct(dqkv.shape, BF16), input_output_aliases={4: 0},
        scratch_shapes=[pltpu.VMEM((2, nc, BQ, CS), BF16), pltpu.VMEM((2, nc, BQ, CS), BF16),
                        pltpu.VMEM((nc, PAIR, CS), F32), pltpu.VMEM((nc, PAIR, CS), F32), pltpu.VMEM((3, t, PAIR), BF16)],
        compiler_params=_params(("parallel", "arbitrary")),
    )(qkv, qkv, qkv, dy, dqkv)


def _gates_col_spec(t):
    return pl.BlockSpec((2, t, 1), lambda p, *_: (p, 0, 0))


def _gates_row_spec(nc):
    return pl.BlockSpec((2, nc, 1, CS), lambda p, *_: (p, 0, 0, 0))


def fox_pair_fwd(name, qkv, c_col, c_row):
    t = qkv.shape[0]

    def body(q_ref, k_ref, v_ref, cc_ref, cr_ref, o_ref, lse_ref):
        lanes = _head_lanes((BQ, PAIR))

        def qblock(i, _):
            q = q_ref[_qrows(i), :]
            q_heads = [_only_head(q, lanes[h]) for h in range(2)]
            ct = [cc_ref[h, _qrows(i), :] for h in range(2)]

            def step(kc, state, dchunk):
                k = k_ref[_krows(kc), :]
                v = v_ref[_krows(kc), :]
                out = []
                r0 = _live_rows(dchunk)
                for h in range(2):
                    m, l, acc = (_tail(a, r0) for a in state[h])
                    s = _dot(_tail(q_heads[h], r0), k, NT) + _tail(ct[h], r0) - cr_ref[h, kc]
                    if dchunk is not None:
                        s = jnp.where(_diag_mask(dchunk, True), s, NEG_BIG)
                    m_new = jnp.maximum(m, jnp.max(s, axis=1, keepdims=True))
                    alpha = jnp.exp(m - m_new)
                    p = jnp.exp(s - m_new)
                    if dchunk is not None:
                        p = jnp.where(_diag_mask(dchunk, True), p, 0.0)
                    new = (m_new, l * alpha + jnp.sum(p, axis=1, keepdims=True), acc * alpha + _dot(p, v, NN))
                    out.append(tuple(_with_tail(old, tail, r0) for old, tail in zip(state[h], new)))
                return tuple(out)

            init = (jnp.full((BQ, 1), NEG_BIG, F32), jnp.zeros((BQ, 1), F32), jnp.zeros((BQ, PAIR), F32))
            (m0, l0, acc0), (m1, l1, acc1) = _walk_chunks(i, step, (init, init), False)
            o_ref[_qrows(i), :] = jnp.where(lanes[0], acc0 / l0, acc1 / l1).astype(BF16)
            lse_ref[0, _qrows(i), :] = m0 + jnp.log(l0)
            lse_ref[1, _qrows(i), :] = m1 + jnp.log(l1)
            return 0

        lax.fori_loop(0, t // BQ, qblock, 0)

    return pl.pallas_call(
        body, name=name, grid=(N_PAIRS,),
        in_specs=[_pair_spec(t, 3 * N_PAIRS), _pair_spec(t, 4 * N_PAIRS), _pair_spec(t, 5 * N_PAIRS),
                  _gates_col_spec(t), _gates_row_spec(t // CS)],
        out_specs=[_pair_spec(t, 0), _gates_col_spec(t)],
        out_shape=[jax.ShapeDtypeStruct((t, WIDTH), BF16), jax.ShapeDtypeStruct((N_HEADS, t, 1), F32)],
        compiler_params=_params(("parallel",)),
    )(qkv, qkv, qkv, c_col, c_row)


def fox_pair_bwd(name, qkv, y, dy, lse, c_col, c_row, dqkv):
    t = qkv.shape[0]
    nc = t // CS

    def body(q_ref, k_ref, v_ref, o_in_ref, do_ref, lse_ref, cc_ref, cr_ref, _, o_ref, dcc_ref, dcr_ref,
             dkt_acc, dvt_acc, dcr_acc, res):
        @pl.when(pl.program_id(1) == 0)
        def _():
            lanes = _head_lanes((BQ, PAIR))
            dkt_acc[...] = jnp.zeros_like(dkt_acc)
            dvt_acc[...] = jnp.zeros_like(dvt_acc)
            dcr_acc[...] = jnp.zeros_like(dcr_acc)

            def qblock(i, _):
                q = q_ref[_qrows(i), :]
                do = do_ref[_qrows(i), :]
                q_heads = [_only_head(q, lanes[h]) for h in range(2)]
                do_heads = [_only_head(do, lanes[h]) for h in range(2)]
                qt_heads = [qh.astype(F32).T.astype(BF16) for qh in q_heads]
                dot_heads = [dh.astype(F32).T.astype(BF16) for dh in do_heads]
                prod = do.astype(F32) * o_in_ref[_qrows(i), :].astype(F32)
                delta = [jnp.sum(_only_head(prod, lanes[h]), axis=1, keepdims=True) for h in range(2)]
                ct = [cc_ref[h, _qrows(i), :] for h in range(2)]
                lse_i = [lse_ref[h, _qrows(i), :] for h in range(2)]

                def step(kc, state, dchunk):
                    k = k_ref[_krows(kc), :]
                    v = v_ref[_krows(kc), :]
                    out = []
                    r0 = _live_rows(dchunk)
                    for h in range(2):
                        dq, dct = state[h]
                        s = _dot(_tail(q_heads[h], r0), k, NT) + _tail(ct[h], r0) - cr_ref[h, kc]
                        p = jnp.exp(s - _tail(lse_i[h], r0))
                        if dchunk is not None:
                            p = jnp.where(_diag_mask(dchunk, True), p, 0.0)
                        ds = p * (_dot(_tail(do_heads[h], r0), v, NT) - _tail(delta[h], r0))
                        dvt_acc[kc] += _dot(_tail(dot_heads[h], r0, axis=1), p, NN)
                        dsb = ds.astype(BF16)
                        dkt_acc[kc] += _dot(_tail(qt_heads[h], r0, axis=1), dsb, NN)
                        dcr_acc[h, kc] -= jnp.sum(ds, axis=0, keepdims=True)
                        out.append((_with_tail(dq, _tail(dq, r0) + _dot(dsb, k, NN), r0),
                                    _with_tail(dct, _tail(dct, r0) + jnp.sum(ds, axis=1, keepdims=True), r0)))
                    return tuple(out)

                zero = (jnp.zeros((BQ, PAIR), F32), jnp.zeros((BQ, 1), F32))
                (dq0, dct0), (dq1, dct1) = _walk_chunks(i, step, (zero, zero), False)
                res[0, _qrows(i), :] = (jnp.where(lanes[0], dq0, dq1) * ATTN_SCALE).astype(BF16)
                lane = _iota2((BQ, PAIR), 1)
                dcc_ref[_qrows(i), :] = jnp.where(lane == 0, dct0, jnp.where(lane == 1, dct1, 0.0))
                return 0

            lax.fori_loop(0, t // BQ, qblock, 0)
            _flush_transposed(dkt_acc, res, 1)
            _flush_transposed(dvt_acc, res, 2)
            dcr_ref[...] = dcr_acc[...]

        _emit_dqkv(res, o_ref)

    return pl.pallas_call(
        body, name=name, grid=(N_PAIRS, 3),
        in_specs=[_pair_spec(t, 3 * N_PAIRS), _pair_spec(t, 4 * N_PAIRS), _pair_spec(t, 5 * N_PAIRS), _pair_spec(t, 0),
                  _pair_spec(t, 0), _gates_col_spec(t), _gates_col_spec(t), _gates_row_spec(nc), _ANY],
        out_specs=[pl.BlockSpec((t, PAIR), lambda p, s: (0, (3 + s) * N_PAIRS + p)),
                   pl.BlockSpec((None, t, PAIR), lambda p, s: (p, 0, 0)), _gates_row_spec(nc)],
        out_shape=[jax.ShapeDtypeStruct(dqkv.shape, BF16), jax.ShapeDtypeStruct((N_PAIRS, t, PAIR), F32),
                   jax.ShapeDtypeStruct((N_HEADS, nc, 1, CS), F32)],
        input_output_aliases={8: 0},
        scratch_shapes=[pltpu.VMEM((nc, PAIR, CS), F32), pltpu.VMEM((nc, PAIR, CS), F32), pltpu.VMEM((2, nc, 1, CS), F32),
                        pltpu.VMEM((3, t, PAIR), BF16)],
        compiler_params=_params(("parallel", "arbitrary")),
    )(qkv, qkv, qkv, y, dy, lse, c_col, c_row, dqkv)


def loss_head(name, x, g, target):
    t, d = x.shape
    tr = _row_tile(t, 256)

    def body(x_ref, g_ref, t_ref, dx_ref, gn_ref, loss_ref):
        xv = x_ref[...]
        r = lax.rsqrt(jnp.mean(xv * xv, axis=-1, keepdims=True) + RMS_EPS)
        xhat = xv * r
        gv = g_ref[...]
        err = xhat * gv - t_ref[...]
        part_loss = 0.5 * jnp.sum(jnp.mean(err * err, axis=-1, keepdims=True), axis=0, keepdims=True)
        dy = err * (1.0 / d)
        dyg = dy * gv
        dx_ref[...] = r * (dyg - xhat * jnp.mean(dyg * xhat, axis=-1, keepdims=True))
        part_g = jnp.sum(dy * xhat, axis=0, keepdims=True)

        @pl.when(pl.program_id(0) == 0)
        def _():
            gn_ref[...] = part_g
            loss_ref[...] = part_loss

        @pl.when(pl.program_id(0) != 0)
        def _():
            gn_ref[...] += part_g
            loss_ref[...] += part_loss

    row = pl.BlockSpec((tr, d), lambda i: (i, 0))
    return pl.pallas_call(
        body, name=name, grid=(t // tr,),
        in_specs=[row, pl.BlockSpec((1, d), lambda i: (0, 0)), row],
        out_specs=[row, pl.BlockSpec((1, d), lambda i: (0, 0)), pl.BlockSpec((1, 1), lambda i: (0, 0))],
        out_shape=[jax.ShapeDtypeStruct((t, d), F32), jax.ShapeDtypeStruct((1, d), F32), jax.ShapeDtypeStruct((1, 1), F32)],
        compiler_params=_params(("arbitrary",)),
    )(x, g, target)


def _place():
    return lax.axis_index("x"), lax.axis_index("y"), lax.axis_index("c")


def _other_chips(x, y):
    return [(1 - x, y), (x, 1 - y), (1 - x, 1 - y)]


def _half(ref, c, rows):
    if rows % 32 == 0:
        return ref.at[:, pl.ds(c * (rows // 2), rows // 2), :]
    cols = ref.shape[2]
    return ref.at[:, :, pl.ds(c * (cols // 2), cols // 2)]


_ANY = pl.BlockSpec(memory_space=pl.ANY)


def gather_weights(name, bufs):
    n = len(bufs)

    def body(*refs):
        outs = refs[n:2 * n]
        send_sems, recv_sems = refs[2 * n:]
        x, y, c = _place()
        chips = _other_chips(x, y)
        me = 2 * x + y
        sibling = (x, y, 1 - c)
        first, passed = [], []
        for i in range(n):
            rows = outs[i].shape[2]
            mine = _half(outs[i].at[me], c, rows)
            for j, (qx, qy) in enumerate(chips):
                k = 6 * i + j
                rc = pltpu.make_async_remote_copy(
                    src_ref=mine, dst_ref=mine,
                    send_sem=send_sems.at[k], recv_sem=recv_sems.at[k], device_id=(qx, qy, c), device_id_type=MESH)
                rc.start()
                first.append(rc)
        for i in range(n):
            rows = outs[i].shape[2]
            for j, (qx, qy) in enumerate(chips):
                k = 6 * i + j
                block = _half(outs[i].at[2 * qx + qy], c, rows)
                pltpu.make_async_remote_copy(
                    src_ref=block, dst_ref=block, send_sem=send_sems.at[k], recv_sem=recv_sems.at[k],
                    device_id=(qx, qy, c), device_id_type=MESH).wait_recv()
                fw = pltpu.make_async_remote_copy(
                    src_ref=block, dst_ref=block, send_sem=send_sems.at[k + 3], recv_sem=recv_sems.at[k + 3],
                    device_id=sibling, device_id_type=MESH)
                fw.start()
                passed.append(fw)
        for i in range(n):
            rows = outs[i].shape[2]
            for j, (qx, qy) in enumerate(chips):
                k = 6 * i + j + 3
                block = _half(outs[i].at[2 * qx + qy], 1 - c, rows)
                pltpu.make_async_remote_copy(
                    src_ref=block, dst_ref=block, send_sem=send_sems.at[k], recv_sem=recv_sems.at[k],
                    device_id=sibling, device_id_type=MESH).wait_recv()
        for cp in first + passed:
            cp.wait_send()

    return pl.pallas_call(
        body, name=name, in_specs=[_ANY] * n, out_specs=[_ANY] * n,
        out_shape=[jax.ShapeDtypeStruct(b.shape, b.dtype) for b in bufs],
        input_output_aliases={i: i for i in range(n)},
        scratch_shapes=[pltpu.SemaphoreType.DMA((6 * n,)), pltpu.SemaphoreType.DMA((6 * n,))],
        compiler_params=pltpu.CompilerParams(has_side_effects=True),
    )(*bufs)


_HBM = pl.BlockSpec(memory_space=pltpu.HBM)
_SEM = pl.BlockSpec(memory_space=pltpu.SEMAPHORE)
_DATAFLOW = pltpu.SideEffectType.DATAFLOW_SIDE_EFFECTING


def _in_hbm(a):
    return pltpu.with_memory_space_constraint(a, pltpu.HBM)


def _gather_ici_copies(bufs, send_sems, recv_sems, arrivals):
    x, y, c = _place()
    me = 2 * x + y
    copies = []
    for i, buf in enumerate(bufs):
        rows = buf.shape[2]
        for j, (qx, qy) in enumerate(_other_chips(x, y)):
            block = _half(buf.at[2 * qx + qy if arrivals else me], c, rows)
            copies.append(pltpu.make_async_remote_copy(
                src_ref=block, dst_ref=block, send_sem=send_sems.at[3 * i + j], recv_sem=recv_sems.at[3 * i + j],
                device_id=(qx, qy, c), device_id_type=MESH))
    return copies


def gather_ici_start(name, bufs, after):
    n = len(bufs)

    def body(*refs):
        ins = refs[:n]
        send_sems, recv_sems = refs[n + 1], refs[n + 2]
        token = refs[-1]
        for send in _gather_ici_copies(ins, send_sems, recv_sems, False):
            send.start()
        token[...] = jnp.zeros_like(token)

    res = pl.pallas_call(
        body, name=name,
        out_shape=(pltpu.SemaphoreType.DMA((3 * n,)), pltpu.SemaphoreType.DMA((3 * n,)), *[pltpu.HBM(b.shape, b.dtype) for b in bufs],
                   jax.ShapeDtypeStruct((8, 128), F32)),
        in_specs=[_HBM] * n + [_ANY], out_specs=(_SEM, _SEM, *[_HBM] * n, pl.BlockSpec(memory_space=pltpu.VMEM)),
        input_output_aliases={i: 2 + i for i in range(n)},
        compiler_params=pltpu.CompilerParams(has_side_effects=_DATAFLOW),
    )(*[_in_hbm(b) for b in bufs], after)
    return res[0], res[1], list(res[2:2 + n]), res[-1]


def gather_ici_wait(name, send_sems, recv_sems, bufs, after):
    n = len(bufs)

    def body(*refs):
        ins = refs[:n]
        send_sems_ref, recv_sems_ref = refs[n], refs[n + 1]
        for send in _gather_ici_copies(ins, send_sems_ref, recv_sems_ref, False):
            send.wait_send()
        for recv in _gather_ici_copies(ins, send_sems_ref, recv_sems_ref, True):
            recv.wait_recv()

    return pl.pallas_call(
        body, name=name, out_shape=tuple(pltpu.HBM(b.shape, b.dtype) for b in bufs),
        in_specs=[_HBM] * n + [_SEM, _SEM, _ANY], out_specs=tuple([_HBM] * n),
        input_output_aliases={i: i for i in range(n)},
        compiler_params=pltpu.CompilerParams(has_side_effects=_DATAFLOW),
    )(*bufs, send_sems, recv_sems, after)


def gather_forward(name, bufs):
    n = len(bufs)

    def body(*refs):
        outs = refs[n:2 * n]
        send_sems, recv_sems = refs[2 * n:]
        x, y, c = _place()
        sibling = (x, y, 1 - c)
        sends = []
        for i in range(n):
            rows = outs[i].shape[2]
            for j, (qx, qy) in enumerate(_other_chips(x, y)):
                block = _half(outs[i].at[2 * qx + qy], c, rows)
                fw = pltpu.make_async_remote_copy(
                    src_ref=block, dst_ref=block, send_sem=send_sems.at[3 * i + j], recv_sem=recv_sems.at[3 * i + j],
                    device_id=sibling, device_id_type=MESH)
                fw.start()
                sends.append(fw)
        for i in range(n):
            rows = outs[i].shape[2]
            for j, (qx, qy) in enumerate(_other_chips(x, y)):
                block = _half(outs[i].at[2 * qx + qy], 1 - c, rows)
                pltpu.make_async_remote_copy(
                    src_ref=block, dst_ref=block, send_sem=send_sems.at[3 * i + j], recv_sem=recv_sems.at[3 * i + j],
                    device_id=sibling, device_id_type=MESH).wait_recv()
        for fw in sends:
            fw.wait_send()

    return pl.pallas_call(
        body, name=name, in_specs=[_ANY] * n, out_specs=[_ANY] * n,
        out_shape=[jax.ShapeDtypeStruct(b.shape, b.dtype) for b in bufs],
        input_output_aliases={i: i for i in range(n)},
        scratch_shapes=[pltpu.SemaphoreType.DMA((3 * n,)), pltpu.SemaphoreType.DMA((3 * n,))],
        compiler_params=pltpu.CompilerParams(has_side_effects=True),
    )(*bufs)


def _between_chips_copies(parts, lands, send_sems, recv_sems):
    x, y, c = _place()
    copies = []
    for i, (part, land) in enumerate(zip(parts, lands)):
        for j, (qx, qy) in enumerate(_other_chips(x, y)):
            copies.append(pltpu.make_async_remote_copy(
                src_ref=part.at[2 * qx + qy], dst_ref=land.at[j], send_sem=send_sems.at[3 * i + j], recv_sem=recv_sems.at[3 * i + j],
                device_id=(qx, qy, c), device_id_type=MESH))
    return copies


def between_chips_start(name, parts):
    n = len(parts)
    lands = [lax.empty((N_CHIPS - 1,) + p.shape[1:], p.dtype) for p in parts]

    def body(*refs):
        send_sems, recv_sems = refs[2 * n], refs[2 * n + 1]
        token = refs[-1]
        for cp in _between_chips_copies(refs[:n], refs[n:2 * n], send_sems, recv_sems):
            cp.start()
        token[...] = jnp.zeros_like(token)

    res = pl.pallas_call(
        body, name=name,
        out_shape=(pltpu.SemaphoreType.DMA((3 * n,)), pltpu.SemaphoreType.DMA((3 * n,)),
                   *[pltpu.HBM(a.shape, a.dtype) for a in parts + lands], jax.ShapeDtypeStruct((8, 128), F32)),
        in_specs=[_HBM] * (2 * n), out_specs=(_SEM, _SEM, *[_HBM] * (2 * n), pl.BlockSpec(memory_space=pltpu.VMEM)),
        input_output_aliases={i: 2 + i for i in range(2 * n)},
        compiler_params=pltpu.CompilerParams(has_side_effects=_DATAFLOW),
    )(*[_in_hbm(a) for a in parts + lands])
    return res[0], res[1], list(res[2:2 + n]), list(res[2 + n:2 + 2 * n]), res[-1]


def between_chips_wait(name, send_sems, recv_sems, parts, lands, after):
    n = len(parts)

    def body(*refs):
        for cp in _between_chips_copies(refs[:n], refs[n:2 * n], refs[2 * n], refs[2 * n + 1]):
            cp.wait_send()
            cp.wait_recv()

    res = pl.pallas_call(
        body, name=name, out_shape=tuple(pltpu.HBM(a.shape, a.dtype) for a in parts + lands),
        in_specs=[_HBM] * (2 * n) + [_SEM, _SEM, _ANY], out_specs=tuple([_HBM] * (2 * n)),
        input_output_aliases={i: i for i in range(2 * n)},
        compiler_params=pltpu.CompilerParams(has_side_effects=_DATAFLOW),
    )(*parts, *lands, send_sems, recv_sems, after)
    return list(res[:n]), list(res[n:])


def exchange_start(name, arrays, n_copies, copies, after=()):
    n = len(arrays)

    def body(*refs):
        send_sems, recv_sems = refs[n + len(after)], refs[n + len(after) + 1]
        for cp in copies(refs[:n], send_sems, recv_sems):
            cp.start()
        refs[-1][...] = jnp.zeros_like(refs[-1])

    res = pl.pallas_call(
        body, name=name,
        out_shape=(pltpu.SemaphoreType.DMA((n_copies,)), pltpu.SemaphoreType.DMA((n_copies,)),
                   *[pltpu.HBM(a.shape, a.dtype) for a in arrays], jax.ShapeDtypeStruct((8, 128), F32)),
        in_specs=[_HBM] * n + [_ANY] * len(after),
        out_specs=(_SEM, _SEM, *[_HBM] * n, pl.BlockSpec(memory_space=pltpu.VMEM)),
        input_output_aliases={i: 2 + i for i in range(n)},
        compiler_params=pltpu.CompilerParams(has_side_effects=_DATAFLOW),
    )(*[_in_hbm(a) for a in arrays], *after)
    return res[0], res[1], list(res[2:2 + n]), res[-1]


def exchange_wait(name, send_sems, recv_sems, arrays, copies, after):
    n = len(arrays)

    def body(*refs):
        for cp in copies(refs[:n], refs[n], refs[n + 1]):
            cp.wait_send()
            cp.wait_recv()

    return list(pl.pallas_call(
        body, name=name, out_shape=tuple(pltpu.HBM(a.shape, a.dtype) for a in arrays),
        in_specs=[_HBM] * n + [_SEM, _SEM, _ANY], out_specs=tuple([_HBM] * n),
        input_output_aliases={i: i for i in range(n)},
        compiler_params=pltpu.CompilerParams(has_side_effects=_DATAFLOW),
    )(*arrays, send_sems, recv_sems, after))


def _to_sibling_copies(n):
    def copies(refs, send_sems, recv_sems):
        x, y, c = _place()
        out = []
        for i in range(n):
            rows = refs[i].shape[2]
            out.append(pltpu.make_async_remote_copy(
                src_ref=refs[i].at[:, :, pl.ds((1 - c) * (rows // 2), rows // 2), :], dst_ref=refs[n + i],
                send_sem=send_sems.at[i], recv_sem=recv_sems.at[i], device_id=(x, y, 1 - c), device_id_type=MESH))
        return out
    return copies


def _share_copies(n):
    def copies(refs, send_sems, recv_sems):
        x, y, c = _place()
        out = []
        for i in range(n):
            mine = _half(refs[i], c, refs[i].shape[1])
            out.append(pltpu.make_async_remote_copy(
                src_ref=mine, dst_ref=mine, send_sem=send_sems.at[i], recv_sem=recv_sems.at[i],
                device_id=(x, y, 1 - c), device_id_type=MESH))
        return out
    return copies


def send_half_to_sibling(name, grads):
    n = len(grads)

    def body(*refs):
        srcs, outs = refs[:n], refs[n:2 * n]
        send_sems, recv_sems = refs[2 * n:]
        x, y, c = _place()
        sibling = (x, y, 1 - c)
        copies = []
        for i in range(n):
            rows = srcs[i].shape[2]
            rc = pltpu.make_async_remote_copy(
                src_ref=srcs[i].at[:, :, pl.ds((1 - c) * (rows // 2), rows // 2), :], dst_ref=outs[i],
                send_sem=send_sems.at[i], recv_sem=recv_sems.at[i], device_id=sibling, device_id_type=MESH)
            rc.start()
            copies.append(rc)
        for rc in copies:
            rc.wait()

    def half_shape(g):
        s = g.shape
        return jax.ShapeDtypeStruct((s[0], s[1], s[2] // 2, s[3]), g.dtype)

    return pl.pallas_call(
        body, name=name, in_specs=[_ANY] * n, out_specs=[_ANY] * n, out_shape=[half_shape(g) for g in grads],
        scratch_shapes=[pltpu.SemaphoreType.DMA((n,)), pltpu.SemaphoreType.DMA((n,))],
        compiler_params=pltpu.CompilerParams(has_side_effects=True),
    )(*grads)


def exchange_between_chips(name, parts):
    n = len(parts)

    def body(*refs):
        srcs, outs = refs[:n], refs[n:2 * n]
        send_sems, recv_sems = refs[2 * n:]
        x, y, c = _place()
        chips = _other_chips(x, y)
        copies = []
        for i in range(n):
            for j, (qx, qy) in enumerate(chips):
                k = 3 * i + j
                rc = pltpu.make_async_remote_copy(
                    src_ref=srcs[i].at[2 * qx + qy], dst_ref=outs[i].at[j],
                    send_sem=send_sems.at[k], recv_sem=recv_sems.at[k], device_id=(qx, qy, c), device_id_type=MESH)
                rc.start()
                copies.append(rc)
        for rc in copies:
            rc.wait()

    return pl.pallas_call(
        body, name=name, in_specs=[_ANY] * n, out_specs=[_ANY] * n,
        out_shape=[jax.ShapeDtypeStruct((N_CHIPS - 1,) + p.shape[1:], p.dtype) for p in parts],
        scratch_shapes=[pltpu.SemaphoreType.DMA((3 * n,)), pltpu.SemaphoreType.DMA((3 * n,))],
        compiler_params=pltpu.CompilerParams(has_side_effects=True),
    )(*parts)


def share_halves(name, bufs):
    n = len(bufs)

    def body(*refs):
        outs = refs[n:2 * n]
        send_sems, recv_sems = refs[2 * n:]
        x, y, c = _place()
        copies = []
        for i in range(n):
            mine = _half(outs[i], c, outs[i].shape[1])
            rc = pltpu.make_async_remote_copy(
                src_ref=mine, dst_ref=mine, send_sem=send_sems.at[i], recv_sem=recv_sems.at[i],
                device_id=(x, y, 1 - c), device_id_type=MESH)
            rc.start()
            copies.append(rc)
        for i in range(n):
            theirs = _half(outs[i], 1 - c, outs[i].shape[1])
            pltpu.make_async_remote_copy(
                src_ref=theirs, dst_ref=theirs, send_sem=send_sems.at[i], recv_sem=recv_sems.at[i],
                device_id=(x, y, 1 - c), device_id_type=MESH).wait_recv()
        for rc in copies:
            rc.wait_send()

    return pl.pallas_call(
        body, name=name, in_specs=[_ANY] * n, out_specs=[_ANY] * n,
        out_shape=[jax.ShapeDtypeStruct(b.shape, b.dtype) for b in bufs],
        input_output_aliases={i: i for i in range(n)},
        scratch_shapes=[pltpu.SemaphoreType.DMA((n,)), pltpu.SemaphoreType.DMA((n,))],
        compiler_params=pltpu.CompilerParams(has_side_effects=True),
    )(*bufs)


def pair_sum(name, grad, recv, c):
    ns, na, rh, cols = recv.shape
    tr = _row_tile(rh, 256) if rh % 256 == 0 else rh
    nt = rh // tr

    def body(c_ref, g_ref, r_ref, o_ref):
        o_ref[...] = (g_ref[...].astype(F32) + r_ref[...].astype(F32)).astype(BF16)

    blk = (None, None, tr, cols)
    return pl.pallas_call(
        body, name=name,
        grid_spec=pltpu.PrefetchScalarGridSpec(
            num_scalar_prefetch=1, grid=(ns, na, nt),
            in_specs=[pl.BlockSpec(blk, lambda s, a, r, c_ref: (s, a, c_ref[0] * nt + r, 0)),
                      pl.BlockSpec(blk, lambda s, a, r, c_ref: (s, a, r, 0))],
            out_specs=pl.BlockSpec(blk, lambda s, a, r, c_ref: (s, a, r, 0))),
        out_shape=jax.ShapeDtypeStruct(recv.shape, BF16),
        compiler_params=_params(("parallel", "parallel", "parallel")),
    )(c, grad, recv)


def chip_sum(name, parts, landed, place):
    _, na, rh, cols = parts.shape
    tr = _row_tile(rh, 256) if rh % 256 == 0 else rh
    nt = rh // tr

    def body(place_ref, p_ref, l_ref, o_ref):
        total = p_ref[...].astype(F32)
        for s in range(N_CHIPS - 1):
            total = total + l_ref[s].astype(F32)
        o_ref[...] = total

    return pl.pallas_call(
        body, name=name,
        grid_spec=pltpu.PrefetchScalarGridSpec(
            num_scalar_prefetch=1, grid=(na, nt),
            in_specs=[pl.BlockSpec((None, None, tr, cols), lambda a, r, pr: (pr[1], a, r, 0)),
                      pl.BlockSpec((N_CHIPS - 1, None, tr, cols), lambda a, r, pr: (0, a, r, 0))],
            out_specs=pl.BlockSpec((None, tr, cols), lambda a, r, pr: (a, pr[0] * nt + r, 0))),
        out_shape=jax.ShapeDtypeStruct((na, 2 * rh, cols), F32),
        compiler_params=_params(("parallel", "parallel")),
    )(place, parts, landed)


def reduce_scatter_1(tag, grads):
    n = len(grads)
    lands = [lax.empty((g.shape[0], g.shape[1], g.shape[2] // 2, g.shape[3]), g.dtype) for g in grads]
    send_sems, recv_sems, arrays, token = exchange_start(f"{tag}_to_sibling_start", list(grads) + lands, n, _to_sibling_copies(n))
    return (send_sems, recv_sems, arrays), token


def reduce_scatter_2(tag, state, place, after):
    send_sems, recv_sems, arrays = state
    n = len(arrays) // 2
    arrays = exchange_wait(f"{tag}_to_sibling_wait", send_sems, recv_sems, arrays, _to_sibling_copies(n), after)
    parts = [pair_sum(f"{tag}_pair_sum{i}", g, r, place) for i, (g, r) in enumerate(zip(arrays[:n], arrays[n:]))]
    send_sems, recv_sems, parts, lands, token = between_chips_start(f"{tag}_between_chips_start", parts)
    return (send_sems, recv_sems, parts, lands), token


def reduce_scatter_3(tag, state, place, after):
    send_sems, recv_sems, parts, lands = state
    parts, landed = between_chips_wait(f"{tag}_between_chips_wait", send_sems, recv_sems, parts, lands, after)
    halves = [chip_sum(f"{tag}_chip_sum{i}", p, l, place) for i, (p, l) in enumerate(zip(parts, landed))]
    send_sems, recv_sems, halves, token = exchange_start(f"{tag}_share_start", halves, len(halves), _share_copies(len(halves)))
    return (send_sems, recv_sems, halves), token


def reduce_scatter_4(tag, state, after):
    send_sems, recv_sems, halves = state
    return exchange_wait(f"{tag}_share_wait", send_sems, recv_sems, halves, _share_copies(len(halves)), after)


def _adamw_math(w, g, m, v):
    m = ADAM_B1 * m + (1.0 - ADAM_B1) * g
    v = ADAM_B2 * v + (1.0 - ADAM_B2) * (g * g)
    m_hat = m / (1.0 - ADAM_B1 ** ADAM_STEP)
    v_hat = v / (1.0 - ADAM_B2 ** ADAM_STEP)
    delta = -ADAM_LR * (m_hat / (jnp.sqrt(v_hat) + ADAM_EPS) + ADAM_WD * w)
    return delta, m, v


def adamw(name, w, g, m, v, after=()):
    rows, cols = w.shape
    tr = _row_tile(rows, 256) if rows % 256 == 0 else rows // 2

    def body(w_ref, g_ref, m_ref, v_ref, d_ref, mo_ref, vo_ref):
        d_ref[...], mo_ref[...], vo_ref[...] = _adamw_math(w_ref[...], g_ref[...], m_ref[...], v_ref[...])

    blk = pl.BlockSpec((tr, cols), lambda i: (i, 0))
    return pl.pallas_call(
        _ordered_after(body, 4, len(after)), name=name, grid=(rows // tr,), in_specs=[blk] * 4 + [_ANY] * len(after),
        out_specs=[blk] * 3, out_shape=[jax.ShapeDtypeStruct(w.shape, F32)] * 3, compiler_params=_params(("parallel",)),
    )(w, g, m, v, *after)


def adamw_rows(name, w, g, m, v, after=()):
    rows, _, cols = w.shape
    tr = next(r for r in (128, 110, 64, 32, 16, 8, 1) if rows % r == 0)

    def body(w_ref, g_ref, m_ref, v_ref, d_ref, mo_ref, vo_ref):
        d_ref[...], mo_ref[...], vo_ref[...] = _adamw_math(w_ref[...], g_ref[...], m_ref[...], v_ref[...])

    blk = pl.BlockSpec((tr, 1, cols), lambda i: (i, 0, 0))
    return pl.pallas_call(
        _ordered_after(body, 4, len(after)), name=name, grid=(rows // tr,), in_specs=[blk] * 4 + [_ANY] * len(after),
        out_specs=[blk] * 3, out_shape=[jax.ShapeDtypeStruct(w.shape, F32)] * 3, compiler_params=_params(("parallel",)),
    )(w, g, m, v, *after)


def adamw_stacked(name, ws, g, ms, vs, after=()):
    n = len(ws)
    rows, cols = ws[0].shape
    tr = next(r for r in (128, 88, 64, 32, 16, 8) if rows % r == 0)

    def body(*refs):
        w_refs, m_refs, v_refs, g_ref = refs[:n], refs[n:2 * n], refs[2 * n:3 * n], refs[3 * n]
        outs = refs[3 * n + 1:]
        for i in range(n):
            outs[i][...], outs[n + i][...], outs[2 * n + i][...] = _adamw_math(
                w_refs[i][...], g_ref[i], m_refs[i][...], v_refs[i][...])

    blk = pl.BlockSpec((tr, cols), lambda r: (r, 0))
    res = pl.pallas_call(
        _ordered_after(body, 3 * n + 1, len(after)), name=name, grid=(rows // tr,),
        in_specs=[blk] * (3 * n) + [pl.BlockSpec((n, tr, cols), lambda r: (0, r, 0))] + [_ANY] * len(after),
        out_specs=[blk] * (3 * n),
        out_shape=[jax.ShapeDtypeStruct((rows, cols), F32)] * (3 * n), compiler_params=_params(("parallel",)),
    )(*ws, *ms, *vs, g, *after)
    return res[:n], res[n:2 * n], res[2 * n:]


def small_allreduce_adamw(name, g_part, w, m, v):
    rows, cols = g_part.shape

    def body(g_ref, w_ref, m_ref, v_ref, sum_ref, d_ref, mo_ref, vo_ref, land, send_sems, recv_sems):
        x, y, c = _place()
        me = 4 * x + 2 * y + c
        land[me] = g_ref[...]
        copies = []
        for r in range(1, 8):
            peer = (x ^ (r >> 2), y ^ ((r >> 1) & 1), c ^ (r & 1))
            rc = pltpu.make_async_remote_copy(
                src_ref=g_ref, dst_ref=land.at[me], send_sem=send_sems.at[r - 1], recv_sem=recv_sems.at[r - 1],
                device_id=peer, device_id_type=MESH)
            rc.start()
            copies.append(rc)
        for rc in copies:
            rc.wait()
        total = land[0]
        for s in range(1, 8):
            total = total + land[s]
        sum_ref[...] = total
        d_ref[...], mo_ref[...], vo_ref[...] = _adamw_math(w_ref[...], total, m_ref[...], v_ref[...])

    vmem = pl.BlockSpec(memory_space=pltpu.VMEM)
    return pl.pallas_call(
        body, name=name, in_specs=[vmem] * 4, out_specs=[vmem] * 4,
        out_shape=[jax.ShapeDtypeStruct((rows, cols), F32)] * 4,
        scratch_shapes=[pltpu.VMEM((8, rows, cols), F32), pltpu.SemaphoreType.DMA((7,)), pltpu.SemaphoreType.DMA((7,))],
        compiler_params=pltpu.CompilerParams(has_side_effects=True),
    )(g_part, w, m, v)


def _heads(a):
    t, w = a.shape
    return a.reshape(t, w // HEAD_DIM, HEAD_DIM).transpose(1, 0, 2)


def _unheads(a):
    n, t, _ = a.shape
    return a.transpose(1, 0, 2).reshape(t, n * HEAD_DIM)


def _cols_from_shards(g):
    ns, r, cols = g.shape
    return g.transpose(1, 0, 2).reshape(r, ns * cols)


def _shards_from_cols(a):
    r, cols = a.shape
    return a.reshape(r, N_CHIPS, cols // N_CHIPS).transpose(1, 0, 2)


def kernel(x, norm_ffn1, w_ffn1_gate, w_ffn1_up, w_ffn1_down, norm_mix, w_in, b_forget, w_gate, b_gate, w_up_a, w_up_b, w_out, norm_ffn2, w_ffn2_gate, w_ffn2_up, w_ffn2_down, norm_final, loss_target, m_norm_ffn1, m_w_ffn1_gate, m_w_ffn1_up, m_w_ffn1_down, m_norm_mix, m_w_in, m_b_forget, m_w_gate, m_b_gate, m_w_up_a, m_w_up_b, m_w_out, m_norm_ffn2, m_w_ffn2_gate, m_w_ffn2_up, m_w_ffn2_down, m_norm_final, v_norm_ffn1, v_w_ffn1_gate, v_w_ffn1_up, v_w_ffn1_down, v_norm_mix, v_w_in, v_b_forget, v_w_gate, v_b_gate, v_w_up_a, v_w_up_b, v_w_out, v_norm_ffn2, v_w_ffn2_gate, v_w_ffn2_up, v_w_ffn2_down, v_norm_final):
    t, d = x.shape[1], x.shape[2]
    in4 = w_in.shape[2]
    gate4 = w_gate.shape[2]
    up4 = w_up_a.shape[2]
    in_cols = N_CHIPS * in4
    n_forget = in_cols - QKV_COLS
    assert w_up_a.shape[1] == WIDTH and d == 2 * WIDTH and n_forget == N_HEADS
    nq = t // BQ
    chip = 2 * lax.axis_index("x") + lax.axis_index("y")
    c_arr = jnp.stack([lax.axis_index("c"), chip]).astype(jnp.int32)
    x2d = x[0]
    target = loss_target[0]

    def slot(shard):
        return lax.dynamic_update_slice(lax.empty((N_CHIPS,) + shard.shape, BF16), shard.astype(BF16)[None], (chip, 0, 0, 0))

    def ffn_views(wg, wu, wd):
        return [wg[0].T, wu[0].T, wd[0]]

    ffn1_w, ffn1_m, ffn1_v = (ffn_views(w_ffn1_gate, w_ffn1_up, w_ffn1_down), ffn_views(m_w_ffn1_gate, m_w_ffn1_up, m_w_ffn1_down),
                              ffn_views(v_w_ffn1_gate, v_w_ffn1_up, v_w_ffn1_down))
    ffn2_w, ffn2_m, ffn2_v = (ffn_views(w_ffn2_gate, w_ffn2_up, w_ffn2_down), ffn_views(m_w_ffn2_gate, m_w_ffn2_up, m_w_ffn2_down),
                              ffn_views(v_w_ffn2_gate, v_w_ffn2_up, v_w_ffn2_down))
    in_pad = -(-in4 // 32) * 32
    in_sh = slot(w_in[0].T[None])
    gt_sh = slot(w_gate[0].T[None])
    up_sh = slot(jnp.concatenate([w_up_a[0], w_up_b[0]], axis=0)[None])
    wo_sh = slot(w_out)
    f1_send, f1_recv, f1_bufs, f1_token = gather_ici_start("gather_ffn1_start", [slot(jnp.stack(ffn1_w))], norm_ffn1)
    mx_send, mx_recv, mx_bufs, mx_token = gather_ici_start("gather_mixer_start", [in_sh, gt_sh, up_sh, wo_sh], f1_token)
    f2_send, f2_recv, f2_bufs, f2_token = gather_ici_start("gather_ffn2_start", [slot(jnp.stack(ffn2_w))], mx_token)

    normed1 = rms_fwd("ffn1_rms", x2d, norm_ffn1, after=(f2_token,))
    (w3_1,) = gather_forward("gather_ffn1_forward", gather_ici_wait("gather_ffn1_wait", f1_send, f1_recv, f1_bufs, normed1[0]))
    x1, saved1 = ffn_forward("ffn1", x2d, norm_ffn1, w3_1, normed=normed1)
    w_in_g, w_gate_g, wup, wo = gather_forward(
        "gather_mixer_forward", gather_ici_wait("gather_mixer_wait", mx_send, mx_recv, mx_bufs, x1))
    wup = wup[:, 0]
    w_in_t = w_in_g.reshape(in_cols, d)
    w_gate_t = w_gate_g.reshape(2 * d, d)
    w_f_t = jnp.pad(w_in_t[QKV_COLS:], ((0, QB - n_forget), (0, 0)))
    wo_full = wo.reshape(d, d)
    b_forget_row = jnp.pad(b_forget, ((0, 0), (0, QB - n_forget)))

    h2, rstd2 = rms_fwd("mix_rms", x1, norm_mix)
    qkv = proj("mix_proj_qkv", h2, w_in_t, jnp.zeros((1, QKV_COLS), F32), 0, QKV_COLS, WIDTH, BF16, scaled_tiles=(0, 3))
    pc = proj("mix_proj_gates", h2, w_gate_t, b_gate, 0, 2 * d, WIDTH, F32)
    f_logit = proj("mix_proj_forget", h2, w_f_t, jnp.zeros((1, QB), F32), 0, QB, QB, F32)
    c_cum = fox_prep("fox_prep", f_logit, b_forget_row)
    c_heads = c_cum[:, :N_HEADS].T
    c_col = c_heads[:, :, None]
    c_row = c_heads.reshape(N_HEADS, t // CS, 1, CS)
    ya = sb_pair_fwd("sb_fwd", qkv)
    yb, lse = fox_pair_fwd("fox_fwd", qkv, c_col, c_row)
    ua, ub, mixed = mix_fwd("mix_fwd", ya, yb, wup, pc)
    x2 = mm_residual("mix_out", mixed[None], wo_full[None], pl.BlockSpec((1, d, d), lambda m: (0, 0, 0)), x1, 1.0)
    (w3_2,) = gather_forward("gather_ffn2_forward", gather_ici_wait("gather_ffn2_wait", f2_send, f2_recv, f2_bufs, x2))
    x3, saved2 = ffn_forward("ffn2", x2, norm_ffn2, w3_2)
    dx3, gn_final, loss_part = loss_head("loss_head", x3, norm_final[None], target)

    g_w3_2, dab2 = ffn_backward_weights("ffn2", dx3, saved2, w3_2)
    rs_ffn2, token = reduce_scatter_1("rs_ffn2", [g_w3_2])
    dx2, gn_ffn2 = ffn_backward_input("ffn2", dx3, saved2, dab2, norm_ffn2, w3_2, after=(token,))
    rs_ffn2, rs_ffn2_token = reduce_scatter_2("rs_ffn2", rs_ffn2, c_arr, dx2)

    dua, dub, dpa, dpb, gba, gbb = mix_bwd("mix_bwd", dx2, wo_full, pc, ua, ub, after=(rs_ffn2_token,))
    g_bgate = jnp.concatenate([gba, gbb], axis=1)
    g_wo = mm_plain("wgrad_out", mixed, dx2, TN, BF16, tk_target=1024)
    dya = up_bwd("dya", dua, wup, 0)
    dyb = up_bwd("dyb", dub, wup, 1)
    g_up = up_wgrad("wgrad_up", ya, yb, dua, dub)
    dqkv = sb_pair_bwd("sb_bwd", qkv, dya, lax.empty((t, QKV_COLS), BF16))
    dqkv, dcc, dcr = fox_pair_bwd("fox_bwd", qkv, yb, dyb, lse, c_col, c_row, dqkv)
    dc = dcc[:, :, :2].transpose(1, 0, 2).reshape(t, N_HEADS) + dcr.reshape(N_HEADS, t).T
    df, g_bf = fox_gate_bwd("fox_gate_bwd", jnp.pad(dc, ((0, 0), (0, QB - N_HEADS))), f_logit, b_forget_row)
    df = df.astype(BF16)
    g_qkv_t = wgrad_cat("wgrad_qkv", h2, dqkv)
    g_f_t = wgrad_cat("wgrad_forget", h2, df)
    g_gate_t = jnp.stack([wgrad_cat("wgrad_gate_a", h2, dpa), wgrad_cat("wgrad_gate_b", h2, dpb)])
    g_in_t = jnp.concatenate([g_qkv_t, g_f_t[:n_forget]], axis=0).reshape(N_CHIPS, in4, d)
    rs_mixer, token = reduce_scatter_1(
        "rs_mixer", [jnp.pad(g_in_t, ((0, 0), (0, in_pad - in4), (0, 0)))[:, None], g_gate_t.reshape(N_CHIPS, 1, gate4, d),
                     g_up[:, None], g_wo.reshape(N_CHIPS, 1, d // N_CHIPS, d)])
    dx1, gn_mix = mixer_dh("mix_dh", [(dqkv, w_in_t, 0), (dpa, w_gate_t, 0), (dpb, w_gate_t, 1), (df, w_f_t, 0)],
                           x1, rstd2, norm_mix, dx2, after=(token,))
    rs_mixer, rs_mixer_token = reduce_scatter_2("rs_mixer", rs_mixer, c_arr, dx1)
    rs_ffn2, rs_ffn2_token = reduce_scatter_3("rs_ffn2", rs_ffn2, c_arr, dx1)

    g_w3_1, dab1 = ffn_backward_weights("ffn1", dx1, saved1, w3_1, after=(rs_mixer_token, rs_ffn2_token))
    rs_ffn1, token = reduce_scatter_1("rs_ffn1", [g_w3_1])
    dx0, gn_ffn1 = ffn_backward_input("ffn1", dx1, saved1, dab1, norm_ffn1, w3_1, after=(token,))
    rs_ffn1, rs_ffn1_token = reduce_scatter_2("rs_ffn1", rs_ffn1, c_arr, dx0)
    rs_mixer, rs_mixer_token = reduce_scatter_3("rs_mixer", rs_mixer, c_arr, dx0)
    (s_w3_2,) = reduce_scatter_4("rs_ffn2", rs_ffn2, dx0)

    def pack_small(n1, nm, n2, nf, bg, bf, last):
        return jnp.concatenate([n1, nm, n2, nf, bg.reshape(2, d), jnp.pad(bf, ((0, 0), (0, d - n_forget))), last], axis=0)

    zero_row = jnp.zeros((1, d), F32)
    g_small = pack_small(gn_ffn1, gn_mix, gn_ffn2, gn_final, g_bgate, g_bf[:, :n_forget], jnp.pad(loss_part, ((0, 0), (0, d - 1))))
    w_small = pack_small(norm_ffn1, norm_mix, norm_ffn2, norm_final[None], b_gate, b_forget, zero_row)
    m_small = pack_small(m_norm_ffn1, m_norm_mix, m_norm_ffn2, m_norm_final[None], m_b_gate, m_b_forget, zero_row)
    v_small = pack_small(v_norm_ffn1, v_norm_mix, v_norm_ffn2, v_norm_final[None], v_b_gate, v_b_forget, zero_row)
    smalls = small_allreduce_adamw("small_allreduce_adamw", g_small, w_small, m_small, v_small)

    def unpack_small(p):
        return {"norm_ffn1": p[0:1], "norm_mix": p[1:2], "norm_ffn2": p[2:3], "norm_final": p[3], "b_gate": p[4:6].reshape(1, 2 * d),
                "b_forget": p[6:7, :n_forget]}

    loss = smalls[0][7, 0]
    small_out = [unpack_small(p) for p in smalls]

    big_out = {}

    def adamw_ffn(tag, s_w3, ws, ms, vs, after):
        deltas, new_ms, new_vs = adamw_stacked(f"adamw_{tag}", ws, s_w3, ms, vs, after)
        for which, part in ((GATE, "gate"), (UP, "up"), (DOWN, "down")):
            back = (lambda a: a[None]) if which == DOWN else (lambda a: a.T[None])
            big_out[f"w_{tag}_{part}"] = tuple(back(a) for a in (s_w3[which], deltas[which], new_ms[which], new_vs[which]))
        return deltas[DOWN]

    last = adamw_ffn("ffn2", s_w3_2, ffn2_w, ffn2_m, ffn2_v, (rs_ffn1_token, rs_mixer_token))
    s_in, s_gt, s_up, s_wo = reduce_scatter_4("rs_mixer", rs_mixer, last)
    grads = {"w_gate": s_gt[0].T, "w_up_a": s_up[0, :WIDTH], "w_up_b": s_up[0, WIDTH:], "w_out": s_wo[0]}
    weights = {"w_gate": (w_gate, m_w_gate, v_w_gate), "w_up_a": (w_up_a, m_w_up_a, v_w_up_a),
               "w_up_b": (w_up_b, m_w_up_b, v_w_up_b), "w_out": (w_out, m_w_out, v_w_out)}
    for wname, (w, m, v) in weights.items():
        g = grads[wname]
        delta, new_m, new_v = adamw(f"adamw_{wname}", w[0], g, m[0], v[0])
        big_out[wname] = (g[None], delta[None], new_m[None], new_v[None])
    rows_of = lambda a: jnp.transpose(a, (2, 0, 1))
    g_in_rows = s_in[0, :in4][:, None, :]
    in_rows = adamw_rows("adamw_w_in", rows_of(w_in), g_in_rows, rows_of(m_w_in), rows_of(v_w_in))
    big_out["w_in"] = tuple(jnp.transpose(a, (1, 2, 0)) for a in (g_in_rows, *in_rows))

    rs_ffn1, token = reduce_scatter_3("rs_ffn1", rs_ffn1, c_arr, in_rows[0])
    (s_w3_1,) = reduce_scatter_4("rs_ffn1", rs_ffn1, token)
    adamw_ffn("ffn1", s_w3_1, ffn1_w, ffn1_m, ffn1_v, ())

    order = ["norm_ffn1", "w_ffn1_gate", "w_ffn1_up", "w_ffn1_down", "norm_mix", "w_in", "b_forget", "w_gate", "b_gate",
             "w_up_a", "w_up_b", "w_out", "norm_ffn2", "w_ffn2_gate", "w_ffn2_up", "w_ffn2_down", "norm_final"]
    outs = [loss, dx0[None]]
    for kind in range(4):
        for wname in order:
            outs.append(big_out[wname][kind] if wname in big_out else small_out[kind][wname])
    return tuple(outs)
```

```python
import functools

import jax
import jax.numpy as jnp
from jax import lax
from jax.experimental import pallas as pl
from jax.experimental.pallas import tpu as pltpu

F32 = jnp.float32
BF16 = jnp.bfloat16

HEAD_DIM = 64
N_HEADS = 8
WIDTH = N_HEADS * HEAD_DIM
QKV_COLS = 6 * WIDTH
RMS_EPS = 1e-6
ATTN_SCALE = HEAD_DIM ** -0.5
N_CHIPS = 4
QB = 128
BQ = 1024
CS = 256
N_SUB = BQ // CS
NEG_BIG = -1e30

ADAM_LR = 0.001
ADAM_B1 = 0.9
ADAM_B2 = 0.999
ADAM_EPS = 1e-08
ADAM_WD = 0.01
ADAM_STEP = 10

VMEM_LIMIT_BYTES = 48 * 1024 * 1024
MESH = pl.DeviceIdType.MESH

NN = ((1,), (0,))
NT = ((1,), (1,))
TN = ((0,), (0,))


def _params(semantics):
    return pltpu.CompilerParams(dimension_semantics=semantics, vmem_limit_bytes=VMEM_LIMIT_BYTES)


def _dot(a, b, contract):
    return lax.dot_general(a.astype(BF16), b.astype(BF16), (contract, ((), ())), preferred_element_type=F32)


def _sigmoid(x):
    return 1.0 / (1.0 + jnp.exp(-x))


def _log1pexp_neg_abs(z):
    return jnp.log(1.0 + jnp.exp(-jnp.abs(z)))


def _split3(x):
    hi = x.astype(BF16)
    r1 = x - hi.astype(F32)
    mid = r1.astype(BF16)
    lo = (r1 - mid.astype(F32)).astype(BF16)
    return hi, mid, lo


def _dot_exact_lhs01(m01, x):
    hi, mid, lo = _split3(x)
    d = lambda p: lax.dot_general(m01, p, (NN, ((), ())), preferred_element_type=F32)
    return d(hi) + d(mid) + d(lo)


def _iota2(shape, dim):
    return lax.broadcasted_iota(jnp.int32, shape, dim)


def _mm(name, pairs, contract, grid, pair_specs, out_shape, out_specs, acc_shape, nk, epilogue,
        extras=(), extra_specs=(), semantics=None):
    n_pairs = len(pairs)
    n_extra = len(extras)
    n_out = len(out_shape)

    def body(*refs):
        ab = refs[:2 * n_pairs]
        ex = refs[2 * n_pairs:2 * n_pairs + n_extra]
        outs = refs[2 * n_pairs + n_extra:2 * n_pairs + n_extra + n_out]
        ids = [pl.program_id(i) for i in range(len(grid))]
        k = ids[-1]
        part = _dot(ab[0][...], ab[1][...], contract)
        for p in range(1, n_pairs):
            part += _dot(ab[2 * p][...], ab[2 * p + 1][...], contract)
        if nk == 1:
            epilogue(part, ex, outs, ids)
            return
        acc = refs[-1]

        @pl.when(k == 0)
        def _():
            acc[...] = part

        @pl.when(k != 0)
        def _():
            acc[...] += part

        @pl.when(k == nk - 1)
        def _():
            epilogue(acc[...], ex, outs, ids)

    operands = [t for pair in pairs for t in pair] + list(extras)
    in_specs = [s for pair in pair_specs for s in pair] + list(extra_specs)
    if semantics is None:
        semantics = ("parallel",) * (len(grid) - 1) + ("arbitrary",)
    return pl.pallas_call(
        body, name=name, grid=grid, in_specs=in_specs, out_specs=list(out_specs), out_shape=list(out_shape),
        scratch_shapes=[] if nk == 1 else [pltpu.VMEM(acc_shape, F32)], compiler_params=_params(semantics),
    )(*operands)


def _ordered_after(body, n_in, n_after):
    def wrapped(*refs):
        return body(*refs[:n_in], *refs[n_in + n_after:])
    return wrapped


def _row_tile(rows, target):
    t = min(rows, target)
    while rows % t:
        t //= 2
    return t


def rms_fwd(name, x, g, after=()):
    t, d = x.shape
    tr = _row_tile(t, 256)

    def body(x_ref, g_ref, h_ref, r_ref):
        xv = x_ref[...]
        r = lax.rsqrt(jnp.mean(xv * xv, axis=-1, keepdims=True) + RMS_EPS)
        h_ref[...] = (xv * r * g_ref[...]).astype(BF16)
        r_ref[...] = r

    return pl.pallas_call(
        _ordered_after(body, 2, len(after)), name=name, grid=(t // tr,),
        in_specs=[pl.BlockSpec((tr, d), lambda i: (i, 0)), pl.BlockSpec((1, d), lambda i: (0, 0))] + [_ANY] * len(after),
        out_specs=[pl.BlockSpec((tr, d), lambda i: (i, 0)), pl.BlockSpec((tr, 1), lambda i: (i, 0))],
        out_shape=[jax.ShapeDtypeStruct((t, d), BF16), jax.ShapeDtypeStruct((t, 1), F32)],
        compiler_params=_params(("parallel",)),
    )(x, g, *after)


GATE, UP, DOWN = 0, 1, 2


def _ffn_w_spec(which, f4, d, index_of_j):
    return pl.BlockSpec((None, None, f4, d), lambda *ids: (index_of_j(*ids), which, 0, 0))


def ffn_up(name, h, w3):
    t, d = h.shape
    ns, _, f4, _ = w3.shape
    tm = _row_tile(t, 512)

    def body(h_ref, wg_ref, wu_ref, a_ref, b_ref, s_ref):
        hv = h_ref[...]
        a = _dot(hv, wg_ref[...], NT)
        b = _dot(hv, wu_ref[...], NT)
        a_ref[...] = a.astype(BF16)
        b_ref[...] = b.astype(BF16)
        s_ref[...] = (a * _sigmoid(a) * b).astype(BF16)

    act_spec = pl.BlockSpec((None, tm, f4), lambda j, m: (j, m, 0))
    return pl.pallas_call(
        body, name=name, grid=(ns, t // tm),
        in_specs=[pl.BlockSpec((tm, d), lambda j, m: (m, 0)),
                  _ffn_w_spec(GATE, f4, d, lambda j, m: j), _ffn_w_spec(UP, f4, d, lambda j, m: j)],
        out_specs=[act_spec, act_spec, act_spec],
        out_shape=[jax.ShapeDtypeStruct((ns, t, f4), BF16)] * 3,
        compiler_params=_params(("parallel", "parallel")),
    )(h, w3, w3)


def mm_residual(name, s, w, w_spec, x, scale):
    nj, t, kdim = s.shape
    n = x.shape[1]
    tm = _row_tile(t, 512)

    def body(s_ref, w_ref, x_ref, o_ref):
        acc = _dot(s_ref[0], w_ref[0], NN)
        for j in range(1, nj):
            acc += _dot(s_ref[j], w_ref[j], NN)
        o_ref[...] = x_ref[...] + scale * acc

    row = pl.BlockSpec((tm, n), lambda m: (m, 0))
    return pl.pallas_call(
        body, name=name, grid=(t // tm,),
        in_specs=[pl.BlockSpec((nj, tm, kdim), lambda m: (0, m, 0)), w_spec, row], out_specs=row,
        out_shape=jax.ShapeDtypeStruct((t, n), F32), compiler_params=_params(("parallel",)),
    )(s, w, x)


def ffn_bwd_act(name, dx, w3, a, b, after=()):
    t, d = dx.shape
    ns, _, f4, _ = w3.shape
    tm = _row_tile(t, 512)

    def body(dx_ref, wd_ref, a_ref, b_ref, da_ref, db_ref):
        ds = _dot(0.5 * dx_ref[...], wd_ref[...], NT)
        av = a_ref[...].astype(F32)
        sig = _sigmoid(av)
        da_ref[...] = (ds * b_ref[...].astype(F32) * (sig * (1.0 + av * (1.0 - sig)))).astype(BF16)
        db_ref[...] = (ds * (av * sig)).astype(BF16)

    act_spec = pl.BlockSpec((None, tm, f4), lambda j, m: (j, m, 0))
    return pl.pallas_call(
        _ordered_after(body, 4, len(after)), name=name, grid=(ns, t // tm),
        in_specs=[pl.BlockSpec((tm, d), lambda j, m: (m, 0)), _ffn_w_spec(DOWN, f4, d, lambda j, m: j), act_spec, act_spec]
        + [_ANY] * len(after),
        out_specs=[act_spec, act_spec],
        out_shape=[jax.ShapeDtypeStruct((ns, t, f4), BF16)] * 2,
        compiler_params=_params(("parallel", "parallel")),
    )(dx, w3, a, b, *after)


def ffn_wgrad(name, h, da, db, s, dx):
    t, d = h.shape
    ns, _, f4 = da.shape
    tk = _row_tile(t, 1024)
    nk = t // tk

    def body(h_ref, da_ref, db_ref, s_ref, dx_ref, o_ref, acc):
        k = pl.program_id(1)

        @pl.when(k == 0)
        def _():
            acc[...] = jnp.zeros_like(acc)

        hv = h_ref[...]
        acc[GATE] += _dot(da_ref[...], hv, TN)
        acc[UP] += _dot(db_ref[...], hv, TN)
        acc[DOWN] += _dot(s_ref[...], 0.5 * dx_ref[...], TN)

        @pl.when(k == nk - 1)
        def _():
            o_ref[...] = acc[...].astype(BF16)

    act_spec = pl.BlockSpec((None, tk, f4), lambda j, k: (j, k, 0))
    row_spec = pl.BlockSpec((tk, d), lambda j, k: (k, 0))
    return pl.pallas_call(
        body, name=name, grid=(ns, nk),
        in_specs=[row_spec, act_spec, act_spec, act_spec, row_spec],
        out_specs=pl.BlockSpec((None, 3, f4, d), lambda j, k: (j, 0, 0, 0)),
        out_shape=jax.ShapeDtypeStruct((ns, 3, f4, d), BF16),
        scratch_shapes=[pltpu.VMEM((3, f4, d), F32)],
        compiler_params=_params(("parallel", "arbitrary")),
    )(h, da, db, s, dx)


def _rms_bwd_tail(dh, x_ref, r_ref, g_ref, dxin_ref, dx_ref, gn_ref, row_tile_index):
    r = r_ref[...]
    xhat = x_ref[...] * r
    dhg = dh * g_ref[...]
    dx_ref[...] = dxin_ref[...] + r * (dhg - xhat * jnp.mean(dhg * xhat, axis=-1, keepdims=True))
    part = jnp.sum(dh * xhat, axis=0, keepdims=True)

    @pl.when(row_tile_index == 0)
    def _():
        gn_ref[...] = part

    @pl.when(row_tile_index != 0)
    def _():
        gn_ref[...] += part


def ffn_dh(name, da, db, w3, x, rstd, g, dx_in, after=()):
    ns, t, f4 = da.shape
    d = x.shape[1]
    tm = _row_tile(t, 256)

    def body(da_ref, db_ref, wg_ref, wu_ref, x_ref, r_ref, g_ref, dxin_ref, dx_ref, gn_ref):
        dh = _dot(da_ref[0], wg_ref[0], NN) + _dot(db_ref[0], wu_ref[0], NN)
        for j in range(1, ns):
            dh += _dot(da_ref[j], wg_ref[j], NN) + _dot(db_ref[j], wu_ref[j], NN)
        _rms_bwd_tail(dh, x_ref, r_ref, g_ref, dxin_ref, dx_ref, gn_ref, pl.program_id(0))

    act = pl.BlockSpec((ns, tm, f4), lambda m: (0, m, 0))
    row = pl.BlockSpec((tm, d), lambda m: (m, 0))
    gain = pl.BlockSpec((1, d), lambda m: (0, 0))
    return pl.pallas_call(
        _ordered_after(body, 8, len(after)), name=name, grid=(t // tm,),
        in_specs=[act, act, pl.BlockSpec((ns, None, f4, d), lambda m: (0, GATE, 0, 0)),
                  pl.BlockSpec((ns, None, f4, d), lambda m: (0, UP, 0, 0)), row, pl.BlockSpec((tm, 1), lambda m: (m, 0)), gain, row]
        + [_ANY] * len(after),
        out_specs=[row, gain], out_shape=[jax.ShapeDtypeStruct((t, d), F32), jax.ShapeDtypeStruct((1, d), F32)],
        compiler_params=_params(("arbitrary",)),
    )(da, db, w3, w3, x, rstd, g, dx_in, *after)


def mixer_dh(name, parts, x, rstd, g, dx_in, after=()):
    t, d = x.shape
    n = len(parts)
    tm = _row_tile(t, 256)

    def body(*refs):
        dh = _dot(refs[0][...], refs[n][...], NN)
        for i in range(1, n):
            dh += _dot(refs[i][...], refs[n + i][...], NN)
        _rms_bwd_tail(dh, *refs[2 * n:2 * n + 6], pl.program_id(0))

    row = pl.BlockSpec((tm, d), lambda m: (m, 0))
    gain = pl.BlockSpec((1, d), lambda m: (0, 0))
    act_specs = [pl.BlockSpec((tm, a.shape[1]), lambda m: (m, 0)) for a, _, _ in parts]
    w_specs = [pl.BlockSpec((a.shape[1], d), functools.partial(lambda m, blk: (blk, 0), blk=blk)) for a, _, blk in parts]
    return pl.pallas_call(
        _ordered_after(body, 2 * n + 4, len(after)), name=name, grid=(t // tm,),
        in_specs=act_specs + w_specs + [row, pl.BlockSpec((tm, 1), lambda m: (m, 0)), gain, row] + [_ANY] * len(after),
        out_specs=[row, gain], out_shape=[jax.ShapeDtypeStruct((t, d), F32), jax.ShapeDtypeStruct((1, d), F32)],
        compiler_params=_params(("arbitrary",)),
    )(*[a for a, _, _ in parts], *[w for _, w, _ in parts], x, rstd, g, dx_in, *after)


def ffn_forward(tag, x, g_norm, w3, normed=None):
    _, _, f4, d = w3.shape
    h, rstd = normed if normed is not None else rms_fwd(f"{tag}_rms", x, g_norm)
    a, b, s = ffn_up(f"{tag}_up", h, w3)
    x_out = mm_residual(f"{tag}_down", s, w3, pl.BlockSpec((w3.shape[0], None, f4, d), lambda m: (0, DOWN, 0, 0)), x, 0.5)
    return x_out, (x, h, rstd, a, b, s)


def ffn_backward_weights(tag, dx, saved, w3, after=()):
    x, h, rstd, a, b, s = saved
    da, db = ffn_bwd_act(f"{tag}_bwd_act", dx, w3, a, b, after)
    return ffn_wgrad(f"{tag}_wgrad", h, da, db, s, dx), (da, db)


def ffn_backward_input(tag, dx, saved, dab, g_norm, w3, after=()):
    x, h, rstd, a, b, s = saved
    return ffn_dh(f"{tag}_dh", dab[0], dab[1], w3, x, rstd, g_norm, dx, after)


def proj(name, h, wcat_t, bias, first_col, n_cols, tn, out_dtype, scaled_tiles=()):
    t, d = h.shape
    tm = _row_tile(t, 512)
    off = first_col // tn

    def epilogue(acc, ex, outs, ids):
        val = acc + ex[0][...]
        if scaled_tiles:
            hit = functools.reduce(jnp.logical_or, [ids[0] == s for s in scaled_tiles])
            val = val * jnp.where(hit, ATTN_SCALE, 1.0)
        outs[0][...] = val.astype(out_dtype)

    return _mm(
        name, [(h, wcat_t)], NT, (n_cols // tn, t // tm, 1),
        [(pl.BlockSpec((tm, d), lambda j, m, k: (m, 0)), pl.BlockSpec((tn, d), lambda j, m, k: (off + j, 0)))],
        [jax.ShapeDtypeStruct((t, n_cols), out_dtype)], [pl.BlockSpec((tm, tn), lambda j, m, k: (m, j))], (tm, tn), 1, epilogue,
        extras=[bias], extra_specs=[pl.BlockSpec((1, tn), lambda j, m, k: (0, off + j))],
    )[0]


def mix_fwd(name, ya, yb, wup, pc):
    t, w = ya.shape
    ns, _, tn = wup.shape
    d = ns * tn
    tm = _row_tile(t, 512)

    def body(ya_ref, yb_ref, wa_ref, wb_ref, pa_ref, pb_ref, ua_ref, ub_ref, mx_ref):
        ua = _dot(ya_ref[...], wa_ref[...], NN)
        ub = _dot(yb_ref[...], wb_ref[...], NN)
        ua_ref[...] = ua
        ub_ref[...] = ub
        mx_ref[...] = (_sigmoid(pa_ref[...]) * ua + _sigmoid(pb_ref[...]) * ub).astype(BF16)

    y_spec = pl.BlockSpec((tm, w), lambda m, n: (m, 0))
    o_spec = pl.BlockSpec((tm, tn), lambda m, n: (m, n))
    return pl.pallas_call(
        body, name=name, grid=(t // tm, ns),
        in_specs=[y_spec, y_spec, pl.BlockSpec((None, w, tn), lambda m, n: (n, 0, 0)), pl.BlockSpec((None, w, tn), lambda m, n: (n, 1, 0)),
                  o_spec, pl.BlockSpec((tm, tn), lambda m, n: (m, ns + n))],
        out_specs=[o_spec, o_spec, o_spec],
        out_shape=[jax.ShapeDtypeStruct((t, d), F32), jax.ShapeDtypeStruct((t, d), F32), jax.ShapeDtypeStruct((t, d), BF16)],
        compiler_params=_params(("parallel", "parallel")),
    )(ya, yb, wup, wup, pc, pc)


def up_bwd(name, du, wup, branch):
    t, d = du.shape
    ns, w2, tn = wup.shape
    w = w2 // 2
    tm = _row_tile(t, 512)

    def body(du_ref, w_ref, o_ref):
        acc = _dot(du_ref[:, 0:tn], w_ref[0], NT)
        for j in range(1, ns):
            acc += _dot(du_ref[:, j * tn:(j + 1) * tn], w_ref[j], NT)
        o_ref[...] = acc.astype(BF16)

    return pl.pallas_call(
        body, name=name, grid=(t // tm,),
        in_specs=[pl.BlockSpec((tm, d), lambda m: (m, 0)), pl.BlockSpec((ns, w, tn), lambda m: (0, branch, 0))],
        out_specs=pl.BlockSpec((tm, w), lambda m: (m, 0)), out_shape=jax.ShapeDtypeStruct((t, w), BF16),
        compiler_params=_params(("parallel",)),
    )(du, wup)


def up_wgrad(name, ya, yb, dua, dub):
    t, w = ya.shape
    d = dua.shape[1]
    tn = d // N_CHIPS

    def body(ya_ref, yb_ref, dua_ref, dub_ref, o_ref):
        o_ref[0:w, :] = _dot(ya_ref[...], dua_ref[...], TN).astype(BF16)
        o_ref[w:2 * w, :] = _dot(yb_ref[...], dub_ref[...], TN).astype(BF16)

    y_spec = pl.BlockSpec((t, w), lambda j: (0, 0))
    du_spec = pl.BlockSpec((t, tn), lambda j: (0, j))
    return pl.pallas_call(
        body, name=name, grid=(N_CHIPS,), in_specs=[y_spec, y_spec, du_spec, du_spec],
        out_specs=pl.BlockSpec((None, 2 * w, tn), lambda j: (j, 0, 0)),
        out_shape=jax.ShapeDtypeStruct((N_CHIPS, 2 * w, tn), BF16), compiler_params=_params(("parallel",)),
    )(ya, yb, dua, dub)


def mix_bwd(name, dx, wo, pc, ua, ub, after=()):
    t, d = dx.shape
    tm = _row_tile(t, 512)
    tn = 512
    off_a = 0
    off_b = d // tn

    def body(dx_ref, wo_ref, pa_ref, pb_ref, ua_ref, ub_ref, dua_ref, dub_ref, dpa_ref, dpb_ref, ba_ref, bb_ref):
        dm = _dot(dx_ref[...], wo_ref[...], NT)
        ga = _sigmoid(pa_ref[...])
        gb = _sigmoid(pb_ref[...])
        dua_ref[...] = (dm * ga).astype(BF16)
        dub_ref[...] = (dm * gb).astype(BF16)
        dpa = dm * ua_ref[...] * ga * (1.0 - ga)
        dpb = dm * ub_ref[...] * gb * (1.0 - gb)
        dpa_ref[...] = dpa.astype(BF16)
        dpb_ref[...] = dpb.astype(BF16)
        sa = jnp.sum(dpa, axis=0, keepdims=True)
        sb = jnp.sum(dpb, axis=0, keepdims=True)

        @pl.when(pl.program_id(1) == 0)
        def _():
            ba_ref[...] = sa
            bb_ref[...] = sb

        @pl.when(pl.program_id(1) != 0)
        def _():
            ba_ref[...] += sa
            bb_ref[...] += sb

    tile = pl.BlockSpec((tm, tn), lambda n, m: (m, n))
    bias = pl.BlockSpec((1, tn), lambda n, m: (0, n))
    return pl.pallas_call(
        _ordered_after(body, 6, len(after)), name=name, grid=(d // tn, t // tm),
        in_specs=[pl.BlockSpec((tm, d), lambda n, m: (m, 0)), pl.BlockSpec((tn, d), lambda n, m: (n, 0)),
                  pl.BlockSpec((tm, tn), lambda n, m: (m, off_a + n)), pl.BlockSpec((tm, tn), lambda n, m: (m, off_b + n)),
                  tile, tile] + [_ANY] * len(after),
        out_specs=[tile, tile, tile, tile, bias, bias],
        out_shape=[jax.ShapeDtypeStruct((t, d), BF16)] * 4 + [jax.ShapeDtypeStruct((1, d), F32)] * 2,
        compiler_params=_params(("parallel", "arbitrary")),
    )(dx, wo, pc, pc, ua, ub, *after)


def mm_plain(name, a, b, contract, out_dtype, tk_target=512):
    if contract == NN:
        m, kdim = a.shape
        n = b.shape[1]
    elif contract == NT:
        m, kdim = a.shape
        n = b.shape[0]
    else:
        kdim, m = a.shape
        n = b.shape[1]
    tm = _row_tile(m, 512)
    tk = _row_tile(kdim, tk_target)
    nk = kdim // tk
    if contract == TN:
        a_spec = pl.BlockSpec((tk, tm), lambda i, k: (k, i))
    else:
        a_spec = pl.BlockSpec((tm, tk), lambda i, k: (i, k))
    if contract == NT:
        b_spec = pl.BlockSpec((n, tk), lambda i, k: (0, k))
    else:
        b_spec = pl.BlockSpec((tk, n), lambda i, k: (k, 0))

    def epilogue(acc, ex, outs, ids):
        outs[0][...] = acc.astype(out_dtype)

    return _mm(name, [(a, b)], contract, (m // tm, nk), [(a_spec, b_spec)],
               [jax.ShapeDtypeStruct((m, n), out_dtype)], [pl.BlockSpec((tm, n), lambda i, k: (i, 0))], (tm, n), nk, epilogue)[0]


def wgrad_cat(name, h, dcat):
    t, d = h.shape
    n = dcat.shape[1]
    tn = next(c for c in (768, 512, 256, 128) if n % c == 0)
    tk = _row_tile(t, 2048)

    def epilogue(acc, ex, outs, ids):
        outs[0][...] = acc.astype(BF16)

    return _mm(
        name, [(dcat, h)], TN, (n // tn, t // tk),
        [(pl.BlockSpec((tk, tn), lambda j, k: (k, j)), pl.BlockSpec((tk, d), lambda j, k: (k, 0)))],
        [jax.ShapeDtypeStruct((n, d), BF16)], [pl.BlockSpec((tn, d), lambda j, k: (j, 0))], (tn, d), t // tk, epilogue,
    )[0]


def fox_prep(name, f, bias):
    t, lanes = f.shape
    nchunk = t // QB

    def body(f_ref, b_ref, c_ref):
        lower = (_iota2((QB, QB), 1) <= _iota2((QB, QB), 0)).astype(BF16)

        def chunk(n, carry):
            rows = pl.ds(pl.multiple_of(n * QB, QB), QB)
            u = f_ref[rows, :] + b_ref[...]
            lf = jnp.minimum(u, 0.0) - _log1pexp_neg_abs(u)
            c = _dot_exact_lhs01(lower, lf) + carry
            c_ref[rows, :] = c
            return c[QB - 1:QB, :]

        lax.fori_loop(0, nchunk, chunk, jnp.zeros((1, lanes), F32))

    return pl.pallas_call(body, name=name, out_shape=jax.ShapeDtypeStruct((t, lanes), F32),
                          compiler_params=pltpu.CompilerParams(vmem_limit_bytes=VMEM_LIMIT_BYTES))(f, bias)


def fox_gate_bwd(name, dc, f, bias):
    t, lanes = dc.shape
    nchunk = t // QB

    def body(dc_ref, f_ref, b_ref, df_ref, gb_ref):
        upper = (_iota2((QB, QB), 1) >= _iota2((QB, QB), 0)).astype(BF16)

        def chunk(n, carry):
            tail, total = carry
            rows = pl.ds(pl.multiple_of((nchunk - 1 - n) * QB, QB), QB)
            dlf = _dot_exact_lhs01(upper, dc_ref[rows, :]) + tail
            u = f_ref[rows, :] + b_ref[...]
            df = dlf * jnp.exp(jnp.minimum(-u, 0.0) - _log1pexp_neg_abs(u))
            df_ref[rows, :] = df
            return dlf[0:1, :], total + jnp.sum(df, axis=0, keepdims=True)

        zero = jnp.zeros((1, lanes), F32)
        _, total = lax.fori_loop(0, nchunk, chunk, (zero, zero))
        gb_ref[...] = total

    return pl.pallas_call(body, name=name,
                          out_shape=[jax.ShapeDtypeStruct((t, lanes), F32), jax.ShapeDtypeStruct((1, lanes), F32)],
                          compiler_params=pltpu.CompilerParams(vmem_limit_bytes=VMEM_LIMIT_BYTES))(dc, f, bias)


def _qrows(i):
    return pl.ds(pl.multiple_of(i * BQ, BQ), BQ)


def _krows(kc):
    return pl.ds(pl.multiple_of(kc * CS, CS), CS)


def _dot_split2_rhs01(x, m01):
    hi = x.astype(BF16)
    lo = (x - hi.astype(F32)).astype(BF16)
    d = lambda p: lax.dot_general(p, m01, (NN, ((), ())), preferred_element_type=F32)
    return d(hi) + d(lo)


def _live_rows(dchunk):
    return 0 if dchunk is None else dchunk * CS


def _diag_mask(dchunk, inclusive):
    shape = (BQ - _live_rows(dchunk), CS)
    return _iota2(shape, 1) <= _iota2(shape, 0) if inclusive else _iota2(shape, 1) < _iota2(shape, 0)


def _tail(x, r0, axis=0):
    return x if r0 == 0 else (x[r0:] if axis == 0 else x[:, r0:])


def _with_tail(old, tail, r0):
    return tail if r0 == 0 else jnp.concatenate([old[:r0], tail], axis=0)


def _walk_chunks(i, step, init, right_to_left):
    order = list(reversed(range(N_SUB))) if right_to_left else list(range(N_SUB))

    def diagonal(state):
        for dchunk in order:
            state = step(i * N_SUB + dchunk, state, dchunk)
        return state

    def group(n, state):
        base = ((i - 1 - n) if right_to_left else n) * N_SUB
        for dchunk in order:
            state = step(base + dchunk, state, None)
        return state

    if right_to_left:
        return lax.fori_loop(0, i, group, diagonal(init))
    return diagonal(lax.fori_loop(0, i, group, init))


PAIR = 2 * HEAD_DIM
N_PAIRS = N_HEADS // 2


def _pair_spec(t, first_block):
    return pl.BlockSpec((t, PAIR), lambda p, *_: (0, first_block + p))


def _head_lanes(shape):
    lane = _iota2(shape, len(shape) - 1)
    return [lane < HEAD_DIM, lane >= HEAD_DIM]


def _only_head(x, lanes_of_head):
    return jnp.where(lanes_of_head, x, jnp.zeros_like(x))


LOG2_E = 1.4426950408889634


def _sb_chunk_weights(q_h, k, later, carry, dchunk):
    z = _dot(q_h, k, NT) * LOG2_E
    lnb = -jnp.maximum(z, 0.0) - jnp.log2(1.0 + jnp.exp2(-jnp.abs(z)))
    lsz = lnb + z
    if dchunk is not None:
        lnb = jnp.where(_diag_mask(dchunk, False), lnb, 0.0)
    w = jnp.exp2(lsz + _dot_split2_rhs01(lnb, later) + carry)
    if dchunk is not None:
        w = jnp.where(_diag_mask(dchunk, False), w, 0.0)
    return w, lsz, lnb


def sb_pair_fwd(name, qkv):
    t = qkv.shape[0]

    def body(q_ref, k_ref, v_ref, o_ref):
        later = (_iota2((CS, CS), 0) > _iota2((CS, CS), 1)).astype(BF16)
        lanes = _head_lanes((BQ, PAIR))

        def qblock(i, _):
            q = q_ref[_qrows(i), :]
            q_heads = [_only_head(q, lanes[h]) for h in range(2)]

            def step(kc, state, dchunk):
                k = k_ref[_krows(kc), :]
                v = v_ref[_krows(kc), :]
                out = []
                r0 = _live_rows(dchunk)
                for h in range(2):
                    carry, acc = state[h]
                    w, _, lnb = _sb_chunk_weights(_tail(q_heads[h], r0), k, later, _tail(carry, r0), dchunk)
                    out.append((_with_tail(carry, _tail(carry, r0) + jnp.sum(lnb, axis=1, keepdims=True), r0),
                                _with_tail(acc, _tail(acc, r0) + _dot(w, v, NN), r0)))
                return tuple(out)

            zero = (jnp.zeros((BQ, 1), F32), jnp.zeros((BQ, PAIR), F32))
            (_, acc0), (_, acc1) = _walk_chunks(i, step, (zero, zero), True)
            o_ref[_qrows(i), :] = jnp.where(lanes[0], acc0, acc1).astype(BF16)
            return 0

        lax.fori_loop(0, t // BQ, qblock, 0)

    return pl.pallas_call(
        body, name=name, grid=(N_PAIRS,),
        in_specs=[_pair_spec(t, 0), _pair_spec(t, N_PAIRS), _pair_spec(t, 2 * N_PAIRS)],
        out_specs=_pair_spec(t, 0), out_shape=jax.ShapeDtypeStruct((t, WIDTH), BF16),
        compiler_params=_params(("parallel",)),
    )(qkv, qkv, qkv)


def _emit_dqkv(res, o_ref):
    o_ref[...] = res[pl.program_id(1)]


def _flush_transposed(acc, res, which):
    for kc in range(acc.shape[0]):
        res[which, kc * CS:(kc + 1) * CS, :] = acc[kc].T.astype(BF16)


def sb_pair_bwd(name, qkv, dy, dqkv):
    t = qkv.shape[0]
    nc = t // CS

    def body(q_ref, k_ref, v_ref, do_ref, _, o_ref, g_s, b_s, dkt_acc, dvt_acc, res):
        @pl.when(pl.program_id(1) == 0)
        def _():
            later = (_iota2((CS, CS), 0) > _iota2((CS, CS), 1)).astype(BF16)
            earlier = (_iota2((CS, CS), 0) < _iota2((CS, CS), 1)).astype(BF16)
            lanes = _head_lanes((BQ, PAIR))
            dkt_acc[...] = jnp.zeros_like(dkt_acc)
            dvt_acc[...] = jnp.zeros_like(dvt_acc)

            def qblock(i, _):
                q = q_ref[_qrows(i), :]
                do = do_ref[_qrows(i), :]
                q_heads = [_only_head(q, lanes[h]) for h in range(2)]
                do_heads = [_only_head(do, lanes[h]) for h in range(2)]
                qt_heads = [qh.astype(F32).T.astype(BF16) for qh in q_heads]
                dot_heads = [dh.astype(F32).T.astype(BF16) for dh in do_heads]

                def step1(kc, carries, dchunk):
                    k = k_ref[_krows(kc), :]
                    v = v_ref[_krows(kc), :]
                    out = []
                    r0 = _live_rows(dchunk)
                    for h in range(2):
                        carry = _tail(carries[h], r0)
                        w, lsz, lnb = _sb_chunk_weights(_tail(q_heads[h], r0), k, later, carry, dchunk)
                        g_s[h, kc, r0:, :] = (w * _dot(_tail(do_heads[h], r0), v, NT)).astype(BF16)
                        b_s[h, kc, r0:, :] = jnp.exp2(lsz).astype(BF16)
                        dvt_acc[kc] += _dot(_tail(dot_heads[h], r0, axis=1), w, NN)
                        out.append(_with_tail(carries[h], carry + jnp.sum(lnb, axis=1, keepdims=True), r0))
                    return tuple(out)

                zero = jnp.zeros((BQ, 1), F32)
                _walk_chunks(i, step1, (zero, zero), True)

                def step2(kc, state, dchunk):
                    k = k_ref[_krows(kc), :]
                    out = []
                    r0 = _live_rows(dchunk)
                    for h in range(2):
                        before, dq = state[h]
                        g16 = g_s[h, kc, r0:, :]
                        g = g16.astype(F32)
                        beta = b_s[h, kc, r0:, :].astype(F32)
                        prefix = lax.dot_general(g16, earlier, (NN, ((), ())), preferred_element_type=F32) + _tail(before, r0)
                        dz = g * (1.0 - beta) - beta * prefix
                        if dchunk is not None:
                            dz = jnp.where(_diag_mask(dchunk, False), dz, 0.0)
                        dzb = dz.astype(BF16)
                        dkt_acc[kc] += _dot(_tail(qt_heads[h], r0, axis=1), dzb, NN)
                        out.append((_with_tail(before, _tail(before, r0) + jnp.sum(g, axis=1, keepdims=True), r0),
                                    _with_tail(dq, _tail(dq, r0) + _dot(dzb, k, NN), r0)))
                    return tuple(out)

                start = (zero, jnp.zeros((BQ, PAIR), F32))
                (_, dq0), (_, dq1) = _walk_chunks(i, step2, (start, start), False)
                res[0, _qrows(i), :] = (jnp.where(lanes[0], dq0, dq1) * ATTN_SCALE).astype(BF16)
                return 0

            lax.fori_loop(0, t // BQ, qblock, 0)
            _flush_transposed(dkt_acc, res, 1)
            _flush_transposed(dvt_acc, res, 2)

        _emit_dqkv(res, o_ref)

    return pl.pallas_call(
        body, name=name, grid=(N_PAIRS, 3),
        in_specs=[_pair_spec(t, 0), _pair_spec(t, N_PAIRS), _pair_spec(t, 2 * N_PAIRS), _pair_spec(t, 0), _ANY],
        out_specs=pl.BlockSpec((t, PAIR), lambda p, s: (0, s * N_PAIRS + p)),
        out_shape=jax.ShapeDtypeStruct(dqkv.shape, BF16), input_output_aliases={4: 0},
        scratch_shapes=[pltpu.VMEM((2, nc, BQ, CS), BF16), pltpu.VMEM((2, nc, BQ, CS), BF16),
                        pltpu.VMEM((nc, PAIR, CS), F32), pltpu.VMEM((nc, PAIR, CS), F32), pltpu.VMEM((3, t, PAIR), BF16)],
        compiler_params=_params(("parallel", "arbitrary")),
    )(qkv, qkv, qkv, dy, dqkv)


def _gates_col_spec(t):
    return pl.BlockSpec((2, t, 1), lambda p, *_: (p, 0, 0))


def _gates_row_spec(nc):
    return pl.BlockSpec((2, nc, 1, CS), lambda p, *_: (p, 0, 0, 0))


def fox_pair_fwd(name, qkv, c_col, c_row):
    t = qkv.shape[0]

    def body(q_ref, k_ref, v_ref, cc_ref, cr_ref, o_ref, lse_ref):
        lanes = _head_lanes((BQ, PAIR))

        def qblock(i, _):
            q = q_ref[_qrows(i), :]
            q_heads = [_only_head(q, lanes[h]) for h in range(2)]
            ct = [cc_ref[h, _qrows(i), :] for h in range(2)]

            def step(kc, state, dchunk):
                k = k_ref[_krows(kc), :]
                v = v_ref[_krows(kc), :]
                out = []
                r0 = _live_rows(dchunk)
                for h in range(2):
                    m, l, acc = (_tail(a, r0) for a in state[h])
                    s = _dot(_tail(q_heads[h], r0), k, NT) + _tail(ct[h], r0) - cr_ref[h, kc]
                    if dchunk is not None:
                        s = jnp.where(_diag_mask(dchunk, True), s, NEG_BIG)
                    m_new = jnp.maximum(m, jnp.max(s, axis=1, keepdims=True))
                    alpha = jnp.exp(m - m_new)
                    p = jnp.exp(s - m_new)
                    if dchunk is not None:
                        p = jnp.where(_diag_mask(dchunk, True), p, 0.0)
                    new = (m_new, l * alpha + jnp.sum(p, axis=1, keepdims=True), acc * alpha + _dot(p, v, NN))
                    out.append(tuple(_with_tail(old, tail, r0) for old, tail in zip(state[h], new)))
                return tuple(out)

            init = (jnp.full((BQ, 1), NEG_BIG, F32), jnp.zeros((BQ, 1), F32), jnp.zeros((BQ, PAIR), F32))
            (m0, l0, acc0), (m1, l1, acc1) = _walk_chunks(i, step, (init, init), False)
            o_ref[_qrows(i), :] = jnp.where(lanes[0], acc0 / l0, acc1 / l1).astype(BF16)
            lse_ref[0, _qrows(i), :] = m0 + jnp.log(l0)
            lse_ref[1, _qrows(i), :] = m1 + jnp.log(l1)
            return 0

        lax.fori_loop(0, t // BQ, qblock, 0)

    return pl.pallas_call(
        body, name=name, grid=(N_PAIRS,),
        in_specs=[_pair_spec(t, 3 * N_PAIRS), _pair_spec(t, 4 * N_PAIRS), _pair_spec(t, 5 * N_PAIRS),
                  _gates_col_spec(t), _gates_row_spec(t // CS)],
        out_specs=[_pair_spec(t, 0), _gates_col_spec(t)],
        out_shape=[jax.ShapeDtypeStruct((t, WIDTH), BF16), jax.ShapeDtypeStruct((N_HEADS, t, 1), F32)],
        compiler_params=_params(("parallel",)),
    )(qkv, qkv, qkv, c_col, c_row)


def fox_pair_bwd(name, qkv, y, dy, lse, c_col, c_row, dqkv):
    t = qkv.shape[0]
    nc = t // CS

    def body(q_ref, k_ref, v_ref, o_in_ref, do_ref, lse_ref, cc_ref, cr_ref, _, o_ref, dcc_ref, dcr_ref,
             dkt_acc, dvt_acc, dcr_acc, res):
        @pl.when(pl.program_id(1) == 0)
        def _():
            lanes = _head_lanes((BQ, PAIR))
            dkt_acc[...] = jnp.zeros_like(dkt_acc)
            dvt_acc[...] = jnp.zeros_like(dvt_acc)
            dcr_acc[...] = jnp.zeros_like(dcr_acc)

            def qblock(i, _):
                q = q_ref[_qrows(i), :]
                do = do_ref[_qrows(i), :]
                q_heads = [_only_head(q, lanes[h]) for h in range(2)]
                do_heads = [_only_head(do, lanes[h]) for h in range(2)]
                qt_heads = [qh.astype(F32).T.astype(BF16) for qh in q_heads]
                dot_heads = [dh.astype(F32).T.astype(BF16) for dh in do_heads]
                prod = do.astype(F32) * o_in_ref[_qrows(i), :].astype(F32)
                delta = [jnp.sum(_only_head(prod, lanes[h]), axis=1, keepdims=True) for h in range(2)]
                ct = [cc_ref[h, _qrows(i), :] for h in range(2)]
                lse_i = [lse_ref[h, _qrows(i), :] for h in range(2)]

                def step(kc, state, dchunk):
                    k = k_ref[_krows(kc), :]
                    v = v_ref[_krows(kc), :]
                    out = []
                    r0 = _live_rows(dchunk)
                    for h in range(2):
                        dq, dct = state[h]
                        s = _dot(_tail(q_heads[h], r0), k, NT) + _tail(ct[h], r0) - cr_ref[h, kc]
                        p = jnp.exp(s - _tail(lse_i[h], r0))
                        if dchunk is not None:
                            p = jnp.where(_diag_mask(dchunk, True), p, 0.0)
                        ds = p * (_dot(_tail(do_heads[h], r0), v, NT) - _tail(delta[h], r0))
                        dvt_acc[kc] += _dot(_tail(dot_heads[h], r0, axis=1), p, NN)
                        dsb = ds.astype(BF16)
                        dkt_acc[kc] += _dot(_tail(qt_heads[h], r0, axis=1), dsb, NN)
                        dcr_acc[h, kc] -= jnp.sum(ds, axis=0, keepdims=True)
                        out.append((_with_tail(dq, _tail(dq, r0) + _dot(dsb, k, NN), r0),
                                    _with_tail(dct, _tail(dct, r0) + jnp.sum(ds, axis=1, keepdims=True), r0)))
                    return tuple(out)

                zero = (jnp.zeros((BQ, PAIR), F32), jnp.zeros((BQ, 1), F32))
                (dq0, dct0), (dq1, dct1) = _walk_chunks(i, step, (zero, zero), False)
                res[0, _qrows(i), :] = (jnp.where(lanes[0], dq0, dq1) * ATTN_SCALE).astype(BF16)
                lane = _iota2((BQ, PAIR), 1)
                dcc_ref[_qrows(i), :] = jnp.where(lane == 0, dct0, jnp.where(lane == 1, dct1, 0.0))
                return 0

            lax.fori_loop(0, t // BQ, qblock, 0)
            _flush_transposed(dkt_acc, res, 1)
            _flush_transposed(dvt_acc, res, 2)
            dcr_ref[...] = dcr_acc[...]

        _emit_dqkv(res, o_ref)

    return pl.pallas_call(
        body, name=name, grid=(N_PAIRS, 3),
        in_specs=[_pair_spec(t, 3 * N_PAIRS), _pair_spec(t, 4 * N_PAIRS), _pair_spec(t, 5 * N_PAIRS), _pair_spec(t, 0),
                  _pair_spec(t, 0), _gates_col_spec(t), _gates_col_spec(t), _gates_row_spec(nc), _ANY],
        out_specs=[pl.BlockSpec((t, PAIR), lambda p, s: (0, (3 + s) * N_PAIRS + p)),
                   pl.BlockSpec((None, t, PAIR), lambda p, s: (p, 0, 0)), _gates_row_spec(nc)],
        out_shape=[jax.ShapeDtypeStruct(dqkv.shape, BF16), jax.ShapeDtypeStruct((N_PAIRS, t, PAIR), F32),
                   jax.ShapeDtypeStruct((N_HEADS, nc, 1, CS), F32)],
        input_output_aliases={8: 0},
        scratch_shapes=[pltpu.VMEM((nc, PAIR, CS), F32), pltpu.VMEM((nc, PAIR, CS), F32), pltpu.VMEM((2, nc, 1, CS), F32),
                        pltpu.VMEM((3, t, PAIR), BF16)],
        compiler_params=_params(("parallel", "arbitrary")),
    )(qkv, qkv, qkv, y, dy, lse, c_col, c_row, dqkv)


def loss_head(name, x, g, target):
    t, d = x.shape
    tr = _row_tile(t, 256)

    def body(x_ref, g_ref, t_ref, dx_ref, gn_ref, loss_ref):
        xv = x_ref[...]
        r = lax.rsqrt(jnp.mean(xv * xv, axis=-1, keepdims=True) + RMS_EPS)
        xhat = xv * r
        gv = g_ref[...]
        err = xhat * gv - t_ref[...]
        part_loss = 0.5 * jnp.sum(jnp.mean(err * err, axis=-1, keepdims=True), axis=0, keepdims=True)
        dy = err * (1.0 / d)
        dyg = dy * gv
        dx_ref[...] = r * (dyg - xhat * jnp.mean(dyg * xhat, axis=-1, keepdims=True))
        part_g = jnp.sum(dy * xhat, axis=0, keepdims=True)

        @pl.when(pl.program_id(0) == 0)
        def _():
            gn_ref[...] = part_g
            loss_ref[...] = part_loss

        @pl.when(pl.program_id(0) != 0)
        def _():
            gn_ref[...] += part_g
            loss_ref[...] += part_loss

    row = pl.BlockSpec((tr, d), lambda i: (i, 0))
    return pl.pallas_call(
        body, name=name, grid=(t // tr,),
        in_specs=[row, pl.BlockSpec((1, d), lambda i: (0, 0)), row],
        out_specs=[row, pl.BlockSpec((1, d), lambda i: (0, 0)), pl.BlockSpec((1, 1), lambda i: (0, 0))],
        out_shape=[jax.ShapeDtypeStruct((t, d), F32), jax.ShapeDtypeStruct((1, d), F32), jax.ShapeDtypeStruct((1, 1), F32)],
        compiler_params=_params(("arbitrary",)),
    )(x, g, target)


def _place():
    return lax.axis_index("x"), lax.axis_index("y"), lax.axis_index("c")


def _other_chips(x, y):
    return [(1 - x, y), (x, 1 - y), (1 - x, 1 - y)]


def _half(ref, c, rows):
    if rows % 32 == 0:
        return ref.at[:, pl.ds(c * (rows // 2), rows // 2), :]
    cols = ref.shape[2]
    return ref.at[:, :, pl.ds(c * (cols // 2), cols // 2)]


_ANY = pl.BlockSpec(memory_space=pl.ANY)


_HBM = pl.BlockSpec(memory_space=pltpu.HBM)
_SEM = pl.BlockSpec(memory_space=pltpu.SEMAPHORE)
_DATAFLOW = pltpu.SideEffectType.DATAFLOW_SIDE_EFFECTING


def _in_hbm(a):
    return pltpu.with_memory_space_constraint(a, pltpu.HBM)


def _gather_ici_copies(bufs, send_sems, recv_sems, arrivals):
    x, y, c = _place()
    me = 2 * x + y
    copies = []
    for i, buf in enumerate(bufs):
        rows = buf.shape[2]
        for j, (qx, qy) in enumerate(_other_chips(x, y)):
            block = _half(buf.at[2 * qx + qy if arrivals else me], c, rows)
            copies.append(pltpu.make_async_remote_copy(
                src_ref=block, dst_ref=block, send_sem=send_sems.at[3 * i + j], recv_sem=recv_sems.at[3 * i + j],
                device_id=(qx, qy, c), device_id_type=MESH))
    return copies


def gather_ici_start(name, bufs, after):
    n = len(bufs)

    def body(*refs):
        ins = refs[:n]
        send_sems, recv_sems = refs[n + 1], refs[n + 2]
        token = refs[-1]
        for send in _gather_ici_copies(ins, send_sems, recv_sems, False):
            send.start()
        token[...] = jnp.zeros_like(token)

    res = pl.pallas_call(
        body, name=name,
        out_shape=(pltpu.SemaphoreType.DMA((3 * n,)), pltpu.SemaphoreType.DMA((3 * n,)), *[pltpu.HBM(b.shape, b.dtype) for b in bufs],
                   jax.ShapeDtypeStruct((8, 128), F32)),
        in_specs=[_HBM] * n + [_ANY], out_specs=(_SEM, _SEM, *[_HBM] * n, pl.BlockSpec(memory_space=pltpu.VMEM)),
        input_output_aliases={i: 2 + i for i in range(n)},
        compiler_params=pltpu.CompilerParams(has_side_effects=_DATAFLOW),
    )(*[_in_hbm(b) for b in bufs], after)
    return res[0], res[1], list(res[2:2 + n]), res[-1]


def gather_ici_wait(name, send_sems, recv_sems, bufs, after):
    n = len(bufs)

    def body(*refs):
        ins = refs[:n]
        send_sems_ref, recv_sems_ref = refs[n], refs[n + 1]
        for send in _gather_ici_copies(ins, send_sems_ref, recv_sems_ref, False):
            send.wait_send()
        for recv in _gather_ici_copies(ins, send_sems_ref, recv_sems_ref, True):
            recv.wait_recv()

    return pl.pallas_call(
        body, name=name, out_shape=tuple(pltpu.HBM(b.shape, b.dtype) for b in bufs),
        in_specs=[_HBM] * n + [_SEM, _SEM, _ANY], out_specs=tuple([_HBM] * n),
        input_output_aliases={i: i for i in range(n)},
        compiler_params=pltpu.CompilerParams(has_side_effects=_DATAFLOW),
    )(*bufs, send_sems, recv_sems, after)


def gather_forward(name, bufs):
    n = len(bufs)

    def body(*refs):
        outs = refs[n:2 * n]
        send_sems, recv_sems = refs[2 * n:]
        x, y, c = _place()
        sibling = (x, y, 1 - c)
        sends = []
        for i in range(n):
            rows = outs[i].shape[2]
            for j, (qx, qy) in enumerate(_other_chips(x, y)):
                block = _half(outs[i].at[2 * qx + qy], c, rows)
                fw = pltpu.make_async_remote_copy(
                    src_ref=block, dst_ref=block, send_sem=send_sems.at[3 * i + j], recv_sem=recv_sems.at[3 * i + j],
                    device_id=sibling, device_id_type=MESH)
                fw.start()
                sends.append(fw)
        for i in range(n):
            rows = outs[i].shape[2]
            for j, (qx, qy) in enumerate(_other_chips(x, y)):
                block = _half(outs[i].at[2 * qx + qy], 1 - c, rows)
                pltpu.make_async_remote_copy(
                    src_ref=block, dst_ref=block, send_sem=send_sems.at[3 * i + j], recv_sem=recv_sems.at[3 * i + j],
                    device_id=sibling, device_id_type=MESH).wait_recv()
        for fw in sends:
            fw.wait_send()

    return pl.pallas_call(
        body, name=name, in_specs=[_ANY] * n, out_specs=[_ANY] * n,
        out_shape=[jax.ShapeDtypeStruct(b.shape, b.dtype) for b in bufs],
        input_output_aliases={i: i for i in range(n)},
        scratch_shapes=[pltpu.SemaphoreType.DMA((3 * n,)), pltpu.SemaphoreType.DMA((3 * n,))],
        compiler_params=pltpu.CompilerParams(has_side_effects=True),
    )(*bufs)


def _between_chips_copies(parts, lands, send_sems, recv_sems):
    x, y, c = _place()
    copies = []
    for i, (part, land) in enumerate(zip(parts, lands)):
        for j, (qx, qy) in enumerate(_other_chips(x, y)):
            copies.append(pltpu.make_async_remote_copy(
                src_ref=part.at[2 * qx + qy], dst_ref=land.at[j], send_sem=send_sems.at[3 * i + j], recv_sem=recv_sems.at[3 * i + j],
                device_id=(qx, qy, c), device_id_type=MESH))
    return copies


def between_chips_start(name, parts):
    n = len(parts)
    lands = [lax.empty((N_CHIPS - 1,) + p.shape[1:], p.dtype) for p in parts]

    def body(*refs):
        send_sems, recv_sems = refs[2 * n], refs[2 * n + 1]
        token = refs[-1]
        for cp in _between_chips_copies(refs[:n], refs[n:2 * n], send_sems, recv_sems):
            cp.start()
        token[...] = jnp.zeros_like(token)

    res = pl.pallas_call(
        body, name=name,
        out_shape=(pltpu.SemaphoreType.DMA((3 * n,)), pltpu.SemaphoreType.DMA((3 * n,)),
                   *[pltpu.HBM(a.shape, a.dtype) for a in parts + lands], jax.ShapeDtypeStruct((8, 128), F32)),
        in_specs=[_HBM] * (2 * n), out_specs=(_SEM, _SEM, *[_HBM] * (2 * n), pl.BlockSpec(memory_space=pltpu.VMEM)),
        input_output_aliases={i: 2 + i for i in range(2 * n)},
        compiler_params=pltpu.CompilerParams(has_side_effects=_DATAFLOW),
    )(*[_in_hbm(a) for a in parts + lands])
    return res[0], res[1], list(res[2:2 + n]), list(res[2 + n:2 + 2 * n]), res[-1]


def between_chips_wait(name, send_sems, recv_sems, parts, lands, after):
    n = len(parts)

    def body(*refs):
        for cp in _between_chips_copies(refs[:n], refs[n:2 * n], refs[2 * n], refs[2 * n + 1]):
            cp.wait_send()
            cp.wait_recv()

    res = pl.pallas_call(
        body, name=name, out_shape=tuple(pltpu.HBM(a.shape, a.dtype) for a in parts + lands),
        in_specs=[_HBM] * (2 * n) + [_SEM, _SEM, _ANY], out_specs=tuple([_HBM] * (2 * n)),
        input_output_aliases={i: i for i in range(2 * n)},
        compiler_params=pltpu.CompilerParams(has_side_effects=_DATAFLOW),
    )(*parts, *lands, send_sems, recv_sems, after)
    return list(res[:n]), list(res[n:])


def exchange_start(name, arrays, n_copies, copies, after=()):
    n = len(arrays)

    def body(*refs):
        send_sems, recv_sems = refs[n + len(after)], refs[n + len(after) + 1]
        for cp in copies(refs[:n], send_sems, recv_sems):
            cp.start()
        refs[-1][...] = jnp.zeros_like(refs[-1])

    res = pl.pallas_call(
        body, name=name,
        out_shape=(pltpu.SemaphoreType.DMA((n_copies,)), pltpu.SemaphoreType.DMA((n_copies,)),
                   *[pltpu.HBM(a.shape, a.dtype) for a in arrays], jax.ShapeDtypeStruct((8, 128), F32)),
        in_specs=[_HBM] * n + [_ANY] * len(after),
        out_specs=(_SEM, _SEM, *[_HBM] * n, pl.BlockSpec(memory_space=pltpu.VMEM)),
        input_output_aliases={i: 2 + i for i in range(n)},
        compiler_params=pltpu.CompilerParams(has_side_effects=_DATAFLOW),
    )(*[_in_hbm(a) for a in arrays], *after)
    return res[0], res[1], list(res[2:2 + n]), res[-1]


def exchange_wait(name, send_sems, recv_sems, arrays, copies, after):
    n = len(arrays)

    def body(*refs):
        for cp in copies(refs[:n], refs[n], refs[n + 1]):
            cp.wait_send()
            cp.wait_recv()

    return list(pl.pallas_call(
        body, name=name, out_shape=tuple(pltpu.HBM(a.shape, a.dtype) for a in arrays),
        in_specs=[_HBM] * n + [_SEM, _SEM, _ANY], out_specs=tuple([_HBM] * n),
        input_output_aliases={i: i for i in range(n)},
        compiler_params=pltpu.CompilerParams(has_side_effects=_DATAFLOW),
    )(*arrays, send_sems, recv_sems, after))


def _to_sibling_copies(n):
    def copies(refs, send_sems, recv_sems):
        x, y, c = _place()
        out = []
        for i in range(n):
            rows = refs[i].shape[2]
            out.append(pltpu.make_async_remote_copy(
                src_ref=refs[i].at[:, :, pl.ds((1 - c) * (rows // 2), rows // 2), :], dst_ref=refs[n + i],
                send_sem=send_sems.at[i], recv_sem=recv_sems.at[i], device_id=(x, y, 1 - c), device_id_type=MESH))
        return out
    return copies


def _share_copies(n):
    def copies(refs, send_sems, recv_sems):
        x, y, c = _place()
        out = []
        for i in range(n):
            mine = _half(refs[i], c, refs[i].shape[1])
            out.append(pltpu.make_async_remote_copy(
                src_ref=mine, dst_ref=mine, send_sem=send_sems.at[i], recv_sem=recv_sems.at[i],
                device_id=(x, y, 1 - c), device_id_type=MESH))
        return out
    return copies


def pair_sum(name, grad, recv, c):
    ns, na, rh, cols = recv.shape
    tr = _row_tile(rh, 256) if rh % 256 == 0 else rh
    nt = rh // tr

    def body(c_ref, g_ref, r_ref, o_ref):
        o_ref[...] = (g_ref[...].astype(F32) + r_ref[...].astype(F32)).astype(BF16)

    blk = (None, None, tr, cols)
    return pl.pallas_call(
        body, name=name,
        grid_spec=pltpu.PrefetchScalarGridSpec(
            num_scalar_prefetch=1, grid=(ns, na, nt),
            in_specs=[pl.BlockSpec(blk, lambda s, a, r, c_ref: (s, a, c_ref[0] * nt + r, 0)),
                      pl.BlockSpec(blk, lambda s, a, r, c_ref: (s, a, r, 0))],
            out_specs=pl.BlockSpec(blk, lambda s, a, r, c_ref: (s, a, r, 0))),
        out_shape=jax.ShapeDtypeStruct(recv.shape, BF16),
        compiler_params=_params(("parallel", "parallel", "parallel")),
    )(c, grad, recv)


def chip_sum(name, parts, landed, place):
    _, na, rh, cols = parts.shape
    tr = _row_tile(rh, 256) if rh % 256 == 0 else rh
    nt = rh // tr

    def body(place_ref, p_ref, l_ref, o_ref):
        total = p_ref[...].astype(F32)
        for s in range(N_CHIPS - 1):
            total = total + l_ref[s].astype(F32)
        o_ref[...] = total

    return pl.pallas_call(
        body, name=name,
        grid_spec=pltpu.PrefetchScalarGridSpec(
            num_scalar_prefetch=1, grid=(na, nt),
            in_specs=[pl.BlockSpec((None, None, tr, cols), lambda a, r, pr: (pr[1], a, r, 0)),
                      pl.BlockSpec((N_CHIPS - 1, None, tr, cols), lambda a, r, pr: (0, a, r, 0))],
            out_specs=pl.BlockSpec((None, tr, cols), lambda a, r, pr: (a, pr[0] * nt + r, 0))),
        out_shape=jax.ShapeDtypeStruct((na, 2 * rh, cols), F32),
        compiler_params=_params(("parallel", "parallel")),
    )(place, parts, landed)


def reduce_scatter_1(tag, grads):
    n = len(grads)
    lands = [lax.empty((g.shape[0], g.shape[1], g.shape[2] // 2, g.shape[3]), g.dtype) for g in grads]
    send_sems, recv_sems, arrays, token = exchange_start(f"{tag}_to_sibling_start", list(grads) + lands, n, _to_sibling_copies(n))
    return (send_sems, recv_sems, arrays), token


def reduce_scatter_2(tag, state, place, after):
    send_sems, recv_sems, arrays = state
    n = len(arrays) // 2
    arrays = exchange_wait(f"{tag}_to_sibling_wait", send_sems, recv_sems, arrays, _to_sibling_copies(n), after)
    parts = [pair_sum(f"{tag}_pair_sum{i}", g, r, place) for i, (g, r) in enumerate(zip(arrays[:n], arrays[n:]))]
    send_sems, recv_sems, parts, lands, token = between_chips_start(f"{tag}_between_chips_start", parts)
    return (send_sems, recv_sems, parts, lands), token


def reduce_scatter_3(tag, state, place, after):
    send_sems, recv_sems, parts, lands = state
    parts, landed = between_chips_wait(f"{tag}_between_chips_wait", send_sems, recv_sems, parts, lands, after)
    halves = [chip_sum(f"{tag}_chip_sum{i}", p, l, place) for i, (p, l) in enumerate(zip(parts, landed))]
    send_sems, recv_sems, halves, token = exchange_start(f"{tag}_share_start", halves, len(halves), _share_copies(len(halves)))
    return (send_sems, recv_sems, halves), token


def reduce_scatter_4(tag, state, after):
    send_sems, recv_sems, halves = state
    return exchange_wait(f"{tag}_share_wait", send_sems, recv_sems, halves, _share_copies(len(halves)), after)


def _adamw_math(w, g, m, v):
    m = ADAM_B1 * m + (1.0 - ADAM_B1) * g
    v = ADAM_B2 * v + (1.0 - ADAM_B2) * (g * g)
    m_hat = m / (1.0 - ADAM_B1 ** ADAM_STEP)
    v_hat = v / (1.0 - ADAM_B2 ** ADAM_STEP)
    delta = -ADAM_LR * (m_hat / (jnp.sqrt(v_hat) + ADAM_EPS) + ADAM_WD * w)
    return delta, m, v


def adamw(name, w, g, m, v, after=()):
    rows, cols = w.shape
    tr = _row_tile(rows, 256) if rows % 256 == 0 else rows // 2

    def body(w_ref, g_ref, m_ref, v_ref, d_ref, mo_ref, vo_ref):
        d_ref[...], mo_ref[...], vo_ref[...] = _adamw_math(w_ref[...], g_ref[...], m_ref[...], v_ref[...])

    blk = pl.BlockSpec((tr, cols), lambda i: (i, 0))
    return pl.pallas_call(
        _ordered_after(body, 4, len(after)), name=name, grid=(rows // tr,), in_specs=[blk] * 4 + [_ANY] * len(after),
        out_specs=[blk] * 3, out_shape=[jax.ShapeDtypeStruct(w.shape, F32)] * 3, compiler_params=_params(("parallel",)),
    )(w, g, m, v, *after)


def adamw_rows(name, w, g, m, v, after=()):
    rows, _, cols = w.shape
    tr = next(r for r in (128, 110, 64, 32, 16, 8, 1) if rows % r == 0)

    def body(w_ref, g_ref, m_ref, v_ref, d_ref, mo_ref, vo_ref):
        d_ref[...], mo_ref[...], vo_ref[...] = _adamw_math(w_ref[...], g_ref[...], m_ref[...], v_ref[...])

    blk = pl.BlockSpec((tr, 1, cols), lambda i: (i, 0, 0))
    return pl.pallas_call(
        _ordered_after(body, 4, len(after)), name=name, grid=(rows // tr,), in_specs=[blk] * 4 + [_ANY] * len(after),
        out_specs=[blk] * 3, out_shape=[jax.ShapeDtypeStruct(w.shape, F32)] * 3, compiler_params=_params(("parallel",)),
    )(w, g, m, v, *after)


def adamw_stacked(name, ws, g, ms, vs, after=()):
    n = len(ws)
    rows, cols = ws[0].shape
    tr = next(r for r in (128, 88, 64, 32, 16, 8) if rows % r == 0)

    def body(*refs):
        w_refs, m_refs, v_refs, g_ref = refs[:n], refs[n:2 * n], refs[2 * n:3 * n], refs[3 * n]
        outs = refs[3 * n + 1:]
        for i in range(n):
            outs[i][...], outs[n + i][...], outs[2 * n + i][...] = _adamw_math(
                w_refs[i][...], g_ref[i], m_refs[i][...], v_refs[i][...])

    blk = pl.BlockSpec((tr, cols), lambda r: (r, 0))
    res = pl.pallas_call(
        _ordered_after(body, 3 * n + 1, len(after)), name=name, grid=(rows // tr,),
        in_specs=[blk] * (3 * n) + [pl.BlockSpec((n, tr, cols), lambda r: (0, r, 0))] + [_ANY] * len(after),
        out_specs=[blk] * (3 * n),
        out_shape=[jax.ShapeDtypeStruct((rows, cols), F32)] * (3 * n), compiler_params=_params(("parallel",)),
    )(*ws, *ms, *vs, g, *after)
    return res[:n], res[n:2 * n], res[2 * n:]


def small_allreduce_adamw(name, g_part, w, m, v):
    rows, cols = g_part.shape

    def body(g_ref, w_ref, m_ref, v_ref, sum_ref, d_ref, mo_ref, vo_ref, land, send_sems, recv_sems):
        x, y, c = _place()
        me = 4 * x + 2 * y + c
        land[me] = g_ref[...]
        copies = []
        for r in range(1, 8):
            peer = (x ^ (r >> 2), y ^ ((r >> 1) & 1), c ^ (r & 1))
            rc = pltpu.make_async_remote_copy(
                src_ref=g_ref, dst_ref=land.at[me], send_sem=send_sems.at[r - 1], recv_sem=recv_sems.at[r - 1],
                device_id=peer, device_id_type=MESH)
            rc.start()
            copies.append(rc)
        for rc in copies:
            rc.wait()
        total = land[0]
        for s in range(1, 8):
            total = total + land[s]
        sum_ref[...] = total
        d_ref[...], mo_ref[...], vo_ref[...] = _adamw_math(w_ref[...], total, m_ref[...], v_ref[...])

    vmem = pl.BlockSpec(memory_space=pltpu.VMEM)
    return pl.pallas_call(
        body, name=name, in_specs=[vmem] * 4, out_specs=[vmem] * 4,
        out_shape=[jax.ShapeDtypeStruct((rows, cols), F32)] * 4,
        scratch_shapes=[pltpu.VMEM((8, rows, cols), F32), pltpu.SemaphoreType.DMA((7,)), pltpu.SemaphoreType.DMA((7,))],
        compiler_params=pltpu.CompilerParams(has_side_effects=True),
    )(g_part, w, m, v)


def kernel(x, norm_ffn1, w_ffn1_gate, w_ffn1_up, w_ffn1_down, norm_mix, w_in, b_forget, w_gate, b_gate, w_up_a, w_up_b, w_out, norm_ffn2, w_ffn2_gate, w_ffn2_up, w_ffn2_down, norm_final, loss_target, m_norm_ffn1, m_w_ffn1_gate, m_w_ffn1_up, m_w_ffn1_down, m_norm_mix, m_w_in, m_b_forget, m_w_gate, m_b_gate, m_w_up_a, m_w_up_b, m_w_out, m_norm_ffn2, m_w_ffn2_gate, m_w_ffn2_up, m_w_ffn2_down, m_norm_final, v_norm_ffn1, v_w_ffn1_gate, v_w_ffn1_up, v_w_ffn1_down, v_norm_mix, v_w_in, v_b_forget, v_w_gate, v_b_gate, v_w_up_a, v_w_up_b, v_w_out, v_norm_ffn2, v_w_ffn2_gate, v_w_ffn2_up, v_w_ffn2_down, v_norm_final):
    t, d = x.shape[1], x.shape[2]
    in4 = w_in.shape[2]
    gate4 = w_gate.shape[2]
    in_cols = N_CHIPS * in4
    n_forget = in_cols - QKV_COLS
    assert w_up_a.shape[1] == WIDTH and d == 2 * WIDTH and n_forget == N_HEADS and t % BQ == 0
    chip = 2 * lax.axis_index("x") + lax.axis_index("y")
    c_arr = jnp.stack([lax.axis_index("c"), chip]).astype(jnp.int32)
    x2d = x[0]
    target = loss_target[0]

    def slot(shard):
        return lax.dynamic_update_slice(lax.empty((N_CHIPS,) + shard.shape, BF16), shard.astype(BF16)[None], (chip, 0, 0, 0))

    def ffn_views(wg, wu, wd):
        return [wg[0].T, wu[0].T, wd[0]]

    ffn1_w, ffn1_m, ffn1_v = (ffn_views(w_ffn1_gate, w_ffn1_up, w_ffn1_down), ffn_views(m_w_ffn1_gate, m_w_ffn1_up, m_w_ffn1_down),
                              ffn_views(v_w_ffn1_gate, v_w_ffn1_up, v_w_ffn1_down))
    ffn2_w, ffn2_m, ffn2_v = (ffn_views(w_ffn2_gate, w_ffn2_up, w_ffn2_down), ffn_views(m_w_ffn2_gate, m_w_ffn2_up, m_w_ffn2_down),
                              ffn_views(v_w_ffn2_gate, v_w_ffn2_up, v_w_ffn2_down))
    in_pad = -(-in4 // 32) * 32
    in_sh = slot(w_in[0].T[None])
    gt_sh = slot(w_gate[0].T[None])
    up_sh = slot(jnp.concatenate([w_up_a[0], w_up_b[0]], axis=0)[None])
    wo_sh = slot(w_out)
    f1_send, f1_recv, f1_bufs, f1_token = gather_ici_start("gather_ffn1_start", [slot(jnp.stack(ffn1_w))], norm_ffn1)
    mx_send, mx_recv, mx_bufs, mx_token = gather_ici_start("gather_mixer_start", [in_sh, gt_sh, up_sh, wo_sh], f1_token)
    f2_send, f2_recv, f2_bufs, f2_token = gather_ici_start("gather_ffn2_start", [slot(jnp.stack(ffn2_w))], mx_token)

    normed1 = rms_fwd("ffn1_rms", x2d, norm_ffn1, after=(f2_token,))
    (w3_1,) = gather_forward("gather_ffn1_forward", gather_ici_wait("gather_ffn1_wait", f1_send, f1_recv, f1_bufs, normed1[0]))
    x1, saved1 = ffn_forward("ffn1", x2d, norm_ffn1, w3_1, normed=normed1)
    w_in_g, w_gate_g, wup, wo = gather_forward(
        "gather_mixer_forward", gather_ici_wait("gather_mixer_wait", mx_send, mx_recv, mx_bufs, x1))
    wup = wup[:, 0]
    w_in_t = w_in_g.reshape(in_cols, d)
    w_gate_t = w_gate_g.reshape(2 * d, d)
    w_f_t = jnp.pad(w_in_t[QKV_COLS:], ((0, QB - n_forget), (0, 0)))
    wo_full = wo.reshape(d, d)
    b_forget_row = jnp.pad(b_forget, ((0, 0), (0, QB - n_forget)))

    h2, rstd2 = rms_fwd("mix_rms", x1, norm_mix)
    qkv = proj("mix_proj_qkv", h2, w_in_t, jnp.zeros((1, QKV_COLS), F32), 0, QKV_COLS, WIDTH, BF16, scaled_tiles=(0, 3))
    pc = proj("mix_proj_gates", h2, w_gate_t, b_gate, 0, 2 * d, WIDTH, F32)
    f_logit = proj("mix_proj_forget", h2, w_f_t, jnp.zeros((1, QB), F32), 0, QB, QB, F32)
    c_cum = fox_prep("fox_prep", f_logit, b_forget_row)
    c_heads = c_cum[:, :N_HEADS].T
    c_col = c_heads[:, :, None]
    c_row = c_heads.reshape(N_HEADS, t // CS, 1, CS)
    ya = sb_pair_fwd("sb_fwd", qkv)
    yb, lse = fox_pair_fwd("fox_fwd", qkv, c_col, c_row)
    ua, ub, mixed = mix_fwd("mix_fwd", ya, yb, wup, pc)
    x2 = mm_residual("mix_out", mixed[None], wo_full[None], pl.BlockSpec((1, d, d), lambda m: (0, 0, 0)), x1, 1.0)
    (w3_2,) = gather_forward("gather_ffn2_forward", gather_ici_wait("gather_ffn2_wait", f2_send, f2_recv, f2_bufs, x2))
    x3, saved2 = ffn_forward("ffn2", x2, norm_ffn2, w3_2)
    dx3, gn_final, loss_part = loss_head("loss_head", x3, norm_final[None], target)

    g_w3_2, dab2 = ffn_backward_weights("ffn2", dx3, saved2, w3_2)
    rs_ffn2, token = reduce_scatter_1("rs_ffn2", [g_w3_2])
    dx2, gn_ffn2 = ffn_backward_input("ffn2", dx3, saved2, dab2, norm_ffn2, w3_2, after=(token,))
    rs_ffn2, rs_ffn2_token = reduce_scatter_2("rs_ffn2", rs_ffn2, c_arr, dx2)

    dua, dub, dpa, dpb, gba, gbb = mix_bwd("mix_bwd", dx2, wo_full, pc, ua, ub, after=(rs_ffn2_token,))
    g_bgate = jnp.concatenate([gba, gbb], axis=1)
    g_wo = mm_plain("wgrad_out", mixed, dx2, TN, BF16, tk_target=1024)
    dya = up_bwd("dya", dua, wup, 0)
    dyb = up_bwd("dyb", dub, wup, 1)
    g_up = up_wgrad("wgrad_up", ya, yb, dua, dub)
    dqkv = sb_pair_bwd("sb_bwd", qkv, dya, lax.empty((t, QKV_COLS), BF16))
    dqkv, dcc, dcr = fox_pair_bwd("fox_bwd", qkv, yb, dyb, lse, c_col, c_row, dqkv)
    dc = dcc[:, :, :2].transpose(1, 0, 2).reshape(t, N_HEADS) + dcr.reshape(N_HEADS, t).T
    df, g_bf = fox_gate_bwd("fox_gate_bwd", jnp.pad(dc, ((0, 0), (0, QB - N_HEADS))), f_logit, b_forget_row)
    df = df.astype(BF16)
    g_qkv_t = wgrad_cat("wgrad_qkv", h2, dqkv)
    g_f_t = wgrad_cat("wgrad_forget", h2, df)
    g_gate_t = jnp.stack([wgrad_cat("wgrad_gate_a", h2, dpa), wgrad_cat("wgrad_gate_b", h2, dpb)])
    g_in_t = jnp.concatenate([g_qkv_t, g_f_t[:n_forget]], axis=0).reshape(N_CHIPS, in4, d)
    rs_mixer, token = reduce_scatter_1(
        "rs_mixer", [jnp.pad(g_in_t, ((0, 0), (0, in_pad - in4), (0, 0)))[:, None], g_gate_t.reshape(N_CHIPS, 1, gate4, d),
                     g_up[:, None], g_wo.reshape(N_CHIPS, 1, d // N_CHIPS, d)])
    dx1, gn_mix = mixer_dh("mix_dh", [(dqkv, w_in_t, 0), (dpa, w_gate_t, 0), (dpb, w_gate_t, 1), (df, w_f_t, 0)],
                           x1, rstd2, norm_mix, dx2, after=(token,))
    rs_mixer, rs_mixer_token = reduce_scatter_2("rs_mixer", rs_mixer, c_arr, dx1)
    rs_ffn2, rs_ffn2_token = reduce_scatter_3("rs_ffn2", rs_ffn2, c_arr, dx1)

    g_w3_1, dab1 = ffn_backward_weights("ffn1", dx1, saved1, w3_1, after=(rs_mixer_token, rs_ffn2_token))
    rs_ffn1, token = reduce_scatter_1("rs_ffn1", [g_w3_1])
    dx0, gn_ffn1 = ffn_backward_input("ffn1", dx1, saved1, dab1, norm_ffn1, w3_1, after=(token,))
    rs_ffn1, rs_ffn1_token = reduce_scatter_2("rs_ffn1", rs_ffn1, c_arr, dx0)
    rs_mixer, rs_mixer_token = reduce_scatter_3("rs_mixer", rs_mixer, c_arr, dx0)
    (s_w3_2,) = reduce_scatter_4("rs_ffn2", rs_ffn2, dx0)

    def pack_small(n1, nm, n2, nf, bg, bf, last):
        return jnp.concatenate([n1, nm, n2, nf, bg.reshape(2, d), jnp.pad(bf, ((0, 0), (0, d - n_forget))), last], axis=0)

    zero_row = jnp.zeros((1, d), F32)
    g_small = pack_small(gn_ffn1, gn_mix, gn_ffn2, gn_final, g_bgate, g_bf[:, :n_forget], jnp.pad(loss_part, ((0, 0), (0, d - 1))))
    w_small = pack_small(norm_ffn1, norm_mix, norm_ffn2, norm_final[None], b_gate, b_forget, zero_row)
    m_small = pack_small(m_norm_ffn1, m_norm_mix, m_norm_ffn2, m_norm_final[None], m_b_gate, m_b_forget, zero_row)
    v_small = pack_small(v_norm_ffn1, v_norm_mix, v_norm_ffn2, v_norm_final[None], v_b_gate, v_b_forget, zero_row)
    smalls = small_allreduce_adamw("small_allreduce_adamw", g_small, w_small, m_small, v_small)

    def unpack_small(p):
        return {"norm_ffn1": p[0:1], "norm_mix": p[1:2], "norm_ffn2": p[2:3], "norm_final": p[3], "b_gate": p[4:6].reshape(1, 2 * d),
                "b_forget": p[6:7, :n_forget]}

    loss = smalls[0][7, 0]
    small_out = [unpack_small(p) for p in smalls]

    big_out = {}

    def adamw_ffn(tag, s_w3, ws, ms, vs, after):
        deltas, new_ms, new_vs = adamw_stacked(f"adamw_{tag}", ws, s_w3, ms, vs, after)
        for which, part in ((GATE, "gate"), (UP, "up"), (DOWN, "down")):
            back = (lambda a: a[None]) if which == DOWN else (lambda a: a.T[None])
            big_out[f"w_{tag}_{part}"] = tuple(back(a) for a in (s_w3[which], deltas[which], new_ms[which], new_vs[which]))
        return deltas[DOWN]

    last = adamw_ffn("ffn2", s_w3_2, ffn2_w, ffn2_m, ffn2_v, (rs_ffn1_token, rs_mixer_token))
    s_in, s_gt, s_up, s_wo = reduce_scatter_4("rs_mixer", rs_mixer, last)
    grads = {"w_gate": s_gt[0].T, "w_up_a": s_up[0, :WIDTH], "w_up_b": s_up[0, WIDTH:], "w_out": s_wo[0]}
    weights = {"w_gate": (w_gate, m_w_gate, v_w_gate), "w_up_a": (w_up_a, m_w_up_a, v_w_up_a),
               "w_up_b": (w_up_b, m_w_up_b, v_w_up_b), "w_out": (w_out, m_w_out, v_w_out)}
    for wname, (w, m, v) in weights.items():
        g = grads[wname]
        delta, new_m, new_v = adamw(f"adamw_{wname}", w[0], g, m[0], v[0])
        big_out[wname] = (g[None], delta[None], new_m[None], new_v[None])
    rows_of = lambda a: jnp.transpose(a, (2, 0, 1))
    g_in_rows = s_in[0, :in4][:, None, :]
    in_rows = adamw_rows("adamw_w_in", rows_of(w_in), g_in_rows, rows_of(m_w_in), rows_of(v_w_in))
    big_out["w_in"] = tuple(jnp.transpose(a, (1, 2, 0)) for a in (g_in_rows, *in_rows))

    rs_ffn1, token = reduce_scatter_3("rs_ffn1", rs_ffn1, c_arr, in_rows[0])
    (s_w3_1,) = reduce_scatter_4("rs_ffn1", rs_ffn1, token)
    adamw_ffn("ffn1", s_w3_1, ffn1_w, ffn1_m, ffn1_v, ())

    order = ["norm_ffn1", "w_ffn1_gate", "w_ffn1_up", "w_ffn1_down", "norm_mix", "w_in", "b_forget", "w_gate", "b_gate",
             "w_up_a", "w_up_b", "w_out", "norm_ffn2", "w_ffn2_gate", "w_ffn2_up", "w_ffn2_down", "norm_final"]
    outs = [loss, dx0[None]]
    for kind in range(4):
        for wname in order:
            outs.append(big_out[wname][kind] if wname in big_out else small_out[kind][wname])
    return tuple(outs)
```

```python
import functools

import jax
import jax.numpy as jnp
from jax import lax
from jax.experimental import pallas as pl
from jax.experimental.pallas import tpu as pltpu

F32 = jnp.float32
BF16 = jnp.bfloat16

HEAD_DIM = 64
N_HEADS = 8
WIDTH = N_HEADS * HEAD_DIM
QKV_COLS = 6 * WIDTH
RMS_EPS = 1e-6
ATTN_SCALE = HEAD_DIM ** -0.5
N_CHIPS = 4
QB = 128
BQ = 1024
CS = 256
N_SUB = BQ // CS
NEG_BIG = -1e30

ADAM_LR = 0.001
ADAM_B1 = 0.9
ADAM_B2 = 0.999
ADAM_EPS = 1e-08
ADAM_WD = 0.01
ADAM_STEP = 10

VMEM_LIMIT_BYTES = 48 * 1024 * 1024
MESH = pl.DeviceIdType.MESH

NN = ((1,), (0,))
NT = ((1,), (1,))
TN = ((0,), (0,))


def _params(semantics):
    return pltpu.CompilerParams(dimension_semantics=semantics, vmem_limit_bytes=VMEM_LIMIT_BYTES)


def _dot(a, b, contract):
    return lax.dot_general(a.astype(BF16), b.astype(BF16), (contract, ((), ())), preferred_element_type=F32)


def _sigmoid(x):
    return 1.0 / (1.0 + jnp.exp(-x))


def _log1pexp_neg_abs(z):
    return jnp.log(1.0 + jnp.exp(-jnp.abs(z)))


def _split3(x):
    hi = x.astype(BF16)
    r1 = x - hi.astype(F32)
    mid = r1.astype(BF16)
    lo = (r1 - mid.astype(F32)).astype(BF16)
    return hi, mid, lo


def _dot_exact_lhs01(m01, x):
    hi, mid, lo = _split3(x)
    d = lambda p: lax.dot_general(m01, p, (NN, ((), ())), preferred_element_type=F32)
    return d(hi) + d(mid) + d(lo)


def _iota2(shape, dim):
    return lax.broadcasted_iota(jnp.int32, shape, dim)


def _mm(name, pairs, contract, grid, pair_specs, out_shape, out_specs, acc_shape, nk, epilogue,
        extras=(), extra_specs=(), semantics=None):
    n_pairs = len(pairs)
    n_extra = len(extras)
    n_out = len(out_shape)

    def body(*refs):
        ab = refs[:2 * n_pairs]
        ex = refs[2 * n_pairs:2 * n_pairs + n_extra]
        outs = refs[2 * n_pairs + n_extra:2 * n_pairs + n_extra + n_out]
        ids = [pl.program_id(i) for i in range(len(grid))]
        k = ids[-1]
        part = _dot(ab[0][...], ab[1][...], contract)
        for p in range(1, n_pairs):
            part += _dot(ab[2 * p][...], ab[2 * p + 1][...], contract)
        if nk == 1:
            epilogue(part, ex, outs, ids)
            return
        acc = refs[-1]

        @pl.when(k == 0)
        def _():
            acc[...] = part

        @pl.when(k != 0)
        def _():
            acc[...] += part

        @pl.when(k == nk - 1)
        def _():
            epilogue(acc[...], ex, outs, ids)

    operands = [t for pair in pairs for t in pair] + list(extras)
    in_specs = [s for pair in pair_specs for s in pair] + list(extra_specs)
    if semantics is None:
        semantics = ("parallel",) * (len(grid) - 1) + ("arbitrary",)
    return pl.pallas_call(
        body, name=name, grid=grid, in_specs=in_specs, out_specs=list(out_specs), out_shape=list(out_shape),
        scratch_shapes=[] if nk == 1 else [pltpu.VMEM(acc_shape, F32)], compiler_params=_params(semantics),
    )(*operands)


def _ordered_after(body, n_in, n_after):
    def wrapped(*refs):
        return body(*refs[:n_in], *refs[n_in + n_after:])
    return wrapped


def _row_tile(rows, target):
    t = min(rows, target)
    while rows % t:
        t //= 2
    return t


def rms_fwd(name, x, g, after=()):
    t, d = x.shape
    tr = _row_tile(t, 256)

    def body(x_ref, g_ref, h_ref, r_ref):
        xv = x_ref[...]
        r = lax.rsqrt(jnp.mean(xv * xv, axis=-1, keepdims=True) + RMS_EPS)
        h_ref[...] = (xv * r * g_ref[...]).astype(BF16)
        r_ref[...] = r

    return pl.pallas_call(
        _ordered_after(body, 2, len(after)), name=name, grid=(t // tr,),
        in_specs=[pl.BlockSpec((tr, d), lambda i: (i, 0)), pl.BlockSpec((1, d), lambda i: (0, 0))] + [_ANY] * len(after),
        out_specs=[pl.BlockSpec((tr, d), lambda i: (i, 0)), pl.BlockSpec((tr, 1), lambda i: (i, 0))],
        out_shape=[jax.ShapeDtypeStruct((t, d), BF16), jax.ShapeDtypeStruct((t, 1), F32)],
        compiler_params=_params(("parallel",)),
    )(x, g, *after)


GATE, UP, DOWN = 0, 1, 2


def _ffn_w_spec(which, f4, d, index_of_j):
    return pl.BlockSpec((None, None, f4, d), lambda *ids: (index_of_j(*ids), which, 0, 0))


def ffn_up(name, h, w3):
    t, d = h.shape
    ns, _, f4, _ = w3.shape
    tm = _row_tile(t, 512)

    def body(h_ref, wg_ref, wu_ref, a_ref, b_ref, s_ref):
        hv = h_ref[...]
        a = _dot(hv, wg_ref[...], NT)
        b = _dot(hv, wu_ref[...], NT)
        a_ref[...] = a.astype(BF16)
        b_ref[...] = b.astype(BF16)
        s_ref[...] = (a * _sigmoid(a) * b).astype(BF16)

    act_spec = pl.BlockSpec((None, tm, f4), lambda j, m: (j, m, 0))
    return pl.pallas_call(
        body, name=name, grid=(ns, t // tm),
        in_specs=[pl.BlockSpec((tm, d), lambda j, m: (m, 0)),
                  _ffn_w_spec(GATE, f4, d, lambda j, m: j), _ffn_w_spec(UP, f4, d, lambda j, m: j)],
        out_specs=[act_spec, act_spec, act_spec],
        out_shape=[jax.ShapeDtypeStruct((ns, t, f4), BF16)] * 3,
        compiler_params=_params(("parallel", "parallel")),
    )(h, w3, w3)


def mm_residual(name, s, w, w_spec, x, scale, after=()):
    nj, t, kdim = s.shape
    n = x.shape[1]
    tm = _row_tile(t, 512)

    def body(s_ref, w_ref, x_ref, o_ref):
        acc = _dot(s_ref[0], w_ref[0], NN)
        for j in range(1, nj):
            acc += _dot(s_ref[j], w_ref[j], NN)
        o_ref[...] = x_ref[...] + scale * acc

    row = pl.BlockSpec((tm, n), lambda m: (m, 0))
    return pl.pallas_call(
        _ordered_after(body, 3, len(after)), name=name, grid=(t // tm,),
        in_specs=[pl.BlockSpec((nj, tm, kdim), lambda m: (0, m, 0)), w_spec, row] + [_ANY] * len(after), out_specs=row,
        out_shape=jax.ShapeDtypeStruct((t, n), F32), compiler_params=_params(("parallel",)),
    )(s, w, x, *after)


def ffn_bwd_act(name, dx, w3, a, b, after=()):
    t, d = dx.shape
    ns, _, f4, _ = w3.shape
    tm = _row_tile(t, 512)

    def body(dx_ref, wd_ref, a_ref, b_ref, da_ref, db_ref):
        ds = _dot(0.5 * dx_ref[...], wd_ref[...], NT)
        av = a_ref[...].astype(F32)
        sig = _sigmoid(av)
        da_ref[...] = (ds * b_ref[...].astype(F32) * (sig * (1.0 + av * (1.0 - sig)))).astype(BF16)
        db_ref[...] = (ds * (av * sig)).astype(BF16)

    act_spec = pl.BlockSpec((None, tm, f4), lambda j, m: (j, m, 0))
    return pl.pallas_call(
        _ordered_after(body, 4, len(after)), name=name, grid=(ns, t // tm),
        in_specs=[pl.BlockSpec((tm, d), lambda j, m: (m, 0)), _ffn_w_spec(DOWN, f4, d, lambda j, m: j), act_spec, act_spec]
        + [_ANY] * len(after),
        out_specs=[act_spec, act_spec],
        out_shape=[jax.ShapeDtypeStruct((ns, t, f4), BF16)] * 2,
        compiler_params=_params(("parallel", "parallel")),
    )(dx, w3, a, b, *after)


def ffn_wgrad(name, h, da, db, s, dx):
    t, d = h.shape
    ns, _, f4 = da.shape
    tk = _row_tile(t, 1024)
    nk = t // tk

    def body(h_ref, da_ref, db_ref, s_ref, dx_ref, o_ref, acc):
        k = pl.program_id(1)

        @pl.when(k == 0)
        def _():
            acc[...] = jnp.zeros_like(acc)

        hv = h_ref[...]
        acc[GATE] += _dot(da_ref[...], hv, TN)
        acc[UP] += _dot(db_ref[...], hv, TN)
        acc[DOWN] += _dot(s_ref[...], 0.5 * dx_ref[...], TN)

        @pl.when(k == nk - 1)
        def _():
            o_ref[...] = acc[...].astype(BF16)

    act_spec = pl.BlockSpec((None, tk, f4), lambda j, k: (j, k, 0))
    row_spec = pl.BlockSpec((tk, d), lambda j, k: (k, 0))
    return pl.pallas_call(
        body, name=name, grid=(ns, nk),
        in_specs=[row_spec, act_spec, act_spec, act_spec, row_spec],
        out_specs=pl.BlockSpec((None, 3, f4, d), lambda j, k: (j, 0, 0, 0)),
        out_shape=jax.ShapeDtypeStruct((ns, 3, f4, d), BF16),
        scratch_shapes=[pltpu.VMEM((3, f4, d), F32)],
        compiler_params=_params(("parallel", "arbitrary")),
    )(h, da, db, s, dx)


def _rms_bwd_tail(dh, x_ref, r_ref, g_ref, dxin_ref, dx_ref, gn_ref, row_tile_index):
    r = r_ref[...]
    xhat = x_ref[...] * r
    dhg = dh * g_ref[...]
    dx_ref[...] = dxin_ref[...] + r * (dhg - xhat * jnp.mean(dhg * xhat, axis=-1, keepdims=True))
    part = jnp.sum(dh * xhat, axis=0, keepdims=True)

    @pl.when(row_tile_index == 0)
    def _():
        gn_ref[...] = part

    @pl.when(row_tile_index != 0)
    def _():
        gn_ref[...] += part


def ffn_dh(name, da, db, w3, x, rstd, g, dx_in, after=()):
    ns, t, f4 = da.shape
    d = x.shape[1]
    tm = _row_tile(t, 256)

    def body(da_ref, db_ref, wg_ref, wu_ref, x_ref, r_ref, g_ref, dxin_ref, dx_ref, gn_ref):
        dh = _dot(da_ref[0], wg_ref[0], NN) + _dot(db_ref[0], wu_ref[0], NN)
        for j in range(1, ns):
            dh += _dot(da_ref[j], wg_ref[j], NN) + _dot(db_ref[j], wu_ref[j], NN)
        _rms_bwd_tail(dh, x_ref, r_ref, g_ref, dxin_ref, dx_ref, gn_ref, pl.program_id(0))

    act = pl.BlockSpec((ns, tm, f4), lambda m: (0, m, 0))
    row = pl.BlockSpec((tm, d), lambda m: (m, 0))
    gain = pl.BlockSpec((1, d), lambda m: (0, 0))
    return pl.pallas_call(
        _ordered_after(body, 8, len(after)), name=name, grid=(t // tm,),
        in_specs=[act, act, pl.BlockSpec((ns, None, f4, d), lambda m: (0, GATE, 0, 0)),
                  pl.BlockSpec((ns, None, f4, d), lambda m: (0, UP, 0, 0)), row, pl.BlockSpec((tm, 1), lambda m: (m, 0)), gain, row]
        + [_ANY] * len(after),
        out_specs=[row, gain], out_shape=[jax.ShapeDtypeStruct((t, d), F32), jax.ShapeDtypeStruct((1, d), F32)],
        compiler_params=_params(("arbitrary",)),
    )(da, db, w3, w3, x, rstd, g, dx_in, *after)


def mixer_dh(name, parts, x, rstd, g, dx_in, after=()):
    t, d = x.shape
    n = len(parts)
    tm = _row_tile(t, 256)

    def body(*refs):
        dh = _dot(refs[0][...], refs[n][...], NN)
        for i in range(1, n):
            dh += _dot(refs[i][...], refs[n + i][...], NN)
        _rms_bwd_tail(dh, *refs[2 * n:2 * n + 6], pl.program_id(0))

    row = pl.BlockSpec((tm, d), lambda m: (m, 0))
    gain = pl.BlockSpec((1, d), lambda m: (0, 0))
    act_specs = [pl.BlockSpec((tm, a.shape[1]), lambda m: (m, 0)) for a, _, _ in parts]
    w_specs = [pl.BlockSpec((a.shape[1], d), functools.partial(lambda m, blk: (blk, 0), blk=blk)) for a, _, blk in parts]
    return pl.pallas_call(
        _ordered_after(body, 2 * n + 4, len(after)), name=name, grid=(t // tm,),
        in_specs=act_specs + w_specs + [row, pl.BlockSpec((tm, 1), lambda m: (m, 0)), gain, row] + [_ANY] * len(after),
        out_specs=[row, gain], out_shape=[jax.ShapeDtypeStruct((t, d), F32), jax.ShapeDtypeStruct((1, d), F32)],
        compiler_params=_params(("arbitrary",)),
    )(*[a for a, _, _ in parts], *[w for _, w, _ in parts], x, rstd, g, dx_in, *after)


def ffn_forward(tag, x, g_norm, w3, normed=None, between=None):
    _, _, f4, d = w3.shape
    h, rstd = normed if normed is not None else rms_fwd(f"{tag}_rms", x, g_norm)
    a, b, s = ffn_up(f"{tag}_up", h, w3)
    after = between(s) if between is not None else ()
    x_out = mm_residual(f"{tag}_down", s, w3, pl.BlockSpec((w3.shape[0], None, f4, d), lambda m: (0, DOWN, 0, 0)), x, 0.5, after)
    return x_out, (x, h, rstd, a, b, s)


def ffn_backward_weights(tag, dx, saved, w3, after=()):
    x, h, rstd, a, b, s = saved
    da, db = ffn_bwd_act(f"{tag}_bwd_act", dx, w3, a, b, after)
    return ffn_wgrad(f"{tag}_wgrad", h, da, db, s, dx), (da, db)


def ffn_backward_input(tag, dx, saved, dab, g_norm, w3, after=()):
    x, h, rstd, a, b, s = saved
    return ffn_dh(f"{tag}_dh", dab[0], dab[1], w3, x, rstd, g_norm, dx, after)


def proj(name, h, wcat_t, bias, first_col, n_cols, tn, out_dtype, scaled_tiles=()):
    t, d = h.shape
    tm = _row_tile(t, 512)
    off = first_col // tn

    def epilogue(acc, ex, outs, ids):
        val = acc + ex[0][...]
        if scaled_tiles:
            hit = functools.reduce(jnp.logical_or, [ids[0] == s for s in scaled_tiles])
            val = val * jnp.where(hit, ATTN_SCALE, 1.0)
        outs[0][...] = val.astype(out_dtype)

    return _mm(
        name, [(h, wcat_t)], NT, (n_cols // tn, t // tm, 1),
        [(pl.BlockSpec((tm, d), lambda j, m, k: (m, 0)), pl.BlockSpec((tn, d), lambda j, m, k: (off + j, 0)))],
        [jax.ShapeDtypeStruct((t, n_cols), out_dtype)], [pl.BlockSpec((tm, tn), lambda j, m, k: (m, j))], (tm, tn), 1, epilogue,
        extras=[bias], extra_specs=[pl.BlockSpec((1, tn), lambda j, m, k: (0, off + j))],
    )[0]


def mix_fwd(name, ya, yb, wup, pc):
    t, w = ya.shape
    ns, _, tn = wup.shape
    d = ns * tn
    tm = _row_tile(t, 512)

    def body(ya_ref, yb_ref, wa_ref, wb_ref, pa_ref, pb_ref, ua_ref, ub_ref, mx_ref):
        ua = _dot(ya_ref[...], wa_ref[...], NN)
        ub = _dot(yb_ref[...], wb_ref[...], NN)
        ua_ref[...] = ua
        ub_ref[...] = ub
        mx_ref[...] = (_sigmoid(pa_ref[...]) * ua + _sigmoid(pb_ref[...]) * ub).astype(BF16)

    y_spec = pl.BlockSpec((tm, w), lambda m, n: (m, 0))
    o_spec = pl.BlockSpec((tm, tn), lambda m, n: (m, n))
    return pl.pallas_call(
        body, name=name, grid=(t // tm, ns),
        in_specs=[y_spec, y_spec, pl.BlockSpec((None, w, tn), lambda m, n: (n, 0, 0)), pl.BlockSpec((None, w, tn), lambda m, n: (n, 1, 0)),
                  o_spec, pl.BlockSpec((tm, tn), lambda m, n: (m, ns + n))],
        out_specs=[o_spec, o_spec, o_spec],
        out_shape=[jax.ShapeDtypeStruct((t, d), F32), jax.ShapeDtypeStruct((t, d), F32), jax.ShapeDtypeStruct((t, d), BF16)],
        compiler_params=_params(("parallel", "parallel")),
    )(ya, yb, wup, wup, pc, pc)


def up_bwd(name, du, wup, branch):
    t, d = du.shape
    ns, w2, tn = wup.shape
    w = w2 // 2
    tm = _row_tile(t, 512)

    def body(du_ref, w_ref, o_ref):
        acc = _dot(du_ref[:, 0:tn], w_ref[0], NT)
        for j in range(1, ns):
            acc += _dot(du_ref[:, j * tn:(j + 1) * tn], w_ref[j], NT)
        o_ref[...] = acc.astype(BF16)

    return pl.pallas_call(
        body, name=name, grid=(t // tm,),
        in_specs=[pl.BlockSpec((tm, d), lambda m: (m, 0)), pl.BlockSpec((ns, w, tn), lambda m: (0, branch, 0))],
        out_specs=pl.BlockSpec((tm, w), lambda m: (m, 0)), out_shape=jax.ShapeDtypeStruct((t, w), BF16),
        compiler_params=_params(("parallel",)),
    )(du, wup)


def up_wgrad(name, ya, yb, dua, dub):
    t, w = ya.shape
    d = dua.shape[1]
    tn = d // N_CHIPS

    def body(ya_ref, yb_ref, dua_ref, dub_ref, o_ref):
        o_ref[0:w, :] = _dot(ya_ref[...], dua_ref[...], TN).astype(BF16)
        o_ref[w:2 * w, :] = _dot(yb_ref[...], dub_ref[...], TN).astype(BF16)

    y_spec = pl.BlockSpec((t, w), lambda j: (0, 0))
    du_spec = pl.BlockSpec((t, tn), lambda j: (0, j))
    return pl.pallas_call(
        body, name=name, grid=(N_CHIPS,), in_specs=[y_spec, y_spec, du_spec, du_spec],
        out_specs=pl.BlockSpec((None, 2 * w, tn), lambda j: (j, 0, 0)),
        out_shape=jax.ShapeDtypeStruct((N_CHIPS, 2 * w, tn), BF16), compiler_params=_params(("parallel",)),
    )(ya, yb, dua, dub)


def mix_bwd(name, dx, wo, pc, ua, ub, after=()):
    t, d = dx.shape
    tm = _row_tile(t, 512)
    tn = 512
    off_a = 0
    off_b = d // tn

    def body(dx_ref, wo_ref, pa_ref, pb_ref, ua_ref, ub_ref, dua_ref, dub_ref, dpa_ref, dpb_ref, ba_ref, bb_ref):
        dm = _dot(dx_ref[...], wo_ref[...], NT)
        ga = _sigmoid(pa_ref[...])
        gb = _sigmoid(pb_ref[...])
        dua_ref[...] = (dm * ga).astype(BF16)
        dub_ref[...] = (dm * gb).astype(BF16)
        dpa = dm * ua_ref[...] * ga * (1.0 - ga)
        dpb = dm * ub_ref[...] * gb * (1.0 - gb)
        dpa_ref[...] = dpa.astype(BF16)
        dpb_ref[...] = dpb.astype(BF16)
        sa = jnp.sum(dpa, axis=0, keepdims=True)
        sb = jnp.sum(dpb, axis=0, keepdims=True)

        @pl.when(pl.program_id(1) == 0)
        def _():
            ba_ref[...] = sa
            bb_ref[...] = sb

        @pl.when(pl.program_id(1) != 0)
        def _():
            ba_ref[...] += sa
            bb_ref[...] += sb

    tile = pl.BlockSpec((tm, tn), lambda n, m: (m, n))
    bias = pl.BlockSpec((1, tn), lambda n, m: (0, n))
    return pl.pallas_call(
        _ordered_after(body, 6, len(after)), name=name, grid=(d // tn, t // tm),
        in_specs=[pl.BlockSpec((tm, d), lambda n, m: (m, 0)), pl.BlockSpec((tn, d), lambda n, m: (n, 0)),
                  pl.BlockSpec((tm, tn), lambda n, m: (m, off_a + n)), pl.BlockSpec((tm, tn), lambda n, m: (m, off_b + n)),
                  tile, tile] + [_ANY] * len(after),
        out_specs=[tile, tile, tile, tile, bias, bias],
        out_shape=[jax.ShapeDtypeStruct((t, d), BF16)] * 4 + [jax.ShapeDtypeStruct((1, d), F32)] * 2,
        compiler_params=_params(("parallel", "arbitrary")),
    )(dx, wo, pc, pc, ua, ub, *after)


def mm_plain(name, a, b, contract, out_dtype, tk_target=512):
    if contract == NN:
        m, kdim = a.shape
        n = b.shape[1]
    elif contract == NT:
        m, kdim = a.shape
        n = b.shape[0]
    else:
        kdim, m = a.shape
        n = b.shape[1]
    tm = _row_tile(m, 512)
    tk = _row_tile(kdim, tk_target)
    nk = kdim // tk
    if contract == TN:
        a_spec = pl.BlockSpec((tk, tm), lambda i, k: (k, i))
    else:
        a_spec = pl.BlockSpec((tm, tk), lambda i, k: (i, k))
    if contract == NT:
        b_spec = pl.BlockSpec((n, tk), lambda i, k: (0, k))
    else:
        b_spec = pl.BlockSpec((tk, n), lambda i, k: (k, 0))

    def epilogue(acc, ex, outs, ids):
        outs[0][...] = acc.astype(out_dtype)

    return _mm(name, [(a, b)], contract, (m // tm, nk), [(a_spec, b_spec)],
               [jax.ShapeDtypeStruct((m, n), out_dtype)], [pl.BlockSpec((tm, n), lambda i, k: (i, 0))], (tm, n), nk, epilogue)[0]


def wgrad_cat(name, h, dcat):
    t, d = h.shape
    n = dcat.shape[1]
    tn = next(c for c in (768, 512, 256, 128) if n % c == 0)
    tk = _row_tile(t, 2048)

    def epilogue(acc, ex, outs, ids):
        outs[0][...] = acc.astype(BF16)

    return _mm(
        name, [(dcat, h)], TN, (n // tn, t // tk),
        [(pl.BlockSpec((tk, tn), lambda j, k: (k, j)), pl.BlockSpec((tk, d), lambda j, k: (k, 0)))],
        [jax.ShapeDtypeStruct((n, d), BF16)], [pl.BlockSpec((tn, d), lambda j, k: (j, 0))], (tn, d), t // tk, epilogue,
    )[0]


def fox_prep(name, f, bias):
    t, lanes = f.shape
    nchunk = t // QB

    def body(f_ref, b_ref, c_ref):
        lower = (_iota2((QB, QB), 1) <= _iota2((QB, QB), 0)).astype(BF16)

        def chunk(n, carry):
            rows = pl.ds(pl.multiple_of(n * QB, QB), QB)
            u = f_ref[rows, :] + b_ref[...]
            lf = jnp.minimum(u, 0.0) - _log1pexp_neg_abs(u)
            c = _dot_exact_lhs01(lower, lf) + carry
            c_ref[rows, :] = c
            return c[QB - 1:QB, :]

        lax.fori_loop(0, nchunk, chunk, jnp.zeros((1, lanes), F32))

    return pl.pallas_call(body, name=name, out_shape=jax.ShapeDtypeStruct((t, lanes), F32),
                          compiler_params=pltpu.CompilerParams(vmem_limit_bytes=VMEM_LIMIT_BYTES))(f, bias)


def fox_gate_bwd(name, dc, f, bias):
    t, lanes = dc.shape
    nchunk = t // QB

    def body(dc_ref, f_ref, b_ref, df_ref, gb_ref):
        upper = (_iota2((QB, QB), 1) >= _iota2((QB, QB), 0)).astype(BF16)

        def chunk(n, carry):
            tail, total = carry
            rows = pl.ds(pl.multiple_of((nchunk - 1 - n) * QB, QB), QB)
            dlf = _dot_exact_lhs01(upper, dc_ref[rows, :]) + tail
            u = f_ref[rows, :] + b_ref[...]
            df = dlf * jnp.exp(jnp.minimum(-u, 0.0) - _log1pexp_neg_abs(u))
            df_ref[rows, :] = df
            return dlf[0:1, :], total + jnp.sum(df, axis=0, keepdims=True)

        zero = jnp.zeros((1, lanes), F32)
        _, total = lax.fori_loop(0, nchunk, chunk, (zero, zero))
        gb_ref[...] = total

    return pl.pallas_call(body, name=name,
                          out_shape=[jax.ShapeDtypeStruct((t, lanes), F32), jax.ShapeDtypeStruct((1, lanes), F32)],
                          compiler_params=pltpu.CompilerParams(vmem_limit_bytes=VMEM_LIMIT_BYTES))(dc, f, bias)


def _qrows(i):
    return pl.ds(pl.multiple_of(i * BQ, BQ), BQ)


def _krows(kc):
    return pl.ds(pl.multiple_of(kc * CS, CS), CS)


def _dot_split2_rhs01(x, m01):
    hi = x.astype(BF16)
    lo = (x - hi.astype(F32)).astype(BF16)
    d = lambda p: lax.dot_general(p, m01, (NN, ((), ())), preferred_element_type=F32)
    return d(hi) + d(lo)


def _live_rows(dchunk):
    return 0 if dchunk is None else dchunk * CS


def _diag_mask(dchunk, inclusive):
    shape = (BQ - _live_rows(dchunk), CS)
    return _iota2(shape, 1) <= _iota2(shape, 0) if inclusive else _iota2(shape, 1) < _iota2(shape, 0)


def _tail(x, r0, axis=0):
    return x if r0 == 0 else (x[r0:] if axis == 0 else x[:, r0:])


def _with_tail(old, tail, r0):
    return tail if r0 == 0 else jnp.concatenate([old[:r0], tail], axis=0)


def _walk_chunks(i, step, init, right_to_left):
    order = list(reversed(range(N_SUB))) if right_to_left else list(range(N_SUB))

    def diagonal(state):
        for dchunk in order:
            state = step(i * N_SUB + dchunk, state, dchunk)
        return state

    def group(n, state):
        base = ((i - 1 - n) if right_to_left else n) * N_SUB
        for dchunk in order:
            state = step(base + dchunk, state, None)
        return state

    if right_to_left:
        return lax.fori_loop(0, i, group, diagonal(init))
    return diagonal(lax.fori_loop(0, i, group, init))


PAIR = 2 * HEAD_DIM
N_PAIRS = N_HEADS // 2


def _pair_spec(t, first_block):
    return pl.BlockSpec((t, PAIR), lambda p, *_: (0, first_block + p))


def _head_lanes(shape):
    lane = _iota2(shape, len(shape) - 1)
    return [lane < HEAD_DIM, lane >= HEAD_DIM]


def _only_head(x, lanes_of_head):
    return jnp.where(lanes_of_head, x, jnp.zeros_like(x))


LOG2_E = 1.4426950408889634


def _sb_chunk_weights(q_h, k, later, carry, dchunk):
    z = _dot(q_h, k, NT) * LOG2_E
    lnb = -jnp.maximum(z, 0.0) - jnp.log2(1.0 + jnp.exp2(-jnp.abs(z)))
    lsz = lnb + z
    if dchunk is not None:
        lnb = jnp.where(_diag_mask(dchunk, False), lnb, 0.0)
    w = jnp.exp2(lsz + _dot_split2_rhs01(lnb, later) + carry)
    if dchunk is not None:
        w = jnp.where(_diag_mask(dchunk, False), w, 0.0)
    return w, lsz, lnb


def sb_pair_fwd(name, qkv):
    t = qkv.shape[0]

    def body(q_ref, k_ref, v_ref, o_ref):
        later = (_iota2((CS, CS), 0) > _iota2((CS, CS), 1)).astype(BF16)
        lanes = _head_lanes((BQ, PAIR))

        def qblock(i, _):
            q = q_ref[_qrows(i), :]
            q_heads = [_only_head(q, lanes[h]) for h in range(2)]

            def step(kc, state, dchunk):
                k = k_ref[_krows(kc), :]
                v = v_ref[_krows(kc), :]
                out = []
                r0 = _live_rows(dchunk)
                for h in range(2):
                    carry, acc = state[h]
                    w, _, lnb = _sb_chunk_weights(_tail(q_heads[h], r0), k, later, _tail(carry, r0), dchunk)
                    out.append((_with_tail(carry, _tail(carry, r0) + jnp.sum(lnb, axis=1, keepdims=True), r0),
                                _with_tail(acc, _tail(acc, r0) + _dot(w, v, NN), r0)))
                return tuple(out)

            zero = (jnp.zeros((BQ, 1), F32), jnp.zeros((BQ, PAIR), F32))
            (_, acc0), (_, acc1) = _walk_chunks(i, step, (zero, zero), True)
            o_ref[_qrows(i), :] = jnp.where(lanes[0], acc0, acc1).astype(BF16)
            return 0

        lax.fori_loop(0, t // BQ, qblock, 0)

    return pl.pallas_call(
        body, name=name, grid=(N_PAIRS,),
        in_specs=[_pair_spec(t, 0), _pair_spec(t, N_PAIRS), _pair_spec(t, 2 * N_PAIRS)],
        out_specs=_pair_spec(t, 0), out_shape=jax.ShapeDtypeStruct((t, WIDTH), BF16),
        compiler_params=_params(("parallel",)),
    )(qkv, qkv, qkv)


def _emit_dqkv(res, o_ref):
    o_ref[...] = res[pl.program_id(1)]


def _flush_transposed(acc, res, which):
    for kc in range(acc.shape[0]):
        res[which, kc * CS:(kc + 1) * CS, :] = acc[kc].T.astype(BF16)


def sb_pair_bwd(name, qkv, dy, dqkv):
    t = qkv.shape[0]
    nc = t // CS

    def body(q_ref, k_ref, v_ref, do_ref, _, o_ref, g_s, b_s, dkt_acc, dvt_acc, res):
        @pl.when(pl.program_id(1) == 0)
        def _():
            later = (_iota2((CS, CS), 0) > _iota2((CS, CS), 1)).astype(BF16)
            earlier = (_iota2((CS, CS), 0) < _iota2((CS, CS), 1)).astype(BF16)
            lanes = _head_lanes((BQ, PAIR))
            dkt_acc[...] = jnp.zeros_like(dkt_acc)
            dvt_acc[...] = jnp.zeros_like(dvt_acc)

            def qblock(i, _):
                q = q_ref[_qrows(i), :]
                do = do_ref[_qrows(i), :]
                q_heads = [_only_head(q, lanes[h]) for h in range(2)]
                do_heads = [_only_head(do, lanes[h]) for h in range(2)]
                qt_heads = [qh.astype(F32).T.astype(BF16) for qh in q_heads]
                dot_heads = [dh.astype(F32).T.astype(BF16) for dh in do_heads]

                def step1(kc, carries, dchunk):
                    k = k_ref[_krows(kc), :]
                    v = v_ref[_krows(kc), :]
                    out = []
                    r0 = _live_rows(dchunk)
                    for h in range(2):
                        carry = _tail(carries[h], r0)
                        w, lsz, lnb = _sb_chunk_weights(_tail(q_heads[h], r0), k, later, carry, dchunk)
                        g_s[h, kc, r0:, :] = (w * _dot(_tail(do_heads[h], r0), v, NT)).astype(BF16)
                        b_s[h, kc, r0:, :] = jnp.exp2(lsz).astype(BF16)
                        dvt_acc[kc] += _dot(_tail(dot_heads[h], r0, axis=1), w, NN)
                        out.append(_with_tail(carries[h], carry + jnp.sum(lnb, axis=1, keepdims=True), r0))
                    return tuple(out)

                zero = jnp.zeros((BQ, 1), F32)
                _walk_chunks(i, step1, (zero, zero), True)

                def step2(kc, state, dchunk):
                    k = k_ref[_krows(kc), :]
                    out = []
                    r0 = _live_rows(dchunk)
                    for h in range(2):
                        before, dq = state[h]
                        g16 = g_s[h, kc, r0:, :]
                        g = g16.astype(F32)
                        beta = b_s[h, kc, r0:, :].astype(F32)
                        prefix = lax.dot_general(g16, earlier, (NN, ((), ())), preferred_element_type=F32) + _tail(before, r0)
                        dz = g * (1.0 - beta) - beta * prefix
                        if dchunk is not None:
                            dz = jnp.where(_diag_mask(dchunk, False), dz, 0.0)
                        dzb = dz.astype(BF16)
                        dkt_acc[kc] += _dot(_tail(qt_heads[h], r0, axis=1), dzb, NN)
                        out.append((_with_tail(before, _tail(before, r0) + jnp.sum(g, axis=1, keepdims=True), r0),
                                    _with_tail(dq, _tail(dq, r0) + _dot(dzb, k, NN), r0)))
                    return tuple(out)

                start = (zero, jnp.zeros((BQ, PAIR), F32))
                (_, dq0), (_, dq1) = _walk_chunks(i, step2, (start, start), False)
                res[0, _qrows(i), :] = (jnp.where(lanes[0], dq0, dq1) * ATTN_SCALE).astype(BF16)
                return 0

            lax.fori_loop(0, t // BQ, qblock, 0)
            _flush_transposed(dkt_acc, res, 1)
            _flush_transposed(dvt_acc, res, 2)

        _emit_dqkv(res, o_ref)

    return pl.pallas_call(
        body, name=name, grid=(N_PAIRS, 3),
        in_specs=[_pair_spec(t, 0), _pair_spec(t, N_PAIRS), _pair_spec(t, 2 * N_PAIRS), _pair_spec(t, 0), _ANY],
        out_specs=pl.BlockSpec((t, PAIR), lambda p, s: (0, s * N_PAIRS + p)),
        out_shape=jax.ShapeDtypeStruct(dqkv.shape, BF16), input_output_aliases={4: 0},
        scratch_shapes=[pltpu.VMEM((2, nc, BQ, CS), BF16), pltpu.VMEM((2, nc, BQ, CS), BF16),
                        pltpu.VMEM((nc, PAIR, CS), F32), pltpu.VMEM((nc, PAIR, CS), F32), pltpu.VMEM((3, t, PAIR), BF16)],
        compiler_params=_params(("parallel", "arbitrary")),
    )(qkv, qkv, qkv, dy, dqkv)


def _gates_col_spec(t):
    return pl.BlockSpec((2, t, 1), lambda p, *_: (p, 0, 0))


def _gates_row_spec(nc):
    return pl.BlockSpec((2, nc, 1, CS), lambda p, *_: (p, 0, 0, 0))


def fox_pair_fwd(name, qkv, c_col, c_row):
    t = qkv.shape[0]

    def body(q_ref, k_ref, v_ref, cc_ref, cr_ref, o_ref, lse_ref):
        lanes = _head_lanes((BQ, PAIR))

        def qblock(i, _):
            q = q_ref[_qrows(i), :]
            q_heads = [_only_head(q, lanes[h]) for h in range(2)]
            ct = [cc_ref[h, _qrows(i), :] for h in range(2)]

            def step(kc, state, dchunk):
                k = k_ref[_krows(kc), :]
                v = v_ref[_krows(kc), :]
                out = []
                r0 = _live_rows(dchunk)
                for h in range(2):
                    m, l, acc = (_tail(a, r0) for a in state[h])
                    s = _dot(_tail(q_heads[h], r0), k, NT) + _tail(ct[h], r0) - cr_ref[h, kc]
                    if dchunk is not None:
                        s = jnp.where(_diag_mask(dchunk, True), s, NEG_BIG)
                    m_new = jnp.maximum(m, jnp.max(s, axis=1, keepdims=True))
                    alpha = jnp.exp(m - m_new)
                    p = jnp.exp(s - m_new)
                    if dchunk is not None:
                        p = jnp.where(_diag_mask(dchunk, True), p, 0.0)
                    new = (m_new, l * alpha + jnp.sum(p, axis=1, keepdims=True), acc * alpha + _dot(p, v, NN))
                    out.append(tuple(_with_tail(old, tail, r0) for old, tail in zip(state[h], new)))
                return tuple(out)

            init = (jnp.full((BQ, 1), NEG_BIG, F32), jnp.zeros((BQ, 1), F32), jnp.zeros((BQ, PAIR), F32))
            (m0, l0, acc0), (m1, l1, acc1) = _walk_chunks(i, step, (init, init), False)
            o_ref[_qrows(i), :] = jnp.where(lanes[0], acc0 / l0, acc1 / l1).astype(BF16)
            lse_ref[0, _qrows(i), :] = m0 + jnp.log(l0)
            lse_ref[1, _qrows(i), :] = m1 + jnp.log(l1)
            return 0

        lax.fori_loop(0, t // BQ, qblock, 0)

    return pl.pallas_call(
        body, name=name, grid=(N_PAIRS,),
        in_specs=[_pair_spec(t, 3 * N_PAIRS), _pair_spec(t, 4 * N_PAIRS), _pair_spec(t, 5 * N_PAIRS),
                  _gates_col_spec(t), _gates_row_spec(t // CS)],
        out_specs=[_pair_spec(t, 0), _gates_col_spec(t)],
        out_shape=[jax.ShapeDtypeStruct((t, WIDTH), BF16), jax.ShapeDtypeStruct((N_HEADS, t, 1), F32)],
        compiler_params=_params(("parallel",)),
    )(qkv, qkv, qkv, c_col, c_row)


def fox_pair_bwd(name, qkv, y, dy, lse, c_col, c_row, dqkv):
    t = qkv.shape[0]
    nc = t // CS

    def body(q_ref, k_ref, v_ref, o_in_ref, do_ref, lse_ref, cc_ref, cr_ref, _, o_ref, dcc_ref, dcr_ref,
             dkt_acc, dvt_acc, dcr_acc, res):
        @pl.when(pl.program_id(1) == 0)
        def _():
            lanes = _head_lanes((BQ, PAIR))
            dkt_acc[...] = jnp.zeros_like(dkt_acc)
            dvt_acc[...] = jnp.zeros_like(dvt_acc)
            dcr_acc[...] = jnp.zeros_like(dcr_acc)

            def qblock(i, _):
                q = q_ref[_qrows(i), :]
                do = do_ref[_qrows(i), :]
                q_heads = [_only_head(q, lanes[h]) for h in range(2)]
                do_heads = [_only_head(do, lanes[h]) for h in range(2)]
                qt_heads = [qh.astype(F32).T.astype(BF16) for qh in q_heads]
                dot_heads = [dh.astype(F32).T.astype(BF16) for dh in do_heads]
                prod = do.astype(F32) * o_in_ref[_qrows(i), :].astype(F32)
                delta = [jnp.sum(_only_head(prod, lanes[h]), axis=1, keepdims=True) for h in range(2)]
                ct = [cc_ref[h, _qrows(i), :] for h in range(2)]
                lse_i = [lse_ref[h, _qrows(i), :] for h in range(2)]

                def step(kc, state, dchunk):
                    k = k_ref[_krows(kc), :]
                    v = v_ref[_krows(kc), :]
                    out = []
                    r0 = _live_rows(dchunk)
                    for h in range(2):
                        dq, dct = state[h]
                        s = _dot(_tail(q_heads[h], r0), k, NT) + _tail(ct[h], r0) - cr_ref[h, kc]
                        p = jnp.exp(s - _tail(lse_i[h], r0))
                        if dchunk is not None:
                            p = jnp.where(_diag_mask(dchunk, True), p, 0.0)
                        ds = p * (_dot(_tail(do_heads[h], r0), v, NT) - _tail(delta[h], r0))
                        dvt_acc[kc] += _dot(_tail(dot_heads[h], r0, axis=1), p, NN)
                        dsb = ds.astype(BF16)
                        dkt_acc[kc] += _dot(_tail(qt_heads[h], r0, axis=1), dsb, NN)
                        dcr_acc[h, kc] -= jnp.sum(ds, axis=0, keepdims=True)
                        out.append((_with_tail(dq, _tail(dq, r0) + _dot(dsb, k, NN), r0),
                                    _with_tail(dct, _tail(dct, r0) + jnp.sum(ds, axis=1, keepdims=True), r0)))
                    return tuple(out)

                zero = (jnp.zeros((BQ, PAIR), F32), jnp.zeros((BQ, 1), F32))
                (dq0, dct0), (dq1, dct1) = _walk_chunks(i, step, (zero, zero), False)
                res[0, _qrows(i), :] = (jnp.where(lanes[0], dq0, dq1) * ATTN_SCALE).astype(BF16)
                lane = _iota2((BQ, PAIR), 1)
                dcc_ref[_qrows(i), :] = jnp.where(lane == 0, dct0, jnp.where(lane == 1, dct1, 0.0))
                return 0

            lax.fori_loop(0, t // BQ, qblock, 0)
            _flush_transposed(dkt_acc, res, 1)
            _flush_transposed(dvt_acc, res, 2)
            dcr_ref[...] = dcr_acc[...]

        _emit_dqkv(res, o_ref)

    return pl.pallas_call(
        body, name=name, grid=(N_PAIRS, 3),
        in_specs=[_pair_spec(t, 3 * N_PAIRS), _pair_spec(t, 4 * N_PAIRS), _pair_spec(t, 5 * N_PAIRS), _pair_spec(t, 0),
                  _pair_spec(t, 0), _gates_col_spec(t), _gates_col_spec(t), _gates_row_spec(nc), _ANY],
        out_specs=[pl.BlockSpec((t, PAIR), lambda p, s: (0, (3 + s) * N_PAIRS + p)),
                   pl.BlockSpec((None, t, PAIR), lambda p, s: (p, 0, 0)), _gates_row_spec(nc)],
        out_shape=[jax.ShapeDtypeStruct(dqkv.shape, BF16), jax.ShapeDtypeStruct((N_PAIRS, t, PAIR), F32),
                   jax.ShapeDtypeStruct((N_HEADS, nc, 1, CS), F32)],
        input_output_aliases={8: 0},
        scratch_shapes=[pltpu.VMEM((nc, PAIR, CS), F32), pltpu.VMEM((nc, PAIR, CS), F32), pltpu.VMEM((2, nc, 1, CS), F32),
                        pltpu.VMEM((3, t, PAIR), BF16)],
        compiler_params=_params(("parallel", "arbitrary")),
    )(qkv, qkv, qkv, y, dy, lse, c_col, c_row, dqkv)


def loss_head(name, x, g, target):
    t, d = x.shape
    tr = _row_tile(t, 256)

    def body(x_ref, g_ref, t_ref, dx_ref, gn_ref, loss_ref):
        xv = x_ref[...]
        r = lax.rsqrt(jnp.mean(xv * xv, axis=-1, keepdims=True) + RMS_EPS)
        xhat = xv * r
        gv = g_ref[...]
        err = xhat * gv - t_ref[...]
        part_loss = 0.5 * jnp.sum(jnp.mean(err * err, axis=-1, keepdims=True), axis=0, keepdims=True)
        dy = err * (1.0 / d)
        dyg = dy * gv
        dx_ref[...] = r * (dyg - xhat * jnp.mean(dyg * xhat, axis=-1, keepdims=True))
        part_g = jnp.sum(dy * xhat, axis=0, keepdims=True)

        @pl.when(pl.program_id(0) == 0)
        def _():
            gn_ref[...] = part_g
            loss_ref[...] = part_loss

        @pl.when(pl.program_id(0) != 0)
        def _():
            gn_ref[...] += part_g
            loss_ref[...] += part_loss

    row = pl.BlockSpec((tr, d), lambda i: (i, 0))
    return pl.pallas_call(
        body, name=name, grid=(t // tr,),
        in_specs=[row, pl.BlockSpec((1, d), lambda i: (0, 0)), row],
        out_specs=[row, pl.BlockSpec((1, d), lambda i: (0, 0)), pl.BlockSpec((1, 1), lambda i: (0, 0))],
        out_shape=[jax.ShapeDtypeStruct((t, d), F32), jax.ShapeDtypeStruct((1, d), F32), jax.ShapeDtypeStruct((1, 1), F32)],
        compiler_params=_params(("arbitrary",)),
    )(x, g, target)


def _place():
    return lax.axis_index("x"), lax.axis_index("y"), lax.axis_index("c")


def _other_chips(x, y):
    return [(1 - x, y), (x, 1 - y), (1 - x, 1 - y)]


def _half(ref, c, rows):
    if rows % 32 == 0:
        return ref.at[:, pl.ds(c * (rows // 2), rows // 2), :]
    cols = ref.shape[2]
    return ref.at[:, :, pl.ds(c * (cols // 2), cols // 2)]


_ANY = pl.BlockSpec(memory_space=pl.ANY)


_HBM = pl.BlockSpec(memory_space=pltpu.HBM)
_SEM = pl.BlockSpec(memory_space=pltpu.SEMAPHORE)
_DATAFLOW = pltpu.SideEffectType.DATAFLOW_SIDE_EFFECTING


def _in_hbm(a):
    return pltpu.with_memory_space_constraint(a, pltpu.HBM)


def _gather_ici_copies(bufs, send_sems, recv_sems, arrivals):
    x, y, c = _place()
    me = 2 * x + y
    copies = []
    for i, buf in enumerate(bufs):
        rows = buf.shape[2]
        for j, (qx, qy) in enumerate(_other_chips(x, y)):
            block = _half(buf.at[2 * qx + qy if arrivals else me], c, rows)
            copies.append(pltpu.make_async_remote_copy(
                src_ref=block, dst_ref=block, send_sem=send_sems.at[3 * i + j], recv_sem=recv_sems.at[3 * i + j],
                device_id=(qx, qy, c), device_id_type=MESH))
    return copies


def gather_ici_start(name, bufs, after):
    n = len(bufs)

    def body(*refs):
        ins = refs[:n]
        send_sems, recv_sems = refs[n + 1], refs[n + 2]
        token = refs[-1]
        for send in _gather_ici_copies(ins, send_sems, recv_sems, False):
            send.start()
        token[...] = jnp.zeros_like(token)

    res = pl.pallas_call(
        body, name=name,
        out_shape=(pltpu.SemaphoreType.DMA((3 * n,)), pltpu.SemaphoreType.DMA((3 * n,)), *[pltpu.HBM(b.shape, b.dtype) for b in bufs],
                   jax.ShapeDtypeStruct((8, 128), F32)),
        in_specs=[_HBM] * n + [_ANY], out_specs=(_SEM, _SEM, *[_HBM] * n, pl.BlockSpec(memory_space=pltpu.VMEM)),
        input_output_aliases={i: 2 + i for i in range(n)},
        compiler_params=pltpu.CompilerParams(has_side_effects=_DATAFLOW),
    )(*[_in_hbm(b) for b in bufs], after)
    return res[0], res[1], list(res[2:2 + n]), res[-1]


def gather_ici_wait(name, send_sems, recv_sems, bufs, after):
    n = len(bufs)

    def body(*refs):
        ins = refs[:n]
        send_sems_ref, recv_sems_ref = refs[n], refs[n + 1]
        for send in _gather_ici_copies(ins, send_sems_ref, recv_sems_ref, False):
            send.wait_send()
        for recv in _gather_ici_copies(ins, send_sems_ref, recv_sems_ref, True):
            recv.wait_recv()

    return pl.pallas_call(
        body, name=name, out_shape=tuple(pltpu.HBM(b.shape, b.dtype) for b in bufs),
        in_specs=[_HBM] * n + [_SEM, _SEM, _ANY], out_specs=tuple([_HBM] * n),
        input_output_aliases={i: i for i in range(n)},
        compiler_params=pltpu.CompilerParams(has_side_effects=_DATAFLOW),
    )(*bufs, send_sems, recv_sems, after)


def gather_forward(name, bufs):
    n = len(bufs)

    def body(*refs):
        outs = refs[n:2 * n]
        send_sems, recv_sems = refs[2 * n:]
        x, y, c = _place()
        sibling = (x, y, 1 - c)
        sends = []
        for i in range(n):
            rows = outs[i].shape[2]
            for j, (qx, qy) in enumerate(_other_chips(x, y)):
                block = _half(outs[i].at[2 * qx + qy], c, rows)
                fw = pltpu.make_async_remote_copy(
                    src_ref=block, dst_ref=block, send_sem=send_sems.at[3 * i + j], recv_sem=recv_sems.at[3 * i + j],
                    device_id=sibling, device_id_type=MESH)
                fw.start()
                sends.append(fw)
        for i in range(n):
            rows = outs[i].shape[2]
            for j, (qx, qy) in enumerate(_other_chips(x, y)):
                block = _half(outs[i].at[2 * qx + qy], 1 - c, rows)
                pltpu.make_async_remote_copy(
                    src_ref=block, dst_ref=block, send_sem=send_sems.at[3 * i + j], recv_sem=recv_sems.at[3 * i + j],
                    device_id=sibling, device_id_type=MESH).wait_recv()
        for fw in sends:
            fw.wait_send()

    return pl.pallas_call(
        body, name=name, in_specs=[_ANY] * n, out_specs=[_ANY] * n,
        out_shape=[jax.ShapeDtypeStruct(b.shape, b.dtype) for b in bufs],
        input_output_aliases={i: i for i in range(n)},
        scratch_shapes=[pltpu.SemaphoreType.DMA((3 * n,)), pltpu.SemaphoreType.DMA((3 * n,))],
        compiler_params=pltpu.CompilerParams(has_side_effects=True),
    )(*bufs)


def _between_chips_copies(parts, lands, send_sems, recv_sems):
    x, y, c = _place()
    copies = []
    for i, (part, land) in enumerate(zip(parts, lands)):
        for j, (qx, qy) in enumerate(_other_chips(x, y)):
            copies.append(pltpu.make_async_remote_copy(
                src_ref=part.at[2 * qx + qy], dst_ref=land.at[j], send_sem=send_sems.at[3 * i + j], recv_sem=recv_sems.at[3 * i + j],
                device_id=(qx, qy, c), device_id_type=MESH))
    return copies


def between_chips_start(name, parts):
    n = len(parts)
    lands = [lax.empty((N_CHIPS - 1,) + p.shape[1:], p.dtype) for p in parts]

    def body(*refs):
        send_sems, recv_sems = refs[2 * n], refs[2 * n + 1]
        token = refs[-1]
        for cp in _between_chips_copies(refs[:n], refs[n:2 * n], send_sems, recv_sems):
            cp.start()
        token[...] = jnp.zeros_like(token)

    res = pl.pallas_call(
        body, name=name,
        out_shape=(pltpu.SemaphoreType.DMA((3 * n,)), pltpu.SemaphoreType.DMA((3 * n,)),
                   *[pltpu.HBM(a.shape, a.dtype) for a in parts + lands], jax.ShapeDtypeStruct((8, 128), F32)),
        in_specs=[_HBM] * (2 * n), out_specs=(_SEM, _SEM, *[_HBM] * (2 * n), pl.BlockSpec(memory_space=pltpu.VMEM)),
        input_output_aliases={i: 2 + i for i in range(2 * n)},
        compiler_params=pltpu.CompilerParams(has_side_effects=_DATAFLOW),
    )(*[_in_hbm(a) for a in parts + lands])
    return res[0], res[1], list(res[2:2 + n]), list(res[2 + n:2 + 2 * n]), res[-1]


def between_chips_wait(name, send_sems, recv_sems, parts, lands, after):
    n = len(parts)

    def body(*refs):
        for cp in _between_chips_copies(refs[:n], refs[n:2 * n], refs[2 * n], refs[2 * n + 1]):
            cp.wait_send()
            cp.wait_recv()

    res = pl.pallas_call(
        body, name=name, out_shape=tuple(pltpu.HBM(a.shape, a.dtype) for a in parts + lands),
        in_specs=[_HBM] * (2 * n) + [_SEM, _SEM, _ANY], out_specs=tuple([_HBM] * (2 * n)),
        input_output_aliases={i: i for i in range(2 * n)},
        compiler_params=pltpu.CompilerParams(has_side_effects=_DATAFLOW),
    )(*parts, *lands, send_sems, recv_sems, after)
    return list(res[:n]), list(res[n:])


def exchange_start(name, arrays, n_copies, copies, after=()):
    n = len(arrays)

    def body(*refs):
        send_sems, recv_sems = refs[n + len(after)], refs[n + len(after) + 1]
        for cp in copies(refs[:n], send_sems, recv_sems):
            cp.start()
        refs[-1][...] = jnp.zeros_like(refs[-1])

    res = pl.pallas_call(
        body, name=name,
        out_shape=(pltpu.SemaphoreType.DMA((n_copies,)), pltpu.SemaphoreType.DMA((n_copies,)),
                   *[pltpu.HBM(a.shape, a.dtype) for a in arrays], jax.ShapeDtypeStruct((8, 128), F32)),
        in_specs=[_HBM] * n + [_ANY] * len(after),
        out_specs=(_SEM, _SEM, *[_HBM] * n, pl.BlockSpec(memory_space=pltpu.VMEM)),
        input_output_aliases={i: 2 + i for i in range(n)},
        compiler_params=pltpu.CompilerParams(has_side_effects=_DATAFLOW),
    )(*[_in_hbm(a) for a in arrays], *after)
    return res[0], res[1], list(res[2:2 + n]), res[-1]


def exchange_wait(name, send_sems, recv_sems, arrays, copies, after):
    n = len(arrays)

    def body(*refs):
        for cp in copies(refs[:n], refs[n], refs[n + 1]):
            cp.wait_send()
            cp.wait_recv()

    return list(pl.pallas_call(
        body, name=name, out_shape=tuple(pltpu.HBM(a.shape, a.dtype) for a in arrays),
        in_specs=[_HBM] * n + [_SEM, _SEM, _ANY], out_specs=tuple([_HBM] * n),
        input_output_aliases={i: i for i in range(n)},
        compiler_params=pltpu.CompilerParams(has_side_effects=_DATAFLOW),
    )(*arrays, send_sems, recv_sems, after))


def _to_sibling_copies(n):
    def copies(refs, send_sems, recv_sems):
        x, y, c = _place()
        out = []
        for i in range(n):
            rows = refs[i].shape[2]
            out.append(pltpu.make_async_remote_copy(
                src_ref=refs[i].at[:, :, pl.ds((1 - c) * (rows // 2), rows // 2), :], dst_ref=refs[n + i],
                send_sem=send_sems.at[i], recv_sem=recv_sems.at[i], device_id=(x, y, 1 - c), device_id_type=MESH))
        return out
    return copies


def _forward_copies(n):
    def copies(refs, send_sems, recv_sems):
        x, y, c = _place()
        out = []
        for i in range(n):
            rows = refs[i].shape[2]
            for j, (qx, qy) in enumerate(_other_chips(x, y)):
                block = _half(refs[i].at[2 * qx + qy], c, rows)
                out.append(pltpu.make_async_remote_copy(
                    src_ref=block, dst_ref=block, send_sem=send_sems.at[3 * i + j], recv_sem=recv_sems.at[3 * i + j],
                    device_id=(x, y, 1 - c), device_id_type=MESH))
        return out
    return copies


def _share_copies(n):
    def copies(refs, send_sems, recv_sems):
        x, y, c = _place()
        out = []
        for i in range(n):
            mine = _half(refs[i], c, refs[i].shape[1])
            out.append(pltpu.make_async_remote_copy(
                src_ref=mine, dst_ref=mine, send_sem=send_sems.at[i], recv_sem=recv_sems.at[i],
                device_id=(x, y, 1 - c), device_id_type=MESH))
        return out
    return copies


def pair_sum(name, grad, recv, c):
    ns, na, rh, cols = recv.shape
    tr = _row_tile(rh, 256) if rh % 256 == 0 else rh
    nt = rh // tr

    def body(c_ref, g_ref, r_ref, o_ref):
        o_ref[...] = (g_ref[...].astype(F32) + r_ref[...].astype(F32)).astype(BF16)

    blk = (None, None, tr, cols)
    return pl.pallas_call(
        body, name=name,
        grid_spec=pltpu.PrefetchScalarGridSpec(
            num_scalar_prefetch=1, grid=(ns, na, nt),
            in_specs=[pl.BlockSpec(blk, lambda s, a, r, c_ref: (s, a, c_ref[0] * nt + r, 0)),
                      pl.BlockSpec(blk, lambda s, a, r, c_ref: (s, a, r, 0))],
            out_specs=pl.BlockSpec(blk, lambda s, a, r, c_ref: (s, a, r, 0))),
        out_shape=jax.ShapeDtypeStruct(recv.shape, BF16),
        compiler_params=_params(("parallel", "parallel", "parallel")),
    )(c, grad, recv)


def chip_sum(name, parts, landed, place):
    _, na, rh, cols = parts.shape
    tr = _row_tile(rh, 256) if rh % 256 == 0 else rh
    nt = rh // tr

    def body(place_ref, p_ref, l_ref, o_ref):
        total = p_ref[...].astype(F32)
        for s in range(N_CHIPS - 1):
            total = total + l_ref[s].astype(F32)
        o_ref[...] = total

    return pl.pallas_call(
        body, name=name,
        grid_spec=pltpu.PrefetchScalarGridSpec(
            num_scalar_prefetch=1, grid=(na, nt),
            in_specs=[pl.BlockSpec((None, None, tr, cols), lambda a, r, pr: (pr[1], a, r, 0)),
                      pl.BlockSpec((N_CHIPS - 1, None, tr, cols), lambda a, r, pr: (0, a, r, 0))],
            out_specs=pl.BlockSpec((None, tr, cols), lambda a, r, pr: (a, pr[0] * nt + r, 0))),
        out_shape=jax.ShapeDtypeStruct((na, 2 * rh, cols), F32),
        compiler_params=_params(("parallel", "parallel")),
    )(place, parts, landed)


def reduce_scatter_1(tag, grads):
    n = len(grads)
    lands = [lax.empty((g.shape[0], g.shape[1], g.shape[2] // 2, g.shape[3]), g.dtype) for g in grads]
    send_sems, recv_sems, arrays, token = exchange_start(f"{tag}_to_sibling_start", list(grads) + lands, n, _to_sibling_copies(n))
    return (send_sems, recv_sems, arrays), token


def reduce_scatter_2(tag, state, place, after):
    send_sems, recv_sems, arrays = state
    n = len(arrays) // 2
    arrays = exchange_wait(f"{tag}_to_sibling_wait", send_sems, recv_sems, arrays, _to_sibling_copies(n), after)
    parts = [pair_sum(f"{tag}_pair_sum{i}", g, r, place) for i, (g, r) in enumerate(zip(arrays[:n], arrays[n:]))]
    send_sems, recv_sems, parts, lands, token = between_chips_start(f"{tag}_between_chips_start", parts)
    return (send_sems, recv_sems, parts, lands), token


def reduce_scatter_3(tag, state, place, after):
    send_sems, recv_sems, parts, lands = state
    parts, landed = between_chips_wait(f"{tag}_between_chips_wait", send_sems, recv_sems, parts, lands, after)
    halves = [chip_sum(f"{tag}_chip_sum{i}", p, l, place) for i, (p, l) in enumerate(zip(parts, landed))]
    send_sems, recv_sems, halves, token = exchange_start(f"{tag}_share_start", halves, len(halves), _share_copies(len(halves)))
    return (send_sems, recv_sems, halves), token


def reduce_scatter_4(tag, state, after):
    send_sems, recv_sems, halves = state
    return exchange_wait(f"{tag}_share_wait", send_sems, recv_sems, halves, _share_copies(len(halves)), after)


def _adamw_math(w, g, m, v):
    m = ADAM_B1 * m + (1.0 - ADAM_B1) * g
    v = ADAM_B2 * v + (1.0 - ADAM_B2) * (g * g)
    m_hat = m / (1.0 - ADAM_B1 ** ADAM_STEP)
    v_hat = v / (1.0 - ADAM_B2 ** ADAM_STEP)
    delta = -ADAM_LR * (m_hat / (jnp.sqrt(v_hat) + ADAM_EPS) + ADAM_WD * w)
    return delta, m, v


def adamw(name, w, g, m, v, after=()):
    rows, cols = w.shape
    tr = _row_tile(rows, 256) if rows % 256 == 0 else rows // 2

    def body(w_ref, g_ref, m_ref, v_ref, d_ref, mo_ref, vo_ref):
        d_ref[...], mo_ref[...], vo_ref[...] = _adamw_math(w_ref[...], g_ref[...], m_ref[...], v_ref[...])

    blk = pl.BlockSpec((tr, cols), lambda i: (i, 0))
    return pl.pallas_call(
        _ordered_after(body, 4, len(after)), name=name, grid=(rows // tr,), in_specs=[blk] * 4 + [_ANY] * len(after),
        out_specs=[blk] * 3, out_shape=[jax.ShapeDtypeStruct(w.shape, F32)] * 3, compiler_params=_params(("parallel",)),
    )(w, g, m, v, *after)


def adamw_rows(name, w, g, m, v, after=()):
    rows, _, cols = w.shape
    tr = next(r for r in (128, 110, 64, 32, 16, 8, 1) if rows % r == 0)

    def body(w_ref, g_ref, m_ref, v_ref, d_ref, mo_ref, vo_ref):
        d_ref[...], mo_ref[...], vo_ref[...] = _adamw_math(w_ref[...], g_ref[...], m_ref[...], v_ref[...])

    blk = pl.BlockSpec((tr, 1, cols), lambda i: (i, 0, 0))
    return pl.pallas_call(
        _ordered_after(body, 4, len(after)), name=name, grid=(rows // tr,), in_specs=[blk] * 4 + [_ANY] * len(after),
        out_specs=[blk] * 3, out_shape=[jax.ShapeDtypeStruct(w.shape, F32)] * 3, compiler_params=_params(("parallel",)),
    )(w, g, m, v, *after)


def adamw_stacked(name, ws, g, ms, vs, after=()):
    n = len(ws)
    rows, cols = ws[0].shape
    tr = next(r for r in (128, 88, 64, 32, 16, 8) if rows % r == 0)

    def body(*refs):
        w_refs, m_refs, v_refs, g_ref = refs[:n], refs[n:2 * n], refs[2 * n:3 * n], refs[3 * n]
        outs = refs[3 * n + 1:]
        for i in range(n):
            outs[i][...], outs[n + i][...], outs[2 * n + i][...] = _adamw_math(
                w_refs[i][...], g_ref[i], m_refs[i][...], v_refs[i][...])

    blk = pl.BlockSpec((tr, cols), lambda r: (r, 0))
    res = pl.pallas_call(
        _ordered_after(body, 3 * n + 1, len(after)), name=name, grid=(rows // tr,),
        in_specs=[blk] * (3 * n) + [pl.BlockSpec((n, tr, cols), lambda r: (0, r, 0))] + [_ANY] * len(after),
        out_specs=[blk] * (3 * n),
        out_shape=[jax.ShapeDtypeStruct((rows, cols), F32)] * (3 * n), compiler_params=_params(("parallel",)),
    )(*ws, *ms, *vs, g, *after)
    return res[:n], res[n:2 * n], res[2 * n:]


def small_allreduce_adamw(name, g_part, w, m, v):
    rows, cols = g_part.shape

    def body(g_ref, w_ref, m_ref, v_ref, sum_ref, d_ref, mo_ref, vo_ref, land, send_sems, recv_sems):
        x, y, c = _place()
        me = 4 * x + 2 * y + c
        land[me] = g_ref[...]
        copies = []
        for r in range(1, 8):
            peer = (x ^ (r >> 2), y ^ ((r >> 1) & 1), c ^ (r & 1))
            rc = pltpu.make_async_remote_copy(
                src_ref=g_ref, dst_ref=land.at[me], send_sem=send_sems.at[r - 1], recv_sem=recv_sems.at[r - 1],
                device_id=peer, device_id_type=MESH)
            rc.start()
            copies.append(rc)
        for rc in copies:
            rc.wait()
        total = land[0]
        for s in range(1, 8):
            total = total + land[s]
        sum_ref[...] = total
        d_ref[...], mo_ref[...], vo_ref[...] = _adamw_math(w_ref[...], total, m_ref[...], v_ref[...])

    vmem = pl.BlockSpec(memory_space=pltpu.VMEM)
    return pl.pallas_call(
        body, name=name, in_specs=[vmem] * 4, out_specs=[vmem] * 4,
        out_shape=[jax.ShapeDtypeStruct((rows, cols), F32)] * 4,
        scratch_shapes=[pltpu.VMEM((8, rows, cols), F32), pltpu.SemaphoreType.DMA((7,)), pltpu.SemaphoreType.DMA((7,))],
        compiler_params=pltpu.CompilerParams(has_side_effects=True),
    )(g_part, w, m, v)


def kernel(x, norm_ffn1, w_ffn1_gate, w_ffn1_up, w_ffn1_down, norm_mix, w_in, b_forget, w_gate, b_gate, w_up_a, w_up_b, w_out, norm_ffn2, w_ffn2_gate, w_ffn2_up, w_ffn2_down, norm_final, loss_target, m_norm_ffn1, m_w_ffn1_gate, m_w_ffn1_up, m_w_ffn1_down, m_norm_mix, m_w_in, m_b_forget, m_w_gate, m_b_gate, m_w_up_a, m_w_up_b, m_w_out, m_norm_ffn2, m_w_ffn2_gate, m_w_ffn2_up, m_w_ffn2_down, m_norm_final, v_norm_ffn1, v_w_ffn1_gate, v_w_ffn1_up, v_w_ffn1_down, v_norm_mix, v_w_in, v_b_forget, v_w_gate, v_b_gate, v_w_up_a, v_w_up_b, v_w_out, v_norm_ffn2, v_w_ffn2_gate, v_w_ffn2_up, v_w_ffn2_down, v_norm_final):
    t, d = x.shape[1], x.shape[2]
    in4 = w_in.shape[2]
    gate4 = w_gate.shape[2]
    in_cols = N_CHIPS * in4
    n_forget = in_cols - QKV_COLS
    assert w_up_a.shape[1] == WIDTH and d == 2 * WIDTH and n_forget == N_HEADS and t % BQ == 0
    chip = 2 * lax.axis_index("x") + lax.axis_index("y")
    c_arr = jnp.stack([lax.axis_index("c"), chip]).astype(jnp.int32)
    x2d = x[0]
    target = loss_target[0]

    def slot(shard):
        return lax.dynamic_update_slice(lax.empty((N_CHIPS,) + shard.shape, BF16), shard.astype(BF16)[None], (chip, 0, 0, 0))

    def ffn_views(wg, wu, wd):
        return [wg[0].T, wu[0].T, wd[0]]

    ffn1_w, ffn1_m, ffn1_v = (ffn_views(w_ffn1_gate, w_ffn1_up, w_ffn1_down), ffn_views(m_w_ffn1_gate, m_w_ffn1_up, m_w_ffn1_down),
                              ffn_views(v_w_ffn1_gate, v_w_ffn1_up, v_w_ffn1_down))
    ffn2_w, ffn2_m, ffn2_v = (ffn_views(w_ffn2_gate, w_ffn2_up, w_ffn2_down), ffn_views(m_w_ffn2_gate, m_w_ffn2_up, m_w_ffn2_down),
                              ffn_views(v_w_ffn2_gate, v_w_ffn2_up, v_w_ffn2_down))
    in_pad = -(-in4 // 32) * 32
    in_sh = slot(w_in[0].T[None])
    gt_sh = slot(w_gate[0].T[None])
    up_sh = slot(jnp.concatenate([w_up_a[0], w_up_b[0]], axis=0)[None])
    wo_sh = slot(w_out)
    f1_send, f1_recv, f1_bufs, f1_token = gather_ici_start("gather_ffn1_start", [slot(jnp.stack(ffn1_w))], norm_ffn1)
    mx_send, mx_recv, mx_bufs, mx_token = gather_ici_start("gather_mixer_start", [in_sh, gt_sh, up_sh, wo_sh], f1_token)
    f2_send, f2_recv, f2_bufs, f2_token = gather_ici_start("gather_ffn2_start", [slot(jnp.stack(ffn2_w))], mx_token)

    normed1 = rms_fwd("ffn1_rms", x2d, norm_ffn1, after=(f2_token,))
    (w3_1,) = gather_forward("gather_ffn1_forward", gather_ici_wait("gather_ffn1_wait", f1_send, f1_recv, f1_bufs, normed1[0]))
    forwards = {}

    def forward_behind(tag, send_sems, recv_sems, bufs):
        def start(after):
            landed = gather_ici_wait(f"gather_{tag}_wait", send_sems, recv_sems, bufs, after)
            *forwards[tag], token = exchange_start(f"gather_{tag}_forward_start", landed, 3 * len(landed), _forward_copies(len(landed)))
            return (token,)
        return start

    def forwarded(tag, after):
        send_sems, recv_sems, bufs = forwards[tag]
        return exchange_wait(f"gather_{tag}_forward_wait", send_sems, recv_sems, bufs, _forward_copies(len(bufs)), after)

    x1, saved1 = ffn_forward("ffn1", x2d, norm_ffn1, w3_1, normed=normed1, between=forward_behind("mixer", mx_send, mx_recv, mx_bufs))
    w_in_g, w_gate_g, wup, wo = forwarded("mixer", x1)
    wup = wup[:, 0]
    w_in_t = w_in_g.reshape(in_cols, d)
    w_gate_t = w_gate_g.reshape(2 * d, d)
    w_f_t = jnp.pad(w_in_t[QKV_COLS:], ((0, QB - n_forget), (0, 0)))
    wo_full = wo.reshape(d, d)
    b_forget_row = jnp.pad(b_forget, ((0, 0), (0, QB - n_forget)))

    h2, rstd2 = rms_fwd("mix_rms", x1, norm_mix)
    qkv = proj("mix_proj_qkv", h2, w_in_t, jnp.zeros((1, QKV_COLS), F32), 0, QKV_COLS, WIDTH, BF16, scaled_tiles=(0, 3))
    pc = proj("mix_proj_gates", h2, w_gate_t, b_gate, 0, 2 * d, WIDTH, F32)
    f_logit = proj("mix_proj_forget", h2, w_f_t, jnp.zeros((1, QB), F32), 0, QB, QB, F32)
    c_cum = fox_prep("fox_prep", f_logit, b_forget_row)
    c_heads = c_cum[:, :N_HEADS].T
    c_col = c_heads[:, :, None]
    c_row = c_heads.reshape(N_HEADS, t // CS, 1, CS)
    ya = sb_pair_fwd("sb_fwd", qkv)
    yb, lse = fox_pair_fwd("fox_fwd", qkv, c_col, c_row)
    ua, ub, mixed = mix_fwd("mix_fwd", ya, yb, wup, pc)
    x2 = mm_residual("mix_out", mixed[None], wo_full[None], pl.BlockSpec((1, d, d), lambda m: (0, 0, 0)), x1, 1.0,
                     after=forward_behind("ffn2", f2_send, f2_recv, f2_bufs)(mixed))
    (w3_2,) = forwarded("ffn2", x2)
    x3, saved2 = ffn_forward("ffn2", x2, norm_ffn2, w3_2)
    dx3, gn_final, loss_part = loss_head("loss_head", x3, norm_final[None], target)

    g_w3_2, dab2 = ffn_backward_weights("ffn2", dx3, saved2, w3_2)
    rs_ffn2, token = reduce_scatter_1("rs_ffn2", [g_w3_2])
    dx2, gn_ffn2 = ffn_backward_input("ffn2", dx3, saved2, dab2, norm_ffn2, w3_2, after=(token,))
    rs_ffn2, rs_ffn2_token = reduce_scatter_2("rs_ffn2", rs_ffn2, c_arr, dx2)

    dua, dub, dpa, dpb, gba, gbb = mix_bwd("mix_bwd", dx2, wo_full, pc, ua, ub, after=(rs_ffn2_token,))
    g_bgate = jnp.concatenate([gba, gbb], axis=1)
    g_wo = mm_plain("wgrad_out", mixed, dx2, TN, BF16, tk_target=1024)
    dya = up_bwd("dya", dua, wup, 0)
    dyb = up_bwd("dyb", dub, wup, 1)
    g_up = up_wgrad("wgrad_up", ya, yb, dua, dub)
    dqkv = sb_pair_bwd("sb_bwd", qkv, dya, lax.empty((t, QKV_COLS), BF16))
    dqkv, dcc, dcr = fox_pair_bwd("fox_bwd", qkv, yb, dyb, lse, c_col, c_row, dqkv)
    dc = dcc[:, :, :2].transpose(1, 0, 2).reshape(t, N_HEADS) + dcr.reshape(N_HEADS, t).T
    df, g_bf = fox_gate_bwd("fox_gate_bwd", jnp.pad(dc, ((0, 0), (0, QB - N_HEADS))), f_logit, b_forget_row)
    df = df.astype(BF16)
    g_qkv_t = wgrad_cat("wgrad_qkv", h2, dqkv)
    g_f_t = wgrad_cat("wgrad_forget", h2, df)
    g_gate_t = jnp.stack([wgrad_cat("wgrad_gate_a", h2, dpa), wgrad_cat("wgrad_gate_b", h2, dpb)])
    g_in_t = jnp.concatenate([g_qkv_t, g_f_t[:n_forget]], axis=0).reshape(N_CHIPS, in4, d)
    rs_mixer, token = reduce_scatter_1(
        "rs_mixer", [jnp.pad(g_in_t, ((0, 0), (0, in_pad - in4), (0, 0)))[:, None], g_gate_t.reshape(N_CHIPS, 1, gate4, d),
                     g_up[:, None], g_wo.reshape(N_CHIPS, 1, d // N_CHIPS, d)])
    dx1, gn_mix = mixer_dh("mix_dh", [(dqkv, w_in_t, 0), (dpa, w_gate_t, 0), (dpb, w_gate_t, 1), (df, w_f_t, 0)],
                           x1, rstd2, norm_mix, dx2, after=(token,))
    rs_mixer, rs_mixer_token = reduce_scatter_2("rs_mixer", rs_mixer, c_arr, dx1)
    rs_ffn2, rs_ffn2_token = reduce_scatter_3("rs_ffn2", rs_ffn2, c_arr, dx1)

    g_w3_1, dab1 = ffn_backward_weights("ffn1", dx1, saved1, w3_1, after=(rs_mixer_token, rs_ffn2_token))
    rs_ffn1, token = reduce_scatter_1("rs_ffn1", [g_w3_1])
    dx0, gn_ffn1 = ffn_backward_input("ffn1", dx1, saved1, dab1, norm_ffn1, w3_1, after=(token,))
    rs_ffn1, rs_ffn1_token = reduce_scatter_2("rs_ffn1", rs_ffn1, c_arr, dx0)
    rs_mixer, rs_mixer_token = reduce_scatter_3("rs_mixer", rs_mixer, c_arr, dx0)
    (s_w3_2,) = reduce_scatter_4("rs_ffn2", rs_ffn2, dx0)

    def pack_small(n1, nm, n2, nf, bg, bf, last):
        return jnp.concatenate([n1, nm, n2, nf, bg.reshape(2, d), jnp.pad(bf, ((0, 0), (0, d - n_forget))), last], axis=0)

    zero_row = jnp.zeros((1, d), F32)
    g_small = pack_small(gn_ffn1, gn_mix, gn_ffn2, gn_final, g_bgate, g_bf[:, :n_forget], jnp.pad(loss_part, ((0, 0), (0, d - 1))))
    w_small = pack_small(norm_ffn1, norm_mix, norm_ffn2, norm_final[None], b_gate, b_forget, zero_row)
    m_small = pack_small(m_norm_ffn1, m_norm_mix, m_norm_ffn2, m_norm_final[None], m_b_gate, m_b_forget, zero_row)
    v_small = pack_small(v_norm_ffn1, v_norm_mix, v_norm_ffn2, v_norm_final[None], v_b_gate, v_b_forget, zero_row)
    smalls = small_allreduce_adamw("small_allreduce_adamw", g_small, w_small, m_small, v_small)

    def unpack_small(p):
        return {"norm_ffn1": p[0:1], "norm_mix": p[1:2], "norm_ffn2": p[2:3], "norm_final": p[3], "b_gate": p[4:6].reshape(1, 2 * d),
                "b_forget": p[6:7, :n_forget]}

    loss = smalls[0][7, 0]
    small_out = [unpack_small(p) for p in smalls]

    big_out = {}

    def adamw_ffn(tag, s_w3, ws, ms, vs, after):
        deltas, new_ms, new_vs = adamw_stacked(f"adamw_{tag}", ws, s_w3, ms, vs, after)
        for which, part in ((GATE, "gate"), (UP, "up"), (DOWN, "down")):
            back = (lambda a: a[None]) if which == DOWN else (lambda a: a.T[None])
            big_out[f"w_{tag}_{part}"] = tuple(back(a) for a in (s_w3[which], deltas[which], new_ms[which], new_vs[which]))
        return deltas[DOWN]

    last = adamw_ffn("ffn2", s_w3_2, ffn2_w, ffn2_m, ffn2_v, (rs_ffn1_token, rs_mixer_token))
    s_in, s_gt, s_up, s_wo = reduce_scatter_4("rs_mixer", rs_mixer, last)
    grads = {"w_gate": s_gt[0].T, "w_up_a": s_up[0, :WIDTH], "w_up_b": s_up[0, WIDTH:], "w_out": s_wo[0]}
    weights = {"w_gate": (w_gate, m_w_gate, v_w_gate), "w_up_a": (w_up_a, m_w_up_a, v_w_up_a),
               "w_up_b": (w_up_b, m_w_up_b, v_w_up_b), "w_out": (w_out, m_w_out, v_w_out)}
    for wname, (w, m, v) in weights.items():
        g = grads[wname]
        delta, new_m, new_v = adamw(f"adamw_{wname}", w[0], g, m[0], v[0])
        big_out[wname] = (g[None], delta[None], new_m[None], new_v[None])
    rows_of = lambda a: jnp.transpose(a, (2, 0, 1))
    g_in_rows = s_in[0, :in4][:, None, :]
    in_rows = adamw_rows("adamw_w_in", rows_of(w_in), g_in_rows, rows_of(m_w_in), rows_of(v_w_in))
    big_out["w_in"] = tuple(jnp.transpose(a, (1, 2, 0)) for a in (g_in_rows, *in_rows))

    rs_ffn1, token = reduce_scatter_3("rs_ffn1", rs_ffn1, c_arr, in_rows[0])
    (s_w3_1,) = reduce_scatter_4("rs_ffn1", rs_ffn1, token)
    adamw_ffn("ffn1", s_w3_1, ffn1_w, ffn1_m, ffn1_v, ())

    order = ["norm_ffn1", "w_ffn1_gate", "w_ffn1_up", "w_ffn1_down", "norm_mix", "w_in", "b_forget", "w_gate", "b_gate",
             "w_up_a", "w_up_b", "w_out", "norm_ffn2", "w_ffn2_gate", "w_ffn2_up", "w_ffn2_down", "norm_final"]
    outs = [loss, dx0[None]]
    for kind in range(4):
        for wname in order:
            outs.append(big_out[wname][kind] if wname in big_out else small_out[kind][wname])
    return tuple(outs)
```

```python
import functools

import jax
import jax.numpy as jnp
from jax import lax
from jax.experimental import pallas as pl
from jax.experimental.pallas import tpu as pltpu

F32 = jnp.float32
BF16 = jnp.bfloat16

HEAD_DIM = 64
N_HEADS = 8
WIDTH = N_HEADS * HEAD_DIM
QKV_COLS = 6 * WIDTH
RMS_EPS = 1e-6
ATTN_SCALE = HEAD_DIM ** -0.5
N_CHIPS = 4
QB = 128
BQ = 2048
CS = 256
N_SUB = BQ // CS
NEG_BIG = -1e30

ADAM_LR = 0.001
ADAM_B1 = 0.9
ADAM_B2 = 0.999
ADAM_EPS = 1e-08
ADAM_WD = 0.01
ADAM_STEP = 10

VMEM_LIMIT_BYTES = 48 * 1024 * 1024
MESH = pl.DeviceIdType.MESH

NN = ((1,), (0,))
NT = ((1,), (1,))
TN = ((0,), (0,))


def _params(semantics):
    return pltpu.CompilerParams(dimension_semantics=semantics, vmem_limit_bytes=VMEM_LIMIT_BYTES)


def _dot(a, b, contract):
    return lax.dot_general(a.astype(BF16), b.astype(BF16), (contract, ((), ())), preferred_element_type=F32)


def _sigmoid(x):
    return 1.0 / (1.0 + jnp.exp(-x))


def _log1pexp_neg_abs(z):
    return jnp.log(1.0 + jnp.exp(-jnp.abs(z)))


def _split3(x):
    hi = x.astype(BF16)
    r1 = x - hi.astype(F32)
    mid = r1.astype(BF16)
    lo = (r1 - mid.astype(F32)).astype(BF16)
    return hi, mid, lo


def _dot_exact_lhs01(m01, x):
    hi, mid, lo = _split3(x)
    d = lambda p: lax.dot_general(m01, p, (NN, ((), ())), preferred_element_type=F32)
    return d(hi) + d(mid) + d(lo)


def _iota2(shape, dim):
    return lax.broadcasted_iota(jnp.int32, shape, dim)


def _mm(name, pairs, contract, grid, pair_specs, out_shape, out_specs, acc_shape, nk, epilogue,
        extras=(), extra_specs=(), semantics=None):
    n_pairs = len(pairs)
    n_extra = len(extras)
    n_out = len(out_shape)

    def body(*refs):
        ab = refs[:2 * n_pairs]
        ex = refs[2 * n_pairs:2 * n_pairs + n_extra]
        outs = refs[2 * n_pairs + n_extra:2 * n_pairs + n_extra + n_out]
        ids = [pl.program_id(i) for i in range(len(grid))]
        k = ids[-1]
        part = _dot(ab[0][...], ab[1][...], contract)
        for p in range(1, n_pairs):
            part += _dot(ab[2 * p][...], ab[2 * p + 1][...], contract)
        if nk == 1:
            epilogue(part, ex, outs, ids)
            return
        acc = refs[-1]

        @pl.when(k == 0)
        def _():
            acc[...] = part

        @pl.when(k != 0)
        def _():
            acc[...] += part

        @pl.when(k == nk - 1)
        def _():
            epilogue(acc[...], ex, outs, ids)

    operands = [t for pair in pairs for t in pair] + list(extras)
    in_specs = [s for pair in pair_specs for s in pair] + list(extra_specs)
    if semantics is None:
        semantics = ("parallel",) * (len(grid) - 1) + ("arbitrary",)
    return pl.pallas_call(
        body, name=name, grid=grid, in_specs=in_specs, out_specs=list(out_specs), out_shape=list(out_shape),
        scratch_shapes=[] if nk == 1 else [pltpu.VMEM(acc_shape, F32)], compiler_params=_params(semantics),
    )(*operands)


def _ordered_after(body, n_in, n_after):
    def wrapped(*refs):
        return body(*refs[:n_in], *refs[n_in + n_after:])
    return wrapped


def _row_tile(rows, target):
    t = min(rows, target)
    while rows % t:
        t //= 2
    return t


def rms_fwd(name, x, g, after=()):
    t, d = x.shape
    tr = _row_tile(t, 256)

    def body(x_ref, g_ref, h_ref, r_ref):
        xv = x_ref[...]
        r = lax.rsqrt(jnp.mean(xv * xv, axis=-1, keepdims=True) + RMS_EPS)
        h_ref[...] = (xv * r * g_ref[...]).astype(BF16)
        r_ref[...] = r

    return pl.pallas_call(
        _ordered_after(body, 2, len(after)), name=name, grid=(t // tr,),
        in_specs=[pl.BlockSpec((tr, d), lambda i: (i, 0)), pl.BlockSpec((1, d), lambda i: (0, 0))] + [_ANY] * len(after),
        out_specs=[pl.BlockSpec((tr, d), lambda i: (i, 0)), pl.BlockSpec((tr, 1), lambda i: (i, 0))],
        out_shape=[jax.ShapeDtypeStruct((t, d), BF16), jax.ShapeDtypeStruct((t, 1), F32)],
        compiler_params=_params(("parallel",)),
    )(x, g, *after)


GATE, UP, DOWN = 0, 1, 2


def _ffn_w_spec(which, f4, d, index_of_j):
    return pl.BlockSpec((None, None, f4, d), lambda *ids: (index_of_j(*ids), which, 0, 0))


def ffn_up(name, h, w3):
    t, d = h.shape
    ns, _, f4, _ = w3.shape
    tm = _row_tile(t, 512)

    def body(h_ref, wg_ref, wu_ref, a_ref, b_ref, s_ref):
        hv = h_ref[...]
        a = _dot(hv, wg_ref[...], NT)
        b = _dot(hv, wu_ref[...], NT)
        a_ref[...] = a.astype(BF16)
        b_ref[...] = b.astype(BF16)
        s_ref[...] = (a * _sigmoid(a) * b).astype(BF16)

    act_spec = pl.BlockSpec((None, tm, f4), lambda j, m: (j, m, 0))
    return pl.pallas_call(
        body, name=name, grid=(ns, t // tm),
        in_specs=[pl.BlockSpec((tm, d), lambda j, m: (m, 0)),
                  _ffn_w_spec(GATE, f4, d, lambda j, m: j), _ffn_w_spec(UP, f4, d, lambda j, m: j)],
        out_specs=[act_spec, act_spec, act_spec],
        out_shape=[jax.ShapeDtypeStruct((ns, t, f4), BF16)] * 3,
        compiler_params=_params(("parallel", "parallel")),
    )(h, w3, w3)


def mm_residual(name, s, w, w_spec, x, scale):
    nj, t, kdim = s.shape
    n = x.shape[1]
    tm = _row_tile(t, 512)

    def body(s_ref, w_ref, x_ref, o_ref):
        acc = _dot(s_ref[0], w_ref[0], NN)
        for j in range(1, nj):
            acc += _dot(s_ref[j], w_ref[j], NN)
        o_ref[...] = x_ref[...] + scale * acc

    row = pl.BlockSpec((tm, n), lambda m: (m, 0))
    return pl.pallas_call(
        body, name=name, grid=(t // tm,),
        in_specs=[pl.BlockSpec((nj, tm, kdim), lambda m: (0, m, 0)), w_spec, row], out_specs=row,
        out_shape=jax.ShapeDtypeStruct((t, n), F32), compiler_params=_params(("parallel",)),
    )(s, w, x)


def ffn_bwd_act(name, dx, w3, a, b, after=()):
    t, d = dx.shape
    ns, _, f4, _ = w3.shape
    tm = _row_tile(t, 512)

    def body(dx_ref, wd_ref, a_ref, b_ref, da_ref, db_ref):
        ds = _dot(0.5 * dx_ref[...], wd_ref[...], NT)
        av = a_ref[...].astype(F32)
        sig = _sigmoid(av)
        da_ref[...] = (ds * b_ref[...].astype(F32) * (sig * (1.0 + av * (1.0 - sig)))).astype(BF16)
        db_ref[...] = (ds * (av * sig)).astype(BF16)

    act_spec = pl.BlockSpec((None, tm, f4), lambda j, m: (j, m, 0))
    return pl.pallas_call(
        _ordered_after(body, 4, len(after)), name=name, grid=(ns, t // tm),
        in_specs=[pl.BlockSpec((tm, d), lambda j, m: (m, 0)), _ffn_w_spec(DOWN, f4, d, lambda j, m: j), act_spec, act_spec]
        + [_ANY] * len(after),
        out_specs=[act_spec, act_spec],
        out_shape=[jax.ShapeDtypeStruct((ns, t, f4), BF16)] * 2,
        compiler_params=_params(("parallel", "parallel")),
    )(dx, w3, a, b, *after)


def ffn_wgrad(name, h, da, db, s, dx):
    t, d = h.shape
    ns, _, f4 = da.shape
    tk = _row_tile(t, 1024)
    nk = t // tk

    def body(h_ref, da_ref, db_ref, s_ref, dx_ref, o_ref, acc):
        k = pl.program_id(1)

        @pl.when(k == 0)
        def _():
            acc[...] = jnp.zeros_like(acc)

        hv = h_ref[...]
        acc[GATE] += _dot(da_ref[...], hv, TN)
        acc[UP] += _dot(db_ref[...], hv, TN)
        acc[DOWN] += _dot(s_ref[...], 0.5 * dx_ref[...], TN)

        @pl.when(k == nk - 1)
        def _():
            o_ref[...] = acc[...].astype(BF16)

    act_spec = pl.BlockSpec((None, tk, f4), lambda j, k: (j, k, 0))
    row_spec = pl.BlockSpec((tk, d), lambda j, k: (k, 0))
    return pl.pallas_call(
        body, name=name, grid=(ns, nk),
        in_specs=[row_spec, act_spec, act_spec, act_spec, row_spec],
        out_specs=pl.BlockSpec((None, 3, f4, d), lambda j, k: (j, 0, 0, 0)),
        out_shape=jax.ShapeDtypeStruct((ns, 3, f4, d), BF16),
        scratch_shapes=[pltpu.VMEM((3, f4, d), F32)],
        compiler_params=_params(("parallel", "arbitrary")),
    )(h, da, db, s, dx)


def _rms_bwd_tail(dh, x_ref, r_ref, g_ref, dxin_ref, dx_ref, gn_ref, row_tile_index):
    r = r_ref[...]
    xhat = x_ref[...] * r
    dhg = dh * g_ref[...]
    dx_ref[...] = dxin_ref[...] + r * (dhg - xhat * jnp.mean(dhg * xhat, axis=-1, keepdims=True))
    part = jnp.sum(dh * xhat, axis=0, keepdims=True)

    @pl.when(row_tile_index == 0)
    def _():
        gn_ref[...] = part

    @pl.when(row_tile_index != 0)
    def _():
        gn_ref[...] += part


def ffn_dh(name, da, db, w3, x, rstd, g, dx_in, after=()):
    ns, t, f4 = da.shape
    d = x.shape[1]
    tm = _row_tile(t, 256)

    def body(da_ref, db_ref, wg_ref, wu_ref, x_ref, r_ref, g_ref, dxin_ref, dx_ref, gn_ref):
        dh = _dot(da_ref[0], wg_ref[0], NN) + _dot(db_ref[0], wu_ref[0], NN)
        for j in range(1, ns):
            dh += _dot(da_ref[j], wg_ref[j], NN) + _dot(db_ref[j], wu_ref[j], NN)
        _rms_bwd_tail(dh, x_ref, r_ref, g_ref, dxin_ref, dx_ref, gn_ref, pl.program_id(0))

    act = pl.BlockSpec((ns, tm, f4), lambda m: (0, m, 0))
    row = pl.BlockSpec((tm, d), lambda m: (m, 0))
    gain = pl.BlockSpec((1, d), lambda m: (0, 0))
    return pl.pallas_call(
        _ordered_after(body, 8, len(after)), name=name, grid=(t // tm,),
        in_specs=[act, act, pl.BlockSpec((ns, None, f4, d), lambda m: (0, GATE, 0, 0)),
                  pl.BlockSpec((ns, None, f4, d), lambda m: (0, UP, 0, 0)), row, pl.BlockSpec((tm, 1), lambda m: (m, 0)), gain, row]
        + [_ANY] * len(after),
        out_specs=[row, gain], out_shape=[jax.ShapeDtypeStruct((t, d), F32), jax.ShapeDtypeStruct((1, d), F32)],
        compiler_params=_params(("arbitrary",)),
    )(da, db, w3, w3, x, rstd, g, dx_in, *after)


def mixer_dh(name, parts, x, rstd, g, dx_in, after=()):
    t, d = x.shape
    n = len(parts)
    tm = _row_tile(t, 256)

    def body(*refs):
        dh = _dot(refs[0][...], refs[n][...], NN)
        for i in range(1, n):
            dh += _dot(refs[i][...], refs[n + i][...], NN)
        _rms_bwd_tail(dh, *refs[2 * n:2 * n + 6], pl.program_id(0))

    row = pl.BlockSpec((tm, d), lambda m: (m, 0))
    gain = pl.BlockSpec((1, d), lambda m: (0, 0))
    act_specs = [pl.BlockSpec((tm, a.shape[1]), lambda m: (m, 0)) for a, _, _ in parts]
    w_specs = [pl.BlockSpec((a.shape[1], d), functools.partial(lambda m, blk: (blk, 0), blk=blk)) for a, _, blk in parts]
    return pl.pallas_call(
        _ordered_after(body, 2 * n + 4, len(after)), name=name, grid=(t // tm,),
        in_specs=act_specs + w_specs + [row, pl.BlockSpec((tm, 1), lambda m: (m, 0)), gain, row] + [_ANY] * len(after),
        out_specs=[row, gain], out_shape=[jax.ShapeDtypeStruct((t, d), F32), jax.ShapeDtypeStruct((1, d), F32)],
        compiler_params=_params(("arbitrary",)),
    )(*[a for a, _, _ in parts], *[w for _, w, _ in parts], x, rstd, g, dx_in, *after)


def ffn_forward(tag, x, g_norm, w3, normed=None):
    _, _, f4, d = w3.shape
    h, rstd = normed if normed is not None else rms_fwd(f"{tag}_rms", x, g_norm)
    a, b, s = ffn_up(f"{tag}_up", h, w3)
    x_out = mm_residual(f"{tag}_down", s, w3, pl.BlockSpec((w3.shape[0], None, f4, d), lambda m: (0, DOWN, 0, 0)), x, 0.5)
    return x_out, (x, h, rstd, a, b, s)


def ffn_backward_weights(tag, dx, saved, w3, after=()):
    x, h, rstd, a, b, s = saved
    da, db = ffn_bwd_act(f"{tag}_bwd_act", dx, w3, a, b, after)
    return ffn_wgrad(f"{tag}_wgrad", h, da, db, s, dx), (da, db)


def ffn_backward_input(tag, dx, saved, dab, g_norm, w3, after=()):
    x, h, rstd, a, b, s = saved
    return ffn_dh(f"{tag}_dh", dab[0], dab[1], w3, x, rstd, g_norm, dx, after)


def proj(name, h, wcat_t, bias, first_col, n_cols, tn, out_dtype, scaled_tiles=()):
    t, d = h.shape
    tm = _row_tile(t, 512)
    off = first_col // tn

    def epilogue(acc, ex, outs, ids):
        val = acc + ex[0][...]
        if scaled_tiles:
            hit = functools.reduce(jnp.logical_or, [ids[0] == s for s in scaled_tiles])
            val = val * jnp.where(hit, ATTN_SCALE, 1.0)
        outs[0][...] = val.astype(out_dtype)

    return _mm(
        name, [(h, wcat_t)], NT, (n_cols // tn, t // tm, 1),
        [(pl.BlockSpec((tm, d), lambda j, m, k: (m, 0)), pl.BlockSpec((tn, d), lambda j, m, k: (off + j, 0)))],
        [jax.ShapeDtypeStruct((t, n_cols), out_dtype)], [pl.BlockSpec((tm, tn), lambda j, m, k: (m, j))], (tm, tn), 1, epilogue,
        extras=[bias], extra_specs=[pl.BlockSpec((1, tn), lambda j, m, k: (0, off + j))],
    )[0]


def mix_fwd(name, ya, yb, wup, pc):
    t, w = ya.shape
    ns, _, tn = wup.shape
    d = ns * tn
    tm = _row_tile(t, 512)

    def body(ya_ref, yb_ref, wa_ref, wb_ref, pa_ref, pb_ref, ua_ref, ub_ref, mx_ref):
        ua = _dot(ya_ref[...], wa_ref[...], NN)
        ub = _dot(yb_ref[...], wb_ref[...], NN)
        ua_ref[...] = ua
        ub_ref[...] = ub
        mx_ref[...] = (_sigmoid(pa_ref[...]) * ua + _sigmoid(pb_ref[...]) * ub).astype(BF16)

    y_spec = pl.BlockSpec((tm, w), lambda m, n: (m, 0))
    o_spec = pl.BlockSpec((tm, tn), lambda m, n: (m, n))
    return pl.pallas_call(
        body, name=name, grid=(t // tm, ns),
        in_specs=[y_spec, y_spec, pl.BlockSpec((None, w, tn), lambda m, n: (n, 0, 0)), pl.BlockSpec((None, w, tn), lambda m, n: (n, 1, 0)),
                  o_spec, pl.BlockSpec((tm, tn), lambda m, n: (m, ns + n))],
        out_specs=[o_spec, o_spec, o_spec],
        out_shape=[jax.ShapeDtypeStruct((t, d), F32), jax.ShapeDtypeStruct((t, d), F32), jax.ShapeDtypeStruct((t, d), BF16)],
        compiler_params=_params(("parallel", "parallel")),
    )(ya, yb, wup, wup, pc, pc)


def up_bwd(name, du, wup, branch):
    t, d = du.shape
    ns, w2, tn = wup.shape
    w = w2 // 2
    tm = _row_tile(t, 512)

    def body(du_ref, w_ref, o_ref):
        acc = _dot(du_ref[:, 0:tn], w_ref[0], NT)
        for j in range(1, ns):
            acc += _dot(du_ref[:, j * tn:(j + 1) * tn], w_ref[j], NT)
        o_ref[...] = acc.astype(BF16)

    return pl.pallas_call(
        body, name=name, grid=(t // tm,),
        in_specs=[pl.BlockSpec((tm, d), lambda m: (m, 0)), pl.BlockSpec((ns, w, tn), lambda m: (0, branch, 0))],
        out_specs=pl.BlockSpec((tm, w), lambda m: (m, 0)), out_shape=jax.ShapeDtypeStruct((t, w), BF16),
        compiler_params=_params(("parallel",)),
    )(du, wup)


def up_wgrad(name, ya, yb, dua, dub):
    t, w = ya.shape
    d = dua.shape[1]
    tn = d // N_CHIPS

    def body(ya_ref, yb_ref, dua_ref, dub_ref, o_ref):
        o_ref[0:w, :] = _dot(ya_ref[...], dua_ref[...], TN).astype(BF16)
        o_ref[w:2 * w, :] = _dot(yb_ref[...], dub_ref[...], TN).astype(BF16)

    y_spec = pl.BlockSpec((t, w), lambda j: (0, 0))
    du_spec = pl.BlockSpec((t, tn), lambda j: (0, j))
    return pl.pallas_call(
        body, name=name, grid=(N_CHIPS,), in_specs=[y_spec, y_spec, du_spec, du_spec],
        out_specs=pl.BlockSpec((None, 2 * w, tn), lambda j: (j, 0, 0)),
        out_shape=jax.ShapeDtypeStruct((N_CHIPS, 2 * w, tn), BF16), compiler_params=_params(("parallel",)),
    )(ya, yb, dua, dub)


def mix_bwd(name, dx, wo, pc, ua, ub, after=()):
    t, d = dx.shape
    tm = _row_tile(t, 512)
    tn = 512
    off_a = 0
    off_b = d // tn

    def body(dx_ref, wo_ref, pa_ref, pb_ref, ua_ref, ub_ref, dua_ref, dub_ref, dpa_ref, dpb_ref, ba_ref, bb_ref):
        dm = _dot(dx_ref[...], wo_ref[...], NT)
        ga = _sigmoid(pa_ref[...])
        gb = _sigmoid(pb_ref[...])
        dua_ref[...] = (dm * ga).astype(BF16)
        dub_ref[...] = (dm * gb).astype(BF16)
        dpa = dm * ua_ref[...] * ga * (1.0 - ga)
        dpb = dm * ub_ref[...] * gb * (1.0 - gb)
        dpa_ref[...] = dpa.astype(BF16)
        dpb_ref[...] = dpb.astype(BF16)
        sa = jnp.sum(dpa, axis=0, keepdims=True)
        sb = jnp.sum(dpb, axis=0, keepdims=True)

        @pl.when(pl.program_id(1) == 0)
        def _():
            ba_ref[...] = sa
            bb_ref[...] = sb

        @pl.when(pl.program_id(1) != 0)
        def _():
            ba_ref[...] += sa
            bb_ref[...] += sb

    tile = pl.BlockSpec((tm, tn), lambda n, m: (m, n))
    bias = pl.BlockSpec((1, tn), lambda n, m: (0, n))
    return pl.pallas_call(
        _ordered_after(body, 6, len(after)), name=name, grid=(d // tn, t // tm),
        in_specs=[pl.BlockSpec((tm, d), lambda n, m: (m, 0)), pl.BlockSpec((tn, d), lambda n, m: (n, 0)),
                  pl.BlockSpec((tm, tn), lambda n, m: (m, off_a + n)), pl.BlockSpec((tm, tn), lambda n, m: (m, off_b + n)),
                  tile, tile] + [_ANY] * len(after),
        out_specs=[tile, tile, tile, tile, bias, bias],
        out_shape=[jax.ShapeDtypeStruct((t, d), BF16)] * 4 + [jax.ShapeDtypeStruct((1, d), F32)] * 2,
        compiler_params=_params(("parallel", "arbitrary")),
    )(dx, wo, pc, pc, ua, ub, *after)


def mm_plain(name, a, b, contract, out_dtype, tk_target=512):
    if contract == NN:
        m, kdim = a.shape
        n = b.shape[1]
    elif contract == NT:
        m, kdim = a.shape
        n = b.shape[0]
    else:
        kdim, m = a.shape
        n = b.shape[1]
    tm = _row_tile(m, 512)
    tk = _row_tile(kdim, tk_target)
    nk = kdim // tk
    if contract == TN:
        a_spec = pl.BlockSpec((tk, tm), lambda i, k: (k, i))
    else:
        a_spec = pl.BlockSpec((tm, tk), lambda i, k: (i, k))
    if contract == NT:
        b_spec = pl.BlockSpec((n, tk), lambda i, k: (0, k))
    else:
        b_spec = pl.BlockSpec((tk, n), lambda i, k: (k, 0))

    def epilogue(acc, ex, outs, ids):
        outs[0][...] = acc.astype(out_dtype)

    return _mm(name, [(a, b)], contract, (m // tm, nk), [(a_spec, b_spec)],
               [jax.ShapeDtypeStruct((m, n), out_dtype)], [pl.BlockSpec((tm, n), lambda i, k: (i, 0))], (tm, n), nk, epilogue)[0]


def wgrad_cat(name, h, dcat):
    t, d = h.shape
    n = dcat.shape[1]
    tn = next(c for c in (768, 512, 256, 128) if n % c == 0)
    tk = _row_tile(t, 2048)

    def epilogue(acc, ex, outs, ids):
        outs[0][...] = acc.astype(BF16)

    return _mm(
        name, [(dcat, h)], TN, (n // tn, t // tk),
        [(pl.BlockSpec((tk, tn), lambda j, k: (k, j)), pl.BlockSpec((tk, d), lambda j, k: (k, 0)))],
        [jax.ShapeDtypeStruct((n, d), BF16)], [pl.BlockSpec((tn, d), lambda j, k: (j, 0))], (tn, d), t // tk, epilogue,
    )[0]


def fox_prep(name, f, bias):
    t, lanes = f.shape
    nchunk = t // QB

    def body(f_ref, b_ref, c_ref):
        lower = (_iota2((QB, QB), 1) <= _iota2((QB, QB), 0)).astype(BF16)

        def chunk(n, carry):
            rows = pl.ds(pl.multiple_of(n * QB, QB), QB)
            u = f_ref[rows, :] + b_ref[...]
            lf = jnp.minimum(u, 0.0) - _log1pexp_neg_abs(u)
            c = _dot_exact_lhs01(lower, lf) + carry
            c_ref[rows, :] = c
            return c[QB - 1:QB, :]

        lax.fori_loop(0, nchunk, chunk, jnp.zeros((1, lanes), F32))

    return pl.pallas_call(body, name=name, out_shape=jax.ShapeDtypeStruct((t, lanes), F32),
                          compiler_params=pltpu.CompilerParams(vmem_limit_bytes=VMEM_LIMIT_BYTES))(f, bias)


def fox_gate_bwd(name, dc, f, bias):
    t, lanes = dc.shape
    nchunk = t // QB

    def body(dc_ref, f_ref, b_ref, df_ref, gb_ref):
        upper = (_iota2((QB, QB), 1) >= _iota2((QB, QB), 0)).astype(BF16)

        def chunk(n, carry):
            tail, total = carry
            rows = pl.ds(pl.multiple_of((nchunk - 1 - n) * QB, QB), QB)
            dlf = _dot_exact_lhs01(upper, dc_ref[rows, :]) + tail
            u = f_ref[rows, :] + b_ref[...]
            df = dlf * jnp.exp(jnp.minimum(-u, 0.0) - _log1pexp_neg_abs(u))
            df_ref[rows, :] = df
            return dlf[0:1, :], total + jnp.sum(df, axis=0, keepdims=True)

        zero = jnp.zeros((1, lanes), F32)
        _, total = lax.fori_loop(0, nchunk, chunk, (zero, zero))
        gb_ref[...] = total

    return pl.pallas_call(body, name=name,
                          out_shape=[jax.ShapeDtypeStruct((t, lanes), F32), jax.ShapeDtypeStruct((1, lanes), F32)],
                          compiler_params=pltpu.CompilerParams(vmem_limit_bytes=VMEM_LIMIT_BYTES))(dc, f, bias)


def _qrows(i):
    return pl.ds(pl.multiple_of(i * BQ, BQ), BQ)


def _krows(kc):
    return pl.ds(pl.multiple_of(kc * CS, CS), CS)


def _dot_split2_rhs01(x, m01):
    hi = x.astype(BF16)
    lo = (x - hi.astype(F32)).astype(BF16)
    d = lambda p: lax.dot_general(p, m01, (NN, ((), ())), preferred_element_type=F32)
    return d(hi) + d(lo)


def _live_rows(dchunk):
    return 0 if dchunk is None else dchunk * CS


def _diag_mask(dchunk, inclusive):
    shape = (BQ - _live_rows(dchunk), CS)
    return _iota2(shape, 1) <= _iota2(shape, 0) if inclusive else _iota2(shape, 1) < _iota2(shape, 0)


def _tail(x, r0, axis=0):
    return x if r0 == 0 else (x[r0:] if axis == 0 else x[:, r0:])


def _with_tail(old, tail, r0):
    return tail if r0 == 0 else jnp.concatenate([old[:r0], tail], axis=0)


def _walk_chunks(i, step, init, right_to_left):
    order = list(reversed(range(N_SUB))) if right_to_left else list(range(N_SUB))

    def diagonal(state):
        for dchunk in order:
            state = step(i * N_SUB + dchunk, state, dchunk)
        return state

    def group(n, state):
        base = ((i - 1 - n) if right_to_left else n) * N_SUB
        for dchunk in order:
            state = step(base + dchunk, state, None)
        return state

    if right_to_left:
        return lax.fori_loop(0, i, group, diagonal(init))
    return diagonal(lax.fori_loop(0, i, group, init))


PAIR = 2 * HEAD_DIM
N_PAIRS = N_HEADS // 2


def _pair_spec(t, first_block):
    return pl.BlockSpec((t, PAIR), lambda p, *_: (0, first_block + p))


def _head_lanes(shape):
    lane = _iota2(shape, len(shape) - 1)
    return [lane < HEAD_DIM, lane >= HEAD_DIM]


def _only_head(x, lanes_of_head):
    return jnp.where(lanes_of_head, x, jnp.zeros_like(x))


LOG2_E = 1.4426950408889634


def _sb_chunk_weights(q_h, k, later, carry, dchunk):
    z = _dot(q_h, k, NT) * LOG2_E
    lnb = -jnp.maximum(z, 0.0) - jnp.log2(1.0 + jnp.exp2(-jnp.abs(z)))
    lsz = lnb + z
    if dchunk is not None:
        lnb = jnp.where(_diag_mask(dchunk, False), lnb, 0.0)
    w = jnp.exp2(lsz + _dot_split2_rhs01(lnb, later) + carry)
    if dchunk is not None:
        w = jnp.where(_diag_mask(dchunk, False), w, 0.0)
    return w, lsz, lnb


def sb_pair_fwd(name, qkv):
    t = qkv.shape[0]

    def body(q_ref, k_ref, v_ref, o_ref):
        later = (_iota2((CS, CS), 0) > _iota2((CS, CS), 1)).astype(BF16)
        lanes = _head_lanes((BQ, PAIR))

        def qblock(i, _):
            q = q_ref[_qrows(i), :]
            q_heads = [_only_head(q, lanes[h]) for h in range(2)]

            def step(kc, state, dchunk):
                k = k_ref[_krows(kc), :]
                v = v_ref[_krows(kc), :]
                out = []
                r0 = _live_rows(dchunk)
                for h in range(2):
                    carry, acc = state[h]
                    w, _, lnb = _sb_chunk_weights(_tail(q_heads[h], r0), k, later, _tail(carry, r0), dchunk)
                    out.append((_with_tail(carry, _tail(carry, r0) + jnp.sum(lnb, axis=1, keepdims=True), r0),
                                _with_tail(acc, _tail(acc, r0) + _dot(w, v, NN), r0)))
                return tuple(out)

            zero = (jnp.zeros((BQ, 1), F32), jnp.zeros((BQ, PAIR), F32))
            (_, acc0), (_, acc1) = _walk_chunks(i, step, (zero, zero), True)
            o_ref[_qrows(i), :] = jnp.where(lanes[0], acc0, acc1).astype(BF16)
            return 0

        lax.fori_loop(0, t // BQ, qblock, 0)

    return pl.pallas_call(
        body, name=name, grid=(N_PAIRS,),
        in_specs=[_pair_spec(t, 0), _pair_spec(t, N_PAIRS), _pair_spec(t, 2 * N_PAIRS)],
        out_specs=_pair_spec(t, 0), out_shape=jax.ShapeDtypeStruct((t, WIDTH), BF16),
        compiler_params=_params(("parallel",)),
    )(qkv, qkv, qkv)


def _emit_dqkv(res, o_ref):
    o_ref[...] = res[pl.program_id(1)]


def _flush_transposed(acc, res, which):
    for kc in range(acc.shape[0]):
        res[which, kc * CS:(kc + 1) * CS, :] = acc[kc].T.astype(BF16)


def sb_pair_bwd(name, qkv, dy, dqkv):
    t = qkv.shape[0]
    nc = t // CS

    def body(q_ref, k_ref, v_ref, do_ref, _, o_ref, g_s, b_s, dkt_acc, dvt_acc, res):
        @pl.when(pl.program_id(1) == 0)
        def _():
            later = (_iota2((CS, CS), 0) > _iota2((CS, CS), 1)).astype(BF16)
            earlier = (_iota2((CS, CS), 0) < _iota2((CS, CS), 1)).astype(BF16)
            lanes = _head_lanes((BQ, PAIR))
            dkt_acc[...] = jnp.zeros_like(dkt_acc)
            dvt_acc[...] = jnp.zeros_like(dvt_acc)

            def qblock(i, _):
                q = q_ref[_qrows(i), :]
                do = do_ref[_qrows(i), :]
                zero = jnp.zeros((BQ, 1), F32)
                dqs = []
                for h in range(2):
                    q_h = _only_head(q, lanes[h])
                    do_h = _only_head(do, lanes[h])
                    qt_h = q_h.astype(F32).T.astype(BF16)
                    dot_h = do_h.astype(F32).T.astype(BF16)

                    def step1(kc, carry, dchunk, q_h=q_h, do_h=do_h, dot_h=dot_h):
                        k = k_ref[_krows(kc), :]
                        v = v_ref[_krows(kc), :]
                        r0 = _live_rows(dchunk)
                        live = _tail(carry, r0)
                        w, lsz, lnb = _sb_chunk_weights(_tail(q_h, r0), k, later, live, dchunk)
                        g_s[kc, r0:, :] = (w * _dot(_tail(do_h, r0), v, NT)).astype(BF16)
                        b_s[kc, r0:, :] = jnp.exp2(lsz).astype(BF16)
                        dvt_acc[kc] += _dot(_tail(dot_h, r0, axis=1), w, NN)
                        return _with_tail(carry, live + jnp.sum(lnb, axis=1, keepdims=True), r0)

                    _walk_chunks(i, step1, zero, True)

                    def step2(kc, state, dchunk, qt_h=qt_h):
                        k = k_ref[_krows(kc), :]
                        r0 = _live_rows(dchunk)
                        before, dq = state
                        g16 = g_s[kc, r0:, :]
                        g = g16.astype(F32)
                        beta = b_s[kc, r0:, :].astype(F32)
                        prefix = lax.dot_general(g16, earlier, (NN, ((), ())), preferred_element_type=F32) + _tail(before, r0)
                        dz = g * (1.0 - beta) - beta * prefix
                        if dchunk is not None:
                            dz = jnp.where(_diag_mask(dchunk, False), dz, 0.0)
                        dzb = dz.astype(BF16)
                        dkt_acc[kc] += _dot(_tail(qt_h, r0, axis=1), dzb, NN)
                        return (_with_tail(before, _tail(before, r0) + jnp.sum(g, axis=1, keepdims=True), r0),
                                _with_tail(dq, _tail(dq, r0) + _dot(dzb, k, NN), r0))

                    dqs.append(_walk_chunks(i, step2, (zero, jnp.zeros((BQ, PAIR), F32)), False)[1])
                res[0, _qrows(i), :] = (jnp.where(lanes[0], dqs[0], dqs[1]) * ATTN_SCALE).astype(BF16)
                return 0

            lax.fori_loop(0, t // BQ, qblock, 0)
            _flush_transposed(dkt_acc, res, 1)
            _flush_transposed(dvt_acc, res, 2)

        _emit_dqkv(res, o_ref)

    return pl.pallas_call(
        body, name=name, grid=(N_PAIRS, 3),
        in_specs=[_pair_spec(t, 0), _pair_spec(t, N_PAIRS), _pair_spec(t, 2 * N_PAIRS), _pair_spec(t, 0), _ANY],
        out_specs=pl.BlockSpec((t, PAIR), lambda p, s: (0, s * N_PAIRS + p)),
        out_shape=jax.ShapeDtypeStruct(dqkv.shape, BF16), input_output_aliases={4: 0},
        scratch_shapes=[pltpu.VMEM((nc, BQ, CS), BF16), pltpu.VMEM((nc, BQ, CS), BF16),
                        pltpu.VMEM((nc, PAIR, CS), F32), pltpu.VMEM((nc, PAIR, CS), F32), pltpu.VMEM((3, t, PAIR), BF16)],
        compiler_params=_params(("parallel", "arbitrary")),
    )(qkv, qkv, qkv, dy, dqkv)


def _gates_col_spec(t):
    return pl.BlockSpec((2, t, 1), lambda p, *_: (p, 0, 0))


def _gates_row_spec(nc):
    return pl.BlockSpec((2, nc, 1, CS), lambda p, *_: (p, 0, 0, 0))


def fox_pair_fwd(name, qkv, c_col, c_row):
    t = qkv.shape[0]

    def body(q_ref, k_ref, v_ref, cc_ref, cr_ref, o_ref, lse_ref):
        lanes = _head_lanes((BQ, PAIR))

        def qblock(i, _):
            q = q_ref[_qrows(i), :]
            q_heads = [_only_head(q, lanes[h]) for h in range(2)]
            ct = [cc_ref[h, _qrows(i), :] for h in range(2)]

            def step(kc, state, dchunk):
                k = k_ref[_krows(kc), :]
                v = v_ref[_krows(kc), :]
                out = []
                r0 = _live_rows(dchunk)
                for h in range(2):
                    m, l, acc = (_tail(a, r0) for a in state[h])
                    s = _dot(_tail(q_heads[h], r0), k, NT) + _tail(ct[h], r0) - cr_ref[h, kc]
                    if dchunk is not None:
                        s = jnp.where(_diag_mask(dchunk, True), s, NEG_BIG)
                    m_new = jnp.maximum(m, jnp.max(s, axis=1, keepdims=True))
                    alpha = jnp.exp(m - m_new)
                    p = jnp.exp(s - m_new)
                    if dchunk is not None:
                        p = jnp.where(_diag_mask(dchunk, True), p, 0.0)
                    new = (m_new, l * alpha + jnp.sum(p, axis=1, keepdims=True), acc * alpha + _dot(p, v, NN))
                    out.append(tuple(_with_tail(old, tail, r0) for old, tail in zip(state[h], new)))
                return tuple(out)

            init = (jnp.full((BQ, 1), NEG_BIG, F32), jnp.zeros((BQ, 1), F32), jnp.zeros((BQ, PAIR), F32))
            (m0, l0, acc0), (m1, l1, acc1) = _walk_chunks(i, step, (init, init), False)
            o_ref[_qrows(i), :] = jnp.where(lanes[0], acc0 / l0, acc1 / l1).astype(BF16)
            lse_ref[0, _qrows(i), :] = m0 + jnp.log(l0)
            lse_ref[1, _qrows(i), :] = m1 + jnp.log(l1)
            return 0

        lax.fori_loop(0, t // BQ, qblock, 0)

    return pl.pallas_call(
        body, name=name, grid=(N_PAIRS,),
        in_specs=[_pair_spec(t, 3 * N_PAIRS), _pair_spec(t, 4 * N_PAIRS), _pair_spec(t, 5 * N_PAIRS),
                  _gates_col_spec(t), _gates_row_spec(t // CS)],
        out_specs=[_pair_spec(t, 0), _gates_col_spec(t)],
        out_shape=[jax.ShapeDtypeStruct((t, WIDTH), BF16), jax.ShapeDtypeStruct((N_HEADS, t, 1), F32)],
        compiler_params=_params(("parallel",)),
    )(qkv, qkv, qkv, c_col, c_row)


def fox_pair_bwd(name, qkv, y, dy, lse, c_col, c_row, dqkv):
    t = qkv.shape[0]
    nc = t // CS

    def body(q_ref, k_ref, v_ref, o_in_ref, do_ref, lse_ref, cc_ref, cr_ref, _, o_ref, dcc_ref, dcr_ref,
             dkt_acc, dvt_acc, dcr_acc, res):
        @pl.when(pl.program_id(1) == 0)
        def _():
            lanes = _head_lanes((BQ, PAIR))
            dkt_acc[...] = jnp.zeros_like(dkt_acc)
            dvt_acc[...] = jnp.zeros_like(dvt_acc)
            dcr_acc[...] = jnp.zeros_like(dcr_acc)

            def qblock(i, _):
                q = q_ref[_qrows(i), :]
                do = do_ref[_qrows(i), :]
                q_heads = [_only_head(q, lanes[h]) for h in range(2)]
                do_heads = [_only_head(do, lanes[h]) for h in range(2)]
                qt_heads = [qh.astype(F32).T.astype(BF16) for qh in q_heads]
                dot_heads = [dh.astype(F32).T.astype(BF16) for dh in do_heads]
                prod = do.astype(F32) * o_in_ref[_qrows(i), :].astype(F32)
                delta = [jnp.sum(_only_head(prod, lanes[h]), axis=1, keepdims=True) for h in range(2)]
                ct = [cc_ref[h, _qrows(i), :] for h in range(2)]
                lse_i = [lse_ref[h, _qrows(i), :] for h in range(2)]

                def step(kc, state, dchunk):
                    k = k_ref[_krows(kc), :]
                    v = v_ref[_krows(kc), :]
                    out = []
                    r0 = _live_rows(dchunk)
                    for h in range(2):
                        dq, dct = state[h]
                        s = _dot(_tail(q_heads[h], r0), k, NT) + _tail(ct[h], r0) - cr_ref[h, kc]
                        p = jnp.exp(s - _tail(lse_i[h], r0))
                        if dchunk is not None:
                            p = jnp.where(_diag_mask(dchunk, True), p, 0.0)
                        ds = p * (_dot(_tail(do_heads[h], r0), v, NT) - _tail(delta[h], r0))
                        dvt_acc[kc] += _dot(_tail(dot_heads[h], r0, axis=1), p, NN)
                        dsb = ds.astype(BF16)
                        dkt_acc[kc] += _dot(_tail(qt_heads[h], r0, axis=1), dsb, NN)
                        dcr_acc[h, kc] -= jnp.sum(ds, axis=0, keepdims=True)
                        out.append((_with_tail(dq, _tail(dq, r0) + _dot(dsb, k, NN), r0),
                                    _with_tail(dct, _tail(dct, r0) + jnp.sum(ds, axis=1, keepdims=True), r0)))
                    return tuple(out)

                zero = (jnp.zeros((BQ, PAIR), F32), jnp.zeros((BQ, 1), F32))
                (dq0, dct0), (dq1, dct1) = _walk_chunks(i, step, (zero, zero), False)
                res[0, _qrows(i), :] = (jnp.where(lanes[0], dq0, dq1) * ATTN_SCALE).astype(BF16)
                lane = _iota2((BQ, PAIR), 1)
                dcc_ref[_qrows(i), :] = jnp.where(lane == 0, dct0, jnp.where(lane == 1, dct1, 0.0))
                return 0

            lax.fori_loop(0, t // BQ, qblock, 0)
            _flush_transposed(dkt_acc, res, 1)
            _flush_transposed(dvt_acc, res, 2)
            dcr_ref[...] = dcr_acc[...]

        _emit_dqkv(res, o_ref)

    return pl.pallas_call(
        body, name=name, grid=(N_PAIRS, 3),
        in_specs=[_pair_spec(t, 3 * N_PAIRS), _pair_spec(t, 4 * N_PAIRS), _pair_spec(t, 5 * N_PAIRS), _pair_spec(t, 0),
                  _pair_spec(t, 0), _gates_col_spec(t), _gates_col_spec(t), _gates_row_spec(nc), _ANY],
        out_specs=[pl.BlockSpec((t, PAIR), lambda p, s: (0, (3 + s) * N_PAIRS + p)),
                   pl.BlockSpec((None, t, PAIR), lambda p, s: (p, 0, 0)), _gates_row_spec(nc)],
        out_shape=[jax.ShapeDtypeStruct(dqkv.shape, BF16), jax.ShapeDtypeStruct((N_PAIRS, t, PAIR), F32),
                   jax.ShapeDtypeStruct((N_HEADS, nc, 1, CS), F32)],
        input_output_aliases={8: 0},
        scratch_shapes=[pltpu.VMEM((nc, PAIR, CS), F32), pltpu.VMEM((nc, PAIR, CS), F32), pltpu.VMEM((2, nc, 1, CS), F32),
                        pltpu.VMEM((3, t, PAIR), BF16)],
        compiler_params=_params(("parallel", "arbitrary")),
    )(qkv, qkv, qkv, y, dy, lse, c_col, c_row, dqkv)


def loss_head(name, x, g, target):
    t, d = x.shape
    tr = _row_tile(t, 256)

    def body(x_ref, g_ref, t_ref, dx_ref, gn_ref, loss_ref):
        xv = x_ref[...]
        r = lax.rsqrt(jnp.mean(xv * xv, axis=-1, keepdims=True) + RMS_EPS)
        xhat = xv * r
        gv = g_ref[...]
        err = xhat * gv - t_ref[...]
        part_loss = 0.5 * jnp.sum(jnp.mean(err * err, axis=-1, keepdims=True), axis=0, keepdims=True)
        dy = err * (1.0 / d)
        dyg = dy * gv
        dx_ref[...] = r * (dyg - xhat * jnp.mean(dyg * xhat, axis=-1, keepdims=True))
        part_g = jnp.sum(dy * xhat, axis=0, keepdims=True)

        @pl.when(pl.program_id(0) == 0)
        def _():
            gn_ref[...] = part_g
            loss_ref[...] = part_loss

        @pl.when(pl.program_id(0) != 0)
        def _():
            gn_ref[...] += part_g
            loss_ref[...] += part_loss

    row = pl.BlockSpec((tr, d), lambda i: (i, 0))
    return pl.pallas_call(
        body, name=name, grid=(t // tr,),
        in_specs=[row, pl.BlockSpec((1, d), lambda i: (0, 0)), row],
        out_specs=[row, pl.BlockSpec((1, d), lambda i: (0, 0)), pl.BlockSpec((1, 1), lambda i: (0, 0))],
        out_shape=[jax.ShapeDtypeStruct((t, d), F32), jax.ShapeDtypeStruct((1, d), F32), jax.ShapeDtypeStruct((1, 1), F32)],
        compiler_params=_params(("arbitrary",)),
    )(x, g, target)


def _place():
    return lax.axis_index("x"), lax.axis_index("y"), lax.axis_index("c")


def _other_chips(x, y):
    return [(1 - x, y), (x, 1 - y), (1 - x, 1 - y)]


def _half(ref, c, rows):
    if rows % 32 == 0:
        return ref.at[:, pl.ds(c * (rows // 2), rows // 2), :]
    cols = ref.shape[2]
    return ref.at[:, :, pl.ds(c * (cols // 2), cols // 2)]


_ANY = pl.BlockSpec(memory_space=pl.ANY)


_HBM = pl.BlockSpec(memory_space=pltpu.HBM)
_SEM = pl.BlockSpec(memory_space=pltpu.SEMAPHORE)
_DATAFLOW = pltpu.SideEffectType.DATAFLOW_SIDE_EFFECTING


def _in_hbm(a):
    return pltpu.with_memory_space_constraint(a, pltpu.HBM)


def _gather_ici_copies(bufs, send_sems, recv_sems, arrivals):
    x, y, c = _place()
    me = 2 * x + y
    copies = []
    for i, buf in enumerate(bufs):
        rows = buf.shape[2]
        for j, (qx, qy) in enumerate(_other_chips(x, y)):
            block = _half(buf.at[2 * qx + qy if arrivals else me], c, rows)
            copies.append(pltpu.make_async_remote_copy(
                src_ref=block, dst_ref=block, send_sem=send_sems.at[3 * i + j], recv_sem=recv_sems.at[3 * i + j],
                device_id=(qx, qy, c), device_id_type=MESH))
    return copies


def gather_ici_start(name, bufs, after):
    n = len(bufs)

    def body(*refs):
        ins = refs[:n]
        send_sems, recv_sems = refs[n + 1], refs[n + 2]
        token = refs[-1]
        for send in _gather_ici_copies(ins, send_sems, recv_sems, False):
            send.start()
        token[...] = jnp.zeros_like(token)

    res = pl.pallas_call(
        body, name=name,
        out_shape=(pltpu.SemaphoreType.DMA((3 * n,)), pltpu.SemaphoreType.DMA((3 * n,)), *[pltpu.HBM(b.shape, b.dtype) for b in bufs],
                   jax.ShapeDtypeStruct((8, 128), F32)),
        in_specs=[_HBM] * n + [_ANY], out_specs=(_SEM, _SEM, *[_HBM] * n, pl.BlockSpec(memory_space=pltpu.VMEM)),
        input_output_aliases={i: 2 + i for i in range(n)},
        compiler_params=pltpu.CompilerParams(has_side_effects=_DATAFLOW),
    )(*[_in_hbm(b) for b in bufs], after)
    return res[0], res[1], list(res[2:2 + n]), res[-1]


def gather_ici_wait(name, send_sems, recv_sems, bufs, after):
    n = len(bufs)

    def body(*refs):
        ins = refs[:n]
        send_sems_ref, recv_sems_ref = refs[n], refs[n + 1]
        for send in _gather_ici_copies(ins, send_sems_ref, recv_sems_ref, False):
            send.wait_send()
        for recv in _gather_ici_copies(ins, send_sems_ref, recv_sems_ref, True):
            recv.wait_recv()

    return pl.pallas_call(
        body, name=name, out_shape=tuple(pltpu.HBM(b.shape, b.dtype) for b in bufs),
        in_specs=[_HBM] * n + [_SEM, _SEM, _ANY], out_specs=tuple([_HBM] * n),
        input_output_aliases={i: i for i in range(n)},
        compiler_params=pltpu.CompilerParams(has_side_effects=_DATAFLOW),
    )(*bufs, send_sems, recv_sems, after)


def gather_forward(name, bufs):
    n = len(bufs)

    def body(*refs):
        outs = refs[n:2 * n]
        send_sems, recv_sems = refs[2 * n:]
        x, y, c = _place()
        sibling = (x, y, 1 - c)
        sends = []
        for i in range(n):
            rows = outs[i].shape[2]
            for j, (qx, qy) in enumerate(_other_chips(x, y)):
                block = _half(outs[i].at[2 * qx + qy], c, rows)
                fw = pltpu.make_async_remote_copy(
                    src_ref=block, dst_ref=block, send_sem=send_sems.at[3 * i + j], recv_sem=recv_sems.at[3 * i + j],
                    device_id=sibling, device_id_type=MESH)
                fw.start()
                sends.append(fw)
        for i in range(n):
            rows = outs[i].shape[2]
            for j, (qx, qy) in enumerate(_other_chips(x, y)):
                block = _half(outs[i].at[2 * qx + qy], 1 - c, rows)
                pltpu.make_async_remote_copy(
                    src_ref=block, dst_ref=block, send_sem=send_sems.at[3 * i + j], recv_sem=recv_sems.at[3 * i + j],
                    device_id=sibling, device_id_type=MESH).wait_recv()
        for fw in sends:
            fw.wait_send()

    return pl.pallas_call(
        body, name=name, in_specs=[_ANY] * n, out_specs=[_ANY] * n,
        out_shape=[jax.ShapeDtypeStruct(b.shape, b.dtype) for b in bufs],
        input_output_aliases={i: i for i in range(n)},
        scratch_shapes=[pltpu.SemaphoreType.DMA((3 * n,)), pltpu.SemaphoreType.DMA((3 * n,))],
        compiler_params=pltpu.CompilerParams(has_side_effects=True),
    )(*bufs)


def _between_chips_copies(parts, lands, send_sems, recv_sems):
    x, y, c = _place()
    copies = []
    for i, (part, land) in enumerate(zip(parts, lands)):
        for j, (qx, qy) in enumerate(_other_chips(x, y)):
            copies.append(pltpu.make_async_remote_copy(
                src_ref=part.at[2 * qx + qy], dst_ref=land.at[j], send_sem=send_sems.at[3 * i + j], recv_sem=recv_sems.at[3 * i + j],
                device_id=(qx, qy, c), device_id_type=MESH))
    return copies


def between_chips_start(name, parts):
    n = len(parts)
    lands = [lax.empty((N_CHIPS - 1,) + p.shape[1:], p.dtype) for p in parts]

    def body(*refs):
        send_sems, recv_sems = refs[2 * n], refs[2 * n + 1]
        token = refs[-1]
        for cp in _between_chips_copies(refs[:n], refs[n:2 * n], send_sems, recv_sems):
            cp.start()
        token[...] = jnp.zeros_like(token)

    res = pl.pallas_call(
        body, name=name,
        out_shape=(pltpu.SemaphoreType.DMA((3 * n,)), pltpu.SemaphoreType.DMA((3 * n,)),
                   *[pltpu.HBM(a.shape, a.dtype) for a in parts + lands], jax.ShapeDtypeStruct((8, 128), F32)),
        in_specs=[_HBM] * (2 * n), out_specs=(_SEM, _SEM, *[_HBM] * (2 * n), pl.BlockSpec(memory_space=pltpu.VMEM)),
        input_output_aliases={i: 2 + i for i in range(2 * n)},
        compiler_params=pltpu.CompilerParams(has_side_effects=_DATAFLOW),
    )(*[_in_hbm(a) for a in parts + lands])
    return res[0], res[1], list(res[2:2 + n]), list(res[2 + n:2 + 2 * n]), res[-1]


def between_chips_wait(name, send_sems, recv_sems, parts, lands, after):
    n = len(parts)

    def body(*refs):
        for cp in _between_chips_copies(refs[:n], refs[n:2 * n], refs[2 * n], refs[2 * n + 1]):
            cp.wait_send()
            cp.wait_recv()

    res = pl.pallas_call(
        body, name=name, out_shape=tuple(pltpu.HBM(a.shape, a.dtype) for a in parts + lands),
        in_specs=[_HBM] * (2 * n) + [_SEM, _SEM, _ANY], out_specs=tuple([_HBM] * (2 * n)),
        input_output_aliases={i: i for i in range(2 * n)},
        compiler_params=pltpu.CompilerParams(has_side_effects=_DATAFLOW),
    )(*parts, *lands, send_sems, recv_sems, after)
    return list(res[:n]), list(res[n:])


def exchange_start(name, arrays, n_copies, copies, after=()):
    n = len(arrays)

    def body(*refs):
        send_sems, recv_sems = refs[n + len(after)], refs[n + len(after) + 1]
        for cp in copies(refs[:n], send_sems, recv_sems):
            cp.start()
        refs[-1][...] = jnp.zeros_like(refs[-1])

    res = pl.pallas_call(
        body, name=name,
        out_shape=(pltpu.SemaphoreType.DMA((n_copies,)), pltpu.SemaphoreType.DMA((n_copies,)),
                   *[pltpu.HBM(a.shape, a.dtype) for a in arrays], jax.ShapeDtypeStruct((8, 128), F32)),
        in_specs=[_HBM] * n + [_ANY] * len(after),
        out_specs=(_SEM, _SEM, *[_HBM] * n, pl.BlockSpec(memory_space=pltpu.VMEM)),
        input_output_aliases={i: 2 + i for i in range(n)},
        compiler_params=pltpu.CompilerParams(has_side_effects=_DATAFLOW),
    )(*[_in_hbm(a) for a in arrays], *after)
    return res[0], res[1], list(res[2:2 + n]), res[-1]


def exchange_wait(name, send_sems, recv_sems, arrays, copies, after):
    n = len(arrays)

    def body(*refs):
        for cp in copies(refs[:n], refs[n], refs[n + 1]):
            cp.wait_send()
            cp.wait_recv()

    return list(pl.pallas_call(
        body, name=name, out_shape=tuple(pltpu.HBM(a.shape, a.dtype) for a in arrays),
        in_specs=[_HBM] * n + [_SEM, _SEM, _ANY], out_specs=tuple([_HBM] * n),
        input_output_aliases={i: i for i in range(n)},
        compiler_params=pltpu.CompilerParams(has_side_effects=_DATAFLOW),
    )(*arrays, send_sems, recv_sems, after))


def _to_sibling_copies(n):
    def copies(refs, send_sems, recv_sems):
        x, y, c = _place()
        out = []
        for i in range(n):
            rows = refs[i].shape[2]
            out.append(pltpu.make_async_remote_copy(
                src_ref=refs[i].at[:, :, pl.ds((1 - c) * (rows // 2), rows // 2), :], dst_ref=refs[n + i],
                send_sem=send_sems.at[i], recv_sem=recv_sems.at[i], device_id=(x, y, 1 - c), device_id_type=MESH))
        return out
    return copies


def _share_copies(n):
    def copies(refs, send_sems, recv_sems):
        x, y, c = _place()
        out = []
        for i in range(n):
            mine = _half(refs[i], c, refs[i].shape[1])
            out.append(pltpu.make_async_remote_copy(
                src_ref=mine, dst_ref=mine, send_sem=send_sems.at[i], recv_sem=recv_sems.at[i],
                device_id=(x, y, 1 - c), device_id_type=MESH))
        return out
    return copies


def pair_sum(name, grad, recv, c):
    ns, na, rh, cols = recv.shape
    tr = _row_tile(rh, 256) if rh % 256 == 0 else rh
    nt = rh // tr

    def body(c_ref, g_ref, r_ref, o_ref):
        o_ref[...] = (g_ref[...].astype(F32) + r_ref[...].astype(F32)).astype(BF16)

    blk = (None, None, tr, cols)
    return pl.pallas_call(
        body, name=name,
        grid_spec=pltpu.PrefetchScalarGridSpec(
            num_scalar_prefetch=1, grid=(ns, na, nt),
            in_specs=[pl.BlockSpec(blk, lambda s, a, r, c_ref: (s, a, c_ref[0] * nt + r, 0)),
                      pl.BlockSpec(blk, lambda s, a, r, c_ref: (s, a, r, 0))],
            out_specs=pl.BlockSpec(blk, lambda s, a, r, c_ref: (s, a, r, 0))),
        out_shape=jax.ShapeDtypeStruct(recv.shape, BF16),
        compiler_params=_params(("parallel", "parallel", "parallel")),
    )(c, grad, recv)


def chip_sum(name, parts, landed, place):
    _, na, rh, cols = parts.shape
    tr = _row_tile(rh, 256) if rh % 256 == 0 else rh
    nt = rh // tr

    def body(place_ref, p_ref, l_ref, o_ref):
        total = p_ref[...].astype(F32)
        for s in range(N_CHIPS - 1):
            total = total + l_ref[s].astype(F32)
        o_ref[...] = total

    return pl.pallas_call(
        body, name=name,
        grid_spec=pltpu.PrefetchScalarGridSpec(
            num_scalar_prefetch=1, grid=(na, nt),
            in_specs=[pl.BlockSpec((None, None, tr, cols), lambda a, r, pr: (pr[1], a, r, 0)),
                      pl.BlockSpec((N_CHIPS - 1, None, tr, cols), lambda a, r, pr: (0, a, r, 0))],
            out_specs=pl.BlockSpec((None, tr, cols), lambda a, r, pr: (a, pr[0] * nt + r, 0))),
        out_shape=jax.ShapeDtypeStruct((na, 2 * rh, cols), F32),
        compiler_params=_params(("parallel", "parallel")),
    )(place, parts, landed)


def reduce_scatter_1(tag, grads):
    n = len(grads)
    lands = [lax.empty((g.shape[0], g.shape[1], g.shape[2] // 2, g.shape[3]), g.dtype) for g in grads]
    send_sems, recv_sems, arrays, token = exchange_start(f"{tag}_to_sibling_start", list(grads) + lands, n, _to_sibling_copies(n))
    return (send_sems, recv_sems, arrays), token


def reduce_scatter_2(tag, state, place, after):
    send_sems, recv_sems, arrays = state
    n = len(arrays) // 2
    arrays = exchange_wait(f"{tag}_to_sibling_wait", send_sems, recv_sems, arrays, _to_sibling_copies(n), after)
    parts = [pair_sum(f"{tag}_pair_sum{i}", g, r, place) for i, (g, r) in enumerate(zip(arrays[:n], arrays[n:]))]
    send_sems, recv_sems, parts, lands, token = between_chips_start(f"{tag}_between_chips_start", parts)
    return (send_sems, recv_sems, parts, lands), token


def reduce_scatter_3(tag, state, place, after):
    send_sems, recv_sems, parts, lands = state
    parts, landed = between_chips_wait(f"{tag}_between_chips_wait", send_sems, recv_sems, parts, lands, after)
    halves = [chip_sum(f"{tag}_chip_sum{i}", p, l, place) for i, (p, l) in enumerate(zip(parts, landed))]
    send_sems, recv_sems, halves, token = exchange_start(f"{tag}_share_start", halves, len(halves), _share_copies(len(halves)))
    return (send_sems, recv_sems, halves), token


def reduce_scatter_4(tag, state, after):
    send_sems, recv_sems, halves = state
    return exchange_wait(f"{tag}_share_wait", send_sems, recv_sems, halves, _share_copies(len(halves)), after)


def _adamw_math(w, g, m, v):
    m = ADAM_B1 * m + (1.0 - ADAM_B1) * g
    v = ADAM_B2 * v + (1.0 - ADAM_B2) * (g * g)
    m_hat = m / (1.0 - ADAM_B1 ** ADAM_STEP)
    v_hat = v / (1.0 - ADAM_B2 ** ADAM_STEP)
    delta = -ADAM_LR * (m_hat / (jnp.sqrt(v_hat) + ADAM_EPS) + ADAM_WD * w)
    return delta, m, v


def adamw(name, w, g, m, v, after=()):
    rows, cols = w.shape
    tr = _row_tile(rows, 256) if rows % 256 == 0 else rows // 2

    def body(w_ref, g_ref, m_ref, v_ref, d_ref, mo_ref, vo_ref):
        d_ref[...], mo_ref[...], vo_ref[...] = _adamw_math(w_ref[...], g_ref[...], m_ref[...], v_ref[...])

    blk = pl.BlockSpec((tr, cols), lambda i: (i, 0))
    return pl.pallas_call(
        _ordered_after(body, 4, len(after)), name=name, grid=(rows // tr,), in_specs=[blk] * 4 + [_ANY] * len(after),
        out_specs=[blk] * 3, out_shape=[jax.ShapeDtypeStruct(w.shape, F32)] * 3, compiler_params=_params(("parallel",)),
    )(w, g, m, v, *after)


def adamw_rows(name, w, g, m, v, after=()):
    rows, _, cols = w.shape
    tr = next(r for r in (128, 110, 64, 32, 16, 8, 1) if rows % r == 0)

    def body(w_ref, g_ref, m_ref, v_ref, d_ref, mo_ref, vo_ref):
        d_ref[...], mo_ref[...], vo_ref[...] = _adamw_math(w_ref[...], g_ref[...], m_ref[...], v_ref[...])

    blk = pl.BlockSpec((tr, 1, cols), lambda i: (i, 0, 0))
    return pl.pallas_call(
        _ordered_after(body, 4, len(after)), name=name, grid=(rows // tr,), in_specs=[blk] * 4 + [_ANY] * len(after),
        out_specs=[blk] * 3, out_shape=[jax.ShapeDtypeStruct(w.shape, F32)] * 3, compiler_params=_params(("parallel",)),
    )(w, g, m, v, *after)


def adamw_stacked(name, ws, g, ms, vs, after=()):
    n = len(ws)
    rows, cols = ws[0].shape
    tr = next(r for r in (128, 88, 64, 32, 16, 8) if rows % r == 0)

    def body(*refs):
        w_refs, m_refs, v_refs, g_ref = refs[:n], refs[n:2 * n], refs[2 * n:3 * n], refs[3 * n]
        outs = refs[3 * n + 1:]
        for i in range(n):
            outs[i][...], outs[n + i][...], outs[2 * n + i][...] = _adamw_math(
                w_refs[i][...], g_ref[i], m_refs[i][...], v_refs[i][...])

    blk = pl.BlockSpec((tr, cols), lambda r: (r, 0))
    res = pl.pallas_call(
        _ordered_after(body, 3 * n + 1, len(after)), name=name, grid=(rows // tr,),
        in_specs=[blk] * (3 * n) + [pl.BlockSpec((n, tr, cols), lambda r: (0, r, 0))] + [_ANY] * len(after),
        out_specs=[blk] * (3 * n),
        out_shape=[jax.ShapeDtypeStruct((rows, cols), F32)] * (3 * n), compiler_params=_params(("parallel",)),
    )(*ws, *ms, *vs, g, *after)
    return res[:n], res[n:2 * n], res[2 * n:]


def small_allreduce_adamw(name, g_part, w, m, v):
    rows, cols = g_part.shape

    def body(g_ref, w_ref, m_ref, v_ref, sum_ref, d_ref, mo_ref, vo_ref, land, send_sems, recv_sems):
        x, y, c = _place()
        me = 4 * x + 2 * y + c
        land[me] = g_ref[...]
        copies = []
        for r in range(1, 8):
            peer = (x ^ (r >> 2), y ^ ((r >> 1) & 1), c ^ (r & 1))
            rc = pltpu.make_async_remote_copy(
                src_ref=g_ref, dst_ref=land.at[me], send_sem=send_sems.at[r - 1], recv_sem=recv_sems.at[r - 1],
                device_id=peer, device_id_type=MESH)
            rc.start()
            copies.append(rc)
        for rc in copies:
            rc.wait()
        total = land[0]
        for s in range(1, 8):
            total = total + land[s]
        sum_ref[...] = total
        d_ref[...], mo_ref[...], vo_ref[...] = _adamw_math(w_ref[...], total, m_ref[...], v_ref[...])

    vmem = pl.BlockSpec(memory_space=pltpu.VMEM)
    return pl.pallas_call(
        body, name=name, in_specs=[vmem] * 4, out_specs=[vmem] * 4,
        out_shape=[jax.ShapeDtypeStruct((rows, cols), F32)] * 4,
        scratch_shapes=[pltpu.VMEM((8, rows, cols), F32), pltpu.SemaphoreType.DMA((7,)), pltpu.SemaphoreType.DMA((7,))],
        compiler_params=pltpu.CompilerParams(has_side_effects=True),
    )(g_part, w, m, v)


def kernel(x, norm_ffn1, w_ffn1_gate, w_ffn1_up, w_ffn1_down, norm_mix, w_in, b_forget, w_gate, b_gate, w_up_a, w_up_b, w_out, norm_ffn2, w_ffn2_gate, w_ffn2_up, w_ffn2_down, norm_final, loss_target, m_norm_ffn1, m_w_ffn1_gate, m_w_ffn1_up, m_w_ffn1_down, m_norm_mix, m_w_in, m_b_forget, m_w_gate, m_b_gate, m_w_up_a, m_w_up_b, m_w_out, m_norm_ffn2, m_w_ffn2_gate, m_w_ffn2_up, m_w_ffn2_down, m_norm_final, v_norm_ffn1, v_w_ffn1_gate, v_w_ffn1_up, v_w_ffn1_down, v_norm_mix, v_w_in, v_b_forget, v_w_gate, v_b_gate, v_w_up_a, v_w_up_b, v_w_out, v_norm_ffn2, v_w_ffn2_gate, v_w_ffn2_up, v_w_ffn2_down, v_norm_final):
    t, d = x.shape[1], x.shape[2]
    in4 = w_in.shape[2]
    gate4 = w_gate.shape[2]
    in_cols = N_CHIPS * in4
    n_forget = in_cols - QKV_COLS
    assert w_up_a.shape[1] == WIDTH and d == 2 * WIDTH and n_forget == N_HEADS and t % BQ == 0
    chip = 2 * lax.axis_index("x") + lax.axis_index("y")
    c_arr = jnp.stack([lax.axis_index("c"), chip]).astype(jnp.int32)
    x2d = x[0]
    target = loss_target[0]

    def slot(shard):
        return lax.dynamic_update_slice(lax.empty((N_CHIPS,) + shard.shape, BF16), shard.astype(BF16)[None], (chip, 0, 0, 0))

    def ffn_views(wg, wu, wd):
        return [wg[0].T, wu[0].T, wd[0]]

    ffn1_w, ffn1_m, ffn1_v = (ffn_views(w_ffn1_gate, w_ffn1_up, w_ffn1_down), ffn_views(m_w_ffn1_gate, m_w_ffn1_up, m_w_ffn1_down),
                              ffn_views(v_w_ffn1_gate, v_w_ffn1_up, v_w_ffn1_down))
    ffn2_w, ffn2_m, ffn2_v = (ffn_views(w_ffn2_gate, w_ffn2_up, w_ffn2_down), ffn_views(m_w_ffn2_gate, m_w_ffn2_up, m_w_ffn2_down),
                              ffn_views(v_w_ffn2_gate, v_w_ffn2_up, v_w_ffn2_down))
    in_pad = -(-in4 // 32) * 32
    in_sh = slot(w_in[0].T[None])
    gt_sh = slot(w_gate[0].T[None])
    up_sh = slot(jnp.concatenate([w_up_a[0], w_up_b[0]], axis=0)[None])
    wo_sh = slot(w_out)
    f1_send, f1_recv, f1_bufs, f1_token = gather_ici_start("gather_ffn1_start", [slot(jnp.stack(ffn1_w))], norm_ffn1)
    mx_send, mx_recv, mx_bufs, mx_token = gather_ici_start("gather_mixer_start", [in_sh, gt_sh, up_sh, wo_sh], f1_token)
    f2_send, f2_recv, f2_bufs, f2_token = gather_ici_start("gather_ffn2_start", [slot(jnp.stack(ffn2_w))], mx_token)

    normed1 = rms_fwd("ffn1_rms", x2d, norm_ffn1, after=(f2_token,))
    (w3_1,) = gather_forward("gather_ffn1_forward", gather_ici_wait("gather_ffn1_wait", f1_send, f1_recv, f1_bufs, normed1[0]))
    x1, saved1 = ffn_forward("ffn1", x2d, norm_ffn1, w3_1, normed=normed1)
    w_in_g, w_gate_g, wup, wo = gather_forward(
        "gather_mixer_forward", gather_ici_wait("gather_mixer_wait", mx_send, mx_recv, mx_bufs, x1))
    wup = wup[:, 0]
    w_in_t = w_in_g.reshape(in_cols, d)
    w_gate_t = w_gate_g.reshape(2 * d, d)
    w_f_t = jnp.pad(w_in_t[QKV_COLS:], ((0, QB - n_forget), (0, 0)))
    wo_full = wo.reshape(d, d)
    b_forget_row = jnp.pad(b_forget, ((0, 0), (0, QB - n_forget)))

    h2, rstd2 = rms_fwd("mix_rms", x1, norm_mix)
    qkv = proj("mix_proj_qkv", h2, w_in_t, jnp.zeros((1, QKV_COLS), F32), 0, QKV_COLS, WIDTH, BF16, scaled_tiles=(0, 3))
    pc = proj("mix_proj_gates", h2, w_gate_t, b_gate, 0, 2 * d, WIDTH, F32)
    f_logit = proj("mix_proj_forget", h2, w_f_t, jnp.zeros((1, QB), F32), 0, QB, QB, F32)
    c_cum = fox_prep("fox_prep", f_logit, b_forget_row)
    c_heads = c_cum[:, :N_HEADS].T
    c_col = c_heads[:, :, None]
    c_row = c_heads.reshape(N_HEADS, t // CS, 1, CS)
    ya = sb_pair_fwd("sb_fwd", qkv)
    yb, lse = fox_pair_fwd("fox_fwd", qkv, c_col, c_row)
    ua, ub, mixed = mix_fwd("mix_fwd", ya, yb, wup, pc)
    x2 = mm_residual("mix_out", mixed[None], wo_full[None], pl.BlockSpec((1, d, d), lambda m: (0, 0, 0)), x1, 1.0)
    (w3_2,) = gather_forward("gather_ffn2_forward", gather_ici_wait("gather_ffn2_wait", f2_send, f2_recv, f2_bufs, x2))
    x3, saved2 = ffn_forward("ffn2", x2, norm_ffn2, w3_2)
    dx3, gn_final, loss_part = loss_head("loss_head", x3, norm_final[None], target)

    g_w3_2, dab2 = ffn_backward_weights("ffn2", dx3, saved2, w3_2)
    rs_ffn2, token = reduce_scatter_1("rs_ffn2", [g_w3_2])
    dx2, gn_ffn2 = ffn_backward_input("ffn2", dx3, saved2, dab2, norm_ffn2, w3_2, after=(token,))
    rs_ffn2, rs_ffn2_token = reduce_scatter_2("rs_ffn2", rs_ffn2, c_arr, dx2)

    dua, dub, dpa, dpb, gba, gbb = mix_bwd("mix_bwd", dx2, wo_full, pc, ua, ub, after=(rs_ffn2_token,))
    g_bgate = jnp.concatenate([gba, gbb], axis=1)
    g_wo = mm_plain("wgrad_out", mixed, dx2, TN, BF16, tk_target=1024)
    dya = up_bwd("dya", dua, wup, 0)
    dyb = up_bwd("dyb", dub, wup, 1)
    g_up = up_wgrad("wgrad_up", ya, yb, dua, dub)
    dqkv = sb_pair_bwd("sb_bwd", qkv, dya, lax.empty((t, QKV_COLS), BF16))
    dqkv, dcc, dcr = fox_pair_bwd("fox_bwd", qkv, yb, dyb, lse, c_col, c_row, dqkv)
    dc = dcc[:, :, :2].transpose(1, 0, 2).reshape(t, N_HEADS) + dcr.reshape(N_HEADS, t).T
    df, g_bf = fox_gate_bwd("fox_gate_bwd", jnp.pad(dc, ((0, 0), (0, QB - N_HEADS))), f_logit, b_forget_row)
    df = df.astype(BF16)
    g_qkv_t = wgrad_cat("wgrad_qkv", h2, dqkv)
    g_f_t = wgrad_cat("wgrad_forget", h2, df)
    g_gate_t = jnp.stack([wgrad_cat("wgrad_gate_a", h2, dpa), wgrad_cat("wgrad_gate_b", h2, dpb)])
    g_in_t = jnp.concatenate([g_qkv_t, g_f_t[:n_forget]], axis=0).reshape(N_CHIPS, in4, d)
    rs_mixer, token = reduce_scatter_1(
        "rs_mixer", [jnp.pad(g_in_t, ((0, 0), (0, in_pad - in4), (0, 0)))[:, None], g_gate_t.reshape(N_CHIPS, 1, gate4, d),
                     g_up[:, None], g_wo.reshape(N_CHIPS, 1, d // N_CHIPS, d)])
    dx1, gn_mix = mixer_dh("mix_dh", [(dqkv, w_in_t, 0), (dpa, w_gate_t, 0), (dpb, w_gate_t, 1), (df, w_f_t, 0)],
                           x1, rstd2, norm_mix, dx2, after=(token,))
    rs_mixer, rs_mixer_token = reduce_scatter_2("rs_mixer", rs_mixer, c_arr, dx1)
    rs_ffn2, rs_ffn2_token = reduce_scatter_3("rs_ffn2", rs_ffn2, c_arr, dx1)

    g_w3_1, dab1 = ffn_backward_weights("ffn1", dx1, saved1, w3_1, after=(rs_mixer_token, rs_ffn2_token))
    rs_ffn1, token = reduce_scatter_1("rs_ffn1", [g_w3_1])
    dx0, gn_ffn1 = ffn_backward_input("ffn1", dx1, saved1, dab1, norm_ffn1, w3_1, after=(token,))
    rs_ffn1, rs_ffn1_token = reduce_scatter_2("rs_ffn1", rs_ffn1, c_arr, dx0)
    rs_mixer, rs_mixer_token = reduce_scatter_3("rs_mixer", rs_mixer, c_arr, dx0)
    (s_w3_2,) = reduce_scatter_4("rs_ffn2", rs_ffn2, dx0)

    def pack_small(n1, nm, n2, nf, bg, bf, last):
        return jnp.concatenate([n1, nm, n2, nf, bg.reshape(2, d), jnp.pad(bf, ((0, 0), (0, d - n_forget))), last], axis=0)

    zero_row = jnp.zeros((1, d), F32)
    g_small = pack_small(gn_ffn1, gn_mix, gn_ffn2, gn_final, g_bgate, g_bf[:, :n_forget], jnp.pad(loss_part, ((0, 0), (0, d - 1))))
    w_small = pack_small(norm_ffn1, norm_mix, norm_ffn2, norm_final[None], b_gate, b_forget, zero_row)
    m_small = pack_small(m_norm_ffn1, m_norm_mix, m_norm_ffn2, m_norm_final[None], m_b_gate, m_b_forget, zero_row)
    v_small = pack_small(v_norm_ffn1, v_norm_mix, v_norm_ffn2, v_norm_final[None], v_b_gate, v_b_forget, zero_row)
    smalls = small_allreduce_adamw("small_allreduce_adamw", g_small, w_small, m_small, v_small)

    def unpack_small(p):
        return {"norm_ffn1": p[0:1], "norm_mix": p[1:2], "norm_ffn2": p[2:3], "norm_final": p[3], "b_gate": p[4:6].reshape(1, 2 * d),
                "b_forget": p[6:7, :n_forget]}

    loss = smalls[0][7, 0]
    small_out = [unpack_small(p) for p in smalls]

    big_out = {}

    def adamw_ffn(tag, s_w3, ws, ms, vs, after):
        deltas, new_ms, new_vs = adamw_stacked(f"adamw_{tag}", ws, s_w3, ms, vs, after)
        for which, part in ((GATE, "gate"), (UP, "up"), (DOWN, "down")):
            back = (lambda a: a[None]) if which == DOWN else (lambda a: a.T[None])
            big_out[f"w_{tag}_{part}"] = tuple(back(a) for a in (s_w3[which], deltas[which], new_ms[which], new_vs[which]))
        return deltas[DOWN]

    last = adamw_ffn("ffn2", s_w3_2, ffn2_w, ffn2_m, ffn2_v, (rs_ffn1_token, rs_mixer_token))
    s_in, s_gt, s_up, s_wo = reduce_scatter_4("rs_mixer", rs_mixer, last)
    grads = {"w_gate": s_gt[0].T, "w_up_a": s_up[0, :WIDTH], "w_up_b": s_up[0, WIDTH:], "w_out": s_wo[0]}
    weights = {"w_gate": (w_gate, m_w_gate, v_w_gate), "w_up_a": (w_up_a, m_w_up_a, v_w_up_a),
               "w_up_b": (w_up_b, m_w_up_b, v_w_up_b), "w_out": (w_out, m_w_out, v_w_out)}
    for wname, (w, m, v) in weights.items():
        g = grads[wname]
        delta, new_m, new_v = adamw(f"adamw_{wname}", w[0], g, m[0], v[0])
        big_out[wname] = (g[None], delta[None], new_m[None], new_v[None])
    rows_of = lambda a: jnp.transpose(a, (2, 0, 1))
    g_in_rows = s_in[0, :in4][:, None, :]
    in_rows = adamw_rows("adamw_w_in", rows_of(w_in), g_in_rows, rows_of(m_w_in), rows_of(v_w_in))
    big_out["w_in"] = tuple(jnp.transpose(a, (1, 2, 0)) for a in (g_in_rows, *in_rows))

    rs_ffn1, token = reduce_scatter_3("rs_ffn1", rs_ffn1, c_arr, in_rows[0])
    (s_w3_1,) = reduce_scatter_4("rs_ffn1", rs_ffn1, token)
    adamw_ffn("ffn1", s_w3_1, ffn1_w, ffn1_m, ffn1_v, ())

    order = ["norm_ffn1", "w_ffn1_gate", "w_ffn1_up", "w_ffn1_down", "norm_mix", "w_in", "b_forget", "w_gate", "b_gate",
             "w_up_a", "w_up_b", "w_out", "norm_ffn2", "w_ffn2_gate", "w_ffn2_up", "w_ffn2_down", "norm_final"]
    outs = [loss, dx0[None]]
    for kind in range(4):
        for wname in order:
            outs.append(big_out[wname][kind] if wname in big_out else small_out[kind][wname])
    return tuple(outs)
```

```python
import functools

import jax
import jax.numpy as jnp
from jax import lax
from jax.experimental import pallas as pl
from jax.experimental.pallas import tpu as pltpu

F32 = jnp.float32
BF16 = jnp.bfloat16

HEAD_DIM = 64
N_HEADS = 8
WIDTH = N_HEADS * HEAD_DIM
QKV_COLS = 6 * WIDTH
RMS_EPS = 1e-6
ATTN_SCALE = HEAD_DIM ** -0.5
N_CHIPS = 4
QB = 128
BQ = 2048
CS = 256
N_SUB = BQ // CS
NEG_BIG = -1e30

ADAM_LR = 0.001
ADAM_B1 = 0.9
ADAM_B2 = 0.999
ADAM_EPS = 1e-08
ADAM_WD = 0.01
ADAM_STEP = 10

VMEM_LIMIT_BYTES = 48 * 1024 * 1024
MESH = pl.DeviceIdType.MESH

NN = ((1,), (0,))
NT = ((1,), (1,))
TN = ((0,), (0,))


def _params(semantics):
    return pltpu.CompilerParams(dimension_semantics=semantics, vmem_limit_bytes=VMEM_LIMIT_BYTES)


def _dot(a, b, contract):
    return lax.dot_general(a.astype(BF16), b.astype(BF16), (contract, ((), ())), preferred_element_type=F32)


def _sigmoid(x):
    return 1.0 / (1.0 + jnp.exp(-x))


def _log1pexp_neg_abs(z):
    return jnp.log(1.0 + jnp.exp(-jnp.abs(z)))


def _split3(x):
    hi = x.astype(BF16)
    r1 = x - hi.astype(F32)
    mid = r1.astype(BF16)
    lo = (r1 - mid.astype(F32)).astype(BF16)
    return hi, mid, lo


def _dot_exact_lhs01(m01, x):
    hi, mid, lo = _split3(x)
    d = lambda p: lax.dot_general(m01, p, (NN, ((), ())), preferred_element_type=F32)
    return d(hi) + d(mid) + d(lo)


def _iota2(shape, dim):
    return lax.broadcasted_iota(jnp.int32, shape, dim)


def _mm(name, pairs, contract, grid, pair_specs, out_shape, out_specs, acc_shape, nk, epilogue,
        extras=(), extra_specs=(), semantics=None):
    n_pairs = len(pairs)
    n_extra = len(extras)
    n_out = len(out_shape)

    def body(*refs):
        ab = refs[:2 * n_pairs]
        ex = refs[2 * n_pairs:2 * n_pairs + n_extra]
        outs = refs[2 * n_pairs + n_extra:2 * n_pairs + n_extra + n_out]
        ids = [pl.program_id(i) for i in range(len(grid))]
        k = ids[-1]
        part = _dot(ab[0][...], ab[1][...], contract)
        for p in range(1, n_pairs):
            part += _dot(ab[2 * p][...], ab[2 * p + 1][...], contract)
        if nk == 1:
            epilogue(part, ex, outs, ids)
            return
        acc = refs[-1]

        @pl.when(k == 0)
        def _():
            acc[...] = part

        @pl.when(k != 0)
        def _():
            acc[...] += part

        @pl.when(k == nk - 1)
        def _():
            epilogue(acc[...], ex, outs, ids)

    operands = [t for pair in pairs for t in pair] + list(extras)
    in_specs = [s for pair in pair_specs for s in pair] + list(extra_specs)
    if semantics is None:
        semantics = ("parallel",) * (len(grid) - 1) + ("arbitrary",)
    return pl.pallas_call(
        body, name=name, grid=grid, in_specs=in_specs, out_specs=list(out_specs), out_shape=list(out_shape),
        scratch_shapes=[] if nk == 1 else [pltpu.VMEM(acc_shape, F32)], compiler_params=_params(semantics),
    )(*operands)


def _ordered_after(body, n_in, n_after):
    def wrapped(*refs):
        return body(*refs[:n_in], *refs[n_in + n_after:])
    return wrapped


def _row_tile(rows, target):
    t = min(rows, target)
    while rows % t:
        t //= 2
    return t


def rms_fwd(name, x, g, after=()):
    t, d = x.shape
    tr = _row_tile(t, 256)

    def body(x_ref, g_ref, h_ref, r_ref):
        xv = x_ref[...]
        r = lax.rsqrt(jnp.mean(xv * xv, axis=-1, keepdims=True) + RMS_EPS)
        h_ref[...] = (xv * r * g_ref[...]).astype(BF16)
        r_ref[...] = r

    return pl.pallas_call(
        _ordered_after(body, 2, len(after)), name=name, grid=(t // tr,),
        in_specs=[pl.BlockSpec((tr, d), lambda i: (i, 0)), pl.BlockSpec((1, d), lambda i: (0, 0))] + [_ANY] * len(after),
        out_specs=[pl.BlockSpec((tr, d), lambda i: (i, 0)), pl.BlockSpec((tr, 1), lambda i: (i, 0))],
        out_shape=[jax.ShapeDtypeStruct((t, d), BF16), jax.ShapeDtypeStruct((t, 1), F32)],
        compiler_params=_params(("parallel",)),
    )(x, g, *after)


GATE, UP, DOWN = 0, 1, 2


def _ffn_w_spec(which, f4, d, index_of_j):
    return pl.BlockSpec((None, None, f4, d), lambda *ids: (index_of_j(*ids), which, 0, 0))


def ffn_up(name, h, w3):
    t, d = h.shape
    ns, _, f4, _ = w3.shape
    tm = _row_tile(t, 512)

    def body(h_ref, wg_ref, wu_ref, a_ref, b_ref, s_ref):
        hv = h_ref[...]
        a = _dot(hv, wg_ref[...], NT)
        b = _dot(hv, wu_ref[...], NT)
        a_ref[...] = a.astype(BF16)
        b_ref[...] = b.astype(BF16)
        s_ref[...] = (a * _sigmoid(a) * b).astype(BF16)

    act_spec = pl.BlockSpec((None, tm, f4), lambda j, m: (j, m, 0))
    return pl.pallas_call(
        body, name=name, grid=(ns, t // tm),
        in_specs=[pl.BlockSpec((tm, d), lambda j, m: (m, 0)),
                  _ffn_w_spec(GATE, f4, d, lambda j, m: j), _ffn_w_spec(UP, f4, d, lambda j, m: j)],
        out_specs=[act_spec, act_spec, act_spec],
        out_shape=[jax.ShapeDtypeStruct((ns, t, f4), BF16)] * 3,
        compiler_params=_params(("parallel", "parallel")),
    )(h, w3, w3)


def mm_residual(name, s, w, w_spec, x, scale):
    nj, t, kdim = s.shape
    n = x.shape[1]
    tm = _row_tile(t, 512)

    def body(s_ref, w_ref, x_ref, o_ref):
        acc = _dot(s_ref[0], w_ref[0], NN)
        for j in range(1, nj):
            acc += _dot(s_ref[j], w_ref[j], NN)
        o_ref[...] = x_ref[...] + scale * acc

    row = pl.BlockSpec((tm, n), lambda m: (m, 0))
    return pl.pallas_call(
        body, name=name, grid=(t // tm,),
        in_specs=[pl.BlockSpec((nj, tm, kdim), lambda m: (0, m, 0)), w_spec, row], out_specs=row,
        out_shape=jax.ShapeDtypeStruct((t, n), F32), compiler_params=_params(("parallel",)),
    )(s, w, x)


def ffn_bwd_act(name, dx, w3, a, b, after=()):
    t, d = dx.shape
    ns, _, f4, _ = w3.shape
    tm = _row_tile(t, 512)

    def body(dx_ref, wd_ref, a_ref, b_ref, da_ref, db_ref):
        ds = _dot(0.5 * dx_ref[...], wd_ref[...], NT)
        av = a_ref[...].astype(F32)
        sig = _sigmoid(av)
        da_ref[...] = (ds * b_ref[...].astype(F32) * (sig * (1.0 + av * (1.0 - sig)))).astype(BF16)
        db_ref[...] = (ds * (av * sig)).astype(BF16)

    act_spec = pl.BlockSpec((None, tm, f4), lambda j, m: (j, m, 0))
    return pl.pallas_call(
        _ordered_after(body, 4, len(after)), name=name, grid=(ns, t // tm),
        in_specs=[pl.BlockSpec((tm, d), lambda j, m: (m, 0)), _ffn_w_spec(DOWN, f4, d, lambda j, m: j), act_spec, act_spec]
        + [_ANY] * len(after),
        out_specs=[act_spec, act_spec],
        out_shape=[jax.ShapeDtypeStruct((ns, t, f4), BF16)] * 2,
        compiler_params=_params(("parallel", "parallel")),
    )(dx, w3, a, b, *after)


def ffn_wgrad(name, h, da, db, s, dx):
    t, d = h.shape
    ns, _, f4 = da.shape
    tk = _row_tile(t, 1024)
    nk = t // tk

    def body(h_ref, da_ref, db_ref, s_ref, dx_ref, o_ref, acc):
        k = pl.program_id(1)

        @pl.when(k == 0)
        def _():
            acc[...] = jnp.zeros_like(acc)

        hv = h_ref[...]
        acc[GATE] += _dot(da_ref[...], hv, TN)
        acc[UP] += _dot(db_ref[...], hv, TN)
        acc[DOWN] += _dot(s_ref[...], 0.5 * dx_ref[...], TN)

        @pl.when(k == nk - 1)
        def _():
            o_ref[...] = acc[...].astype(BF16)

    act_spec = pl.BlockSpec((None, tk, f4), lambda j, k: (j, k, 0))
    row_spec = pl.BlockSpec((tk, d), lambda j, k: (k, 0))
    return pl.pallas_call(
        body, name=name, grid=(ns, nk),
        in_specs=[row_spec, act_spec, act_spec, act_spec, row_spec],
        out_specs=pl.BlockSpec((None, 3, f4, d), lambda j, k: (j, 0, 0, 0)),
        out_shape=jax.ShapeDtypeStruct((ns, 3, f4, d), BF16),
        scratch_shapes=[pltpu.VMEM((3, f4, d), F32)],
        compiler_params=_params(("parallel", "arbitrary")),
    )(h, da, db, s, dx)


def _rms_bwd_tail(dh, x_ref, r_ref, g_ref, dxin_ref, dx_ref, gn_ref, row_tile_index):
    r = r_ref[...]
    xhat = x_ref[...] * r
    dhg = dh * g_ref[...]
    dx_ref[...] = dxin_ref[...] + r * (dhg - xhat * jnp.mean(dhg * xhat, axis=-1, keepdims=True))
    part = jnp.sum(dh * xhat, axis=0, keepdims=True)

    @pl.when(row_tile_index == 0)
    def _():
        gn_ref[...] = part

    @pl.when(row_tile_index != 0)
    def _():
        gn_ref[...] += part


def ffn_dh(name, da, db, w3, x, rstd, g, dx_in, after=()):
    ns, t, f4 = da.shape
    d = x.shape[1]
    tm = _row_tile(t, 256)

    def body(da_ref, db_ref, wg_ref, wu_ref, x_ref, r_ref, g_ref, dxin_ref, dx_ref, gn_ref):
        dh = _dot(da_ref[0], wg_ref[0], NN) + _dot(db_ref[0], wu_ref[0], NN)
        for j in range(1, ns):
            dh += _dot(da_ref[j], wg_ref[j], NN) + _dot(db_ref[j], wu_ref[j], NN)
        _rms_bwd_tail(dh, x_ref, r_ref, g_ref, dxin_ref, dx_ref, gn_ref, pl.program_id(0))

    act = pl.BlockSpec((ns, tm, f4), lambda m: (0, m, 0))
    row = pl.BlockSpec((tm, d), lambda m: (m, 0))
    gain = pl.BlockSpec((1, d), lambda m: (0, 0))
    return pl.pallas_call(
        _ordered_after(body, 8, len(after)), name=name, grid=(t // tm,),
        in_specs=[act, act, pl.BlockSpec((ns, None, f4, d), lambda m: (0, GATE, 0, 0)),
                  pl.BlockSpec((ns, None, f4, d), lambda m: (0, UP, 0, 0)), row, pl.BlockSpec((tm, 1), lambda m: (m, 0)), gain, row]
        + [_ANY] * len(after),
        out_specs=[row, gain], out_shape=[jax.ShapeDtypeStruct((t, d), F32), jax.ShapeDtypeStruct((1, d), F32)],
        compiler_params=_params(("arbitrary",)),
    )(da, db, w3, w3, x, rstd, g, dx_in, *after)


def mixer_dh(name, parts, x, rstd, g, dx_in, after=()):
    t, d = x.shape
    n = len(parts)
    tm = _row_tile(t, 256)

    def body(*refs):
        dh = _dot(refs[0][...], refs[n][...], NN)
        for i in range(1, n):
            dh += _dot(refs[i][...], refs[n + i][...], NN)
        _rms_bwd_tail(dh, *refs[2 * n:2 * n + 6], pl.program_id(0))

    row = pl.BlockSpec((tm, d), lambda m: (m, 0))
    gain = pl.BlockSpec((1, d), lambda m: (0, 0))
    act_specs = [pl.BlockSpec((tm, a.shape[1]), lambda m: (m, 0)) for a, _, _ in parts]
    w_specs = [pl.BlockSpec((a.shape[1], d), functools.partial(lambda m, blk: (blk, 0), blk=blk)) for a, _, blk in parts]
    return pl.pallas_call(
        _ordered_after(body, 2 * n + 4, len(after)), name=name, grid=(t // tm,),
        in_specs=act_specs + w_specs + [row, pl.BlockSpec((tm, 1), lambda m: (m, 0)), gain, row] + [_ANY] * len(after),
        out_specs=[row, gain], out_shape=[jax.ShapeDtypeStruct((t, d), F32), jax.ShapeDtypeStruct((1, d), F32)],
        compiler_params=_params(("arbitrary",)),
    )(*[a for a, _, _ in parts], *[w for _, w, _ in parts], x, rstd, g, dx_in, *after)


def ffn_forward(tag, x, g_norm, w3, normed=None):
    _, _, f4, d = w3.shape
    h, rstd = normed if normed is not None else rms_fwd(f"{tag}_rms", x, g_norm)
    a, b, s = ffn_up(f"{tag}_up", h, w3)
    x_out = mm_residual(f"{tag}_down", s, w3, pl.BlockSpec((w3.shape[0], None, f4, d), lambda m: (0, DOWN, 0, 0)), x, 0.5)
    return x_out, (x, h, rstd, a, b, s)


def ffn_backward_weights(tag, dx, saved, w3, after=()):
    x, h, rstd, a, b, s = saved
    da, db = ffn_bwd_act(f"{tag}_bwd_act", dx, w3, a, b, after)
    return ffn_wgrad(f"{tag}_wgrad", h, da, db, s, dx), (da, db)


def ffn_backward_input(tag, dx, saved, dab, g_norm, w3, after=()):
    x, h, rstd, a, b, s = saved
    return ffn_dh(f"{tag}_dh", dab[0], dab[1], w3, x, rstd, g_norm, dx, after)


def proj(name, h, wcat_t, bias, first_col, n_cols, tn, out_dtype, scaled_tiles=()):
    t, d = h.shape
    tm = _row_tile(t, 512)
    off = first_col // tn

    def epilogue(acc, ex, outs, ids):
        val = acc + ex[0][...]
        if scaled_tiles:
            hit = functools.reduce(jnp.logical_or, [ids[0] == s for s in scaled_tiles])
            val = val * jnp.where(hit, ATTN_SCALE, 1.0)
        outs[0][...] = val.astype(out_dtype)

    return _mm(
        name, [(h, wcat_t)], NT, (n_cols // tn, t // tm, 1),
        [(pl.BlockSpec((tm, d), lambda j, m, k: (m, 0)), pl.BlockSpec((tn, d), lambda j, m, k: (off + j, 0)))],
        [jax.ShapeDtypeStruct((t, n_cols), out_dtype)], [pl.BlockSpec((tm, tn), lambda j, m, k: (m, j))], (tm, tn), 1, epilogue,
        extras=[bias], extra_specs=[pl.BlockSpec((1, tn), lambda j, m, k: (0, off + j))],
    )[0]


def mix_fwd(name, ya, yb, wup, pc):
    t, w = ya.shape
    ns, _, tn = wup.shape
    d = ns * tn
    tm = _row_tile(t, 512)

    def body(ya_ref, yb_ref, wa_ref, wb_ref, pa_ref, pb_ref, ua_ref, ub_ref, mx_ref):
        ua = _dot(ya_ref[...], wa_ref[...], NN)
        ub = _dot(yb_ref[...], wb_ref[...], NN)
        ua_ref[...] = ua
        ub_ref[...] = ub
        mx_ref[...] = (_sigmoid(pa_ref[...]) * ua + _sigmoid(pb_ref[...]) * ub).astype(BF16)

    y_spec = pl.BlockSpec((tm, w), lambda m, n: (m, 0))
    o_spec = pl.BlockSpec((tm, tn), lambda m, n: (m, n))
    return pl.pallas_call(
        body, name=name, grid=(t // tm, ns),
        in_specs=[y_spec, y_spec, pl.BlockSpec((None, w, tn), lambda m, n: (n, 0, 0)), pl.BlockSpec((None, w, tn), lambda m, n: (n, 1, 0)),
                  o_spec, pl.BlockSpec((tm, tn), lambda m, n: (m, ns + n))],
        out_specs=[o_spec, o_spec, o_spec],
        out_shape=[jax.ShapeDtypeStruct((t, d), F32), jax.ShapeDtypeStruct((t, d), F32), jax.ShapeDtypeStruct((t, d), BF16)],
        compiler_params=_params(("parallel", "parallel")),
    )(ya, yb, wup, wup, pc, pc)


def up_bwd(name, du, wup, branch):
    t, d = du.shape
    ns, w2, tn = wup.shape
    w = w2 // 2
    tm = _row_tile(t, 512)

    def body(du_ref, w_ref, o_ref):
        acc = _dot(du_ref[:, 0:tn], w_ref[0], NT)
        for j in range(1, ns):
            acc += _dot(du_ref[:, j * tn:(j + 1) * tn], w_ref[j], NT)
        o_ref[...] = acc.astype(BF16)

    return pl.pallas_call(
        body, name=name, grid=(t // tm,),
        in_specs=[pl.BlockSpec((tm, d), lambda m: (m, 0)), pl.BlockSpec((ns, w, tn), lambda m: (0, branch, 0))],
        out_specs=pl.BlockSpec((tm, w), lambda m: (m, 0)), out_shape=jax.ShapeDtypeStruct((t, w), BF16),
        compiler_params=_params(("parallel",)),
    )(du, wup)


def up_wgrad(name, ya, yb, dua, dub):
    t, w = ya.shape
    d = dua.shape[1]
    tn = d // N_CHIPS

    def body(ya_ref, yb_ref, dua_ref, dub_ref, o_ref):
        o_ref[0:w, :] = _dot(ya_ref[...], dua_ref[...], TN).astype(BF16)
        o_ref[w:2 * w, :] = _dot(yb_ref[...], dub_ref[...], TN).astype(BF16)

    y_spec = pl.BlockSpec((t, w), lambda j: (0, 0))
    du_spec = pl.BlockSpec((t, tn), lambda j: (0, j))
    return pl.pallas_call(
        body, name=name, grid=(N_CHIPS,), in_specs=[y_spec, y_spec, du_spec, du_spec],
        out_specs=pl.BlockSpec((None, 2 * w, tn), lambda j: (j, 0, 0)),
        out_shape=jax.ShapeDtypeStruct((N_CHIPS, 2 * w, tn), BF16), compiler_params=_params(("parallel",)),
    )(ya, yb, dua, dub)


def mix_bwd(name, dx, wo, pc, ua, ub, after=()):
    t, d = dx.shape
    tm = _row_tile(t, 512)
    tn = 512
    off_a = 0
    off_b = d // tn

    def body(dx_ref, wo_ref, pa_ref, pb_ref, ua_ref, ub_ref, dua_ref, dub_ref, dpa_ref, dpb_ref, ba_ref, bb_ref):
        dm = _dot(dx_ref[...], wo_ref[...], NT)
        ga = _sigmoid(pa_ref[...])
        gb = _sigmoid(pb_ref[...])
        dua_ref[...] = (dm * ga).astype(BF16)
        dub_ref[...] = (dm * gb).astype(BF16)
        dpa = dm * ua_ref[...] * ga * (1.0 - ga)
        dpb = dm * ub_ref[...] * gb * (1.0 - gb)
        dpa_ref[...] = dpa.astype(BF16)
        dpb_ref[...] = dpb.astype(BF16)
        sa = jnp.sum(dpa, axis=0, keepdims=True)
        sb = jnp.sum(dpb, axis=0, keepdims=True)

        @pl.when(pl.program_id(1) == 0)
        def _():
            ba_ref[...] = sa
            bb_ref[...] = sb

        @pl.when(pl.program_id(1) != 0)
        def _():
            ba_ref[...] += sa
            bb_ref[...] += sb

    tile = pl.BlockSpec((tm, tn), lambda n, m: (m, n))
    bias = pl.BlockSpec((1, tn), lambda n, m: (0, n))
    return pl.pallas_call(
        _ordered_after(body, 6, len(after)), name=name, grid=(d // tn, t // tm),
        in_specs=[pl.BlockSpec((tm, d), lambda n, m: (m, 0)), pl.BlockSpec((tn, d), lambda n, m: (n, 0)),
                  pl.BlockSpec((tm, tn), lambda n, m: (m, off_a + n)), pl.BlockSpec((tm, tn), lambda n, m: (m, off_b + n)),
                  tile, tile] + [_ANY] * len(after),
        out_specs=[tile, tile, tile, tile, bias, bias],
        out_shape=[jax.ShapeDtypeStruct((t, d), BF16)] * 4 + [jax.ShapeDtypeStruct((1, d), F32)] * 2,
        compiler_params=_params(("parallel", "arbitrary")),
    )(dx, wo, pc, pc, ua, ub, *after)


def mm_plain(name, a, b, contract, out_dtype, tk_target=512):
    if contract == NN:
        m, kdim = a.shape
        n = b.shape[1]
    elif contract == NT:
        m, kdim = a.shape
        n = b.shape[0]
    else:
        kdim, m = a.shape
        n = b.shape[1]
    tm = _row_tile(m, 512)
    tk = _row_tile(kdim, tk_target)
    nk = kdim // tk
    if contract == TN:
        a_spec = pl.BlockSpec((tk, tm), lambda i, k: (k, i))
    else:
        a_spec = pl.BlockSpec((tm, tk), lambda i, k: (i, k))
    if contract == NT:
        b_spec = pl.BlockSpec((n, tk), lambda i, k: (0, k))
    else:
        b_spec = pl.BlockSpec((tk, n), lambda i, k: (k, 0))

    def epilogue(acc, ex, outs, ids):
        outs[0][...] = acc.astype(out_dtype)

    return _mm(name, [(a, b)], contract, (m // tm, nk), [(a_spec, b_spec)],
               [jax.ShapeDtypeStruct((m, n), out_dtype)], [pl.BlockSpec((tm, n), lambda i, k: (i, 0))], (tm, n), nk, epilogue)[0]


def wgrad_cat(name, h, dcat):
    t, d = h.shape
    n = dcat.shape[1]
    tn = next(c for c in (768, 512, 256, 128) if n % c == 0)
    tk = _row_tile(t, 2048)

    def epilogue(acc, ex, outs, ids):
        outs[0][...] = acc.astype(BF16)

    return _mm(
        name, [(dcat, h)], TN, (n // tn, t // tk),
        [(pl.BlockSpec((tk, tn), lambda j, k: (k, j)), pl.BlockSpec((tk, d), lambda j, k: (k, 0)))],
        [jax.ShapeDtypeStruct((n, d), BF16)], [pl.BlockSpec((tn, d), lambda j, k: (j, 0))], (tn, d), t // tk, epilogue,
    )[0]


def fox_prep(name, f, bias):
    t, lanes = f.shape
    nchunk = t // QB

    def body(f_ref, b_ref, c_ref):
        lower = (_iota2((QB, QB), 1) <= _iota2((QB, QB), 0)).astype(BF16)

        def chunk(n, carry):
            rows = pl.ds(pl.multiple_of(n * QB, QB), QB)
            u = f_ref[rows, :] + b_ref[...]
            lf = jnp.minimum(u, 0.0) - _log1pexp_neg_abs(u)
            c = _dot_exact_lhs01(lower, lf) + carry
            c_ref[rows, :] = c
            return c[QB - 1:QB, :]

        lax.fori_loop(0, nchunk, chunk, jnp.zeros((1, lanes), F32))

    return pl.pallas_call(body, name=name, out_shape=jax.ShapeDtypeStruct((t, lanes), F32),
                          compiler_params=pltpu.CompilerParams(vmem_limit_bytes=VMEM_LIMIT_BYTES))(f, bias)


def fox_gate_bwd(name, dc, f, bias):
    t, lanes = dc.shape
    nchunk = t // QB

    def body(dc_ref, f_ref, b_ref, df_ref, gb_ref):
        upper = (_iota2((QB, QB), 1) >= _iota2((QB, QB), 0)).astype(BF16)

        def chunk(n, carry):
            tail, total = carry
            rows = pl.ds(pl.multiple_of((nchunk - 1 - n) * QB, QB), QB)
            dlf = _dot_exact_lhs01(upper, dc_ref[rows, :]) + tail
            u = f_ref[rows, :] + b_ref[...]
            df = dlf * jnp.exp(jnp.minimum(-u, 0.0) - _log1pexp_neg_abs(u))
            df_ref[rows, :] = df
            return dlf[0:1, :], total + jnp.sum(df, axis=0, keepdims=True)

        zero = jnp.zeros((1, lanes), F32)
        _, total = lax.fori_loop(0, nchunk, chunk, (zero, zero))
        gb_ref[...] = total

    return pl.pallas_call(body, name=name,
                          out_shape=[jax.ShapeDtypeStruct((t, lanes), F32), jax.ShapeDtypeStruct((1, lanes), F32)],
                          compiler_params=pltpu.CompilerParams(vmem_limit_bytes=VMEM_LIMIT_BYTES))(dc, f, bias)


def _qrows(i):
    return pl.ds(pl.multiple_of(i * BQ, BQ), BQ)


def _krows(kc):
    return pl.ds(pl.multiple_of(kc * CS, CS), CS)


def _dot_split2_rhs01(x, m01):
    hi = x.astype(BF16)
    lo = (x - hi.astype(F32)).astype(BF16)
    d = lambda p: lax.dot_general(p, m01, (NN, ((), ())), preferred_element_type=F32)
    return d(hi) + d(lo)


def _live_rows(dchunk):
    return 0 if dchunk is None else dchunk * CS


def _diag_mask(dchunk, inclusive):
    shape = (BQ - _live_rows(dchunk), CS)
    return _iota2(shape, 1) <= _iota2(shape, 0) if inclusive else _iota2(shape, 1) < _iota2(shape, 0)


def _tail(x, r0, axis=0):
    return x if r0 == 0 else (x[r0:] if axis == 0 else x[:, r0:])


def _with_tail(old, tail, r0):
    return tail if r0 == 0 else jnp.concatenate([old[:r0], tail], axis=0)


def _walk_chunks(i, step, init, right_to_left):
    order = list(reversed(range(N_SUB))) if right_to_left else list(range(N_SUB))

    def diagonal(state):
        for dchunk in order:
            state = step(i * N_SUB + dchunk, state, dchunk)
        return state

    def group(n, state):
        base = ((i - 1 - n) if right_to_left else n) * N_SUB
        for dchunk in order:
            state = step(base + dchunk, state, None)
        return state

    if right_to_left:
        return lax.fori_loop(0, i, group, diagonal(init))
    return diagonal(lax.fori_loop(0, i, group, init))


PAIR = 2 * HEAD_DIM
N_PAIRS = N_HEADS // 2


def _pair_spec(t, first_block):
    return pl.BlockSpec((t, PAIR), lambda p, *_: (0, first_block + p))


def _head_lanes(shape):
    lane = _iota2(shape, len(shape) - 1)
    return [lane < HEAD_DIM, lane >= HEAD_DIM]


def _only_head(x, lanes_of_head):
    return jnp.where(lanes_of_head, x, jnp.zeros_like(x))


LOG2_E = 1.4426950408889634


def _sb_chunk_weights(q_h, k, later, carry, dchunk):
    z = _dot(q_h, k, NT) * LOG2_E
    lnb = -jnp.maximum(z, 0.0) - jnp.log2(1.0 + jnp.exp2(-jnp.abs(z)))
    lsz = lnb + z
    if dchunk is not None:
        lnb = jnp.where(_diag_mask(dchunk, False), lnb, 0.0)
    w = jnp.exp2(lsz + _dot_split2_rhs01(lnb, later) + carry)
    if dchunk is not None:
        w = jnp.where(_diag_mask(dchunk, False), w, 0.0)
    return w, lsz, lnb


def sb_pair_fwd(name, qkv):
    t = qkv.shape[0]

    def body(q_ref, k_ref, v_ref, o_ref):
        later = (_iota2((CS, CS), 0) > _iota2((CS, CS), 1)).astype(BF16)
        lanes = _head_lanes((BQ, PAIR))

        def qblock(i, _):
            q = q_ref[_qrows(i), :]
            q_heads = [_only_head(q, lanes[h]) for h in range(2)]

            def step(kc, state, dchunk):
                k = k_ref[_krows(kc), :]
                v = v_ref[_krows(kc), :]
                out = []
                r0 = _live_rows(dchunk)
                for h in range(2):
                    carry, acc = state[h]
                    w, _, lnb = _sb_chunk_weights(_tail(q_heads[h], r0), k, later, _tail(carry, r0), dchunk)
                    out.append((_with_tail(carry, _tail(carry, r0) + jnp.sum(lnb, axis=1, keepdims=True), r0),
                                _with_tail(acc, _tail(acc, r0) + _dot(w, v, NN), r0)))
                return tuple(out)

            zero = (jnp.zeros((BQ, 1), F32), jnp.zeros((BQ, PAIR), F32))
            (_, acc0), (_, acc1) = _walk_chunks(i, step, (zero, zero), True)
            o_ref[_qrows(i), :] = jnp.where(lanes[0], acc0, acc1).astype(BF16)
            return 0

        lax.fori_loop(0, t // BQ, qblock, 0)

    return pl.pallas_call(
        body, name=name, grid=(N_PAIRS,),
        in_specs=[_pair_spec(t, 0), _pair_spec(t, N_PAIRS), _pair_spec(t, 2 * N_PAIRS)],
        out_specs=_pair_spec(t, 0), out_shape=jax.ShapeDtypeStruct((t, WIDTH), BF16),
        compiler_params=_params(("parallel",)),
    )(qkv, qkv, qkv)


def _emit_dqkv(res, o_ref):
    o_ref[...] = res[pl.program_id(1)]


def _flush_transposed(acc, res, which):
    for kc in range(acc.shape[0]):
        res[which, kc * CS:(kc + 1) * CS, :] = acc[kc].T.astype(BF16)


def sb_pair_bwd(name, qkv, dy, dqkv):
    t = qkv.shape[0]
    nc = t // CS

    def body(q_ref, k_ref, v_ref, do_ref, _, o_ref, g_s, b_s, dkt_acc, dvt_acc, res):
        @pl.when(pl.program_id(1) == 0)
        def _():
            later = (_iota2((CS, CS), 0) > _iota2((CS, CS), 1)).astype(BF16)
            earlier = (_iota2((CS, CS), 0) < _iota2((CS, CS), 1)).astype(BF16)
            lanes = _head_lanes((BQ, PAIR))
            dkt_acc[...] = jnp.zeros_like(dkt_acc)
            dvt_acc[...] = jnp.zeros_like(dvt_acc)

            def qblock(i, _):
                q = q_ref[_qrows(i), :]
                do = do_ref[_qrows(i), :]
                zero = jnp.zeros((BQ, 1), F32)
                dqs = []
                for h in range(2):
                    q_h = _only_head(q, lanes[h])
                    do_h = _only_head(do, lanes[h])
                    qt_h = q_h.astype(F32).T.astype(BF16)
                    dot_h = do_h.astype(F32).T.astype(BF16)

                    def step1(kc, carry, dchunk, q_h=q_h, do_h=do_h, dot_h=dot_h):
                        k = k_ref[_krows(kc), :]
                        v = v_ref[_krows(kc), :]
                        r0 = _live_rows(dchunk)
                        live = _tail(carry, r0)
                        w, lsz, lnb = _sb_chunk_weights(_tail(q_h, r0), k, later, live, dchunk)
                        g_s[kc, r0:, :] = (w * _dot(_tail(do_h, r0), v, NT)).astype(BF16)
                        b_s[kc, r0:, :] = jnp.exp2(lsz).astype(BF16)
                        dvt_acc[kc] += _dot(_tail(dot_h, r0, axis=1), w, NN)
                        return _with_tail(carry, live + jnp.sum(lnb, axis=1, keepdims=True), r0)

                    _walk_chunks(i, step1, zero, True)

                    def step2(kc, state, dchunk, qt_h=qt_h):
                        k = k_ref[_krows(kc), :]
                        r0 = _live_rows(dchunk)
                        before, dq = state
                        g16 = g_s[kc, r0:, :]
                        g = g16.astype(F32)
                        beta = b_s[kc, r0:, :].astype(F32)
                        prefix = lax.dot_general(g16, earlier, (NN, ((), ())), preferred_element_type=F32) + _tail(before, r0)
                        dz = g * (1.0 - beta) - beta * prefix
                        if dchunk is not None:
                            dz = jnp.where(_diag_mask(dchunk, False), dz, 0.0)
                        dzb = dz.astype(BF16)
                        dkt_acc[kc] += _dot(_tail(qt_h, r0, axis=1), dzb, NN)
                        return (_with_tail(before, _tail(before, r0) + jnp.sum(g, axis=1, keepdims=True), r0),
                                _with_tail(dq, _tail(dq, r0) + _dot(dzb, k, NN), r0))

                    dqs.append(_walk_chunks(i, step2, (zero, jnp.zeros((BQ, PAIR), F32)), False)[1])
                res[0, _qrows(i), :] = (jnp.where(lanes[0], dqs[0], dqs[1]) * ATTN_SCALE).astype(BF16)
                return 0

            lax.fori_loop(0, t // BQ, qblock, 0)
            _flush_transposed(dkt_acc, res, 1)
            _flush_transposed(dvt_acc, res, 2)

        _emit_dqkv(res, o_ref)

    return pl.pallas_call(
        body, name=name, grid=(N_PAIRS, 3),
        in_specs=[_pair_spec(t, 0), _pair_spec(t, N_PAIRS), _pair_spec(t, 2 * N_PAIRS), _pair_spec(t, 0), _ANY],
        out_specs=pl.BlockSpec((t, PAIR), lambda p, s: (0, s * N_PAIRS + p)),
        out_shape=jax.ShapeDtypeStruct(dqkv.shape, BF16), input_output_aliases={4: 0},
        scratch_shapes=[pltpu.VMEM((nc, BQ, CS), BF16), pltpu.VMEM((nc, BQ, CS), BF16),
                        pltpu.VMEM((nc, PAIR, CS), F32), pltpu.VMEM((nc, PAIR, CS), F32), pltpu.VMEM((3, t, PAIR), BF16)],
        compiler_params=_params(("parallel", "arbitrary")),
    )(qkv, qkv, qkv, dy, dqkv)


def _gates_col_spec(t):
    return pl.BlockSpec((2, t, 1), lambda p, *_: (p, 0, 0))


def _gates_row_spec(nc):
    return pl.BlockSpec((2, nc, 1, CS), lambda p, *_: (p, 0, 0, 0))


def fox_pair_fwd(name, qkv, c_col, c_row):
    t = qkv.shape[0]

    def body(q_ref, k_ref, v_ref, cc_ref, cr_ref, o_ref, lse_ref):
        lanes = _head_lanes((BQ, PAIR))

        def qblock(i, _):
            q = q_ref[_qrows(i), :]
            q_heads = [_only_head(q, lanes[h]) for h in range(2)]
            ct = [cc_ref[h, _qrows(i), :] for h in range(2)]

            def step(kc, state, dchunk):
                k = k_ref[_krows(kc), :]
                v = v_ref[_krows(kc), :]
                out = []
                r0 = _live_rows(dchunk)
                for h in range(2):
                    m, l, acc = (_tail(a, r0) for a in state[h])
                    s = _dot(_tail(q_heads[h], r0), k, NT) + _tail(ct[h], r0) - cr_ref[h, kc]
                    if dchunk is not None:
                        s = jnp.where(_diag_mask(dchunk, True), s, NEG_BIG)
                    m_new = jnp.maximum(m, jnp.max(s, axis=1, keepdims=True))
                    alpha = jnp.exp(m - m_new)
                    p = jnp.exp(s - m_new)
                    if dchunk is not None:
                        p = jnp.where(_diag_mask(dchunk, True), p, 0.0)
                    new = (m_new, l * alpha + jnp.sum(p, axis=1, keepdims=True), acc * alpha + _dot(p, v, NN))
                    out.append(tuple(_with_tail(old, tail, r0) for old, tail in zip(state[h], new)))
                return tuple(out)

            init = (jnp.full((BQ, 1), NEG_BIG, F32), jnp.zeros((BQ, 1), F32), jnp.zeros((BQ, PAIR), F32))
            (m0, l0, acc0), (m1, l1, acc1) = _walk_chunks(i, step, (init, init), False)
            o_ref[_qrows(i), :] = jnp.where(lanes[0], acc0 / l0, acc1 / l1).astype(BF16)
            lse_ref[0, _qrows(i), :] = m0 + jnp.log(l0)
            lse_ref[1, _qrows(i), :] = m1 + jnp.log(l1)
            return 0

        lax.fori_loop(0, t // BQ, qblock, 0)

    return pl.pallas_call(
        body, name=name, grid=(N_PAIRS,),
        in_specs=[_pair_spec(t, 3 * N_PAIRS), _pair_spec(t, 4 * N_PAIRS), _pair_spec(t, 5 * N_PAIRS),
                  _gates_col_spec(t), _gates_row_spec(t // CS)],
        out_specs=[_pair_spec(t, 0), _gates_col_spec(t)],
        out_shape=[jax.ShapeDtypeStruct((t, WIDTH), BF16), jax.ShapeDtypeStruct((N_HEADS, t, 1), F32)],
        compiler_params=_params(("parallel",)),
    )(qkv, qkv, qkv, c_col, c_row)


def fox_pair_bwd(name, qkv, y, dy, lse, c_col, c_row, dqkv):
    t = qkv.shape[0]
    nc = t // CS

    def body(q_ref, k_ref, v_ref, o_in_ref, do_ref, lse_ref, cc_ref, cr_ref, _, o_ref, dcc_ref, dcr_ref,
             dkt_acc, dvt_acc, dcr_acc, res):
        @pl.when(pl.program_id(1) == 0)
        def _():
            lanes = _head_lanes((BQ, PAIR))
            dkt_acc[...] = jnp.zeros_like(dkt_acc)
            dvt_acc[...] = jnp.zeros_like(dvt_acc)
            dcr_acc[...] = jnp.zeros_like(dcr_acc)

            def qblock(i, _):
                q = q_ref[_qrows(i), :]
                do = do_ref[_qrows(i), :]
                q_heads = [_only_head(q, lanes[h]) for h in range(2)]
                do_heads = [_only_head(do, lanes[h]) for h in range(2)]
                qt_heads = [qh.astype(F32).T.astype(BF16) for qh in q_heads]
                dot_heads = [dh.astype(F32).T.astype(BF16) for dh in do_heads]
                prod = do.astype(F32) * o_in_ref[_qrows(i), :].astype(F32)
                delta = [jnp.sum(_only_head(prod, lanes[h]), axis=1, keepdims=True) for h in range(2)]
                ct = [cc_ref[h, _qrows(i), :] for h in range(2)]
                lse_i = [lse_ref[h, _qrows(i), :] for h in range(2)]

                def step(kc, state, dchunk):
                    k = k_ref[_krows(kc), :]
                    v = v_ref[_krows(kc), :]
                    out = []
                    r0 = _live_rows(dchunk)
                    for h in range(2):
                        dq, dct = state[h]
                        s = _dot(_tail(q_heads[h], r0), k, NT) + _tail(ct[h], r0) - cr_ref[h, kc]
                        p = jnp.exp(s - _tail(lse_i[h], r0))
                        if dchunk is not None:
                            p = jnp.where(_diag_mask(dchunk, True), p, 0.0)
                        ds = p * (_dot(_tail(do_heads[h], r0), v, NT) - _tail(delta[h], r0))
                        dvt_acc[kc] += _dot(_tail(dot_heads[h], r0, axis=1), p, NN)
                        dsb = ds.astype(BF16)
                        dkt_acc[kc] += _dot(_tail(qt_heads[h], r0, axis=1), dsb, NN)
                        dcr_acc[h, kc] -= jnp.sum(ds, axis=0, keepdims=True)
                        out.append((_with_tail(dq, _tail(dq, r0) + _dot(dsb, k, NN), r0),
                                    _with_tail(dct, _tail(dct, r0) + jnp.sum(ds, axis=1, keepdims=True), r0)))
                    return tuple(out)

                zero = (jnp.zeros((BQ, PAIR), F32), jnp.zeros((BQ, 1), F32))
                (dq0, dct0), (dq1, dct1) = _walk_chunks(i, step, (zero, zero), False)
                res[0, _qrows(i), :] = (jnp.where(lanes[0], dq0, dq1) * ATTN_SCALE).astype(BF16)
                lane = _iota2((BQ, PAIR), 1)
                dcc_ref[_qrows(i), :] = jnp.where(lane == 0, dct0, jnp.where(lane == 1, dct1, 0.0))
                return 0

            lax.fori_loop(0, t // BQ, qblock, 0)
            _flush_transposed(dkt_acc, res, 1)
            _flush_transposed(dvt_acc, res, 2)
            dcr_ref[...] = dcr_acc[...]

        _emit_dqkv(res, o_ref)

    return pl.pallas_call(
        body, name=name, grid=(N_PAIRS, 3),
        in_specs=[_pair_spec(t, 3 * N_PAIRS), _pair_spec(t, 4 * N_PAIRS), _pair_spec(t, 5 * N_PAIRS), _pair_spec(t, 0),
                  _pair_spec(t, 0), _gates_col_spec(t), _gates_col_spec(t), _gates_row_spec(nc), _ANY],
        out_specs=[pl.BlockSpec((t, PAIR), lambda p, s: (0, (3 + s) * N_PAIRS + p)),
                   pl.BlockSpec((None, t, PAIR), lambda p, s: (p, 0, 0)), _gates_row_spec(nc)],
        out_shape=[jax.ShapeDtypeStruct(dqkv.shape, BF16), jax.ShapeDtypeStruct((N_PAIRS, t, PAIR), F32),
                   jax.ShapeDtypeStruct((N_HEADS, nc, 1, CS), F32)],
        input_output_aliases={8: 0},
        scratch_shapes=[pltpu.VMEM((nc, PAIR, CS), F32), pltpu.VMEM((nc, PAIR, CS), F32), pltpu.VMEM((2, nc, 1, CS), F32),
                        pltpu.VMEM((3, t, PAIR), BF16)],
        compiler_params=_params(("parallel", "arbitrary")),
    )(qkv, qkv, qkv, y, dy, lse, c_col, c_row, dqkv)


def loss_head(name, x, g, target):
    t, d = x.shape
    tr = _row_tile(t, 256)

    def body(x_ref, g_ref, t_ref, dx_ref, gn_ref, loss_ref):
        xv = x_ref[...]
        r = lax.rsqrt(jnp.mean(xv * xv, axis=-1, keepdims=True) + RMS_EPS)
        xhat = xv * r
        gv = g_ref[...]
        err = xhat * gv - t_ref[...]
        part_loss = 0.5 * jnp.sum(jnp.mean(err * err, axis=-1, keepdims=True), axis=0, keepdims=True)
        dy = err * (1.0 / d)
        dyg = dy * gv
        dx_ref[...] = r * (dyg - xhat * jnp.mean(dyg * xhat, axis=-1, keepdims=True))
        part_g = jnp.sum(dy * xhat, axis=0, keepdims=True)

        @pl.when(pl.program_id(0) == 0)
        def _():
            gn_ref[...] = part_g
            loss_ref[...] = part_loss

        @pl.when(pl.program_id(0) != 0)
        def _():
            gn_ref[...] += part_g
            loss_ref[...] += part_loss

    row = pl.BlockSpec((tr, d), lambda i: (i, 0))
    return pl.pallas_call(
        body, name=name, grid=(t // tr,),
        in_specs=[row, pl.BlockSpec((1, d), lambda i: (0, 0)), row],
        out_specs=[row, pl.BlockSpec((1, d), lambda i: (0, 0)), pl.BlockSpec((1, 1), lambda i: (0, 0))],
        out_shape=[jax.ShapeDtypeStruct((t, d), F32), jax.ShapeDtypeStruct((1, d), F32), jax.ShapeDtypeStruct((1, 1), F32)],
        compiler_params=_params(("arbitrary",)),
    )(x, g, target)


def _place():
    return lax.axis_index("x"), lax.axis_index("y"), lax.axis_index("c")


def _other_chips(x, y):
    return [(1 - x, y), (x, 1 - y), (1 - x, 1 - y)]


def _half(ref, c, rows):
    if rows % 32 == 0:
        return ref.at[:, pl.ds(c * (rows // 2), rows // 2), :]
    cols = ref.shape[2]
    return ref.at[:, :, pl.ds(c * (cols // 2), cols // 2)]


_ANY = pl.BlockSpec(memory_space=pl.ANY)


_HBM = pl.BlockSpec(memory_space=pltpu.HBM)
_SEM = pl.BlockSpec(memory_space=pltpu.SEMAPHORE)
_DATAFLOW = pltpu.SideEffectType.DATAFLOW_SIDE_EFFECTING


def _in_hbm(a):
    return pltpu.with_memory_space_constraint(a, pltpu.HBM)


def _gather_ici_copies(bufs, send_sems, recv_sems, arrivals):
    x, y, c = _place()
    me = 2 * x + y
    copies = []
    for i, buf in enumerate(bufs):
        rows = buf.shape[2]
        for j, (qx, qy) in enumerate(_other_chips(x, y)):
            block = _half(buf.at[2 * qx + qy if arrivals else me], c, rows)
            copies.append(pltpu.make_async_remote_copy(
                src_ref=block, dst_ref=block, send_sem=send_sems.at[3 * i + j], recv_sem=recv_sems.at[3 * i + j],
                device_id=(qx, qy, c), device_id_type=MESH))
    return copies


def gather_ici_start(name, bufs, after):
    n = len(bufs)

    def body(*refs):
        ins = refs[:n]
        send_sems, recv_sems = refs[n + 1], refs[n + 2]
        token = refs[-1]
        for send in _gather_ici_copies(ins, send_sems, recv_sems, False):
            send.start()
        token[...] = jnp.zeros_like(token)

    res = pl.pallas_call(
        body, name=name,
        out_shape=(pltpu.SemaphoreType.DMA((3 * n,)), pltpu.SemaphoreType.DMA((3 * n,)), *[pltpu.HBM(b.shape, b.dtype) for b in bufs],
                   jax.ShapeDtypeStruct((8, 128), F32)),
        in_specs=[_HBM] * n + [_ANY], out_specs=(_SEM, _SEM, *[_HBM] * n, pl.BlockSpec(memory_space=pltpu.VMEM)),
        input_output_aliases={i: 2 + i for i in range(n)},
        compiler_params=pltpu.CompilerParams(has_side_effects=_DATAFLOW),
    )(*[_in_hbm(b) for b in bufs], after)
    return res[0], res[1], list(res[2:2 + n]), res[-1]


def gather_ici_wait(name, send_sems, recv_sems, bufs, after):
    n = len(bufs)

    def body(*refs):
        ins = refs[:n]
        send_sems_ref, recv_sems_ref = refs[n], refs[n + 1]
        for send in _gather_ici_copies(ins, send_sems_ref, recv_sems_ref, False):
            send.wait_send()
        for recv in _gather_ici_copies(ins, send_sems_ref, recv_sems_ref, True):
            recv.wait_recv()

    return pl.pallas_call(
        body, name=name, out_shape=tuple(pltpu.HBM(b.shape, b.dtype) for b in bufs),
        in_specs=[_HBM] * n + [_SEM, _SEM, _ANY], out_specs=tuple([_HBM] * n),
        input_output_aliases={i: i for i in range(n)},
        compiler_params=pltpu.CompilerParams(has_side_effects=_DATAFLOW),
    )(*bufs, send_sems, recv_sems, after)


def gather_forward(name, bufs):
    n = len(bufs)

    def body(*refs):
        outs = refs[n:2 * n]
        send_sems, recv_sems = refs[2 * n:]
        x, y, c = _place()
        sibling = (x, y, 1 - c)
        sends = []
        for i in range(n):
            rows = outs[i].shape[2]
            for j, (qx, qy) in enumerate(_other_chips(x, y)):
                block = _half(outs[i].at[2 * qx + qy], c, rows)
                fw = pltpu.make_async_remote_copy(
                    src_ref=block, dst_ref=block, send_sem=send_sems.at[3 * i + j], recv_sem=recv_sems.at[3 * i + j],
                    device_id=sibling, device_id_type=MESH)
                fw.start()
                sends.append(fw)
        for i in range(n):
            rows = outs[i].shape[2]
            for j, (qx, qy) in enumerate(_other_chips(x, y)):
                block = _half(outs[i].at[2 * qx + qy], 1 - c, rows)
                pltpu.make_async_remote_copy(
                    src_ref=block, dst_ref=block, send_sem=send_sems.at[3 * i + j], recv_sem=recv_sems.at[3 * i + j],
                    device_id=sibling, device_id_type=MESH).wait_recv()
        for fw in sends:
            fw.wait_send()

    return pl.pallas_call(
        body, name=name, in_specs=[_ANY] * n, out_specs=[_ANY] * n,
        out_shape=[jax.ShapeDtypeStruct(b.shape, b.dtype) for b in bufs],
        input_output_aliases={i: i for i in range(n)},
        scratch_shapes=[pltpu.SemaphoreType.DMA((3 * n,)), pltpu.SemaphoreType.DMA((3 * n,))],
        compiler_params=pltpu.CompilerParams(has_side_effects=True),
    )(*bufs)


def _between_chips_copies(parts, lands, send_sems, recv_sems):
    x, y, c = _place()
    copies = []
    for i, (part, land) in enumerate(zip(parts, lands)):
        for j, (qx, qy) in enumerate(_other_chips(x, y)):
            copies.append(pltpu.make_async_remote_copy(
                src_ref=part.at[2 * qx + qy], dst_ref=land.at[j], send_sem=send_sems.at[3 * i + j], recv_sem=recv_sems.at[3 * i + j],
                device_id=(qx, qy, c), device_id_type=MESH))
    return copies


def between_chips_start(name, parts):
    n = len(parts)
    lands = [lax.empty((N_CHIPS - 1,) + p.shape[1:], p.dtype) for p in parts]

    def body(*refs):
        send_sems, recv_sems = refs[2 * n], refs[2 * n + 1]
        token = refs[-1]
        for cp in _between_chips_copies(refs[:n], refs[n:2 * n], send_sems, recv_sems):
            cp.start()
        token[...] = jnp.zeros_like(token)

    res = pl.pallas_call(
        body, name=name,
        out_shape=(pltpu.SemaphoreType.DMA((3 * n,)), pltpu.SemaphoreType.DMA((3 * n,)),
                   *[pltpu.HBM(a.shape, a.dtype) for a in parts + lands], jax.ShapeDtypeStruct((8, 128), F32)),
        in_specs=[_HBM] * (2 * n), out_specs=(_SEM, _SEM, *[_HBM] * (2 * n), pl.BlockSpec(memory_space=pltpu.VMEM)),
        input_output_aliases={i: 2 + i for i in range(2 * n)},
        compiler_params=pltpu.CompilerParams(has_side_effects=_DATAFLOW),
    )(*[_in_hbm(a) for a in parts + lands])
    return res[0], res[1], list(res[2:2 + n]), list(res[2 + n:2 + 2 * n]), res[-1]


def between_chips_wait(name, send_sems, recv_sems, parts, lands, after):
    n = len(parts)

    def body(*refs):
        for cp in _between_chips_copies(refs[:n], refs[n:2 * n], refs[2 * n], refs[2 * n + 1]):
            cp.wait_send()
            cp.wait_recv()

    res = pl.pallas_call(
        body, name=name, out_shape=tuple(pltpu.HBM(a.shape, a.dtype) for a in parts + lands),
        in_specs=[_HBM] * (2 * n) + [_SEM, _SEM, _ANY], out_specs=tuple([_HBM] * (2 * n)),
        input_output_aliases={i: i for i in range(2 * n)},
        compiler_params=pltpu.CompilerParams(has_side_effects=_DATAFLOW),
    )(*parts, *lands, send_sems, recv_sems, after)
    return list(res[:n]), list(res[n:])


def exchange_start(name, arrays, n_copies, copies, after=()):
    n = len(arrays)

    def body(*refs):
        send_sems, recv_sems = refs[n + len(after)], refs[n + len(after) + 1]
        for cp in copies(refs[:n], send_sems, recv_sems):
            cp.start()
        refs[-1][...] = jnp.zeros_like(refs[-1])

    res = pl.pallas_call(
        body, name=name,
        out_shape=(pltpu.SemaphoreType.DMA((n_copies,)), pltpu.SemaphoreType.DMA((n_copies,)),
                   *[pltpu.HBM(a.shape, a.dtype) for a in arrays], jax.ShapeDtypeStruct((8, 128), F32)),
        in_specs=[_HBM] * n + [_ANY] * len(after),
        out_specs=(_SEM, _SEM, *[_HBM] * n, pl.BlockSpec(memory_space=pltpu.VMEM)),
        input_output_aliases={i: 2 + i for i in range(n)},
        compiler_params=pltpu.CompilerParams(has_side_effects=_DATAFLOW),
    )(*[_in_hbm(a) for a in arrays], *after)
    return res[0], res[1], list(res[2:2 + n]), res[-1]


def exchange_wait(name, send_sems, recv_sems, arrays, copies, after):
    n = len(arrays)

    def body(*refs):
        for cp in copies(refs[:n], refs[n], refs[n + 1]):
            cp.wait_send()
            cp.wait_recv()

    return list(pl.pallas_call(
        body, name=name, out_shape=tuple(pltpu.HBM(a.shape, a.dtype) for a in arrays),
        in_specs=[_HBM] * n + [_SEM, _SEM, _ANY], out_specs=tuple([_HBM] * n),
        input_output_aliases={i: i for i in range(n)},
        compiler_params=pltpu.CompilerParams(has_side_effects=_DATAFLOW),
    )(*arrays, send_sems, recv_sems, after))


def _to_sibling_copies(n):
    def copies(refs, send_sems, recv_sems):
        x, y, c = _place()
        out = []
        for i in range(n):
            rows = refs[i].shape[2]
            out.append(pltpu.make_async_remote_copy(
                src_ref=refs[i].at[:, :, pl.ds((1 - c) * (rows // 2), rows // 2), :], dst_ref=refs[n + i],
                send_sem=send_sems.at[i], recv_sem=recv_sems.at[i], device_id=(x, y, 1 - c), device_id_type=MESH))
        return out
    return copies


def _share_copies(n):
    def copies(refs, send_sems, recv_sems):
        x, y, c = _place()
        out = []
        for i in range(n):
            mine = _half(refs[i], c, refs[i].shape[1])
            out.append(pltpu.make_async_remote_copy(
                src_ref=mine, dst_ref=mine, send_sem=send_sems.at[i], recv_sem=recv_sems.at[i],
                device_id=(x, y, 1 - c), device_id_type=MESH))
        return out
    return copies


def pair_sum(name, grad, recv, c):
    ns, na, rh, cols = recv.shape
    tr = _row_tile(rh, 256) if rh % 256 == 0 else rh
    nt = rh // tr

    def body(c_ref, g_ref, r_ref, o_ref):
        o_ref[...] = (g_ref[...].astype(F32) + r_ref[...].astype(F32)).astype(BF16)

    blk = (None, None, tr, cols)
    return pl.pallas_call(
        body, name=name,
        grid_spec=pltpu.PrefetchScalarGridSpec(
            num_scalar_prefetch=1, grid=(ns, na, nt),
            in_specs=[pl.BlockSpec(blk, lambda s, a, r, c_ref: (s, a, c_ref[0] * nt + r, 0)),
                      pl.BlockSpec(blk, lambda s, a, r, c_ref: (s, a, r, 0))],
            out_specs=pl.BlockSpec(blk, lambda s, a, r, c_ref: (s, a, r, 0))),
        out_shape=jax.ShapeDtypeStruct(recv.shape, BF16),
        compiler_params=_params(("parallel", "parallel", "parallel")),
    )(c, grad, recv)


def chip_sum(name, parts, landed, place):
    _, na, rh, cols = parts.shape
    tr = _row_tile(rh, 256) if rh % 256 == 0 else rh
    nt = rh // tr

    def body(place_ref, p_ref, l_ref, o_ref):
        total = p_ref[...].astype(F32)
        for s in range(N_CHIPS - 1):
            total = total + l_ref[s].astype(F32)
        o_ref[...] = total

    return pl.pallas_call(
        body, name=name,
        grid_spec=pltpu.PrefetchScalarGridSpec(
            num_scalar_prefetch=1, grid=(na, nt),
            in_specs=[pl.BlockSpec((None, None, tr, cols), lambda a, r, pr: (pr[1], a, r, 0)),
                      pl.BlockSpec((N_CHIPS - 1, None, tr, cols), lambda a, r, pr: (0, a, r, 0))],
            out_specs=pl.BlockSpec((None, tr, cols), lambda a, r, pr: (a, pr[0] * nt + r, 0))),
        out_shape=jax.ShapeDtypeStruct((na, 2 * rh, cols), F32),
        compiler_params=_params(("parallel", "parallel")),
    )(place, parts, landed)


def reduce_scatter_1(tag, grads):
    n = len(grads)
    lands = [lax.empty((g.shape[0], g.shape[1], g.shape[2] // 2, g.shape[3]), g.dtype) for g in grads]
    send_sems, recv_sems, arrays, token = exchange_start(f"{tag}_to_sibling_start", list(grads) + lands, n, _to_sibling_copies(n))
    return (send_sems, recv_sems, arrays), token


def reduce_scatter_2(tag, state, place, after):
    send_sems, recv_sems, arrays = state
    n = len(arrays) // 2
    arrays = exchange_wait(f"{tag}_to_sibling_wait", send_sems, recv_sems, arrays, _to_sibling_copies(n), after)
    parts = [pair_sum(f"{tag}_pair_sum{i}", g, r, place) for i, (g, r) in enumerate(zip(arrays[:n], arrays[n:]))]
    send_sems, recv_sems, parts, lands, token = between_chips_start(f"{tag}_between_chips_start", parts)
    return (send_sems, recv_sems, parts, lands), token


def reduce_scatter_3(tag, state, place, after):
    send_sems, recv_sems, parts, lands = state
    parts, landed = between_chips_wait(f"{tag}_between_chips_wait", send_sems, recv_sems, parts, lands, after)
    halves = [chip_sum(f"{tag}_chip_sum{i}", p, l, place) for i, (p, l) in enumerate(zip(parts, landed))]
    send_sems, recv_sems, halves, token = exchange_start(f"{tag}_share_start", halves, len(halves), _share_copies(len(halves)))
    return (send_sems, recv_sems, halves), token


def reduce_scatter_4(tag, state, after):
    send_sems, recv_sems, halves = state
    return exchange_wait(f"{tag}_share_wait", send_sems, recv_sems, halves, _share_copies(len(halves)), after)


def _adamw_math(w, g, m, v):
    m = ADAM_B1 * m + (1.0 - ADAM_B1) * g
    v = ADAM_B2 * v + (1.0 - ADAM_B2) * (g * g)
    m_hat = m / (1.0 - ADAM_B1 ** ADAM_STEP)
    v_hat = v / (1.0 - ADAM_B2 ** ADAM_STEP)
    delta = -ADAM_LR * (m_hat / (jnp.sqrt(v_hat) + ADAM_EPS) + ADAM_WD * w)
    return delta, m, v


def adamw(name, w, g, m, v, after=()):
    rows, cols = w.shape
    tr = _row_tile(rows, 256) if rows % 256 == 0 else rows // 2

    def body(w_ref, g_ref, m_ref, v_ref, d_ref, mo_ref, vo_ref):
        d_ref[...], mo_ref[...], vo_ref[...] = _adamw_math(w_ref[...], g_ref[...], m_ref[...], v_ref[...])

    blk = pl.BlockSpec((tr, cols), lambda i: (i, 0))
    return pl.pallas_call(
        _ordered_after(body, 4, len(after)), name=name, grid=(rows // tr,), in_specs=[blk] * 4 + [_ANY] * len(after),
        out_specs=[blk] * 3, out_shape=[jax.ShapeDtypeStruct(w.shape, F32)] * 3, compiler_params=_params(("parallel",)),
    )(w, g, m, v, *after)


def adamw_rows(name, w, g, m, v, after=()):
    rows, _, cols = w.shape
    tr = next(r for r in (128, 110, 64, 32, 16, 8, 1) if rows % r == 0)

    def body(w_ref, g_ref, m_ref, v_ref, d_ref, mo_ref, vo_ref):
        d_ref[...], mo_ref[...], vo_ref[...] = _adamw_math(w_ref[...], g_ref[...], m_ref[...], v_ref[...])

    blk = pl.BlockSpec((tr, 1, cols), lambda i: (i, 0, 0))
    return pl.pallas_call(
        _ordered_after(body, 4, len(after)), name=name, grid=(rows // tr,), in_specs=[blk] * 4 + [_ANY] * len(after),
        out_specs=[blk] * 3, out_shape=[jax.ShapeDtypeStruct(w.shape, F32)] * 3, compiler_params=_params(("parallel",)),
    )(w, g, m, v, *after)


def adamw_stacked(name, ws, g, ms, vs, after=()):
    n = len(ws)
    rows, cols = ws[0].shape
    tr = next(r for r in (128, 88, 64, 32, 16, 8) if rows % r == 0)

    def body(*refs):
        w_refs, m_refs, v_refs, g_ref = refs[:n], refs[n:2 * n], refs[2 * n:3 * n], refs[3 * n]
        outs = refs[3 * n + 1:]
        for i in range(n):
            outs[i][...], outs[n + i][...], outs[2 * n + i][...] = _adamw_math(
                w_refs[i][...], g_ref[i], m_refs[i][...], v_refs[i][...])

    blk = pl.BlockSpec((tr, cols), lambda r: (r, 0))
    res = pl.pallas_call(
        _ordered_after(body, 3 * n + 1, len(after)), name=name, grid=(rows // tr,),
        in_specs=[blk] * (3 * n) + [pl.BlockSpec((n, tr, cols), lambda r: (0, r, 0))] + [_ANY] * len(after),
        out_specs=[blk] * (3 * n),
        out_shape=[jax.ShapeDtypeStruct((rows, cols), F32)] * (3 * n), compiler_params=_params(("parallel",)),
    )(*ws, *ms, *vs, g, *after)
    return res[:n], res[n:2 * n], res[2 * n:]


def small_allreduce_adamw(name, g_part, w, m, v):
    rows, cols = g_part.shape

    def body(g_ref, w_ref, m_ref, v_ref, sum_ref, d_ref, mo_ref, vo_ref, land, send_sems, recv_sems):
        x, y, c = _place()
        me = 4 * x + 2 * y + c
        land[me] = g_ref[...]
        copies = []
        for r in range(1, 8):
            peer = (x ^ (r >> 2), y ^ ((r >> 1) & 1), c ^ (r & 1))
            rc = pltpu.make_async_remote_copy(
                src_ref=g_ref, dst_ref=land.at[me], send_sem=send_sems.at[r - 1], recv_sem=recv_sems.at[r - 1],
                device_id=peer, device_id_type=MESH)
            rc.start()
            copies.append(rc)
        for rc in copies:
            rc.wait()
        total = land[0]
        for s in range(1, 8):
            total = total + land[s]
        sum_ref[...] = total
        d_ref[...], mo_ref[...], vo_ref[...] = _adamw_math(w_ref[...], total, m_ref[...], v_ref[...])

    vmem = pl.BlockSpec(memory_space=pltpu.VMEM)
    return pl.pallas_call(
        body, name=name, in_specs=[vmem] * 4, out_specs=[vmem] * 4,
        out_shape=[jax.ShapeDtypeStruct((rows, cols), F32)] * 4,
        scratch_shapes=[pltpu.VMEM((8, rows, cols), F32), pltpu.SemaphoreType.DMA((7,)), pltpu.SemaphoreType.DMA((7,))],
        compiler_params=pltpu.CompilerParams(has_side_effects=True),
    )(g_part, w, m, v)


def kernel(x, norm_ffn1, w_ffn1_gate, w_ffn1_up, w_ffn1_down, norm_mix, w_in, b_forget, w_gate, b_gate, w_up_a, w_up_b, w_out, norm_ffn2, w_ffn2_gate, w_ffn2_up, w_ffn2_down, norm_final, loss_target, m_norm_ffn1, m_w_ffn1_gate, m_w_ffn1_up, m_w_ffn1_down, m_norm_mix, m_w_in, m_b_forget, m_w_gate, m_b_gate, m_w_up_a, m_w_up_b, m_w_out, m_norm_ffn2, m_w_ffn2_gate, m_w_ffn2_up, m_w_ffn2_down, m_norm_final, v_norm_ffn1, v_w_ffn1_gate, v_w_ffn1_up, v_w_ffn1_down, v_norm_mix, v_w_in, v_b_forget, v_w_gate, v_b_gate, v_w_up_a, v_w_up_b, v_w_out, v_norm_ffn2, v_w_ffn2_gate, v_w_ffn2_up, v_w_ffn2_down, v_norm_final):
    t, d = x.shape[1], x.shape[2]
    in4 = w_in.shape[2]
    gate4 = w_gate.shape[2]
    in_cols = N_CHIPS * in4
    n_forget = in_cols - QKV_COLS
    assert w_up_a.shape[1] == WIDTH and d == 2 * WIDTH and n_forget == N_HEADS and t % BQ == 0
    chip = 2 * lax.axis_index("x") + lax.axis_index("y")
    c_arr = jnp.stack([lax.axis_index("c"), chip]).astype(jnp.int32)
    x2d = x[0]
    target = loss_target[0]

    def slot(shard):
        return lax.dynamic_update_slice(lax.empty((N_CHIPS,) + shard.shape, BF16), shard.astype(BF16)[None], (chip, 0, 0, 0))

    def ffn_views(wg, wu, wd):
        return [wg[0].T, wu[0].T, wd[0]]

    ffn1_w, ffn1_m, ffn1_v = (ffn_views(w_ffn1_gate, w_ffn1_up, w_ffn1_down), ffn_views(m_w_ffn1_gate, m_w_ffn1_up, m_w_ffn1_down),
                              ffn_views(v_w_ffn1_gate, v_w_ffn1_up, v_w_ffn1_down))
    ffn2_w, ffn2_m, ffn2_v = (ffn_views(w_ffn2_gate, w_ffn2_up, w_ffn2_down), ffn_views(m_w_ffn2_gate, m_w_ffn2_up, m_w_ffn2_down),
                              ffn_views(v_w_ffn2_gate, v_w_ffn2_up, v_w_ffn2_down))
    in_pad = -(-in4 // 32) * 32
    in_sh = slot(w_in[0].T[None])
    gt_sh = slot(w_gate[0].T[None])
    up_sh = slot(jnp.concatenate([w_up_a[0], w_up_b[0]], axis=0)[None])
    wo_sh = slot(w_out)
    f1_send, f1_recv, f1_bufs, f1_token = gather_ici_start("gather_ffn1_start", [slot(jnp.stack(ffn1_w))], norm_ffn1)
    mx_send, mx_recv, mx_bufs, mx_token = gather_ici_start("gather_mixer_start", [in_sh, gt_sh, up_sh, wo_sh], f1_token)
    f2_send, f2_recv, f2_bufs, f2_token = gather_ici_start("gather_ffn2_start", [slot(jnp.stack(ffn2_w))], mx_token)

    normed1 = rms_fwd("ffn1_rms", x2d, norm_ffn1, after=(f2_token,))
    (w3_1,) = gather_forward("gather_ffn1_forward", gather_ici_wait("gather_ffn1_wait", f1_send, f1_recv, f1_bufs, normed1[0]))
    x1, saved1 = ffn_forward("ffn1", x2d, norm_ffn1, w3_1, normed=normed1)
    w_in_g, w_gate_g, wup, wo = gather_forward(
        "gather_mixer_forward", gather_ici_wait("gather_mixer_wait", mx_send, mx_recv, mx_bufs, x1))
    wup = wup[:, 0]
    w_in_t = w_in_g.reshape(in_cols, d)
    w_gate_t = w_gate_g.reshape(2 * d, d)
    w_f_t = jnp.pad(w_in_t[QKV_COLS:], ((0, QB - n_forget), (0, 0)))
    wo_full = wo.reshape(d, d)
    b_forget_row = jnp.pad(b_forget, ((0, 0), (0, QB - n_forget)))

    h2, rstd2 = rms_fwd("mix_rms", x1, norm_mix)
    qkv = proj("mix_proj_qkv", h2, w_in_t, jnp.zeros((1, QKV_COLS), F32), 0, QKV_COLS, WIDTH, BF16, scaled_tiles=(0, 3))
    pc = proj("mix_proj_gates", h2, w_gate_t, b_gate, 0, 2 * d, WIDTH, F32)
    f_logit = proj("mix_proj_forget", h2, w_f_t, jnp.zeros((1, QB), F32), 0, QB, QB, F32)
    c_cum = fox_prep("fox_prep", f_logit, b_forget_row)
    c_heads = c_cum[:, :N_HEADS].T
    c_col = c_heads[:, :, None]
    c_row = c_heads.reshape(N_HEADS, t // CS, 1, CS)
    ya = sb_pair_fwd("sb_fwd", qkv)
    yb, lse = fox_pair_fwd("fox_fwd", qkv, c_col, c_row)
    ua, ub, mixed = mix_fwd("mix_fwd", ya, yb, wup, pc)
    x2 = mm_residual("mix_out", mixed[None], wo_full[None], pl.BlockSpec((1, d, d), lambda m: (0, 0, 0)), x1, 1.0)
    (w3_2,) = gather_forward("gather_ffn2_forward", gather_ici_wait("gather_ffn2_wait", f2_send, f2_recv, f2_bufs, x2))
    x3, saved2 = ffn_forward("ffn2", x2, norm_ffn2, w3_2)
    dx3, gn_final, loss_part = loss_head("loss_head", x3, norm_final[None], target)

    g_w3_2, dab2 = ffn_backward_weights("ffn2", dx3, saved2, w3_2)
    rs_ffn2, token = reduce_scatter_1("rs_ffn2", [g_w3_2])
    dx2, gn_ffn2 = ffn_backward_input("ffn2", dx3, saved2, dab2, norm_ffn2, w3_2, after=(token,))
    rs_ffn2, rs_ffn2_token = reduce_scatter_2("rs_ffn2", rs_ffn2, c_arr, dx2)

    dua, dub, dpa, dpb, gba, gbb = mix_bwd("mix_bwd", dx2, wo_full, pc, ua, ub, after=(rs_ffn2_token,))
    g_bgate = jnp.concatenate([gba, gbb], axis=1)
    g_wo = mm_plain("wgrad_out", mixed, dx2, TN, BF16, tk_target=1024)
    dya = up_bwd("dya", dua, wup, 0)
    dyb = up_bwd("dyb", dub, wup, 1)
    g_up = up_wgrad("wgrad_up", ya, yb, dua, dub)
    dqkv = sb_pair_bwd("sb_bwd", qkv, dya, lax.empty((t, QKV_COLS), BF16))
    dqkv, dcc, dcr = fox_pair_bwd("fox_bwd", qkv, yb, dyb, lse, c_col, c_row, dqkv)
    dc = dcc[:, :, :2].transpose(1, 0, 2).reshape(t, N_HEADS) + dcr.reshape(N_HEADS, t).T
    df, g_bf = fox_gate_bwd("fox_gate_bwd", jnp.pad(dc, ((0, 0), (0, QB - N_HEADS))), f_logit, b_forget_row)
    df = df.astype(BF16)
    g_qkv_t = wgrad_cat("wgrad_qkv", h2, dqkv)
    g_f_t = wgrad_cat("wgrad_forget", h2, df)
    g_gate_t = jnp.stack([wgrad_cat("wgrad_gate_a", h2, dpa), wgrad_cat("wgrad_gate_b", h2, dpb)])
    g_in_t = jnp.concatenate([g_qkv_t, g_f_t[:n_forget]], axis=0).reshape(N_CHIPS, in4, d)
    rs_mixer, token = reduce_scatter_1(
        "rs_mixer", [jnp.pad(g_in_t, ((0, 0), (0, in_pad - in4), (0, 0)))[:, None], g_gate_t.reshape(N_CHIPS, 1, gate4, d),
                     g_up[:, None], g_wo.reshape(N_CHIPS, 1, d // N_CHIPS, d)])
    dx1, gn_mix = mixer_dh("mix_dh", [(dqkv, w_in_t, 0), (dpa, w_gate_t, 0), (dpb, w_gate_t, 1), (df, w_f_t, 0)],
                           x1, rstd2, norm_mix, dx2, after=(token,))
    rs_mixer, rs_mixer_token = reduce_scatter_2("rs_mixer", rs_mixer, c_arr, dx1)
    rs_ffn2, rs_ffn2_token = reduce_scatter_3("rs_ffn2", rs_ffn2, c_arr, dx1)

    big_out = {}

    def adamw_ffn(tag, s_w3, ws, ms, vs, after):
        deltas, new_ms, new_vs = adamw_stacked(f"adamw_{tag}", ws, s_w3, ms, vs, after)
        for which, part in ((GATE, "gate"), (UP, "up"), (DOWN, "down")):
            back = (lambda a: a[None]) if which == DOWN else (lambda a: a.T[None])
            big_out[f"w_{tag}_{part}"] = tuple(back(a) for a in (s_w3[which], deltas[which], new_ms[which], new_vs[which]))
        return deltas[DOWN]

    g_w3_1, dab1 = ffn_backward_weights("ffn1", dx1, saved1, w3_1, after=(rs_mixer_token, rs_ffn2_token))
    rs_ffn1, token = reduce_scatter_1("rs_ffn1", [g_w3_1])
    rs_mixer, rs_mixer_token = reduce_scatter_3("rs_mixer", rs_mixer, c_arr, g_w3_1)
    (s_w3_2,) = reduce_scatter_4("rs_ffn2", rs_ffn2, g_w3_1)
    last = adamw_ffn("ffn2", s_w3_2, ffn2_w, ffn2_m, ffn2_v, (token, rs_mixer_token))
    rs_ffn1, rs_ffn1_token = reduce_scatter_2("rs_ffn1", rs_ffn1, c_arr, last)
    dx0, gn_ffn1 = ffn_backward_input("ffn1", dx1, saved1, dab1, norm_ffn1, w3_1, after=(rs_ffn1_token,))

    def pack_small(n1, nm, n2, nf, bg, bf, last):
        return jnp.concatenate([n1, nm, n2, nf, bg.reshape(2, d), jnp.pad(bf, ((0, 0), (0, d - n_forget))), last], axis=0)

    zero_row = jnp.zeros((1, d), F32)
    g_small = pack_small(gn_ffn1, gn_mix, gn_ffn2, gn_final, g_bgate, g_bf[:, :n_forget], jnp.pad(loss_part, ((0, 0), (0, d - 1))))
    w_small = pack_small(norm_ffn1, norm_mix, norm_ffn2, norm_final[None], b_gate, b_forget, zero_row)
    m_small = pack_small(m_norm_ffn1, m_norm_mix, m_norm_ffn2, m_norm_final[None], m_b_gate, m_b_forget, zero_row)
    v_small = pack_small(v_norm_ffn1, v_norm_mix, v_norm_ffn2, v_norm_final[None], v_b_gate, v_b_forget, zero_row)
    smalls = small_allreduce_adamw("small_allreduce_adamw", g_small, w_small, m_small, v_small)

    def unpack_small(p):
        return {"norm_ffn1": p[0:1], "norm_mix": p[1:2], "norm_ffn2": p[2:3], "norm_final": p[3], "b_gate": p[4:6].reshape(1, 2 * d),
                "b_forget": p[6:7, :n_forget]}

    loss = smalls[0][7, 0]
    small_out = [unpack_small(p) for p in smalls]

    s_in, s_gt, s_up, s_wo = reduce_scatter_4("rs_mixer", rs_mixer, dx0)
    grads = {"w_gate": s_gt[0].T, "w_up_a": s_up[0, :WIDTH], "w_up_b": s_up[0, WIDTH:], "w_out": s_wo[0]}
    weights = {"w_gate": (w_gate, m_w_gate, v_w_gate), "w_up_a": (w_up_a, m_w_up_a, v_w_up_a),
               "w_up_b": (w_up_b, m_w_up_b, v_w_up_b), "w_out": (w_out, m_w_out, v_w_out)}
    for wname, (w, m, v) in weights.items():
        g = grads[wname]
        delta, new_m, new_v = adamw(f"adamw_{wname}", w[0], g, m[0], v[0])
        big_out[wname] = (g[None], delta[None], new_m[None], new_v[None])
    rows_of = lambda a: jnp.transpose(a, (2, 0, 1))
    g_in_rows = s_in[0, :in4][:, None, :]
    in_rows = adamw_rows("adamw_w_in", rows_of(w_in), g_in_rows, rows_of(m_w_in), rows_of(v_w_in))
    big_out["w_in"] = tuple(jnp.transpose(a, (1, 2, 0)) for a in (g_in_rows, *in_rows))

    rs_ffn1, token = reduce_scatter_3("rs_ffn1", rs_ffn1, c_arr, in_rows[0])
    (s_w3_1,) = reduce_scatter_4("rs_ffn1", rs_ffn1, token)
    adamw_ffn("ffn1", s_w3_1, ffn1_w, ffn1_m, ffn1_v, ())

    order = ["norm_ffn1", "w_ffn1_gate", "w_ffn1_up", "w_ffn1_down", "norm_mix", "w_in", "b_forget", "w_gate", "b_gate",
             "w_up_a", "w_up_b", "w_out", "norm_ffn2", "w_ffn2_gate", "w_ffn2_up", "w_ffn2_down", "norm_final"]
    outs = [loss, dx0[None]]
    for kind in range(4):
        for wname in order:
            outs.append(big_out[wname][kind] if wname in big_out else small_out[kind][wname])
    return tuple(outs)
```

```python
import functools

import jax
import jax.numpy as jnp
from jax import lax
from jax.experimental import pallas as pl
from jax.experimental.pallas import tpu as pltpu

F32 = jnp.float32
BF16 = jnp.bfloat16

HEAD_DIM = 64
N_HEADS = 8
WIDTH = N_HEADS * HEAD_DIM
QKV_COLS = 6 * WIDTH
RMS_EPS = 1e-6
ATTN_SCALE = HEAD_DIM ** -0.5
N_CHIPS = 4
QB = 128
BQ = 2048
CS = 256
N_SUB = BQ // CS
NEG_BIG = -1e30

ADAM_LR = 0.001
ADAM_B1 = 0.9
ADAM_B2 = 0.999
ADAM_EPS = 1e-08
ADAM_WD = 0.01
ADAM_STEP = 10

VMEM_LIMIT_BYTES = 48 * 1024 * 1024
MESH = pl.DeviceIdType.MESH

NN = ((1,), (0,))
NT = ((1,), (1,))
TN = ((0,), (0,))


def _params(semantics):
    return pltpu.CompilerParams(dimension_semantics=semantics, vmem_limit_bytes=VMEM_LIMIT_BYTES)


def _dot(a, b, contract):
    return lax.dot_general(a.astype(BF16), b.astype(BF16), (contract, ((), ())), preferred_element_type=F32)


def _sigmoid(x):
    return 1.0 / (1.0 + jnp.exp(-x))


def _log1pexp_neg_abs(z):
    return jnp.log(1.0 + jnp.exp(-jnp.abs(z)))


def _split3(x):
    hi = x.astype(BF16)
    r1 = x - hi.astype(F32)
    mid = r1.astype(BF16)
    lo = (r1 - mid.astype(F32)).astype(BF16)
    return hi, mid, lo


def _dot_exact_lhs01(m01, x):
    hi, mid, lo = _split3(x)
    d = lambda p: lax.dot_general(m01, p, (NN, ((), ())), preferred_element_type=F32)
    return d(hi) + d(mid) + d(lo)


def _iota2(shape, dim):
    return lax.broadcasted_iota(jnp.int32, shape, dim)


def _mm(name, pairs, contract, grid, pair_specs, out_shape, out_specs, acc_shape, nk, epilogue,
        extras=(), extra_specs=(), semantics=None):
    n_pairs = len(pairs)
    n_extra = len(extras)
    n_out = len(out_shape)

    def body(*refs):
        ab = refs[:2 * n_pairs]
        ex = refs[2 * n_pairs:2 * n_pairs + n_extra]
        outs = refs[2 * n_pairs + n_extra:2 * n_pairs + n_extra + n_out]
        ids = [pl.program_id(i) for i in range(len(grid))]
        k = ids[-1]
        part = _dot(ab[0][...], ab[1][...], contract)
        for p in range(1, n_pairs):
            part += _dot(ab[2 * p][...], ab[2 * p + 1][...], contract)
        if nk == 1:
            epilogue(part, ex, outs, ids)
            return
        acc = refs[-1]

        @pl.when(k == 0)
        def _():
            acc[...] = part

        @pl.when(k != 0)
        def _():
            acc[...] += part

        @pl.when(k == nk - 1)
        def _():
            epilogue(acc[...], ex, outs, ids)

    operands = [t for pair in pairs for t in pair] + list(extras)
    in_specs = [s for pair in pair_specs for s in pair] + list(extra_specs)
    if semantics is None:
        semantics = ("parallel",) * (len(grid) - 1) + ("arbitrary",)
    return pl.pallas_call(
        body, name=name, grid=grid, in_specs=in_specs, out_specs=list(out_specs), out_shape=list(out_shape),
        scratch_shapes=[] if nk == 1 else [pltpu.VMEM(acc_shape, F32)], compiler_params=_params(semantics),
    )(*operands)


def _ordered_after(body, n_in, n_after):
    def wrapped(*refs):
        return body(*refs[:n_in], *refs[n_in + n_after:])
    return wrapped


def _row_tile(rows, target):
    t = min(rows, target)
    while rows % t:
        t //= 2
    return t


def rms_fwd(name, x, g, after=()):
    t, d = x.shape
    tr = _row_tile(t, 256)

    def body(x_ref, g_ref, h_ref, r_ref):
        xv = x_ref[...]
        r = lax.rsqrt(jnp.mean(xv * xv, axis=-1, keepdims=True) + RMS_EPS)
        h_ref[...] = (xv * r * g_ref[...]).astype(BF16)
        r_ref[...] = r

    return pl.pallas_call(
        _ordered_after(body, 2, len(after)), name=name, grid=(t // tr,),
        in_specs=[pl.BlockSpec((tr, d), lambda i: (i, 0)), pl.BlockSpec((1, d), lambda i: (0, 0))] + [_ANY] * len(after),
        out_specs=[pl.BlockSpec((tr, d), lambda i: (i, 0)), pl.BlockSpec((tr, 1), lambda i: (i, 0))],
        out_shape=[jax.ShapeDtypeStruct((t, d), BF16), jax.ShapeDtypeStruct((t, 1), F32)],
        compiler_params=_params(("parallel",)),
    )(x, g, *after)


GATE, UP, DOWN = 0, 1, 2


def _ffn_w_spec(which, f4, d, index_of_j):
    return pl.BlockSpec((None, None, f4, d), lambda *ids: (index_of_j(*ids), which, 0, 0))


def ffn_up(name, h, w3):
    t, d = h.shape
    ns, _, f4, _ = w3.shape
    tm = _row_tile(t, 512)

    def body(h_ref, wg_ref, wu_ref, a_ref, b_ref, s_ref):
        hv = h_ref[...]
        a = _dot(hv, wg_ref[...], NT)
        b = _dot(hv, wu_ref[...], NT)
        a_ref[...] = a.astype(BF16)
        b_ref[...] = b.astype(BF16)
        s_ref[...] = (a * _sigmoid(a) * b).astype(BF16)

    act_spec = pl.BlockSpec((None, tm, f4), lambda j, m: (j, m, 0))
    return pl.pallas_call(
        body, name=name, grid=(ns, t // tm),
        in_specs=[pl.BlockSpec((tm, d), lambda j, m: (m, 0)),
                  _ffn_w_spec(GATE, f4, d, lambda j, m: j), _ffn_w_spec(UP, f4, d, lambda j, m: j)],
        out_specs=[act_spec, act_spec, act_spec],
        out_shape=[jax.ShapeDtypeStruct((ns, t, f4), BF16)] * 3,
        compiler_params=_params(("parallel", "parallel")),
    )(h, w3, w3)


def mm_residual(name, s, w, w_spec, x, scale):
    nj, t, kdim = s.shape
    n = x.shape[1]
    tm = _row_tile(t, 512)

    def body(s_ref, w_ref, x_ref, o_ref):
        acc = _dot(s_ref[0], w_ref[0], NN)
        for j in range(1, nj):
            acc += _dot(s_ref[j], w_ref[j], NN)
        o_ref[...] = x_ref[...] + scale * acc

    row = pl.BlockSpec((tm, n), lambda m: (m, 0))
    return pl.pallas_call(
        body, name=name, grid=(t // tm,),
        in_specs=[pl.BlockSpec((nj, tm, kdim), lambda m: (0, m, 0)), w_spec, row], out_specs=row,
        out_shape=jax.ShapeDtypeStruct((t, n), F32), compiler_params=_params(("parallel",)),
    )(s, w, x)


def ffn_bwd_act(name, dx, w3, a, b, after=()):
    t, d = dx.shape
    ns, _, f4, _ = w3.shape
    tm = _row_tile(t, 512)

    def body(dx_ref, wd_ref, a_ref, b_ref, da_ref, db_ref):
        ds = _dot(0.5 * dx_ref[...], wd_ref[...], NT)
        av = a_ref[...].astype(F32)
        sig = _sigmoid(av)
        da_ref[...] = (ds * b_ref[...].astype(F32) * (sig * (1.0 + av * (1.0 - sig)))).astype(BF16)
        db_ref[...] = (ds * (av * sig)).astype(BF16)

    act_spec = pl.BlockSpec((None, tm, f4), lambda j, m: (j, m, 0))
    return pl.pallas_call(
        _ordered_after(body, 4, len(after)), name=name, grid=(ns, t // tm),
        in_specs=[pl.BlockSpec((tm, d), lambda j, m: (m, 0)), _ffn_w_spec(DOWN, f4, d, lambda j, m: j), act_spec, act_spec]
        + [_ANY] * len(after),
        out_specs=[act_spec, act_spec],
        out_shape=[jax.ShapeDtypeStruct((ns, t, f4), BF16)] * 2,
        compiler_params=_params(("parallel", "parallel")),
    )(dx, w3, a, b, *after)


def ffn_wgrad(name, h, da, db, s, dx):
    t, d = h.shape
    ns, _, f4 = da.shape
    tk = _row_tile(t, 1024)
    nk = t // tk

    def body(h_ref, da_ref, db_ref, s_ref, dx_ref, o_ref, acc):
        k = pl.program_id(1)

        @pl.when(k == 0)
        def _():
            acc[...] = jnp.zeros_like(acc)

        hv = h_ref[...]
        acc[GATE] += _dot(da_ref[...], hv, TN)
        acc[UP] += _dot(db_ref[...], hv, TN)
        acc[DOWN] += _dot(s_ref[...], 0.5 * dx_ref[...], TN)

        @pl.when(k == nk - 1)
        def _():
            o_ref[...] = acc[...].astype(BF16)

    act_spec = pl.BlockSpec((None, tk, f4), lambda j, k: (j, k, 0))
    row_spec = pl.BlockSpec((tk, d), lambda j, k: (k, 0))
    return pl.pallas_call(
        body, name=name, grid=(ns, nk),
        in_specs=[row_spec, act_spec, act_spec, act_spec, row_spec],
        out_specs=pl.BlockSpec((None, 3, f4, d), lambda j, k: (j, 0, 0, 0)),
        out_shape=jax.ShapeDtypeStruct((ns, 3, f4, d), BF16),
        scratch_shapes=[pltpu.VMEM((3, f4, d), F32)],
        compiler_params=_params(("parallel", "arbitrary")),
    )(h, da, db, s, dx)


def _rms_bwd_tail(dh, x_ref, r_ref, g_ref, dxin_ref, dx_ref, gn_ref, row_tile_index):
    r = r_ref[...]
    xhat = x_ref[...] * r
    dhg = dh * g_ref[...]
    dx_ref[...] = dxin_ref[...] + r * (dhg - xhat * jnp.mean(dhg * xhat, axis=-1, keepdims=True))
    part = jnp.sum(dh * xhat, axis=0, keepdims=True)

    @pl.when(row_tile_index == 0)
    def _():
        gn_ref[...] = part

    @pl.when(row_tile_index != 0)
    def _():
        gn_ref[...] += part


def ffn_dh(name, da, db, w3, x, rstd, g, dx_in, after=()):
    ns, t, f4 = da.shape
    d = x.shape[1]
    tm = _row_tile(t, 256)

    def body(da_ref, db_ref, wg_ref, wu_ref, x_ref, r_ref, g_ref, dxin_ref, dx_ref, gn_ref):
        dh = _dot(da_ref[0], wg_ref[0], NN) + _dot(db_ref[0], wu_ref[0], NN)
        for j in range(1, ns):
            dh += _dot(da_ref[j], wg_ref[j], NN) + _dot(db_ref[j], wu_ref[j], NN)
        _rms_bwd_tail(dh, x_ref, r_ref, g_ref, dxin_ref, dx_ref, gn_ref, pl.program_id(0))

    act = pl.BlockSpec((ns, tm, f4), lambda m: (0, m, 0))
    row = pl.BlockSpec((tm, d), lambda m: (m, 0))
    gain = pl.BlockSpec((1, d), lambda m: (0, 0))
    return pl.pallas_call(
        _ordered_after(body, 8, len(after)), name=name, grid=(t // tm,),
        in_specs=[act, act, pl.BlockSpec((ns, None, f4, d), lambda m: (0, GATE, 0, 0)),
                  pl.BlockSpec((ns, None, f4, d), lambda m: (0, UP, 0, 0)), row, pl.BlockSpec((tm, 1), lambda m: (m, 0)), gain, row]
        + [_ANY] * len(after),
        out_specs=[row, gain], out_shape=[jax.ShapeDtypeStruct((t, d), F32), jax.ShapeDtypeStruct((1, d), F32)],
        compiler_params=_params(("arbitrary",)),
    )(da, db, w3, w3, x, rstd, g, dx_in, *after)


def mixer_dh(name, parts, x, rstd, g, dx_in, after=()):
    t, d = x.shape
    n = len(parts)
    tm = _row_tile(t, 256)

    def body(*refs):
        dh = _dot(refs[0][...], refs[n][...], NN)
        for i in range(1, n):
            dh += _dot(refs[i][...], refs[n + i][...], NN)
        _rms_bwd_tail(dh, *refs[2 * n:2 * n + 6], pl.program_id(0))

    row = pl.BlockSpec((tm, d), lambda m: (m, 0))
    gain = pl.BlockSpec((1, d), lambda m: (0, 0))
    act_specs = [pl.BlockSpec((tm, a.shape[1]), lambda m: (m, 0)) for a, _, _ in parts]
    w_specs = [pl.BlockSpec((a.shape[1], d), functools.partial(lambda m, blk: (blk, 0), blk=blk)) for a, _, blk in parts]
    return pl.pallas_call(
        _ordered_after(body, 2 * n + 4, len(after)), name=name, grid=(t // tm,),
        in_specs=act_specs + w_specs + [row, pl.BlockSpec((tm, 1), lambda m: (m, 0)), gain, row] + [_ANY] * len(after),
        out_specs=[row, gain], out_shape=[jax.ShapeDtypeStruct((t, d), F32), jax.ShapeDtypeStruct((1, d), F32)],
        compiler_params=_params(("arbitrary",)),
    )(*[a for a, _, _ in parts], *[w for _, w, _ in parts], x, rstd, g, dx_in, *after)


def ffn_forward(tag, x, g_norm, w3, normed=None):
    _, _, f4, d = w3.shape
    h, rstd = normed if normed is not None else rms_fwd(f"{tag}_rms", x, g_norm)
    a, b, s = ffn_up(f"{tag}_up", h, w3)
    x_out = mm_residual(f"{tag}_down", s, w3, pl.BlockSpec((w3.shape[0], None, f4, d), lambda m: (0, DOWN, 0, 0)), x, 0.5)
    return x_out, (x, h, rstd, a, b, s)


def ffn_backward_weights(tag, dx, saved, w3, after=()):
    x, h, rstd, a, b, s = saved
    da, db = ffn_bwd_act(f"{tag}_bwd_act", dx, w3, a, b, after)
    return ffn_wgrad(f"{tag}_wgrad", h, da, db, s, dx), (da, db)


def ffn_backward_input(tag, dx, saved, dab, g_norm, w3, after=()):
    x, h, rstd, a, b, s = saved
    return ffn_dh(f"{tag}_dh", dab[0], dab[1], w3, x, rstd, g_norm, dx, after)


def proj(name, h, wcat_t, bias, first_col, n_cols, tn, out_dtype, scaled_tiles=()):
    t, d = h.shape
    tm = _row_tile(t, 512)
    off = first_col // tn

    def epilogue(acc, ex, outs, ids):
        val = acc + ex[0][...]
        if scaled_tiles:
            hit = functools.reduce(jnp.logical_or, [ids[0] == s for s in scaled_tiles])
            val = val * jnp.where(hit, ATTN_SCALE, 1.0)
        outs[0][...] = val.astype(out_dtype)

    return _mm(
        name, [(h, wcat_t)], NT, (n_cols // tn, t // tm, 1),
        [(pl.BlockSpec((tm, d), lambda j, m, k: (m, 0)), pl.BlockSpec((tn, d), lambda j, m, k: (off + j, 0)))],
        [jax.ShapeDtypeStruct((t, n_cols), out_dtype)], [pl.BlockSpec((tm, tn), lambda j, m, k: (m, j))], (tm, tn), 1, epilogue,
        extras=[bias], extra_specs=[pl.BlockSpec((1, tn), lambda j, m, k: (0, off + j))],
    )[0]


def mix_fwd(name, ya, yb, wup, pc):
    t, w = ya.shape
    ns, _, tn = wup.shape
    d = ns * tn
    tm = _row_tile(t, 512)

    def body(ya_ref, yb_ref, wa_ref, wb_ref, pa_ref, pb_ref, ua_ref, ub_ref, mx_ref):
        ua = _dot(ya_ref[...], wa_ref[...], NN)
        ub = _dot(yb_ref[...], wb_ref[...], NN)
        ua_ref[...] = ua
        ub_ref[...] = ub
        mx_ref[...] = (_sigmoid(pa_ref[...]) * ua + _sigmoid(pb_ref[...]) * ub).astype(BF16)

    y_spec = pl.BlockSpec((tm, w), lambda m, n: (m, 0))
    o_spec = pl.BlockSpec((tm, tn), lambda m, n: (m, n))
    return pl.pallas_call(
        body, name=name, grid=(t // tm, ns),
        in_specs=[y_spec, y_spec, pl.BlockSpec((None, w, tn), lambda m, n: (n, 0, 0)), pl.BlockSpec((None, w, tn), lambda m, n: (n, 1, 0)),
                  o_spec, pl.BlockSpec((tm, tn), lambda m, n: (m, ns + n))],
        out_specs=[o_spec, o_spec, o_spec],
        out_shape=[jax.ShapeDtypeStruct((t, d), F32), jax.ShapeDtypeStruct((t, d), F32), jax.ShapeDtypeStruct((t, d), BF16)],
        compiler_params=_params(("parallel", "parallel")),
    )(ya, yb, wup, wup, pc, pc)


def up_bwd(name, du, wup, branch):
    t, d = du.shape
    ns, w2, tn = wup.shape
    w = w2 // 2
    tm = _row_tile(t, 512)

    def body(du_ref, w_ref, o_ref):
        acc = _dot(du_ref[:, 0:tn], w_ref[0], NT)
        for j in range(1, ns):
            acc += _dot(du_ref[:, j * tn:(j + 1) * tn], w_ref[j], NT)
        o_ref[...] = acc.astype(BF16)

    return pl.pallas_call(
        body, name=name, grid=(t // tm,),
        in_specs=[pl.BlockSpec((tm, d), lambda m: (m, 0)), pl.BlockSpec((ns, w, tn), lambda m: (0, branch, 0))],
        out_specs=pl.BlockSpec((tm, w), lambda m: (m, 0)), out_shape=jax.ShapeDtypeStruct((t, w), BF16),
        compiler_params=_params(("parallel",)),
    )(du, wup)


def up_wgrad(name, ya, yb, dua, dub):
    t, w = ya.shape
    d = dua.shape[1]
    tn = d // N_CHIPS

    def body(ya_ref, yb_ref, dua_ref, dub_ref, o_ref):
        o_ref[0:w, :] = _dot(ya_ref[...], dua_ref[...], TN).astype(BF16)
        o_ref[w:2 * w, :] = _dot(yb_ref[...], dub_ref[...], TN).astype(BF16)

    y_spec = pl.BlockSpec((t, w), lambda j: (0, 0))
    du_spec = pl.BlockSpec((t, tn), lambda j: (0, j))
    return pl.pallas_call(
        body, name=name, grid=(N_CHIPS,), in_specs=[y_spec, y_spec, du_spec, du_spec],
        out_specs=pl.BlockSpec((None, 2 * w, tn), lambda j: (j, 0, 0)),
        out_shape=jax.ShapeDtypeStruct((N_CHIPS, 2 * w, tn), BF16), compiler_params=_params(("parallel",)),
    )(ya, yb, dua, dub)


def mix_bwd(name, dx, wo, pc, ua, ub, after=()):
    t, d = dx.shape
    tm = _row_tile(t, 512)
    tn = 512
    off_a = 0
    off_b = d // tn

    def body(dx_ref, wo_ref, pa_ref, pb_ref, ua_ref, ub_ref, dua_ref, dub_ref, dpa_ref, dpb_ref, ba_ref, bb_ref):
        dm = _dot(dx_ref[...], wo_ref[...], NT)
        ga = _sigmoid(pa_ref[...])
        gb = _sigmoid(pb_ref[...])
        dua_ref[...] = (dm * ga).astype(BF16)
        dub_ref[...] = (dm * gb).astype(BF16)
        dpa = dm * ua_ref[...] * ga * (1.0 - ga)
        dpb = dm * ub_ref[...] * gb * (1.0 - gb)
        dpa_ref[...] = dpa.astype(BF16)
        dpb_ref[...] = dpb.astype(BF16)
        sa = jnp.sum(dpa, axis=0, keepdims=True)
        sb = jnp.sum(dpb, axis=0, keepdims=True)

        @pl.when(pl.program_id(1) == 0)
        def _():
            ba_ref[...] = sa
            bb_ref[...] = sb

        @pl.when(pl.program_id(1) != 0)
        def _():
            ba_ref[...] += sa
            bb_ref[...] += sb

    tile = pl.BlockSpec((tm, tn), lambda n, m: (m, n))
    bias = pl.BlockSpec((1, tn), lambda n, m: (0, n))
    return pl.pallas_call(
        _ordered_after(body, 6, len(after)), name=name, grid=(d // tn, t // tm),
        in_specs=[pl.BlockSpec((tm, d), lambda n, m: (m, 0)), pl.BlockSpec((tn, d), lambda n, m: (n, 0)),
                  pl.BlockSpec((tm, tn), lambda n, m: (m, off_a + n)), pl.BlockSpec((tm, tn), lambda n, m: (m, off_b + n)),
                  tile, tile] + [_ANY] * len(after),
        out_specs=[tile, tile, tile, tile, bias, bias],
        out_shape=[jax.ShapeDtypeStruct((t, d), BF16)] * 4 + [jax.ShapeDtypeStruct((1, d), F32)] * 2,
        compiler_params=_params(("parallel", "arbitrary")),
    )(dx, wo, pc, pc, ua, ub, *after)


def mm_plain(name, a, b, contract, out_dtype, tk_target=512):
    if contract == NN:
        m, kdim = a.shape
        n = b.shape[1]
    elif contract == NT:
        m, kdim = a.shape
        n = b.shape[0]
    else:
        kdim, m = a.shape
        n = b.shape[1]
    tm = _row_tile(m, 512)
    tk = _row_tile(kdim, tk_target)
    nk = kdim // tk
    if contract == TN:
        a_spec = pl.BlockSpec((tk, tm), lambda i, k: (k, i))
    else:
        a_spec = pl.BlockSpec((tm, tk), lambda i, k: (i, k))
    if contract == NT:
        b_spec = pl.BlockSpec((n, tk), lambda i, k: (0, k))
    else:
        b_spec = pl.BlockSpec((tk, n), lambda i, k: (k, 0))

    def epilogue(acc, ex, outs, ids):
        outs[0][...] = acc.astype(out_dtype)

    return _mm(name, [(a, b)], contract, (m // tm, nk), [(a_spec, b_spec)],
               [jax.ShapeDtypeStruct((m, n), out_dtype)], [pl.BlockSpec((tm, n), lambda i, k: (i, 0))], (tm, n), nk, epilogue)[0]


def wgrad_cat(name, h, dcat):
    t, d = h.shape
    n = dcat.shape[1]
    tn = next(c for c in (768, 512, 256, 128) if n % c == 0)
    tk = _row_tile(t, 2048)

    def epilogue(acc, ex, outs, ids):
        outs[0][...] = acc.astype(BF16)

    return _mm(
        name, [(dcat, h)], TN, (n // tn, t // tk),
        [(pl.BlockSpec((tk, tn), lambda j, k: (k, j)), pl.BlockSpec((tk, d), lambda j, k: (k, 0)))],
        [jax.ShapeDtypeStruct((n, d), BF16)], [pl.BlockSpec((tn, d), lambda j, k: (j, 0))], (tn, d), t // tk, epilogue,
    )[0]


def fox_prep(name, f, bias):
    t, lanes = f.shape
    nchunk = t // QB

    def body(f_ref, b_ref, c_ref):
        lower = (_iota2((QB, QB), 1) <= _iota2((QB, QB), 0)).astype(BF16)

        def chunk(n, carry):
            rows = pl.ds(pl.multiple_of(n * QB, QB), QB)
            u = f_ref[rows, :] + b_ref[...]
            lf = jnp.minimum(u, 0.0) - _log1pexp_neg_abs(u)
            c = _dot_exact_lhs01(lower, lf) + carry
            c_ref[rows, :] = c
            return c[QB - 1:QB, :]

        lax.fori_loop(0, nchunk, chunk, jnp.zeros((1, lanes), F32))

    return pl.pallas_call(body, name=name, out_shape=jax.ShapeDtypeStruct((t, lanes), F32),
                          compiler_params=pltpu.CompilerParams(vmem_limit_bytes=VMEM_LIMIT_BYTES))(f, bias)


def fox_gate_bwd(name, dc, f, bias):
    t, lanes = dc.shape
    nchunk = t // QB

    def body(dc_ref, f_ref, b_ref, df_ref, gb_ref):
        upper = (_iota2((QB, QB), 1) >= _iota2((QB, QB), 0)).astype(BF16)

        def chunk(n, carry):
            tail, total = carry
            rows = pl.ds(pl.multiple_of((nchunk - 1 - n) * QB, QB), QB)
            dlf = _dot_exact_lhs01(upper, dc_ref[rows, :]) + tail
            u = f_ref[rows, :] + b_ref[...]
            df = dlf * jnp.exp(jnp.minimum(-u, 0.0) - _log1pexp_neg_abs(u))
            df_ref[rows, :] = df
            return dlf[0:1, :], total + jnp.sum(df, axis=0, keepdims=True)

        zero = jnp.zeros((1, lanes), F32)
        _, total = lax.fori_loop(0, nchunk, chunk, (zero, zero))
        gb_ref[...] = total

    return pl.pallas_call(body, name=name,
                          out_shape=[jax.ShapeDtypeStruct((t, lanes), F32), jax.ShapeDtypeStruct((1, lanes), F32)],
                          compiler_params=pltpu.CompilerParams(vmem_limit_bytes=VMEM_LIMIT_BYTES))(dc, f, bias)


def _qrows(i):
    return pl.ds(pl.multiple_of(i * BQ, BQ), BQ)


def _krows(kc):
    return pl.ds(pl.multiple_of(kc * CS, CS), CS)


def _dot_split2_rhs01(x, m01):
    hi = x.astype(BF16)
    lo = (x - hi.astype(F32)).astype(BF16)
    d = lambda p: lax.dot_general(p, m01, (NN, ((), ())), preferred_element_type=F32)
    return d(hi) + d(lo)


def _live_rows(dchunk):
    return 0 if dchunk is None else dchunk * CS


def _diag_mask(dchunk, inclusive):
    shape = (BQ - _live_rows(dchunk), CS)
    return _iota2(shape, 1) <= _iota2(shape, 0) if inclusive else _iota2(shape, 1) < _iota2(shape, 0)


def _tail(x, r0, axis=0):
    return x if r0 == 0 else (x[r0:] if axis == 0 else x[:, r0:])


def _with_tail(old, tail, r0):
    return tail if r0 == 0 else jnp.concatenate([old[:r0], tail], axis=0)


def _walk_chunks(i, step, init, right_to_left):
    order = list(reversed(range(N_SUB))) if right_to_left else list(range(N_SUB))

    def diagonal(state):
        for dchunk in order:
            state = step(i * N_SUB + dchunk, state, dchunk)
        return state

    def group(n, state):
        base = ((i - 1 - n) if right_to_left else n) * N_SUB
        for dchunk in order:
            state = step(base + dchunk, state, None)
        return state

    if right_to_left:
        return lax.fori_loop(0, i, group, diagonal(init))
    return diagonal(lax.fori_loop(0, i, group, init))


PAIR = 2 * HEAD_DIM
N_PAIRS = N_HEADS // 2


def _pair_spec(t, first_block):
    return pl.BlockSpec((t, PAIR), lambda p, *_: (0, first_block + p))


def _head_lanes(shape):
    lane = _iota2(shape, len(shape) - 1)
    return [lane < HEAD_DIM, lane >= HEAD_DIM]


def _only_head(x, lanes_of_head):
    return jnp.where(lanes_of_head, x, jnp.zeros_like(x))


LOG2_E = 1.4426950408889634


def _sb_chunk_weights(q_h, k, later, carry, dchunk):
    z = _dot(q_h, k, NT) * LOG2_E
    lnb = -jnp.maximum(z, 0.0) - jnp.log2(1.0 + jnp.exp2(-jnp.abs(z)))
    lsz = lnb + z
    if dchunk is not None:
        lnb = jnp.where(_diag_mask(dchunk, False), lnb, 0.0)
    w = jnp.exp2(lsz + _dot_split2_rhs01(lnb, later) + carry)
    if dchunk is not None:
        w = jnp.where(_diag_mask(dchunk, False), w, 0.0)
    return w, lsz, lnb


def sb_pair_fwd(name, qkv):
    t = qkv.shape[0]

    def body(q_ref, k_ref, v_ref, o_ref):
        later = (_iota2((CS, CS), 0) > _iota2((CS, CS), 1)).astype(BF16)
        lanes = _head_lanes((BQ, PAIR))

        def qblock(i, _):
            q = q_ref[_qrows(i), :]
            q_heads = [_only_head(q, lanes[h]) for h in range(2)]

            def step(kc, state, dchunk):
                k = k_ref[_krows(kc), :]
                v = v_ref[_krows(kc), :]
                out = []
                r0 = _live_rows(dchunk)
                for h in range(2):
                    carry, acc = state[h]
                    w, _, lnb = _sb_chunk_weights(_tail(q_heads[h], r0), k, later, _tail(carry, r0), dchunk)
                    out.append((_with_tail(carry, _tail(carry, r0) + jnp.sum(lnb, axis=1, keepdims=True), r0),
                                _with_tail(acc, _tail(acc, r0) + _dot(w, v, NN), r0)))
                return tuple(out)

            zero = (jnp.zeros((BQ, 1), F32), jnp.zeros((BQ, PAIR), F32))
            (_, acc0), (_, acc1) = _walk_chunks(i, step, (zero, zero), True)
            o_ref[_qrows(i), :] = jnp.where(lanes[0], acc0, acc1).astype(BF16)
            return 0

        lax.fori_loop(0, t // BQ, qblock, 0)

    return pl.pallas_call(
        body, name=name, grid=(N_PAIRS,),
        in_specs=[_pair_spec(t, 0), _pair_spec(t, N_PAIRS), _pair_spec(t, 2 * N_PAIRS)],
        out_specs=_pair_spec(t, 0), out_shape=jax.ShapeDtypeStruct((t, WIDTH), BF16),
        compiler_params=_params(("parallel",)),
    )(qkv, qkv, qkv)


def _emit_dqkv(res, o_ref):
    o_ref[...] = res[pl.program_id(1)]


def _flush_transposed(acc, res, which):
    for kc in range(acc.shape[0]):
        res[which, kc * CS:(kc + 1) * CS, :] = acc[kc].T.astype(BF16)


def sb_pair_bwd(name, qkv, dy, dqkv):
    t = qkv.shape[0]
    nc = t // CS

    def body(q_ref, k_ref, v_ref, do_ref, _, o_ref, g_s, b_s, dkt_acc, dvt_acc, res):
        @pl.when(pl.program_id(1) == 0)
        def _():
            later = (_iota2((CS, CS), 0) > _iota2((CS, CS), 1)).astype(BF16)
            earlier = (_iota2((CS, CS), 0) < _iota2((CS, CS), 1)).astype(BF16)
            lanes = _head_lanes((BQ, PAIR))
            dkt_acc[...] = jnp.zeros_like(dkt_acc)
            dvt_acc[...] = jnp.zeros_like(dvt_acc)

            def qblock(i, _):
                q = q_ref[_qrows(i), :]
                do = do_ref[_qrows(i), :]
                zero = jnp.zeros((BQ, 1), F32)
                dqs = []
                for h in range(2):
                    q_h = _only_head(q, lanes[h])
                    do_h = _only_head(do, lanes[h])
                    qt_h = q_h.astype(F32).T.astype(BF16)
                    dot_h = do_h.astype(F32).T.astype(BF16)

                    def step1(kc, carry, dchunk, q_h=q_h, do_h=do_h, dot_h=dot_h):
                        k = k_ref[_krows(kc), :]
                        v = v_ref[_krows(kc), :]
                        r0 = _live_rows(dchunk)
                        live = _tail(carry, r0)
                        w, lsz, lnb = _sb_chunk_weights(_tail(q_h, r0), k, later, live, dchunk)
                        g_s[kc, r0:, :] = (w * _dot(_tail(do_h, r0), v, NT)).astype(BF16)
                        b_s[kc, r0:, :] = jnp.exp2(lsz).astype(BF16)
                        dvt_acc[kc] += _dot(_tail(dot_h, r0, axis=1), w, NN)
                        return _with_tail(carry, live + jnp.sum(lnb, axis=1, keepdims=True), r0)

                    _walk_chunks(i, step1, zero, True)

                    def step2(kc, state, dchunk, qt_h=qt_h):
                        k = k_ref[_krows(kc), :]
                        r0 = _live_rows(dchunk)
                        before, dq = state
                        g16 = g_s[kc, r0:, :]
                        g = g16.astype(F32)
                        beta = b_s[kc, r0:, :].astype(F32)
                        prefix = lax.dot_general(g16, earlier, (NN, ((), ())), preferred_element_type=F32) + _tail(before, r0)
                        dz = g * (1.0 - beta) - beta * prefix
                        if dchunk is not None:
                            dz = jnp.where(_diag_mask(dchunk, False), dz, 0.0)
                        dzb = dz.astype(BF16)
                        dkt_acc[kc] += _dot(_tail(qt_h, r0, axis=1), dzb, NN)
                        return (_with_tail(before, _tail(before, r0) + jnp.sum(g, axis=1, keepdims=True), r0),
                                _with_tail(dq, _tail(dq, r0) + _dot(dzb, k, NN), r0))

                    dqs.append(_walk_chunks(i, step2, (zero, jnp.zeros((BQ, PAIR), F32)), False)[1])
                res[0, _qrows(i), :] = (jnp.where(lanes[0], dqs[0], dqs[1]) * ATTN_SCALE).astype(BF16)
                return 0

            lax.fori_loop(0, t // BQ, qblock, 0)
            _flush_transposed(dkt_acc, res, 1)
            _flush_transposed(dvt_acc, res, 2)

        _emit_dqkv(res, o_ref)

    return pl.pallas_call(
        body, name=name, grid=(N_PAIRS, 3),
        in_specs=[_pair_spec(t, 0), _pair_spec(t, N_PAIRS), _pair_spec(t, 2 * N_PAIRS), _pair_spec(t, 0), _ANY],
        out_specs=pl.BlockSpec((t, PAIR), lambda p, s: (0, s * N_PAIRS + p)),
        out_shape=jax.ShapeDtypeStruct(dqkv.shape, BF16), input_output_aliases={4: 0},
        scratch_shapes=[pltpu.VMEM((nc, BQ, CS), BF16), pltpu.VMEM((nc, BQ, CS), BF16),
                        pltpu.VMEM((nc, PAIR, CS), F32), pltpu.VMEM((nc, PAIR, CS), F32), pltpu.VMEM((3, t, PAIR), BF16)],
        compiler_params=_params(("parallel", "arbitrary")),
    )(qkv, qkv, qkv, dy, dqkv)


def _gates_col_spec(t):
    return pl.BlockSpec((2, t, 1), lambda p, *_: (p, 0, 0))


def _gates_row_spec(nc):
    return pl.BlockSpec((2, nc, 1, CS), lambda p, *_: (p, 0, 0, 0))


def fox_pair_fwd(name, qkv, c_col, c_row):
    t = qkv.shape[0]

    def body(q_ref, k_ref, v_ref, cc_ref, cr_ref, o_ref, lse_ref):
        lanes = _head_lanes((BQ, PAIR))

        def qblock(i, _):
            q = q_ref[_qrows(i), :]
            q_heads = [_only_head(q, lanes[h]) for h in range(2)]
            ct = [cc_ref[h, _qrows(i), :] for h in range(2)]

            def step(kc, state, dchunk):
                k = k_ref[_krows(kc), :]
                v = v_ref[_krows(kc), :]
                out = []
                r0 = _live_rows(dchunk)
                for h in range(2):
                    m, l, acc = (_tail(a, r0) for a in state[h])
                    s = _dot(_tail(q_heads[h], r0), k, NT) + _tail(ct[h], r0) - cr_ref[h, kc]
                    if dchunk is not None:
                        s = jnp.where(_diag_mask(dchunk, True), s, NEG_BIG)
                    m_new = jnp.maximum(m, jnp.max(s, axis=1, keepdims=True))
                    alpha = jnp.exp(m - m_new)
                    p = jnp.exp(s - m_new)
                    if dchunk is not None:
                        p = jnp.where(_diag_mask(dchunk, True), p, 0.0)
                    new = (m_new, l * alpha + jnp.sum(p, axis=1, keepdims=True), acc * alpha + _dot(p, v, NN))
                    out.append(tuple(_with_tail(old, tail, r0) for old, tail in zip(state[h], new)))
                return tuple(out)

            init = (jnp.full((BQ, 1), NEG_BIG, F32), jnp.zeros((BQ, 1), F32), jnp.zeros((BQ, PAIR), F32))
            (m0, l0, acc0), (m1, l1, acc1) = _walk_chunks(i, step, (init, init), False)
            o_ref[_qrows(i), :] = jnp.where(lanes[0], acc0 / l0, acc1 / l1).astype(BF16)
            lse_ref[0, _qrows(i), :] = m0 + jnp.log(l0)
            lse_ref[1, _qrows(i), :] = m1 + jnp.log(l1)
            return 0

        lax.fori_loop(0, t // BQ, qblock, 0)

    return pl.pallas_call(
        body, name=name, grid=(N_PAIRS,),
        in_specs=[_pair_spec(t, 3 * N_PAIRS), _pair_spec(t, 4 * N_PAIRS), _pair_spec(t, 5 * N_PAIRS),
                  _gates_col_spec(t), _gates_row_spec(t // CS)],
        out_specs=[_pair_spec(t, 0), _gates_col_spec(t)],
        out_shape=[jax.ShapeDtypeStruct((t, WIDTH), BF16), jax.ShapeDtypeStruct((N_HEADS, t, 1), F32)],
        compiler_params=_params(("parallel",)),
    )(qkv, qkv, qkv, c_col, c_row)


def fox_pair_bwd(name, qkv, y, dy, lse, c_col, c_row, dqkv):
    t = qkv.shape[0]
    nc = t // CS

    def body(q_ref, k_ref, v_ref, o_in_ref, do_ref, lse_ref, cc_ref, cr_ref, _, o_ref, dcc_ref, dcr_ref,
             dkt_acc, dvt_acc, dcr_acc, res):
        @pl.when(pl.program_id(1) == 0)
        def _():
            lanes = _head_lanes((BQ, PAIR))
            dkt_acc[...] = jnp.zeros_like(dkt_acc)
            dvt_acc[...] = jnp.zeros_like(dvt_acc)
            dcr_acc[...] = jnp.zeros_like(dcr_acc)

            def qblock(i, _):
                q = q_ref[_qrows(i), :]
                do = do_ref[_qrows(i), :]
                q_heads = [_only_head(q, lanes[h]) for h in range(2)]
                do_heads = [_only_head(do, lanes[h]) for h in range(2)]
                qt_heads = [qh.astype(F32).T.astype(BF16) for qh in q_heads]
                dot_heads = [dh.astype(F32).T.astype(BF16) for dh in do_heads]
                prod = do.astype(F32) * o_in_ref[_qrows(i), :].astype(F32)
                delta = [jnp.sum(_only_head(prod, lanes[h]), axis=1, keepdims=True) for h in range(2)]
                ct = [cc_ref[h, _qrows(i), :] for h in range(2)]
                lse_i = [lse_ref[h, _qrows(i), :] for h in range(2)]

                def step(kc, state, dchunk):
                    k = k_ref[_krows(kc), :]
                    v = v_ref[_krows(kc), :]
                    out = []
                    r0 = _live_rows(dchunk)
                    for h in range(2):
                        dq, dct = state[h]
                        s = _dot(_tail(q_heads[h], r0), k, NT) + _tail(ct[h], r0) - cr_ref[h, kc]
                        p = jnp.exp(s - _tail(lse_i[h], r0))
                        if dchunk is not None:
                            p = jnp.where(_diag_mask(dchunk, True), p, 0.0)
                        ds = p * (_dot(_tail(do_heads[h], r0), v, NT) - _tail(delta[h], r0))
                        dvt_acc[kc] += _dot(_tail(dot_heads[h], r0, axis=1), p, NN)
                        dsb = ds.astype(BF16)
                        dkt_acc[kc] += _dot(_tail(qt_heads[h], r0, axis=1), dsb, NN)
                        dcr_acc[h, kc] -= jnp.sum(ds, axis=0, keepdims=True)
                        out.append((_with_tail(dq, _tail(dq, r0) + _dot(dsb, k, NN), r0),
                                    _with_tail(dct, _tail(dct, r0) + jnp.sum(ds, axis=1, keepdims=True), r0)))
                    return tuple(out)

                zero = (jnp.zeros((BQ, PAIR), F32), jnp.zeros((BQ, 1), F32))
                (dq0, dct0), (dq1, dct1) = _walk_chunks(i, step, (zero, zero), False)
                res[0, _qrows(i), :] = (jnp.where(lanes[0], dq0, dq1) * ATTN_SCALE).astype(BF16)
                lane = _iota2((BQ, PAIR), 1)
                dcc_ref[_qrows(i), :] = jnp.where(lane == 0, dct0, jnp.where(lane == 1, dct1, 0.0))
                return 0

            lax.fori_loop(0, t // BQ, qblock, 0)
            _flush_transposed(dkt_acc, res, 1)
            _flush_transposed(dvt_acc, res, 2)
            dcr_ref[...] = dcr_acc[...]

        _emit_dqkv(res, o_ref)

    return pl.pallas_call(
        body, name=name, grid=(N_PAIRS, 3),
        in_specs=[_pair_spec(t, 3 * N_PAIRS), _pair_spec(t, 4 * N_PAIRS), _pair_spec(t, 5 * N_PAIRS), _pair_spec(t, 0),
                  _pair_spec(t, 0), _gates_col_spec(t), _gates_col_spec(t), _gates_row_spec(nc), _ANY],
        out_specs=[pl.BlockSpec((t, PAIR), lambda p, s: (0, (3 + s) * N_PAIRS + p)),
                   pl.BlockSpec((None, t, PAIR), lambda p, s: (p, 0, 0)), _gates_row_spec(nc)],
        out_shape=[jax.ShapeDtypeStruct(dqkv.shape, BF16), jax.ShapeDtypeStruct((N_PAIRS, t, PAIR), F32),
                   jax.ShapeDtypeStruct((N_HEADS, nc, 1, CS), F32)],
        input_output_aliases={8: 0},
        scratch_shapes=[pltpu.VMEM((nc, PAIR, CS), F32), pltpu.VMEM((nc, PAIR, CS), F32), pltpu.VMEM((2, nc, 1, CS), F32),
                        pltpu.VMEM((3, t, PAIR), BF16)],
        compiler_params=_params(("parallel", "arbitrary")),
    )(qkv, qkv, qkv, y, dy, lse, c_col, c_row, dqkv)


def loss_head(name, x, g, target):
    t, d = x.shape
    tr = _row_tile(t, 256)

    def body(x_ref, g_ref, t_ref, dx_ref, gn_ref, loss_ref):
        xv = x_ref[...]
        r = lax.rsqrt(jnp.mean(xv * xv, axis=-1, keepdims=True) + RMS_EPS)
        xhat = xv * r
        gv = g_ref[...]
        err = xhat * gv - t_ref[...]
        part_loss = 0.5 * jnp.sum(jnp.mean(err * err, axis=-1, keepdims=True), axis=0, keepdims=True)
        dy = err * (1.0 / d)
        dyg = dy * gv
        dx_ref[...] = r * (dyg - xhat * jnp.mean(dyg * xhat, axis=-1, keepdims=True))
        part_g = jnp.sum(dy * xhat, axis=0, keepdims=True)

        @pl.when(pl.program_id(0) == 0)
        def _():
            gn_ref[...] = part_g
            loss_ref[...] = part_loss

        @pl.when(pl.program_id(0) != 0)
        def _():
            gn_ref[...] += part_g
            loss_ref[...] += part_loss

    row = pl.BlockSpec((tr, d), lambda i: (i, 0))
    return pl.pallas_call(
        body, name=name, grid=(t // tr,),
        in_specs=[row, pl.BlockSpec((1, d), lambda i: (0, 0)), row],
        out_specs=[row, pl.BlockSpec((1, d), lambda i: (0, 0)), pl.BlockSpec((1, 1), lambda i: (0, 0))],
        out_shape=[jax.ShapeDtypeStruct((t, d), F32), jax.ShapeDtypeStruct((1, d), F32), jax.ShapeDtypeStruct((1, 1), F32)],
        compiler_params=_params(("arbitrary",)),
    )(x, g, target)


def _place():
    return lax.axis_index("x"), lax.axis_index("y"), lax.axis_index("c")


def _other_chips(x, y):
    return [(1 - x, y), (x, 1 - y), (1 - x, 1 - y)]


def _half(ref, c, rows):
    if rows % 32 == 0:
        return ref.at[:, pl.ds(c * (rows // 2), rows // 2), :]
    cols = ref.shape[2]
    return ref.at[:, :, pl.ds(c * (cols // 2), cols // 2)]


_ANY = pl.BlockSpec(memory_space=pl.ANY)


_HBM = pl.BlockSpec(memory_space=pltpu.HBM)
_SEM = pl.BlockSpec(memory_space=pltpu.SEMAPHORE)
_DATAFLOW = pltpu.SideEffectType.DATAFLOW_SIDE_EFFECTING


def _in_hbm(a):
    return pltpu.with_memory_space_constraint(a, pltpu.HBM)


def _gather_ici_copies(bufs, send_sems, recv_sems, arrivals):
    x, y, c = _place()
    me = 2 * x + y
    copies = []
    for i, buf in enumerate(bufs):
        rows = buf.shape[2]
        for j, (qx, qy) in enumerate(_other_chips(x, y)):
            block = _half(buf.at[2 * qx + qy if arrivals else me], c, rows)
            copies.append(pltpu.make_async_remote_copy(
                src_ref=block, dst_ref=block, send_sem=send_sems.at[3 * i + j], recv_sem=recv_sems.at[3 * i + j],
                device_id=(qx, qy, c), device_id_type=MESH))
    return copies


def gather_ici_start(name, bufs, after):
    n = len(bufs)

    def body(*refs):
        ins = refs[:n]
        send_sems, recv_sems = refs[n + 1], refs[n + 2]
        token = refs[-1]
        for send in _gather_ici_copies(ins, send_sems, recv_sems, False):
            send.start()
        token[...] = jnp.zeros_like(token)

    res = pl.pallas_call(
        body, name=name,
        out_shape=(pltpu.SemaphoreType.DMA((3 * n,)), pltpu.SemaphoreType.DMA((3 * n,)), *[pltpu.HBM(b.shape, b.dtype) for b in bufs],
                   jax.ShapeDtypeStruct((8, 128), F32)),
        in_specs=[_HBM] * n + [_ANY], out_specs=(_SEM, _SEM, *[_HBM] * n, pl.BlockSpec(memory_space=pltpu.VMEM)),
        input_output_aliases={i: 2 + i for i in range(n)},
        compiler_params=pltpu.CompilerParams(has_side_effects=_DATAFLOW),
    )(*[_in_hbm(b) for b in bufs], after)
    return res[0], res[1], list(res[2:2 + n]), res[-1]


def gather_ici_wait(name, send_sems, recv_sems, bufs, after):
    n = len(bufs)

    def body(*refs):
        ins = refs[:n]
        send_sems_ref, recv_sems_ref = refs[n], refs[n + 1]
        for send in _gather_ici_copies(ins, send_sems_ref, recv_sems_ref, False):
            send.wait_send()
        for recv in _gather_ici_copies(ins, send_sems_ref, recv_sems_ref, True):
            recv.wait_recv()

    return pl.pallas_call(
        body, name=name, out_shape=tuple(pltpu.HBM(b.shape, b.dtype) for b in bufs),
        in_specs=[_HBM] * n + [_SEM, _SEM, _ANY], out_specs=tuple([_HBM] * n),
        input_output_aliases={i: i for i in range(n)},
        compiler_params=pltpu.CompilerParams(has_side_effects=_DATAFLOW),
    )(*bufs, send_sems, recv_sems, after)


def gather_forward(name, bufs):
    n = len(bufs)

    def body(*refs):
        outs = refs[n:2 * n]
        send_sems, recv_sems = refs[2 * n:]
        x, y, c = _place()
        sibling = (x, y, 1 - c)
        sends = []
        for i in range(n):
            rows = outs[i].shape[2]
            for j, (qx, qy) in enumerate(_other_chips(x, y)):
                block = _half(outs[i].at[2 * qx + qy], c, rows)
                fw = pltpu.make_async_remote_copy(
                    src_ref=block, dst_ref=block, send_sem=send_sems.at[3 * i + j], recv_sem=recv_sems.at[3 * i + j],
                    device_id=sibling, device_id_type=MESH)
                fw.start()
                sends.append(fw)
        for i in range(n):
            rows = outs[i].shape[2]
            for j, (qx, qy) in enumerate(_other_chips(x, y)):
                block = _half(outs[i].at[2 * qx + qy], 1 - c, rows)
                pltpu.make_async_remote_copy(
                    src_ref=block, dst_ref=block, send_sem=send_sems.at[3 * i + j], recv_sem=recv_sems.at[3 * i + j],
                    device_id=sibling, device_id_type=MESH).wait_recv()
        for fw in sends:
            fw.wait_send()

    return pl.pallas_call(
        body, name=name, in_specs=[_ANY] * n, out_specs=[_ANY] * n,
        out_shape=[jax.ShapeDtypeStruct(b.shape, b.dtype) for b in bufs],
        input_output_aliases={i: i for i in range(n)},
        scratch_shapes=[pltpu.SemaphoreType.DMA((3 * n,)), pltpu.SemaphoreType.DMA((3 * n,))],
        compiler_params=pltpu.CompilerParams(has_side_effects=True),
    )(*bufs)


def _between_chips_copies(parts, lands, send_sems, recv_sems):
    x, y, c = _place()
    copies = []
    for i, (part, land) in enumerate(zip(parts, lands)):
        for j, (qx, qy) in enumerate(_other_chips(x, y)):
            copies.append(pltpu.make_async_remote_copy(
                src_ref=part.at[2 * qx + qy], dst_ref=land.at[j], send_sem=send_sems.at[3 * i + j], recv_sem=recv_sems.at[3 * i + j],
                device_id=(qx, qy, c), device_id_type=MESH))
    return copies


def between_chips_start(name, parts):
    n = len(parts)
    lands = [lax.empty((N_CHIPS - 1,) + p.shape[1:], p.dtype) for p in parts]

    def body(*refs):
        send_sems, recv_sems = refs[2 * n], refs[2 * n + 1]
        token = refs[-1]
        for cp in _between_chips_copies(refs[:n], refs[n:2 * n], send_sems, recv_sems):
            cp.start()
        token[...] = jnp.zeros_like(token)

    res = pl.pallas_call(
        body, name=name,
        out_shape=(pltpu.SemaphoreType.DMA((3 * n,)), pltpu.SemaphoreType.DMA((3 * n,)),
                   *[pltpu.HBM(a.shape, a.dtype) for a in parts + lands], jax.ShapeDtypeStruct((8, 128), F32)),
        in_specs=[_HBM] * (2 * n), out_specs=(_SEM, _SEM, *[_HBM] * (2 * n), pl.BlockSpec(memory_space=pltpu.VMEM)),
        input_output_aliases={i: 2 + i for i in range(2 * n)},
        compiler_params=pltpu.CompilerParams(has_side_effects=_DATAFLOW),
    )(*[_in_hbm(a) for a in parts + lands])
    return res[0], res[1], list(res[2:2 + n]), list(res[2 + n:2 + 2 * n]), res[-1]


def between_chips_wait(name, send_sems, recv_sems, parts, lands, after):
    n = len(parts)

    def body(*refs):
        for cp in _between_chips_copies(refs[:n], refs[n:2 * n], refs[2 * n], refs[2 * n + 1]):
            cp.wait_send()
            cp.wait_recv()

    res = pl.pallas_call(
        body, name=name, out_shape=tuple(pltpu.HBM(a.shape, a.dtype) for a in parts + lands),
        in_specs=[_HBM] * (2 * n) + [_SEM, _SEM, _ANY], out_specs=tuple([_HBM] * (2 * n)),
        input_output_aliases={i: i for i in range(2 * n)},
        compiler_params=pltpu.CompilerParams(has_side_effects=_DATAFLOW),
    )(*parts, *lands, send_sems, recv_sems, after)
    return list(res[:n]), list(res[n:])


def exchange_start(name, arrays, n_copies, copies, after=()):
    n = len(arrays)

    def body(*refs):
        send_sems, recv_sems = refs[n + len(after)], refs[n + len(after) + 1]
        for cp in copies(refs[:n], send_sems, recv_sems):
            cp.start()
        refs[-1][...] = jnp.zeros_like(refs[-1])

    res = pl.pallas_call(
        body, name=name,
        out_shape=(pltpu.SemaphoreType.DMA((n_copies,)), pltpu.SemaphoreType.DMA((n_copies,)),
                   *[pltpu.HBM(a.shape, a.dtype) for a in arrays], jax.ShapeDtypeStruct((8, 128), F32)),
        in_specs=[_HBM] * n + [_ANY] * len(after),
        out_specs=(_SEM, _SEM, *[_HBM] * n, pl.BlockSpec(memory_space=pltpu.VMEM)),
        input_output_aliases={i: 2 + i for i in range(n)},
        compiler_params=pltpu.CompilerParams(has_side_effects=_DATAFLOW),
    )(*[_in_hbm(a) for a in arrays], *after)
    return res[0], res[1], list(res[2:2 + n]), res[-1]


def exchange_wait(name, send_sems, recv_sems, arrays, copies, after):
    n = len(arrays)

    def body(*refs):
        for cp in copies(refs[:n], refs[n], refs[n + 1]):
            cp.wait_send()
            cp.wait_recv()

    return list(pl.pallas_call(
        body, name=name, out_shape=tuple(pltpu.HBM(a.shape, a.dtype) for a in arrays),
        in_specs=[_HBM] * n + [_SEM, _SEM, _ANY], out_specs=tuple([_HBM] * n),
        input_output_aliases={i: i for i in range(n)},
        compiler_params=pltpu.CompilerParams(has_side_effects=_DATAFLOW),
    )(*arrays, send_sems, recv_sems, after))


def _to_sibling_copies(n):
    def copies(refs, send_sems, recv_sems):
        x, y, c = _place()
        out = []
        for i in range(n):
            rows = refs[i].shape[2]
            out.append(pltpu.make_async_remote_copy(
                src_ref=refs[i].at[:, :, pl.ds((1 - c) * (rows // 2), rows // 2), :], dst_ref=refs[n + i],
                send_sem=send_sems.at[i], recv_sem=recv_sems.at[i], device_id=(x, y, 1 - c), device_id_type=MESH))
        return out
    return copies


def _share_copies(n):
    def copies(refs, send_sems, recv_sems):
        x, y, c = _place()
        out = []
        for i in range(n):
            mine = _half(refs[i], c, refs[i].shape[1])
            out.append(pltpu.make_async_remote_copy(
                src_ref=mine, dst_ref=mine, send_sem=send_sems.at[i], recv_sem=recv_sems.at[i],
                device_id=(x, y, 1 - c), device_id_type=MESH))
        return out
    return copies


def pair_sum(name, grad, recv, c):
    ns, na, rh, cols = recv.shape
    tr = _row_tile(rh, 256) if rh % 256 == 0 else rh
    nt = rh // tr

    def body(c_ref, g_ref, r_ref, o_ref):
        o_ref[...] = (g_ref[...].astype(F32) + r_ref[...].astype(F32)).astype(BF16)

    blk = (None, None, tr, cols)
    return pl.pallas_call(
        body, name=name,
        grid_spec=pltpu.PrefetchScalarGridSpec(
            num_scalar_prefetch=1, grid=(ns, na, nt),
            in_specs=[pl.BlockSpec(blk, lambda s, a, r, c_ref: (s, a, c_ref[0] * nt + r, 0)),
                      pl.BlockSpec(blk, lambda s, a, r, c_ref: (s, a, r, 0))],
            out_specs=pl.BlockSpec(blk, lambda s, a, r, c_ref: (s, a, r, 0))),
        out_shape=jax.ShapeDtypeStruct(recv.shape, BF16),
        compiler_params=_params(("parallel", "parallel", "parallel")),
    )(c, grad, recv)


def chip_sum(name, parts, landed, place):
    _, na, rh, cols = parts.shape
    tr = _row_tile(rh, 256) if rh % 256 == 0 else rh
    nt = rh // tr

    def body(place_ref, p_ref, l_ref, o_ref):
        total = p_ref[...].astype(F32)
        for s in range(N_CHIPS - 1):
            total = total + l_ref[s].astype(F32)
        o_ref[...] = total

    return pl.pallas_call(
        body, name=name,
        grid_spec=pltpu.PrefetchScalarGridSpec(
            num_scalar_prefetch=1, grid=(na, nt),
            in_specs=[pl.BlockSpec((None, None, tr, cols), lambda a, r, pr: (pr[1], a, r, 0)),
                      pl.BlockSpec((N_CHIPS - 1, None, tr, cols), lambda a, r, pr: (0, a, r, 0))],
            out_specs=pl.BlockSpec((None, tr, cols), lambda a, r, pr: (a, pr[0] * nt + r, 0))),
        out_shape=jax.ShapeDtypeStruct((na, 2 * rh, cols), F32),
        compiler_params=_params(("parallel", "parallel")),
    )(place, parts, landed)


def reduce_scatter_1(tag, grads):
    n = len(grads)
    lands = [lax.empty((g.shape[0], g.shape[1], g.shape[2] // 2, g.shape[3]), g.dtype) for g in grads]
    send_sems, recv_sems, arrays, token = exchange_start(f"{tag}_to_sibling_start", list(grads) + lands, n, _to_sibling_copies(n))
    return (send_sems, recv_sems, arrays), token


def reduce_scatter_2(tag, state, place, after):
    send_sems, recv_sems, arrays = state
    n = len(arrays) // 2
    arrays = exchange_wait(f"{tag}_to_sibling_wait", send_sems, recv_sems, arrays, _to_sibling_copies(n), after)
    parts = [pair_sum(f"{tag}_pair_sum{i}", g, r, place) for i, (g, r) in enumerate(zip(arrays[:n], arrays[n:]))]
    send_sems, recv_sems, parts, lands, token = between_chips_start(f"{tag}_between_chips_start", parts)
    return (send_sems, recv_sems, parts, lands), token


def reduce_scatter_3(tag, state, place, after):
    send_sems, recv_sems, parts, lands = state
    parts, landed = between_chips_wait(f"{tag}_between_chips_wait", send_sems, recv_sems, parts, lands, after)
    halves = [chip_sum(f"{tag}_chip_sum{i}", p, l, place) for i, (p, l) in enumerate(zip(parts, landed))]
    send_sems, recv_sems, halves, token = exchange_start(f"{tag}_share_start", halves, len(halves), _share_copies(len(halves)))
    return (send_sems, recv_sems, halves), token


def reduce_scatter_4(tag, state, after):
    send_sems, recv_sems, halves = state
    return exchange_wait(f"{tag}_share_wait", send_sems, recv_sems, halves, _share_copies(len(halves)), after)


def _adamw_math(w, g, m, v):
    m = ADAM_B1 * m + (1.0 - ADAM_B1) * g
    v = ADAM_B2 * v + (1.0 - ADAM_B2) * (g * g)
    m_hat = m / (1.0 - ADAM_B1 ** ADAM_STEP)
    v_hat = v / (1.0 - ADAM_B2 ** ADAM_STEP)
    delta = -ADAM_LR * (m_hat / (jnp.sqrt(v_hat) + ADAM_EPS) + ADAM_WD * w)
    return delta, m, v


def adamw(name, w, g, m, v, after=()):
    rows, cols = w.shape
    tr = _row_tile(rows, 256) if rows % 256 == 0 else rows // 2

    def body(w_ref, g_ref, m_ref, v_ref, d_ref, mo_ref, vo_ref):
        d_ref[...], mo_ref[...], vo_ref[...] = _adamw_math(w_ref[...], g_ref[...], m_ref[...], v_ref[...])

    blk = pl.BlockSpec((tr, cols), lambda i: (i, 0))
    return pl.pallas_call(
        _ordered_after(body, 4, len(after)), name=name, grid=(rows // tr,), in_specs=[blk] * 4 + [_ANY] * len(after),
        out_specs=[blk] * 3, out_shape=[jax.ShapeDtypeStruct(w.shape, F32)] * 3, compiler_params=_params(("parallel",)),
    )(w, g, m, v, *after)


def adamw_rows(name, w, g, m, v, after=()):
    rows, _, cols = w.shape
    tr = next(r for r in (128, 110, 64, 32, 16, 8, 1) if rows % r == 0)

    def body(w_ref, g_ref, m_ref, v_ref, d_ref, mo_ref, vo_ref):
        d_ref[...], mo_ref[...], vo_ref[...] = _adamw_math(w_ref[...], g_ref[...], m_ref[...], v_ref[...])

    blk = pl.BlockSpec((tr, 1, cols), lambda i: (i, 0, 0))
    return pl.pallas_call(
        _ordered_after(body, 4, len(after)), name=name, grid=(rows // tr,), in_specs=[blk] * 4 + [_ANY] * len(after),
        out_specs=[blk] * 3, out_shape=[jax.ShapeDtypeStruct(w.shape, F32)] * 3, compiler_params=_params(("parallel",)),
    )(w, g, m, v, *after)


def adamw_stacked(name, ws, g, ms, vs, after=()):
    n = len(ws)
    rows, cols = ws[0].shape
    tr = next(r for r in (128, 88, 64, 32, 16, 8) if rows % r == 0)

    def body(*refs):
        w_refs, m_refs, v_refs, g_ref = refs[:n], refs[n:2 * n], refs[2 * n:3 * n], refs[3 * n]
        outs = refs[3 * n + 1:]
        for i in range(n):
            outs[i][...], outs[n + i][...], outs[2 * n + i][...] = _adamw_math(
                w_refs[i][...], g_ref[i], m_refs[i][...], v_refs[i][...])

    blk = pl.BlockSpec((tr, cols), lambda r: (r, 0))
    res = pl.pallas_call(
        _ordered_after(body, 3 * n + 1, len(after)), name=name, grid=(rows // tr,),
        in_specs=[blk] * (3 * n) + [pl.BlockSpec((n, tr, cols), lambda r: (0, r, 0))] + [_ANY] * len(after),
        out_specs=[blk] * (3 * n),
        out_shape=[jax.ShapeDtypeStruct((rows, cols), F32)] * (3 * n), compiler_params=_params(("parallel",)),
    )(*ws, *ms, *vs, g, *after)
    return res[:n], res[n:2 * n], res[2 * n:]


def small_allreduce_adamw(name, g_part, w, m, v, after=()):
    rows, cols = g_part.shape

    def body(g_ref, w_ref, m_ref, v_ref, sum_ref, d_ref, mo_ref, vo_ref, land, send_sems, recv_sems):
        x, y, c = _place()
        me = 4 * x + 2 * y + c
        land[me] = g_ref[...]
        copies = []
        for r in range(1, 8):
            peer = (x ^ (r >> 2), y ^ ((r >> 1) & 1), c ^ (r & 1))
            rc = pltpu.make_async_remote_copy(
                src_ref=g_ref, dst_ref=land.at[me], send_sem=send_sems.at[r - 1], recv_sem=recv_sems.at[r - 1],
                device_id=peer, device_id_type=MESH)
            rc.start()
            copies.append(rc)
        for rc in copies:
            rc.wait()
        total = land[0]
        for s in range(1, 8):
            total = total + land[s]
        sum_ref[...] = total
        d_ref[...], mo_ref[...], vo_ref[...] = _adamw_math(w_ref[...], total, m_ref[...], v_ref[...])

    vmem = pl.BlockSpec(memory_space=pltpu.VMEM)
    return pl.pallas_call(
        _ordered_after(body, 4, len(after)), name=name, in_specs=[vmem] * 4 + [_ANY] * len(after), out_specs=[vmem] * 4,
        out_shape=[jax.ShapeDtypeStruct((rows, cols), F32)] * 4,
        scratch_shapes=[pltpu.VMEM((8, rows, cols), F32), pltpu.SemaphoreType.DMA((7,)), pltpu.SemaphoreType.DMA((7,))],
        compiler_params=pltpu.CompilerParams(has_side_effects=True),
    )(g_part, w, m, v, *after)


def kernel(x, norm_ffn1, w_ffn1_gate, w_ffn1_up, w_ffn1_down, norm_mix, w_in, b_forget, w_gate, b_gate, w_up_a, w_up_b, w_out, norm_ffn2, w_ffn2_gate, w_ffn2_up, w_ffn2_down, norm_final, loss_target, m_norm_ffn1, m_w_ffn1_gate, m_w_ffn1_up, m_w_ffn1_down, m_norm_mix, m_w_in, m_b_forget, m_w_gate, m_b_gate, m_w_up_a, m_w_up_b, m_w_out, m_norm_ffn2, m_w_ffn2_gate, m_w_ffn2_up, m_w_ffn2_down, m_norm_final, v_norm_ffn1, v_w_ffn1_gate, v_w_ffn1_up, v_w_ffn1_down, v_norm_mix, v_w_in, v_b_forget, v_w_gate, v_b_gate, v_w_up_a, v_w_up_b, v_w_out, v_norm_ffn2, v_w_ffn2_gate, v_w_ffn2_up, v_w_ffn2_down, v_norm_final):
    t, d = x.shape[1], x.shape[2]
    in4 = w_in.shape[2]
    gate4 = w_gate.shape[2]
    in_cols = N_CHIPS * in4
    n_forget = in_cols - QKV_COLS
    assert w_up_a.shape[1] == WIDTH and d == 2 * WIDTH and n_forget == N_HEADS and t % BQ == 0
    chip = 2 * lax.axis_index("x") + lax.axis_index("y")
    c_arr = jnp.stack([lax.axis_index("c"), chip]).astype(jnp.int32)
    x2d = x[0]
    target = loss_target[0]

    def slot(shard):
        return lax.dynamic_update_slice(lax.empty((N_CHIPS,) + shard.shape, BF16), shard.astype(BF16)[None], (chip, 0, 0, 0))

    def ffn_views(wg, wu, wd):
        return [wg[0].T, wu[0].T, wd[0]]

    ffn1_w, ffn1_m, ffn1_v = (ffn_views(w_ffn1_gate, w_ffn1_up, w_ffn1_down), ffn_views(m_w_ffn1_gate, m_w_ffn1_up, m_w_ffn1_down),
                              ffn_views(v_w_ffn1_gate, v_w_ffn1_up, v_w_ffn1_down))
    ffn2_w, ffn2_m, ffn2_v = (ffn_views(w_ffn2_gate, w_ffn2_up, w_ffn2_down), ffn_views(m_w_ffn2_gate, m_w_ffn2_up, m_w_ffn2_down),
                              ffn_views(v_w_ffn2_gate, v_w_ffn2_up, v_w_ffn2_down))
    in_pad = -(-in4 // 32) * 32
    in_sh = slot(w_in[0].T[None])
    gt_sh = slot(w_gate[0].T[None])
    up_sh = slot(jnp.concatenate([w_up_a[0], w_up_b[0]], axis=0)[None])
    wo_sh = slot(w_out)
    f1_send, f1_recv, f1_bufs, f1_token = gather_ici_start("gather_ffn1_start", [slot(jnp.stack(ffn1_w))], norm_ffn1)
    mx_send, mx_recv, mx_bufs, mx_token = gather_ici_start("gather_mixer_start", [in_sh, gt_sh, up_sh, wo_sh], f1_token)
    f2_send, f2_recv, f2_bufs, f2_token = gather_ici_start("gather_ffn2_start", [slot(jnp.stack(ffn2_w))], mx_token)

    normed1 = rms_fwd("ffn1_rms", x2d, norm_ffn1, after=(f2_token,))
    (w3_1,) = gather_forward("gather_ffn1_forward", gather_ici_wait("gather_ffn1_wait", f1_send, f1_recv, f1_bufs, normed1[0]))
    x1, saved1 = ffn_forward("ffn1", x2d, norm_ffn1, w3_1, normed=normed1)
    w_in_g, w_gate_g, wup, wo = gather_forward(
        "gather_mixer_forward", gather_ici_wait("gather_mixer_wait", mx_send, mx_recv, mx_bufs, x1))
    wup = wup[:, 0]
    w_in_t = w_in_g.reshape(in_cols, d)
    w_gate_t = w_gate_g.reshape(2 * d, d)
    w_f_t = jnp.pad(w_in_t[QKV_COLS:], ((0, QB - n_forget), (0, 0)))
    wo_full = wo.reshape(d, d)
    b_forget_row = jnp.pad(b_forget, ((0, 0), (0, QB - n_forget)))

    h2, rstd2 = rms_fwd("mix_rms", x1, norm_mix)
    qkv = proj("mix_proj_qkv", h2, w_in_t, jnp.zeros((1, QKV_COLS), F32), 0, QKV_COLS, WIDTH, BF16, scaled_tiles=(0, 3))
    pc = proj("mix_proj_gates", h2, w_gate_t, b_gate, 0, 2 * d, WIDTH, F32)
    f_logit = proj("mix_proj_forget", h2, w_f_t, jnp.zeros((1, QB), F32), 0, QB, QB, F32)
    c_cum = fox_prep("fox_prep", f_logit, b_forget_row)
    c_heads = c_cum[:, :N_HEADS].T
    c_col = c_heads[:, :, None]
    c_row = c_heads.reshape(N_HEADS, t // CS, 1, CS)
    ya = sb_pair_fwd("sb_fwd", qkv)
    yb, lse = fox_pair_fwd("fox_fwd", qkv, c_col, c_row)
    ua, ub, mixed = mix_fwd("mix_fwd", ya, yb, wup, pc)
    x2 = mm_residual("mix_out", mixed[None], wo_full[None], pl.BlockSpec((1, d, d), lambda m: (0, 0, 0)), x1, 1.0)
    (w3_2,) = gather_forward("gather_ffn2_forward", gather_ici_wait("gather_ffn2_wait", f2_send, f2_recv, f2_bufs, x2))
    x3, saved2 = ffn_forward("ffn2", x2, norm_ffn2, w3_2)
    dx3, gn_final, loss_part = loss_head("loss_head", x3, norm_final[None], target)

    g_w3_2, dab2 = ffn_backward_weights("ffn2", dx3, saved2, w3_2)
    rs_ffn2, token = reduce_scatter_1("rs_ffn2", [g_w3_2])
    dx2, gn_ffn2 = ffn_backward_input("ffn2", dx3, saved2, dab2, norm_ffn2, w3_2, after=(token,))
    rs_ffn2, rs_ffn2_token = reduce_scatter_2("rs_ffn2", rs_ffn2, c_arr, dx2)

    dua, dub, dpa, dpb, gba, gbb = mix_bwd("mix_bwd", dx2, wo_full, pc, ua, ub, after=(rs_ffn2_token,))
    g_bgate = jnp.concatenate([gba, gbb], axis=1)
    g_wo = mm_plain("wgrad_out", mixed, dx2, TN, BF16, tk_target=1024)
    dya = up_bwd("dya", dua, wup, 0)
    dyb = up_bwd("dyb", dub, wup, 1)
    g_up = up_wgrad("wgrad_up", ya, yb, dua, dub)
    dqkv = sb_pair_bwd("sb_bwd", qkv, dya, lax.empty((t, QKV_COLS), BF16))
    dqkv, dcc, dcr = fox_pair_bwd("fox_bwd", qkv, yb, dyb, lse, c_col, c_row, dqkv)
    dc = dcc[:, :, :2].transpose(1, 0, 2).reshape(t, N_HEADS) + dcr.reshape(N_HEADS, t).T
    df, g_bf = fox_gate_bwd("fox_gate_bwd", jnp.pad(dc, ((0, 0), (0, QB - N_HEADS))), f_logit, b_forget_row)
    df = df.astype(BF16)
    g_qkv_t = wgrad_cat("wgrad_qkv", h2, dqkv)
    g_f_t = wgrad_cat("wgrad_forget", h2, df)
    g_gate_t = jnp.stack([wgrad_cat("wgrad_gate_a", h2, dpa), wgrad_cat("wgrad_gate_b", h2, dpb)])
    g_in_t = jnp.concatenate([g_qkv_t, g_f_t[:n_forget]], axis=0).reshape(N_CHIPS, in4, d)
    rs_mixer, token = reduce_scatter_1(
        "rs_mixer", [jnp.pad(g_in_t, ((0, 0), (0, in_pad - in4), (0, 0)))[:, None], g_gate_t.reshape(N_CHIPS, 1, gate4, d),
                     g_up[:, None], g_wo.reshape(N_CHIPS, 1, d // N_CHIPS, d)])
    dx1, gn_mix = mixer_dh("mix_dh", [(dqkv, w_in_t, 0), (dpa, w_gate_t, 0), (dpb, w_gate_t, 1), (df, w_f_t, 0)],
                           x1, rstd2, norm_mix, dx2, after=(token,))
    rs_mixer, rs_mixer_token = reduce_scatter_2("rs_mixer", rs_mixer, c_arr, dx1)
    rs_ffn2, rs_ffn2_token = reduce_scatter_3("rs_ffn2", rs_ffn2, c_arr, dx1)

    big_out = {}

    def adamw_ffn(tag, s_w3, ws, ms, vs, after):
        deltas, new_ms, new_vs = adamw_stacked(f"adamw_{tag}", ws, s_w3, ms, vs, after)
        for which, part in ((GATE, "gate"), (UP, "up"), (DOWN, "down")):
            back = (lambda a: a[None]) if which == DOWN else (lambda a: a.T[None])
            big_out[f"w_{tag}_{part}"] = tuple(back(a) for a in (s_w3[which], deltas[which], new_ms[which], new_vs[which]))
        return deltas[DOWN]

    g_w3_1, dab1 = ffn_backward_weights("ffn1", dx1, saved1, w3_1, after=(rs_mixer_token, rs_ffn2_token))
    rs_ffn1, token = reduce_scatter_1("rs_ffn1", [g_w3_1])
    rs_mixer, rs_mixer_token = reduce_scatter_3("rs_mixer", rs_mixer, c_arr, g_w3_1)
    (s_w3_2,) = reduce_scatter_4("rs_ffn2", rs_ffn2, g_w3_1)
    last = adamw_ffn("ffn2", s_w3_2, ffn2_w, ffn2_m, ffn2_v, (token, rs_mixer_token))
    rs_ffn1, rs_ffn1_token = reduce_scatter_2("rs_ffn1", rs_ffn1, c_arr, last)
    dx0, gn_ffn1 = ffn_backward_input("ffn1", dx1, saved1, dab1, norm_ffn1, w3_1, after=(rs_ffn1_token,))

    def pack_small(n1, nm, n2, nf, bg, bf, last):
        return jnp.concatenate([n1, nm, n2, nf, bg.reshape(2, d), jnp.pad(bf, ((0, 0), (0, d - n_forget))), last], axis=0)

    zero_row = jnp.zeros((1, d), F32)
    g_small = pack_small(gn_ffn1, gn_mix, gn_ffn2, gn_final, g_bgate, g_bf[:, :n_forget], jnp.pad(loss_part, ((0, 0), (0, d - 1))))
    w_small = pack_small(norm_ffn1, norm_mix, norm_ffn2, norm_final[None], b_gate, b_forget, zero_row)
    m_small = pack_small(m_norm_ffn1, m_norm_mix, m_norm_ffn2, m_norm_final[None], m_b_gate, m_b_forget, zero_row)
    v_small = pack_small(v_norm_ffn1, v_norm_mix, v_norm_ffn2, v_norm_final[None], v_b_gate, v_b_forget, zero_row)

    s_in, s_gt, s_up, s_wo = reduce_scatter_4("rs_mixer", rs_mixer, dx0)
    grads = {"w_gate": s_gt[0].T, "w_up_a": s_up[0, :WIDTH], "w_up_b": s_up[0, WIDTH:], "w_out": s_wo[0]}
    weights = {"w_gate": (w_gate, m_w_gate, v_w_gate), "w_up_a": (w_up_a, m_w_up_a, v_w_up_a),
               "w_up_b": (w_up_b, m_w_up_b, v_w_up_b), "w_out": (w_out, m_w_out, v_w_out)}
    for wname, (w, m, v) in weights.items():
        g = grads[wname]
        delta, new_m, new_v = adamw(f"adamw_{wname}", w[0], g, m[0], v[0])
        big_out[wname] = (g[None], delta[None], new_m[None], new_v[None])
    rows_of = lambda a: jnp.transpose(a, (2, 0, 1))
    g_in_rows = s_in[0, :in4][:, None, :]
    in_rows = adamw_rows("adamw_w_in", rows_of(w_in), g_in_rows, rows_of(m_w_in), rows_of(v_w_in))
    big_out["w_in"] = tuple(jnp.transpose(a, (1, 2, 0)) for a in (g_in_rows, *in_rows))

    rs_ffn1, token = reduce_scatter_3("rs_ffn1", rs_ffn1, c_arr, in_rows[0])
    smalls = small_allreduce_adamw("small_allreduce_adamw", g_small, w_small, m_small, v_small, after=(token,))
    (s_w3_1,) = reduce_scatter_4("rs_ffn1", rs_ffn1, smalls[0])
    adamw_ffn("ffn1", s_w3_1, ffn1_w, ffn1_m, ffn1_v, ())

    def unpack_small(p):
        return {"norm_ffn1": p[0:1], "norm_mix": p[1:2], "norm_ffn2": p[2:3], "norm_final": p[3], "b_gate": p[4:6].reshape(1, 2 * d),
                "b_forget": p[6:7, :n_forget]}

    loss = smalls[0][7, 0]
    small_out = [unpack_small(p) for p in smalls]

    order = ["norm_ffn1", "w_ffn1_gate", "w_ffn1_up", "w_ffn1_down", "norm_mix", "w_in", "b_forget", "w_gate", "b_gate",
             "w_up_a", "w_up_b", "w_out", "norm_ffn2", "w_ffn2_gate", "w_ffn2_up", "w_ffn2_down", "norm_final"]
    outs = [loss, dx0[None]]
    for kind in range(4):
        for wname in order:
            outs.append(big_out[wname][kind] if wname in big_out else small_out[kind][wname])
    return tuple(outs)
```

```python
import functools

import jax
import jax.numpy as jnp
from jax import lax
from jax.experimental import pallas as pl
from jax.experimental.pallas import tpu as pltpu

F32 = jnp.float32
BF16 = jnp.bfloat16

HEAD_DIM = 64
N_HEADS = 8
WIDTH = N_HEADS * HEAD_DIM
QKV_COLS = 6 * WIDTH
RMS_EPS = 1e-6
ATTN_SCALE = HEAD_DIM ** -0.5
N_CHIPS = 4
QB = 128
BQ = 2048
CS = 256
N_SUB = BQ // CS
NEG_BIG = -1e30

ADAM_LR = 0.001
ADAM_B1 = 0.9
ADAM_B2 = 0.999
ADAM_EPS = 1e-08
ADAM_WD = 0.01
ADAM_STEP = 10

VMEM_LIMIT_BYTES = 48 * 1024 * 1024
MESH = pl.DeviceIdType.MESH

NN = ((1,), (0,))
NT = ((1,), (1,))
TN = ((0,), (0,))


def _params(semantics):
    return pltpu.CompilerParams(dimension_semantics=semantics, vmem_limit_bytes=VMEM_LIMIT_BYTES)


def _dot(a, b, contract):
    return lax.dot_general(a.astype(BF16), b.astype(BF16), (contract, ((), ())), preferred_element_type=F32)


def _sigmoid(x):
    return 1.0 / (1.0 + jnp.exp(-x))


def _log1pexp_neg_abs(z):
    return jnp.log(1.0 + jnp.exp(-jnp.abs(z)))


def _split3(x):
    hi = x.astype(BF16)
    r1 = x - hi.astype(F32)
    mid = r1.astype(BF16)
    lo = (r1 - mid.astype(F32)).astype(BF16)
    return hi, mid, lo


def _dot_exact_lhs01(m01, x):
    hi, mid, lo = _split3(x)
    d = lambda p: lax.dot_general(m01, p, (NN, ((), ())), preferred_element_type=F32)
    return d(hi) + d(mid) + d(lo)


def _iota2(shape, dim):
    return lax.broadcasted_iota(jnp.int32, shape, dim)


def _mm(name, pairs, contract, grid, pair_specs, out_shape, out_specs, acc_shape, nk, epilogue,
        extras=(), extra_specs=(), semantics=None):
    n_pairs = len(pairs)
    n_extra = len(extras)
    n_out = len(out_shape)

    def body(*refs):
        ab = refs[:2 * n_pairs]
        ex = refs[2 * n_pairs:2 * n_pairs + n_extra]
        outs = refs[2 * n_pairs + n_extra:2 * n_pairs + n_extra + n_out]
        ids = [pl.program_id(i) for i in range(len(grid))]
        k = ids[-1]
        part = _dot(ab[0][...], ab[1][...], contract)
        for p in range(1, n_pairs):
            part += _dot(ab[2 * p][...], ab[2 * p + 1][...], contract)
        if nk == 1:
            epilogue(part, ex, outs, ids)
            return
        acc = refs[-1]

        @pl.when(k == 0)
        def _():
            acc[...] = part

        @pl.when(k != 0)
        def _():
            acc[...] += part

        @pl.when(k == nk - 1)
        def _():
            epilogue(acc[...], ex, outs, ids)

    operands = [t for pair in pairs for t in pair] + list(extras)
    in_specs = [s for pair in pair_specs for s in pair] + list(extra_specs)
    if semantics is None:
        semantics = ("parallel",) * (len(grid) - 1) + ("arbitrary",)
    return pl.pallas_call(
        body, name=name, grid=grid, in_specs=in_specs, out_specs=list(out_specs), out_shape=list(out_shape),
        scratch_shapes=[] if nk == 1 else [pltpu.VMEM(acc_shape, F32)], compiler_params=_params(semantics),
    )(*operands)


def _ordered_after(body, n_in, n_after):
    def wrapped(*refs):
        return body(*refs[:n_in], *refs[n_in + n_after:])
    return wrapped


def _row_tile(rows, target):
    t = min(rows, target)
    while rows % t:
        t //= 2
    return t


def rms_fwd(name, x, g, after=()):
    t, d = x.shape
    tr = _row_tile(t, 256)

    def body(x_ref, g_ref, h_ref, r_ref):
        xv = x_ref[...]
        r = lax.rsqrt(jnp.mean(xv * xv, axis=-1, keepdims=True) + RMS_EPS)
        h_ref[...] = (xv * r * g_ref[...]).astype(BF16)
        r_ref[...] = r

    return pl.pallas_call(
        _ordered_after(body, 2, len(after)), name=name, grid=(t // tr,),
        in_specs=[pl.BlockSpec((tr, d), lambda i: (i, 0)), pl.BlockSpec((1, d), lambda i: (0, 0))] + [_ANY] * len(after),
        out_specs=[pl.BlockSpec((tr, d), lambda i: (i, 0)), pl.BlockSpec((tr, 1), lambda i: (i, 0))],
        out_shape=[jax.ShapeDtypeStruct((t, d), BF16), jax.ShapeDtypeStruct((t, 1), F32)],
        compiler_params=_params(("parallel",)),
    )(x, g, *after)


GATE, UP, DOWN = 0, 1, 2


def _ffn_w_spec(which, f4, d, index_of_j):
    return pl.BlockSpec((None, None, f4, d), lambda *ids: (index_of_j(*ids), which, 0, 0))


def ffn_up(name, h, w3):
    t, d = h.shape
    ns, _, f4, _ = w3.shape
    tm = _row_tile(t, 512)

    def body(h_ref, wg_ref, wu_ref, a_ref, b_ref, s_ref):
        hv = h_ref[...]
        a = _dot(hv, wg_ref[...], NT)
        b = _dot(hv, wu_ref[...], NT)
        a_ref[...] = a.astype(BF16)
        b_ref[...] = b.astype(BF16)
        s_ref[...] = (a * _sigmoid(a) * b).astype(BF16)

    act_spec = pl.BlockSpec((None, tm, f4), lambda j, m: (j, m, 0))
    return pl.pallas_call(
        body, name=name, grid=(ns, t // tm),
        in_specs=[pl.BlockSpec((tm, d), lambda j, m: (m, 0)),
                  _ffn_w_spec(GATE, f4, d, lambda j, m: j), _ffn_w_spec(UP, f4, d, lambda j, m: j)],
        out_specs=[act_spec, act_spec, act_spec],
        out_shape=[jax.ShapeDtypeStruct((ns, t, f4), BF16)] * 3,
        compiler_params=_params(("parallel", "parallel")),
    )(h, w3, w3)


def mm_residual(name, s, w, w_spec, x, scale):
    nj, t, kdim = s.shape
    n = x.shape[1]
    tm = _row_tile(t, 512)

    def body(s_ref, w_ref, x_ref, o_ref):
        acc = _dot(s_ref[0], w_ref[0], NN)
        for j in range(1, nj):
            acc += _dot(s_ref[j], w_ref[j], NN)
        o_ref[...] = x_ref[...] + scale * acc

    row = pl.BlockSpec((tm, n), lambda m: (m, 0))
    return pl.pallas_call(
        body, name=name, grid=(t // tm,),
        in_specs=[pl.BlockSpec((nj, tm, kdim), lambda m: (0, m, 0)), w_spec, row], out_specs=row,
        out_shape=jax.ShapeDtypeStruct((t, n), F32), compiler_params=_params(("parallel",)),
    )(s, w, x)


def ffn_bwd_act(name, dx, w3, a, b, after=()):
    t, d = dx.shape
    ns, _, f4, _ = w3.shape
    tm = _row_tile(t, 512)

    def body(dx_ref, wd_ref, a_ref, b_ref, da_ref, db_ref):
        ds = _dot(0.5 * dx_ref[...], wd_ref[...], NT)
        av = a_ref[...].astype(F32)
        sig = _sigmoid(av)
        da_ref[...] = (ds * b_ref[...].astype(F32) * (sig * (1.0 + av * (1.0 - sig)))).astype(BF16)
        db_ref[...] = (ds * (av * sig)).astype(BF16)

    act_spec = pl.BlockSpec((None, tm, f4), lambda j, m: (j, m, 0))
    return pl.pallas_call(
        _ordered_after(body, 4, len(after)), name=name, grid=(ns, t // tm),
        in_specs=[pl.BlockSpec((tm, d), lambda j, m: (m, 0)), _ffn_w_spec(DOWN, f4, d, lambda j, m: j), act_spec, act_spec]
        + [_ANY] * len(after),
        out_specs=[act_spec, act_spec],
        out_shape=[jax.ShapeDtypeStruct((ns, t, f4), BF16)] * 2,
        compiler_params=_params(("parallel", "parallel")),
    )(dx, w3, a, b, *after)


def ffn_wgrad(name, h, da, db, s, dx):
    t, d = h.shape
    ns, _, f4 = da.shape
    tk = _row_tile(t, 1024)
    nk = t // tk

    def body(h_ref, da_ref, db_ref, s_ref, dx_ref, o_ref, acc):
        k = pl.program_id(1)

        @pl.when(k == 0)
        def _():
            acc[...] = jnp.zeros_like(acc)

        hv = h_ref[...]
        acc[GATE] += _dot(da_ref[...], hv, TN)
        acc[UP] += _dot(db_ref[...], hv, TN)
        acc[DOWN] += _dot(s_ref[...], 0.5 * dx_ref[...], TN)

        @pl.when(k == nk - 1)
        def _():
            o_ref[...] = acc[...].astype(BF16)

    act_spec = pl.BlockSpec((None, tk, f4), lambda j, k: (j, k, 0))
    row_spec = pl.BlockSpec((tk, d), lambda j, k: (k, 0))
    return pl.pallas_call(
        body, name=name, grid=(ns, nk),
        in_specs=[row_spec, act_spec, act_spec, act_spec, row_spec],
        out_specs=pl.BlockSpec((None, 3, f4, d), lambda j, k: (j, 0, 0, 0)),
        out_shape=jax.ShapeDtypeStruct((ns, 3, f4, d), BF16),
        scratch_shapes=[pltpu.VMEM((3, f4, d), F32)],
        compiler_params=_params(("parallel", "arbitrary")),
    )(h, da, db, s, dx)


def _rms_bwd_tail(dh, x_ref, r_ref, g_ref, dxin_ref, dx_ref, gn_ref, row_tile_index):
    r = r_ref[...]
    xhat = x_ref[...] * r
    dhg = dh * g_ref[...]
    dx_ref[...] = dxin_ref[...] + r * (dhg - xhat * jnp.mean(dhg * xhat, axis=-1, keepdims=True))
    part = jnp.sum(dh * xhat, axis=0, keepdims=True)

    @pl.when(row_tile_index == 0)
    def _():
        gn_ref[...] = part

    @pl.when(row_tile_index != 0)
    def _():
        gn_ref[...] += part


def ffn_dh(name, da, db, w3, x, rstd, g, dx_in, after=()):
    ns, t, f4 = da.shape
    d = x.shape[1]
    tm = _row_tile(t, 256)

    def body(da_ref, db_ref, wg_ref, wu_ref, x_ref, r_ref, g_ref, dxin_ref, dx_ref, gn_ref):
        dh = _dot(da_ref[0], wg_ref[0], NN) + _dot(db_ref[0], wu_ref[0], NN)
        for j in range(1, ns):
            dh += _dot(da_ref[j], wg_ref[j], NN) + _dot(db_ref[j], wu_ref[j], NN)
        _rms_bwd_tail(dh, x_ref, r_ref, g_ref, dxin_ref, dx_ref, gn_ref, pl.program_id(0))

    act = pl.BlockSpec((ns, tm, f4), lambda m: (0, m, 0))
    row = pl.BlockSpec((tm, d), lambda m: (m, 0))
    gain = pl.BlockSpec((1, d), lambda m: (0, 0))
    return pl.pallas_call(
        _ordered_after(body, 8, len(after)), name=name, grid=(t // tm,),
        in_specs=[act, act, pl.BlockSpec((ns, None, f4, d), lambda m: (0, GATE, 0, 0)),
                  pl.BlockSpec((ns, None, f4, d), lambda m: (0, UP, 0, 0)), row, pl.BlockSpec((tm, 1), lambda m: (m, 0)), gain, row]
        + [_ANY] * len(after),
        out_specs=[row, gain], out_shape=[jax.ShapeDtypeStruct((t, d), F32), jax.ShapeDtypeStruct((1, d), F32)],
        compiler_params=_params(("arbitrary",)),
    )(da, db, w3, w3, x, rstd, g, dx_in, *after)


def mixer_dh(name, parts, x, rstd, g, dx_in, after=()):
    t, d = x.shape
    n = len(parts)
    tm = _row_tile(t, 256)

    def body(*refs):
        dh = _dot(refs[0][...], refs[n][...], NN)
        for i in range(1, n):
            dh += _dot(refs[i][...], refs[n + i][...], NN)
        _rms_bwd_tail(dh, *refs[2 * n:2 * n + 6], pl.program_id(0))

    row = pl.BlockSpec((tm, d), lambda m: (m, 0))
    gain = pl.BlockSpec((1, d), lambda m: (0, 0))
    act_specs = [pl.BlockSpec((tm, a.shape[1]), lambda m: (m, 0)) for a, _, _ in parts]
    w_specs = [pl.BlockSpec((a.shape[1], d), functools.partial(lambda m, blk: (blk, 0), blk=blk)) for a, _, blk in parts]
    return pl.pallas_call(
        _ordered_after(body, 2 * n + 4, len(after)), name=name, grid=(t // tm,),
        in_specs=act_specs + w_specs + [row, pl.BlockSpec((tm, 1), lambda m: (m, 0)), gain, row] + [_ANY] * len(after),
        out_specs=[row, gain], out_shape=[jax.ShapeDtypeStruct((t, d), F32), jax.ShapeDtypeStruct((1, d), F32)],
        compiler_params=_params(("arbitrary",)),
    )(*[a for a, _, _ in parts], *[w for _, w, _ in parts], x, rstd, g, dx_in, *after)


def ffn_forward(tag, x, g_norm, w3, normed=None):
    _, _, f4, d = w3.shape
    h, rstd = normed if normed is not None else rms_fwd(f"{tag}_rms", x, g_norm)
    a, b, s = ffn_up(f"{tag}_up", h, w3)
    x_out = mm_residual(f"{tag}_down", s, w3, pl.BlockSpec((w3.shape[0], None, f4, d), lambda m: (0, DOWN, 0, 0)), x, 0.5)
    return x_out, (x, h, rstd, a, b, s)


def ffn_backward_weights(tag, dx, saved, w3, after=()):
    x, h, rstd, a, b, s = saved
    da, db = ffn_bwd_act(f"{tag}_bwd_act", dx, w3, a, b, after)
    return ffn_wgrad(f"{tag}_wgrad", h, da, db, s, dx), (da, db)


def ffn_backward_input(tag, dx, saved, dab, g_norm, w3, after=()):
    x, h, rstd, a, b, s = saved
    return ffn_dh(f"{tag}_dh", dab[0], dab[1], w3, x, rstd, g_norm, dx, after)


def proj(name, h, wcat_t, bias, first_col, n_cols, tn, out_dtype, scaled_tiles=()):
    t, d = h.shape
    tm = _row_tile(t, 512)
    off = first_col // tn

    def epilogue(acc, ex, outs, ids):
        val = acc + ex[0][...]
        if scaled_tiles:
            hit = functools.reduce(jnp.logical_or, [ids[0] == s for s in scaled_tiles])
            val = val * jnp.where(hit, ATTN_SCALE, 1.0)
        outs[0][...] = val.astype(out_dtype)

    return _mm(
        name, [(h, wcat_t)], NT, (n_cols // tn, t // tm, 1),
        [(pl.BlockSpec((tm, d), lambda j, m, k: (m, 0)), pl.BlockSpec((tn, d), lambda j, m, k: (off + j, 0)))],
        [jax.ShapeDtypeStruct((t, n_cols), out_dtype)], [pl.BlockSpec((tm, tn), lambda j, m, k: (m, j))], (tm, tn), 1, epilogue,
        extras=[bias], extra_specs=[pl.BlockSpec((1, tn), lambda j, m, k: (0, off + j))],
    )[0]


def mix_fwd(name, ya, yb, wup, pc):
    t, w = ya.shape
    ns, _, tn = wup.shape
    d = ns * tn
    tm = _row_tile(t, 512)

    def body(ya_ref, yb_ref, wa_ref, wb_ref, pa_ref, pb_ref, ua_ref, ub_ref, mx_ref):
        ua = _dot(ya_ref[...], wa_ref[...], NN)
        ub = _dot(yb_ref[...], wb_ref[...], NN)
        ua_ref[...] = ua
        ub_ref[...] = ub
        mx_ref[...] = (_sigmoid(pa_ref[...]) * ua + _sigmoid(pb_ref[...]) * ub).astype(BF16)

    y_spec = pl.BlockSpec((tm, w), lambda m, n: (m, 0))
    o_spec = pl.BlockSpec((tm, tn), lambda m, n: (m, n))
    return pl.pallas_call(
        body, name=name, grid=(t // tm, ns),
        in_specs=[y_spec, y_spec, pl.BlockSpec((None, w, tn), lambda m, n: (n, 0, 0)), pl.BlockSpec((None, w, tn), lambda m, n: (n, 1, 0)),
                  o_spec, pl.BlockSpec((tm, tn), lambda m, n: (m, ns + n))],
        out_specs=[o_spec, o_spec, o_spec],
        out_shape=[jax.ShapeDtypeStruct((t, d), F32), jax.ShapeDtypeStruct((t, d), F32), jax.ShapeDtypeStruct((t, d), BF16)],
        compiler_params=_params(("parallel", "parallel")),
    )(ya, yb, wup, wup, pc, pc)


def up_bwd(name, du, wup, branch):
    t, d = du.shape
    ns, w2, tn = wup.shape
    w = w2 // 2
    tm = _row_tile(t, 512)

    def body(du_ref, w_ref, o_ref):
        acc = _dot(du_ref[:, 0:tn], w_ref[0], NT)
        for j in range(1, ns):
            acc += _dot(du_ref[:, j * tn:(j + 1) * tn], w_ref[j], NT)
        o_ref[...] = acc.astype(BF16)

    return pl.pallas_call(
        body, name=name, grid=(t // tm,),
        in_specs=[pl.BlockSpec((tm, d), lambda m: (m, 0)), pl.BlockSpec((ns, w, tn), lambda m: (0, branch, 0))],
        out_specs=pl.BlockSpec((tm, w), lambda m: (m, 0)), out_shape=jax.ShapeDtypeStruct((t, w), BF16),
        compiler_params=_params(("parallel",)),
    )(du, wup)


def up_wgrad(name, ya, yb, dua, dub):
    t, w = ya.shape
    d = dua.shape[1]
    tn = d // N_CHIPS

    def body(ya_ref, yb_ref, dua_ref, dub_ref, o_ref):
        o_ref[0:w, :] = _dot(ya_ref[...], dua_ref[...], TN).astype(BF16)
        o_ref[w:2 * w, :] = _dot(yb_ref[...], dub_ref[...], TN).astype(BF16)

    y_spec = pl.BlockSpec((t, w), lambda j: (0, 0))
    du_spec = pl.BlockSpec((t, tn), lambda j: (0, j))
    return pl.pallas_call(
        body, name=name, grid=(N_CHIPS,), in_specs=[y_spec, y_spec, du_spec, du_spec],
        out_specs=pl.BlockSpec((None, 2 * w, tn), lambda j: (j, 0, 0)),
        out_shape=jax.ShapeDtypeStruct((N_CHIPS, 2 * w, tn), BF16), compiler_params=_params(("parallel",)),
    )(ya, yb, dua, dub)


def mix_bwd(name, dx, wo, pc, ua, ub, after=()):
    t, d = dx.shape
    tm = _row_tile(t, 512)
    tn = 512
    off_a = 0
    off_b = d // tn

    def body(dx_ref, wo_ref, pa_ref, pb_ref, ua_ref, ub_ref, dua_ref, dub_ref, dpa_ref, dpb_ref, ba_ref, bb_ref):
        dm = _dot(dx_ref[...], wo_ref[...], NT)
        ga = _sigmoid(pa_ref[...])
        gb = _sigmoid(pb_ref[...])
        dua_ref[...] = (dm * ga).astype(BF16)
        dub_ref[...] = (dm * gb).astype(BF16)
        dpa = dm * ua_ref[...] * ga * (1.0 - ga)
        dpb = dm * ub_ref[...] * gb * (1.0 - gb)
        dpa_ref[...] = dpa.astype(BF16)
        dpb_ref[...] = dpb.astype(BF16)
        sa = jnp.sum(dpa, axis=0, keepdims=True)
        sb = jnp.sum(dpb, axis=0, keepdims=True)

        @pl.when(pl.program_id(1) == 0)
        def _():
            ba_ref[...] = sa
            bb_ref[...] = sb

        @pl.when(pl.program_id(1) != 0)
        def _():
            ba_ref[...] += sa
            bb_ref[...] += sb

    tile = pl.BlockSpec((tm, tn), lambda n, m: (m, n))
    bias = pl.BlockSpec((1, tn), lambda n, m: (0, n))
    return pl.pallas_call(
        _ordered_after(body, 6, len(after)), name=name, grid=(d // tn, t // tm),
        in_specs=[pl.BlockSpec((tm, d), lambda n, m: (m, 0)), pl.BlockSpec((tn, d), lambda n, m: (n, 0)),
                  pl.BlockSpec((tm, tn), lambda n, m: (m, off_a + n)), pl.BlockSpec((tm, tn), lambda n, m: (m, off_b + n)),
                  tile, tile] + [_ANY] * len(after),
        out_specs=[tile, tile, tile, tile, bias, bias],
        out_shape=[jax.ShapeDtypeStruct((t, d), BF16)] * 4 + [jax.ShapeDtypeStruct((1, d), F32)] * 2,
        compiler_params=_params(("parallel", "arbitrary")),
    )(dx, wo, pc, pc, ua, ub, *after)


def mm_plain(name, a, b, contract, out_dtype, tk_target=512):
    if contract == NN:
        m, kdim = a.shape
        n = b.shape[1]
    elif contract == NT:
        m, kdim = a.shape
        n = b.shape[0]
    else:
        kdim, m = a.shape
        n = b.shape[1]
    tm = _row_tile(m, 512)
    tk = _row_tile(kdim, tk_target)
    nk = kdim // tk
    if contract == TN:
        a_spec = pl.BlockSpec((tk, tm), lambda i, k: (k, i))
    else:
        a_spec = pl.BlockSpec((tm, tk), lambda i, k: (i, k))
    if contract == NT:
        b_spec = pl.BlockSpec((n, tk), lambda i, k: (0, k))
    else:
        b_spec = pl.BlockSpec((tk, n), lambda i, k: (k, 0))

    def epilogue(acc, ex, outs, ids):
        outs[0][...] = acc.astype(out_dtype)

    return _mm(name, [(a, b)], contract, (m // tm, nk), [(a_spec, b_spec)],
               [jax.ShapeDtypeStruct((m, n), out_dtype)], [pl.BlockSpec((tm, n), lambda i, k: (i, 0))], (tm, n), nk, epilogue)[0]


def wgrad_cat(name, h, dcat):
    t, d = h.shape
    n = dcat.shape[1]
    tn = next(c for c in (768, 512, 256, 128) if n % c == 0)
    tk = _row_tile(t, 2048)

    def epilogue(acc, ex, outs, ids):
        outs[0][...] = acc.astype(BF16)

    return _mm(
        name, [(dcat, h)], TN, (n // tn, t // tk),
        [(pl.BlockSpec((tk, tn), lambda j, k: (k, j)), pl.BlockSpec((tk, d), lambda j, k: (k, 0)))],
        [jax.ShapeDtypeStruct((n, d), BF16)], [pl.BlockSpec((tn, d), lambda j, k: (j, 0))], (tn, d), t // tk, epilogue,
    )[0]


def fox_prep(name, f, bias):
    t, lanes = f.shape
    nchunk = t // QB

    def body(f_ref, b_ref, c_ref):
        lower = (_iota2((QB, QB), 1) <= _iota2((QB, QB), 0)).astype(BF16)

        def chunk(n, carry):
            rows = pl.ds(pl.multiple_of(n * QB, QB), QB)
            u = f_ref[rows, :] + b_ref[...]
            lf = jnp.minimum(u, 0.0) - _log1pexp_neg_abs(u)
            c = _dot_exact_lhs01(lower, lf) + carry
            c_ref[rows, :] = c
            return c[QB - 1:QB, :]

        lax.fori_loop(0, nchunk, chunk, jnp.zeros((1, lanes), F32))

    return pl.pallas_call(body, name=name, out_shape=jax.ShapeDtypeStruct((t, lanes), F32),
                          compiler_params=pltpu.CompilerParams(vmem_limit_bytes=VMEM_LIMIT_BYTES))(f, bias)


def fox_gate_bwd(name, dc, f, bias):
    t, lanes = dc.shape
    nchunk = t // QB

    def body(dc_ref, f_ref, b_ref, df_ref, gb_ref):
        upper = (_iota2((QB, QB), 1) >= _iota2((QB, QB), 0)).astype(BF16)

        def chunk(n, carry):
            tail, total = carry
            rows = pl.ds(pl.multiple_of((nchunk - 1 - n) * QB, QB), QB)
            dlf = _dot_exact_lhs01(upper, dc_ref[rows, :]) + tail
            u = f_ref[rows, :] + b_ref[...]
            df = dlf * jnp.exp(jnp.minimum(-u, 0.0) - _log1pexp_neg_abs(u))
            df_ref[rows, :] = df
            return dlf[0:1, :], total + jnp.sum(df, axis=0, keepdims=True)

        zero = jnp.zeros((1, lanes), F32)
        _, total = lax.fori_loop(0, nchunk, chunk, (zero, zero))
        gb_ref[...] = total

    return pl.pallas_call(body, name=name,
                          out_shape=[jax.ShapeDtypeStruct((t, lanes), F32), jax.ShapeDtypeStruct((1, lanes), F32)],
                          compiler_params=pltpu.CompilerParams(vmem_limit_bytes=VMEM_LIMIT_BYTES))(dc, f, bias)


def _qrows(i):
    return pl.ds(pl.multiple_of(i * BQ, BQ), BQ)


def _krows(kc):
    return pl.ds(pl.multiple_of(kc * CS, CS), CS)


def _dot_split2_rhs01(x, m01):
    hi = x.astype(BF16)
    lo = (x - hi.astype(F32)).astype(BF16)
    d = lambda p: lax.dot_general(p, m01, (NN, ((), ())), preferred_element_type=F32)
    return d(hi) + d(lo)


def _live_rows(dchunk):
    return 0 if dchunk is None else dchunk * CS


def _diag_mask(dchunk, inclusive):
    shape = (BQ - _live_rows(dchunk), CS)
    return _iota2(shape, 1) <= _iota2(shape, 0) if inclusive else _iota2(shape, 1) < _iota2(shape, 0)


def _tail(x, r0, axis=0):
    return x if r0 == 0 else (x[r0:] if axis == 0 else x[:, r0:])


def _with_tail(old, tail, r0):
    return tail if r0 == 0 else jnp.concatenate([old[:r0], tail], axis=0)


def _walk_chunks(i, step, init, right_to_left):
    order = list(reversed(range(N_SUB))) if right_to_left else list(range(N_SUB))

    def diagonal(state):
        for dchunk in order:
            state = step(i * N_SUB + dchunk, state, dchunk)
        return state

    def group(n, state):
        base = ((i - 1 - n) if right_to_left else n) * N_SUB
        for dchunk in order:
            state = step(base + dchunk, state, None)
        return state

    if right_to_left:
        return lax.fori_loop(0, i, group, diagonal(init))
    return diagonal(lax.fori_loop(0, i, group, init))


PAIR = 2 * HEAD_DIM
N_PAIRS = N_HEADS // 2


def _pair_spec(t, first_block):
    return pl.BlockSpec((t, PAIR), lambda p, *_: (0, first_block + p))


def _head_lanes(shape):
    lane = _iota2(shape, len(shape) - 1)
    return [lane < HEAD_DIM, lane >= HEAD_DIM]


def _only_head(x, lanes_of_head):
    return jnp.where(lanes_of_head, x, jnp.zeros_like(x))


LOG2_E = 1.4426950408889634


def _sb_chunk_weights(q_h, k, later, carry, dchunk):
    z = _dot(q_h, k, NT) * LOG2_E
    lnb = -jnp.maximum(z, 0.0) - jnp.log2(1.0 + jnp.exp2(-jnp.abs(z)))
    lsz = lnb + z
    if dchunk is not None:
        lnb = jnp.where(_diag_mask(dchunk, False), lnb, 0.0)
    w = jnp.exp2(lsz + _dot_split2_rhs01(lnb, later) + carry)
    if dchunk is not None:
        w = jnp.where(_diag_mask(dchunk, False), w, 0.0)
    return w, lsz, lnb


def sb_pair_fwd(name, qkv):
    t = qkv.shape[0]

    def body(q_ref, k_ref, v_ref, o_ref):
        later = (_iota2((CS, CS), 0) > _iota2((CS, CS), 1)).astype(BF16)
        lanes = _head_lanes((BQ, PAIR))

        def qblock(i, _):
            q = q_ref[_qrows(i), :]
            q_heads = [_only_head(q, lanes[h]) for h in range(2)]

            def step(kc, state, dchunk):
                k = k_ref[_krows(kc), :]
                v = v_ref[_krows(kc), :]
                out = []
                r0 = _live_rows(dchunk)
                for h in range(2):
                    carry, acc = state[h]
                    w, _, lnb = _sb_chunk_weights(_tail(q_heads[h], r0), k, later, _tail(carry, r0), dchunk)
                    out.append((_with_tail(carry, _tail(carry, r0) + jnp.sum(lnb, axis=1, keepdims=True), r0),
                                _with_tail(acc, _tail(acc, r0) + _dot(w, v, NN), r0)))
                return tuple(out)

            zero = (jnp.zeros((BQ, 1), F32), jnp.zeros((BQ, PAIR), F32))
            (_, acc0), (_, acc1) = _walk_chunks(i, step, (zero, zero), True)
            o_ref[_qrows(i), :] = jnp.where(lanes[0], acc0, acc1).astype(BF16)
            return 0

        lax.fori_loop(0, t // BQ, qblock, 0)

    return pl.pallas_call(
        body, name=name, grid=(N_PAIRS,),
        in_specs=[_pair_spec(t, 0), _pair_spec(t, N_PAIRS), _pair_spec(t, 2 * N_PAIRS)],
        out_specs=_pair_spec(t, 0), out_shape=jax.ShapeDtypeStruct((t, WIDTH), BF16),
        compiler_params=_params(("parallel",)),
    )(qkv, qkv, qkv)


def _emit_dqkv(res, o_ref):
    o_ref[...] = res[pl.program_id(1)]


def _flush_transposed(acc, res, which):
    for kc in range(acc.shape[0]):
        res[which, kc * CS:(kc + 1) * CS, :] = acc[kc].T.astype(BF16)


def sb_pair_bwd(name, qkv, dy, dqkv):
    t = qkv.shape[0]
    nc = t // CS

    def body(q_ref, k_ref, v_ref, do_ref, _, o_ref, g_s, b_s, dkt_acc, dvt_acc, res):
        @pl.when(pl.program_id(1) == 0)
        def _():
            later = (_iota2((CS, CS), 0) > _iota2((CS, CS), 1)).astype(BF16)
            earlier = (_iota2((CS, CS), 0) < _iota2((CS, CS), 1)).astype(BF16)
            lanes = _head_lanes((BQ, PAIR))
            dkt_acc[...] = jnp.zeros_like(dkt_acc)
            dvt_acc[...] = jnp.zeros_like(dvt_acc)

            def qblock(i, _):
                q = q_ref[_qrows(i), :]
                do = do_ref[_qrows(i), :]
                zero = jnp.zeros((BQ, 1), F32)
                dqs = []
                for h in range(2):
                    q_h = _only_head(q, lanes[h])
                    do_h = _only_head(do, lanes[h])
                    qt_h = q_h.astype(F32).T.astype(BF16)
                    dot_h = do_h.astype(F32).T.astype(BF16)

                    def step1(kc, carry, dchunk, q_h=q_h, do_h=do_h, dot_h=dot_h):
                        k = k_ref[_krows(kc), :]
                        v = v_ref[_krows(kc), :]
                        r0 = _live_rows(dchunk)
                        live = _tail(carry, r0)
                        w, lsz, lnb = _sb_chunk_weights(_tail(q_h, r0), k, later, live, dchunk)
                        g_s[kc, r0:, :] = (w * _dot(_tail(do_h, r0), v, NT)).astype(BF16)
                        b_s[kc, r0:, :] = jnp.exp2(lsz).astype(BF16)
                        dvt_acc[kc] += _dot(_tail(dot_h, r0, axis=1), w, NN)
                        return _with_tail(carry, live + jnp.sum(lnb, axis=1, keepdims=True), r0)

                    _walk_chunks(i, step1, zero, True)

                    def step2(kc, state, dchunk, qt_h=qt_h):
                        k = k_ref[_krows(kc), :]
                        r0 = _live_rows(dchunk)
                        before, dq = state
                        g16 = g_s[kc, r0:, :]
                        g = g16.astype(F32)
                        beta = b_s[kc, r0:, :].astype(F32)
                        prefix = lax.dot_general(g16, earlier, (NN, ((), ())), preferred_element_type=F32) + _tail(before, r0)
                        dz = g * (1.0 - beta) - beta * prefix
                        if dchunk is not None:
                            dz = jnp.where(_diag_mask(dchunk, False), dz, 0.0)
                        dzb = dz.astype(BF16)
                        dkt_acc[kc] += _dot(_tail(qt_h, r0, axis=1), dzb, NN)
                        return (_with_tail(before, _tail(before, r0) + jnp.sum(g, axis=1, keepdims=True), r0),
                                _with_tail(dq, _tail(dq, r0) + _dot(dzb, k, NN), r0))

                    dqs.append(_walk_chunks(i, step2, (zero, jnp.zeros((BQ, PAIR), F32)), False)[1])
                res[0, _qrows(i), :] = (jnp.where(lanes[0], dqs[0], dqs[1]) * ATTN_SCALE).astype(BF16)
                return 0

            lax.fori_loop(0, t // BQ, qblock, 0)
            _flush_transposed(dkt_acc, res, 1)
            _flush_transposed(dvt_acc, res, 2)

        _emit_dqkv(res, o_ref)

    return pl.pallas_call(
        body, name=name, grid=(N_PAIRS, 3),
        in_specs=[_pair_spec(t, 0), _pair_spec(t, N_PAIRS), _pair_spec(t, 2 * N_PAIRS), _pair_spec(t, 0), _ANY],
        out_specs=pl.BlockSpec((t, PAIR), lambda p, s: (0, s * N_PAIRS + p)),
        out_shape=jax.ShapeDtypeStruct(dqkv.shape, BF16), input_output_aliases={4: 0},
        scratch_shapes=[pltpu.VMEM((nc, BQ, CS), BF16), pltpu.VMEM((nc, BQ, CS), BF16),
                        pltpu.VMEM((nc, PAIR, CS), F32), pltpu.VMEM((nc, PAIR, CS), F32), pltpu.VMEM((3, t, PAIR), BF16)],
        compiler_params=_params(("parallel", "arbitrary")),
    )(qkv, qkv, qkv, dy, dqkv)


def _gates_col_spec(t):
    return pl.BlockSpec((2, t, 1), lambda p, *_: (p, 0, 0))


def _gates_row_spec(nc):
    return pl.BlockSpec((2, nc, 1, CS), lambda p, *_: (p, 0, 0, 0))


def fox_pair_fwd(name, qkv, c_col, c_row):
    t = qkv.shape[0]

    def body(q_ref, k_ref, v_ref, cc_ref, cr_ref, o_ref, lse_ref):
        lanes = _head_lanes((BQ, PAIR))

        def qblock(i, _):
            q = q_ref[_qrows(i), :]
            q_heads = [_only_head(q, lanes[h]) for h in range(2)]
            ct = [cc_ref[h, _qrows(i), :] for h in range(2)]

            def step(kc, state, dchunk):
                k = k_ref[_krows(kc), :]
                v = v_ref[_krows(kc), :]
                out = []
                r0 = _live_rows(dchunk)
                for h in range(2):
                    m, l, acc = (_tail(a, r0) for a in state[h])
                    s = _dot(_tail(q_heads[h], r0), k, NT) + _tail(ct[h], r0) - cr_ref[h, kc]
                    if dchunk is not None:
                        s = jnp.where(_diag_mask(dchunk, True), s, NEG_BIG)
                    m_new = jnp.maximum(m, jnp.max(s, axis=1, keepdims=True))
                    alpha = jnp.exp(m - m_new)
                    p = jnp.exp(s - m_new)
                    if dchunk is not None:
                        p = jnp.where(_diag_mask(dchunk, True), p, 0.0)
                    new = (m_new, l * alpha + jnp.sum(p, axis=1, keepdims=True), acc * alpha + _dot(p, v, NN))
                    out.append(tuple(_with_tail(old, tail, r0) for old, tail in zip(state[h], new)))
                return tuple(out)

            init = (jnp.full((BQ, 1), NEG_BIG, F32), jnp.zeros((BQ, 1), F32), jnp.zeros((BQ, PAIR), F32))
            (m0, l0, acc0), (m1, l1, acc1) = _walk_chunks(i, step, (init, init), False)
            o_ref[_qrows(i), :] = jnp.where(lanes[0], acc0 / l0, acc1 / l1).astype(BF16)
            lse_ref[0, _qrows(i), :] = m0 + jnp.log(l0)
            lse_ref[1, _qrows(i), :] = m1 + jnp.log(l1)
            return 0

        lax.fori_loop(0, t // BQ, qblock, 0)

    return pl.pallas_call(
        body, name=name, grid=(N_PAIRS,),
        in_specs=[_pair_spec(t, 3 * N_PAIRS), _pair_spec(t, 4 * N_PAIRS), _pair_spec(t, 5 * N_PAIRS),
                  _gates_col_spec(t), _gates_row_spec(t // CS)],
        out_specs=[_pair_spec(t, 0), _gates_col_spec(t)],
        out_shape=[jax.ShapeDtypeStruct((t, WIDTH), BF16), jax.ShapeDtypeStruct((N_HEADS, t, 1), F32)],
        compiler_params=_params(("parallel",)),
    )(qkv, qkv, qkv, c_col, c_row)


def fox_pair_bwd(name, qkv, y, dy, lse, c_col, c_row, dqkv):
    t = qkv.shape[0]
    nc = t // CS

    def body(q_ref, k_ref, v_ref, o_in_ref, do_ref, lse_ref, cc_ref, cr_ref, _, o_ref, dcc_ref, dcr_ref,
             dkt_acc, dvt_acc, dcr_acc, res):
        @pl.when(pl.program_id(1) == 0)
        def _():
            lanes = _head_lanes((BQ, PAIR))
            dkt_acc[...] = jnp.zeros_like(dkt_acc)
            dvt_acc[...] = jnp.zeros_like(dvt_acc)
            dcr_acc[...] = jnp.zeros_like(dcr_acc)

            def qblock(i, _):
                q = q_ref[_qrows(i), :]
                do = do_ref[_qrows(i), :]
                q_heads = [_only_head(q, lanes[h]) for h in range(2)]
                do_heads = [_only_head(do, lanes[h]) for h in range(2)]
                qt_heads = [qh.astype(F32).T.astype(BF16) for qh in q_heads]
                dot_heads = [dh.astype(F32).T.astype(BF16) for dh in do_heads]
                prod = do.astype(F32) * o_in_ref[_qrows(i), :].astype(F32)
                delta = [jnp.sum(_only_head(prod, lanes[h]), axis=1, keepdims=True) for h in range(2)]
                ct = [cc_ref[h, _qrows(i), :] for h in range(2)]
                lse_i = [lse_ref[h, _qrows(i), :] for h in range(2)]

                def step(kc, state, dchunk):
                    k = k_ref[_krows(kc), :]
                    v = v_ref[_krows(kc), :]
                    out = []
                    r0 = _live_rows(dchunk)
                    for h in range(2):
                        dq, dct = state[h]
                        s = _dot(_tail(q_heads[h], r0), k, NT) + _tail(ct[h], r0) - cr_ref[h, kc]
                        p = jnp.exp(s - _tail(lse_i[h], r0))
                        if dchunk is not None:
                            p = jnp.where(_diag_mask(dchunk, True), p, 0.0)
                        ds = p * (_dot(_tail(do_heads[h], r0), v, NT) - _tail(delta[h], r0))
                        dvt_acc[kc] += _dot(_tail(dot_heads[h], r0, axis=1), p, NN)
                        dsb = ds.astype(BF16)
                        dkt_acc[kc] += _dot(_tail(qt_heads[h], r0, axis=1), dsb, NN)
                        dcr_acc[h, kc] -= jnp.sum(ds, axis=0, keepdims=True)
                        out.append((_with_tail(dq, _tail(dq, r0) + _dot(dsb, k, NN), r0),
                                    _with_tail(dct, _tail(dct, r0) + jnp.sum(ds, axis=1, keepdims=True), r0)))
                    return tuple(out)

                zero = (jnp.zeros((BQ, PAIR), F32), jnp.zeros((BQ, 1), F32))
                (dq0, dct0), (dq1, dct1) = _walk_chunks(i, step, (zero, zero), False)
                res[0, _qrows(i), :] = (jnp.where(lanes[0], dq0, dq1) * ATTN_SCALE).astype(BF16)
                lane = _iota2((BQ, PAIR), 1)
                dcc_ref[_qrows(i), :] = jnp.where(lane == 0, dct0, jnp.where(lane == 1, dct1, 0.0))
                return 0

            lax.fori_loop(0, t // BQ, qblock, 0)
            _flush_transposed(dkt_acc, res, 1)
            _flush_transposed(dvt_acc, res, 2)
            dcr_ref[...] = dcr_acc[...]

        _emit_dqkv(res, o_ref)

    return pl.pallas_call(
        body, name=name, grid=(N_PAIRS, 3),
        in_specs=[_pair_spec(t, 3 * N_PAIRS), _pair_spec(t, 4 * N_PAIRS), _pair_spec(t, 5 * N_PAIRS), _pair_spec(t, 0),
                  _pair_spec(t, 0), _gates_col_spec(t), _gates_col_spec(t), _gates_row_spec(nc), _ANY],
        out_specs=[pl.BlockSpec((t, PAIR), lambda p, s: (0, (3 + s) * N_PAIRS + p)),
                   pl.BlockSpec((None, t, PAIR), lambda p, s: (p, 0, 0)), _gates_row_spec(nc)],
        out_shape=[jax.ShapeDtypeStruct(dqkv.shape, BF16), jax.ShapeDtypeStruct((N_PAIRS, t, PAIR), F32),
                   jax.ShapeDtypeStruct((N_HEADS, nc, 1, CS), F32)],
        input_output_aliases={8: 0},
        scratch_shapes=[pltpu.VMEM((nc, PAIR, CS), F32), pltpu.VMEM((nc, PAIR, CS), F32), pltpu.VMEM((2, nc, 1, CS), F32),
                        pltpu.VMEM((3, t, PAIR), BF16)],
        compiler_params=_params(("parallel", "arbitrary")),
    )(qkv, qkv, qkv, y, dy, lse, c_col, c_row, dqkv)


def loss_head(name, x, g, target):
    t, d = x.shape
    tr = _row_tile(t, 256)

    def body(x_ref, g_ref, t_ref, dx_ref, gn_ref, loss_ref):
        xv = x_ref[...]
        r = lax.rsqrt(jnp.mean(xv * xv, axis=-1, keepdims=True) + RMS_EPS)
        xhat = xv * r
        gv = g_ref[...]
        err = xhat * gv - t_ref[...]
        part_loss = 0.5 * jnp.sum(jnp.mean(err * err, axis=-1, keepdims=True), axis=0, keepdims=True)
        dy = err * (1.0 / d)
        dyg = dy * gv
        dx_ref[...] = r * (dyg - xhat * jnp.mean(dyg * xhat, axis=-1, keepdims=True))
        part_g = jnp.sum(dy * xhat, axis=0, keepdims=True)

        @pl.when(pl.program_id(0) == 0)
        def _():
            gn_ref[...] = part_g
            loss_ref[...] = part_loss

        @pl.when(pl.program_id(0) != 0)
        def _():
            gn_ref[...] += part_g
            loss_ref[...] += part_loss

    row = pl.BlockSpec((tr, d), lambda i: (i, 0))
    return pl.pallas_call(
        body, name=name, grid=(t // tr,),
        in_specs=[row, pl.BlockSpec((1, d), lambda i: (0, 0)), row],
        out_specs=[row, pl.BlockSpec((1, d), lambda i: (0, 0)), pl.BlockSpec((1, 1), lambda i: (0, 0))],
        out_shape=[jax.ShapeDtypeStruct((t, d), F32), jax.ShapeDtypeStruct((1, d), F32), jax.ShapeDtypeStruct((1, 1), F32)],
        compiler_params=_params(("arbitrary",)),
    )(x, g, target)


def _place():
    return lax.axis_index("x"), lax.axis_index("y"), lax.axis_index("c")


def _other_chips(x, y):
    return [(1 - x, y), (x, 1 - y), (1 - x, 1 - y)]


def _half(ref, c, rows):
    if rows % 32 == 0:
        return ref.at[:, pl.ds(c * (rows // 2), rows // 2), :]
    cols = ref.shape[2]
    return ref.at[:, :, pl.ds(c * (cols // 2), cols // 2)]


_ANY = pl.BlockSpec(memory_space=pl.ANY)


_HBM = pl.BlockSpec(memory_space=pltpu.HBM)
_SEM = pl.BlockSpec(memory_space=pltpu.SEMAPHORE)
_DATAFLOW = pltpu.SideEffectType.DATAFLOW_SIDE_EFFECTING


def _in_hbm(a):
    return pltpu.with_memory_space_constraint(a, pltpu.HBM)


def _gather_ici_copies(bufs, send_sems, recv_sems, arrivals):
    x, y, c = _place()
    me = 2 * x + y
    copies = []
    for i, buf in enumerate(bufs):
        rows = buf.shape[2]
        for j, (qx, qy) in enumerate(_other_chips(x, y)):
            block = _half(buf.at[2 * qx + qy if arrivals else me], c, rows)
            copies.append(pltpu.make_async_remote_copy(
                src_ref=block, dst_ref=block, send_sem=send_sems.at[3 * i + j], recv_sem=recv_sems.at[3 * i + j],
                device_id=(qx, qy, c), device_id_type=MESH))
    return copies


def gather_ici_start(name, bufs, after):
    n = len(bufs)

    def body(*refs):
        ins = refs[:n]
        send_sems, recv_sems = refs[n + 1], refs[n + 2]
        token = refs[-1]
        for send in _gather_ici_copies(ins, send_sems, recv_sems, False):
            send.start()
        token[...] = jnp.zeros_like(token)

    res = pl.pallas_call(
        body, name=name,
        out_shape=(pltpu.SemaphoreType.DMA((3 * n,)), pltpu.SemaphoreType.DMA((3 * n,)), *[pltpu.HBM(b.shape, b.dtype) for b in bufs],
                   jax.ShapeDtypeStruct((8, 128), F32)),
        in_specs=[_HBM] * n + [_ANY], out_specs=(_SEM, _SEM, *[_HBM] * n, pl.BlockSpec(memory_space=pltpu.VMEM)),
        input_output_aliases={i: 2 + i for i in range(n)},
        compiler_params=pltpu.CompilerParams(has_side_effects=_DATAFLOW),
    )(*[_in_hbm(b) for b in bufs], after)
    return res[0], res[1], list(res[2:2 + n]), res[-1]


def gather_ici_wait(name, send_sems, recv_sems, bufs, after):
    n = len(bufs)

    def body(*refs):
        ins = refs[:n]
        send_sems_ref, recv_sems_ref = refs[n], refs[n + 1]
        for send in _gather_ici_copies(ins, send_sems_ref, recv_sems_ref, False):
            send.wait_send()
        for recv in _gather_ici_copies(ins, send_sems_ref, recv_sems_ref, True):
            recv.wait_recv()

    return pl.pallas_call(
        body, name=name, out_shape=tuple(pltpu.HBM(b.shape, b.dtype) for b in bufs),
        in_specs=[_HBM] * n + [_SEM, _SEM, _ANY], out_specs=tuple([_HBM] * n),
        input_output_aliases={i: i for i in range(n)},
        compiler_params=pltpu.CompilerParams(has_side_effects=_DATAFLOW),
    )(*bufs, send_sems, recv_sems, after)


def gather_forward(name, bufs):
    n = len(bufs)

    def body(*refs):
        outs = refs[n:2 * n]
        send_sems, recv_sems = refs[2 * n:]
        x, y, c = _place()
        sibling = (x, y, 1 - c)
        sends = []
        for i in range(n):
            rows = outs[i].shape[2]
            for j, (qx, qy) in enumerate(_other_chips(x, y)):
                block = _half(outs[i].at[2 * qx + qy], c, rows)
                fw = pltpu.make_async_remote_copy(
                    src_ref=block, dst_ref=block, send_sem=send_sems.at[3 * i + j], recv_sem=recv_sems.at[3 * i + j],
                    device_id=sibling, device_id_type=MESH)
                fw.start()
                sends.append(fw)
        for i in range(n):
            rows = outs[i].shape[2]
            for j, (qx, qy) in enumerate(_other_chips(x, y)):
                block = _half(outs[i].at[2 * qx + qy], 1 - c, rows)
                pltpu.make_async_remote_copy(
                    src_ref=block, dst_ref=block, send_sem=send_sems.at[3 * i + j], recv_sem=recv_sems.at[3 * i + j],
                    device_id=sibling, device_id_type=MESH).wait_recv()
        for fw in sends:
            fw.wait_send()

    return pl.pallas_call(
        body, name=name, in_specs=[_ANY] * n, out_specs=[_ANY] * n,
        out_shape=[jax.ShapeDtypeStruct(b.shape, b.dtype) for b in bufs],
        input_output_aliases={i: i for i in range(n)},
        scratch_shapes=[pltpu.SemaphoreType.DMA((3 * n,)), pltpu.SemaphoreType.DMA((3 * n,))],
        compiler_params=pltpu.CompilerParams(has_side_effects=True),
    )(*bufs)


def _between_chips_copies(parts, lands, send_sems, recv_sems):
    x, y, c = _place()
    copies = []
    for i, (part, land) in enumerate(zip(parts, lands)):
        for j, (qx, qy) in enumerate(_other_chips(x, y)):
            copies.append(pltpu.make_async_remote_copy(
                src_ref=part.at[2 * qx + qy], dst_ref=land.at[j], send_sem=send_sems.at[3 * i + j], recv_sem=recv_sems.at[3 * i + j],
                device_id=(qx, qy, c), device_id_type=MESH))
    return copies


def between_chips_start(name, parts):
    n = len(parts)
    lands = [lax.empty((N_CHIPS - 1,) + p.shape[1:], p.dtype) for p in parts]

    def body(*refs):
        send_sems, recv_sems = refs[2 * n], refs[2 * n + 1]
        token = refs[-1]
        for cp in _between_chips_copies(refs[:n], refs[n:2 * n], send_sems, recv_sems):
            cp.start()
        token[...] = jnp.zeros_like(token)

    res = pl.pallas_call(
        body, name=name,
        out_shape=(pltpu.SemaphoreType.DMA((3 * n,)), pltpu.SemaphoreType.DMA((3 * n,)),
                   *[pltpu.HBM(a.shape, a.dtype) for a in parts + lands], jax.ShapeDtypeStruct((8, 128), F32)),
        in_specs=[_HBM] * (2 * n), out_specs=(_SEM, _SEM, *[_HBM] * (2 * n), pl.BlockSpec(memory_space=pltpu.VMEM)),
        input_output_aliases={i: 2 + i for i in range(2 * n)},
        compiler_params=pltpu.CompilerParams(has_side_effects=_DATAFLOW),
    )(*[_in_hbm(a) for a in parts + lands])
    return res[0], res[1], list(res[2:2 + n]), list(res[2 + n:2 + 2 * n]), res[-1]


def between_chips_wait(name, send_sems, recv_sems, parts, lands, after):
    n = len(parts)

    def body(*refs):
        for cp in _between_chips_copies(refs[:n], refs[n:2 * n], refs[2 * n], refs[2 * n + 1]):
            cp.wait_send()
            cp.wait_recv()

    res = pl.pallas_call(
        body, name=name, out_shape=tuple(pltpu.HBM(a.shape, a.dtype) for a in parts + lands),
        in_specs=[_HBM] * (2 * n) + [_SEM, _SEM, _ANY], out_specs=tuple([_HBM] * (2 * n)),
        input_output_aliases={i: i for i in range(2 * n)},
        compiler_params=pltpu.CompilerParams(has_side_effects=_DATAFLOW),
    )(*parts, *lands, send_sems, recv_sems, after)
    return list(res[:n]), list(res[n:])


def exchange_start(name, arrays, n_copies, copies, after=()):
    n = len(arrays)

    def body(*refs):
        send_sems, recv_sems = refs[n + len(after)], refs[n + len(after) + 1]
        for cp in copies(refs[:n], send_sems, recv_sems):
            cp.start()
        refs[-1][...] = jnp.zeros_like(refs[-1])

    res = pl.pallas_call(
        body, name=name,
        out_shape=(pltpu.SemaphoreType.DMA((n_copies,)), pltpu.SemaphoreType.DMA((n_copies,)),
                   *[pltpu.HBM(a.shape, a.dtype) for a in arrays], jax.ShapeDtypeStruct((8, 128), F32)),
        in_specs=[_HBM] * n + [_ANY] * len(after),
        out_specs=(_SEM, _SEM, *[_HBM] * n, pl.BlockSpec(memory_space=pltpu.VMEM)),
        input_output_aliases={i: 2 + i for i in range(n)},
        compiler_params=pltpu.CompilerParams(has_side_effects=_DATAFLOW),
    )(*[_in_hbm(a) for a in arrays], *after)
    return res[0], res[1], list(res[2:2 + n]), res[-1]


def exchange_wait(name, send_sems, recv_sems, arrays, copies, after):
    n = len(arrays)

    def body(*refs):
        for cp in copies(refs[:n], refs[n], refs[n + 1]):
            cp.wait_send()
            cp.wait_recv()

    return list(pl.pallas_call(
        body, name=name, out_shape=tuple(pltpu.HBM(a.shape, a.dtype) for a in arrays),
        in_specs=[_HBM] * n + [_SEM, _SEM, _ANY], out_specs=tuple([_HBM] * n),
        input_output_aliases={i: i for i in range(n)},
        compiler_params=pltpu.CompilerParams(has_side_effects=_DATAFLOW),
    )(*arrays, send_sems, recv_sems, after))


def _to_sibling_copies(n):
    def copies(refs, send_sems, recv_sems):
        x, y, c = _place()
        out = []
        for i in range(n):
            rows = refs[i].shape[2]
            out.append(pltpu.make_async_remote_copy(
                src_ref=refs[i].at[:, :, pl.ds((1 - c) * (rows // 2), rows // 2), :], dst_ref=refs[n + i],
                send_sem=send_sems.at[i], recv_sem=recv_sems.at[i], device_id=(x, y, 1 - c), device_id_type=MESH))
        return out
    return copies


def _share_copies(n):
    def copies(refs, send_sems, recv_sems):
        x, y, c = _place()
        out = []
        for i in range(n):
            mine = _half(refs[i], c, refs[i].shape[1])
            out.append(pltpu.make_async_remote_copy(
                src_ref=mine, dst_ref=mine, send_sem=send_sems.at[i], recv_sem=recv_sems.at[i],
                device_id=(x, y, 1 - c), device_id_type=MESH))
        return out
    return copies


def pair_sum(name, grad, recv, c):
    ns, na, rh, cols = recv.shape
    tr = _row_tile(rh, 256) if rh % 256 == 0 else rh
    nt = rh // tr

    def body(c_ref, g_ref, r_ref, o_ref):
        o_ref[...] = (g_ref[...].astype(F32) + r_ref[...].astype(F32)).astype(BF16)

    blk = (None, None, tr, cols)
    return pl.pallas_call(
        body, name=name,
        grid_spec=pltpu.PrefetchScalarGridSpec(
            num_scalar_prefetch=1, grid=(ns, na, nt),
            in_specs=[pl.BlockSpec(blk, lambda s, a, r, c_ref: (s, a, c_ref[0] * nt + r, 0)),
                      pl.BlockSpec(blk, lambda s, a, r, c_ref: (s, a, r, 0))],
            out_specs=pl.BlockSpec(blk, lambda s, a, r, c_ref: (s, a, r, 0))),
        out_shape=jax.ShapeDtypeStruct(recv.shape, BF16),
        compiler_params=_params(("parallel", "parallel", "parallel")),
    )(c, grad, recv)


def chip_sum(name, parts, landed, place):
    _, na, rh, cols = parts.shape
    tr = _row_tile(rh, 256) if rh % 256 == 0 else rh
    nt = rh // tr

    def body(place_ref, p_ref, l_ref, o_ref):
        total = p_ref[...].astype(F32)
        for s in range(N_CHIPS - 1):
            total = total + l_ref[s].astype(F32)
        o_ref[...] = total

    return pl.pallas_call(
        body, name=name,
        grid_spec=pltpu.PrefetchScalarGridSpec(
            num_scalar_prefetch=1, grid=(na, nt),
            in_specs=[pl.BlockSpec((None, None, tr, cols), lambda a, r, pr: (pr[1], a, r, 0)),
                      pl.BlockSpec((N_CHIPS - 1, None, tr, cols), lambda a, r, pr: (0, a, r, 0))],
            out_specs=pl.BlockSpec((None, tr, cols), lambda a, r, pr: (a, pr[0] * nt + r, 0))),
        out_shape=jax.ShapeDtypeStruct((na, 2 * rh, cols), F32),
        compiler_params=_params(("parallel", "parallel")),
    )(place, parts, landed)


def reduce_scatter_1(tag, grads):
    n = len(grads)
    lands = [lax.empty((g.shape[0], g.shape[1], g.shape[2] // 2, g.shape[3]), g.dtype) for g in grads]
    send_sems, recv_sems, arrays, token = exchange_start(f"{tag}_to_sibling_start", list(grads) + lands, n, _to_sibling_copies(n))
    return (send_sems, recv_sems, arrays), token


def reduce_scatter_2(tag, state, place, after):
    send_sems, recv_sems, arrays = state
    n = len(arrays) // 2
    arrays = exchange_wait(f"{tag}_to_sibling_wait", send_sems, recv_sems, arrays, _to_sibling_copies(n), after)
    parts = [pair_sum(f"{tag}_pair_sum{i}", g, r, place) for i, (g, r) in enumerate(zip(arrays[:n], arrays[n:]))]
    send_sems, recv_sems, parts, lands, token = between_chips_start(f"{tag}_between_chips_start", parts)
    return (send_sems, recv_sems, parts, lands), token


def reduce_scatter_3(tag, state, place, after):
    send_sems, recv_sems, parts, lands = state
    parts, landed = between_chips_wait(f"{tag}_between_chips_wait", send_sems, recv_sems, parts, lands, after)
    halves = [chip_sum(f"{tag}_chip_sum{i}", p, l, place) for i, (p, l) in enumerate(zip(parts, landed))]
    send_sems, recv_sems, halves, token = exchange_start(f"{tag}_share_start", halves, len(halves), _share_copies(len(halves)))
    return (send_sems, recv_sems, halves), token


def reduce_scatter_4(tag, state, after):
    send_sems, recv_sems, halves = state
    return exchange_wait(f"{tag}_share_wait", send_sems, recv_sems, halves, _share_copies(len(halves)), after)


def _adamw_math(w, g, m, v):
    m = ADAM_B1 * m + (1.0 - ADAM_B1) * g
    v = ADAM_B2 * v + (1.0 - ADAM_B2) * (g * g)
    m_hat = m / (1.0 - ADAM_B1 ** ADAM_STEP)
    v_hat = v / (1.0 - ADAM_B2 ** ADAM_STEP)
    delta = -ADAM_LR * (m_hat / (jnp.sqrt(v_hat) + ADAM_EPS) + ADAM_WD * w)
    return delta, m, v


def adamw(name, w, g, m, v, after=()):
    rows, cols = w.shape
    tr = _row_tile(rows, 256) if rows % 256 == 0 else rows // 2

    def body(w_ref, g_ref, m_ref, v_ref, d_ref, mo_ref, vo_ref):
        d_ref[...], mo_ref[...], vo_ref[...] = _adamw_math(w_ref[...], g_ref[...], m_ref[...], v_ref[...])

    blk = pl.BlockSpec((tr, cols), lambda i: (i, 0))
    return pl.pallas_call(
        _ordered_after(body, 4, len(after)), name=name, grid=(rows // tr,), in_specs=[blk] * 4 + [_ANY] * len(after),
        out_specs=[blk] * 3, out_shape=[jax.ShapeDtypeStruct(w.shape, F32)] * 3, compiler_params=_params(("parallel",)),
    )(w, g, m, v, *after)


def adamw_rows(name, w, g, m, v, after=()):
    rows, _, cols = w.shape
    tr = next(r for r in (128, 110, 64, 32, 16, 8, 1) if rows % r == 0)

    def body(w_ref, g_ref, m_ref, v_ref, d_ref, mo_ref, vo_ref):
        d_ref[...], mo_ref[...], vo_ref[...] = _adamw_math(w_ref[...], g_ref[...], m_ref[...], v_ref[...])

    blk = pl.BlockSpec((tr, 1, cols), lambda i: (i, 0, 0))
    return pl.pallas_call(
        _ordered_after(body, 4, len(after)), name=name, grid=(rows // tr,), in_specs=[blk] * 4 + [_ANY] * len(after),
        out_specs=[blk] * 3, out_shape=[jax.ShapeDtypeStruct(w.shape, F32)] * 3, compiler_params=_params(("parallel",)),
    )(w, g, m, v, *after)


def adamw_stacked(name, ws, g, ms, vs, after=()):
    n = len(ws)
    rows, cols = ws[0].shape
    tr = next(r for r in (128, 88, 64, 32, 16, 8) if rows % r == 0)

    def body(*refs):
        w_refs, m_refs, v_refs, g_ref = refs[:n], refs[n:2 * n], refs[2 * n:3 * n], refs[3 * n]
        outs = refs[3 * n + 1:]
        for i in range(n):
            outs[i][...], outs[n + i][...], outs[2 * n + i][...] = _adamw_math(
                w_refs[i][...], g_ref[i], m_refs[i][...], v_refs[i][...])

    blk = pl.BlockSpec((tr, cols), lambda r: (r, 0))
    res = pl.pallas_call(
        _ordered_after(body, 3 * n + 1, len(after)), name=name, grid=(rows // tr,),
        in_specs=[blk] * (3 * n) + [pl.BlockSpec((n, tr, cols), lambda r: (0, r, 0))] + [_ANY] * len(after),
        out_specs=[blk] * (3 * n),
        out_shape=[jax.ShapeDtypeStruct((rows, cols), F32)] * (3 * n), compiler_params=_params(("parallel",)),
    )(*ws, *ms, *vs, g, *after)
    return res[:n], res[n:2 * n], res[2 * n:]


def small_allreduce_adamw(name, g_part, w, m, v, after=()):
    rows, cols = g_part.shape

    def body(g_ref, w_ref, m_ref, v_ref, sum_ref, d_ref, mo_ref, vo_ref, land, send_sems, recv_sems):
        x, y, c = _place()
        me = 4 * x + 2 * y + c
        land[me] = g_ref[...]
        copies = []
        for r in range(1, 8):
            peer = (x ^ (r >> 2), y ^ ((r >> 1) & 1), c ^ (r & 1))
            rc = pltpu.make_async_remote_copy(
                src_ref=g_ref, dst_ref=land.at[me], send_sem=send_sems.at[r - 1], recv_sem=recv_sems.at[r - 1],
                device_id=peer, device_id_type=MESH)
            rc.start()
            copies.append(rc)
        for rc in copies:
            rc.wait()
        total = land[0]
        for s in range(1, 8):
            total = total + land[s]
        sum_ref[...] = total
        d_ref[...], mo_ref[...], vo_ref[...] = _adamw_math(w_ref[...], total, m_ref[...], v_ref[...])

    vmem = pl.BlockSpec(memory_space=pltpu.VMEM)
    return pl.pallas_call(
        _ordered_after(body, 4, len(after)), name=name, in_specs=[vmem] * 4 + [_ANY] * len(after), out_specs=[vmem] * 4,
        out_shape=[jax.ShapeDtypeStruct((rows, cols), F32)] * 4,
        scratch_shapes=[pltpu.VMEM((8, rows, cols), F32), pltpu.SemaphoreType.DMA((7,)), pltpu.SemaphoreType.DMA((7,))],
        compiler_params=pltpu.CompilerParams(has_side_effects=True),
    )(g_part, w, m, v, *after)


def kernel(x, norm_ffn1, w_ffn1_gate, w_ffn1_up, w_ffn1_down, norm_mix, w_in, b_forget, w_gate, b_gate, w_up_a, w_up_b, w_out, norm_ffn2, w_ffn2_gate, w_ffn2_up, w_ffn2_down, norm_final, loss_target, m_norm_ffn1, m_w_ffn1_gate, m_w_ffn1_up, m_w_ffn1_down, m_norm_mix, m_w_in, m_b_forget, m_w_gate, m_b_gate, m_w_up_a, m_w_up_b, m_w_out, m_norm_ffn2, m_w_ffn2_gate, m_w_ffn2_up, m_w_ffn2_down, m_norm_final, v_norm_ffn1, v_w_ffn1_gate, v_w_ffn1_up, v_w_ffn1_down, v_norm_mix, v_w_in, v_b_forget, v_w_gate, v_b_gate, v_w_up_a, v_w_up_b, v_w_out, v_norm_ffn2, v_w_ffn2_gate, v_w_ffn2_up, v_w_ffn2_down, v_norm_final):
    t, d = x.shape[1], x.shape[2]
    in4 = w_in.shape[2]
    gate4 = w_gate.shape[2]
    in_cols = N_CHIPS * in4
    n_forget = in_cols - QKV_COLS
    assert w_up_a.shape[1] == WIDTH and d == 2 * WIDTH and n_forget == N_HEADS and t % BQ == 0
    chip = 2 * lax.axis_index("x") + lax.axis_index("y")
    c_arr = jnp.stack([lax.axis_index("c"), chip]).astype(jnp.int32)
    x2d = x[0]
    target = loss_target[0]

    def slot(shard):
        return lax.dynamic_update_slice(lax.empty((N_CHIPS,) + shard.shape, BF16), shard.astype(BF16)[None], (chip, 0, 0, 0))

    def ffn_views(wg, wu, wd):
        return [wg[0].T, wu[0].T, wd[0]]

    ffn1_w, ffn1_m, ffn1_v = (ffn_views(w_ffn1_gate, w_ffn1_up, w_ffn1_down), ffn_views(m_w_ffn1_gate, m_w_ffn1_up, m_w_ffn1_down),
                              ffn_views(v_w_ffn1_gate, v_w_ffn1_up, v_w_ffn1_down))
    ffn2_w, ffn2_m, ffn2_v = (ffn_views(w_ffn2_gate, w_ffn2_up, w_ffn2_down), ffn_views(m_w_ffn2_gate, m_w_ffn2_up, m_w_ffn2_down),
                              ffn_views(v_w_ffn2_gate, v_w_ffn2_up, v_w_ffn2_down))
    in_pad = -(-in4 // 32) * 32
    in_sh = slot(w_in[0].T[None])
    gt_sh = slot(w_gate[0].T[None])
    up_sh = slot(jnp.concatenate([w_up_a[0], w_up_b[0]], axis=0)[None])
    wo_sh = slot(w_out)
    f1_send, f1_recv, f1_bufs, f1_token = gather_ici_start("gather_ffn1_start", [slot(jnp.stack(ffn1_w))], norm_ffn1)
    mx_send, mx_recv, mx_bufs, mx_token = gather_ici_start("gather_mixer_start", [in_sh, gt_sh, up_sh, wo_sh], f1_token)
    f2_send, f2_recv, f2_bufs, f2_token = gather_ici_start("gather_ffn2_start", [slot(jnp.stack(ffn2_w))], mx_token)

    normed1 = rms_fwd("ffn1_rms", x2d, norm_ffn1, after=(f2_token,))
    (w3_1,) = gather_forward("gather_ffn1_forward", gather_ici_wait("gather_ffn1_wait", f1_send, f1_recv, f1_bufs, normed1[0]))
    x1, saved1 = ffn_forward("ffn1", x2d, norm_ffn1, w3_1, normed=normed1)
    w_in_g, w_gate_g, wup, wo = gather_forward(
        "gather_mixer_forward", gather_ici_wait("gather_mixer_wait", mx_send, mx_recv, mx_bufs, x1))
    wup = wup[:, 0]
    w_in_t = w_in_g.reshape(in_cols, d)
    w_gate_t = w_gate_g.reshape(2 * d, d)
    w_f_t = jnp.pad(w_in_t[QKV_COLS:], ((0, QB - n_forget), (0, 0)))
    wo_full = wo.reshape(d, d)
    b_forget_row = jnp.pad(b_forget, ((0, 0), (0, QB - n_forget)))

    h2, rstd2 = rms_fwd("mix_rms", x1, norm_mix)
    qkv = proj("mix_proj_qkv", h2, w_in_t, jnp.zeros((1, QKV_COLS), F32), 0, QKV_COLS, WIDTH, BF16, scaled_tiles=(0, 3))
    pc = proj("mix_proj_gates", h2, w_gate_t, b_gate, 0, 2 * d, WIDTH, F32)
    f_logit = proj("mix_proj_forget", h2, w_f_t, jnp.zeros((1, QB), F32), 0, QB, QB, F32)
    c_cum = fox_prep("fox_prep", f_logit, b_forget_row)
    c_heads = c_cum[:, :N_HEADS].T
    c_col = c_heads[:, :, None]
    c_row = c_heads.reshape(N_HEADS, t // CS, 1, CS)
    ya = sb_pair_fwd("sb_fwd", qkv)
    yb, lse = fox_pair_fwd("fox_fwd", qkv, c_col, c_row)
    ua, ub, mixed = mix_fwd("mix_fwd", ya, yb, wup, pc)
    x2 = mm_residual("mix_out", mixed[None], wo_full[None], pl.BlockSpec((1, d, d), lambda m: (0, 0, 0)), x1, 1.0)
    (w3_2,) = gather_forward("gather_ffn2_forward", gather_ici_wait("gather_ffn2_wait", f2_send, f2_recv, f2_bufs, x2))
    x3, saved2 = ffn_forward("ffn2", x2, norm_ffn2, w3_2)
    dx3, gn_final, loss_part = loss_head("loss_head", x3, norm_final[None], target)

    g_w3_2, dab2 = ffn_backward_weights("ffn2", dx3, saved2, w3_2)
    rs_ffn2, token = reduce_scatter_1("rs_ffn2", [g_w3_2])
    dx2, gn_ffn2 = ffn_backward_input("ffn2", dx3, saved2, dab2, norm_ffn2, w3_2, after=(token,))
    rs_ffn2, rs_ffn2_token = reduce_scatter_2("rs_ffn2", rs_ffn2, c_arr, dx2)

    dua, dub, dpa, dpb, gba, gbb = mix_bwd("mix_bwd", dx2, wo_full, pc, ua, ub, after=(rs_ffn2_token,))
    g_bgate = jnp.concatenate([gba, gbb], axis=1)
    g_wo = mm_plain("wgrad_out", mixed, dx2, TN, BF16, tk_target=1024)
    dya = up_bwd("dya", dua, wup, 0)
    dyb = up_bwd("dyb", dub, wup, 1)
    g_up = up_wgrad("wgrad_up", ya, yb, dua, dub)
    dqkv = sb_pair_bwd("sb_bwd", qkv, dya, lax.empty((t, QKV_COLS), BF16))
    dqkv, dcc, dcr = fox_pair_bwd("fox_bwd", qkv, yb, dyb, lse, c_col, c_row, dqkv)
    dc = dcc[:, :, :2].transpose(1, 0, 2).reshape(t, N_HEADS) + dcr.reshape(N_HEADS, t).T
    df, g_bf = fox_gate_bwd("fox_gate_bwd", jnp.pad(dc, ((0, 0), (0, QB - N_HEADS))), f_logit, b_forget_row)
    df = df.astype(BF16)
    g_qkv_t = wgrad_cat("wgrad_qkv", h2, dqkv)
    g_f_t = wgrad_cat("wgrad_forget", h2, df)
    g_gate_t = jnp.stack([wgrad_cat("wgrad_gate_a", h2, dpa), wgrad_cat("wgrad_gate_b", h2, dpb)])
    g_in_t = jnp.concatenate([g_qkv_t, g_f_t[:n_forget]], axis=0).reshape(N_CHIPS, in4, d)
    rs_mixer, token = reduce_scatter_1(
        "rs_mixer", [jnp.pad(g_in_t, ((0, 0), (0, in_pad - in4), (0, 0)))[:, None], g_gate_t.reshape(N_CHIPS, 1, gate4, d),
                     g_up[:, None], g_wo.reshape(N_CHIPS, 1, d // N_CHIPS, d)])
    dx1, gn_mix = mixer_dh("mix_dh", [(dqkv, w_in_t, 0), (dpa, w_gate_t, 0), (dpb, w_gate_t, 1), (df, w_f_t, 0)],
                           x1, rstd2, norm_mix, dx2, after=(token,))
    rs_mixer, rs_mixer_token = reduce_scatter_2("rs_mixer", rs_mixer, c_arr, dx1)
    rs_ffn2, rs_ffn2_token = reduce_scatter_3("rs_ffn2", rs_ffn2, c_arr, dx1)

    big_out = {}

    def adamw_ffn(tag, s_w3, ws, ms, vs, after):
        deltas, new_ms, new_vs = adamw_stacked(f"adamw_{tag}", ws, s_w3, ms, vs, after)
        for which, part in ((GATE, "gate"), (UP, "up"), (DOWN, "down")):
            back = (lambda a: a[None]) if which == DOWN else (lambda a: a.T[None])
            big_out[f"w_{tag}_{part}"] = tuple(back(a) for a in (s_w3[which], deltas[which], new_ms[which], new_vs[which]))
        return deltas[DOWN]

    g_w3_1, dab1 = ffn_backward_weights("ffn1", dx1, saved1, w3_1, after=(rs_mixer_token, rs_ffn2_token))
    rs_ffn1, token = reduce_scatter_1("rs_ffn1", [g_w3_1])
    rs_mixer, rs_mixer_token = reduce_scatter_3("rs_mixer", rs_mixer, c_arr, token)
    (s_w3_2,) = reduce_scatter_4("rs_ffn2", rs_ffn2, rs_mixer_token)
    rs_ffn1, rs_ffn1_token = reduce_scatter_2("rs_ffn1", rs_ffn1, c_arr, s_w3_2)
    adamw_ffn("ffn2", s_w3_2, ffn2_w, ffn2_m, ffn2_v, (rs_ffn1_token,))
    dx0, gn_ffn1 = ffn_backward_input("ffn1", dx1, saved1, dab1, norm_ffn1, w3_1, after=(rs_ffn1_token,))

    def pack_small(n1, nm, n2, nf, bg, bf, last):
        return jnp.concatenate([n1, nm, n2, nf, bg.reshape(2, d), jnp.pad(bf, ((0, 0), (0, d - n_forget))), last], axis=0)

    zero_row = jnp.zeros((1, d), F32)
    g_small = pack_small(gn_ffn1, gn_mix, gn_ffn2, gn_final, g_bgate, g_bf[:, :n_forget], jnp.pad(loss_part, ((0, 0), (0, d - 1))))
    w_small = pack_small(norm_ffn1, norm_mix, norm_ffn2, norm_final[None], b_gate, b_forget, zero_row)
    m_small = pack_small(m_norm_ffn1, m_norm_mix, m_norm_ffn2, m_norm_final[None], m_b_gate, m_b_forget, zero_row)
    v_small = pack_small(v_norm_ffn1, v_norm_mix, v_norm_ffn2, v_norm_final[None], v_b_gate, v_b_forget, zero_row)

    s_in, s_gt, s_up, s_wo = reduce_scatter_4("rs_mixer", rs_mixer, dx0)
    grads = {"w_gate": s_gt[0].T, "w_up_a": s_up[0, :WIDTH], "w_up_b": s_up[0, WIDTH:], "w_out": s_wo[0]}
    weights = {"w_gate": (w_gate, m_w_gate, v_w_gate), "w_up_a": (w_up_a, m_w_up_a, v_w_up_a),
               "w_up_b": (w_up_b, m_w_up_b, v_w_up_b), "w_out": (w_out, m_w_out, v_w_out)}
    for wname, (w, m, v) in weights.items():
        g = grads[wname]
        delta, new_m, new_v = adamw(f"adamw_{wname}", w[0], g, m[0], v[0])
        big_out[wname] = (g[None], delta[None], new_m[None], new_v[None])
    rows_of = lambda a: jnp.transpose(a, (2, 0, 1))
    g_in_rows = s_in[0, :in4][:, None, :]
    in_rows = adamw_rows("adamw_w_in", rows_of(w_in), g_in_rows, rows_of(m_w_in), rows_of(v_w_in))
    big_out["w_in"] = tuple(jnp.transpose(a, (1, 2, 0)) for a in (g_in_rows, *in_rows))

    rs_ffn1, token = reduce_scatter_3("rs_ffn1", rs_ffn1, c_arr, in_rows[0])
    smalls = small_allreduce_adamw("small_allreduce_adamw", g_small, w_small, m_small, v_small, after=(token,))
    (s_w3_1,) = reduce_scatter_4("rs_ffn1", rs_ffn1, smalls[0])
    adamw_ffn("ffn1", s_w3_1, ffn1_w, ffn1_m, ffn1_v, ())

    def unpack_small(p):
        return {"norm_ffn1": p[0:1], "norm_mix": p[1:2], "norm_ffn2": p[2:3], "norm_final": p[3], "b_gate": p[4:6].reshape(1, 2 * d),
                "b_forget": p[6:7, :n_forget]}

    loss = smalls[0][7, 0]
    small_out = [unpack_small(p) for p in smalls]

    order = ["norm_ffn1", "w_ffn1_gate", "w_ffn1_up", "w_ffn1_down", "norm_mix", "w_in", "b_forget", "w_gate", "b_gate",
             "w_up_a", "w_up_b", "w_out", "norm_ffn2", "w_ffn2_gate", "w_ffn2_up", "w_ffn2_down", "norm_final"]
    outs = [loss, dx0[None]]
    for kind in range(4):
        for wname in order:
            outs.append(big_out[wname][kind] if wname in big_out else small_out[kind][wname])
    return tuple(outs)
```

```python
import functools

import jax
import jax.numpy as jnp
from jax import lax
from jax.experimental import pallas as pl
from jax.experimental.pallas import tpu as pltpu

F32 = jnp.float32
BF16 = jnp.bfloat16

HEAD_DIM = 64
N_HEADS = 8
WIDTH = N_HEADS * HEAD_DIM
QKV_COLS = 6 * WIDTH
RMS_EPS = 1e-6
ATTN_SCALE = HEAD_DIM ** -0.5
N_CHIPS = 4
QB = 128
BQ = 2048
CS = 256
N_SUB = BQ // CS
NEG_BIG = -1e30

ADAM_LR = 0.001
ADAM_B1 = 0.9
ADAM_B2 = 0.999
ADAM_EPS = 1e-08
ADAM_WD = 0.01
ADAM_STEP = 10

VMEM_LIMIT_BYTES = 48 * 1024 * 1024
MESH = pl.DeviceIdType.MESH

NN = ((1,), (0,))
NT = ((1,), (1,))
TN = ((0,), (0,))


def _params(semantics):
    return pltpu.CompilerParams(dimension_semantics=semantics, vmem_limit_bytes=VMEM_LIMIT_BYTES)


def _dot(a, b, contract):
    return lax.dot_general(a.astype(BF16), b.astype(BF16), (contract, ((), ())), preferred_element_type=F32)


def _sigmoid(x):
    return 1.0 / (1.0 + jnp.exp(-x))


def _log1pexp_neg_abs(z):
    return jnp.log(1.0 + jnp.exp(-jnp.abs(z)))


def _split3(x):
    hi = x.astype(BF16)
    r1 = x - hi.astype(F32)
    mid = r1.astype(BF16)
    lo = (r1 - mid.astype(F32)).astype(BF16)
    return hi, mid, lo


def _dot_exact_lhs01(m01, x):
    hi, mid, lo = _split3(x)
    d = lambda p: lax.dot_general(m01, p, (NN, ((), ())), preferred_element_type=F32)
    return d(hi) + d(mid) + d(lo)


def _iota2(shape, dim):
    return lax.broadcasted_iota(jnp.int32, shape, dim)


def _mm(name, pairs, contract, grid, pair_specs, out_shape, out_specs, acc_shape, nk, epilogue,
        extras=(), extra_specs=(), semantics=None):
    n_pairs = len(pairs)
    n_extra = len(extras)
    n_out = len(out_shape)

    def body(*refs):
        ab = refs[:2 * n_pairs]
        ex = refs[2 * n_pairs:2 * n_pairs + n_extra]
        outs = refs[2 * n_pairs + n_extra:2 * n_pairs + n_extra + n_out]
        ids = [pl.program_id(i) for i in range(len(grid))]
        k = ids[-1]
        part = _dot(ab[0][...], ab[1][...], contract)
        for p in range(1, n_pairs):
            part += _dot(ab[2 * p][...], ab[2 * p + 1][...], contract)
        if nk == 1:
            epilogue(part, ex, outs, ids)
            return
        acc = refs[-1]

        @pl.when(k == 0)
        def _():
            acc[...] = part

        @pl.when(k != 0)
        def _():
            acc[...] += part

        @pl.when(k == nk - 1)
        def _():
            epilogue(acc[...], ex, outs, ids)

    operands = [t for pair in pairs for t in pair] + list(extras)
    in_specs = [s for pair in pair_specs for s in pair] + list(extra_specs)
    if semantics is None:
        semantics = ("parallel",) * (len(grid) - 1) + ("arbitrary",)
    return pl.pallas_call(
        body, name=name, grid=grid, in_specs=in_specs, out_specs=list(out_specs), out_shape=list(out_shape),
        scratch_shapes=[] if nk == 1 else [pltpu.VMEM(acc_shape, F32)], compiler_params=_params(semantics),
    )(*operands)


def _ordered_after(body, n_in, n_after):
    def wrapped(*refs):
        return body(*refs[:n_in], *refs[n_in + n_after:])
    return wrapped


def _row_tile(rows, target):
    t = min(rows, target)
    while rows % t:
        t //= 2
    return t


def rms_fwd(name, x, g, after=()):
    t, d = x.shape
    tr = _row_tile(t, 256)

    def body(x_ref, g_ref, h_ref, r_ref):
        xv = x_ref[...]
        r = lax.rsqrt(jnp.mean(xv * xv, axis=-1, keepdims=True) + RMS_EPS)
        h_ref[...] = (xv * r * g_ref[...]).astype(BF16)
        r_ref[...] = r

    return pl.pallas_call(
        _ordered_after(body, 2, len(after)), name=name, grid=(t // tr,),
        in_specs=[pl.BlockSpec((tr, d), lambda i: (i, 0)), pl.BlockSpec((1, d), lambda i: (0, 0))] + [_ANY] * len(after),
        out_specs=[pl.BlockSpec((tr, d), lambda i: (i, 0)), pl.BlockSpec((tr, 1), lambda i: (i, 0))],
        out_shape=[jax.ShapeDtypeStruct((t, d), BF16), jax.ShapeDtypeStruct((t, 1), F32)],
        compiler_params=_params(("parallel",)),
    )(x, g, *after)


GATE, UP, DOWN = 0, 1, 2


def _ffn_w_spec(which, f4, d, index_of_j):
    return pl.BlockSpec((None, None, f4, d), lambda *ids: (index_of_j(*ids), which, 0, 0))


def ffn_up(name, h, w3):
    t, d = h.shape
    ns, _, f4, _ = w3.shape
    tm = _row_tile(t, 512)

    def body(h_ref, wg_ref, wu_ref, a_ref, b_ref, s_ref):
        hv = h_ref[...]
        a = _dot(hv, wg_ref[...], NT)
        b = _dot(hv, wu_ref[...], NT)
        a_ref[...] = a.astype(BF16)
        b_ref[...] = b.astype(BF16)
        s_ref[...] = (a * _sigmoid(a) * b).astype(BF16)

    act_spec = pl.BlockSpec((None, tm, f4), lambda j, m: (j, m, 0))
    return pl.pallas_call(
        body, name=name, grid=(ns, t // tm),
        in_specs=[pl.BlockSpec((tm, d), lambda j, m: (m, 0)),
                  _ffn_w_spec(GATE, f4, d, lambda j, m: j), _ffn_w_spec(UP, f4, d, lambda j, m: j)],
        out_specs=[act_spec, act_spec, act_spec],
        out_shape=[jax.ShapeDtypeStruct((ns, t, f4), BF16)] * 3,
        compiler_params=_params(("parallel", "parallel")),
    )(h, w3, w3)


def mm_residual(name, s, w, w_spec, x, scale):
    nj, t, kdim = s.shape
    n = x.shape[1]
    tm = _row_tile(t, 512)

    def body(s_ref, w_ref, x_ref, o_ref):
        acc = _dot(s_ref[0], w_ref[0], NN)
        for j in range(1, nj):
            acc += _dot(s_ref[j], w_ref[j], NN)
        o_ref[...] = x_ref[...] + scale * acc

    row = pl.BlockSpec((tm, n), lambda m: (m, 0))
    return pl.pallas_call(
        body, name=name, grid=(t // tm,),
        in_specs=[pl.BlockSpec((nj, tm, kdim), lambda m: (0, m, 0)), w_spec, row], out_specs=row,
        out_shape=jax.ShapeDtypeStruct((t, n), F32), compiler_params=_params(("parallel",)),
    )(s, w, x)


def ffn_bwd_act(name, dx, w3, a, b, after=()):
    t, d = dx.shape
    ns, _, f4, _ = w3.shape
    tm = _row_tile(t, 512)

    def body(dx_ref, wd_ref, a_ref, b_ref, da_ref, db_ref):
        ds = _dot(0.5 * dx_ref[...], wd_ref[...], NT)
        av = a_ref[...].astype(F32)
        sig = _sigmoid(av)
        da_ref[...] = (ds * b_ref[...].astype(F32) * (sig * (1.0 + av * (1.0 - sig)))).astype(BF16)
        db_ref[...] = (ds * (av * sig)).astype(BF16)

    act_spec = pl.BlockSpec((None, tm, f4), lambda j, m: (j, m, 0))
    return pl.pallas_call(
        _ordered_after(body, 4, len(after)), name=name, grid=(ns, t // tm),
        in_specs=[pl.BlockSpec((tm, d), lambda j, m: (m, 0)), _ffn_w_spec(DOWN, f4, d, lambda j, m: j), act_spec, act_spec]
        + [_ANY] * len(after),
        out_specs=[act_spec, act_spec],
        out_shape=[jax.ShapeDtypeStruct((ns, t, f4), BF16)] * 2,
        compiler_params=_params(("parallel", "parallel")),
    )(dx, w3, a, b, *after)


def ffn_wgrad(name, h, da, db, s, dx):
    t, d = h.shape
    ns, _, f4 = da.shape
    tk = _row_tile(t, 1024)
    nk = t // tk

    def body(h_ref, da_ref, db_ref, s_ref, dx_ref, o_ref, acc):
        k = pl.program_id(1)

        @pl.when(k == 0)
        def _():
            acc[...] = jnp.zeros_like(acc)

        hv = h_ref[...]
        acc[GATE] += _dot(da_ref[...], hv, TN)
        acc[UP] += _dot(db_ref[...], hv, TN)
        acc[DOWN] += _dot(s_ref[...], 0.5 * dx_ref[...], TN)

        @pl.when(k == nk - 1)
        def _():
            o_ref[...] = acc[...].astype(BF16)

    act_spec = pl.BlockSpec((None, tk, f4), lambda j, k: (j, k, 0))
    row_spec = pl.BlockSpec((tk, d), lambda j, k: (k, 0))
    return pl.pallas_call(
        body, name=name, grid=(ns, nk),
        in_specs=[row_spec, act_spec, act_spec, act_spec, row_spec],
        out_specs=pl.BlockSpec((None, 3, f4, d), lambda j, k: (j, 0, 0, 0)),
        out_shape=jax.ShapeDtypeStruct((ns, 3, f4, d), BF16),
        scratch_shapes=[pltpu.VMEM((3, f4, d), F32)],
        compiler_params=_params(("parallel", "arbitrary")),
    )(h, da, db, s, dx)


def _rms_bwd_tail(dh, x_ref, r_ref, g_ref, dxin_ref, dx_ref, gn_ref, row_tile_index):
    r = r_ref[...]
    xhat = x_ref[...] * r
    dhg = dh * g_ref[...]
    dx_ref[...] = dxin_ref[...] + r * (dhg - xhat * jnp.mean(dhg * xhat, axis=-1, keepdims=True))
    part = jnp.sum(dh * xhat, axis=0, keepdims=True)

    @pl.when(row_tile_index == 0)
    def _():
        gn_ref[...] = part

    @pl.when(row_tile_index != 0)
    def _():
        gn_ref[...] += part


def ffn_dh(name, da, db, w3, x, rstd, g, dx_in, after=()):
    ns, t, f4 = da.shape
    d = x.shape[1]
    tm = _row_tile(t, 256)

    def body(da_ref, db_ref, wg_ref, wu_ref, x_ref, r_ref, g_ref, dxin_ref, dx_ref, gn_ref):
        dh = _dot(da_ref[0], wg_ref[0], NN) + _dot(db_ref[0], wu_ref[0], NN)
        for j in range(1, ns):
            dh += _dot(da_ref[j], wg_ref[j], NN) + _dot(db_ref[j], wu_ref[j], NN)
        _rms_bwd_tail(dh, x_ref, r_ref, g_ref, dxin_ref, dx_ref, gn_ref, pl.program_id(0))

    act = pl.BlockSpec((ns, tm, f4), lambda m: (0, m, 0))
    row = pl.BlockSpec((tm, d), lambda m: (m, 0))
    gain = pl.BlockSpec((1, d), lambda m: (0, 0))
    return pl.pallas_call(
        _ordered_after(body, 8, len(after)), name=name, grid=(t // tm,),
        in_specs=[act, act, pl.BlockSpec((ns, None, f4, d), lambda m: (0, GATE, 0, 0)),
                  pl.BlockSpec((ns, None, f4, d), lambda m: (0, UP, 0, 0)), row, pl.BlockSpec((tm, 1), lambda m: (m, 0)), gain, row]
        + [_ANY] * len(after),
        out_specs=[row, gain], out_shape=[jax.ShapeDtypeStruct((t, d), F32), jax.ShapeDtypeStruct((1, d), F32)],
        compiler_params=_params(("arbitrary",)),
    )(da, db, w3, w3, x, rstd, g, dx_in, *after)


def mixer_dh(name, parts, x, rstd, g, dx_in, after=()):
    t, d = x.shape
    n = len(parts)
    tm = _row_tile(t, 256)

    def body(*refs):
        dh = _dot(refs[0][...], refs[n][...], NN)
        for i in range(1, n):
            dh += _dot(refs[i][...], refs[n + i][...], NN)
        _rms_bwd_tail(dh, *refs[2 * n:2 * n + 6], pl.program_id(0))

    row = pl.BlockSpec((tm, d), lambda m: (m, 0))
    gain = pl.BlockSpec((1, d), lambda m: (0, 0))
    act_specs = [pl.BlockSpec((tm, a.shape[1]), lambda m: (m, 0)) for a, _, _ in parts]
    w_specs = [pl.BlockSpec((a.shape[1], d), functools.partial(lambda m, blk: (blk, 0), blk=blk)) for a, _, blk in parts]
    return pl.pallas_call(
        _ordered_after(body, 2 * n + 4, len(after)), name=name, grid=(t // tm,),
        in_specs=act_specs + w_specs + [row, pl.BlockSpec((tm, 1), lambda m: (m, 0)), gain, row] + [_ANY] * len(after),
        out_specs=[row, gain], out_shape=[jax.ShapeDtypeStruct((t, d), F32), jax.ShapeDtypeStruct((1, d), F32)],
        compiler_params=_params(("arbitrary",)),
    )(*[a for a, _, _ in parts], *[w for _, w, _ in parts], x, rstd, g, dx_in, *after)


def ffn_forward(tag, x, g_norm, w3, normed=None):
    _, _, f4, d = w3.shape
    h, rstd = normed if normed is not None else rms_fwd(f"{tag}_rms", x, g_norm)
    a, b, s = ffn_up(f"{tag}_up", h, w3)
    x_out = mm_residual(f"{tag}_down", s, w3, pl.BlockSpec((w3.shape[0], None, f4, d), lambda m: (0, DOWN, 0, 0)), x, 0.5)
    return x_out, (x, h, rstd, a, b, s)


def ffn_backward_weights(tag, dx, saved, w3, after=()):
    x, h, rstd, a, b, s = saved
    da, db = ffn_bwd_act(f"{tag}_bwd_act", dx, w3, a, b, after)
    return ffn_wgrad(f"{tag}_wgrad", h, da, db, s, dx), (da, db)


def ffn_backward_input(tag, dx, saved, dab, g_norm, w3, after=()):
    x, h, rstd, a, b, s = saved
    return ffn_dh(f"{tag}_dh", dab[0], dab[1], w3, x, rstd, g_norm, dx, after)


def proj(name, h, wcat_t, bias, first_col, n_cols, tn, out_dtype, scaled_tiles=()):
    t, d = h.shape
    tm = _row_tile(t, 512)
    off = first_col // tn

    def epilogue(acc, ex, outs, ids):
        val = acc + ex[0][...]
        if scaled_tiles:
            hit = functools.reduce(jnp.logical_or, [ids[0] == s for s in scaled_tiles])
            val = val * jnp.where(hit, ATTN_SCALE, 1.0)
        outs[0][...] = val.astype(out_dtype)

    return _mm(
        name, [(h, wcat_t)], NT, (n_cols // tn, t // tm, 1),
        [(pl.BlockSpec((tm, d), lambda j, m, k: (m, 0)), pl.BlockSpec((tn, d), lambda j, m, k: (off + j, 0)))],
        [jax.ShapeDtypeStruct((t, n_cols), out_dtype)], [pl.BlockSpec((tm, tn), lambda j, m, k: (m, j))], (tm, tn), 1, epilogue,
        extras=[bias], extra_specs=[pl.BlockSpec((1, tn), lambda j, m, k: (0, off + j))],
    )[0]


def mix_fwd(name, ya, yb, wup, pc):
    t, w = ya.shape
    ns, _, tn = wup.shape
    d = ns * tn
    tm = _row_tile(t, 512)

    def body(ya_ref, yb_ref, wa_ref, wb_ref, pa_ref, pb_ref, ua_ref, ub_ref, mx_ref):
        ua = _dot(ya_ref[...], wa_ref[...], NN)
        ub = _dot(yb_ref[...], wb_ref[...], NN)
        ua_ref[...] = ua
        ub_ref[...] = ub
        mx_ref[...] = (_sigmoid(pa_ref[...]) * ua + _sigmoid(pb_ref[...]) * ub).astype(BF16)

    y_spec = pl.BlockSpec((tm, w), lambda m, n: (m, 0))
    o_spec = pl.BlockSpec((tm, tn), lambda m, n: (m, n))
    return pl.pallas_call(
        body, name=name, grid=(t // tm, ns),
        in_specs=[y_spec, y_spec, pl.BlockSpec((None, w, tn), lambda m, n: (n, 0, 0)), pl.BlockSpec((None, w, tn), lambda m, n: (n, 1, 0)),
                  o_spec, pl.BlockSpec((tm, tn), lambda m, n: (m, ns + n))],
        out_specs=[o_spec, o_spec, o_spec],
        out_shape=[jax.ShapeDtypeStruct((t, d), F32), jax.ShapeDtypeStruct((t, d), F32), jax.ShapeDtypeStruct((t, d), BF16)],
        compiler_params=_params(("parallel", "parallel")),
    )(ya, yb, wup, wup, pc, pc)


def up_bwd(name, du, wup, branch):
    t, d = du.shape
    ns, w2, tn = wup.shape
    w = w2 // 2
    tm = _row_tile(t, 512)

    def body(du_ref, w_ref, o_ref):
        acc = _dot(du_ref[:, 0:tn], w_ref[0], NT)
        for j in range(1, ns):
            acc += _dot(du_ref[:, j * tn:(j + 1) * tn], w_ref[j], NT)
        o_ref[...] = acc.astype(BF16)

    return pl.pallas_call(
        body, name=name, grid=(t // tm,),
        in_specs=[pl.BlockSpec((tm, d), lambda m: (m, 0)), pl.BlockSpec((ns, w, tn), lambda m: (0, branch, 0))],
        out_specs=pl.BlockSpec((tm, w), lambda m: (m, 0)), out_shape=jax.ShapeDtypeStruct((t, w), BF16),
        compiler_params=_params(("parallel",)),
    )(du, wup)


def up_wgrad(name, ya, yb, dua, dub):
    t, w = ya.shape
    d = dua.shape[1]
    tn = d // N_CHIPS

    def body(ya_ref, yb_ref, dua_ref, dub_ref, o_ref):
        o_ref[0:w, :] = _dot(ya_ref[...], dua_ref[...], TN).astype(BF16)
        o_ref[w:2 * w, :] = _dot(yb_ref[...], dub_ref[...], TN).astype(BF16)

    y_spec = pl.BlockSpec((t, w), lambda j: (0, 0))
    du_spec = pl.BlockSpec((t, tn), lambda j: (0, j))
    return pl.pallas_call(
        body, name=name, grid=(N_CHIPS,), in_specs=[y_spec, y_spec, du_spec, du_spec],
        out_specs=pl.BlockSpec((None, 2 * w, tn), lambda j: (j, 0, 0)),
        out_shape=jax.ShapeDtypeStruct((N_CHIPS, 2 * w, tn), BF16), compiler_params=_params(("parallel",)),
    )(ya, yb, dua, dub)


def mix_bwd(name, dx, wo, pc, ua, ub, after=()):
    t, d = dx.shape
    tm = _row_tile(t, 512)
    tn = 512
    off_a = 0
    off_b = d // tn

    def body(dx_ref, wo_ref, pa_ref, pb_ref, ua_ref, ub_ref, dua_ref, dub_ref, dpa_ref, dpb_ref, ba_ref, bb_ref):
        dm = _dot(dx_ref[...], wo_ref[...], NT)
        ga = _sigmoid(pa_ref[...])
        gb = _sigmoid(pb_ref[...])
        dua_ref[...] = (dm * ga).astype(BF16)
        dub_ref[...] = (dm * gb).astype(BF16)
        dpa = dm * ua_ref[...] * ga * (1.0 - ga)
        dpb = dm * ub_ref[...] * gb * (1.0 - gb)
        dpa_ref[...] = dpa.astype(BF16)
        dpb_ref[...] = dpb.astype(BF16)
        sa = jnp.sum(dpa, axis=0, keepdims=True)
        sb = jnp.sum(dpb, axis=0, keepdims=True)

        @pl.when(pl.program_id(1) == 0)
        def _():
            ba_ref[...] = sa
            bb_ref[...] = sb

        @pl.when(pl.program_id(1) != 0)
        def _():
            ba_ref[...] += sa
            bb_ref[...] += sb

    tile = pl.BlockSpec((tm, tn), lambda n, m: (m, n))
    bias = pl.BlockSpec((1, tn), lambda n, m: (0, n))
    return pl.pallas_call(
        _ordered_after(body, 6, len(after)), name=name, grid=(d // tn, t // tm),
        in_specs=[pl.BlockSpec((tm, d), lambda n, m: (m, 0)), pl.BlockSpec((tn, d), lambda n, m: (n, 0)),
                  pl.BlockSpec((tm, tn), lambda n, m: (m, off_a + n)), pl.BlockSpec((tm, tn), lambda n, m: (m, off_b + n)),
                  tile, tile] + [_ANY] * len(after),
        out_specs=[tile, tile, tile, tile, bias, bias],
        out_shape=[jax.ShapeDtypeStruct((t, d), BF16)] * 4 + [jax.ShapeDtypeStruct((1, d), F32)] * 2,
        compiler_params=_params(("parallel", "arbitrary")),
    )(dx, wo, pc, pc, ua, ub, *after)


def mm_plain(name, a, b, contract, out_dtype, tk_target=512):
    if contract == NN:
        m, kdim = a.shape
        n = b.shape[1]
    elif contract == NT:
        m, kdim = a.shape
        n = b.shape[0]
    else:
        kdim, m = a.shape
        n = b.shape[1]
    tm = _row_tile(m, 512)
    tk = _row_tile(kdim, tk_target)
    nk = kdim // tk
    if contract == TN:
        a_spec = pl.BlockSpec((tk, tm), lambda i, k: (k, i))
    else:
        a_spec = pl.BlockSpec((tm, tk), lambda i, k: (i, k))
    if contract == NT:
        b_spec = pl.BlockSpec((n, tk), lambda i, k: (0, k))
    else:
        b_spec = pl.BlockSpec((tk, n), lambda i, k: (k, 0))

    def epilogue(acc, ex, outs, ids):
        outs[0][...] = acc.astype(out_dtype)

    return _mm(name, [(a, b)], contract, (m // tm, nk), [(a_spec, b_spec)],
               [jax.ShapeDtypeStruct((m, n), out_dtype)], [pl.BlockSpec((tm, n), lambda i, k: (i, 0))], (tm, n), nk, epilogue)[0]


def wgrad_cat(name, h, dcat):
    t, d = h.shape
    n = dcat.shape[1]
    tn = next(c for c in (768, 512, 256, 128) if n % c == 0)
    tk = _row_tile(t, 2048)

    def epilogue(acc, ex, outs, ids):
        outs[0][...] = acc.astype(BF16)

    return _mm(
        name, [(dcat, h)], TN, (n // tn, t // tk),
        [(pl.BlockSpec((tk, tn), lambda j, k: (k, j)), pl.BlockSpec((tk, d), lambda j, k: (k, 0)))],
        [jax.ShapeDtypeStruct((n, d), BF16)], [pl.BlockSpec((tn, d), lambda j, k: (j, 0))], (tn, d), t // tk, epilogue,
    )[0]


def fox_prep(name, f, bias):
    t, lanes = f.shape
    nchunk = t // QB

    def body(f_ref, b_ref, c_ref):
        lower = (_iota2((QB, QB), 1) <= _iota2((QB, QB), 0)).astype(BF16)

        def chunk(n, carry):
            rows = pl.ds(pl.multiple_of(n * QB, QB), QB)
            u = f_ref[rows, :] + b_ref[...]
            lf = jnp.minimum(u, 0.0) - _log1pexp_neg_abs(u)
            c = _dot_exact_lhs01(lower, lf) + carry
            c_ref[rows, :] = c
            return c[QB - 1:QB, :]

        lax.fori_loop(0, nchunk, chunk, jnp.zeros((1, lanes), F32))

    return pl.pallas_call(body, name=name, out_shape=jax.ShapeDtypeStruct((t, lanes), F32),
                          compiler_params=pltpu.CompilerParams(vmem_limit_bytes=VMEM_LIMIT_BYTES))(f, bias)


def fox_gate_bwd(name, dc, f, bias):
    t, lanes = dc.shape
    nchunk = t // QB

    def body(dc_ref, f_ref, b_ref, df_ref, gb_ref):
        upper = (_iota2((QB, QB), 1) >= _iota2((QB, QB), 0)).astype(BF16)

        def chunk(n, carry):
            tail, total = carry
            rows = pl.ds(pl.multiple_of((nchunk - 1 - n) * QB, QB), QB)
            dlf = _dot_exact_lhs01(upper, dc_ref[rows, :]) + tail
            u = f_ref[rows, :] + b_ref[...]
            df = dlf * jnp.exp(jnp.minimum(-u, 0.0) - _log1pexp_neg_abs(u))
            df_ref[rows, :] = df
            return dlf[0:1, :], total + jnp.sum(df, axis=0, keepdims=True)

        zero = jnp.zeros((1, lanes), F32)
        _, total = lax.fori_loop(0, nchunk, chunk, (zero, zero))
        gb_ref[...] = total

    return pl.pallas_call(body, name=name,
                          out_shape=[jax.ShapeDtypeStruct((t, lanes), F32), jax.ShapeDtypeStruct((1, lanes), F32)],
                          compiler_params=pltpu.CompilerParams(vmem_limit_bytes=VMEM_LIMIT_BYTES))(dc, f, bias)


def _qrows(i):
    return pl.ds(pl.multiple_of(i * BQ, BQ), BQ)


def _krows(kc):
    return pl.ds(pl.multiple_of(kc * CS, CS), CS)


def _dot_split2_rhs01(x, m01):
    hi = x.astype(BF16)
    lo = (x - hi.astype(F32)).astype(BF16)
    d = lambda p: lax.dot_general(p, m01, (NN, ((), ())), preferred_element_type=F32)
    return d(hi) + d(lo)


def _live_rows(dchunk):
    return 0 if dchunk is None else dchunk * CS


def _diag_mask(dchunk, inclusive):
    shape = (BQ - _live_rows(dchunk), CS)
    return _iota2(shape, 1) <= _iota2(shape, 0) if inclusive else _iota2(shape, 1) < _iota2(shape, 0)


def _tail(x, r0, axis=0):
    return x if r0 == 0 else (x[r0:] if axis == 0 else x[:, r0:])


def _with_tail(old, tail, r0):
    return tail if r0 == 0 else jnp.concatenate([old[:r0], tail], axis=0)


def _walk_chunks(i, step, init, right_to_left):
    order = list(reversed(range(N_SUB))) if right_to_left else list(range(N_SUB))

    def diagonal(state):
        for dchunk in order:
            state = step(i * N_SUB + dchunk, state, dchunk)
        return state

    def group(n, state):
        base = ((i - 1 - n) if right_to_left else n) * N_SUB
        for dchunk in order:
            state = step(base + dchunk, state, None)
        return state

    if right_to_left:
        return lax.fori_loop(0, i, group, diagonal(init))
    return diagonal(lax.fori_loop(0, i, group, init))


PAIR = 2 * HEAD_DIM
N_PAIRS = N_HEADS // 2


def _pair_spec(t, first_block):
    return pl.BlockSpec((t, PAIR), lambda p, *_: (0, first_block + p))


def _head_lanes(shape):
    lane = _iota2(shape, len(shape) - 1)
    return [lane < HEAD_DIM, lane >= HEAD_DIM]


def _only_head(x, lanes_of_head):
    return jnp.where(lanes_of_head, x, jnp.zeros_like(x))


LOG2_E = 1.4426950408889634


def _sb_chunk_weights(q_h, k, later, carry, dchunk):
    z = _dot(q_h, k, NT) * LOG2_E
    lnb = -jnp.maximum(z, 0.0) - jnp.log2(1.0 + jnp.exp2(-jnp.abs(z)))
    lsz = lnb + z
    if dchunk is not None:
        lnb = jnp.where(_diag_mask(dchunk, False), lnb, 0.0)
    w = jnp.exp2(lsz + _dot_split2_rhs01(lnb, later) + carry)
    if dchunk is not None:
        w = jnp.where(_diag_mask(dchunk, False), w, 0.0)
    return w, lsz, lnb


STRIP = 256


def _sb_chunk_weights_staged(q_h, k, later, carry, dchunk, scratch):
    z_s, l_s, hi_s, lo_s = scratch
    n = q_h.shape[0]
    z_s[0:n, :] = _dot(q_h, k, NT) * LOG2_E
    sums = []
    for s0 in range(0, n, STRIP):
        rows = slice(s0, s0 + STRIP)
        z = z_s[rows, :]
        lnb = -jnp.maximum(z, 0.0) - jnp.log2(1.0 + jnp.exp2(-jnp.abs(z)))
        l_s[rows, :] = lnb + z
        if dchunk is not None and s0 < CS:
            lnb = jnp.where(_iota2((STRIP, CS), 1) < _iota2((STRIP, CS), 0) + s0, lnb, 0.0)
        hi = lnb.astype(BF16)
        hi_s[rows, :] = hi
        lo_s[rows, :] = (lnb - hi.astype(F32)).astype(BF16)
        sums.append(jnp.sum(lnb, axis=1, keepdims=True))
    d = lambda p: lax.dot_general(p, later, (NN, ((), ())), preferred_element_type=F32)
    z_s[0:n, :] = d(hi_s[0:n, :]) + d(lo_s[0:n, :])
    for s0 in range(0, n, STRIP):
        rows = slice(s0, s0 + STRIP)
        w = jnp.exp2(l_s[rows, :] + z_s[rows, :] + carry[rows, :])
        if dchunk is not None and s0 < CS:
            w = jnp.where(_iota2((STRIP, CS), 1) < _iota2((STRIP, CS), 0) + s0, w, 0.0)
        hi_s[rows, :] = w.astype(BF16)
    return hi_s[0:n, :], jnp.concatenate(sums, axis=0)


def sb_pair_fwd(name, qkv):
    t = qkv.shape[0]

    def body(q_ref, k_ref, v_ref, o_ref, z_s, l_s, hi_s, lo_s):
        later = (_iota2((CS, CS), 0) > _iota2((CS, CS), 1)).astype(BF16)
        lanes = _head_lanes((BQ, PAIR))

        def qblock(i, _):
            q = q_ref[_qrows(i), :]
            q_heads = [_only_head(q, lanes[h]) for h in range(2)]

            def step(kc, state, dchunk):
                k = k_ref[_krows(kc), :]
                v = v_ref[_krows(kc), :]
                out = []
                r0 = _live_rows(dchunk)
                for h in range(2):
                    carry, acc = state[h]
                    w, sums = _sb_chunk_weights_staged(_tail(q_heads[h], r0), k, later, _tail(carry, r0), dchunk,
                                                       (z_s.at[h], l_s.at[h], hi_s.at[h], lo_s.at[h]))
                    out.append((_with_tail(carry, _tail(carry, r0) + sums, r0),
                                _with_tail(acc, _tail(acc, r0) + _dot(w, v, NN), r0)))
                return tuple(out)

            zero = (jnp.zeros((BQ, 1), F32), jnp.zeros((BQ, PAIR), F32))
            (_, acc0), (_, acc1) = _walk_chunks(i, step, (zero, zero), True)
            o_ref[_qrows(i), :] = jnp.where(lanes[0], acc0, acc1).astype(BF16)
            return 0

        lax.fori_loop(0, t // BQ, qblock, 0)

    return pl.pallas_call(
        body, name=name, grid=(N_PAIRS,),
        in_specs=[_pair_spec(t, 0), _pair_spec(t, N_PAIRS), _pair_spec(t, 2 * N_PAIRS)],
        out_specs=_pair_spec(t, 0), out_shape=jax.ShapeDtypeStruct((t, WIDTH), BF16),
        scratch_shapes=[pltpu.VMEM((2, BQ, CS), F32), pltpu.VMEM((2, BQ, CS), F32),
                        pltpu.VMEM((2, BQ, CS), BF16), pltpu.VMEM((2, BQ, CS), BF16)],
        compiler_params=_params(("parallel",)),
    )(qkv, qkv, qkv)


def _emit_dqkv(res, o_ref):
    o_ref[...] = res[pl.program_id(1)]


def _flush_transposed(acc, res, which):
    for kc in range(acc.shape[0]):
        res[which, kc * CS:(kc + 1) * CS, :] = acc[kc].T.astype(BF16)


def sb_pair_bwd(name, qkv, dy, dqkv):
    t = qkv.shape[0]
    nc = t // CS

    def body(q_ref, k_ref, v_ref, do_ref, _, o_ref, g_s, b_s, dkt_acc, dvt_acc, res):
        @pl.when(pl.program_id(1) == 0)
        def _():
            later = (_iota2((CS, CS), 0) > _iota2((CS, CS), 1)).astype(BF16)
            earlier = (_iota2((CS, CS), 0) < _iota2((CS, CS), 1)).astype(BF16)
            lanes = _head_lanes((BQ, PAIR))
            dkt_acc[...] = jnp.zeros_like(dkt_acc)
            dvt_acc[...] = jnp.zeros_like(dvt_acc)

            def qblock(i, _):
                q = q_ref[_qrows(i), :]
                do = do_ref[_qrows(i), :]
                zero = jnp.zeros((BQ, 1), F32)
                dqs = []
                for h in range(2):
                    q_h = _only_head(q, lanes[h])
                    do_h = _only_head(do, lanes[h])
                    qt_h = q_h.astype(F32).T.astype(BF16)
                    dot_h = do_h.astype(F32).T.astype(BF16)

                    def step1(kc, carry, dchunk, q_h=q_h, do_h=do_h, dot_h=dot_h):
                        k = k_ref[_krows(kc), :]
                        v = v_ref[_krows(kc), :]
                        r0 = _live_rows(dchunk)
                        live = _tail(carry, r0)
                        w, lsz, lnb = _sb_chunk_weights(_tail(q_h, r0), k, later, live, dchunk)
                        g_s[kc, r0:, :] = (w * _dot(_tail(do_h, r0), v, NT)).astype(BF16)
                        b_s[kc, r0:, :] = jnp.exp2(lsz).astype(BF16)
                        dvt_acc[kc] += _dot(_tail(dot_h, r0, axis=1), w, NN)
                        return _with_tail(carry, live + jnp.sum(lnb, axis=1, keepdims=True), r0)

                    _walk_chunks(i, step1, zero, True)

                    def step2(kc, state, dchunk, qt_h=qt_h):
                        k = k_ref[_krows(kc), :]
                        r0 = _live_rows(dchunk)
                        before, dq = state
                        g16 = g_s[kc, r0:, :]
                        g = g16.astype(F32)
                        beta = b_s[kc, r0:, :].astype(F32)
                        prefix = lax.dot_general(g16, earlier, (NN, ((), ())), preferred_element_type=F32) + _tail(before, r0)
                        dz = g * (1.0 - beta) - beta * prefix
                        if dchunk is not None:
                            dz = jnp.where(_diag_mask(dchunk, False), dz, 0.0)
                        dzb = dz.astype(BF16)
                        dkt_acc[kc] += _dot(_tail(qt_h, r0, axis=1), dzb, NN)
                        return (_with_tail(before, _tail(before, r0) + jnp.sum(g, axis=1, keepdims=True), r0),
                                _with_tail(dq, _tail(dq, r0) + _dot(dzb, k, NN), r0))

                    dqs.append(_walk_chunks(i, step2, (zero, jnp.zeros((BQ, PAIR), F32)), False)[1])
                res[0, _qrows(i), :] = (jnp.where(lanes[0], dqs[0], dqs[1]) * ATTN_SCALE).astype(BF16)
                return 0

            lax.fori_loop(0, t // BQ, qblock, 0)
            _flush_transposed(dkt_acc, res, 1)
            _flush_transposed(dvt_acc, res, 2)

        _emit_dqkv(res, o_ref)

    return pl.pallas_call(
        body, name=name, grid=(N_PAIRS, 3),
        in_specs=[_pair_spec(t, 0), _pair_spec(t, N_PAIRS), _pair_spec(t, 2 * N_PAIRS), _pair_spec(t, 0), _ANY],
        out_specs=pl.BlockSpec((t, PAIR), lambda p, s: (0, s * N_PAIRS + p)),
        out_shape=jax.ShapeDtypeStruct(dqkv.shape, BF16), input_output_aliases={4: 0},
        scratch_shapes=[pltpu.VMEM((nc, BQ, CS), BF16), pltpu.VMEM((nc, BQ, CS), BF16),
                        pltpu.VMEM((nc, PAIR, CS), F32), pltpu.VMEM((nc, PAIR, CS), F32), pltpu.VMEM((3, t, PAIR), BF16)],
        compiler_params=_params(("parallel", "arbitrary")),
    )(qkv, qkv, qkv, dy, dqkv)


def _gates_col_spec(t):
    return pl.BlockSpec((2, t, 1), lambda p, *_: (p, 0, 0))


def _gates_row_spec(nc):
    return pl.BlockSpec((2, nc, 1, CS), lambda p, *_: (p, 0, 0, 0))


def fox_pair_fwd(name, qkv, c_col, c_row):
    t = qkv.shape[0]

    def body(q_ref, k_ref, v_ref, cc_ref, cr_ref, o_ref, lse_ref):
        lanes = _head_lanes((BQ, PAIR))

        def qblock(i, _):
            q = q_ref[_qrows(i), :]
            q_heads = [_only_head(q, lanes[h]) for h in range(2)]
            ct = [cc_ref[h, _qrows(i), :] for h in range(2)]

            def step(kc, state, dchunk):
                k = k_ref[_krows(kc), :]
                v = v_ref[_krows(kc), :]
                out = []
                r0 = _live_rows(dchunk)
                for h in range(2):
                    m, l, acc = (_tail(a, r0) for a in state[h])
                    s = _dot(_tail(q_heads[h], r0), k, NT) + _tail(ct[h], r0) - cr_ref[h, kc]
                    if dchunk is not None:
                        s = jnp.where(_diag_mask(dchunk, True), s, NEG_BIG)
                    m_new = jnp.maximum(m, jnp.max(s, axis=1, keepdims=True))
                    alpha = jnp.exp(m - m_new)
                    p = jnp.exp(s - m_new)
                    if dchunk is not None:
                        p = jnp.where(_diag_mask(dchunk, True), p, 0.0)
                    new = (m_new, l * alpha + jnp.sum(p, axis=1, keepdims=True), acc * alpha + _dot(p, v, NN))
                    out.append(tuple(_with_tail(old, tail, r0) for old, tail in zip(state[h], new)))
                return tuple(out)

            init = (jnp.full((BQ, 1), NEG_BIG, F32), jnp.zeros((BQ, 1), F32), jnp.zeros((BQ, PAIR), F32))
            (m0, l0, acc0), (m1, l1, acc1) = _walk_chunks(i, step, (init, init), False)
            o_ref[_qrows(i), :] = jnp.where(lanes[0], acc0 / l0, acc1 / l1).astype(BF16)
            lse_ref[0, _qrows(i), :] = m0 + jnp.log(l0)
            lse_ref[1, _qrows(i), :] = m1 + jnp.log(l1)
            return 0

        lax.fori_loop(0, t // BQ, qblock, 0)

    return pl.pallas_call(
        body, name=name, grid=(N_PAIRS,),
        in_specs=[_pair_spec(t, 3 * N_PAIRS), _pair_spec(t, 4 * N_PAIRS), _pair_spec(t, 5 * N_PAIRS),
                  _gates_col_spec(t), _gates_row_spec(t // CS)],
        out_specs=[_pair_spec(t, 0), _gates_col_spec(t)],
        out_shape=[jax.ShapeDtypeStruct((t, WIDTH), BF16), jax.ShapeDtypeStruct((N_HEADS, t, 1), F32)],
        compiler_params=_params(("parallel",)),
    )(qkv, qkv, qkv, c_col, c_row)


def fox_pair_bwd(name, qkv, y, dy, lse, c_col, c_row, dqkv):
    t = qkv.shape[0]
    nc = t // CS

    def body(q_ref, k_ref, v_ref, o_in_ref, do_ref, lse_ref, cc_ref, cr_ref, _, o_ref, dcc_ref, dcr_ref,
             dkt_acc, dvt_acc, dcr_acc, res):
        @pl.when(pl.program_id(1) == 0)
        def _():
            lanes = _head_lanes((BQ, PAIR))
            dkt_acc[...] = jnp.zeros_like(dkt_acc)
            dvt_acc[...] = jnp.zeros_like(dvt_acc)
            dcr_acc[...] = jnp.zeros_like(dcr_acc)

            def qblock(i, _):
                q = q_ref[_qrows(i), :]
                do = do_ref[_qrows(i), :]
                q_heads = [_only_head(q, lanes[h]) for h in range(2)]
                do_heads = [_only_head(do, lanes[h]) for h in range(2)]
                qt_heads = [qh.astype(F32).T.astype(BF16) for qh in q_heads]
                dot_heads = [dh.astype(F32).T.astype(BF16) for dh in do_heads]
                prod = do.astype(F32) * o_in_ref[_qrows(i), :].astype(F32)
                delta = [jnp.sum(_only_head(prod, lanes[h]), axis=1, keepdims=True) for h in range(2)]
                ct = [cc_ref[h, _qrows(i), :] for h in range(2)]
                lse_i = [lse_ref[h, _qrows(i), :] for h in range(2)]

                def step(kc, state, dchunk):
                    k = k_ref[_krows(kc), :]
                    v = v_ref[_krows(kc), :]
                    out = []
                    r0 = _live_rows(dchunk)
                    for h in range(2):
                        dq, dct = state[h]
                        s = _dot(_tail(q_heads[h], r0), k, NT) + _tail(ct[h], r0) - cr_ref[h, kc]
                        p = jnp.exp(s - _tail(lse_i[h], r0))
                        if dchunk is not None:
                            p = jnp.where(_diag_mask(dchunk, True), p, 0.0)
                        ds = p * (_dot(_tail(do_heads[h], r0), v, NT) - _tail(delta[h], r0))
                        dvt_acc[kc] += _dot(_tail(dot_heads[h], r0, axis=1), p, NN)
                        dsb = ds.astype(BF16)
                        dkt_acc[kc] += _dot(_tail(qt_heads[h], r0, axis=1), dsb, NN)
                        dcr_acc[h, kc] -= jnp.sum(ds, axis=0, keepdims=True)
                        out.append((_with_tail(dq, _tail(dq, r0) + _dot(dsb, k, NN), r0),
                                    _with_tail(dct, _tail(dct, r0) + jnp.sum(ds, axis=1, keepdims=True), r0)))
                    return tuple(out)

                zero = (jnp.zeros((BQ, PAIR), F32), jnp.zeros((BQ, 1), F32))
                (dq0, dct0), (dq1, dct1) = _walk_chunks(i, step, (zero, zero), False)
                res[0, _qrows(i), :] = (jnp.where(lanes[0], dq0, dq1) * ATTN_SCALE).astype(BF16)
                lane = _iota2((BQ, PAIR), 1)
                dcc_ref[_qrows(i), :] = jnp.where(lane == 0, dct0, jnp.where(lane == 1, dct1, 0.0))
                return 0

            lax.fori_loop(0, t // BQ, qblock, 0)
            _flush_transposed(dkt_acc, res, 1)
            _flush_transposed(dvt_acc, res, 2)
            dcr_ref[...] = dcr_acc[...]

        _emit_dqkv(res, o_ref)

    return pl.pallas_call(
        body, name=name, grid=(N_PAIRS, 3),
        in_specs=[_pair_spec(t, 3 * N_PAIRS), _pair_spec(t, 4 * N_PAIRS), _pair_spec(t, 5 * N_PAIRS), _pair_spec(t, 0),
                  _pair_spec(t, 0), _gates_col_spec(t), _gates_col_spec(t), _gates_row_spec(nc), _ANY],
        out_specs=[pl.BlockSpec((t, PAIR), lambda p, s: (0, (3 + s) * N_PAIRS + p)),
                   pl.BlockSpec((None, t, PAIR), lambda p, s: (p, 0, 0)), _gates_row_spec(nc)],
        out_shape=[jax.ShapeDtypeStruct(dqkv.shape, BF16), jax.ShapeDtypeStruct((N_PAIRS, t, PAIR), F32),
                   jax.ShapeDtypeStruct((N_HEADS, nc, 1, CS), F32)],
        input_output_aliases={8: 0},
        scratch_shapes=[pltpu.VMEM((nc, PAIR, CS), F32), pltpu.VMEM((nc, PAIR, CS), F32), pltpu.VMEM((2, nc, 1, CS), F32),
                        pltpu.VMEM((3, t, PAIR), BF16)],
        compiler_params=_params(("parallel", "arbitrary")),
    )(qkv, qkv, qkv, y, dy, lse, c_col, c_row, dqkv)


def loss_head(name, x, g, target):
    t, d = x.shape
    tr = _row_tile(t, 256)

    def body(x_ref, g_ref, t_ref, dx_ref, gn_ref, loss_ref):
        xv = x_ref[...]
        r = lax.rsqrt(jnp.mean(xv * xv, axis=-1, keepdims=True) + RMS_EPS)
        xhat = xv * r
        gv = g_ref[...]
        err = xhat * gv - t_ref[...]
        part_loss = 0.5 * jnp.sum(jnp.mean(err * err, axis=-1, keepdims=True), axis=0, keepdims=True)
        dy = err * (1.0 / d)
        dyg = dy * gv
        dx_ref[...] = r * (dyg - xhat * jnp.mean(dyg * xhat, axis=-1, keepdims=True))
        part_g = jnp.sum(dy * xhat, axis=0, keepdims=True)

        @pl.when(pl.program_id(0) == 0)
        def _():
            gn_ref[...] = part_g
            loss_ref[...] = part_loss

        @pl.when(pl.program_id(0) != 0)
        def _():
            gn_ref[...] += part_g
            loss_ref[...] += part_loss

    row = pl.BlockSpec((tr, d), lambda i: (i, 0))
    return pl.pallas_call(
        body, name=name, grid=(t // tr,),
        in_specs=[row, pl.BlockSpec((1, d), lambda i: (0, 0)), row],
        out_specs=[row, pl.BlockSpec((1, d), lambda i: (0, 0)), pl.BlockSpec((1, 1), lambda i: (0, 0))],
        out_shape=[jax.ShapeDtypeStruct((t, d), F32), jax.ShapeDtypeStruct((1, d), F32), jax.ShapeDtypeStruct((1, 1), F32)],
        compiler_params=_params(("arbitrary",)),
    )(x, g, target)


def _place():
    return lax.axis_index("x"), lax.axis_index("y"), lax.axis_index("c")


def _other_chips(x, y):
    return [(1 - x, y), (x, 1 - y), (1 - x, 1 - y)]


def _half(ref, c, rows):
    if rows % 32 == 0:
        return ref.at[:, pl.ds(c * (rows // 2), rows // 2), :]
    cols = ref.shape[2]
    return ref.at[:, :, pl.ds(c * (cols // 2), cols // 2)]


_ANY = pl.BlockSpec(memory_space=pl.ANY)


_HBM = pl.BlockSpec(memory_space=pltpu.HBM)
_SEM = pl.BlockSpec(memory_space=pltpu.SEMAPHORE)
_DATAFLOW = pltpu.SideEffectType.DATAFLOW_SIDE_EFFECTING


def _in_hbm(a):
    return pltpu.with_memory_space_constraint(a, pltpu.HBM)


def _gather_ici_copies(bufs, send_sems, recv_sems, arrivals):
    x, y, c = _place()
    me = 2 * x + y
    copies = []
    for i, buf in enumerate(bufs):
        rows = buf.shape[2]
        for j, (qx, qy) in enumerate(_other_chips(x, y)):
            block = _half(buf.at[2 * qx + qy if arrivals else me], c, rows)
            copies.append(pltpu.make_async_remote_copy(
                src_ref=block, dst_ref=block, send_sem=send_sems.at[3 * i + j], recv_sem=recv_sems.at[3 * i + j],
                device_id=(qx, qy, c), device_id_type=MESH))
    return copies


def gather_ici_start(name, bufs, after):
    n = len(bufs)

    def body(*refs):
        ins = refs[:n]
        send_sems, recv_sems = refs[n + 1], refs[n + 2]
        token = refs[-1]
        for send in _gather_ici_copies(ins, send_sems, recv_sems, False):
            send.start()
        token[...] = jnp.zeros_like(token)

    res = pl.pallas_call(
        body, name=name,
        out_shape=(pltpu.SemaphoreType.DMA((3 * n,)), pltpu.SemaphoreType.DMA((3 * n,)), *[pltpu.HBM(b.shape, b.dtype) for b in bufs],
                   jax.ShapeDtypeStruct((8, 128), F32)),
        in_specs=[_HBM] * n + [_ANY], out_specs=(_SEM, _SEM, *[_HBM] * n, pl.BlockSpec(memory_space=pltpu.VMEM)),
        input_output_aliases={i: 2 + i for i in range(n)},
        compiler_params=pltpu.CompilerParams(has_side_effects=_DATAFLOW),
    )(*[_in_hbm(b) for b in bufs], after)
    return res[0], res[1], list(res[2:2 + n]), res[-1]


def gather_ici_wait(name, send_sems, recv_sems, bufs, after):
    n = len(bufs)

    def body(*refs):
        ins = refs[:n]
        send_sems_ref, recv_sems_ref = refs[n], refs[n + 1]
        for send in _gather_ici_copies(ins, send_sems_ref, recv_sems_ref, False):
            send.wait_send()
        for recv in _gather_ici_copies(ins, send_sems_ref, recv_sems_ref, True):
            recv.wait_recv()

    return pl.pallas_call(
        body, name=name, out_shape=tuple(pltpu.HBM(b.shape, b.dtype) for b in bufs),
        in_specs=[_HBM] * n + [_SEM, _SEM, _ANY], out_specs=tuple([_HBM] * n),
        input_output_aliases={i: i for i in range(n)},
        compiler_params=pltpu.CompilerParams(has_side_effects=_DATAFLOW),
    )(*bufs, send_sems, recv_sems, after)


def gather_forward(name, bufs):
    n = len(bufs)

    def body(*refs):
        outs = refs[n:2 * n]
        send_sems, recv_sems = refs[2 * n:]
        x, y, c = _place()
        sibling = (x, y, 1 - c)
        sends = []
        for i in range(n):
            rows = outs[i].shape[2]
            for j, (qx, qy) in enumerate(_other_chips(x, y)):
                block = _half(outs[i].at[2 * qx + qy], c, rows)
                fw = pltpu.make_async_remote_copy(
                    src_ref=block, dst_ref=block, send_sem=send_sems.at[3 * i + j], recv_sem=recv_sems.at[3 * i + j],
                    device_id=sibling, device_id_type=MESH)
                fw.start()
                sends.append(fw)
        for i in range(n):
            rows = outs[i].shape[2]
            for j, (qx, qy) in enumerate(_other_chips(x, y)):
                block = _half(outs[i].at[2 * qx + qy], 1 - c, rows)
                pltpu.make_async_remote_copy(
                    src_ref=block, dst_ref=block, send_sem=send_sems.at[3 * i + j], recv_sem=recv_sems.at[3 * i + j],
                    device_id=sibling, device_id_type=MESH).wait_recv()
        for fw in sends:
            fw.wait_send()

    return pl.pallas_call(
        body, name=name, in_specs=[_ANY] * n, out_specs=[_ANY] * n,
        out_shape=[jax.ShapeDtypeStruct(b.shape, b.dtype) for b in bufs],
        input_output_aliases={i: i for i in range(n)},
        scratch_shapes=[pltpu.SemaphoreType.DMA((3 * n,)), pltpu.SemaphoreType.DMA((3 * n,))],
        compiler_params=pltpu.CompilerParams(has_side_effects=True),
    )(*bufs)


def _between_chips_copies(parts, lands, send_sems, recv_sems):
    x, y, c = _place()
    copies = []
    for i, (part, land) in enumerate(zip(parts, lands)):
        for j, (qx, qy) in enumerate(_other_chips(x, y)):
            copies.append(pltpu.make_async_remote_copy(
                src_ref=part.at[2 * qx + qy], dst_ref=land.at[j], send_sem=send_sems.at[3 * i + j], recv_sem=recv_sems.at[3 * i + j],
                device_id=(qx, qy, c), device_id_type=MESH))
    return copies


def between_chips_start(name, parts):
    n = len(parts)
    lands = [lax.empty((N_CHIPS - 1,) + p.shape[1:], p.dtype) for p in parts]

    def body(*refs):
        send_sems, recv_sems = refs[2 * n], refs[2 * n + 1]
        token = refs[-1]
        for cp in _between_chips_copies(refs[:n], refs[n:2 * n], send_sems, recv_sems):
            cp.start()
        token[...] = jnp.zeros_like(token)

    res = pl.pallas_call(
        body, name=name,
        out_shape=(pltpu.SemaphoreType.DMA((3 * n,)), pltpu.SemaphoreType.DMA((3 * n,)),
                   *[pltpu.HBM(a.shape, a.dtype) for a in parts + lands], jax.ShapeDtypeStruct((8, 128), F32)),
        in_specs=[_HBM] * (2 * n), out_specs=(_SEM, _SEM, *[_HBM] * (2 * n), pl.BlockSpec(memory_space=pltpu.VMEM)),
        input_output_aliases={i: 2 + i for i in range(2 * n)},
        compiler_params=pltpu.CompilerParams(has_side_effects=_DATAFLOW),
    )(*[_in_hbm(a) for a in parts + lands])
    return res[0], res[1], list(res[2:2 + n]), list(res[2 + n:2 + 2 * n]), res[-1]


def between_chips_wait(name, send_sems, recv_sems, parts, lands, after):
    n = len(parts)

    def body(*refs):
        for cp in _between_chips_copies(refs[:n], refs[n:2 * n], refs[2 * n], refs[2 * n + 1]):
            cp.wait_send()
            cp.wait_recv()

    res = pl.pallas_call(
        body, name=name, out_shape=tuple(pltpu.HBM(a.shape, a.dtype) for a in parts + lands),
        in_specs=[_HBM] * (2 * n) + [_SEM, _SEM, _ANY], out_specs=tuple([_HBM] * (2 * n)),
        input_output_aliases={i: i for i in range(2 * n)},
        compiler_params=pltpu.CompilerParams(has_side_effects=_DATAFLOW),
    )(*parts, *lands, send_sems, recv_sems, after)
    return list(res[:n]), list(res[n:])


def exchange_start(name, arrays, n_copies, copies, after=()):
    n = len(arrays)

    def body(*refs):
        send_sems, recv_sems = refs[n + len(after)], refs[n + len(after) + 1]
        for cp in copies(refs[:n], send_sems, recv_sems):
            cp.start()
        refs[-1][...] = jnp.zeros_like(refs[-1])

    res = pl.pallas_call(
        body, name=name,
        out_shape=(pltpu.SemaphoreType.DMA((n_copies,)), pltpu.SemaphoreType.DMA((n_copies,)),
                   *[pltpu.HBM(a.shape, a.dtype) for a in arrays], jax.ShapeDtypeStruct((8, 128), F32)),
        in_specs=[_HBM] * n + [_ANY] * len(after),
        out_specs=(_SEM, _SEM, *[_HBM] * n, pl.BlockSpec(memory_space=pltpu.VMEM)),
        input_output_aliases={i: 2 + i for i in range(n)},
        compiler_params=pltpu.CompilerParams(has_side_effects=_DATAFLOW),
    )(*[_in_hbm(a) for a in arrays], *after)
    return res[0], res[1], list(res[2:2 + n]), res[-1]


def exchange_wait(name, send_sems, recv_sems, arrays, copies, after):
    n = len(arrays)

    def body(*refs):
        for cp in copies(refs[:n], refs[n], refs[n + 1]):
            cp.wait_send()
            cp.wait_recv()

    return list(pl.pallas_call(
        body, name=name, out_shape=tuple(pltpu.HBM(a.shape, a.dtype) for a in arrays),
        in_specs=[_HBM] * n + [_SEM, _SEM, _ANY], out_specs=tuple([_HBM] * n),
        input_output_aliases={i: i for i in range(n)},
        compiler_params=pltpu.CompilerParams(has_side_effects=_DATAFLOW),
    )(*arrays, send_sems, recv_sems, after))


def _to_sibling_copies(n):
    def copies(refs, send_sems, recv_sems):
        x, y, c = _place()
        out = []
        for i in range(n):
            rows = refs[i].shape[2]
            out.append(pltpu.make_async_remote_copy(
                src_ref=refs[i].at[:, :, pl.ds((1 - c) * (rows // 2), rows // 2), :], dst_ref=refs[n + i],
                send_sem=send_sems.at[i], recv_sem=recv_sems.at[i], device_id=(x, y, 1 - c), device_id_type=MESH))
        return out
    return copies


def _share_copies(n):
    def copies(refs, send_sems, recv_sems):
        x, y, c = _place()
        out = []
        for i in range(n):
            mine = _half(refs[i], c, refs[i].shape[1])
            out.append(pltpu.make_async_remote_copy(
                src_ref=mine, dst_ref=mine, send_sem=send_sems.at[i], recv_sem=recv_sems.at[i],
                device_id=(x, y, 1 - c), device_id_type=MESH))
        return out
    return copies


def pair_sum(name, grad, recv, c):
    ns, na, rh, cols = recv.shape
    tr = _row_tile(rh, 256) if rh % 256 == 0 else rh
    nt = rh // tr

    def body(c_ref, g_ref, r_ref, o_ref):
        o_ref[...] = (g_ref[...].astype(F32) + r_ref[...].astype(F32)).astype(BF16)

    blk = (None, None, tr, cols)
    return pl.pallas_call(
        body, name=name,
        grid_spec=pltpu.PrefetchScalarGridSpec(
            num_scalar_prefetch=1, grid=(ns, na, nt),
            in_specs=[pl.BlockSpec(blk, lambda s, a, r, c_ref: (s, a, c_ref[0] * nt + r, 0)),
                      pl.BlockSpec(blk, lambda s, a, r, c_ref: (s, a, r, 0))],
            out_specs=pl.BlockSpec(blk, lambda s, a, r, c_ref: (s, a, r, 0))),
        out_shape=jax.ShapeDtypeStruct(recv.shape, BF16),
        compiler_params=_params(("parallel", "parallel", "parallel")),
    )(c, grad, recv)


def chip_sum(name, parts, landed, place):
    _, na, rh, cols = parts.shape
    tr = _row_tile(rh, 256) if rh % 256 == 0 else rh
    nt = rh // tr

    def body(place_ref, p_ref, l_ref, o_ref):
        total = p_ref[...].astype(F32)
        for s in range(N_CHIPS - 1):
            total = total + l_ref[s].astype(F32)
        o_ref[...] = total

    return pl.pallas_call(
        body, name=name,
        grid_spec=pltpu.PrefetchScalarGridSpec(
            num_scalar_prefetch=1, grid=(na, nt),
            in_specs=[pl.BlockSpec((None, None, tr, cols), lambda a, r, pr: (pr[1], a, r, 0)),
                      pl.BlockSpec((N_CHIPS - 1, None, tr, cols), lambda a, r, pr: (0, a, r, 0))],
            out_specs=pl.BlockSpec((None, tr, cols), lambda a, r, pr: (a, pr[0] * nt + r, 0))),
        out_shape=jax.ShapeDtypeStruct((na, 2 * rh, cols), F32),
        compiler_params=_params(("parallel", "parallel")),
    )(place, parts, landed)


def reduce_scatter_1(tag, grads):
    n = len(grads)
    lands = [lax.empty((g.shape[0], g.shape[1], g.shape[2] // 2, g.shape[3]), g.dtype) for g in grads]
    send_sems, recv_sems, arrays, token = exchange_start(f"{tag}_to_sibling_start", list(grads) + lands, n, _to_sibling_copies(n))
    return (send_sems, recv_sems, arrays), token


def reduce_scatter_2(tag, state, place, after):
    send_sems, recv_sems, arrays = state
    n = len(arrays) // 2
    arrays = exchange_wait(f"{tag}_to_sibling_wait", send_sems, recv_sems, arrays, _to_sibling_copies(n), after)
    parts = [pair_sum(f"{tag}_pair_sum{i}", g, r, place) for i, (g, r) in enumerate(zip(arrays[:n], arrays[n:]))]
    send_sems, recv_sems, parts, lands, token = between_chips_start(f"{tag}_between_chips_start", parts)
    return (send_sems, recv_sems, parts, lands), token


def reduce_scatter_3(tag, state, place, after):
    send_sems, recv_sems, parts, lands = state
    parts, landed = between_chips_wait(f"{tag}_between_chips_wait", send_sems, recv_sems, parts, lands, after)
    halves = [chip_sum(f"{tag}_chip_sum{i}", p, l, place) for i, (p, l) in enumerate(zip(parts, landed))]
    send_sems, recv_sems, halves, token = exchange_start(f"{tag}_share_start", halves, len(halves), _share_copies(len(halves)))
    return (send_sems, recv_sems, halves), token


def reduce_scatter_4(tag, state, after):
    send_sems, recv_sems, halves = state
    return exchange_wait(f"{tag}_share_wait", send_sems, recv_sems, halves, _share_copies(len(halves)), after)


def _adamw_math(w, g, m, v):
    m = ADAM_B1 * m + (1.0 - ADAM_B1) * g
    v = ADAM_B2 * v + (1.0 - ADAM_B2) * (g * g)
    m_hat = m / (1.0 - ADAM_B1 ** ADAM_STEP)
    v_hat = v / (1.0 - ADAM_B2 ** ADAM_STEP)
    delta = -ADAM_LR * (m_hat / (jnp.sqrt(v_hat) + ADAM_EPS) + ADAM_WD * w)
    return delta, m, v


def adamw(name, w, g, m, v, after=()):
    rows, cols = w.shape
    tr = _row_tile(rows, 256) if rows % 256 == 0 else rows // 2

    def body(w_ref, g_ref, m_ref, v_ref, d_ref, mo_ref, vo_ref):
        d_ref[...], mo_ref[...], vo_ref[...] = _adamw_math(w_ref[...], g_ref[...], m_ref[...], v_ref[...])

    blk = pl.BlockSpec((tr, cols), lambda i: (i, 0))
    return pl.pallas_call(
        _ordered_after(body, 4, len(after)), name=name, grid=(rows // tr,), in_specs=[blk] * 4 + [_ANY] * len(after),
        out_specs=[blk] * 3, out_shape=[jax.ShapeDtypeStruct(w.shape, F32)] * 3, compiler_params=_params(("parallel",)),
    )(w, g, m, v, *after)


def adamw_rows(name, w, g, m, v, after=()):
    rows, _, cols = w.shape
    tr = next(r for r in (128, 110, 64, 32, 16, 8, 1) if rows % r == 0)

    def body(w_ref, g_ref, m_ref, v_ref, d_ref, mo_ref, vo_ref):
        d_ref[...], mo_ref[...], vo_ref[...] = _adamw_math(w_ref[...], g_ref[...], m_ref[...], v_ref[...])

    blk = pl.BlockSpec((tr, 1, cols), lambda i: (i, 0, 0))
    return pl.pallas_call(
        _ordered_after(body, 4, len(after)), name=name, grid=(rows // tr,), in_specs=[blk] * 4 + [_ANY] * len(after),
        out_specs=[blk] * 3, out_shape=[jax.ShapeDtypeStruct(w.shape, F32)] * 3, compiler_params=_params(("parallel",)),
    )(w, g, m, v, *after)


def adamw_stacked(name, ws, g, ms, vs, after=()):
    n = len(ws)
    rows, cols = ws[0].shape
    tr = next(r for r in (128, 88, 64, 32, 16, 8) if rows % r == 0)

    def body(*refs):
        w_refs, m_refs, v_refs, g_ref = refs[:n], refs[n:2 * n], refs[2 * n:3 * n], refs[3 * n]
        outs = refs[3 * n + 1:]
        for i in range(n):
            outs[i][...], outs[n + i][...], outs[2 * n + i][...] = _adamw_math(
                w_refs[i][...], g_ref[i], m_refs[i][...], v_refs[i][...])

    blk = pl.BlockSpec((tr, cols), lambda r: (r, 0))
    res = pl.pallas_call(
        _ordered_after(body, 3 * n + 1, len(after)), name=name, grid=(rows // tr,),
        in_specs=[blk] * (3 * n) + [pl.BlockSpec((n, tr, cols), lambda r: (0, r, 0))] + [_ANY] * len(after),
        out_specs=[blk] * (3 * n),
        out_shape=[jax.ShapeDtypeStruct((rows, cols), F32)] * (3 * n), compiler_params=_params(("parallel",)),
    )(*ws, *ms, *vs, g, *after)
    return res[:n], res[n:2 * n], res[2 * n:]


def small_allreduce_adamw(name, g_part, w, m, v, after=()):
    rows, cols = g_part.shape

    def body(g_ref, w_ref, m_ref, v_ref, sum_ref, d_ref, mo_ref, vo_ref, land, send_sems, recv_sems):
        x, y, c = _place()
        me = 4 * x + 2 * y + c
        land[me] = g_ref[...]
        copies = []
        for r in range(1, 8):
            peer = (x ^ (r >> 2), y ^ ((r >> 1) & 1), c ^ (r & 1))
            rc = pltpu.make_async_remote_copy(
                src_ref=g_ref, dst_ref=land.at[me], send_sem=send_sems.at[r - 1], recv_sem=recv_sems.at[r - 1],
                device_id=peer, device_id_type=MESH)
            rc.start()
            copies.append(rc)
        for rc in copies:
            rc.wait()
        total = land[0]
        for s in range(1, 8):
            total = total + land[s]
        sum_ref[...] = total
        d_ref[...], mo_ref[...], vo_ref[...] = _adamw_math(w_ref[...], total, m_ref[...], v_ref[...])

    vmem = pl.BlockSpec(memory_space=pltpu.VMEM)
    return pl.pallas_call(
        _ordered_after(body, 4, len(after)), name=name, in_specs=[vmem] * 4 + [_ANY] * len(after), out_specs=[vmem] * 4,
        out_shape=[jax.ShapeDtypeStruct((rows, cols), F32)] * 4,
        scratch_shapes=[pltpu.VMEM((8, rows, cols), F32), pltpu.SemaphoreType.DMA((7,)), pltpu.SemaphoreType.DMA((7,))],
        compiler_params=pltpu.CompilerParams(has_side_effects=True),
    )(g_part, w, m, v, *after)


def kernel(x, norm_ffn1, w_ffn1_gate, w_ffn1_up, w_ffn1_down, norm_mix, w_in, b_forget, w_gate, b_gate, w_up_a, w_up_b, w_out, norm_ffn2, w_ffn2_gate, w_ffn2_up, w_ffn2_down, norm_final, loss_target, m_norm_ffn1, m_w_ffn1_gate, m_w_ffn1_up, m_w_ffn1_down, m_norm_mix, m_w_in, m_b_forget, m_w_gate, m_b_gate, m_w_up_a, m_w_up_b, m_w_out, m_norm_ffn2, m_w_ffn2_gate, m_w_ffn2_up, m_w_ffn2_down, m_norm_final, v_norm_ffn1, v_w_ffn1_gate, v_w_ffn1_up, v_w_ffn1_down, v_norm_mix, v_w_in, v_b_forget, v_w_gate, v_b_gate, v_w_up_a, v_w_up_b, v_w_out, v_norm_ffn2, v_w_ffn2_gate, v_w_ffn2_up, v_w_ffn2_down, v_norm_final):
    t, d = x.shape[1], x.shape[2]
    in4 = w_in.shape[2]
    gate4 = w_gate.shape[2]
    in_cols = N_CHIPS * in4
    n_forget = in_cols - QKV_COLS
    assert w_up_a.shape[1] == WIDTH and d == 2 * WIDTH and n_forget == N_HEADS and t % BQ == 0
    chip = 2 * lax.axis_index("x") + lax.axis_index("y")
    c_arr = jnp.stack([lax.axis_index("c"), chip]).astype(jnp.int32)
    x2d = x[0]
    target = loss_target[0]

    def slot(shard):
        return lax.dynamic_update_slice(lax.empty((N_CHIPS,) + shard.shape, BF16), shard.astype(BF16)[None], (chip, 0, 0, 0))

    def ffn_views(wg, wu, wd):
        return [wg[0].T, wu[0].T, wd[0]]

    ffn1_w, ffn1_m, ffn1_v = (ffn_views(w_ffn1_gate, w_ffn1_up, w_ffn1_down), ffn_views(m_w_ffn1_gate, m_w_ffn1_up, m_w_ffn1_down),
                              ffn_views(v_w_ffn1_gate, v_w_ffn1_up, v_w_ffn1_down))
    ffn2_w, ffn2_m, ffn2_v = (ffn_views(w_ffn2_gate, w_ffn2_up, w_ffn2_down), ffn_views(m_w_ffn2_gate, m_w_ffn2_up, m_w_ffn2_down),
                              ffn_views(v_w_ffn2_gate, v_w_ffn2_up, v_w_ffn2_down))
    in_pad = -(-in4 // 32) * 32
    in_sh = slot(w_in[0].T[None])
    gt_sh = slot(w_gate[0].T[None])
    up_sh = slot(jnp.concatenate([w_up_a[0], w_up_b[0]], axis=0)[None])
    wo_sh = slot(w_out)
    f1_send, f1_recv, f1_bufs, f1_token = gather_ici_start("gather_ffn1_start", [slot(jnp.stack(ffn1_w))], norm_ffn1)
    mx_send, mx_recv, mx_bufs, mx_token = gather_ici_start("gather_mixer_start", [in_sh, gt_sh, up_sh, wo_sh], f1_token)
    f2_send, f2_recv, f2_bufs, f2_token = gather_ici_start("gather_ffn2_start", [slot(jnp.stack(ffn2_w))], mx_token)

    normed1 = rms_fwd("ffn1_rms", x2d, norm_ffn1, after=(f2_token,))
    (w3_1,) = gather_forward("gather_ffn1_forward", gather_ici_wait("gather_ffn1_wait", f1_send, f1_recv, f1_bufs, normed1[0]))
    x1, saved1 = ffn_forward("ffn1", x2d, norm_ffn1, w3_1, normed=normed1)
    w_in_g, w_gate_g, wup, wo = gather_forward(
        "gather_mixer_forward", gather_ici_wait("gather_mixer_wait", mx_send, mx_recv, mx_bufs, x1))
    wup = wup[:, 0]
    w_in_t = w_in_g.reshape(in_cols, d)
    w_gate_t = w_gate_g.reshape(2 * d, d)
    w_f_t = jnp.pad(w_in_t[QKV_COLS:], ((0, QB - n_forget), (0, 0)))
    wo_full = wo.reshape(d, d)
    b_forget_row = jnp.pad(b_forget, ((0, 0), (0, QB - n_forget)))

    h2, rstd2 = rms_fwd("mix_rms", x1, norm_mix)
    qkv = proj("mix_proj_qkv", h2, w_in_t, jnp.zeros((1, QKV_COLS), F32), 0, QKV_COLS, WIDTH, BF16, scaled_tiles=(0, 3))
    pc = proj("mix_proj_gates", h2, w_gate_t, b_gate, 0, 2 * d, WIDTH, F32)
    f_logit = proj("mix_proj_forget", h2, w_f_t, jnp.zeros((1, QB), F32), 0, QB, QB, F32)
    c_cum = fox_prep("fox_prep", f_logit, b_forget_row)
    c_heads = c_cum[:, :N_HEADS].T
    c_col = c_heads[:, :, None]
    c_row = c_heads.reshape(N_HEADS, t // CS, 1, CS)
    ya = sb_pair_fwd("sb_fwd", qkv)
    yb, lse = fox_pair_fwd("fox_fwd", qkv, c_col, c_row)
    ua, ub, mixed = mix_fwd("mix_fwd", ya, yb, wup, pc)
    x2 = mm_residual("mix_out", mixed[None], wo_full[None], pl.BlockSpec((1, d, d), lambda m: (0, 0, 0)), x1, 1.0)
    (w3_2,) = gather_forward("gather_ffn2_forward", gather_ici_wait("gather_ffn2_wait", f2_send, f2_recv, f2_bufs, x2))
    x3, saved2 = ffn_forward("ffn2", x2, norm_ffn2, w3_2)
    dx3, gn_final, loss_part = loss_head("loss_head", x3, norm_final[None], target)

    g_w3_2, dab2 = ffn_backward_weights("ffn2", dx3, saved2, w3_2)
    rs_ffn2, token = reduce_scatter_1("rs_ffn2", [g_w3_2])
    dx2, gn_ffn2 = ffn_backward_input("ffn2", dx3, saved2, dab2, norm_ffn2, w3_2, after=(token,))
    rs_ffn2, rs_ffn2_token = reduce_scatter_2("rs_ffn2", rs_ffn2, c_arr, dx2)

    dua, dub, dpa, dpb, gba, gbb = mix_bwd("mix_bwd", dx2, wo_full, pc, ua, ub, after=(rs_ffn2_token,))
    g_bgate = jnp.concatenate([gba, gbb], axis=1)
    g_wo = mm_plain("wgrad_out", mixed, dx2, TN, BF16, tk_target=1024)
    dya = up_bwd("dya", dua, wup, 0)
    dyb = up_bwd("dyb", dub, wup, 1)
    g_up = up_wgrad("wgrad_up", ya, yb, dua, dub)
    dqkv = sb_pair_bwd("sb_bwd", qkv, dya, lax.empty((t, QKV_COLS), BF16))
    dqkv, dcc, dcr = fox_pair_bwd("fox_bwd", qkv, yb, dyb, lse, c_col, c_row, dqkv)
    dc = dcc[:, :, :2].transpose(1, 0, 2).reshape(t, N_HEADS) + dcr.reshape(N_HEADS, t).T
    df, g_bf = fox_gate_bwd("fox_gate_bwd", jnp.pad(dc, ((0, 0), (0, QB - N_HEADS))), f_logit, b_forget_row)
    df = df.astype(BF16)
    g_qkv_t = wgrad_cat("wgrad_qkv", h2, dqkv)
    g_f_t = wgrad_cat("wgrad_forget", h2, df)
    g_gate_t = jnp.stack([wgrad_cat("wgrad_gate_a", h2, dpa), wgrad_cat("wgrad_gate_b", h2, dpb)])
    g_in_t = jnp.concatenate([g_qkv_t, g_f_t[:n_forget]], axis=0).reshape(N_CHIPS, in4, d)
    rs_mixer, token = reduce_scatter_1(
        "rs_mixer", [jnp.pad(g_in_t, ((0, 0), (0, in_pad - in4), (0, 0)))[:, None], g_gate_t.reshape(N_CHIPS, 1, gate4, d),
                     g_up[:, None], g_wo.reshape(N_CHIPS, 1, d // N_CHIPS, d)])
    dx1, gn_mix = mixer_dh("mix_dh", [(dqkv, w_in_t, 0), (dpa, w_gate_t, 0), (dpb, w_gate_t, 1), (df, w_f_t, 0)],
                           x1, rstd2, norm_mix, dx2, after=(token,))
    rs_mixer, rs_mixer_token = reduce_scatter_2("rs_mixer", rs_mixer, c_arr, dx1)
    rs_ffn2, rs_ffn2_token = reduce_scatter_3("rs_ffn2", rs_ffn2, c_arr, dx1)

    big_out = {}

    def adamw_ffn(tag, s_w3, ws, ms, vs, after):
        deltas, new_ms, new_vs = adamw_stacked(f"adamw_{tag}", ws, s_w3, ms, vs, after)
        for which, part in ((GATE, "gate"), (UP, "up"), (DOWN, "down")):
            back = (lambda a: a[None]) if which == DOWN else (lambda a: a.T[None])
            big_out[f"w_{tag}_{part}"] = tuple(back(a) for a in (s_w3[which], deltas[which], new_ms[which], new_vs[which]))
        return deltas[DOWN]

    g_w3_1, dab1 = ffn_backward_weights("ffn1", dx1, saved1, w3_1, after=(rs_mixer_token, rs_ffn2_token))
    rs_ffn1, token = reduce_scatter_1("rs_ffn1", [g_w3_1])
    rs_mixer, rs_mixer_token = reduce_scatter_3("rs_mixer", rs_mixer, c_arr, g_w3_1)
    (s_w3_2,) = reduce_scatter_4("rs_ffn2", rs_ffn2, g_w3_1)
    last = adamw_ffn("ffn2", s_w3_2, ffn2_w, ffn2_m, ffn2_v, (token, rs_mixer_token))
    rs_ffn1, rs_ffn1_token = reduce_scatter_2("rs_ffn1", rs_ffn1, c_arr, last)
    dx0, gn_ffn1 = ffn_backward_input("ffn1", dx1, saved1, dab1, norm_ffn1, w3_1, after=(rs_ffn1_token,))

    def pack_small(n1, nm, n2, nf, bg, bf, last):
        return jnp.concatenate([n1, nm, n2, nf, bg.reshape(2, d), jnp.pad(bf, ((0, 0), (0, d - n_forget))), last], axis=0)

    zero_row = jnp.zeros((1, d), F32)
    g_small = pack_small(gn_ffn1, gn_mix, gn_ffn2, gn_final, g_bgate, g_bf[:, :n_forget], jnp.pad(loss_part, ((0, 0), (0, d - 1))))
    w_small = pack_small(norm_ffn1, norm_mix, norm_ffn2, norm_final[None], b_gate, b_forget, zero_row)
    m_small = pack_small(m_norm_ffn1, m_norm_mix, m_norm_ffn2, m_norm_final[None], m_b_gate, m_b_forget, zero_row)
    v_small = pack_small(v_norm_ffn1, v_norm_mix, v_norm_ffn2, v_norm_final[None], v_b_gate, v_b_forget, zero_row)

    s_in, s_gt, s_up, s_wo = reduce_scatter_4("rs_mixer", rs_mixer, dx0)
    grads = {"w_gate": s_gt[0].T, "w_up_a": s_up[0, :WIDTH], "w_up_b": s_up[0, WIDTH:], "w_out": s_wo[0]}
    weights = {"w_gate": (w_gate, m_w_gate, v_w_gate), "w_up_a": (w_up_a, m_w_up_a, v_w_up_a),
               "w_up_b": (w_up_b, m_w_up_b, v_w_up_b), "w_out": (w_out, m_w_out, v_w_out)}
    for wname, (w, m, v) in weights.items():
        g = grads[wname]
        delta, new_m, new_v = adamw(f"adamw_{wname}", w[0], g, m[0], v[0])
        big_out[wname] = (g[None], delta[None], new_m[None], new_v[None])
    rows_of = lambda a: jnp.transpose(a, (2, 0, 1))
    g_in_rows = s_in[0, :in4][:, None, :]
    in_rows = adamw_rows("adamw_w_in", rows_of(w_in), g_in_rows, rows_of(m_w_in), rows_of(v_w_in))
    big_out["w_in"] = tuple(jnp.transpose(a, (1, 2, 0)) for a in (g_in_rows, *in_rows))

    rs_ffn1, token = reduce_scatter_3("rs_ffn1", rs_ffn1, c_arr, in_rows[0])
    smalls = small_allreduce_adamw("small_allreduce_adamw", g_small, w_small, m_small, v_small, after=(token,))
    (s_w3_1,) = reduce_scatter_4("rs_ffn1", rs_ffn1, smalls[0])
    adamw_ffn("ffn1", s_w3_1, ffn1_w, ffn1_m, ffn1_v, ())

    def unpack_small(p):
        return {"norm_ffn1": p[0:1], "norm_mix": p[1:2], "norm_ffn2": p[2:3], "norm_final": p[3], "b_gate": p[4:6].reshape(1, 2 * d),
                "b_forget": p[6:7, :n_forget]}

    loss = smalls[0][7, 0]
    small_out = [unpack_small(p) for p in smalls]

    order = ["norm_ffn1", "w_ffn1_gate", "w_ffn1_up", "w_ffn1_down", "norm_mix", "w_in", "b_forget", "w_gate", "b_gate",
             "w_up_a", "w_up_b", "w_out", "norm_ffn2", "w_ffn2_gate", "w_ffn2_up", "w_ffn2_down", "norm_final"]
    outs = [loss, dx0[None]]
    for kind in range(4):
        for wname in order:
            outs.append(big_out[wname][kind] if wname in big_out else small_out[kind][wname])
    return tuple(outs)
```

```python
import functools

import jax
import jax.numpy as jnp
from jax import lax
from jax.experimental import pallas as pl
from jax.experimental.pallas import tpu as pltpu

F32 = jnp.float32
BF16 = jnp.bfloat16

HEAD_DIM = 64
N_HEADS = 8
WIDTH = N_HEADS * HEAD_DIM
QKV_COLS = 6 * WIDTH
RMS_EPS = 1e-6
ATTN_SCALE = HEAD_DIM ** -0.5
N_CHIPS = 4
QB = 128
BQ = 2048
CS = 256
N_SUB = BQ // CS
NEG_BIG = -1e30

ADAM_LR = 0.001
ADAM_B1 = 0.9
ADAM_B2 = 0.999
ADAM_EPS = 1e-08
ADAM_WD = 0.01
ADAM_STEP = 10

VMEM_LIMIT_BYTES = 48 * 1024 * 1024
MESH = pl.DeviceIdType.MESH

NN = ((1,), (0,))
NT = ((1,), (1,))
TN = ((0,), (0,))


def _params(semantics):
    return pltpu.CompilerParams(dimension_semantics=semantics, vmem_limit_bytes=VMEM_LIMIT_BYTES)


def _dot(a, b, contract):
    return lax.dot_general(a.astype(BF16), b.astype(BF16), (contract, ((), ())), preferred_element_type=F32)


def _sigmoid(x):
    return 1.0 / (1.0 + jnp.exp(-x))


def _log1pexp_neg_abs(z):
    return jnp.log(1.0 + jnp.exp(-jnp.abs(z)))


def _split3(x):
    hi = x.astype(BF16)
    r1 = x - hi.astype(F32)
    mid = r1.astype(BF16)
    lo = (r1 - mid.astype(F32)).astype(BF16)
    return hi, mid, lo


def _dot_exact_lhs01(m01, x):
    hi, mid, lo = _split3(x)
    d = lambda p: lax.dot_general(m01, p, (NN, ((), ())), preferred_element_type=F32)
    return d(hi) + d(mid) + d(lo)


def _iota2(shape, dim):
    return lax.broadcasted_iota(jnp.int32, shape, dim)


def _mm(name, pairs, contract, grid, pair_specs, out_shape, out_specs, acc_shape, nk, epilogue,
        extras=(), extra_specs=(), semantics=None):
    n_pairs = len(pairs)
    n_extra = len(extras)
    n_out = len(out_shape)

    def body(*refs):
        ab = refs[:2 * n_pairs]
        ex = refs[2 * n_pairs:2 * n_pairs + n_extra]
        outs = refs[2 * n_pairs + n_extra:2 * n_pairs + n_extra + n_out]
        ids = [pl.program_id(i) for i in range(len(grid))]
        k = ids[-1]
        part = _dot(ab[0][...], ab[1][...], contract)
        for p in range(1, n_pairs):
            part += _dot(ab[2 * p][...], ab[2 * p + 1][...], contract)
        if nk == 1:
            epilogue(part, ex, outs, ids)
            return
        acc = refs[-1]

        @pl.when(k == 0)
        def _():
            acc[...] = part

        @pl.when(k != 0)
        def _():
            acc[...] += part

        @pl.when(k == nk - 1)
        def _():
            epilogue(acc[...], ex, outs, ids)

    operands = [t for pair in pairs for t in pair] + list(extras)
    in_specs = [s for pair in pair_specs for s in pair] + list(extra_specs)
    if semantics is None:
        semantics = ("parallel",) * (len(grid) - 1) + ("arbitrary",)
    return pl.pallas_call(
        body, name=name, grid=grid, in_specs=in_specs, out_specs=list(out_specs), out_shape=list(out_shape),
        scratch_shapes=[] if nk == 1 else [pltpu.VMEM(acc_shape, F32)], compiler_params=_params(semantics),
    )(*operands)


def _ordered_after(body, n_in, n_after):
    def wrapped(*refs):
        return body(*refs[:n_in], *refs[n_in + n_after:])
    return wrapped


def _row_tile(rows, target):
    t = min(rows, target)
    while rows % t:
        t //= 2
    return t


def rms_fwd(name, x, g, after=()):
    t, d = x.shape
    tr = _row_tile(t, 256)

    def body(x_ref, g_ref, h_ref, r_ref):
        xv = x_ref[...]
        r = lax.rsqrt(jnp.mean(xv * xv, axis=-1, keepdims=True) + RMS_EPS)
        h_ref[...] = (xv * r * g_ref[...]).astype(BF16)
        r_ref[...] = r

    return pl.pallas_call(
        _ordered_after(body, 2, len(after)), name=name, grid=(t // tr,),
        in_specs=[pl.BlockSpec((tr, d), lambda i: (i, 0)), pl.BlockSpec((1, d), lambda i: (0, 0))] + [_ANY] * len(after),
        out_specs=[pl.BlockSpec((tr, d), lambda i: (i, 0)), pl.BlockSpec((tr, 1), lambda i: (i, 0))],
        out_shape=[jax.ShapeDtypeStruct((t, d), BF16), jax.ShapeDtypeStruct((t, 1), F32)],
        compiler_params=_params(("parallel",)),
    )(x, g, *after)


GATE, UP, DOWN = 0, 1, 2


def _ffn_w_spec(which, f4, d, index_of_j):
    return pl.BlockSpec((None, None, f4, d), lambda *ids: (index_of_j(*ids), which, 0, 0))


def ffn_up(name, h, w3):
    t, d = h.shape
    ns, _, f4, _ = w3.shape
    tm = _row_tile(t, 512)

    def body(h_ref, wg_ref, wu_ref, a_ref, b_ref, s_ref):
        hv = h_ref[...]
        a = _dot(hv, wg_ref[...], NT)
        b = _dot(hv, wu_ref[...], NT)
        a_ref[...] = a.astype(BF16)
        b_ref[...] = b.astype(BF16)
        s_ref[...] = (a * _sigmoid(a) * b).astype(BF16)

    act_spec = pl.BlockSpec((None, tm, f4), lambda j, m: (j, m, 0))
    return pl.pallas_call(
        body, name=name, grid=(ns, t // tm),
        in_specs=[pl.BlockSpec((tm, d), lambda j, m: (m, 0)),
                  _ffn_w_spec(GATE, f4, d, lambda j, m: j), _ffn_w_spec(UP, f4, d, lambda j, m: j)],
        out_specs=[act_spec, act_spec, act_spec],
        out_shape=[jax.ShapeDtypeStruct((ns, t, f4), BF16)] * 3,
        compiler_params=_params(("parallel", "parallel")),
    )(h, w3, w3)


def mm_residual(name, s, w, w_spec, x, scale):
    nj, t, kdim = s.shape
    n = x.shape[1]
    tm = _row_tile(t, 512)

    def body(s_ref, w_ref, x_ref, o_ref):
        acc = _dot(s_ref[0], w_ref[0], NN)
        for j in range(1, nj):
            acc += _dot(s_ref[j], w_ref[j], NN)
        o_ref[...] = x_ref[...] + scale * acc

    row = pl.BlockSpec((tm, n), lambda m: (m, 0))
    return pl.pallas_call(
        body, name=name, grid=(t // tm,),
        in_specs=[pl.BlockSpec((nj, tm, kdim), lambda m: (0, m, 0)), w_spec, row], out_specs=row,
        out_shape=jax.ShapeDtypeStruct((t, n), F32), compiler_params=_params(("parallel",)),
    )(s, w, x)


def ffn_bwd_act(name, dx, w3, a, b, after=()):
    t, d = dx.shape
    ns, _, f4, _ = w3.shape
    tm = _row_tile(t, 512)

    def body(dx_ref, wd_ref, a_ref, b_ref, da_ref, db_ref):
        ds = _dot(0.5 * dx_ref[...], wd_ref[...], NT)
        av = a_ref[...].astype(F32)
        sig = _sigmoid(av)
        da_ref[...] = (ds * b_ref[...].astype(F32) * (sig * (1.0 + av * (1.0 - sig)))).astype(BF16)
        db_ref[...] = (ds * (av * sig)).astype(BF16)

    act_spec = pl.BlockSpec((None, tm, f4), lambda j, m: (j, m, 0))
    return pl.pallas_call(
        _ordered_after(body, 4, len(after)), name=name, grid=(ns, t // tm),
        in_specs=[pl.BlockSpec((tm, d), lambda j, m: (m, 0)), _ffn_w_spec(DOWN, f4, d, lambda j, m: j), act_spec, act_spec]
        + [_ANY] * len(after),
        out_specs=[act_spec, act_spec],
        out_shape=[jax.ShapeDtypeStruct((ns, t, f4), BF16)] * 2,
        compiler_params=_params(("parallel", "parallel")),
    )(dx, w3, a, b, *after)


def ffn_wgrad(name, h, da, db, s, dx):
    t, d = h.shape
    ns, _, f4 = da.shape
    tk = _row_tile(t, 1024)
    nk = t // tk

    def body(h_ref, da_ref, db_ref, s_ref, dx_ref, o_ref, acc):
        k = pl.program_id(1)

        @pl.when(k == 0)
        def _():
            acc[...] = jnp.zeros_like(acc)

        hv = h_ref[...]
        acc[GATE] += _dot(da_ref[...], hv, TN)
        acc[UP] += _dot(db_ref[...], hv, TN)
        acc[DOWN] += _dot(s_ref[...], 0.5 * dx_ref[...], TN)

        @pl.when(k == nk - 1)
        def _():
            o_ref[...] = acc[...].astype(BF16)

    act_spec = pl.BlockSpec((None, tk, f4), lambda j, k: (j, k, 0))
    row_spec = pl.BlockSpec((tk, d), lambda j, k: (k, 0))
    return pl.pallas_call(
        body, name=name, grid=(ns, nk),
        in_specs=[row_spec, act_spec, act_spec, act_spec, row_spec],
        out_specs=pl.BlockSpec((None, 3, f4, d), lambda j, k: (j, 0, 0, 0)),
        out_shape=jax.ShapeDtypeStruct((ns, 3, f4, d), BF16),
        scratch_shapes=[pltpu.VMEM((3, f4, d), F32)],
        compiler_params=_params(("parallel", "arbitrary")),
    )(h, da, db, s, dx)


def _rms_bwd_tail(dh, x_ref, r_ref, g_ref, dxin_ref, dx_ref, gn_ref, row_tile_index):
    r = r_ref[...]
    xhat = x_ref[...] * r
    dhg = dh * g_ref[...]
    dx_ref[...] = dxin_ref[...] + r * (dhg - xhat * jnp.mean(dhg * xhat, axis=-1, keepdims=True))
    part = jnp.sum(dh * xhat, axis=0, keepdims=True)

    @pl.when(row_tile_index == 0)
    def _():
        gn_ref[...] = part

    @pl.when(row_tile_index != 0)
    def _():
        gn_ref[...] += part


def ffn_dh(name, da, db, w3, x, rstd, g, dx_in, after=()):
    ns, t, f4 = da.shape
    d = x.shape[1]
    tm = _row_tile(t, 256)

    def body(da_ref, db_ref, wg_ref, wu_ref, x_ref, r_ref, g_ref, dxin_ref, dx_ref, gn_ref):
        dh = _dot(da_ref[0], wg_ref[0], NN) + _dot(db_ref[0], wu_ref[0], NN)
        for j in range(1, ns):
            dh += _dot(da_ref[j], wg_ref[j], NN) + _dot(db_ref[j], wu_ref[j], NN)
        _rms_bwd_tail(dh, x_ref, r_ref, g_ref, dxin_ref, dx_ref, gn_ref, pl.program_id(0))

    act = pl.BlockSpec((ns, tm, f4), lambda m: (0, m, 0))
    row = pl.BlockSpec((tm, d), lambda m: (m, 0))
    gain = pl.BlockSpec((1, d), lambda m: (0, 0))
    return pl.pallas_call(
        _ordered_after(body, 8, len(after)), name=name, grid=(t // tm,),
        in_specs=[act, act, pl.BlockSpec((ns, None, f4, d), lambda m: (0, GATE, 0, 0)),
                  pl.BlockSpec((ns, None, f4, d), lambda m: (0, UP, 0, 0)), row, pl.BlockSpec((tm, 1), lambda m: (m, 0)), gain, row]
        + [_ANY] * len(after),
        out_specs=[row, gain], out_shape=[jax.ShapeDtypeStruct((t, d), F32), jax.ShapeDtypeStruct((1, d), F32)],
        compiler_params=_params(("arbitrary",)),
    )(da, db, w3, w3, x, rstd, g, dx_in, *after)


def mixer_dh(name, parts, x, rstd, g, dx_in, after=()):
    t, d = x.shape
    n = len(parts)
    tm = _row_tile(t, 256)

    def body(*refs):
        dh = _dot(refs[0][...], refs[n][...], NN)
        for i in range(1, n):
            dh += _dot(refs[i][...], refs[n + i][...], NN)
        _rms_bwd_tail(dh, *refs[2 * n:2 * n + 6], pl.program_id(0))

    row = pl.BlockSpec((tm, d), lambda m: (m, 0))
    gain = pl.BlockSpec((1, d), lambda m: (0, 0))
    act_specs = [pl.BlockSpec((tm, a.shape[1]), lambda m: (m, 0)) for a, _, _ in parts]
    w_specs = [pl.BlockSpec((a.shape[1], d), functools.partial(lambda m, blk: (blk, 0), blk=blk)) for a, _, blk in parts]
    return pl.pallas_call(
        _ordered_after(body, 2 * n + 4, len(after)), name=name, grid=(t // tm,),
        in_specs=act_specs + w_specs + [row, pl.BlockSpec((tm, 1), lambda m: (m, 0)), gain, row] + [_ANY] * len(after),
        out_specs=[row, gain], out_shape=[jax.ShapeDtypeStruct((t, d), F32), jax.ShapeDtypeStruct((1, d), F32)],
        compiler_params=_params(("arbitrary",)),
    )(*[a for a, _, _ in parts], *[w for _, w, _ in parts], x, rstd, g, dx_in, *after)


def ffn_forward(tag, x, g_norm, w3, normed=None):
    _, _, f4, d = w3.shape
    h, rstd = normed if normed is not None else rms_fwd(f"{tag}_rms", x, g_norm)
    a, b, s = ffn_up(f"{tag}_up", h, w3)
    x_out = mm_residual(f"{tag}_down", s, w3, pl.BlockSpec((w3.shape[0], None, f4, d), lambda m: (0, DOWN, 0, 0)), x, 0.5)
    return x_out, (x, h, rstd, a, b, s)


def ffn_backward_weights(tag, dx, saved, w3, after=()):
    x, h, rstd, a, b, s = saved
    da, db = ffn_bwd_act(f"{tag}_bwd_act", dx, w3, a, b, after)
    return ffn_wgrad(f"{tag}_wgrad", h, da, db, s, dx), (da, db)


def ffn_backward_input(tag, dx, saved, dab, g_norm, w3, after=()):
    x, h, rstd, a, b, s = saved
    return ffn_dh(f"{tag}_dh", dab[0], dab[1], w3, x, rstd, g_norm, dx, after)


def proj(name, h, wcat_t, bias, first_col, n_cols, tn, out_dtype, scaled_tiles=()):
    t, d = h.shape
    tm = _row_tile(t, 512)
    off = first_col // tn

    def epilogue(acc, ex, outs, ids):
        val = acc + ex[0][...]
        if scaled_tiles:
            hit = functools.reduce(jnp.logical_or, [ids[0] == s for s in scaled_tiles])
            val = val * jnp.where(hit, ATTN_SCALE, 1.0)
        outs[0][...] = val.astype(out_dtype)

    return _mm(
        name, [(h, wcat_t)], NT, (n_cols // tn, t // tm, 1),
        [(pl.BlockSpec((tm, d), lambda j, m, k: (m, 0)), pl.BlockSpec((tn, d), lambda j, m, k: (off + j, 0)))],
        [jax.ShapeDtypeStruct((t, n_cols), out_dtype)], [pl.BlockSpec((tm, tn), lambda j, m, k: (m, j))], (tm, tn), 1, epilogue,
        extras=[bias], extra_specs=[pl.BlockSpec((1, tn), lambda j, m, k: (0, off + j))],
    )[0]


def mix_fwd(name, ya, yb, wup, pc):
    t, w = ya.shape
    ns, _, tn = wup.shape
    d = ns * tn
    tm = _row_tile(t, 512)

    def body(ya_ref, yb_ref, wa_ref, wb_ref, pa_ref, pb_ref, ua_ref, ub_ref, mx_ref):
        ua = _dot(ya_ref[...], wa_ref[...], NN)
        ub = _dot(yb_ref[...], wb_ref[...], NN)
        ua_ref[...] = ua
        ub_ref[...] = ub
        mx_ref[...] = (_sigmoid(pa_ref[...]) * ua + _sigmoid(pb_ref[...]) * ub).astype(BF16)

    y_spec = pl.BlockSpec((tm, w), lambda m, n: (m, 0))
    o_spec = pl.BlockSpec((tm, tn), lambda m, n: (m, n))
    return pl.pallas_call(
        body, name=name, grid=(t // tm, ns),
        in_specs=[y_spec, y_spec, pl.BlockSpec((None, w, tn), lambda m, n: (n, 0, 0)), pl.BlockSpec((None, w, tn), lambda m, n: (n, 1, 0)),
                  o_spec, pl.BlockSpec((tm, tn), lambda m, n: (m, ns + n))],
        out_specs=[o_spec, o_spec, o_spec],
        out_shape=[jax.ShapeDtypeStruct((t, d), F32), jax.ShapeDtypeStruct((t, d), F32), jax.ShapeDtypeStruct((t, d), BF16)],
        compiler_params=_params(("parallel", "parallel")),
    )(ya, yb, wup, wup, pc, pc)


def up_bwd(name, du, wup, branch):
    t, d = du.shape
    ns, w2, tn = wup.shape
    w = w2 // 2
    tm = _row_tile(t, 512)

    def body(du_ref, w_ref, o_ref):
        acc = _dot(du_ref[:, 0:tn], w_ref[0], NT)
        for j in range(1, ns):
            acc += _dot(du_ref[:, j * tn:(j + 1) * tn], w_ref[j], NT)
        o_ref[...] = acc.astype(BF16)

    return pl.pallas_call(
        body, name=name, grid=(t // tm,),
        in_specs=[pl.BlockSpec((tm, d), lambda m: (m, 0)), pl.BlockSpec((ns, w, tn), lambda m: (0, branch, 0))],
        out_specs=pl.BlockSpec((tm, w), lambda m: (m, 0)), out_shape=jax.ShapeDtypeStruct((t, w), BF16),
        compiler_params=_params(("parallel",)),
    )(du, wup)


def up_wgrad(name, ya, yb, dua, dub):
    t, w = ya.shape
    d = dua.shape[1]
    tn = d // N_CHIPS

    def body(ya_ref, yb_ref, dua_ref, dub_ref, o_ref):
        o_ref[0:w, :] = _dot(ya_ref[...], dua_ref[...], TN).astype(BF16)
        o_ref[w:2 * w, :] = _dot(yb_ref[...], dub_ref[...], TN).astype(BF16)

    y_spec = pl.BlockSpec((t, w), lambda j: (0, 0))
    du_spec = pl.BlockSpec((t, tn), lambda j: (0, j))
    return pl.pallas_call(
        body, name=name, grid=(N_CHIPS,), in_specs=[y_spec, y_spec, du_spec, du_spec],
        out_specs=pl.BlockSpec((None, 2 * w, tn), lambda j: (j, 0, 0)),
        out_shape=jax.ShapeDtypeStruct((N_CHIPS, 2 * w, tn), BF16), compiler_params=_params(("parallel",)),
    )(ya, yb, dua, dub)


def mix_bwd(name, dx, wo, pc, ua, ub, after=()):
    t, d = dx.shape
    tm = _row_tile(t, 512)
    tn = 512
    off_a = 0
    off_b = d // tn

    def body(dx_ref, wo_ref, pa_ref, pb_ref, ua_ref, ub_ref, dua_ref, dub_ref, dpa_ref, dpb_ref, ba_ref, bb_ref):
        dm = _dot(dx_ref[...], wo_ref[...], NT)
        ga = _sigmoid(pa_ref[...])
        gb = _sigmoid(pb_ref[...])
        dua_ref[...] = (dm * ga).astype(BF16)
        dub_ref[...] = (dm * gb).astype(BF16)
        dpa = dm * ua_ref[...] * ga * (1.0 - ga)
        dpb = dm * ub_ref[...] * gb * (1.0 - gb)
        dpa_ref[...] = dpa.astype(BF16)
        dpb_ref[...] = dpb.astype(BF16)
        sa = jnp.sum(dpa, axis=0, keepdims=True)
        sb = jnp.sum(dpb, axis=0, keepdims=True)

        @pl.when(pl.program_id(1) == 0)
        def _():
            ba_ref[...] = sa
            bb_ref[...] = sb

        @pl.when(pl.program_id(1) != 0)
        def _():
            ba_ref[...] += sa
            bb_ref[...] += sb

    tile = pl.BlockSpec((tm, tn), lambda n, m: (m, n))
    bias = pl.BlockSpec((1, tn), lambda n, m: (0, n))
    return pl.pallas_call(
        _ordered_after(body, 6, len(after)), name=name, grid=(d // tn, t // tm),
        in_specs=[pl.BlockSpec((tm, d), lambda n, m: (m, 0)), pl.BlockSpec((tn, d), lambda n, m: (n, 0)),
                  pl.BlockSpec((tm, tn), lambda n, m: (m, off_a + n)), pl.BlockSpec((tm, tn), lambda n, m: (m, off_b + n)),
                  tile, tile] + [_ANY] * len(after),
        out_specs=[tile, tile, tile, tile, bias, bias],
        out_shape=[jax.ShapeDtypeStruct((t, d), BF16)] * 4 + [jax.ShapeDtypeStruct((1, d), F32)] * 2,
        compiler_params=_params(("parallel", "arbitrary")),
    )(dx, wo, pc, pc, ua, ub, *after)


def mm_plain(name, a, b, contract, out_dtype, tk_target=512):
    if contract == NN:
        m, kdim = a.shape
        n = b.shape[1]
    elif contract == NT:
        m, kdim = a.shape
        n = b.shape[0]
    else:
        kdim, m = a.shape
        n = b.shape[1]
    tm = _row_tile(m, 512)
    tk = _row_tile(kdim, tk_target)
    nk = kdim // tk
    if contract == TN:
        a_spec = pl.BlockSpec((tk, tm), lambda i, k: (k, i))
    else:
        a_spec = pl.BlockSpec((tm, tk), lambda i, k: (i, k))
    if contract == NT:
        b_spec = pl.BlockSpec((n, tk), lambda i, k: (0, k))
    else:
        b_spec = pl.BlockSpec((tk, n), lambda i, k: (k, 0))

    def epilogue(acc, ex, outs, ids):
        outs[0][...] = acc.astype(out_dtype)

    return _mm(name, [(a, b)], contract, (m // tm, nk), [(a_spec, b_spec)],
               [jax.ShapeDtypeStruct((m, n), out_dtype)], [pl.BlockSpec((tm, n), lambda i, k: (i, 0))], (tm, n), nk, epilogue)[0]


def wgrad_cat(name, h, dcat):
    t, d = h.shape
    n = dcat.shape[1]
    tn = next(c for c in (768, 512, 256, 128) if n % c == 0)
    tk = _row_tile(t, 2048)

    def epilogue(acc, ex, outs, ids):
        outs[0][...] = acc.astype(BF16)

    return _mm(
        name, [(dcat, h)], TN, (n // tn, t // tk),
        [(pl.BlockSpec((tk, tn), lambda j, k: (k, j)), pl.BlockSpec((tk, d), lambda j, k: (k, 0)))],
        [jax.ShapeDtypeStruct((n, d), BF16)], [pl.BlockSpec((tn, d), lambda j, k: (j, 0))], (tn, d), t // tk, epilogue,
    )[0]


def fox_prep(name, f, bias):
    t, lanes = f.shape
    nchunk = t // QB

    def body(f_ref, b_ref, c_ref):
        lower = (_iota2((QB, QB), 1) <= _iota2((QB, QB), 0)).astype(BF16)

        def chunk(n, carry):
            rows = pl.ds(pl.multiple_of(n * QB, QB), QB)
            u = f_ref[rows, :] + b_ref[...]
            lf = jnp.minimum(u, 0.0) - _log1pexp_neg_abs(u)
            c = _dot_exact_lhs01(lower, lf) + carry
            c_ref[rows, :] = c
            return c[QB - 1:QB, :]

        lax.fori_loop(0, nchunk, chunk, jnp.zeros((1, lanes), F32))

    return pl.pallas_call(body, name=name, out_shape=jax.ShapeDtypeStruct((t, lanes), F32),
                          compiler_params=pltpu.CompilerParams(vmem_limit_bytes=VMEM_LIMIT_BYTES))(f, bias)


def fox_gate_bwd(name, dc, f, bias):
    t, lanes = dc.shape
    nchunk = t // QB

    def body(dc_ref, f_ref, b_ref, df_ref, gb_ref):
        upper = (_iota2((QB, QB), 1) >= _iota2((QB, QB), 0)).astype(BF16)

        def chunk(n, carry):
            tail, total = carry
            rows = pl.ds(pl.multiple_of((nchunk - 1 - n) * QB, QB), QB)
            dlf = _dot_exact_lhs01(upper, dc_ref[rows, :]) + tail
            u = f_ref[rows, :] + b_ref[...]
            df = dlf * jnp.exp(jnp.minimum(-u, 0.0) - _log1pexp_neg_abs(u))
            df_ref[rows, :] = df
            return dlf[0:1, :], total + jnp.sum(df, axis=0, keepdims=True)

        zero = jnp.zeros((1, lanes), F32)
        _, total = lax.fori_loop(0, nchunk, chunk, (zero, zero))
        gb_ref[...] = total

    return pl.pallas_call(body, name=name,
                          out_shape=[jax.ShapeDtypeStruct((t, lanes), F32), jax.ShapeDtypeStruct((1, lanes), F32)],
                          compiler_params=pltpu.CompilerParams(vmem_limit_bytes=VMEM_LIMIT_BYTES))(dc, f, bias)


def _qrows(i):
    return pl.ds(pl.multiple_of(i * BQ, BQ), BQ)


def _krows(kc):
    return pl.ds(pl.multiple_of(kc * CS, CS), CS)


def _dot_split2_rhs01(x, m01):
    hi = x.astype(BF16)
    lo = (x - hi.astype(F32)).astype(BF16)
    d = lambda p: lax.dot_general(p, m01, (NN, ((), ())), preferred_element_type=F32)
    return d(hi) + d(lo)


def _live_rows(dchunk):
    return 0 if dchunk is None else dchunk * CS


def _diag_mask(dchunk, inclusive):
    shape = (BQ - _live_rows(dchunk), CS)
    return _iota2(shape, 1) <= _iota2(shape, 0) if inclusive else _iota2(shape, 1) < _iota2(shape, 0)


def _tail(x, r0, axis=0):
    return x if r0 == 0 else (x[r0:] if axis == 0 else x[:, r0:])


def _with_tail(old, tail, r0):
    return tail if r0 == 0 else jnp.concatenate([old[:r0], tail], axis=0)


def _walk_chunks(i, step, init, right_to_left):
    order = list(reversed(range(N_SUB))) if right_to_left else list(range(N_SUB))

    def diagonal(state):
        for dchunk in order:
            state = step(i * N_SUB + dchunk, state, dchunk)
        return state

    def group(n, state):
        base = ((i - 1 - n) if right_to_left else n) * N_SUB
        for dchunk in order:
            state = step(base + dchunk, state, None)
        return state

    if right_to_left:
        return lax.fori_loop(0, i, group, diagonal(init))
    return diagonal(lax.fori_loop(0, i, group, init))


PAIR = 2 * HEAD_DIM
N_PAIRS = N_HEADS // 2


def _pair_spec(t, first_block):
    return pl.BlockSpec((t, PAIR), lambda p, *_: (0, first_block + p))


def _head_lanes(shape):
    lane = _iota2(shape, len(shape) - 1)
    return [lane < HEAD_DIM, lane >= HEAD_DIM]


def _only_head(x, lanes_of_head):
    return jnp.where(lanes_of_head, x, jnp.zeros_like(x))


LOG2_E = 1.4426950408889634


def _sb_chunk_weights(q_h, k, later, carry, dchunk):
    z = _dot(q_h, k, NT) * LOG2_E
    nz = -z
    lnb = jnp.minimum(nz, 0.0) - jnp.log2(1.0 + jnp.exp2(jnp.minimum(z, nz)))
    lsz = lnb + z
    if dchunk is not None:
        lnb = jnp.where(_diag_mask(dchunk, False), lnb, 0.0)
    w = jnp.exp2(lsz + _dot_split2_rhs01(lnb, later) + carry)
    if dchunk is not None:
        w = jnp.where(_diag_mask(dchunk, False), w, 0.0)
    return w, lsz, lnb


def sb_pair_fwd(name, qkv):
    t = qkv.shape[0]

    def body(q_ref, k_ref, v_ref, o_ref):
        later = (_iota2((CS, CS), 0) > _iota2((CS, CS), 1)).astype(BF16)
        lanes = _head_lanes((BQ, PAIR))

        def qblock(i, _):
            q = q_ref[_qrows(i), :]
            q_heads = [_only_head(q, lanes[h]) for h in range(2)]

            def step(kc, state, dchunk):
                k = k_ref[_krows(kc), :]
                v = v_ref[_krows(kc), :]
                out = []
                r0 = _live_rows(dchunk)
                for h in range(2):
                    carry, acc = state[h]
                    w, _, lnb = _sb_chunk_weights(_tail(q_heads[h], r0), k, later, _tail(carry, r0), dchunk)
                    out.append((_with_tail(carry, _tail(carry, r0) + jnp.sum(lnb, axis=1, keepdims=True), r0),
                                _with_tail(acc, _tail(acc, r0) + _dot(w, v, NN), r0)))
                return tuple(out)

            zero = (jnp.zeros((BQ, 1), F32), jnp.zeros((BQ, PAIR), F32))
            (_, acc0), (_, acc1) = _walk_chunks(i, step, (zero, zero), True)
            o_ref[_qrows(i), :] = jnp.where(lanes[0], acc0, acc1).astype(BF16)
            return 0

        lax.fori_loop(0, t // BQ, qblock, 0)

    return pl.pallas_call(
        body, name=name, grid=(N_PAIRS,),
        in_specs=[_pair_spec(t, 0), _pair_spec(t, N_PAIRS), _pair_spec(t, 2 * N_PAIRS)],
        out_specs=_pair_spec(t, 0), out_shape=jax.ShapeDtypeStruct((t, WIDTH), BF16),
        compiler_params=_params(("parallel",)),
    )(qkv, qkv, qkv)


def _emit_dqkv(res, o_ref):
    o_ref[...] = res[pl.program_id(1)]


def _flush_transposed(acc, res, which):
    for kc in range(acc.shape[0]):
        res[which, kc * CS:(kc + 1) * CS, :] = acc[kc].T.astype(BF16)


def sb_pair_bwd(name, qkv, dy, dqkv):
    t = qkv.shape[0]
    nc = t // CS

    def body(q_ref, k_ref, v_ref, do_ref, _, o_ref, g_s, b_s, dkt_acc, dvt_acc, res):
        @pl.when(pl.program_id(1) == 0)
        def _():
            later = (_iota2((CS, CS), 0) > _iota2((CS, CS), 1)).astype(BF16)
            earlier = (_iota2((CS, CS), 0) < _iota2((CS, CS), 1)).astype(BF16)
            lanes = _head_lanes((BQ, PAIR))
            dkt_acc[...] = jnp.zeros_like(dkt_acc)
            dvt_acc[...] = jnp.zeros_like(dvt_acc)

            def qblock(i, _):
                q = q_ref[_qrows(i), :]
                do = do_ref[_qrows(i), :]
                zero = jnp.zeros((BQ, 1), F32)
                dqs = []
                for h in range(2):
                    q_h = _only_head(q, lanes[h])
                    do_h = _only_head(do, lanes[h])
                    qt_h = q_h.astype(F32).T.astype(BF16)
                    dot_h = do_h.astype(F32).T.astype(BF16)

                    def step1(kc, carry, dchunk, q_h=q_h, do_h=do_h, dot_h=dot_h):
                        k = k_ref[_krows(kc), :]
                        v = v_ref[_krows(kc), :]
                        r0 = _live_rows(dchunk)
                        live = _tail(carry, r0)
                        w, lsz, lnb = _sb_chunk_weights(_tail(q_h, r0), k, later, live, dchunk)
                        g_s[kc, r0:, :] = (w * _dot(_tail(do_h, r0), v, NT)).astype(BF16)
                        b_s[kc, r0:, :] = jnp.exp2(lsz).astype(BF16)
                        dvt_acc[kc] += _dot(_tail(dot_h, r0, axis=1), w, NN)
                        return _with_tail(carry, live + jnp.sum(lnb, axis=1, keepdims=True), r0)

                    _walk_chunks(i, step1, zero, True)

                    def step2(kc, state, dchunk, qt_h=qt_h):
                        k = k_ref[_krows(kc), :]
                        r0 = _live_rows(dchunk)
                        before, dq = state
                        g16 = g_s[kc, r0:, :]
                        g = g16.astype(F32)
                        beta = b_s[kc, r0:, :].astype(F32)
                        prefix = lax.dot_general(g16, earlier, (NN, ((), ())), preferred_element_type=F32) + _tail(before, r0)
                        dz = g * (1.0 - beta) - beta * prefix
                        if dchunk is not None:
                            dz = jnp.where(_diag_mask(dchunk, False), dz, 0.0)
                        dzb = dz.astype(BF16)
                        dkt_acc[kc] += _dot(_tail(qt_h, r0, axis=1), dzb, NN)
                        return (_with_tail(before, _tail(before, r0) + jnp.sum(g, axis=1, keepdims=True), r0),
                                _with_tail(dq, _tail(dq, r0) + _dot(dzb, k, NN), r0))

                    dqs.append(_walk_chunks(i, step2, (zero, jnp.zeros((BQ, PAIR), F32)), False)[1])
                res[0, _qrows(i), :] = (jnp.where(lanes[0], dqs[0], dqs[1]) * ATTN_SCALE).astype(BF16)
                return 0

            lax.fori_loop(0, t // BQ, qblock, 0)
            _flush_transposed(dkt_acc, res, 1)
            _flush_transposed(dvt_acc, res, 2)

        _emit_dqkv(res, o_ref)

    return pl.pallas_call(
        body, name=name, grid=(N_PAIRS, 3),
        in_specs=[_pair_spec(t, 0), _pair_spec(t, N_PAIRS), _pair_spec(t, 2 * N_PAIRS), _pair_spec(t, 0), _ANY],
        out_specs=pl.BlockSpec((t, PAIR), lambda p, s: (0, s * N_PAIRS + p)),
        out_shape=jax.ShapeDtypeStruct(dqkv.shape, BF16), input_output_aliases={4: 0},
        scratch_shapes=[pltpu.VMEM((nc, BQ, CS), BF16), pltpu.VMEM((nc, BQ, CS), BF16),
                        pltpu.VMEM((nc, PAIR, CS), F32), pltpu.VMEM((nc, PAIR, CS), F32), pltpu.VMEM((3, t, PAIR), BF16)],
        compiler_params=_params(("parallel", "arbitrary")),
    )(qkv, qkv, qkv, dy, dqkv)


def _gates_col_spec(t):
    return pl.BlockSpec((2, t, 1), lambda p, *_: (p, 0, 0))


def _gates_row_spec(nc):
    return pl.BlockSpec((2, nc, 1, CS), lambda p, *_: (p, 0, 0, 0))


def fox_pair_fwd(name, qkv, c_col, c_row):
    t = qkv.shape[0]

    def body(q_ref, k_ref, v_ref, cc_ref, cr_ref, o_ref, lse_ref):
        lanes = _head_lanes((BQ, PAIR))

        def qblock(i, _):
            q = q_ref[_qrows(i), :]
            q_heads = [_only_head(q, lanes[h]) for h in range(2)]
            ct = [cc_ref[h, _qrows(i), :] for h in range(2)]

            def step(kc, state, dchunk):
                k = k_ref[_krows(kc), :]
                v = v_ref[_krows(kc), :]
                out = []
                r0 = _live_rows(dchunk)
                for h in range(2):
                    m, l, acc = (_tail(a, r0) for a in state[h])
                    s = _dot(_tail(q_heads[h], r0), k, NT) + _tail(ct[h], r0) - cr_ref[h, kc]
                    if dchunk is not None:
                        s = jnp.where(_diag_mask(dchunk, True), s, NEG_BIG)
                    m_new = jnp.maximum(m, jnp.max(s, axis=1, keepdims=True))
                    alpha = jnp.exp(m - m_new)
                    p = jnp.exp(s - m_new)
                    if dchunk is not None:
                        p = jnp.where(_diag_mask(dchunk, True), p, 0.0)
                    new = (m_new, l * alpha + jnp.sum(p, axis=1, keepdims=True), acc * alpha + _dot(p, v, NN))
                    out.append(tuple(_with_tail(old, tail, r0) for old, tail in zip(state[h], new)))
                return tuple(out)

            init = (jnp.full((BQ, 1), NEG_BIG, F32), jnp.zeros((BQ, 1), F32), jnp.zeros((BQ, PAIR), F32))
            (m0, l0, acc0), (m1, l1, acc1) = _walk_chunks(i, step, (init, init), False)
            o_ref[_qrows(i), :] = jnp.where(lanes[0], acc0 / l0, acc1 / l1).astype(BF16)
            lse_ref[0, _qrows(i), :] = m0 + jnp.log(l0)
            lse_ref[1, _qrows(i), :] = m1 + jnp.log(l1)
            return 0

        lax.fori_loop(0, t // BQ, qblock, 0)

    return pl.pallas_call(
        body, name=name, grid=(N_PAIRS,),
        in_specs=[_pair_spec(t, 3 * N_PAIRS), _pair_spec(t, 4 * N_PAIRS), _pair_spec(t, 5 * N_PAIRS),
                  _gates_col_spec(t), _gates_row_spec(t // CS)],
        out_specs=[_pair_spec(t, 0), _gates_col_spec(t)],
        out_shape=[jax.ShapeDtypeStruct((t, WIDTH), BF16), jax.ShapeDtypeStruct((N_HEADS, t, 1), F32)],
        compiler_params=_params(("parallel",)),
    )(qkv, qkv, qkv, c_col, c_row)


def fox_pair_bwd(name, qkv, y, dy, lse, c_col, c_row, dqkv):
    t = qkv.shape[0]
    nc = t // CS

    def body(q_ref, k_ref, v_ref, o_in_ref, do_ref, lse_ref, cc_ref, cr_ref, _, o_ref, dcc_ref, dcr_ref,
             dkt_acc, dvt_acc, dcr_acc, res):
        @pl.when(pl.program_id(1) == 0)
        def _():
            lanes = _head_lanes((BQ, PAIR))
            dkt_acc[...] = jnp.zeros_like(dkt_acc)
            dvt_acc[...] = jnp.zeros_like(dvt_acc)
            dcr_acc[...] = jnp.zeros_like(dcr_acc)

            def qblock(i, _):
                q = q_ref[_qrows(i), :]
                do = do_ref[_qrows(i), :]
                q_heads = [_only_head(q, lanes[h]) for h in range(2)]
                do_heads = [_only_head(do, lanes[h]) for h in range(2)]
                qt_heads = [qh.astype(F32).T.astype(BF16) for qh in q_heads]
                dot_heads = [dh.astype(F32).T.astype(BF16) for dh in do_heads]
                prod = do.astype(F32) * o_in_ref[_qrows(i), :].astype(F32)
                delta = [jnp.sum(_only_head(prod, lanes[h]), axis=1, keepdims=True) for h in range(2)]
                ct = [cc_ref[h, _qrows(i), :] for h in range(2)]
                lse_i = [lse_ref[h, _qrows(i), :] for h in range(2)]

                def step(kc, state, dchunk):
                    k = k_ref[_krows(kc), :]
                    v = v_ref[_krows(kc), :]
                    out = []
                    r0 = _live_rows(dchunk)
                    for h in range(2):
                        dq, dct = state[h]
                        s = _dot(_tail(q_heads[h], r0), k, NT) + _tail(ct[h], r0) - cr_ref[h, kc]
                        p = jnp.exp(s - _tail(lse_i[h], r0))
                        if dchunk is not None:
                            p = jnp.where(_diag_mask(dchunk, True), p, 0.0)
                        ds = p * (_dot(_tail(do_heads[h], r0), v, NT) - _tail(delta[h], r0))
                        dvt_acc[kc] += _dot(_tail(dot_heads[h], r0, axis=1), p, NN)
                        dsb = ds.astype(BF16)
                        dkt_acc[kc] += _dot(_tail(qt_heads[h], r0, axis=1), dsb, NN)
                        dcr_acc[h, kc] -= jnp.sum(ds, axis=0, keepdims=True)
                        out.append((_with_tail(dq, _tail(dq, r0) + _dot(dsb, k, NN), r0),
                                    _with_tail(dct, _tail(dct, r0) + jnp.sum(ds, axis=1, keepdims=True), r0)))
                    return tuple(out)

                zero = (jnp.zeros((BQ, PAIR), F32), jnp.zeros((BQ, 1), F32))
                (dq0, dct0), (dq1, dct1) = _walk_chunks(i, step, (zero, zero), False)
                res[0, _qrows(i), :] = (jnp.where(lanes[0], dq0, dq1) * ATTN_SCALE).astype(BF16)
                lane = _iota2((BQ, PAIR), 1)
                dcc_ref[_qrows(i), :] = jnp.where(lane == 0, dct0, jnp.where(lane == 1, dct1, 0.0))
                return 0

            lax.fori_loop(0, t // BQ, qblock, 0)
            _flush_transposed(dkt_acc, res, 1)
            _flush_transposed(dvt_acc, res, 2)
            dcr_ref[...] = dcr_acc[...]

        _emit_dqkv(res, o_ref)

    return pl.pallas_call(
        body, name=name, grid=(N_PAIRS, 3),
        in_specs=[_pair_spec(t, 3 * N_PAIRS), _pair_spec(t, 4 * N_PAIRS), _pair_spec(t, 5 * N_PAIRS), _pair_spec(t, 0),
                  _pair_spec(t, 0), _gates_col_spec(t), _gates_col_spec(t), _gates_row_spec(nc), _ANY],
        out_specs=[pl.BlockSpec((t, PAIR), lambda p, s: (0, (3 + s) * N_PAIRS + p)),
                   pl.BlockSpec((None, t, PAIR), lambda p, s: (p, 0, 0)), _gates_row_spec(nc)],
        out_shape=[jax.ShapeDtypeStruct(dqkv.shape, BF16), jax.ShapeDtypeStruct((N_PAIRS, t, PAIR), F32),
                   jax.ShapeDtypeStruct((N_HEADS, nc, 1, CS), F32)],
        input_output_aliases={8: 0},
        scratch_shapes=[pltpu.VMEM((nc, PAIR, CS), F32), pltpu.VMEM((nc, PAIR, CS), F32), pltpu.VMEM((2, nc, 1, CS), F32),
                        pltpu.VMEM((3, t, PAIR), BF16)],
        compiler_params=_params(("parallel", "arbitrary")),
    )(qkv, qkv, qkv, y, dy, lse, c_col, c_row, dqkv)


def loss_head(name, x, g, target):
    t, d = x.shape
    tr = _row_tile(t, 256)

    def body(x_ref, g_ref, t_ref, dx_ref, gn_ref, loss_ref):
        xv = x_ref[...]
        r = lax.rsqrt(jnp.mean(xv * xv, axis=-1, keepdims=True) + RMS_EPS)
        xhat = xv * r
        gv = g_ref[...]
        err = xhat * gv - t_ref[...]
        part_loss = 0.5 * jnp.sum(jnp.mean(err * err, axis=-1, keepdims=True), axis=0, keepdims=True)
        dy = err * (1.0 / d)
        dyg = dy * gv
        dx_ref[...] = r * (dyg - xhat * jnp.mean(dyg * xhat, axis=-1, keepdims=True))
        part_g = jnp.sum(dy * xhat, axis=0, keepdims=True)

        @pl.when(pl.program_id(0) == 0)
        def _():
            gn_ref[...] = part_g
            loss_ref[...] = part_loss

        @pl.when(pl.program_id(0) != 0)
        def _():
            gn_ref[...] += part_g
            loss_ref[...] += part_loss

    row = pl.BlockSpec((tr, d), lambda i: (i, 0))
    return pl.pallas_call(
        body, name=name, grid=(t // tr,),
        in_specs=[row, pl.BlockSpec((1, d), lambda i: (0, 0)), row],
        out_specs=[row, pl.BlockSpec((1, d), lambda i: (0, 0)), pl.BlockSpec((1, 1), lambda i: (0, 0))],
        out_shape=[jax.ShapeDtypeStruct((t, d), F32), jax.ShapeDtypeStruct((1, d), F32), jax.ShapeDtypeStruct((1, 1), F32)],
        compiler_params=_params(("arbitrary",)),
    )(x, g, target)


def _place():
    return lax.axis_index("x"), lax.axis_index("y"), lax.axis_index("c")


def _other_chips(x, y):
    return [(1 - x, y), (x, 1 - y), (1 - x, 1 - y)]


def _half(ref, c, rows):
    if rows % 32 == 0:
        return ref.at[:, pl.ds(c * (rows // 2), rows // 2), :]
    cols = ref.shape[2]
    return ref.at[:, :, pl.ds(c * (cols // 2), cols // 2)]


_ANY = pl.BlockSpec(memory_space=pl.ANY)


_HBM = pl.BlockSpec(memory_space=pltpu.HBM)
_SEM = pl.BlockSpec(memory_space=pltpu.SEMAPHORE)
_DATAFLOW = pltpu.SideEffectType.DATAFLOW_SIDE_EFFECTING


def _in_hbm(a):
    return pltpu.with_memory_space_constraint(a, pltpu.HBM)


def _gather_ici_copies(bufs, send_sems, recv_sems, arrivals):
    x, y, c = _place()
    me = 2 * x + y
    copies = []
    for i, buf in enumerate(bufs):
        rows = buf.shape[2]
        for j, (qx, qy) in enumerate(_other_chips(x, y)):
            block = _half(buf.at[2 * qx + qy if arrivals else me], c, rows)
            copies.append(pltpu.make_async_remote_copy(
                src_ref=block, dst_ref=block, send_sem=send_sems.at[3 * i + j], recv_sem=recv_sems.at[3 * i + j],
                device_id=(qx, qy, c), device_id_type=MESH))
    return copies


def gather_ici_start(name, bufs, after):
    n = len(bufs)

    def body(*refs):
        ins = refs[:n]
        send_sems, recv_sems = refs[n + 1], refs[n + 2]
        token = refs[-1]
        for send in _gather_ici_copies(ins, send_sems, recv_sems, False):
            send.start()
        token[...] = jnp.zeros_like(token)

    res = pl.pallas_call(
        body, name=name,
        out_shape=(pltpu.SemaphoreType.DMA((3 * n,)), pltpu.SemaphoreType.DMA((3 * n,)), *[pltpu.HBM(b.shape, b.dtype) for b in bufs],
                   jax.ShapeDtypeStruct((8, 128), F32)),
        in_specs=[_HBM] * n + [_ANY], out_specs=(_SEM, _SEM, *[_HBM] * n, pl.BlockSpec(memory_space=pltpu.VMEM)),
        input_output_aliases={i: 2 + i for i in range(n)},
        compiler_params=pltpu.CompilerParams(has_side_effects=_DATAFLOW),
    )(*[_in_hbm(b) for b in bufs], after)
    return res[0], res[1], list(res[2:2 + n]), res[-1]


def gather_ici_wait(name, send_sems, recv_sems, bufs, after):
    n = len(bufs)

    def body(*refs):
        ins = refs[:n]
        send_sems_ref, recv_sems_ref = refs[n], refs[n + 1]
        for send in _gather_ici_copies(ins, send_sems_ref, recv_sems_ref, False):
            send.wait_send()
        for recv in _gather_ici_copies(ins, send_sems_ref, recv_sems_ref, True):
            recv.wait_recv()

    return pl.pallas_call(
        body, name=name, out_shape=tuple(pltpu.HBM(b.shape, b.dtype) for b in bufs),
        in_specs=[_HBM] * n + [_SEM, _SEM, _ANY], out_specs=tuple([_HBM] * n),
        input_output_aliases={i: i for i in range(n)},
        compiler_params=pltpu.CompilerParams(has_side_effects=_DATAFLOW),
    )(*bufs, send_sems, recv_sems, after)


def gather_forward(name, bufs):
    n = len(bufs)

    def body(*refs):
        outs = refs[n:2 * n]
        send_sems, recv_sems = refs[2 * n:]
        x, y, c = _place()
        sibling = (x, y, 1 - c)
        sends = []
        for i in range(n):
            rows = outs[i].shape[2]
            for j, (qx, qy) in enumerate(_other_chips(x, y)):
                block = _half(outs[i].at[2 * qx + qy], c, rows)
                fw = pltpu.make_async_remote_copy(
                    src_ref=block, dst_ref=block, send_sem=send_sems.at[3 * i + j], recv_sem=recv_sems.at[3 * i + j],
                    device_id=sibling, device_id_type=MESH)
                fw.start()
                sends.append(fw)
        for i in range(n):
            rows = outs[i].shape[2]
            for j, (qx, qy) in enumerate(_other_chips(x, y)):
                block = _half(outs[i].at[2 * qx + qy], 1 - c, rows)
                pltpu.make_async_remote_copy(
                    src_ref=block, dst_ref=block, send_sem=send_sems.at[3 * i + j], recv_sem=recv_sems.at[3 * i + j],
                    device_id=sibling, device_id_type=MESH).wait_recv()
        for fw in sends:
            fw.wait_send()

    return pl.pallas_call(
        body, name=name, in_specs=[_ANY] * n, out_specs=[_ANY] * n,
        out_shape=[jax.ShapeDtypeStruct(b.shape, b.dtype) for b in bufs],
        input_output_aliases={i: i for i in range(n)},
        scratch_shapes=[pltpu.SemaphoreType.DMA((3 * n,)), pltpu.SemaphoreType.DMA((3 * n,))],
        compiler_params=pltpu.CompilerParams(has_side_effects=True),
    )(*bufs)


def _between_chips_copies(parts, lands, send_sems, recv_sems):
    x, y, c = _place()
    copies = []
    for i, (part, land) in enumerate(zip(parts, lands)):
        for j, (qx, qy) in enumerate(_other_chips(x, y)):
            copies.append(pltpu.make_async_remote_copy(
                src_ref=part.at[2 * qx + qy], dst_ref=land.at[j], send_sem=send_sems.at[3 * i + j], recv_sem=recv_sems.at[3 * i + j],
                device_id=(qx, qy, c), device_id_type=MESH))
    return copies


def between_chips_start(name, parts):
    n = len(parts)
    lands = [lax.empty((N_CHIPS - 1,) + p.shape[1:], p.dtype) for p in parts]

    def body(*refs):
        send_sems, recv_sems = refs[2 * n], refs[2 * n + 1]
        token = refs[-1]
        for cp in _between_chips_copies(refs[:n], refs[n:2 * n], send_sems, recv_sems):
            cp.start()
        token[...] = jnp.zeros_like(token)

    res = pl.pallas_call(
        body, name=name,
        out_shape=(pltpu.SemaphoreType.DMA((3 * n,)), pltpu.SemaphoreType.DMA((3 * n,)),
                   *[pltpu.HBM(a.shape, a.dtype) for a in parts + lands], jax.ShapeDtypeStruct((8, 128), F32)),
        in_specs=[_HBM] * (2 * n), out_specs=(_SEM, _SEM, *[_HBM] * (2 * n), pl.BlockSpec(memory_space=pltpu.VMEM)),
        input_output_aliases={i: 2 + i for i in range(2 * n)},
        compiler_params=pltpu.CompilerParams(has_side_effects=_DATAFLOW),
    )(*[_in_hbm(a) for a in parts + lands])
    return res[0], res[1], list(res[2:2 + n]), list(res[2 + n:2 + 2 * n]), res[-1]


def between_chips_wait(name, send_sems, recv_sems, parts, lands, after):
    n = len(parts)

    def body(*refs):
        for cp in _between_chips_copies(refs[:n], refs[n:2 * n], refs[2 * n], refs[2 * n + 1]):
            cp.wait_send()
            cp.wait_recv()

    res = pl.pallas_call(
        body, name=name, out_shape=tuple(pltpu.HBM(a.shape, a.dtype) for a in parts + lands),
        in_specs=[_HBM] * (2 * n) + [_SEM, _SEM, _ANY], out_specs=tuple([_HBM] * (2 * n)),
        input_output_aliases={i: i for i in range(2 * n)},
        compiler_params=pltpu.CompilerParams(has_side_effects=_DATAFLOW),
    )(*parts, *lands, send_sems, recv_sems, after)
    return list(res[:n]), list(res[n:])


def exchange_start(name, arrays, n_copies, copies, after=()):
    n = len(arrays)

    def body(*refs):
        send_sems, recv_sems = refs[n + len(after)], refs[n + len(after) + 1]
        for cp in copies(refs[:n], send_sems, recv_sems):
            cp.start()
        refs[-1][...] = jnp.zeros_like(refs[-1])

    res = pl.pallas_call(
        body, name=name,
        out_shape=(pltpu.SemaphoreType.DMA((n_copies,)), pltpu.SemaphoreType.DMA((n_copies,)),
                   *[pltpu.HBM(a.shape, a.dtype) for a in arrays], jax.ShapeDtypeStruct((8, 128), F32)),
        in_specs=[_HBM] * n + [_ANY] * len(after),
        out_specs=(_SEM, _SEM, *[_HBM] * n, pl.BlockSpec(memory_space=pltpu.VMEM)),
        input_output_aliases={i: 2 + i for i in range(n)},
        compiler_params=pltpu.CompilerParams(has_side_effects=_DATAFLOW),
    )(*[_in_hbm(a) for a in arrays], *after)
    return res[0], res[1], list(res[2:2 + n]), res[-1]


def exchange_wait(name, send_sems, recv_sems, arrays, copies, after):
    n = len(arrays)

    def body(*refs):
        for cp in copies(refs[:n], refs[n], refs[n + 1]):
            cp.wait_send()
            cp.wait_recv()

    return list(pl.pallas_call(
        body, name=name, out_shape=tuple(pltpu.HBM(a.shape, a.dtype) for a in arrays),
        in_specs=[_HBM] * n + [_SEM, _SEM, _ANY], out_specs=tuple([_HBM] * n),
        input_output_aliases={i: i for i in range(n)},
        compiler_params=pltpu.CompilerParams(has_side_effects=_DATAFLOW),
    )(*arrays, send_sems, recv_sems, after))


def _to_sibling_copies(n):
    def copies(refs, send_sems, recv_sems):
        x, y, c = _place()
        out = []
        for i in range(n):
            rows = refs[i].shape[2]
            out.append(pltpu.make_async_remote_copy(
                src_ref=refs[i].at[:, :, pl.ds((1 - c) * (rows // 2), rows // 2), :], dst_ref=refs[n + i],
                send_sem=send_sems.at[i], recv_sem=recv_sems.at[i], device_id=(x, y, 1 - c), device_id_type=MESH))
        return out
    return copies


def _share_copies(n):
    def copies(refs, send_sems, recv_sems):
        x, y, c = _place()
        out = []
        for i in range(n):
            mine = _half(refs[i], c, refs[i].shape[1])
            out.append(pltpu.make_async_remote_copy(
                src_ref=mine, dst_ref=mine, send_sem=send_sems.at[i], recv_sem=recv_sems.at[i],
                device_id=(x, y, 1 - c), device_id_type=MESH))
        return out
    return copies


def pair_sum(name, grad, recv, c):
    ns, na, rh, cols = recv.shape
    tr = _row_tile(rh, 256) if rh % 256 == 0 else rh
    nt = rh // tr

    def body(c_ref, g_ref, r_ref, o_ref):
        o_ref[...] = (g_ref[...].astype(F32) + r_ref[...].astype(F32)).astype(BF16)

    blk = (None, None, tr, cols)
    return pl.pallas_call(
        body, name=name,
        grid_spec=pltpu.PrefetchScalarGridSpec(
            num_scalar_prefetch=1, grid=(ns, na, nt),
            in_specs=[pl.BlockSpec(blk, lambda s, a, r, c_ref: (s, a, c_ref[0] * nt + r, 0)),
                      pl.BlockSpec(blk, lambda s, a, r, c_ref: (s, a, r, 0))],
            out_specs=pl.BlockSpec(blk, lambda s, a, r, c_ref: (s, a, r, 0))),
        out_shape=jax.ShapeDtypeStruct(recv.shape, BF16),
        compiler_params=_params(("parallel", "parallel", "parallel")),
    )(c, grad, recv)


def chip_sum(name, parts, landed, place):
    _, na, rh, cols = parts.shape
    tr = _row_tile(rh, 256) if rh % 256 == 0 else rh
    nt = rh // tr

    def body(place_ref, p_ref, l_ref, o_ref):
        total = p_ref[...].astype(F32)
        for s in range(N_CHIPS - 1):
            total = total + l_ref[s].astype(F32)
        o_ref[...] = total

    return pl.pallas_call(
        body, name=name,
        grid_spec=pltpu.PrefetchScalarGridSpec(
            num_scalar_prefetch=1, grid=(na, nt),
            in_specs=[pl.BlockSpec((None, None, tr, cols), lambda a, r, pr: (pr[1], a, r, 0)),
                      pl.BlockSpec((N_CHIPS - 1, None, tr, cols), lambda a, r, pr: (0, a, r, 0))],
            out_specs=pl.BlockSpec((None, tr, cols), lambda a, r, pr: (a, pr[0] * nt + r, 0))),
        out_shape=jax.ShapeDtypeStruct((na, 2 * rh, cols), F32),
        compiler_params=_params(("parallel", "parallel")),
    )(place, parts, landed)


def reduce_scatter_1(tag, grads):
    n = len(grads)
    lands = [lax.empty((g.shape[0], g.shape[1], g.shape[2] // 2, g.shape[3]), g.dtype) for g in grads]
    send_sems, recv_sems, arrays, token = exchange_start(f"{tag}_to_sibling_start", list(grads) + lands, n, _to_sibling_copies(n))
    return (send_sems, recv_sems, arrays), token


def reduce_scatter_2(tag, state, place, after):
    send_sems, recv_sems, arrays = state
    n = len(arrays) // 2
    arrays = exchange_wait(f"{tag}_to_sibling_wait", send_sems, recv_sems, arrays, _to_sibling_copies(n), after)
    parts = [pair_sum(f"{tag}_pair_sum{i}", g, r, place) for i, (g, r) in enumerate(zip(arrays[:n], arrays[n:]))]
    send_sems, recv_sems, parts, lands, token = between_chips_start(f"{tag}_between_chips_start", parts)
    return (send_sems, recv_sems, parts, lands), token


def reduce_scatter_3(tag, state, place, after):
    send_sems, recv_sems, parts, lands = state
    parts, landed = between_chips_wait(f"{tag}_between_chips_wait", send_sems, recv_sems, parts, lands, after)
    halves = [chip_sum(f"{tag}_chip_sum{i}", p, l, place) for i, (p, l) in enumerate(zip(parts, landed))]
    send_sems, recv_sems, halves, token = exchange_start(f"{tag}_share_start", halves, len(halves), _share_copies(len(halves)))
    return (send_sems, recv_sems, halves), token


def reduce_scatter_4(tag, state, after):
    send_sems, recv_sems, halves = state
    return exchange_wait(f"{tag}_share_wait", send_sems, recv_sems, halves, _share_copies(len(halves)), after)


def _adamw_math(w, g, m, v):
    m = ADAM_B1 * m + (1.0 - ADAM_B1) * g
    v = ADAM_B2 * v + (1.0 - ADAM_B2) * (g * g)
    m_hat = m / (1.0 - ADAM_B1 ** ADAM_STEP)
    v_hat = v / (1.0 - ADAM_B2 ** ADAM_STEP)
    delta = -ADAM_LR * (m_hat / (jnp.sqrt(v_hat) + ADAM_EPS) + ADAM_WD * w)
    return delta, m, v


def adamw(name, w, g, m, v, after=()):
    rows, cols = w.shape
    tr = _row_tile(rows, 256) if rows % 256 == 0 else rows // 2

    def body(w_ref, g_ref, m_ref, v_ref, d_ref, mo_ref, vo_ref):
        d_ref[...], mo_ref[...], vo_ref[...] = _adamw_math(w_ref[...], g_ref[...], m_ref[...], v_ref[...])

    blk = pl.BlockSpec((tr, cols), lambda i: (i, 0))
    return pl.pallas_call(
        _ordered_after(body, 4, len(after)), name=name, grid=(rows // tr,), in_specs=[blk] * 4 + [_ANY] * len(after),
        out_specs=[blk] * 3, out_shape=[jax.ShapeDtypeStruct(w.shape, F32)] * 3, compiler_params=_params(("parallel",)),
    )(w, g, m, v, *after)


def adamw_rows(name, w, g, m, v, after=()):
    rows, _, cols = w.shape
    tr = next(r for r in (128, 110, 64, 32, 16, 8, 1) if rows % r == 0)

    def body(w_ref, g_ref, m_ref, v_ref, d_ref, mo_ref, vo_ref):
        d_ref[...], mo_ref[...], vo_ref[...] = _adamw_math(w_ref[...], g_ref[...], m_ref[...], v_ref[...])

    blk = pl.BlockSpec((tr, 1, cols), lambda i: (i, 0, 0))
    return pl.pallas_call(
        _ordered_after(body, 4, len(after)), name=name, grid=(rows // tr,), in_specs=[blk] * 4 + [_ANY] * len(after),
        out_specs=[blk] * 3, out_shape=[jax.ShapeDtypeStruct(w.shape, F32)] * 3, compiler_params=_params(("parallel",)),
    )(w, g, m, v, *after)


def adamw_stacked(name, ws, g, ms, vs, after=()):
    n = len(ws)
    rows, cols = ws[0].shape
    tr = next(r for r in (128, 88, 64, 32, 16, 8) if rows % r == 0)

    def body(*refs):
        w_refs, m_refs, v_refs, g_ref = refs[:n], refs[n:2 * n], refs[2 * n:3 * n], refs[3 * n]
        outs = refs[3 * n + 1:]
        for i in range(n):
            outs[i][...], outs[n + i][...], outs[2 * n + i][...] = _adamw_math(
                w_refs[i][...], g_ref[i], m_refs[i][...], v_refs[i][...])

    blk = pl.BlockSpec((tr, cols), lambda r: (r, 0))
    res = pl.pallas_call(
        _ordered_after(body, 3 * n + 1, len(after)), name=name, grid=(rows // tr,),
        in_specs=[blk] * (3 * n) + [pl.BlockSpec((n, tr, cols), lambda r: (0, r, 0))] + [_ANY] * len(after),
        out_specs=[blk] * (3 * n),
        out_shape=[jax.ShapeDtypeStruct((rows, cols), F32)] * (3 * n), compiler_params=_params(("parallel",)),
    )(*ws, *ms, *vs, g, *after)
    return res[:n], res[n:2 * n], res[2 * n:]


def small_allreduce_adamw(name, g_part, w, m, v, after=()):
    rows, cols = g_part.shape

    def body(g_ref, w_ref, m_ref, v_ref, sum_ref, d_ref, mo_ref, vo_ref, land, send_sems, recv_sems):
        x, y, c = _place()
        me = 4 * x + 2 * y + c
        land[me] = g_ref[...]
        copies = []
        for r in range(1, 8):
            peer = (x ^ (r >> 2), y ^ ((r >> 1) & 1), c ^ (r & 1))
            rc = pltpu.make_async_remote_copy(
                src_ref=g_ref, dst_ref=land.at[me], send_sem=send_sems.at[r - 1], recv_sem=recv_sems.at[r - 1],
                device_id=peer, device_id_type=MESH)
            rc.start()
            copies.append(rc)
        for rc in copies:
            rc.wait()
        total = land[0]
        for s in range(1, 8):
            total = total + land[s]
        sum_ref[...] = total
        d_ref[...], mo_ref[...], vo_ref[...] = _adamw_math(w_ref[...], total, m_ref[...], v_ref[...])

    vmem = pl.BlockSpec(memory_space=pltpu.VMEM)
    return pl.pallas_call(
        _ordered_after(body, 4, len(after)), name=name, in_specs=[vmem] * 4 + [_ANY] * len(after), out_specs=[vmem] * 4,
        out_shape=[jax.ShapeDtypeStruct((rows, cols), F32)] * 4,
        scratch_shapes=[pltpu.VMEM((8, rows, cols), F32), pltpu.SemaphoreType.DMA((7,)), pltpu.SemaphoreType.DMA((7,))],
        compiler_params=pltpu.CompilerParams(has_side_effects=True),
    )(g_part, w, m, v, *after)


def kernel(x, norm_ffn1, w_ffn1_gate, w_ffn1_up, w_ffn1_down, norm_mix, w_in, b_forget, w_gate, b_gate, w_up_a, w_up_b, w_out, norm_ffn2, w_ffn2_gate, w_ffn2_up, w_ffn2_down, norm_final, loss_target, m_norm_ffn1, m_w_ffn1_gate, m_w_ffn1_up, m_w_ffn1_down, m_norm_mix, m_w_in, m_b_forget, m_w_gate, m_b_gate, m_w_up_a, m_w_up_b, m_w_out, m_norm_ffn2, m_w_ffn2_gate, m_w_ffn2_up, m_w_ffn2_down, m_norm_final, v_norm_ffn1, v_w_ffn1_gate, v_w_ffn1_up, v_w_ffn1_down, v_norm_mix, v_w_in, v_b_forget, v_w_gate, v_b_gate, v_w_up_a, v_w_up_b, v_w_out, v_norm_ffn2, v_w_ffn2_gate, v_w_ffn2_up, v_w_ffn2_down, v_norm_final):
    t, d = x.shape[1], x.shape[2]
    in4 = w_in.shape[2]
    gate4 = w_gate.shape[2]
    in_cols = N_CHIPS * in4
    n_forget = in_cols - QKV_COLS
    assert w_up_a.shape[1] == WIDTH and d == 2 * WIDTH and n_forget == N_HEADS and t % BQ == 0
    chip = 2 * lax.axis_index("x") + lax.axis_index("y")
    c_arr = jnp.stack([lax.axis_index("c"), chip]).astype(jnp.int32)
    x2d = x[0]
    target = loss_target[0]

    def slot(shard):
        return lax.dynamic_update_slice(lax.empty((N_CHIPS,) + shard.shape, BF16), shard.astype(BF16)[None], (chip, 0, 0, 0))

    def ffn_views(wg, wu, wd):
        return [wg[0].T, wu[0].T, wd[0]]

    ffn1_w, ffn1_m, ffn1_v = (ffn_views(w_ffn1_gate, w_ffn1_up, w_ffn1_down), ffn_views(m_w_ffn1_gate, m_w_ffn1_up, m_w_ffn1_down),
                              ffn_views(v_w_ffn1_gate, v_w_ffn1_up, v_w_ffn1_down))
    ffn2_w, ffn2_m, ffn2_v = (ffn_views(w_ffn2_gate, w_ffn2_up, w_ffn2_down), ffn_views(m_w_ffn2_gate, m_w_ffn2_up, m_w_ffn2_down),
                              ffn_views(v_w_ffn2_gate, v_w_ffn2_up, v_w_ffn2_down))
    in_pad = -(-in4 // 32) * 32
    in_sh = slot(w_in[0].T[None])
    gt_sh = slot(w_gate[0].T[None])
    up_sh = slot(jnp.concatenate([w_up_a[0], w_up_b[0]], axis=0)[None])
    wo_sh = slot(w_out)
    f1_send, f1_recv, f1_bufs, f1_token = gather_ici_start("gather_ffn1_start", [slot(jnp.stack(ffn1_w))], norm_ffn1)
    mx_send, mx_recv, mx_bufs, mx_token = gather_ici_start("gather_mixer_start", [in_sh, gt_sh, up_sh, wo_sh], f1_token)
    f2_send, f2_recv, f2_bufs, f2_token = gather_ici_start("gather_ffn2_start", [slot(jnp.stack(ffn2_w))], mx_token)

    normed1 = rms_fwd("ffn1_rms", x2d, norm_ffn1, after=(f2_token,))
    (w3_1,) = gather_forward("gather_ffn1_forward", gather_ici_wait("gather_ffn1_wait", f1_send, f1_recv, f1_bufs, normed1[0]))
    x1, saved1 = ffn_forward("ffn1", x2d, norm_ffn1, w3_1, normed=normed1)
    w_in_g, w_gate_g, wup, wo = gather_forward(
        "gather_mixer_forward", gather_ici_wait("gather_mixer_wait", mx_send, mx_recv, mx_bufs, x1))
    wup = wup[:, 0]
    w_in_t = w_in_g.reshape(in_cols, d)
    w_gate_t = w_gate_g.reshape(2 * d, d)
    w_f_t = jnp.pad(w_in_t[QKV_COLS:], ((0, QB - n_forget), (0, 0)))
    wo_full = wo.reshape(d, d)
    b_forget_row = jnp.pad(b_forget, ((0, 0), (0, QB - n_forget)))

    h2, rstd2 = rms_fwd("mix_rms", x1, norm_mix)
    qkv = proj("mix_proj_qkv", h2, w_in_t, jnp.zeros((1, QKV_COLS), F32), 0, QKV_COLS, WIDTH, BF16, scaled_tiles=(0, 3))
    pc = proj("mix_proj_gates", h2, w_gate_t, b_gate, 0, 2 * d, WIDTH, F32)
    f_logit = proj("mix_proj_forget", h2, w_f_t, jnp.zeros((1, QB), F32), 0, QB, QB, F32)
    c_cum = fox_prep("fox_prep", f_logit, b_forget_row)
    c_heads = c_cum[:, :N_HEADS].T
    c_col = c_heads[:, :, None]
    c_row = c_heads.reshape(N_HEADS, t // CS, 1, CS)
    ya = sb_pair_fwd("sb_fwd", qkv)
    yb, lse = fox_pair_fwd("fox_fwd", qkv, c_col, c_row)
    ua, ub, mixed = mix_fwd("mix_fwd", ya, yb, wup, pc)
    x2 = mm_residual("mix_out", mixed[None], wo_full[None], pl.BlockSpec((1, d, d), lambda m: (0, 0, 0)), x1, 1.0)
    (w3_2,) = gather_forward("gather_ffn2_forward", gather_ici_wait("gather_ffn2_wait", f2_send, f2_recv, f2_bufs, x2))
    x3, saved2 = ffn_forward("ffn2", x2, norm_ffn2, w3_2)
    dx3, gn_final, loss_part = loss_head("loss_head", x3, norm_final[None], target)

    g_w3_2, dab2 = ffn_backward_weights("ffn2", dx3, saved2, w3_2)
    rs_ffn2, token = reduce_scatter_1("rs_ffn2", [g_w3_2])
    dx2, gn_ffn2 = ffn_backward_input("ffn2", dx3, saved2, dab2, norm_ffn2, w3_2, after=(token,))
    rs_ffn2, rs_ffn2_token = reduce_scatter_2("rs_ffn2", rs_ffn2, c_arr, dx2)

    dua, dub, dpa, dpb, gba, gbb = mix_bwd("mix_bwd", dx2, wo_full, pc, ua, ub, after=(rs_ffn2_token,))
    g_bgate = jnp.concatenate([gba, gbb], axis=1)
    g_wo = mm_plain("wgrad_out", mixed, dx2, TN, BF16, tk_target=1024)
    dya = up_bwd("dya", dua, wup, 0)
    dyb = up_bwd("dyb", dub, wup, 1)
    g_up = up_wgrad("wgrad_up", ya, yb, dua, dub)
    dqkv = sb_pair_bwd("sb_bwd", qkv, dya, lax.empty((t, QKV_COLS), BF16))
    dqkv, dcc, dcr = fox_pair_bwd("fox_bwd", qkv, yb, dyb, lse, c_col, c_row, dqkv)
    dc = dcc[:, :, :2].transpose(1, 0, 2).reshape(t, N_HEADS) + dcr.reshape(N_HEADS, t).T
    df, g_bf = fox_gate_bwd("fox_gate_bwd", jnp.pad(dc, ((0, 0), (0, QB - N_HEADS))), f_logit, b_forget_row)
    df = df.astype(BF16)
    g_qkv_t = wgrad_cat("wgrad_qkv", h2, dqkv)
    g_f_t = wgrad_cat("wgrad_forget", h2, df)
    g_gate_t = jnp.stack([wgrad_cat("wgrad_gate_a", h2, dpa), wgrad_cat("wgrad_gate_b", h2, dpb)])
    g_in_t = jnp.concatenate([g_qkv_t, g_f_t[:n_forget]], axis=0).reshape(N_CHIPS, in4, d)
    rs_mixer, token = reduce_scatter_1(
        "rs_mixer", [jnp.pad(g_in_t, ((0, 0), (0, in_pad - in4), (0, 0)))[:, None], g_gate_t.reshape(N_CHIPS, 1, gate4, d),
                     g_up[:, None], g_wo.reshape(N_CHIPS, 1, d // N_CHIPS, d)])
    dx1, gn_mix = mixer_dh("mix_dh", [(dqkv, w_in_t, 0), (dpa, w_gate_t, 0), (dpb, w_gate_t, 1), (df, w_f_t, 0)],
                           x1, rstd2, norm_mix, dx2, after=(token,))
    rs_mixer, rs_mixer_token = reduce_scatter_2("rs_mixer", rs_mixer, c_arr, dx1)
    rs_ffn2, rs_ffn2_token = reduce_scatter_3("rs_ffn2", rs_ffn2, c_arr, dx1)

    big_out = {}

    def adamw_ffn(tag, s_w3, ws, ms, vs, after):
        deltas, new_ms, new_vs = adamw_stacked(f"adamw_{tag}", ws, s_w3, ms, vs, after)
        for which, part in ((GATE, "gate"), (UP, "up"), (DOWN, "down")):
            back = (lambda a: a[None]) if which == DOWN else (lambda a: a.T[None])
            big_out[f"w_{tag}_{part}"] = tuple(back(a) for a in (s_w3[which], deltas[which], new_ms[which], new_vs[which]))
        return deltas[DOWN]

    g_w3_1, dab1 = ffn_backward_weights("ffn1", dx1, saved1, w3_1, after=(rs_mixer_token, rs_ffn2_token))
    rs_ffn1, token = reduce_scatter_1("rs_ffn1", [g_w3_1])
    rs_mixer, rs_mixer_token = reduce_scatter_3("rs_mixer", rs_mixer, c_arr, g_w3_1)
    (s_w3_2,) = reduce_scatter_4("rs_ffn2", rs_ffn2, g_w3_1)
    last = adamw_ffn("ffn2", s_w3_2, ffn2_w, ffn2_m, ffn2_v, (token, rs_mixer_token))
    rs_ffn1, rs_ffn1_token = reduce_scatter_2("rs_ffn1", rs_ffn1, c_arr, last)
    dx0, gn_ffn1 = ffn_backward_input("ffn1", dx1, saved1, dab1, norm_ffn1, w3_1, after=(rs_ffn1_token,))

    def pack_small(n1, nm, n2, nf, bg, bf, last):
        return jnp.concatenate([n1, nm, n2, nf, bg.reshape(2, d), jnp.pad(bf, ((0, 0), (0, d - n_forget))), last], axis=0)

    zero_row = jnp.zeros((1, d), F32)
    g_small = pack_small(gn_ffn1, gn_mix, gn_ffn2, gn_final, g_bgate, g_bf[:, :n_forget], jnp.pad(loss_part, ((0, 0), (0, d - 1))))
    w_small = pack_small(norm_ffn1, norm_mix, norm_ffn2, norm_final[None], b_gate, b_forget, zero_row)
    m_small = pack_small(m_norm_ffn1, m_norm_mix, m_norm_ffn2, m_norm_final[None], m_b_gate, m_b_forget, zero_row)
    v_small = pack_small(v_norm_ffn1, v_norm_mix, v_norm_ffn2, v_norm_final[None], v_b_gate, v_b_forget, zero_row)

    s_in, s_gt, s_up, s_wo = reduce_scatter_4("rs_mixer", rs_mixer, dx0)
    grads = {"w_gate": s_gt[0].T, "w_up_a": s_up[0, :WIDTH], "w_up_b": s_up[0, WIDTH:], "w_out": s_wo[0]}
    weights = {"w_gate": (w_gate, m_w_gate, v_w_gate), "w_up_a": (w_up_a, m_w_up_a, v_w_up_a),
               "w_up_b": (w_up_b, m_w_up_b, v_w_up_b), "w_out": (w_out, m_w_out, v_w_out)}
    for wname, (w, m, v) in weights.items():
        g = grads[wname]
        delta, new_m, new_v = adamw(f"adamw_{wname}", w[0], g, m[0], v[0])
        big_out[wname] = (g[None], delta[None], new_m[None], new_v[None])
    rows_of = lambda a: jnp.transpose(a, (2, 0, 1))
    g_in_rows = s_in[0, :in4][:, None, :]
    in_rows = adamw_rows("adamw_w_in", rows_of(w_in), g_in_rows, rows_of(m_w_in), rows_of(v_w_in))
    big_out["w_in"] = tuple(jnp.transpose(a, (1, 2, 0)) for a in (g_in_rows, *in_rows))

    rs_ffn1, token = reduce_scatter_3("rs_ffn1", rs_ffn1, c_arr, in_rows[0])
    smalls = small_allreduce_adamw("small_allreduce_adamw", g_small, w_small, m_small, v_small, after=(token,))
    (s_w3_1,) = reduce_scatter_4("rs_ffn1", rs_ffn1, smalls[0])
    adamw_ffn("ffn1", s_w3_1, ffn1_w, ffn1_m, ffn1_v, ())

    def unpack_small(p):
        return {"norm_ffn1": p[0:1], "norm_mix": p[1:2], "norm_ffn2": p[2:3], "norm_final": p[3], "b_gate": p[4:6].reshape(1, 2 * d),
                "b_forget": p[6:7, :n_forget]}

    loss = smalls[0][7, 0]
    small_out = [unpack_small(p) for p in smalls]

    order = ["norm_ffn1", "w_ffn1_gate", "w_ffn1_up", "w_ffn1_down", "norm_mix", "w_in", "b_forget", "w_gate", "b_gate",
             "w_up_a", "w_up_b", "w_out", "norm_ffn2", "w_ffn2_gate", "w_ffn2_up", "w_ffn2_down", "norm_final"]
    outs = [loss, dx0[None]]
    for kind in range(4):
        for wname in order:
            outs.append(big_out[wname][kind] if wname in big_out else small_out[kind][wname])
    return tuple(outs)
```
